```python
import jax, jax.numpy as jnp
from jax import lax
import numpy as np

D_MODEL = 1024
BATCH = 8
SEQ = 8192
DEPTH = 2

N_META = 16
POOL_WINDOWS = (2, 4, 8, 16)
N_POOL_GROUPS = len(POOL_WINDOWS)
POOL_GROUP_DIM = D_MODEL // N_POOL_GROUPS
HEAD_DIM = 64
N_HEADS = D_MODEL // HEAD_DIM
D_FF = ((8 * D_MODEL // 3 + 127) // 128) * 128
CONV_WIDTH = 3
Q_BLOCK = 128
N_A_LAYERS = DEPTH // 2
N_B_LAYERS = DEPTH - N_A_LAYERS
ALPHA = (2.0 * DEPTH) ** 0.25
BETA = (8.0 * DEPTH) ** -0.25
LN_EPS = 1e-5
NEG_INF = -1e30

kernel_name = "yoco_pool_fox_convffn_deepnorm_meta"


def layer_norm(x, g, b):
    xf = x.astype(jnp.float32)
    mu = jnp.mean(xf, axis=-1, keepdims=True)
    xc = xf - mu
    var = jnp.mean(xc * xc, axis=-1, keepdims=True)
    y = xc * lax.rsqrt(var + LN_EPS) * g.astype(jnp.float32) + b.astype(jnp.float32)
    return y.astype(x.dtype)


def multiscale_pool_mixer(h, w_group, scale):
    b_, L, D = h.shape
    G = POOL_GROUP_DIM
    hf = h.astype(jnp.float32)
    cs0 = jnp.pad(jnp.cumsum(hf, axis=1), ((0, 0), (1, 0), (0, 0)))
    t = jnp.arange(1, L + 1, dtype=jnp.float32)
    outs = []
    for g, w in enumerate(POOL_WINDOWS):
        sl = slice(g * G, (g + 1) * G)
        upper = cs0[:, 1:, sl]
        lower = jnp.pad(cs0[:, :L + 1 - w, sl], ((0, 0), (w - 1, 0), (0, 0)))
        count = jnp.minimum(t, float(w))[None, :, None]
        outs.append((upper - lower) / count)
    pooled = jnp.concatenate(outs, axis=-1)
    diff = (pooled - hf).astype(h.dtype).reshape(b_, L, N_POOL_GROUPS, G)
    mixed = jnp.einsum('blgc,gcd->blgd', diff, w_group).reshape(b_, L, D)
    return mixed * scale


def conv_glu_ffn(h, w_in, conv_w, conv_b, w_out):
    L = h.shape[1]
    u = h @ w_in
    up = jnp.pad(u, ((0, 0), (CONV_WIDTH - 1, 0), (0, 0)))
    c = conv_b + sum(conv_w[k] * up[:, k:k + L] for k in range(CONV_WIDTH))
    a, g = jnp.split(c, 2, axis=-1)
    return (jax.nn.silu(a) * g) @ w_out


def padded_layout(L):
    front = (-N_META) % Q_BLOCK
    total = ((front + L + Q_BLOCK - 1) // Q_BLOCK) * Q_BLOCK
    return front, total


def shared_kv(h, w_kv, w_f, b_f):
    b_, L, D = h.shape
    front, Lp = padded_layout(L)
    pad = ((0, 0), (front, Lp - front - L), (0, 0), (0, 0))
    kv = h @ w_kv
    k = jnp.pad(kv[..., :D].reshape(b_, L, N_HEADS, HEAD_DIM), pad).transpose(0, 2, 1, 3)
    v = jnp.pad(kv[..., D:].reshape(b_, L, N_HEADS, HEAD_DIM), pad).transpose(0, 2, 1, 3)
    logf = jax.nn.log_sigmoid((h @ w_f).astype(jnp.float32) + b_f.astype(jnp.float32))
    logf = jnp.pad(logf, ((0, 0), (front, Lp - front - L), (0, 0)))
    c = jnp.cumsum(logf, axis=1).transpose(0, 2, 1)
    return k, v, c


def forgetting_attention(h, w_q, w_o, k, v, c):
    b_, L, D = h.shape
    front, Lp = padded_layout(L)
    nb = Lp // Q_BLOCK
    q = (h @ w_q).reshape(b_, L, N_HEADS, HEAD_DIM)
    q = jnp.pad(q, ((0, 0), (front, Lp - front - L), (0, 0), (0, 0)))
    qb = q.reshape(b_, nb, Q_BLOCK, N_HEADS, HEAD_DIM).transpose(1, 0, 3, 2, 4)
    cq = c.reshape(b_, N_HEADS, nb, Q_BLOCK).transpose(2, 0, 1, 3)
    kpos = jnp.arange(Lp)
    scale = HEAD_DIM ** -0.5

    def block(args):
        i, q_i, cq_i = args
        qpos = i * Q_BLOCK + jnp.arange(Q_BLOCK)
        s = jnp.einsum('bhqd,bhkd->bhqk', q_i, k, preferred_element_type=jnp.float32) * scale
        s = s + cq_i[..., None] - c[:, :, None, :]
        mask = (kpos[None, :] <= qpos[:, None]) & (kpos[None, :] >= front)
        p = jax.nn.softmax(jnp.where(mask, s, NEG_INF), axis=-1)
        return jnp.einsum('bhqk,bhkd->bhqd', p.astype(v.dtype), v)

    o = lax.map(block, (jnp.arange(nb), qb, cq))
    o = o.transpose(1, 0, 3, 2, 4).reshape(b_, Lp, D)[:, front:front + L]
    return o @ w_o


def _fwd_setup_inputs(seed: int = 0) -> dict:
    key = jax.random.key(seed)
    ks = jax.random.split(key, 16)
    D, F, G, H = D_MODEL, D_FF, POOL_GROUP_DIM, N_HEADS
    nrm = jax.random.normal
    return {
        "x": nrm(ks[0], (BATCH, SEQ, D), jnp.float32),
        "meta": nrm(ks[1], (N_META, D), jnp.float32),
        "pool_w": nrm(ks[2], (N_A_LAYERS, N_POOL_GROUPS, G, G), jnp.float32) * (G ** -0.5) * BETA,
        "pool_scale": 1.0 + 0.02 * nrm(ks[3], (N_A_LAYERS, D), jnp.float32),
        "w_kv": nrm(ks[4], (D, 2 * D), jnp.float32) * (D ** -0.5),
        "w_f": nrm(ks[5], (D, H), jnp.float32) * (D ** -0.5),
        "b_f": jax.random.uniform(ks[6], (H,), jnp.float32, 1.0, 6.0),
        "w_q": nrm(ks[7], (N_B_LAYERS, D, D), jnp.float32) * (D ** -0.5),
        "w_o": nrm(ks[8], (N_B_LAYERS, D, D), jnp.float32) * (D ** -0.5) * BETA,
        "ffn_w_in": nrm(ks[9], (DEPTH, D, 2 * F), jnp.float32) * (D ** -0.5),
        "ffn_conv_w": nrm(ks[10], (DEPTH, CONV_WIDTH, 2 * F), jnp.float32) * (CONV_WIDTH ** -0.5),
        "ffn_conv_b": 0.02 * nrm(ks[11], (DEPTH, 2 * F), jnp.float32),
        "ffn_w_out": nrm(ks[12], (DEPTH, F, D), jnp.float32) * (F ** -0.5) * BETA,
        "ln_g": 1.0 + 0.02 * nrm(ks[13], (DEPTH, 2, D), jnp.float32),
        "ln_b": 0.02 * nrm(ks[14], (DEPTH, 2, D), jnp.float32),
    }


def _fwd_reference(x, meta, pool_w, pool_scale, w_kv, w_f, b_f, w_q, w_o, ffn_w_in, ffn_conv_w,
              ffn_conv_b, ffn_w_out, ln_g, ln_b):
    b_ = x.shape[0]
    h = jnp.concatenate(
        [jnp.broadcast_to(meta[None].astype(x.dtype), (b_, N_META, D_MODEL)), x], axis=1)
    shared = None
    for i in range(DEPTH):
        if i < N_A_LAYERS:
            mix = multiscale_pool_mixer(h, pool_w[i], pool_scale[i])
        else:
            if i == N_A_LAYERS:
                shared = shared_kv(h, w_kv, w_f, b_f)
            j = i - N_A_LAYERS
            mix = forgetting_attention(h, w_q[j], w_o[j], shared[0], shared[1], shared[2])
        h = layer_norm(ALPHA * h + mix, ln_g[i, 0], ln_b[i, 0])
        ffn = conv_glu_ffn(h, ffn_w_in[i], ffn_conv_w[i], ffn_conv_b[i], ffn_w_out[i])
        h = layer_norm(ALPHA * h + ffn, ln_g[i, 1], ln_b[i, 1])
    return h[:, N_META:]


import jax as _jax
import jax.numpy as _jnp

TWIN_FORMAT = 'train_step'
FWD_PARAMS = ['x', 'meta', 'pool_w', 'pool_scale', 'w_kv', 'w_f', 'b_f', 'w_q', 'w_o', 'ffn_w_in', 'ffn_conv_w', 'ffn_conv_b', 'ffn_w_out', 'ln_g', 'ln_b']
TWIN_WEIGHTS = ['meta', 'pool_w', 'pool_scale', 'w_kv', 'w_f', 'b_f', 'w_q', 'w_o', 'ffn_w_in', 'ffn_conv_w', 'ffn_conv_b', 'ffn_w_out', 'ln_g', 'ln_b']
TWIN_DIFF_INPUT = 'x'
TWIN_INPUTS = ['x', 'meta', 'pool_w', 'pool_scale', 'w_kv', 'w_f', 'b_f', 'w_q', 'w_o', 'ffn_w_in', 'ffn_conv_w', 'ffn_conv_b', 'ffn_w_out', 'ln_g', 'ln_b', 'loss_target', 'm_meta', 'm_pool_w', 'm_pool_scale', 'm_w_kv', 'm_w_f', 'm_b_f', 'm_w_q', 'm_w_o', 'm_ffn_w_in', 'm_ffn_conv_w', 'm_ffn_conv_b', 'm_ffn_w_out', 'm_ln_g', 'm_ln_b', 'v_meta', 'v_pool_w', 'v_pool_scale', 'v_w_kv', 'v_w_f', 'v_b_f', 'v_w_q', 'v_w_o', 'v_ffn_w_in', 'v_ffn_conv_w', 'v_ffn_conv_b', 'v_ffn_w_out', 'v_ln_g', 'v_ln_b']
TWIN_OUTPUTS = ['loss', 'grad_x', 'grad_meta', 'grad_pool_w', 'grad_pool_scale', 'grad_w_kv', 'grad_w_f', 'grad_b_f', 'grad_w_q', 'grad_w_o', 'grad_ffn_w_in', 'grad_ffn_conv_w', 'grad_ffn_conv_b', 'grad_ffn_w_out', 'grad_ln_g', 'grad_ln_b', 'delta_meta', 'delta_pool_w', 'delta_pool_scale', 'delta_w_kv', 'delta_w_f', 'delta_b_f', 'delta_w_q', 'delta_w_o', 'delta_ffn_w_in', 'delta_ffn_conv_w', 'delta_ffn_conv_b', 'delta_ffn_w_out', 'delta_ln_g', 'delta_ln_b', 'new_m_meta', 'new_m_pool_w', 'new_m_pool_scale', 'new_m_w_kv', 'new_m_w_f', 'new_m_b_f', 'new_m_w_q', 'new_m_w_o', 'new_m_ffn_w_in', 'new_m_ffn_conv_w', 'new_m_ffn_conv_b', 'new_m_ffn_w_out', 'new_m_ln_g', 'new_m_ln_b', 'new_v_meta', 'new_v_pool_w', 'new_v_pool_scale', 'new_v_w_kv', 'new_v_w_f', 'new_v_b_f', 'new_v_w_q', 'new_v_w_o', 'new_v_ffn_w_in', 'new_v_ffn_conv_w', 'new_v_ffn_conv_b', 'new_v_ffn_w_out', 'new_v_ln_g', 'new_v_ln_b']
TWIN_LEAF_KINDS = {'loss': 'loss', 'grad_x': 'grad_x', 'grad_meta': 'grad_w', 'grad_pool_w': 'grad_w', 'grad_pool_scale': 'grad_w', 'grad_w_kv': 'grad_w', 'grad_w_f': 'grad_w', 'grad_b_f': 'grad_w', 'grad_w_q': 'grad_w', 'grad_w_o': 'grad_w', 'grad_ffn_w_in': 'grad_w', 'grad_ffn_conv_w': 'grad_w', 'grad_ffn_conv_b': 'grad_w', 'grad_ffn_w_out': 'grad_w', 'grad_ln_g': 'grad_w', 'grad_ln_b': 'grad_w', 'delta_meta': 'delta_w', 'delta_pool_w': 'delta_w', 'delta_pool_scale': 'delta_w', 'delta_w_kv': 'delta_w', 'delta_w_f': 'delta_w', 'delta_b_f': 'delta_w', 'delta_w_q': 'delta_w', 'delta_w_o': 'delta_w', 'delta_ffn_w_in': 'delta_w', 'delta_ffn_conv_w': 'delta_w', 'delta_ffn_conv_b': 'delta_w', 'delta_ffn_w_out': 'delta_w', 'delta_ln_g': 'delta_w', 'delta_ln_b': 'delta_w', 'new_m_meta': 'new_m', 'new_m_pool_w': 'new_m', 'new_m_pool_scale': 'new_m', 'new_m_w_kv': 'new_m', 'new_m_w_f': 'new_m', 'new_m_b_f': 'new_m', 'new_m_w_q': 'new_m', 'new_m_w_o': 'new_m', 'new_m_ffn_w_in': 'new_m', 'new_m_ffn_conv_w': 'new_m', 'new_m_ffn_conv_b': 'new_m', 'new_m_ffn_w_out': 'new_m', 'new_m_ln_g': 'new_m', 'new_m_ln_b': 'new_m', 'new_v_meta': 'new_v', 'new_v_pool_w': 'new_v', 'new_v_pool_scale': 'new_v', 'new_v_w_kv': 'new_v', 'new_v_w_f': 'new_v', 'new_v_b_f': 'new_v', 'new_v_w_q': 'new_v', 'new_v_w_o': 'new_v', 'new_v_ffn_w_in': 'new_v', 'new_v_ffn_conv_w': 'new_v', 'new_v_ffn_conv_b': 'new_v', 'new_v_ffn_w_out': 'new_v', 'new_v_ln_g': 'new_v', 'new_v_ln_b': 'new_v'}


def _forward(args):
    return _fwd_reference(*[args[k] for k in FWD_PARAMS])


def _output_shape():
    def fwd():
        inp = _fwd_setup_inputs(0)
        return _fwd_reference(*[inp[k] for k in FWD_PARAMS])
    out = _jax.eval_shape(fwd)
    return out.shape, out.dtype

N_MICROBATCH = 1
ADAM_LR = 0.001
ADAM_B1 = 0.9
ADAM_B2 = 0.999
ADAM_EPS = 1e-08
ADAM_WD = 0.01
ADAM_STEP = 10
PER_EXAMPLE_BATCH_AXIS = {'x': 0, 'loss_target': 0}
SHARED_INPUTS = []
_WEIGHT_DTYPES = {'meta': _jnp.float32, 'pool_w': _jnp.float32, 'pool_scale': _jnp.float32, 'w_kv': _jnp.float32, 'w_f': _jnp.float32, 'b_f': _jnp.float32, 'w_q': _jnp.float32, 'w_o': _jnp.float32, 'ffn_w_in': _jnp.float32, 'ffn_conv_w': _jnp.float32, 'ffn_conv_b': _jnp.float32, 'ffn_w_out': _jnp.float32, 'ln_g': _jnp.float32, 'ln_b': _jnp.float32}
MOMENT_SCALE = {'meta': 2.918412e-03, 'pool_w': 1.611187e-01, 'pool_scale': 1.417895e-01, 'w_kv': 2.278054e-02, 'w_f': 4.151921e-02, 'b_f': 1.038269e-01, 'w_q': 2.052814e-02, 'w_o': 4.955012e-02, 'ffn_w_in': 3.311107e-02, 'ffn_conv_w': 3.317172e-02, 'ffn_conv_b': 4.217842e-02, 'ffn_w_out': 1.079982e-01, 'ln_g': 3.210657e+01, 'ln_b': 1.410562e+00}


def _to_microbatches(a, axis):
    t = _jnp.moveaxis(a, axis, 0)
    t = t.reshape((N_MICROBATCH, t.shape[0] // N_MICROBATCH) + t.shape[1:])
    return _jnp.moveaxis(t, 1, axis + 1)


def setup_inputs(seed: int = 0) -> dict:
    inp = _fwd_setup_inputs(seed)
    key = _jax.random.fold_in(_jax.random.key(seed), 7919)
    shape, _ = _output_shape()
    out = dict(inp)
    out["loss_target"] = _jax.random.normal(_jax.random.fold_in(key, 0), shape, _jnp.float32)
    for i, name in enumerate(TWIN_WEIGHTS):
        w = inp[name].astype(_jnp.float32)
        if MOMENT_SCALE is None:
            s = _jnp.sqrt(_jnp.mean(_jnp.square(w)) + 1e-30)
        else:
            s = MOMENT_SCALE[name]
        km, kv = _jax.random.split(_jax.random.fold_in(key, i + 1))
        out[name] = w
        out["m_" + name] = s * _jax.random.normal(km, w.shape, _jnp.float32)
        out["v_" + name] = (s * s) * _jax.random.uniform(kv, w.shape, _jnp.float32, 0.5, 1.5)
    if N_MICROBATCH > 1:
        for name, axis in PER_EXAMPLE_BATCH_AXIS.items():
            out[name] = _to_microbatches(out[name], axis)
    return {'x': out['x'], 'meta': out['meta'], 'pool_w': out['pool_w'], 'pool_scale': out['pool_scale'], 'w_kv': out['w_kv'], 'w_f': out['w_f'], 'b_f': out['b_f'], 'w_q': out['w_q'], 'w_o': out['w_o'], 'ffn_w_in': out['ffn_w_in'], 'ffn_conv_w': out['ffn_conv_w'], 'ffn_conv_b': out['ffn_conv_b'], 'ffn_w_out': out['ffn_w_out'], 'ln_g': out['ln_g'], 'ln_b': out['ln_b'], 'loss_target': out['loss_target'], 'm_meta': out['m_meta'], 'm_pool_w': out['m_pool_w'], 'm_pool_scale': out['m_pool_scale'], 'm_w_kv': out['m_w_kv'], 'm_w_f': out['m_w_f'], 'm_b_f': out['m_b_f'], 'm_w_q': out['m_w_q'], 'm_w_o': out['m_w_o'], 'm_ffn_w_in': out['m_ffn_w_in'], 'm_ffn_conv_w': out['m_ffn_conv_w'], 'm_ffn_conv_b': out['m_ffn_conv_b'], 'm_ffn_w_out': out['m_ffn_w_out'], 'm_ln_g': out['m_ln_g'], 'm_ln_b': out['m_ln_b'], 'v_meta': out['v_meta'], 'v_pool_w': out['v_pool_w'], 'v_pool_scale': out['v_pool_scale'], 'v_w_kv': out['v_w_kv'], 'v_w_f': out['v_w_f'], 'v_b_f': out['v_b_f'], 'v_w_q': out['v_w_q'], 'v_w_o': out['v_w_o'], 'v_ffn_w_in': out['v_ffn_w_in'], 'v_ffn_conv_w': out['v_ffn_conv_w'], 'v_ffn_conv_b': out['v_ffn_conv_b'], 'v_ffn_w_out': out['v_ffn_w_out'], 'v_ln_g': out['v_ln_g'], 'v_ln_b': out['v_ln_b']}


def _loss(weights, diff, rest, loss_target):
    with _jax.named_scope("forward"):
        args = {**rest, TWIN_DIFF_INPUT: diff, **{k: w.astype(_WEIGHT_DTYPES[k]) for k, w in weights.items()}}
        y = _forward(args)
    with _jax.named_scope("loss_head"):
        err = _jnp.square(y.astype(_jnp.float32) - loss_target)
        return 0.5 * _jnp.sum(_jnp.mean(err, axis=-1)) if err.ndim else 0.5 * err


def _adamw(w, g, m, v):
    m = ADAM_B1 * m + (1.0 - ADAM_B1) * g
    v = ADAM_B2 * v + (1.0 - ADAM_B2) * _jnp.square(g)
    m_hat = m / (1.0 - ADAM_B1 ** ADAM_STEP)
    v_hat = v / (1.0 - ADAM_B2 ** ADAM_STEP)
    delta = -ADAM_LR * (m_hat / (_jnp.sqrt(v_hat) + ADAM_EPS) + ADAM_WD * w)
    return delta, m, v


def reference(x, meta, pool_w, pool_scale, w_kv, w_f, b_f, w_q, w_o, ffn_w_in, ffn_conv_w, ffn_conv_b, ffn_w_out, ln_g, ln_b, loss_target, m_meta, m_pool_w, m_pool_scale, m_w_kv, m_w_f, m_b_f, m_w_q, m_w_o, m_ffn_w_in, m_ffn_conv_w, m_ffn_conv_b, m_ffn_w_out, m_ln_g, m_ln_b, v_meta, v_pool_w, v_pool_scale, v_w_kv, v_w_f, v_b_f, v_w_q, v_w_o, v_ffn_w_in, v_ffn_conv_w, v_ffn_conv_b, v_ffn_w_out, v_ln_g, v_ln_b):
    given = dict(x=x, meta=meta, pool_w=pool_w, pool_scale=pool_scale, w_kv=w_kv, w_f=w_f, b_f=b_f, w_q=w_q, w_o=w_o, ffn_w_in=ffn_w_in, ffn_conv_w=ffn_conv_w, ffn_conv_b=ffn_conv_b, ffn_w_out=ffn_w_out, ln_g=ln_g, ln_b=ln_b, loss_target=loss_target, m_meta=m_meta, m_pool_w=m_pool_w, m_pool_scale=m_pool_scale, m_w_kv=m_w_kv, m_w_f=m_w_f, m_b_f=m_b_f, m_w_q=m_w_q, m_w_o=m_w_o, m_ffn_w_in=m_ffn_w_in, m_ffn_conv_w=m_ffn_conv_w, m_ffn_conv_b=m_ffn_conv_b, m_ffn_w_out=m_ffn_w_out, m_ln_g=m_ln_g, m_ln_b=m_ln_b, v_meta=v_meta, v_pool_w=v_pool_w, v_pool_scale=v_pool_scale, v_w_kv=v_w_kv, v_w_f=v_w_f, v_b_f=v_b_f, v_w_q=v_w_q, v_w_o=v_w_o, v_ffn_w_in=v_ffn_w_in, v_ffn_conv_w=v_ffn_conv_w, v_ffn_conv_b=v_ffn_conv_b, v_ffn_w_out=v_ffn_w_out, v_ln_g=v_ln_g, v_ln_b=v_ln_b)
    weights = {n: given[n] for n in TWIN_WEIGHTS}
    shared = {n: given[n] for n in SHARED_INPUTS}
    per_example = {n: given[n] for n in ['x']}
    grad_fn = _jax.value_and_grad(_loss, argnums=(0, 1))

    def one_microbatch(ex, loss_target):
        ex = dict(ex)
        diff = ex.pop(TWIN_DIFF_INPUT)
        return grad_fn(weights, diff, {**shared, **ex}, loss_target)

    if N_MICROBATCH == 1:
        loss, (grad_w, grad_x) = one_microbatch(per_example, given["loss_target"])
    else:
        def body(carry, xs):
            loss_sum, grad_sum = carry
            l_k, (gw_k, gx_k) = one_microbatch(xs[0], xs[1])
            with _jax.named_scope("update"):
                return (loss_sum + l_k, _jax.tree.map(_jnp.add, grad_sum, gw_k)), gx_k

        init = (_jnp.zeros((), _jnp.float32), _jax.tree.map(_jnp.zeros_like, weights))
        (loss, grad_w), grad_x = _jax.lax.scan(body, init, (per_example, given["loss_target"]))
    with _jax.named_scope("update"):
        delta_w, new_m, new_v = {}, {}, {}
        for n in TWIN_WEIGHTS:
            delta_w[n], new_m[n], new_v[n] = _adamw(weights[n], grad_w[n], given["m_" + n], given["v_" + n])
    return (loss, grad_x, *[grad_w[n] for n in TWIN_WEIGHTS], *[delta_w[n] for n in TWIN_WEIGHTS],
            *[new_m[n] for n in TWIN_WEIGHTS], *[new_v[n] for n in TWIN_WEIGHTS])
```

```python
import functools

import jax
import jax.numpy as jnp
from jax import lax
from jax.experimental import pallas as pl
from jax.experimental.pallas import tpu as pltpu

N_META = 16
POOL_WINDOWS = (2, 4, 8, 16)
MAX_WINDOW = max(POOL_WINDOWS)
N_GROUPS = len(POOL_WINDOWS)
HEAD_DIM = 64
DEPTH = 2
CONV_WIDTH = 3
ALPHA = (2.0 * DEPTH) ** 0.25
LN_EPS = 1e-5
NEG_INF = -1e30
ADAM_LR = 0.001
ADAM_B1 = 0.9
ADAM_B2 = 0.999
ADAM_EPS = 1e-08
ADAM_WD = 0.01
ADAM_STEP = 10

F32 = jnp.float32
BF16 = jnp.bfloat16
LANES = 128
SUBLANES = 8
PACK_COLS = 1024
VMEM_LIMIT = 56 * 1024 * 1024
AXES = ("x", "y", "c")
MESH = pl.DeviceIdType.MESH

NN = (((1,), (0,)), ((), ()))
NT = (((1,), (1,)), ((), ()))
TN = (((0,), (0,)), ((), ()))

SHARD_AXIS = {"meta": 1, "pool_w": 2, "pool_scale": 1, "w_kv": 1, "w_f": 0, "w_q": 1, "w_o": 1,
              "ffn_w_in": 2, "ffn_conv_w": 2, "ffn_w_out": 1, "ln_g": 2, "ln_b": 2}
SHARDED = ("meta", "pool_w", "pool_scale", "w_kv", "w_f", "w_q", "w_o", "ffn_w_in", "ffn_conv_w",
           "ffn_w_out", "ln_g", "ln_b")
REPLICATED = ("b_f", "ffn_conv_b")
MATMUL_WEIGHTS = ("pool_w", "w_kv", "w_f", "w_q", "w_o", "ffn_w_in", "ffn_w_out")
VECTOR_WEIGHTS = ("meta", "pool_scale", "ffn_conv_w", "ln_g", "ln_b")
WEIGHT_ORDER = ("meta", "pool_w", "pool_scale", "w_kv", "w_f", "b_f", "w_q", "w_o", "ffn_w_in",
                "ffn_conv_w", "ffn_conv_b", "ffn_w_out", "ln_g", "ln_b")
N_CHIPS = 4
N_DEV = 8


def _cparams(*sem):
    return pltpu.CompilerParams(dimension_semantics=sem, vmem_limit_bytes=VMEM_LIMIT)


def _round_up(n, m):
    return (n + m - 1) // m * m


def _pick(n, cap):
    if n <= cap:
        return n
    best = 0
    for t in range(LANES, cap + 1, LANES):
        if n % t == 0:
            best = t
    assert best, (n, cap)
    return best


def _mm(a, b, mode, out_dtype, name, *, tm, tn, tk, scale=None, add=None):
    if mode == "nn":
        (M, K), N = a.shape, b.shape[1]
    elif mode == "nt":
        (M, K), N = a.shape, b.shape[0]
    else:
        (K, M), N = a.shape, b.shape[1]
    assert M % tm == 0 and N % tn == 0 and K % tk == 0, (name, M, N, K, tm, tn, tk)
    nk = K // tk
    dn = {"nn": NN, "nt": NT, "tn": TN}[mode]
    has_add = add is not None

    def body(*refs):
        a_ref, b_ref = refs[0], refs[1]
        add_ref = refs[2] if has_add else None
        o_ref = refs[3] if has_add else refs[2]
        acc_ref = refs[-1] if nk > 1 else None
        k = pl.program_id(2)
        part = lax.dot_general(a_ref[...], b_ref[...], dn, preferred_element_type=F32)

        def finish(r):
            if scale is not None:
                r = r * scale
            if has_add:
                r = r + add_ref[...]
            o_ref[...] = r.astype(out_dtype)

        if nk == 1:
            finish(part)
        else:
            @pl.when(k == 0)
            def _():
                acc_ref[...] = part

            @pl.when(k > 0)
            def _():
                acc_ref[...] += part

            @pl.when(k == nk - 1)
            def _():
                finish(acc_ref[...])

    if mode == "nn":
        a_spec = pl.BlockSpec((tm, tk), lambda j, i, k: (i, k))
        b_spec = pl.BlockSpec((tk, tn), lambda j, i, k: (k, j))
    elif mode == "nt":
        a_spec = pl.BlockSpec((tm, tk), lambda j, i, k: (i, k))
        b_spec = pl.BlockSpec((tn, tk), lambda j, i, k: (j, k))
    else:
        a_spec = pl.BlockSpec((tk, tm), lambda j, i, k: (k, i))
        b_spec = pl.BlockSpec((tk, tn), lambda j, i, k: (k, j))
    o_spec = pl.BlockSpec((tm, tn), lambda j, i, k: (i, j))
    in_specs = [a_spec, b_spec] + ([o_spec] if has_add else [])
    args = (a, b) + ((add,) if has_add else ())
    return pl.pallas_call(
        body, name=name, grid=(N // tn, M // tm, nk),
        in_specs=in_specs, out_specs=o_spec,
        out_shape=jax.ShapeDtypeStruct((M, N), out_dtype),
        scratch_shapes=[pltpu.VMEM((tm, tn), F32)] if nk > 1 else [],
        compiler_params=_cparams("parallel", "parallel", "arbitrary"),
    )(*args)


def _ln_math(z, g, b):
    mu = jnp.mean(z, axis=-1, keepdims=True)
    zc = z - mu
    var = jnp.mean(zc * zc, axis=-1, keepdims=True)
    rstd = lax.rsqrt(var + LN_EPS)
    xh = zc * rstd
    return xh * g + b, xh, rstd


def _pool_ln_fwd(h0, pw, ps, g, b, name, *, tm):
    Lp, D = h0.shape
    G = D // N_GROUPS
    halo_blocks = tm // MAX_WINDOW

    def body(x_ref, halo_ref, pw_ref, ps_ref, g_ref, b_ref,
             diff_ref, mix_ref, h_ref, hb_ref, xh_ref, rs_ref, ext_ref):
        i = pl.program_id(0)
        ext_ref[0:MAX_WINDOW, :] = jnp.where(i == 0, 0.0, halo_ref[...])
        ext_ref[MAX_WINDOW:MAX_WINDOW + tm, :] = x_ref[...]
        t1 = (i * tm + 1 + lax.broadcasted_iota(jnp.int32, (tm, 1), 0)).astype(F32)
        for gi, w in enumerate(POOL_WINDOWS):
            lo, hi = gi * G, (gi + 1) * G
            xg = x_ref[:, lo:hi]
            win = xg
            for j in range(1, w):
                win = win + ext_ref[MAX_WINDOW - j:MAX_WINDOW - j + tm, lo:hi]
            d = (win / jnp.minimum(t1, float(w)) - xg).astype(BF16)
            diff_ref[:, lo:hi] = d
            mix_ref[:, lo:hi] = jnp.dot(d, pw_ref[gi], preferred_element_type=F32)
        z = ALPHA * x_ref[...] + mix_ref[...] * ps_ref[...]
        h, xh, rstd = _ln_math(z, g_ref[...], b_ref[...])
        h_ref[...] = h
        hb_ref[...] = h.astype(BF16)
        xh_ref[...] = xh
        rs_ref[...] = rstd

    row = pl.BlockSpec((tm, D), lambda i: (i, 0))
    vec = pl.BlockSpec((1, D), lambda i: (0, 0))
    return pl.pallas_call(
        body, name=name, grid=(Lp // tm,),
        in_specs=[row,
                  pl.BlockSpec((MAX_WINDOW, D), lambda i: (jnp.maximum(i * halo_blocks - 1, 0), 0)),
                  pl.BlockSpec((N_GROUPS, G, G), lambda i: (0, 0, 0)), vec, vec, vec],
        out_specs=[row, row, row, row, row, pl.BlockSpec((tm, 1), lambda i: (i, 0))],
        out_shape=[jax.ShapeDtypeStruct((Lp, D), BF16), jax.ShapeDtypeStruct((Lp, D), F32),
                   jax.ShapeDtypeStruct((Lp, D), F32), jax.ShapeDtypeStruct((Lp, D), BF16),
                   jax.ShapeDtypeStruct((Lp, D), F32), jax.ShapeDtypeStruct((Lp, 1), F32)],
        scratch_shapes=[pltpu.VMEM((tm + MAX_WINDOW, D), F32)],
        compiler_params=_cparams("parallel"),
    )(h0, h0, pw, ps, g, b)


def _pool_bwd(dz, mixpre, pw, ps, name, *, tm):
    Lp, D = dz.shape
    G = D // N_GROUPS
    halo_blocks = tm // MAX_WINDOW
    n_halo = Lp // MAX_WINDOW
    ni = Lp // tm
    R = tm + MAX_WINDOW

    def body(dz_ref, halo_ref, mix_ref, pw_ref, ps_ref, dh_ref, dmb_ref, dsc_ref, ext_ref, dp_ref):
        i = pl.program_id(0)
        ext_ref[0:tm, :] = dz_ref[...]
        ext_ref[tm:R, :] = jnp.where(i == ni - 1, 0.0, halo_ref[...])
        dmix = (ext_ref[...] * ps_ref[...]).astype(BF16)
        dmb_ref[...] = dmix[0:tm]

        @pl.when(i == 0)
        def _():
            dsc_ref[...] = jnp.zeros_like(dsc_ref)

        dsc_ref[...] += jnp.sum(dz_ref[...] * mix_ref[...], axis=0, keepdims=True)
        t1 = (i * tm + 1 + lax.broadcasted_iota(jnp.int32, (R, 1), 0)).astype(F32)
        for gi, w in enumerate(POOL_WINDOWS):
            lo, hi = gi * G, (gi + 1) * G
            dd = lax.dot_general(dmix[:, lo:hi], pw_ref[gi], NT, preferred_element_type=F32)
            dp_ref[:, lo:hi] = dd / jnp.minimum(t1, float(w))
            back = dp_ref[0:tm, lo:hi]
            for j in range(1, w):
                back = back + dp_ref[j:j + tm, lo:hi]
            dh_ref[:, lo:hi] = ALPHA * dz_ref[:, lo:hi] - dd[0:tm] + back

    row = pl.BlockSpec((tm, D), lambda i: (i, 0))
    vec = pl.BlockSpec((1, D), lambda i: (0, 0))
    return pl.pallas_call(
        body, name=name, grid=(ni,),
        in_specs=[row,
                  pl.BlockSpec((MAX_WINDOW, D), lambda i: (jnp.minimum((i + 1) * halo_blocks, n_halo - 1), 0)),
                  row, pl.BlockSpec((N_GROUPS, G, G), lambda i: (0, 0, 0)), vec],
        out_specs=[row, row, vec],
        out_shape=[jax.ShapeDtypeStruct((Lp, D), F32), jax.ShapeDtypeStruct((Lp, D), BF16),
                   jax.ShapeDtypeStruct((1, D), F32)],
        scratch_shapes=[pltpu.VMEM((R, D), F32), pltpu.VMEM((R, D), F32)],
        compiler_params=_cparams("arbitrary"),
    )(dz, dz, mixpre, pw, ps)


def _pool_dw(diffb, dmb, name, *, tk):
    Lp, D = diffb.shape
    G = D // N_GROUPS

    def body(a_ref, b_ref, o_ref):
        @pl.when(pl.program_id(1) == 0)
        def _():
            o_ref[...] = jnp.zeros_like(o_ref)

        o_ref[0] += lax.dot_general(a_ref[...], b_ref[...], TN, preferred_element_type=F32)

    blk = pl.BlockSpec((tk, G), lambda g, k: (k, g))
    return pl.pallas_call(
        body, name=name, grid=(N_GROUPS, Lp // tk),
        in_specs=[blk, blk], out_specs=pl.BlockSpec((1, G, G), lambda g, k: (g, 0, 0)),
        out_shape=jax.ShapeDtypeStruct((N_GROUPS, G, G), F32),
        compiler_params=_cparams("parallel", "arbitrary"),
    )(diffb, dmb)


def _ln_fwd(resid, y, g, b, name, *, tm):
    Lp, D = resid.shape

    def body(r_ref, y_ref, g_ref, b_ref, h_ref, hb_ref, xh_ref, rs_ref):
        h, xh, rstd = _ln_math(ALPHA * r_ref[...] + y_ref[...], g_ref[...], b_ref[...])
        h_ref[...] = h
        hb_ref[...] = h.astype(BF16)
        xh_ref[...] = xh
        rs_ref[...] = rstd

    row = pl.BlockSpec((tm, D), lambda i: (i, 0))
    vec = pl.BlockSpec((1, D), lambda i: (0, 0))
    return pl.pallas_call(
        body, name=name, grid=(Lp // tm,),
        in_specs=[row, row, vec, vec],
        out_specs=[row, row, row, pl.BlockSpec((tm, 1), lambda i: (i, 0))],
        out_shape=[jax.ShapeDtypeStruct((Lp, D), F32), jax.ShapeDtypeStruct((Lp, D), BF16),
                   jax.ShapeDtypeStruct((Lp, D), F32), jax.ShapeDtypeStruct((Lp, 1), F32)],
        compiler_params=_cparams("parallel"),
    )(resid, y, g, b)


def _ln_bwd(parts, coefs, xh, rs, g, name, *, tm):
    Lp, D = xh.shape
    n = len(parts)

    def body(*refs):
        part_refs = refs[:n]
        xh_ref, rs_ref, g_ref = refs[n:n + 3]
        dz_ref, dzb_ref, dg_ref, db_ref = refs[n + 3:]
        dy = part_refs[0][...] if coefs[0] == 1.0 else coefs[0] * part_refs[0][...]
        for c, r in zip(coefs[1:], part_refs[1:]):
            dy = dy + (r[...] if c == 1.0 else c * r[...])
        x = xh_ref[...]
        dxh = dy * g_ref[...]
        m1 = jnp.mean(dxh, axis=-1, keepdims=True)
        m2 = jnp.mean(dxh * x, axis=-1, keepdims=True)
        dz = rs_ref[...] * (dxh - m1 - x * m2)
        dz_ref[...] = dz
        dzb_ref[...] = dz.astype(BF16)

        @pl.when(pl.program_id(0) == 0)
        def _():
            dg_ref[...] = jnp.zeros_like(dg_ref)
            db_ref[...] = jnp.zeros_like(db_ref)

        dg_ref[...] += jnp.sum(dy * x, axis=0, keepdims=True)
        db_ref[...] += jnp.sum(dy, axis=0, keepdims=True)

    row = pl.BlockSpec((tm, D), lambda i: (i, 0))
    vec = pl.BlockSpec((1, D), lambda i: (0, 0))
    return pl.pallas_call(
        body, name=name, grid=(Lp // tm,),
        in_specs=[row] * n + [row, pl.BlockSpec((tm, 1), lambda i: (i, 0)), vec],
        out_specs=[row, row, vec, vec],
        out_shape=[jax.ShapeDtypeStruct((Lp, D), F32), jax.ShapeDtypeStruct((Lp, D), BF16),
                   jax.ShapeDtypeStruct((1, D), F32), jax.ShapeDtypeStruct((1, D), F32)],
        compiler_params=_cparams("arbitrary"),
    )(*parts, xh, rs, g)


def _loss_head(h, tgt, name, *, tm, row_lo, row_hi):
    Lp, D = h.shape

    def body(h_ref, t_ref, dy_ref, loss_ref):
        i = pl.program_id(0)
        r = i * tm + lax.broadcasted_iota(jnp.int32, (tm, 1), 0)
        valid = (r >= row_lo) & (r < row_hi)
        e = jnp.where(valid, h_ref[...] - t_ref[...], 0.0)
        dy_ref[...] = e * (1.0 / D)

        @pl.when(i == 0)
        def _():
            loss_ref[...] = jnp.zeros_like(loss_ref)

        loss_ref[...] += 0.5 * jnp.sum(jnp.mean(e * e, axis=-1, keepdims=True), axis=0, keepdims=True)

    row = pl.BlockSpec((tm, D), lambda i: (i, 0))
    return pl.pallas_call(
        body, name=name, grid=(Lp // tm,),
        in_specs=[row, row], out_specs=[row, pl.BlockSpec((1, 1), lambda i: (0, 0))],
        out_shape=[jax.ShapeDtypeStruct((Lp, D), F32), jax.ShapeDtypeStruct((1, 1), F32)],
        compiler_params=_cparams("arbitrary"),
    )(h, tgt)


def _conv_rows(e_ref, cw_ref, cb_ref, r0, n):
    return (cb_ref[...] + cw_ref[0:1, :] * e_ref[r0 - 2:r0 - 2 + n, :]
            + cw_ref[1:2, :] * e_ref[r0 - 1:r0 - 1 + n, :] + cw_ref[2:3, :] * e_ref[r0:r0 + n, :])


def _conv_glu_fwd(u, cw, cb, name, *, tm, tn):
    Lp, F2 = u.shape
    F = F2 // 2
    nj = F // tn
    halo_blocks = tm // SUBLANES

    def body(ua_ref, ug_ref, pa_ref, pg_ref, cwa_ref, cwg_ref, cba_ref, cbg_ref, o_ref, ea_ref, eg_ref):
        first = pl.program_id(1) == 0
        for u_ref, p_ref, e_ref in ((ua_ref, pa_ref, ea_ref), (ug_ref, pg_ref, eg_ref)):
            e_ref[0:SUBLANES, :] = jnp.where(first, 0.0, p_ref[...])
            e_ref[SUBLANES:SUBLANES + tm, :] = u_ref[...]
        a = _conv_rows(ea_ref, cwa_ref, cba_ref, SUBLANES, tm)
        gate = _conv_rows(eg_ref, cwg_ref, cbg_ref, SUBLANES, tm)
        o_ref[...] = (a * jax.nn.sigmoid(a) * gate).astype(BF16)

    def prev(off):
        return pl.BlockSpec((SUBLANES, tn), lambda j, i: (jnp.maximum(i * halo_blocks - 1, 0), j + off))

    def cols(rows, off):
        return pl.BlockSpec((rows, tn), lambda j, i: (0, j + off))

    return pl.pallas_call(
        body, name=name, grid=(nj, Lp // tm),
        in_specs=[pl.BlockSpec((tm, tn), lambda j, i: (i, j)), pl.BlockSpec((tm, tn), lambda j, i: (i, j + nj)),
                  prev(0), prev(nj), cols(CONV_WIDTH, 0), cols(CONV_WIDTH, nj), cols(1, 0), cols(1, nj)],
        out_specs=pl.BlockSpec((tm, tn), lambda j, i: (i, j)),
        out_shape=jax.ShapeDtypeStruct((Lp, F), BF16),
        scratch_shapes=[pltpu.VMEM((tm + SUBLANES, tn), F32)] * 2,
        compiler_params=_cparams("parallel", "parallel"),
    )(u, u, u, u, cw, cw, cb, cb)


def _conv_glu_bwd(u, dact, cw, cb, name, *, tm, tn):
    Lp, F2 = u.shape
    F = F2 // 2
    nj = F // tn
    ni = Lp // tm
    halo_blocks = tm // SUBLANES
    n_halo = Lp // SUBLANES
    H = SUBLANES

    def body(ua_ref, ug_ref, pa_ref, pg_ref, na_ref, ng_ref, da_ref, dn_ref,
             cwa_ref, cwg_ref, cba_ref, cbg_ref,
             dua_ref, dug_ref, dwa_ref, dwg_ref, dba_ref, dbg_ref,
             ea_ref, eg_ref, dca_ref, dcg_ref):
        i = pl.program_id(1)
        first, last = i == 0, i == ni - 1
        for u_ref, p_ref, n_ref, e_ref in ((ua_ref, pa_ref, na_ref, ea_ref), (ug_ref, pg_ref, ng_ref, eg_ref)):
            e_ref[0:H, :] = jnp.where(first, 0.0, p_ref[...])
            e_ref[H:H + tm, :] = u_ref[...]
            e_ref[H + tm:2 * H + tm, :] = n_ref[...]

        @pl.when(first)
        def _():
            for r in (dwa_ref, dwg_ref, dba_ref, dbg_ref):
                r[...] = jnp.zeros_like(r)

        for r0, n, dact_rows in ((H, tm, da_ref[...]), (H + tm, H, jnp.where(last, 0.0, dn_ref[...]))):
            a = _conv_rows(ea_ref, cwa_ref, cba_ref, r0, n)
            gate = _conv_rows(eg_ref, cwg_ref, cbg_ref, r0, n)
            sg = jax.nn.sigmoid(a)
            dca_ref[r0 - H:r0 - H + n, :] = dact_rows * gate * (sg * (1.0 + a * (1.0 - sg)))
            dcg_ref[r0 - H:r0 - H + n, :] = dact_rows * (a * sg)

        for e_ref, dc_ref, cw_ref, du_ref, dw_ref, db_ref in (
                (ea_ref, dca_ref, cwa_ref, dua_ref, dwa_ref, dba_ref),
                (eg_ref, dcg_ref, cwg_ref, dug_ref, dwg_ref, dbg_ref)):
            dc = dc_ref[0:tm, :]
            du_ref[...] = (cw_ref[2:3, :] * dc + cw_ref[1:2, :] * dc_ref[1:1 + tm, :]
                           + cw_ref[0:1, :] * dc_ref[2:2 + tm, :]).astype(BF16)
            db_ref[...] += jnp.sum(dc, axis=0, keepdims=True)
            for k in range(CONV_WIDTH):
                dw_ref[k:k + 1, :] += jnp.sum(dc * e_ref[H - 2 + k:H - 2 + k + tm, :], axis=0, keepdims=True)

    def tile(off):
        return pl.BlockSpec((tm, tn), lambda j, i: (i, j + off))

    def prev(off):
        return pl.BlockSpec((H, tn), lambda j, i: (jnp.maximum(i * halo_blocks - 1, 0), j + off))

    def nxt(off):
        return pl.BlockSpec((H, tn), lambda j, i: (jnp.minimum((i + 1) * halo_blocks, n_halo - 1), j + off))

    def cols(rows, off):
        return pl.BlockSpec((rows, tn), lambda j, i: (0, j + off))

    outs = pl.pallas_call(
        body, name=name, grid=(nj, ni),
        in_specs=[tile(0), tile(nj), prev(0), prev(nj), nxt(0), nxt(nj), tile(0), nxt(0),
                  cols(CONV_WIDTH, 0), cols(CONV_WIDTH, nj), cols(1, 0), cols(1, nj)],
        out_specs=[tile(0), tile(0), cols(CONV_WIDTH, 0), cols(CONV_WIDTH, 0), cols(1, 0), cols(1, 0)],
        out_shape=[jax.ShapeDtypeStruct((Lp, F), BF16), jax.ShapeDtypeStruct((Lp, F), BF16),
                   jax.ShapeDtypeStruct((CONV_WIDTH, F), F32), jax.ShapeDtypeStruct((CONV_WIDTH, F), F32),
                   jax.ShapeDtypeStruct((1, F), F32), jax.ShapeDtypeStruct((1, F), F32)],
        scratch_shapes=[pltpu.VMEM((tm + 2 * H, tn), F32)] * 4,
        compiler_params=_cparams("parallel", "arbitrary"),
    )(u, u, u, u, u, u, dact, dact, cw, cw, cb, cb)
    return outs


def _logf_cumsum(pre, bf, name, *, tm):
    Lp, W = pre.shape

    def body(p_ref, b_ref, c_ref, carry_ref):
        i = pl.program_id(0)

        @pl.when(i == 0)
        def _():
            carry_ref[...] = jnp.zeros_like(carry_ref)

        x = p_ref[...] + b_ref[...]
        lf = jnp.minimum(x, 0.0) - jnp.log(1.0 + jnp.exp(-jnp.abs(x)))
        tri = (lax.broadcasted_iota(jnp.int32, (tm, tm), 0) >= lax.broadcasted_iota(jnp.int32, (tm, tm), 1)).astype(F32)
        c = jnp.dot(tri, lf, precision=lax.Precision.HIGHEST, preferred_element_type=F32) + carry_ref[...]
        c_ref[...] = c
        carry_ref[...] = c[tm - 1:tm, :]

    row = pl.BlockSpec((tm, W), lambda i: (i, 0))
    return pl.pallas_call(
        body, name=name, grid=(Lp // tm,),
        in_specs=[row, pl.BlockSpec((1, W), lambda i: (0, 0))], out_specs=row,
        out_shape=jax.ShapeDtypeStruct((Lp, W), F32),
        scratch_shapes=[pltpu.VMEM((1, W), F32)],
        compiler_params=_cparams("arbitrary"),
    )(pre, bf)


def _logf_bwd(dc_a, dc_b, pre, bf, name, *, tm):
    Lp, W = pre.shape
    ni = Lp // tm

    def body(dca_ref, dcb_ref, p_ref, b_ref, dpb_ref, db_ref, carry_ref):
        i = pl.program_id(0)

        @pl.when(i == 0)
        def _():
            carry_ref[...] = jnp.zeros_like(carry_ref)
            db_ref[...] = jnp.zeros_like(db_ref)

        triu = (lax.broadcasted_iota(jnp.int32, (tm, tm), 0) <= lax.broadcasted_iota(jnp.int32, (tm, tm), 1)).astype(F32)
        dl = jnp.dot(triu, dca_ref[...] + dcb_ref[...], precision=lax.Precision.HIGHEST,
                     preferred_element_type=F32) + carry_ref[...]
        carry_ref[...] = dl[0:1, :]
        dp = dl * jax.nn.sigmoid(-(p_ref[...] + b_ref[...]))
        dpb_ref[...] = dp.astype(BF16)
        db_ref[...] += jnp.sum(dp, axis=0, keepdims=True)

    rev = pl.BlockSpec((tm, W), lambda i: (ni - 1 - i, 0))
    vec = pl.BlockSpec((1, W), lambda i: (0, 0))
    return pl.pallas_call(
        body, name=name, grid=(ni,),
        in_specs=[rev, rev, rev, vec], out_specs=[rev, vec],
        out_shape=[jax.ShapeDtypeStruct((Lp, W), BF16), jax.ShapeDtypeStruct((1, W), F32)],
        scratch_shapes=[pltpu.VMEM((1, W), F32)],
        compiler_params=_cparams("arbitrary"),
    )(dc_a, dc_b, pre, bf)


def _attn_fwd(qh, kh4, vh4, ccol, crow4, name, *, tq):
    H, Lp, dh = qh.shape
    nq = Lp // tq

    def body(q_ref, k_ref, v_ref, ct_ref, cs_ref, o_ref, lse_ref, m_ref, l_ref, acc_ref):
        i = pl.program_id(1)
        q = q_ref[0]
        ct = ct_ref[0]
        m_ref[...] = jnp.full_like(m_ref, NEG_INF)
        l_ref[...] = jnp.zeros_like(l_ref)
        acc_ref[...] = jnp.zeros_like(acc_ref)

        def chunk(j, masked):
            s = lax.dot_general(q, k_ref[0, j], NT, preferred_element_type=F32) + (ct - cs_ref[0, j])
            if masked:
                keep = lax.broadcasted_iota(jnp.int32, (tq, tq), 0) >= lax.broadcasted_iota(jnp.int32, (tq, tq), 1)
                s = jnp.where(keep, s, NEG_INF)
            m_prev = m_ref[...]
            m_new = jnp.maximum(m_prev, jnp.max(s, axis=1, keepdims=True))
            p = jnp.exp(s - m_new)
            a = jnp.exp(m_prev - m_new)
            l_ref[...] = a * l_ref[...] + jnp.sum(p, axis=1, keepdims=True)
            acc_ref[...] = a * acc_ref[...] + jnp.dot(p.astype(BF16), v_ref[0, j], preferred_element_type=F32)
            m_ref[...] = m_new

        def step(j, carry):
            chunk(j, False)
            return carry

        lax.fori_loop(0, i, step, 0)
        chunk(i, True)
        o_ref[0] = acc_ref[...] / l_ref[...]
        lse_ref[0] = m_ref[...] + jnp.log(l_ref[...])

    return pl.pallas_call(
        body, name=name, grid=(H, nq),
        in_specs=[pl.BlockSpec((1, tq, dh), lambda h, i: (h, i, 0)),
                  pl.BlockSpec((1, nq, tq, dh), lambda h, i: (h, 0, 0, 0)),
                  pl.BlockSpec((1, nq, tq, dh), lambda h, i: (h, 0, 0, 0)),
                  pl.BlockSpec((1, tq, 1), lambda h, i: (h, i, 0)),
                  pl.BlockSpec((1, nq, 1, tq), lambda h, i: (h, 0, 0, 0))],
        out_specs=[pl.BlockSpec((1, tq, dh), lambda h, i: (h, i, 0)),
                   pl.BlockSpec((1, tq, 1), lambda h, i: (h, i, 0))],
        out_shape=[jax.ShapeDtypeStruct((H, Lp, dh), F32), jax.ShapeDtypeStruct((H, Lp, 1), F32)],
        scratch_shapes=[pltpu.VMEM((tq, 1), F32), pltpu.VMEM((tq, 1), F32), pltpu.VMEM((tq, dh), F32)],
        compiler_params=_cparams("parallel", "parallel"),
    )(qh, kh4, vh4, ccol, crow4)


def _attn_delta(do, o, name, *, tm, n_heads):
    Lp, D = do.shape

    def body(do_ref, o_ref, d_ref):
        sel = (lax.broadcasted_iota(jnp.int32, (D, LANES), 0) // HEAD_DIM
               == lax.broadcasted_iota(jnp.int32, (D, LANES), 1)).astype(F32)
        do = do_ref[...].astype(BF16).astype(F32)
        d_ref[...] = jnp.dot(do * o_ref[...], sel, precision=lax.Precision.HIGHEST,
                             preferred_element_type=F32)

    row = pl.BlockSpec((tm, D), lambda i: (i, 0))
    return pl.pallas_call(
        body, name=name, grid=(Lp // tm,),
        in_specs=[row, row], out_specs=pl.BlockSpec((tm, LANES), lambda i: (i, 0)),
        out_shape=jax.ShapeDtypeStruct((Lp, LANES), F32),
        compiler_params=_cparams("parallel"),
    )(do, o)


def _attn_bwd(qh4, doh4, kh, vh, lse4, delta4, crow4, ccol, name, *, tq):
    H, nq, _, dh = qh4.shape
    Lp = nq * tq

    def body(q_ref, do_ref, k_ref, v_ref, lse_ref, dl_ref, ct_ref, cs_ref,
             dq_ref, dk_ref, dv_ref, dcs_ref, dcq_ref, dk_acc, dv_acc, dc_acc):
        j = pl.program_id(1)

        @pl.when(j == 0)
        def _():
            dq_ref[...] = jnp.zeros_like(dq_ref)
            dcq_ref[...] = jnp.zeros_like(dcq_ref)

        k = k_ref[0]
        v = v_ref[0]
        cs = cs_ref[0]
        dk_acc[...] = jnp.zeros_like(dk_acc)
        dv_acc[...] = jnp.zeros_like(dv_acc)
        dc_acc[...] = jnp.zeros_like(dc_acc)

        def pair(i, masked):
            q = q_ref[0, i]
            do = do_ref[0, i]
            st = lax.dot_general(k, q, NT, preferred_element_type=F32) + (ct_ref[0, i] - cs)
            if masked:
                keep = lax.broadcasted_iota(jnp.int32, (tq, tq), 1) >= lax.broadcasted_iota(jnp.int32, (tq, tq), 0)
                st = jnp.where(keep, st, NEG_INF)
            pt = jnp.exp(st - lse_ref[0, i])
            dv_acc[...] += jnp.dot(pt.astype(BF16), do, preferred_element_type=F32)
            dpt = lax.dot_general(v, do, NT, preferred_element_type=F32)
            dst = pt * (dpt - dl_ref[0, i])
            dc_acc[...] += jnp.sum(dst, axis=1, keepdims=True)
            dcq_ref[0, i] += jnp.sum(dst, axis=0, keepdims=True)
            dsb = dst.astype(BF16)
            dk_acc[...] += jnp.dot(dsb, q, preferred_element_type=F32)
            dq_ref[0, i] += lax.dot_general(dsb, k, TN, preferred_element_type=F32)

        def step(i, carry):
            pair(i, False)
            return carry

        pair(j, True)
        lax.fori_loop(j + 1, nq, step, 0)
        dk_ref[0] = dk_acc[...]
        dv_ref[0] = dv_acc[...]
        dcs_ref[0] = -dc_acc[...]

    whole = pl.BlockSpec((1, nq, tq, dh), lambda h, j: (h, 0, 0, 0))
    tile = pl.BlockSpec((1, tq, dh), lambda h, j: (h, j, 0))
    rows = pl.BlockSpec((1, nq, 1, tq), lambda h, j: (h, 0, 0, 0))
    col = pl.BlockSpec((1, tq, 1), lambda h, j: (h, j, 0))
    return pl.pallas_call(
        body, name=name, grid=(H, nq),
        in_specs=[whole, whole, tile, tile, rows, rows, rows, col],
        out_specs=[whole, tile, tile, col, rows],
        out_shape=[jax.ShapeDtypeStruct((H, nq, tq, dh), F32), jax.ShapeDtypeStruct((H, Lp, dh), F32),
                   jax.ShapeDtypeStruct((H, Lp, dh), F32), jax.ShapeDtypeStruct((H, Lp, 1), F32),
                   jax.ShapeDtypeStruct((H, nq, 1, tq), F32)],
        scratch_shapes=[pltpu.VMEM((tq, dh), F32), pltpu.VMEM((tq, dh), F32), pltpu.VMEM((tq, 1), F32)],
        compiler_params=_cparams("parallel", "arbitrary"),
    )(qh4, doh4, kh, vh, lse4, delta4, crow4, ccol)


def _remote(src, dst, send_sems, recv_sems, k, to):
    return pltpu.make_async_remote_copy(src_ref=src, dst_ref=dst, send_sem=send_sems.at[k], recv_sem=recv_sems.at[k],
                                        device_id=to, device_id_type=MESH)


def _place():
    x, y, c = lax.axis_index("x"), lax.axis_index("y"), lax.axis_index("c")
    other_chips = [(1 - x, y), (x, 1 - y), (1 - x, 1 - y)]
    return x, y, c, other_chips


def _all_gather_weights(wb, wf, name):
    Rb, C = wb.shape
    Rf = wf.shape[0]
    hb = Rb // 2

    def body(wb_ref, wf_ref, ob_ref, of_ref, send_sems, recv_sems, local_sems):
        x, y, c, chips = _place()
        me = 2 * x + y
        sibling = (x, y, 1 - c)

        def half(chip, core):
            return ob_ref.at[chip, pl.ds(core * hb, hb), :]

        own_b = pltpu.make_async_copy(wb_ref, ob_ref.at[me], local_sems.at[0])
        own_f = pltpu.make_async_copy(wf_ref, of_ref.at[me], local_sems.at[1])
        own_b.start()
        own_f.start()
        sent = []
        for j, (cx, cy) in enumerate(chips):
            sent.append(_remote(wb_ref.at[pl.ds(c * hb, hb), :], half(me, c), send_sems, recv_sems, j, (cx, cy, c)))
            sent.append(_remote(wf_ref, of_ref.at[me], send_sems, recv_sems, 3 + j, (cx, cy, c)))
        for cp in sent:
            cp.start()
        for j, (cx, cy) in enumerate(chips):
            chip = 2 * cx + cy
            _remote(half(chip, c), half(chip, c), send_sems, recv_sems, j, sibling).wait_recv()
            fwd = _remote(half(chip, c), half(chip, c), send_sems, recv_sems, 6 + j, sibling)
            fwd.start()
            sent.append(fwd)
        for j, (cx, cy) in enumerate(chips):
            chip = 2 * cx + cy
            _remote(wf_ref, of_ref.at[chip], send_sems, recv_sems, 3 + j, sibling).wait_recv()
            _remote(half(chip, 1 - c), half(chip, 1 - c), send_sems, recv_sems, 6 + j, sibling).wait_recv()
        for cp in sent:
            cp.wait_send()
        own_b.wait()
        own_f.wait()

    any_spec = pl.BlockSpec(memory_space=pl.ANY)
    return pl.pallas_call(
        body, name=name,
        in_specs=[any_spec, any_spec], out_specs=[any_spec, any_spec],
        out_shape=[jax.ShapeDtypeStruct((N_CHIPS, Rb, C), BF16), jax.ShapeDtypeStruct((N_CHIPS, Rf, C), F32)],
        scratch_shapes=[pltpu.SemaphoreType.DMA((9,)), pltpu.SemaphoreType.DMA((9,)), pltpu.SemaphoreType.DMA((2,))],
    )(wb, wf)


def _sibling_swap(src, name):
    def body(src_ref, dst_ref, send_sems, recv_sems):
        x, y, c, _ = _place()
        cp = _remote(src_ref, dst_ref, send_sems, recv_sems, 0, (x, y, 1 - c))
        cp.start()
        cp.wait()

    any_spec = pl.BlockSpec(memory_space=pl.ANY)
    return pl.pallas_call(
        body, name=name, in_specs=[any_spec], out_specs=any_spec,
        out_shape=jax.ShapeDtypeStruct(src.shape, src.dtype),
        scratch_shapes=[pltpu.SemaphoreType.DMA((1,)), pltpu.SemaphoreType.DMA((1,))],
    )(src)


def _chip_exchange(part, rep, name):
    _, hr, C = part.shape
    rr = rep.shape[0]

    def body(part_ref, rep_ref, land_ref, reps_ref, send_sems, recv_sems, local_sem):
        x, y, c, chips = _place()
        me = 4 * x + 2 * y + c
        own = pltpu.make_async_copy(rep_ref, reps_ref.at[me], local_sem.at[0])
        own.start()
        sent = []
        for j, (cx, cy) in enumerate(chips):
            sent.append(_remote(part_ref.at[2 * cx + cy], land_ref.at[j], send_sems, recv_sems, j, (cx, cy, c)))
        for r in range(1, N_DEV):
            fx, fy, fc = (r >> 2) & 1, (r >> 1) & 1, r & 1
            to = (x ^ fx, y ^ fy, c ^ fc)
            sent.append(_remote(rep_ref, reps_ref.at[me], send_sems, recv_sems, 2 + r, to))
        for cp in sent:
            cp.start()
        for j in range(3):
            _remote(part_ref.at[0], land_ref.at[j], send_sems, recv_sems, j, (x, y, c)).wait_recv()
        for r in range(1, N_DEV):
            fx, fy, fc = (r >> 2) & 1, (r >> 1) & 1, r & 1
            frm = 4 * (x ^ fx) + 2 * (y ^ fy) + (c ^ fc)
            _remote(rep_ref, reps_ref.at[frm], send_sems, recv_sems, 2 + r, (x, y, c)).wait_recv()
        for cp in sent:
            cp.wait_send()
        own.wait()

    any_spec = pl.BlockSpec(memory_space=pl.ANY)
    return pl.pallas_call(
        body, name=name, in_specs=[any_spec, any_spec], out_specs=[any_spec, any_spec],
        out_shape=[jax.ShapeDtypeStruct((3, hr, C), F32), jax.ShapeDtypeStruct((N_DEV, rr, C), F32)],
        scratch_shapes=[pltpu.SemaphoreType.DMA((10,)), pltpu.SemaphoreType.DMA((10,)), pltpu.SemaphoreType.DMA((1,))],
    )(part, rep)


def _adamw_math(w, g, m, v):
    m = ADAM_B1 * m + (1.0 - ADAM_B1) * g
    v = ADAM_B2 * v + (1.0 - ADAM_B2) * (g * g)
    m_hat = m / (1.0 - ADAM_B1 ** ADAM_STEP)
    v_hat = v / (1.0 - ADAM_B2 ** ADAM_STEP)
    delta = -ADAM_LR * (m_hat / (jnp.sqrt(v_hat) + ADAM_EPS) + ADAM_WD * w)
    return delta, m, v


def _add_halves(a, b, name, *, tr):
    n, hr, C = a.shape

    def body(a_ref, b_ref, o_ref):
        o_ref[...] = a_ref[...] + b_ref[...]

    blk = pl.BlockSpec((1, tr, C), lambda s, i: (s, i, 0))
    return pl.pallas_call(
        body, name=name, grid=(n, hr // tr), in_specs=[blk, blk], out_specs=blk,
        out_shape=jax.ShapeDtypeStruct(a.shape, F32), compiler_params=_cparams("parallel", "parallel"),
    )(a, b)


def _sum_adamw(own, landed, w, m, v, name, *, tr):
    n = landed.shape[0]
    hr, C = own.shape

    def body(own_ref, land_ref, w_ref, m_ref, v_ref, o_ref):
        g = own_ref[...]
        for s in range(n):
            g = g + land_ref[s]
        delta, m_new, v_new = _adamw_math(w_ref[...], g, m_ref[...], v_ref[...])
        o_ref[0] = g
        o_ref[1] = delta
        o_ref[2] = m_new
        o_ref[3] = v_new

    blk = pl.BlockSpec((tr, C), lambda i: (i, 0))
    return pl.pallas_call(
        body, name=name, grid=(hr // tr,),
        in_specs=[blk, pl.BlockSpec((n, tr, C), lambda i: (0, i, 0)), blk, blk, blk],
        out_specs=pl.BlockSpec((4, tr, C), lambda i: (0, i, 0)),
        out_shape=jax.ShapeDtypeStruct((4, hr, C), F32), compiler_params=_cparams("parallel"),
    )(own, landed, w, m, v)


def _rows_of(shape):
    n = 1
    for d in shape:
        n *= d
    return -(-n // PACK_COLS)


def _pack(arrays, total_rows, dtype):
    parts = []
    for a in arrays:
        flat = a.reshape(-1).astype(dtype)
        parts.append(jnp.pad(flat, (0, _rows_of(a.shape) * PACK_COLS - flat.shape[0])))
    flat = jnp.concatenate(parts)
    flat = jnp.pad(flat, (0, total_rows * PACK_COLS - flat.shape[0]))
    return flat.reshape(total_rows, PACK_COLS)


def _unpack(buf, shapes):
    lead = buf.shape[:-2]
    out, r = [], 0
    for shp in shapes:
        n = 1
        for d in shp:
            n *= d
        rows = _rows_of(shp)
        piece = buf[..., r:r + rows, :].reshape(lead + (rows * PACK_COLS,))[..., :n]
        out.append(piece.reshape(lead + tuple(shp)))
        r += rows
    return out


def _join_shards(stacked, axis):
    return jnp.concatenate([stacked[s] for s in range(N_CHIPS)], axis=axis)


def _split_shards(full, axis):
    return jnp.stack(jnp.split(full, N_CHIPS, axis=axis))


def _local_step(h0, tgt, W, *, seq, tm):
    Lp, D = h0.shape
    H = D // HEAD_DIM
    F2 = W["ffn_w_in"].shape[-1]
    F = F2 // 2
    te = tm // 2
    nq = Lp // tm
    cap = 1408
    tD, tF, tF2 = _pick(D, cap), _pick(F, cap), _pick(F2, cap)
    t2D = _pick(2 * D, cap)
    tcn = _pick(F, cap)

    def vec(a):
        return a.reshape(1, -1)

    ln_g, ln_b = W["ln_g"], W["ln_b"]
    wf_pad = jnp.pad(W["w_f"], ((0, 0), (0, LANES - H)))
    bf_pad = jnp.pad(W["b_f"], (0, LANES - H)).reshape(1, LANES)

    def ffn_fwd(hb, l, tag):
        u = _mm(hb, W["ffn_w_in"][l], "nn", F32, f"ffn{tag}_up", tm=tm, tn=tF2, tk=tD)
        act = _conv_glu_fwd(u, W["ffn_conv_w"][l], vec(W["ffn_conv_b"][l]), f"ffn{tag}_glu", tm=te, tn=tcn)
        y = _mm(act, W["ffn_w_out"][l], "nn", F32, f"ffn{tag}_down", tm=tm, tn=tD, tk=tF)
        return u, act, y

    def ffn_bwd(dzb, hb, u, act, l, tag):
        dact = _mm(dzb, W["ffn_w_out"][l], "nt", F32, f"ffn{tag}_dact", tm=tm, tn=tF, tk=tD)
        dw_out = _mm(act, dzb, "tn", F32, f"ffn{tag}_dwout", tm=tF, tn=tD, tk=tm)
        dua, dug, dwa, dwg, dba, dbg = _conv_glu_bwd(u, dact, W["ffn_conv_w"][l], vec(W["ffn_conv_b"][l]),
                                                     f"ffn{tag}_dglu", tm=te, tn=tcn)
        du = jnp.concatenate([dua, dug], axis=1)
        dcw = jnp.concatenate([dwa, dwg], axis=1)
        dcb = jnp.concatenate([dba, dbg], axis=1)
        dh = _mm(du, W["ffn_w_in"][l], "nt", F32, f"ffn{tag}_dh", tm=tm, tn=tD, tk=tF2)
        dw_in = _mm(hb, du, "tn", F32, f"ffn{tag}_dwin", tm=tD, tn=tF2, tk=tm)
        return dh, dw_in, dw_out, dcw, dcb[0]

    diffb, mixpre, h1, h1b, xh1, rs1 = _pool_ln_fwd(h0, W["pool_w"][0], W["pool_scale"], vec(ln_g[0, 0]),
                                                    vec(ln_b[0, 0]), "pool_ln_fwd", tm=te)
    u0, act0, y0 = ffn_fwd(h1b, 0, "0")
    h2, h2b, xh2, rs2 = _ln_fwd(h1, y0, vec(ln_g[0, 1]), vec(ln_b[0, 1]), "ln01_fwd", tm=te)

    kvb = _mm(h2b, W["w_kv"], "nn", BF16, "kv_proj", tm=tm, tn=t2D, tk=tD)
    qb = _mm(h2b, W["w_q"][0], "nn", BF16, "q_proj", tm=tm, tn=tD, tk=tD, scale=HEAD_DIM ** -0.5)
    pre = _mm(h2b, wf_pad, "nn", F32, "f_proj", tm=tm, tn=LANES, tk=tD)
    c = _logf_cumsum(pre, bf_pad, "logf_cumsum", tm=tm)

    def heads(a):
        return a.reshape(Lp, H, HEAD_DIM).transpose(1, 0, 2)

    def tokens(a):
        return a.transpose(1, 0, 2).reshape(Lp, D)

    qh, kh, vh = heads(qb), heads(kvb[:, :D]), heads(kvb[:, D:])
    c_t = c[:, :H].T
    ccol = c_t.reshape(H, Lp, 1)
    crow4 = c_t.reshape(H, nq, 1, tm)
    oh, lse = _attn_fwd(qh, kh.reshape(H, nq, tm, HEAD_DIM), vh.reshape(H, nq, tm, HEAD_DIM), ccol, crow4,
                        "attn_fwd", tq=tm)
    o_tok = tokens(oh)
    ob = o_tok.astype(BF16)
    y_attn = _mm(ob, W["w_o"][0], "nn", F32, "o_proj", tm=tm, tn=tD, tk=tD)
    h3, h3b, xh3, rs3 = _ln_fwd(h2, y_attn, vec(ln_g[1, 0]), vec(ln_b[1, 0]), "ln10_fwd", tm=te)
    u1, act1, y1 = ffn_fwd(h3b, 1, "1")
    h4, _, xh4, rs4 = _ln_fwd(h3, y1, vec(ln_g[1, 1]), vec(ln_b[1, 1]), "ln11_fwd", tm=te)
    dy, loss = _loss_head(h4, tgt, "loss_head", tm=te, row_lo=N_META, row_hi=N_META + seq)

    dz4, dz4b, dg11, db11 = _ln_bwd([dy], [1.0], xh4, rs4, vec(ln_g[1, 1]), "ln11_bwd", tm=te)
    dh3, dw_in1, dw_out1, dcw1, dcb1 = ffn_bwd(dz4b, h3b, u1, act1, 1, "1")
    dz3, dz3b, dg10, db10 = _ln_bwd([dz4, dh3], [ALPHA, 1.0], xh3, rs3, vec(ln_g[1, 0]), "ln10_bwd", tm=te)

    do_tok = _mm(dz3b, W["w_o"][0], "nt", F32, "o_proj_dx", tm=tm, tn=tD, tk=tD)
    dw_o = _mm(ob, dz3b, "tn", F32, "o_proj_dw", tm=tD, tn=tD, tk=tm)
    delta = _attn_delta(do_tok, o_tok, "attn_delta", tm=te, n_heads=H)
    doh4 = heads(do_tok.astype(BF16)).reshape(H, nq, tm, HEAD_DIM)
    dqh4, dkh, dvh, dcs, dcq = _attn_bwd(qh.reshape(H, nq, tm, HEAD_DIM), doh4, kh, vh,
                                    lse.reshape(H, nq, 1, tm), delta[:, :H].T.reshape(H, nq, 1, tm), crow4, ccol,
                                    "attn_bwd", tq=tm)
    dqb = tokens(dqh4.reshape(H, Lp, HEAD_DIM)).astype(BF16)
    dkvb = jnp.concatenate([tokens(dkh), tokens(dvh)], axis=1).astype(BF16)
    dc_keys = jnp.pad(dcs.reshape(H, Lp).T, ((0, 0), (0, LANES - H)))
    dc_queries = jnp.pad(dcq.reshape(H, Lp).T, ((0, 0), (0, LANES - H)))
    dpreb, dbf = _logf_bwd(dc_keys, dc_queries, pre, bf_pad, "logf_bwd", tm=tm)

    qs = HEAD_DIM ** -0.5
    dw_q = _mm(h2b, dqb, "tn", F32, "q_proj_dw", tm=tD, tn=tD, tk=tm, scale=qs)
    dw_kv = _mm(h2b, dkvb, "tn", F32, "kv_proj_dw", tm=tD, tn=t2D, tk=tm)
    dw_f = _mm(h2b, dpreb, "tn", F32, "f_proj_dw", tm=tD, tn=LANES, tk=tm)[:, :H]
    dh2 = _mm(dqb, W["w_q"][0], "nt", F32, "q_proj_dx", tm=tm, tn=tD, tk=tD, scale=qs)
    dh2 = _mm(dkvb, W["w_kv"], "nt", F32, "kv_proj_dx", tm=tm, tn=tD, tk=t2D, add=dh2)
    dh2 = _mm(dpreb, wf_pad, "nt", F32, "f_proj_dx", tm=tm, tn=tD, tk=LANES, add=dh2)
    dz2, dz2b, dg01, db01 = _ln_bwd([dz3, dh2], [ALPHA, 1.0], xh2, rs2, vec(ln_g[0, 1]), "ln01_bwd", tm=te)

    dh1, dw_in0, dw_out0, dcw0, dcb0 = ffn_bwd(dz2b, h1b, u0, act0, 0, "0")
    dz1, _, dg00, db00 = _ln_bwd([dz2, dh1], [ALPHA, 1.0], xh1, rs1, vec(ln_g[0, 0]), "ln00_bwd", tm=te)
    dh0, dmb, dscale = _pool_bwd(dz1, mixpre, W["pool_w"][0], W["pool_scale"], "pool_bwd", tm=te)
    dw_pool = _pool_dw(diffb, dmb, "pool_dw", tk=tm)

    grads = {
        "meta": dh0[:N_META],
        "pool_w": dw_pool[None],
        "pool_scale": dscale,
        "w_kv": dw_kv,
        "w_f": dw_f,
        "b_f": dbf[0, :H],
        "w_q": dw_q[None],
        "w_o": dw_o[None],
        "ffn_w_in": jnp.stack([dw_in0, dw_in1]),
        "ffn_conv_w": jnp.stack([dcw0, dcw1]),
        "ffn_conv_b": jnp.stack([dcb0, dcb1]),
        "ffn_w_out": jnp.stack([dw_out0, dw_out1]),
        "ln_g": jnp.stack([jnp.stack([dg00[0], dg01[0]]), jnp.stack([dg10[0], dg11[0]])]),
        "ln_b": jnp.stack([jnp.stack([db00[0], db01[0]]), jnp.stack([db10[0], db11[0]])]),
    }
    return loss, dh0, grads


def _row_tile(length):
    return 640 if length >= 4096 else 128


def kernel(x, meta, pool_w, pool_scale, w_kv, w_f, b_f, w_q, w_o, ffn_w_in, ffn_conv_w, ffn_conv_b, ffn_w_out, ln_g, ln_b, loss_target, m_meta, m_pool_w, m_pool_scale, m_w_kv, m_w_f, m_b_f, m_w_q, m_w_o, m_ffn_w_in, m_ffn_conv_w, m_ffn_conv_b, m_ffn_w_out, m_ln_g, m_ln_b, v_meta, v_pool_w, v_pool_scale, v_w_kv, v_w_f, v_b_f, v_w_q, v_w_o, v_ffn_w_in, v_ffn_conv_w, v_ffn_conv_b, v_ffn_w_out, v_ln_g, v_ln_b):
    weights = dict(meta=meta, pool_w=pool_w, pool_scale=pool_scale, w_kv=w_kv, w_f=w_f, b_f=b_f, w_q=w_q, w_o=w_o,
                   ffn_w_in=ffn_w_in, ffn_conv_w=ffn_conv_w, ffn_conv_b=ffn_conv_b, ffn_w_out=ffn_w_out,
                   ln_g=ln_g, ln_b=ln_b)
    mom1 = dict(meta=m_meta, pool_w=m_pool_w, pool_scale=m_pool_scale, w_kv=m_w_kv, w_f=m_w_f, b_f=m_b_f, w_q=m_w_q,
                w_o=m_w_o, ffn_w_in=m_ffn_w_in, ffn_conv_w=m_ffn_conv_w, ffn_conv_b=m_ffn_conv_b,
                ffn_w_out=m_ffn_w_out, ln_g=m_ln_g, ln_b=m_ln_b)
    mom2 = dict(meta=v_meta, pool_w=v_pool_w, pool_scale=v_pool_scale, w_kv=v_w_kv, w_f=v_w_f, b_f=v_b_f, w_q=v_w_q,
                w_o=v_w_o, ffn_w_in=v_ffn_w_in, ffn_conv_w=v_ffn_conv_w, ffn_conv_b=v_ffn_conv_b,
                ffn_w_out=v_ffn_w_out, ln_g=v_ln_g, ln_b=v_ln_b)
    _, seq, D = x.shape
    L = N_META + seq
    tm = _row_tile(L)
    Lp = _round_up(L, tm)
    c_idx = lax.axis_index("c")
    chip = 2 * lax.axis_index("x") + lax.axis_index("y")

    shard_shapes = {n: weights[n].shape for n in SHARDED}
    rows_b = _round_up(sum(_rows_of(shard_shapes[n]) for n in MATMUL_WEIGHTS), 32)
    rows_f = _round_up(sum(_rows_of(shard_shapes[n]) for n in VECTOR_WEIGHTS), SUBLANES)
    wb = _pack([weights[n] for n in MATMUL_WEIGHTS], rows_b, BF16)
    wf = _pack([weights[n] for n in VECTOR_WEIGHTS], rows_f, F32)
    gb, gf = _all_gather_weights(wb, wf, "weights_all_gather")
    full = {}
    for names, buf in ((MATMUL_WEIGHTS, gb), (VECTOR_WEIGHTS, gf)):
        for n, stacked in zip(names, _unpack(buf, [shard_shapes[n] for n in names])):
            full[n] = _join_shards(stacked, SHARD_AXIS[n])
    full["b_f"] = b_f
    full["ffn_conv_b"] = ffn_conv_b

    pad = jnp.zeros((Lp - L, D), F32)
    h0 = jnp.concatenate([full["meta"], x[0], pad], axis=0)
    tgt = jnp.concatenate([jnp.zeros((N_META, D), F32), loss_target[0], pad], axis=0)
    loss, dh0, grads = _local_step(h0, tgt, full, seq=seq, tm=tm)
    loss = lax.psum(loss[0, 0], AXES)
    grad_x = dh0[N_META:L][None]

    rows = _round_up(sum(_rows_of(shard_shapes[n]) for n in SHARDED), 2 * LANES)
    hr = rows // 2
    shapes = [shard_shapes[n] for n in SHARDED]
    per_chip = [_split_shards(grads[n], SHARD_AXIS[n]) for n in SHARDED]
    gpack = jnp.stack([_pack([p[s] for p in per_chip], rows, F32) for s in range(N_CHIPS)])
    keep = lax.dynamic_slice_in_dim(gpack, c_idx * hr, hr, axis=1)
    give = lax.dynamic_slice_in_dim(gpack, (1 - c_idx) * hr, hr, axis=1)
    got = _sibling_swap(give, "grads_to_sibling")
    part = _add_halves(keep, got, "grads_chip_sum", tr=LANES)

    rep_shapes = [weights[n].shape for n in REPLICATED]
    rows_r = _round_up(sum(_rows_of(s) for s in rep_shapes), SUBLANES)
    rep = _pack([grads[n] for n in REPLICATED], rows_r, F32)
    landed, reps = _chip_exchange(part, rep, "grads_chip_exchange")

    def my_half(d):
        return lax.dynamic_slice_in_dim(_pack([d[n] for n in SHARDED], rows, F32), c_idx * hr, hr, axis=0)

    own = lax.dynamic_index_in_dim(part, chip, axis=0, keepdims=False)
    mine = _sum_adamw(own, landed, my_half(weights), my_half(mom1), my_half(mom2), "adamw_sharded", tr=LANES)
    theirs = _sibling_swap(mine, "results_to_sibling")
    res = jnp.zeros((4, rows, PACK_COLS), F32)
    res = lax.dynamic_update_slice_in_dim(res, mine, c_idx * hr, axis=1)
    res = lax.dynamic_update_slice_in_dim(res, theirs, (1 - c_idx) * hr, axis=1)
    sharded_out = _unpack(res, shapes)

    def packr(d):
        return _pack([d[n] for n in REPLICATED], rows_r, F32)

    res_r = _sum_adamw(reps[0], reps[1:], packr(weights), packr(mom1), packr(mom2), "adamw_replicated", tr=rows_r)
    rep_out = _unpack(res_r, rep_shapes)

    out = {n: a for n, a in zip(SHARDED, sharded_out)}
    out.update({n: a for n, a in zip(REPLICATED, rep_out)})
    result = [loss, grad_x]
    for k in range(4):
        result += [out[n][k] for n in WEIGHT_ORDER]
    return tuple(result)
```

```python
import functools

import jax
import jax.numpy as jnp
from jax import lax
from jax.experimental import pallas as pl
from jax.experimental.pallas import tpu as pltpu

N_META = 16
POOL_WINDOWS = (2, 4, 8, 16)
MAX_WINDOW = max(POOL_WINDOWS)
N_GROUPS = len(POOL_WINDOWS)
HEAD_DIM = 64
DEPTH = 2
CONV_WIDTH = 3
ALPHA = (2.0 * DEPTH) ** 0.25
LN_EPS = 1e-5
NEG_INF = -1e30
ADAM_LR = 0.001
ADAM_B1 = 0.9
ADAM_B2 = 0.999
ADAM_EPS = 1e-08
ADAM_WD = 0.01
ADAM_STEP = 10

F32 = jnp.float32
BF16 = jnp.bfloat16
ATTN_STRIP = 32
ATTN_HEADS = 2
LANES = 128
SUBLANES = 8
PACK_COLS = 1024
VMEM_LIMIT = 56 * 1024 * 1024
AXES = ("x", "y", "c")
MESH = pl.DeviceIdType.MESH

NN = (((1,), (0,)), ((), ()))
NT = (((1,), (1,)), ((), ()))
TN = (((0,), (0,)), ((), ()))

SHARD_AXIS = {"meta": 1, "pool_w": 2, "pool_scale": 1, "w_kv": 1, "w_f": 0, "w_q": 1, "w_o": 1,
              "ffn_w_in": 2, "ffn_conv_w": 2, "ffn_w_out": 1, "ln_g": 2, "ln_b": 2}
SHARDED = ("meta", "pool_w", "pool_scale", "w_kv", "w_f", "w_q", "w_o", "ffn_w_in", "ffn_conv_w",
           "ffn_w_out", "ln_g", "ln_b")
REPLICATED = ("b_f", "ffn_conv_b")
MATMUL_WEIGHTS = ("pool_w", "w_kv", "w_f", "w_q", "w_o", "ffn_w_in", "ffn_w_out")
VECTOR_WEIGHTS = ("meta", "pool_scale", "ffn_conv_w", "ln_g", "ln_b")
WEIGHT_ORDER = ("meta", "pool_w", "pool_scale", "w_kv", "w_f", "b_f", "w_q", "w_o", "ffn_w_in",
                "ffn_conv_w", "ffn_conv_b", "ffn_w_out", "ln_g", "ln_b")
N_CHIPS = 4
N_DEV = 8


def _cparams(*sem):
    return pltpu.CompilerParams(dimension_semantics=sem, vmem_limit_bytes=VMEM_LIMIT)


def _round_up(n, m):
    return (n + m - 1) // m * m


def _pick(n, cap):
    if n <= cap:
        return n
    best = 0
    for t in range(LANES, cap + 1, LANES):
        if n % t == 0:
            best = t
    assert best, (n, cap)
    return best


def _mm(a, b, mode, out_dtype, name, *, tm, tn, tk, scale=None, add=None):
    if mode == "nn":
        (M, K), N = a.shape, b.shape[1]
    elif mode == "nt":
        (M, K), N = a.shape, b.shape[0]
    else:
        (K, M), N = a.shape, b.shape[1]
    assert M % tm == 0 and N % tn == 0 and K % tk == 0, (name, M, N, K, tm, tn, tk)
    nk = K // tk
    dn = {"nn": NN, "nt": NT, "tn": TN}[mode]
    has_add = add is not None

    def body(*refs):
        a_ref, b_ref = refs[0], refs[1]
        add_ref = refs[2] if has_add else None
        o_ref = refs[3] if has_add else refs[2]
        acc_ref = refs[-1] if nk > 1 else None
        k = pl.program_id(2)
        part = lax.dot_general(a_ref[...], b_ref[...], dn, preferred_element_type=F32)

        def finish(r):
            if scale is not None:
                r = r * scale
            if has_add:
                r = r + add_ref[...]
            o_ref[...] = r.astype(out_dtype)

        if nk == 1:
            finish(part)
        else:
            @pl.when(k == 0)
            def _():
                acc_ref[...] = part

            @pl.when(k > 0)
            def _():
                acc_ref[...] += part

            @pl.when(k == nk - 1)
            def _():
                finish(acc_ref[...])

    if mode == "nn":
        a_spec = pl.BlockSpec((tm, tk), lambda j, i, k: (i, k))
        b_spec = pl.BlockSpec((tk, tn), lambda j, i, k: (k, j))
    elif mode == "nt":
        a_spec = pl.BlockSpec((tm, tk), lambda j, i, k: (i, k))
        b_spec = pl.BlockSpec((tn, tk), lambda j, i, k: (j, k))
    else:
        a_spec = pl.BlockSpec((tk, tm), lambda j, i, k: (k, i))
        b_spec = pl.BlockSpec((tk, tn), lambda j, i, k: (k, j))
    o_spec = pl.BlockSpec((tm, tn), lambda j, i, k: (i, j))
    in_specs = [a_spec, b_spec] + ([o_spec] if has_add else [])
    args = (a, b) + ((add,) if has_add else ())
    return pl.pallas_call(
        body, name=name, grid=(N // tn, M // tm, nk),
        in_specs=in_specs, out_specs=o_spec,
        out_shape=jax.ShapeDtypeStruct((M, N), out_dtype),
        scratch_shapes=[pltpu.VMEM((tm, tn), F32)] if nk > 1 else [],
        compiler_params=_cparams("parallel", "parallel", "arbitrary"),
    )(*args)


def _ln_math(z, g, b):
    mu = jnp.mean(z, axis=-1, keepdims=True)
    zc = z - mu
    var = jnp.mean(zc * zc, axis=-1, keepdims=True)
    rstd = lax.rsqrt(var + LN_EPS)
    xh = zc * rstd
    return xh * g + b, xh, rstd


def _pool_ln_fwd(h0, pw, ps, g, b, name, *, tm):
    Lp, D = h0.shape
    G = D // N_GROUPS
    halo_blocks = tm // MAX_WINDOW

    def body(x_ref, halo_ref, pw_ref, ps_ref, g_ref, b_ref,
             diff_ref, mix_ref, h_ref, hb_ref, xh_ref, rs_ref, ext_ref):
        i = pl.program_id(0)
        ext_ref[0:MAX_WINDOW, :] = jnp.where(i == 0, 0.0, halo_ref[...])
        ext_ref[MAX_WINDOW:MAX_WINDOW + tm, :] = x_ref[...]
        t1 = (i * tm + 1 + lax.broadcasted_iota(jnp.int32, (tm, 1), 0)).astype(F32)
        for gi, w in enumerate(POOL_WINDOWS):
            lo, hi = gi * G, (gi + 1) * G
            xg = x_ref[:, lo:hi]
            win = xg
            for j in range(1, w):
                win = win + ext_ref[MAX_WINDOW - j:MAX_WINDOW - j + tm, lo:hi]
            d = (win / jnp.minimum(t1, float(w)) - xg).astype(BF16)
            diff_ref[:, lo:hi] = d
            mix_ref[:, lo:hi] = jnp.dot(d, pw_ref[gi], preferred_element_type=F32)
        z = ALPHA * x_ref[...] + mix_ref[...] * ps_ref[...]
        h, xh, rstd = _ln_math(z, g_ref[...], b_ref[...])
        h_ref[...] = h
        hb_ref[...] = h.astype(BF16)
        xh_ref[...] = xh
        rs_ref[...] = rstd

    row = pl.BlockSpec((tm, D), lambda i: (i, 0))
    vec = pl.BlockSpec((1, D), lambda i: (0, 0))
    return pl.pallas_call(
        body, name=name, grid=(Lp // tm,),
        in_specs=[row,
                  pl.BlockSpec((MAX_WINDOW, D), lambda i: (jnp.maximum(i * halo_blocks - 1, 0), 0)),
                  pl.BlockSpec((N_GROUPS, G, G), lambda i: (0, 0, 0)), vec, vec, vec],
        out_specs=[row, row, row, row, row, pl.BlockSpec((tm, 1), lambda i: (i, 0))],
        out_shape=[jax.ShapeDtypeStruct((Lp, D), BF16), jax.ShapeDtypeStruct((Lp, D), F32),
                   jax.ShapeDtypeStruct((Lp, D), F32), jax.ShapeDtypeStruct((Lp, D), BF16),
                   jax.ShapeDtypeStruct((Lp, D), F32), jax.ShapeDtypeStruct((Lp, 1), F32)],
        scratch_shapes=[pltpu.VMEM((tm + MAX_WINDOW, D), F32)],
        compiler_params=_cparams("parallel"),
    )(h0, h0, pw, ps, g, b)


def _pool_bwd(dz, mixpre, pw, ps, name, *, tm):
    Lp, D = dz.shape
    G = D // N_GROUPS
    halo_blocks = tm // MAX_WINDOW
    n_halo = Lp // MAX_WINDOW
    ni = Lp // tm
    R = tm + MAX_WINDOW

    def body(dz_ref, halo_ref, mix_ref, pw_ref, ps_ref, dh_ref, dmb_ref, dsc_ref, ext_ref, dp_ref):
        i = pl.program_id(0)
        ext_ref[0:tm, :] = dz_ref[...]
        ext_ref[tm:R, :] = jnp.where(i == ni - 1, 0.0, halo_ref[...])
        dmix = (ext_ref[...] * ps_ref[...]).astype(BF16)
        dmb_ref[...] = dmix[0:tm]

        @pl.when(i == 0)
        def _():
            dsc_ref[...] = jnp.zeros_like(dsc_ref)

        dsc_ref[...] += jnp.sum(dz_ref[...] * mix_ref[...], axis=0, keepdims=True)
        t1 = (i * tm + 1 + lax.broadcasted_iota(jnp.int32, (R, 1), 0)).astype(F32)
        for gi, w in enumerate(POOL_WINDOWS):
            lo, hi = gi * G, (gi + 1) * G
            dd = lax.dot_general(dmix[:, lo:hi], pw_ref[gi], NT, preferred_element_type=F32)
            dp_ref[:, lo:hi] = dd / jnp.minimum(t1, float(w))
            back = dp_ref[0:tm, lo:hi]
            for j in range(1, w):
                back = back + dp_ref[j:j + tm, lo:hi]
            dh_ref[:, lo:hi] = ALPHA * dz_ref[:, lo:hi] - dd[0:tm] + back

    row = pl.BlockSpec((tm, D), lambda i: (i, 0))
    vec = pl.BlockSpec((1, D), lambda i: (0, 0))
    return pl.pallas_call(
        body, name=name, grid=(ni,),
        in_specs=[row,
                  pl.BlockSpec((MAX_WINDOW, D), lambda i: (jnp.minimum((i + 1) * halo_blocks, n_halo - 1), 0)),
                  row, pl.BlockSpec((N_GROUPS, G, G), lambda i: (0, 0, 0)), vec],
        out_specs=[row, row, vec],
        out_shape=[jax.ShapeDtypeStruct((Lp, D), F32), jax.ShapeDtypeStruct((Lp, D), BF16),
                   jax.ShapeDtypeStruct((1, D), F32)],
        scratch_shapes=[pltpu.VMEM((R, D), F32), pltpu.VMEM((R, D), F32)],
        compiler_params=_cparams("arbitrary"),
    )(dz, dz, mixpre, pw, ps)


def _pool_dw(diffb, dmb, name, *, tk):
    Lp, D = diffb.shape
    G = D // N_GROUPS

    def body(a_ref, b_ref, o_ref):
        @pl.when(pl.program_id(1) == 0)
        def _():
            o_ref[...] = jnp.zeros_like(o_ref)

        o_ref[0] += lax.dot_general(a_ref[...], b_ref[...], TN, preferred_element_type=F32)

    blk = pl.BlockSpec((tk, G), lambda g, k: (k, g))
    return pl.pallas_call(
        body, name=name, grid=(N_GROUPS, Lp // tk),
        in_specs=[blk, blk], out_specs=pl.BlockSpec((1, G, G), lambda g, k: (g, 0, 0)),
        out_shape=jax.ShapeDtypeStruct((N_GROUPS, G, G), F32),
        compiler_params=_cparams("parallel", "arbitrary"),
    )(diffb, dmb)


def _ln_fwd(resid, y, g, b, name, *, tm):
    Lp, D = resid.shape

    def body(r_ref, y_ref, g_ref, b_ref, h_ref, hb_ref, xh_ref, rs_ref):
        h, xh, rstd = _ln_math(ALPHA * r_ref[...] + y_ref[...], g_ref[...], b_ref[...])
        h_ref[...] = h
        hb_ref[...] = h.astype(BF16)
        xh_ref[...] = xh
        rs_ref[...] = rstd

    row = pl.BlockSpec((tm, D), lambda i: (i, 0))
    vec = pl.BlockSpec((1, D), lambda i: (0, 0))
    return pl.pallas_call(
        body, name=name, grid=(Lp // tm,),
        in_specs=[row, row, vec, vec],
        out_specs=[row, row, row, pl.BlockSpec((tm, 1), lambda i: (i, 0))],
        out_shape=[jax.ShapeDtypeStruct((Lp, D), F32), jax.ShapeDtypeStruct((Lp, D), BF16),
                   jax.ShapeDtypeStruct((Lp, D), F32), jax.ShapeDtypeStruct((Lp, 1), F32)],
        compiler_params=_cparams("parallel"),
    )(resid, y, g, b)


def _ln_bwd(parts, coefs, xh, rs, g, name, *, tm):
    Lp, D = xh.shape
    n = len(parts)

    def body(*refs):
        part_refs = refs[:n]
        xh_ref, rs_ref, g_ref = refs[n:n + 3]
        dz_ref, dzb_ref, dg_ref, db_ref = refs[n + 3:]
        dy = part_refs[0][...] if coefs[0] == 1.0 else coefs[0] * part_refs[0][...]
        for c, r in zip(coefs[1:], part_refs[1:]):
            dy = dy + (r[...] if c == 1.0 else c * r[...])
        x = xh_ref[...]
        dxh = dy * g_ref[...]
        m1 = jnp.mean(dxh, axis=-1, keepdims=True)
        m2 = jnp.mean(dxh * x, axis=-1, keepdims=True)
        dz = rs_ref[...] * (dxh - m1 - x * m2)
        dz_ref[...] = dz
        dzb_ref[...] = dz.astype(BF16)

        @pl.when(pl.program_id(0) == 0)
        def _():
            dg_ref[...] = jnp.zeros_like(dg_ref)
            db_ref[...] = jnp.zeros_like(db_ref)

        dg_ref[...] += jnp.sum(dy * x, axis=0, keepdims=True)
        db_ref[...] += jnp.sum(dy, axis=0, keepdims=True)

    row = pl.BlockSpec((tm, D), lambda i: (i, 0))
    vec = pl.BlockSpec((1, D), lambda i: (0, 0))
    return pl.pallas_call(
        body, name=name, grid=(Lp // tm,),
        in_specs=[row] * n + [row, pl.BlockSpec((tm, 1), lambda i: (i, 0)), vec],
        out_specs=[row, row, vec, vec],
        out_shape=[jax.ShapeDtypeStruct((Lp, D), F32), jax.ShapeDtypeStruct((Lp, D), BF16),
                   jax.ShapeDtypeStruct((1, D), F32), jax.ShapeDtypeStruct((1, D), F32)],
        compiler_params=_cparams("arbitrary"),
    )(*parts, xh, rs, g)


def _loss_head(h, tgt, name, *, tm, row_lo, row_hi):
    Lp, D = h.shape

    def body(h_ref, t_ref, dy_ref, loss_ref):
        i = pl.program_id(0)
        r = i * tm + lax.broadcasted_iota(jnp.int32, (tm, 1), 0)
        valid = (r >= row_lo) & (r < row_hi)
        e = jnp.where(valid, h_ref[...] - t_ref[...], 0.0)
        dy_ref[...] = e * (1.0 / D)

        @pl.when(i == 0)
        def _():
            loss_ref[...] = jnp.zeros_like(loss_ref)

        loss_ref[...] += 0.5 * jnp.sum(jnp.mean(e * e, axis=-1, keepdims=True), axis=0, keepdims=True)

    row = pl.BlockSpec((tm, D), lambda i: (i, 0))
    return pl.pallas_call(
        body, name=name, grid=(Lp // tm,),
        in_specs=[row, row], out_specs=[row, pl.BlockSpec((1, 1), lambda i: (0, 0))],
        out_shape=[jax.ShapeDtypeStruct((Lp, D), F32), jax.ShapeDtypeStruct((1, 1), F32)],
        compiler_params=_cparams("arbitrary"),
    )(h, tgt)


def _conv_rows(e_ref, cw_ref, cb_ref, r0, n):
    return (cb_ref[...] + cw_ref[0:1, :] * e_ref[r0 - 2:r0 - 2 + n, :]
            + cw_ref[1:2, :] * e_ref[r0 - 1:r0 - 1 + n, :] + cw_ref[2:3, :] * e_ref[r0:r0 + n, :])


def _conv_glu_fwd(u, cw, cb, name, *, tm, tn):
    Lp, F2 = u.shape
    F = F2 // 2
    nj = F // tn
    halo_blocks = tm // SUBLANES

    def body(ua_ref, ug_ref, pa_ref, pg_ref, cwa_ref, cwg_ref, cba_ref, cbg_ref, o_ref, ea_ref, eg_ref):
        first = pl.program_id(1) == 0
        for u_ref, p_ref, e_ref in ((ua_ref, pa_ref, ea_ref), (ug_ref, pg_ref, eg_ref)):
            e_ref[0:SUBLANES, :] = jnp.where(first, 0.0, p_ref[...])
            e_ref[SUBLANES:SUBLANES + tm, :] = u_ref[...]
        a = _conv_rows(ea_ref, cwa_ref, cba_ref, SUBLANES, tm)
        gate = _conv_rows(eg_ref, cwg_ref, cbg_ref, SUBLANES, tm)
        o_ref[...] = (a * jax.nn.sigmoid(a) * gate).astype(BF16)

    def prev(off):
        return pl.BlockSpec((SUBLANES, tn), lambda j, i: (jnp.maximum(i * halo_blocks - 1, 0), j + off))

    def cols(rows, off):
        return pl.BlockSpec((rows, tn), lambda j, i: (0, j + off))

    return pl.pallas_call(
        body, name=name, grid=(nj, Lp // tm),
        in_specs=[pl.BlockSpec((tm, tn), lambda j, i: (i, j)), pl.BlockSpec((tm, tn), lambda j, i: (i, j + nj)),
                  prev(0), prev(nj), cols(CONV_WIDTH, 0), cols(CONV_WIDTH, nj), cols(1, 0), cols(1, nj)],
        out_specs=pl.BlockSpec((tm, tn), lambda j, i: (i, j)),
        out_shape=jax.ShapeDtypeStruct((Lp, F), BF16),
        scratch_shapes=[pltpu.VMEM((tm + SUBLANES, tn), F32)] * 2,
        compiler_params=_cparams("parallel", "parallel"),
    )(u, u, u, u, cw, cw, cb, cb)


def _conv_glu_bwd(u, dact, cw, cb, name, *, tm, tn):
    Lp, F2 = u.shape
    F = F2 // 2
    nj = F // tn
    ni = Lp // tm
    halo_blocks = tm // SUBLANES
    n_halo = Lp // SUBLANES
    H = SUBLANES

    def body(ua_ref, ug_ref, pa_ref, pg_ref, na_ref, ng_ref, da_ref, dn_ref,
             cwa_ref, cwg_ref, cba_ref, cbg_ref,
             dua_ref, dug_ref, dwa_ref, dwg_ref, dba_ref, dbg_ref,
             ea_ref, eg_ref, dca_ref, dcg_ref):
        i = pl.program_id(1)
        first, last = i == 0, i == ni - 1
        for u_ref, p_ref, n_ref, e_ref in ((ua_ref, pa_ref, na_ref, ea_ref), (ug_ref, pg_ref, ng_ref, eg_ref)):
            e_ref[0:H, :] = jnp.where(first, 0.0, p_ref[...])
            e_ref[H:H + tm, :] = u_ref[...]
            e_ref[H + tm:2 * H + tm, :] = n_ref[...]

        @pl.when(first)
        def _():
            for r in (dwa_ref, dwg_ref, dba_ref, dbg_ref):
                r[...] = jnp.zeros_like(r)

        for r0, n, dact_rows in ((H, tm, da_ref[...]), (H + tm, H, jnp.where(last, 0.0, dn_ref[...]))):
            a = _conv_rows(ea_ref, cwa_ref, cba_ref, r0, n)
            gate = _conv_rows(eg_ref, cwg_ref, cbg_ref, r0, n)
            sg = jax.nn.sigmoid(a)
            dca_ref[r0 - H:r0 - H + n, :] = dact_rows * gate * (sg * (1.0 + a * (1.0 - sg)))
            dcg_ref[r0 - H:r0 - H + n, :] = dact_rows * (a * sg)

        for e_ref, dc_ref, cw_ref, du_ref, dw_ref, db_ref in (
                (ea_ref, dca_ref, cwa_ref, dua_ref, dwa_ref, dba_ref),
                (eg_ref, dcg_ref, cwg_ref, dug_ref, dwg_ref, dbg_ref)):
            dc = dc_ref[0:tm, :]
            du_ref[...] = (cw_ref[2:3, :] * dc + cw_ref[1:2, :] * dc_ref[1:1 + tm, :]
                           + cw_ref[0:1, :] * dc_ref[2:2 + tm, :]).astype(BF16)
            db_ref[...] += jnp.sum(dc, axis=0, keepdims=True)
            for k in range(CONV_WIDTH):
                dw_ref[k:k + 1, :] += jnp.sum(dc * e_ref[H - 2 + k:H - 2 + k + tm, :], axis=0, keepdims=True)

    def tile(off):
        return pl.BlockSpec((tm, tn), lambda j, i: (i, j + off))

    def prev(off):
        return pl.BlockSpec((H, tn), lambda j, i: (jnp.maximum(i * halo_blocks - 1, 0), j + off))

    def nxt(off):
        return pl.BlockSpec((H, tn), lambda j, i: (jnp.minimum((i + 1) * halo_blocks, n_halo - 1), j + off))

    def cols(rows, off):
        return pl.BlockSpec((rows, tn), lambda j, i: (0, j + off))

    outs = pl.pallas_call(
        body, name=name, grid=(nj, ni),
        in_specs=[tile(0), tile(nj), prev(0), prev(nj), nxt(0), nxt(nj), tile(0), nxt(0),
                  cols(CONV_WIDTH, 0), cols(CONV_WIDTH, nj), cols(1, 0), cols(1, nj)],
        out_specs=[tile(0), tile(0), cols(CONV_WIDTH, 0), cols(CONV_WIDTH, 0), cols(1, 0), cols(1, 0)],
        out_shape=[jax.ShapeDtypeStruct((Lp, F), BF16), jax.ShapeDtypeStruct((Lp, F), BF16),
                   jax.ShapeDtypeStruct((CONV_WIDTH, F), F32), jax.ShapeDtypeStruct((CONV_WIDTH, F), F32),
                   jax.ShapeDtypeStruct((1, F), F32), jax.ShapeDtypeStruct((1, F), F32)],
        scratch_shapes=[pltpu.VMEM((tm + 2 * H, tn), F32)] * 4,
        compiler_params=_cparams("parallel", "arbitrary"),
    )(u, u, u, u, u, u, dact, dact, cw, cw, cb, cb)
    return outs


def _logf_cumsum(pre, bf, name, *, tm):
    Lp, W = pre.shape

    def body(p_ref, b_ref, c_ref, carry_ref):
        i = pl.program_id(0)

        @pl.when(i == 0)
        def _():
            carry_ref[...] = jnp.zeros_like(carry_ref)

        x = p_ref[...] + b_ref[...]
        lf = jnp.minimum(x, 0.0) - jnp.log(1.0 + jnp.exp(-jnp.abs(x)))
        tri = (lax.broadcasted_iota(jnp.int32, (tm, tm), 0) >= lax.broadcasted_iota(jnp.int32, (tm, tm), 1)).astype(F32)
        c = jnp.dot(tri, lf, precision=lax.Precision.HIGHEST, preferred_element_type=F32) + carry_ref[...]
        c_ref[...] = c
        carry_ref[...] = c[tm - 1:tm, :]

    row = pl.BlockSpec((tm, W), lambda i: (i, 0))
    return pl.pallas_call(
        body, name=name, grid=(Lp // tm,),
        in_specs=[row, pl.BlockSpec((1, W), lambda i: (0, 0))], out_specs=row,
        out_shape=jax.ShapeDtypeStruct((Lp, W), F32),
        scratch_shapes=[pltpu.VMEM((1, W), F32)],
        compiler_params=_cparams("arbitrary"),
    )(pre, bf)


def _logf_bwd(dc_a, dc_b, pre, bf, name, *, tm):
    Lp, W = pre.shape
    ni = Lp // tm

    def body(dca_ref, dcb_ref, p_ref, b_ref, dpb_ref, db_ref, carry_ref):
        i = pl.program_id(0)

        @pl.when(i == 0)
        def _():
            carry_ref[...] = jnp.zeros_like(carry_ref)
            db_ref[...] = jnp.zeros_like(db_ref)

        triu = (lax.broadcasted_iota(jnp.int32, (tm, tm), 0) <= lax.broadcasted_iota(jnp.int32, (tm, tm), 1)).astype(F32)
        dl = jnp.dot(triu, dca_ref[...] + dcb_ref[...], precision=lax.Precision.HIGHEST,
                     preferred_element_type=F32) + carry_ref[...]
        carry_ref[...] = dl[0:1, :]
        dp = dl * jax.nn.sigmoid(-(p_ref[...] + b_ref[...]))
        dpb_ref[...] = dp.astype(BF16)
        db_ref[...] += jnp.sum(dp, axis=0, keepdims=True)

    rev = pl.BlockSpec((tm, W), lambda i: (ni - 1 - i, 0))
    vec = pl.BlockSpec((1, W), lambda i: (0, 0))
    return pl.pallas_call(
        body, name=name, grid=(ni,),
        in_specs=[rev, rev, rev, vec], out_specs=[rev, vec],
        out_shape=[jax.ShapeDtypeStruct((Lp, W), BF16), jax.ShapeDtypeStruct((1, W), F32)],
        scratch_shapes=[pltpu.VMEM((1, W), F32)],
        compiler_params=_cparams("arbitrary"),
    )(dc_a, dc_b, pre, bf)


def _attn_fwd(qh, kh4, vh4, ccol4, crow4, name, *, tq):
    H, Lp, dh = qh.shape
    nq = Lp // tq
    S8 = SUBLANES
    HB = ATTN_HEADS
    n_scratch = 5

    def to_column(row8):
        return jnp.transpose(jnp.concatenate([row8] * (LANES // S8), axis=0))

    def body(q_ref, k_ref, v_ref, ct_ref, cs_ref, o_ref, lse_ref, *scratch):
        i = pl.program_id(1)
        heads = [scratch[n_scratch * hb:n_scratch * (hb + 1)] for hb in range(HB)]
        for m_ref, l_ref, acc_ref, _, _ in heads:
            m_ref[...] = jnp.full_like(m_ref, NEG_INF)
            l_ref[...] = jnp.zeros_like(l_ref)
            acc_ref[...] = jnp.zeros_like(acc_ref)

        def chunk(j, masked):
            for hb, (_, _, _, st_ref, _) in enumerate(heads):
                st_ref[...] = lax.dot_general(k_ref[hb, j], q_ref[hb], NT, preferred_element_type=F32)
            for hb, (m_ref, l_ref, acc_ref, st_ref, pt_ref) in enumerate(heads):
                ct = ct_ref[hb, 0]
                mx = jnp.full((S8, tq), NEG_INF, F32)
                for r0 in range(0, tq, ATTN_STRIP):
                    rows = pl.ds(r0, ATTN_STRIP)
                    st = st_ref[rows, :] + (ct - cs_ref[hb, j, rows, :])
                    if masked:
                        keep = (lax.broadcasted_iota(jnp.int32, (ATTN_STRIP, tq), 1)
                                >= r0 + lax.broadcasted_iota(jnp.int32, (ATTN_STRIP, tq), 0))
                        st = jnp.where(keep, st, NEG_INF)
                    st_ref[rows, :] = st
                    for g0 in range(0, ATTN_STRIP, S8):
                        mx = jnp.maximum(mx, st[g0:g0 + S8])
                m_prev = m_ref[...]
                m_new = jnp.maximum(m_prev, jnp.max(mx, axis=0, keepdims=True))
                alpha = jnp.exp(m_prev - m_new)
                m_ref[...] = m_new
                ls = jnp.zeros((S8, tq), F32)
                for r0 in range(0, tq, ATTN_STRIP):
                    pieces = [jnp.exp(st_ref[pl.ds(r0 + g0, S8), :] - m_new) for g0 in range(0, ATTN_STRIP, S8)]
                    for piece in pieces:
                        ls = ls + piece
                    pt_ref[pl.ds(r0, ATTN_STRIP), :] = jnp.concatenate(pieces, axis=0).astype(BF16)
                l_ref[...] = alpha * l_ref[...] + ls
                pv = lax.dot_general(pt_ref[...], v_ref[hb, j], TN, preferred_element_type=F32)
                acc_ref[...] = to_column(alpha)[:, :dh] * acc_ref[...] + pv

        def step(j, carry):
            chunk(j, False)
            return carry

        lax.fori_loop(0, i, step, 0)
        chunk(i, True)
        for hb, (m_ref, l_ref, acc_ref, _, _) in enumerate(heads):
            l_row = jnp.sum(l_ref[...], axis=0, keepdims=True)
            l8 = jnp.concatenate([l_row] * S8, axis=0)
            o_ref[hb] = acc_ref[...] / to_column(l8)[:, :dh]
            lse_ref[hb, 0] = m_ref[0:1, :] + jnp.log(l_row)

    per_head = [pltpu.VMEM((S8, tq), F32), pltpu.VMEM((S8, tq), F32), pltpu.VMEM((tq, dh), F32),
                pltpu.VMEM((tq, tq), F32), pltpu.VMEM((tq, tq), BF16)]
    assert len(per_head) == n_scratch and H % HB == 0
    return pl.pallas_call(
        body, name=name, grid=(H // HB, nq),
        in_specs=[pl.BlockSpec((HB, tq, dh), lambda h, i: (h, i, 0)),
                  pl.BlockSpec((HB, nq, tq, dh), lambda h, i: (h, 0, 0, 0)),
                  pl.BlockSpec((HB, nq, tq, dh), lambda h, i: (h, 0, 0, 0)),
                  pl.BlockSpec((HB, 1, 1, tq), lambda h, i: (h, i, 0, 0)),
                  pl.BlockSpec((HB, nq, tq, 1), lambda h, i: (h, 0, 0, 0))],
        out_specs=[pl.BlockSpec((HB, tq, dh), lambda h, i: (h, i, 0)),
                   pl.BlockSpec((HB, 1, 1, tq), lambda h, i: (h, i, 0, 0))],
        out_shape=[jax.ShapeDtypeStruct((H, Lp, dh), F32), jax.ShapeDtypeStruct((H, nq, 1, tq), F32)],
        scratch_shapes=per_head * HB,
        compiler_params=_cparams("parallel", "parallel"),
    )(qh, kh4, vh4, crow4, ccol4)


def _attn_delta(do, o, name, *, tm, n_heads):
    Lp, D = do.shape

    def body(do_ref, o_ref, d_ref):
        sel = (lax.broadcasted_iota(jnp.int32, (D, LANES), 0) // HEAD_DIM
               == lax.broadcasted_iota(jnp.int32, (D, LANES), 1)).astype(F32)
        do = do_ref[...].astype(BF16).astype(F32)
        d_ref[...] = jnp.dot(do * o_ref[...], sel, precision=lax.Precision.HIGHEST,
                             preferred_element_type=F32)

    row = pl.BlockSpec((tm, D), lambda i: (i, 0))
    return pl.pallas_call(
        body, name=name, grid=(Lp // tm,),
        in_specs=[row, row], out_specs=pl.BlockSpec((tm, LANES), lambda i: (i, 0)),
        out_shape=jax.ShapeDtypeStruct((Lp, LANES), F32),
        compiler_params=_cparams("parallel"),
    )(do, o)


def _attn_bwd(qh4, doh4, kh, vh, lse4, delta4, crow4, ccol, name, *, tq):
    H, nq, _, dh = qh4.shape
    Lp = nq * tq

    def body(q_ref, do_ref, k_ref, v_ref, lse_ref, dl_ref, ct_ref, cs_ref,
             dq_ref, dk_ref, dv_ref, dcs_ref, dcq_ref, dk_acc, dv_acc, dc_acc, st_ref, dp_ref, pt_ref, ds_ref):
        j = pl.program_id(1)

        @pl.when(j == 0)
        def _():
            dq_ref[...] = jnp.zeros_like(dq_ref)
            dcq_ref[...] = jnp.zeros_like(dcq_ref)

        k = k_ref[0]
        v = v_ref[0]
        dk_acc[...] = jnp.zeros_like(dk_acc)
        dv_acc[...] = jnp.zeros_like(dv_acc)
        dc_acc[...] = jnp.zeros_like(dc_acc)

        def pair(i, masked):
            q = q_ref[0, i]
            do = do_ref[0, i]
            st_ref[...] = lax.dot_general(k, q, NT, preferred_element_type=F32)
            dp_ref[...] = lax.dot_general(v, do, NT, preferred_element_type=F32)
            bias_q = ct_ref[0, i] - lse_ref[0, i]
            delta = dl_ref[0, i]
            col_sum = jnp.zeros((SUBLANES, tq), F32)
            for r0 in range(0, tq, ATTN_STRIP):
                rows = pl.ds(r0, ATTN_STRIP)
                st = st_ref[rows, :] + (bias_q - cs_ref[0, rows, :])
                if masked:
                    keep = (lax.broadcasted_iota(jnp.int32, (ATTN_STRIP, tq), 1)
                            >= r0 + lax.broadcasted_iota(jnp.int32, (ATTN_STRIP, tq), 0))
                    st = jnp.where(keep, st, NEG_INF)
                pt = jnp.exp(st)
                dst = pt * (dp_ref[rows, :] - delta)
                pt_ref[rows, :] = pt.astype(BF16)
                ds_ref[rows, :] = dst.astype(BF16)
                dc_acc[rows, :] += jnp.sum(dst, axis=1, keepdims=True)
                for g0 in range(0, ATTN_STRIP, SUBLANES):
                    col_sum = col_sum + dst[g0:g0 + SUBLANES]
            dcq_ref[0, i] += jnp.sum(col_sum, axis=0, keepdims=True)
            dv_acc[...] += jnp.dot(pt_ref[...], do, preferred_element_type=F32)
            dk_acc[...] += jnp.dot(ds_ref[...], q, preferred_element_type=F32)
            dq_ref[0, i] += lax.dot_general(ds_ref[...], k, TN, preferred_element_type=F32)

        def step(i, carry):
            pair(i, False)
            return carry

        pair(j, True)
        lax.fori_loop(j + 1, nq, step, 0)
        dk_ref[0] = dk_acc[...]
        dv_ref[0] = dv_acc[...]
        dcs_ref[0] = -dc_acc[...]

    whole = pl.BlockSpec((1, nq, tq, dh), lambda h, j: (h, 0, 0, 0))
    tile = pl.BlockSpec((1, tq, dh), lambda h, j: (h, j, 0))
    rows = pl.BlockSpec((1, nq, 1, tq), lambda h, j: (h, 0, 0, 0))
    col = pl.BlockSpec((1, tq, 1), lambda h, j: (h, j, 0))
    return pl.pallas_call(
        body, name=name, grid=(H, nq),
        in_specs=[whole, whole, tile, tile, rows, rows, rows, col],
        out_specs=[whole, tile, tile, col, rows],
        out_shape=[jax.ShapeDtypeStruct((H, nq, tq, dh), F32), jax.ShapeDtypeStruct((H, Lp, dh), F32),
                   jax.ShapeDtypeStruct((H, Lp, dh), F32), jax.ShapeDtypeStruct((H, Lp, 1), F32),
                   jax.ShapeDtypeStruct((H, nq, 1, tq), F32)],
        scratch_shapes=[pltpu.VMEM((tq, dh), F32), pltpu.VMEM((tq, dh), F32), pltpu.VMEM((tq, 1), F32),
                        pltpu.VMEM((tq, tq), F32), pltpu.VMEM((tq, tq), F32),
                        pltpu.VMEM((tq, tq), BF16), pltpu.VMEM((tq, tq), BF16)],
        compiler_params=_cparams("parallel", "arbitrary"),
    )(qh4, doh4, kh, vh, lse4, delta4, crow4, ccol)


def _remote(src, dst, send_sems, recv_sems, k, to):
    return pltpu.make_async_remote_copy(src_ref=src, dst_ref=dst, send_sem=send_sems.at[k], recv_sem=recv_sems.at[k],
                                        device_id=to, device_id_type=MESH)


def _place():
    x, y, c = lax.axis_index("x"), lax.axis_index("y"), lax.axis_index("c")
    other_chips = [(1 - x, y), (x, 1 - y), (1 - x, 1 - y)]
    return x, y, c, other_chips


def _all_gather_weights(wb, wf, name):
    Rb, C = wb.shape
    Rf = wf.shape[0]
    hb = Rb // 2

    def body(wb_ref, wf_ref, ob_ref, of_ref, send_sems, recv_sems, local_sems):
        x, y, c, chips = _place()
        me = 2 * x + y
        sibling = (x, y, 1 - c)

        def half(chip, core):
            return ob_ref.at[chip, pl.ds(core * hb, hb), :]

        own_b = pltpu.make_async_copy(wb_ref, ob_ref.at[me], local_sems.at[0])
        own_f = pltpu.make_async_copy(wf_ref, of_ref.at[me], local_sems.at[1])
        own_b.start()
        own_f.start()
        sent = []
        for j, (cx, cy) in enumerate(chips):
            sent.append(_remote(wb_ref.at[pl.ds(c * hb, hb), :], half(me, c), send_sems, recv_sems, j, (cx, cy, c)))
            sent.append(_remote(wf_ref, of_ref.at[me], send_sems, recv_sems, 3 + j, (cx, cy, c)))
        for cp in sent:
            cp.start()
        for j, (cx, cy) in enumerate(chips):
            chip = 2 * cx + cy
            _remote(half(chip, c), half(chip, c), send_sems, recv_sems, j, sibling).wait_recv()
            fwd = _remote(half(chip, c), half(chip, c), send_sems, recv_sems, 6 + j, sibling)
            fwd.start()
            sent.append(fwd)
        for j, (cx, cy) in enumerate(chips):
            chip = 2 * cx + cy
            _remote(wf_ref, of_ref.at[chip], send_sems, recv_sems, 3 + j, sibling).wait_recv()
            _remote(half(chip, 1 - c), half(chip, 1 - c), send_sems, recv_sems, 6 + j, sibling).wait_recv()
        for cp in sent:
            cp.wait_send()
        own_b.wait()
        own_f.wait()

    any_spec = pl.BlockSpec(memory_space=pl.ANY)
    return pl.pallas_call(
        body, name=name,
        in_specs=[any_spec, any_spec], out_specs=[any_spec, any_spec],
        out_shape=[jax.ShapeDtypeStruct((N_CHIPS, Rb, C), BF16), jax.ShapeDtypeStruct((N_CHIPS, Rf, C), F32)],
        scratch_shapes=[pltpu.SemaphoreType.DMA((9,)), pltpu.SemaphoreType.DMA((9,)), pltpu.SemaphoreType.DMA((2,))],
    )(wb, wf)


def _sibling_swap(src, name):
    def body(src_ref, dst_ref, send_sems, recv_sems):
        x, y, c, _ = _place()
        cp = _remote(src_ref, dst_ref, send_sems, recv_sems, 0, (x, y, 1 - c))
        cp.start()
        cp.wait()

    any_spec = pl.BlockSpec(memory_space=pl.ANY)
    return pl.pallas_call(
        body, name=name, in_specs=[any_spec], out_specs=any_spec,
        out_shape=jax.ShapeDtypeStruct(src.shape, src.dtype),
        scratch_shapes=[pltpu.SemaphoreType.DMA((1,)), pltpu.SemaphoreType.DMA((1,))],
    )(src)


def _chip_exchange(part, rep, name):
    _, hr, C = part.shape
    rr = rep.shape[0]

    def body(part_ref, rep_ref, land_ref, reps_ref, send_sems, recv_sems, local_sem):
        x, y, c, chips = _place()
        me = 4 * x + 2 * y + c
        own = pltpu.make_async_copy(rep_ref, reps_ref.at[me], local_sem.at[0])
        own.start()
        sent = []
        for j, (cx, cy) in enumerate(chips):
            sent.append(_remote(part_ref.at[2 * cx + cy], land_ref.at[j], send_sems, recv_sems, j, (cx, cy, c)))
        for r in range(1, N_DEV):
            fx, fy, fc = (r >> 2) & 1, (r >> 1) & 1, r & 1
            to = (x ^ fx, y ^ fy, c ^ fc)
            sent.append(_remote(rep_ref, reps_ref.at[me], send_sems, recv_sems, 2 + r, to))
        for cp in sent:
            cp.start()
        for j in range(3):
            _remote(part_ref.at[0], land_ref.at[j], send_sems, recv_sems, j, (x, y, c)).wait_recv()
        for r in range(1, N_DEV):
            fx, fy, fc = (r >> 2) & 1, (r >> 1) & 1, r & 1
            frm = 4 * (x ^ fx) + 2 * (y ^ fy) + (c ^ fc)
            _remote(rep_ref, reps_ref.at[frm], send_sems, recv_sems, 2 + r, (x, y, c)).wait_recv()
        for cp in sent:
            cp.wait_send()
        own.wait()

    any_spec = pl.BlockSpec(memory_space=pl.ANY)
    return pl.pallas_call(
        body, name=name, in_specs=[any_spec, any_spec], out_specs=[any_spec, any_spec],
        out_shape=[jax.ShapeDtypeStruct((3, hr, C), F32), jax.ShapeDtypeStruct((N_DEV, rr, C), F32)],
        scratch_shapes=[pltpu.SemaphoreType.DMA((10,)), pltpu.SemaphoreType.DMA((10,)), pltpu.SemaphoreType.DMA((1,))],
    )(part, rep)


def _adamw_math(w, g, m, v):
    m = ADAM_B1 * m + (1.0 - ADAM_B1) * g
    v = ADAM_B2 * v + (1.0 - ADAM_B2) * (g * g)
    m_hat = m / (1.0 - ADAM_B1 ** ADAM_STEP)
    v_hat = v / (1.0 - ADAM_B2 ** ADAM_STEP)
    delta = -ADAM_LR * (m_hat / (jnp.sqrt(v_hat) + ADAM_EPS) + ADAM_WD * w)
    return delta, m, v


def _add_halves(a, b, name, *, tr):
    n, hr, C = a.shape

    def body(a_ref, b_ref, o_ref):
        o_ref[...] = a_ref[...] + b_ref[...]

    blk = pl.BlockSpec((1, tr, C), lambda s, i: (s, i, 0))
    return pl.pallas_call(
        body, name=name, grid=(n, hr // tr), in_specs=[blk, blk], out_specs=blk,
        out_shape=jax.ShapeDtypeStruct(a.shape, F32), compiler_params=_cparams("parallel", "parallel"),
    )(a, b)


def _sum_adamw(own, landed, w, m, v, name, *, tr):
    n = landed.shape[0]
    hr, C = own.shape

    def body(own_ref, land_ref, w_ref, m_ref, v_ref, o_ref):
        g = own_ref[...]
        for s in range(n):
            g = g + land_ref[s]
        delta, m_new, v_new = _adamw_math(w_ref[...], g, m_ref[...], v_ref[...])
        o_ref[0] = g
        o_ref[1] = delta
        o_ref[2] = m_new
        o_ref[3] = v_new

    blk = pl.BlockSpec((tr, C), lambda i: (i, 0))
    return pl.pallas_call(
        body, name=name, grid=(hr // tr,),
        in_specs=[blk, pl.BlockSpec((n, tr, C), lambda i: (0, i, 0)), blk, blk, blk],
        out_specs=pl.BlockSpec((4, tr, C), lambda i: (0, i, 0)),
        out_shape=jax.ShapeDtypeStruct((4, hr, C), F32), compiler_params=_cparams("parallel"),
    )(own, landed, w, m, v)


def _rows_of(shape):
    n = 1
    for d in shape:
        n *= d
    return -(-n // PACK_COLS)


def _pack(arrays, total_rows, dtype):
    parts = []
    for a in arrays:
        flat = a.reshape(-1).astype(dtype)
        parts.append(jnp.pad(flat, (0, _rows_of(a.shape) * PACK_COLS - flat.shape[0])))
    flat = jnp.concatenate(parts)
    flat = jnp.pad(flat, (0, total_rows * PACK_COLS - flat.shape[0]))
    return flat.reshape(total_rows, PACK_COLS)


def _unpack(buf, shapes):
    lead = buf.shape[:-2]
    out, r = [], 0
    for shp in shapes:
        n = 1
        for d in shp:
            n *= d
        rows = _rows_of(shp)
        piece = buf[..., r:r + rows, :].reshape(lead + (rows * PACK_COLS,))[..., :n]
        out.append(piece.reshape(lead + tuple(shp)))
        r += rows
    return out


def _join_shards(stacked, axis):
    return jnp.concatenate([stacked[s] for s in range(N_CHIPS)], axis=axis)


def _split_shards(full, axis):
    return jnp.stack(jnp.split(full, N_CHIPS, axis=axis))


def _local_step(h0, tgt, W, *, seq, tm):
    Lp, D = h0.shape
    H = D // HEAD_DIM
    F2 = W["ffn_w_in"].shape[-1]
    F = F2 // 2
    te = tm // 2
    nq = Lp // tm
    cap = 1408
    tD, tF, tF2 = _pick(D, cap), _pick(F, cap), _pick(F2, cap)
    t2D = _pick(2 * D, cap)
    tcn = _pick(F, cap)

    def vec(a):
        return a.reshape(1, -1)

    ln_g, ln_b = W["ln_g"], W["ln_b"]
    wf_pad = jnp.pad(W["w_f"], ((0, 0), (0, LANES - H)))
    bf_pad = jnp.pad(W["b_f"], (0, LANES - H)).reshape(1, LANES)

    def ffn_fwd(hb, l, tag):
        u = _mm(hb, W["ffn_w_in"][l], "nn", F32, f"ffn{tag}_up", tm=tm, tn=tF2, tk=tD)
        act = _conv_glu_fwd(u, W["ffn_conv_w"][l], vec(W["ffn_conv_b"][l]), f"ffn{tag}_glu", tm=te, tn=tcn)
        y = _mm(act, W["ffn_w_out"][l], "nn", F32, f"ffn{tag}_down", tm=tm, tn=tD, tk=tF)
        return u, act, y

    def ffn_bwd(dzb, hb, u, act, l, tag):
        dact = _mm(dzb, W["ffn_w_out"][l], "nt", F32, f"ffn{tag}_dact", tm=tm, tn=tF, tk=tD)
        dw_out = _mm(act, dzb, "tn", F32, f"ffn{tag}_dwout", tm=tF, tn=tD, tk=tm)
        dua, dug, dwa, dwg, dba, dbg = _conv_glu_bwd(u, dact, W["ffn_conv_w"][l], vec(W["ffn_conv_b"][l]),
                                                     f"ffn{tag}_dglu", tm=te, tn=tcn)
        du = jnp.concatenate([dua, dug], axis=1)
        dcw = jnp.concatenate([dwa, dwg], axis=1)
        dcb = jnp.concatenate([dba, dbg], axis=1)
        dh = _mm(du, W["ffn_w_in"][l], "nt", F32, f"ffn{tag}_dh", tm=tm, tn=tD, tk=tF2)
        dw_in = _mm(hb, du, "tn", F32, f"ffn{tag}_dwin", tm=tD, tn=tF2, tk=tm)
        return dh, dw_in, dw_out, dcw, dcb[0]

    diffb, mixpre, h1, h1b, xh1, rs1 = _pool_ln_fwd(h0, W["pool_w"][0], W["pool_scale"], vec(ln_g[0, 0]),
                                                    vec(ln_b[0, 0]), "pool_ln_fwd", tm=te)
    u0, act0, y0 = ffn_fwd(h1b, 0, "0")
    h2, h2b, xh2, rs2 = _ln_fwd(h1, y0, vec(ln_g[0, 1]), vec(ln_b[0, 1]), "ln01_fwd", tm=te)

    kvb = _mm(h2b, W["w_kv"], "nn", BF16, "kv_proj", tm=tm, tn=t2D, tk=tD)
    qb = _mm(h2b, W["w_q"][0], "nn", BF16, "q_proj", tm=tm, tn=tD, tk=tD, scale=HEAD_DIM ** -0.5)
    pre = _mm(h2b, wf_pad, "nn", F32, "f_proj", tm=tm, tn=LANES, tk=tD)
    c = _logf_cumsum(pre, bf_pad, "logf_cumsum", tm=tm)

    def heads(a):
        return a.reshape(Lp, H, HEAD_DIM).transpose(1, 0, 2)

    def tokens(a):
        return a.transpose(1, 0, 2).reshape(Lp, D)

    qh, kh, vh = heads(qb), heads(kvb[:, :D]), heads(kvb[:, D:])
    c_t = c[:, :H].T
    ccol = c_t.reshape(H, Lp, 1)
    crow4 = c_t.reshape(H, nq, 1, tm)
    oh, lse4 = _attn_fwd(qh, kh.reshape(H, nq, tm, HEAD_DIM), vh.reshape(H, nq, tm, HEAD_DIM),
                         ccol.reshape(H, nq, tm, 1), crow4, "attn_fwd", tq=tm)
    o_tok = tokens(oh)
    ob = o_tok.astype(BF16)
    y_attn = _mm(ob, W["w_o"][0], "nn", F32, "o_proj", tm=tm, tn=tD, tk=tD)
    h3, h3b, xh3, rs3 = _ln_fwd(h2, y_attn, vec(ln_g[1, 0]), vec(ln_b[1, 0]), "ln10_fwd", tm=te)
    u1, act1, y1 = ffn_fwd(h3b, 1, "1")
    h4, _, xh4, rs4 = _ln_fwd(h3, y1, vec(ln_g[1, 1]), vec(ln_b[1, 1]), "ln11_fwd", tm=te)
    dy, loss = _loss_head(h4, tgt, "loss_head", tm=te, row_lo=N_META, row_hi=N_META + seq)

    dz4, dz4b, dg11, db11 = _ln_bwd([dy], [1.0], xh4, rs4, vec(ln_g[1, 1]), "ln11_bwd", tm=te)
    dh3, dw_in1, dw_out1, dcw1, dcb1 = ffn_bwd(dz4b, h3b, u1, act1, 1, "1")
    dz3, dz3b, dg10, db10 = _ln_bwd([dz4, dh3], [ALPHA, 1.0], xh3, rs3, vec(ln_g[1, 0]), "ln10_bwd", tm=te)

    do_tok = _mm(dz3b, W["w_o"][0], "nt", F32, "o_proj_dx", tm=tm, tn=tD, tk=tD)
    dw_o = _mm(ob, dz3b, "tn", F32, "o_proj_dw", tm=tD, tn=tD, tk=tm)
    delta = _attn_delta(do_tok, o_tok, "attn_delta", tm=te, n_heads=H)
    doh4 = heads(do_tok.astype(BF16)).reshape(H, nq, tm, HEAD_DIM)
    dqh4, dkh, dvh, dcs, dcq = _attn_bwd(qh.reshape(H, nq, tm, HEAD_DIM), doh4, kh, vh,
                                    lse4, delta[:, :H].T.reshape(H, nq, 1, tm), crow4, ccol,
                                    "attn_bwd", tq=tm)
    dqb = tokens(dqh4.reshape(H, Lp, HEAD_DIM)).astype(BF16)
    dkvb = jnp.concatenate([tokens(dkh), tokens(dvh)], axis=1).astype(BF16)
    dc_keys = jnp.pad(dcs.reshape(H, Lp).T, ((0, 0), (0, LANES - H)))
    dc_queries = jnp.pad(dcq.reshape(H, Lp).T, ((0, 0), (0, LANES - H)))
    dpreb, dbf = _logf_bwd(dc_keys, dc_queries, pre, bf_pad, "logf_bwd", tm=tm)

    qs = HEAD_DIM ** -0.5
    dw_q = _mm(h2b, dqb, "tn", F32, "q_proj_dw", tm=tD, tn=tD, tk=tm, scale=qs)
    dw_kv = _mm(h2b, dkvb, "tn", F32, "kv_proj_dw", tm=tD, tn=t2D, tk=tm)
    dw_f = _mm(h2b, dpreb, "tn", F32, "f_proj_dw", tm=tD, tn=LANES, tk=tm)[:, :H]
    dh2 = _mm(dqb, W["w_q"][0], "nt", F32, "q_proj_dx", tm=tm, tn=tD, tk=tD, scale=qs)
    dh2 = _mm(dkvb, W["w_kv"], "nt", F32, "kv_proj_dx", tm=tm, tn=tD, tk=t2D, add=dh2)
    dh2 = _mm(dpreb, wf_pad, "nt", F32, "f_proj_dx", tm=tm, tn=tD, tk=LANES, add=dh2)
    dz2, dz2b, dg01, db01 = _ln_bwd([dz3, dh2], [ALPHA, 1.0], xh2, rs2, vec(ln_g[0, 1]), "ln01_bwd", tm=te)

    dh1, dw_in0, dw_out0, dcw0, dcb0 = ffn_bwd(dz2b, h1b, u0, act0, 0, "0")
    dz1, _, dg00, db00 = _ln_bwd([dz2, dh1], [ALPHA, 1.0], xh1, rs1, vec(ln_g[0, 0]), "ln00_bwd", tm=te)
    dh0, dmb, dscale = _pool_bwd(dz1, mixpre, W["pool_w"][0], W["pool_scale"], "pool_bwd", tm=te)
    dw_pool = _pool_dw(diffb, dmb, "pool_dw", tk=tm)

    grads = {
        "meta": dh0[:N_META],
        "pool_w": dw_pool[None],
        "pool_scale": dscale,
        "w_kv": dw_kv,
        "w_f": dw_f,
        "b_f": dbf[0, :H],
        "w_q": dw_q[None],
        "w_o": dw_o[None],
        "ffn_w_in": jnp.stack([dw_in0, dw_in1]),
        "ffn_conv_w": jnp.stack([dcw0, dcw1]),
        "ffn_conv_b": jnp.stack([dcb0, dcb1]),
        "ffn_w_out": jnp.stack([dw_out0, dw_out1]),
        "ln_g": jnp.stack([jnp.stack([dg00[0], dg01[0]]), jnp.stack([dg10[0], dg11[0]])]),
        "ln_b": jnp.stack([jnp.stack([db00[0], db01[0]]), jnp.stack([db10[0], db11[0]])]),
    }
    return loss, dh0, grads


def _row_tile(length):
    return 640 if length >= 4096 else 128


def kernel(x, meta, pool_w, pool_scale, w_kv, w_f, b_f, w_q, w_o, ffn_w_in, ffn_conv_w, ffn_conv_b, ffn_w_out, ln_g, ln_b, loss_target, m_meta, m_pool_w, m_pool_scale, m_w_kv, m_w_f, m_b_f, m_w_q, m_w_o, m_ffn_w_in, m_ffn_conv_w, m_ffn_conv_b, m_ffn_w_out, m_ln_g, m_ln_b, v_meta, v_pool_w, v_pool_scale, v_w_kv, v_w_f, v_b_f, v_w_q, v_w_o, v_ffn_w_in, v_ffn_conv_w, v_ffn_conv_b, v_ffn_w_out, v_ln_g, v_ln_b):
    weights = dict(meta=meta, pool_w=pool_w, pool_scale=pool_scale, w_kv=w_kv, w_f=w_f, b_f=b_f, w_q=w_q, w_o=w_o,
                   ffn_w_in=ffn_w_in, ffn_conv_w=ffn_conv_w, ffn_conv_b=ffn_conv_b, ffn_w_out=ffn_w_out,
                   ln_g=ln_g, ln_b=ln_b)
    mom1 = dict(meta=m_meta, pool_w=m_pool_w, pool_scale=m_pool_scale, w_kv=m_w_kv, w_f=m_w_f, b_f=m_b_f, w_q=m_w_q,
                w_o=m_w_o, ffn_w_in=m_ffn_w_in, ffn_conv_w=m_ffn_conv_w, ffn_conv_b=m_ffn_conv_b,
                ffn_w_out=m_ffn_w_out, ln_g=m_ln_g, ln_b=m_ln_b)
    mom2 = dict(meta=v_meta, pool_w=v_pool_w, pool_scale=v_pool_scale, w_kv=v_w_kv, w_f=v_w_f, b_f=v_b_f, w_q=v_w_q,
                w_o=v_w_o, ffn_w_in=v_ffn_w_in, ffn_conv_w=v_ffn_conv_w, ffn_conv_b=v_ffn_conv_b,
                ffn_w_out=v_ffn_w_out, ln_g=v_ln_g, ln_b=v_ln_b)
    _, seq, D = x.shape
    L = N_META + seq
    tm = _row_tile(L)
    Lp = _round_up(L, tm)
    c_idx = lax.axis_index("c")
    chip = 2 * lax.axis_index("x") + lax.axis_index("y")

    shard_shapes = {n: weights[n].shape for n in SHARDED}
    rows_b = _round_up(sum(_rows_of(shard_shapes[n]) for n in MATMUL_WEIGHTS), 32)
    rows_f = _round_up(sum(_rows_of(shard_shapes[n]) for n in VECTOR_WEIGHTS), SUBLANES)
    wb = _pack([weights[n] for n in MATMUL_WEIGHTS], rows_b, BF16)
    wf = _pack([weights[n] for n in VECTOR_WEIGHTS], rows_f, F32)
    gb, gf = _all_gather_weights(wb, wf, "weights_all_gather")
    full = {}
    for names, buf in ((MATMUL_WEIGHTS, gb), (VECTOR_WEIGHTS, gf)):
        for n, stacked in zip(names, _unpack(buf, [shard_shapes[n] for n in names])):
            full[n] = _join_shards(stacked, SHARD_AXIS[n])
    full["b_f"] = b_f
    full["ffn_conv_b"] = ffn_conv_b

    pad = jnp.zeros((Lp - L, D), F32)
    h0 = jnp.concatenate([full["meta"], x[0], pad], axis=0)
    tgt = jnp.concatenate([jnp.zeros((N_META, D), F32), loss_target[0], pad], axis=0)
    loss, dh0, grads = _local_step(h0, tgt, full, seq=seq, tm=tm)
    loss = lax.psum(loss[0, 0], AXES)
    grad_x = dh0[N_META:L][None]

    rows = _round_up(sum(_rows_of(shard_shapes[n]) for n in SHARDED), 2 * LANES)
    hr = rows // 2
    shapes = [shard_shapes[n] for n in SHARDED]
    per_chip = [_split_shards(grads[n], SHARD_AXIS[n]) for n in SHARDED]
    gpack = jnp.stack([_pack([p[s] for p in per_chip], rows, F32) for s in range(N_CHIPS)])
    keep = lax.dynamic_slice_in_dim(gpack, c_idx * hr, hr, axis=1)
    give = lax.dynamic_slice_in_dim(gpack, (1 - c_idx) * hr, hr, axis=1)
    got = _sibling_swap(give, "grads_to_sibling")
    part = _add_halves(keep, got, "grads_chip_sum", tr=LANES)

    rep_shapes = [weights[n].shape for n in REPLICATED]
    rows_r = _round_up(sum(_rows_of(s) for s in rep_shapes), SUBLANES)
    rep = _pack([grads[n] for n in REPLICATED], rows_r, F32)
    landed, reps = _chip_exchange(part, rep, "grads_chip_exchange")

    def my_half(d):
        return lax.dynamic_slice_in_dim(_pack([d[n] for n in SHARDED], rows, F32), c_idx * hr, hr, axis=0)

    own = lax.dynamic_index_in_dim(part, chip, axis=0, keepdims=False)
    mine = _sum_adamw(own, landed, my_half(weights), my_half(mom1), my_half(mom2), "adamw_sharded", tr=LANES)
    theirs = _sibling_swap(mine, "results_to_sibling")
    res = jnp.zeros((4, rows, PACK_COLS), F32)
    res = lax.dynamic_update_slice_in_dim(res, mine, c_idx * hr, axis=1)
    res = lax.dynamic_update_slice_in_dim(res, theirs, (1 - c_idx) * hr, axis=1)
    sharded_out = _unpack(res, shapes)

    def packr(d):
        return _pack([d[n] for n in REPLICATED], rows_r, F32)

    res_r = _sum_adamw(reps[0], reps[1:], packr(weights), packr(mom1), packr(mom2), "adamw_replicated", tr=rows_r)
    rep_out = _unpack(res_r, rep_shapes)

    out = {n: a for n, a in zip(SHARDED, sharded_out)}
    out.update({n: a for n, a in zip(REPLICATED, rep_out)})
    result = [loss, grad_x]
    for k in range(4):
        result += [out[n][k] for n in WEIGHT_ORDER]
    return tuple(result)
```

```python
import functools

import jax
import jax.numpy as jnp
from jax import lax
from jax.experimental import pallas as pl
from jax.experimental.pallas import tpu as pltpu

N_META = 16
POOL_WINDOWS = (2, 4, 8, 16)
MAX_WINDOW = max(POOL_WINDOWS)
N_GROUPS = len(POOL_WINDOWS)
HEAD_DIM = 64
DEPTH = 2
CONV_WIDTH = 3
ALPHA = (2.0 * DEPTH) ** 0.25
LN_EPS = 1e-5
NEG_INF = -1e30
ADAM_LR = 0.001
ADAM_B1 = 0.9
ADAM_B2 = 0.999
ADAM_EPS = 1e-08
ADAM_WD = 0.01
ADAM_STEP = 10

F32 = jnp.float32
BF16 = jnp.bfloat16
ATTN_STRIP = 32
ATTN_HEADS = 2
GLU_STRIP = 16
LANES = 128
SUBLANES = 8
PACK_COLS = 1024
VMEM_LIMIT = 56 * 1024 * 1024
AXES = ("x", "y", "c")
MESH = pl.DeviceIdType.MESH

NN = (((1,), (0,)), ((), ()))
NT = (((1,), (1,)), ((), ()))
TN = (((0,), (0,)), ((), ()))

SHARD_AXIS = {"meta": 1, "pool_w": 2, "pool_scale": 1, "w_kv": 1, "w_f": 0, "w_q": 1, "w_o": 1,
              "ffn_w_in": 2, "ffn_conv_w": 2, "ffn_w_out": 1, "ln_g": 2, "ln_b": 2}
SHARDED = ("meta", "pool_w", "pool_scale", "w_kv", "w_f", "w_q", "w_o", "ffn_w_in", "ffn_conv_w",
           "ffn_w_out", "ln_g", "ln_b")
REPLICATED = ("b_f", "ffn_conv_b")
MATMUL_WEIGHTS = ("pool_w", "w_kv", "w_f", "w_q", "w_o", "ffn_w_in", "ffn_w_out")
VECTOR_WEIGHTS = ("meta", "pool_scale", "ffn_conv_w", "ln_g", "ln_b")
WEIGHT_ORDER = ("meta", "pool_w", "pool_scale", "w_kv", "w_f", "b_f", "w_q", "w_o", "ffn_w_in",
                "ffn_conv_w", "ffn_conv_b", "ffn_w_out", "ln_g", "ln_b")
N_CHIPS = 4
N_DEV = 8


def _cparams(*sem):
    return pltpu.CompilerParams(dimension_semantics=sem, vmem_limit_bytes=VMEM_LIMIT)


def _round_up(n, m):
    return (n + m - 1) // m * m


def _pick(n, cap):
    if n <= cap:
        return n
    best = 0
    for t in range(LANES, cap + 1, LANES):
        if n % t == 0:
            best = t
    assert best, (n, cap)
    return best


def _mm(a, b, mode, out_dtype, name, *, tm, tn, tk, scale=None, add=None):
    if mode == "nn":
        (M, K), N = a.shape, b.shape[1]
    elif mode == "nt":
        (M, K), N = a.shape, b.shape[0]
    else:
        (K, M), N = a.shape, b.shape[1]
    assert M % tm == 0 and N % tn == 0 and K % tk == 0, (name, M, N, K, tm, tn, tk)
    nk = K // tk
    dn = {"nn": NN, "nt": NT, "tn": TN}[mode]
    has_add = add is not None

    def body(*refs):
        a_ref, b_ref = refs[0], refs[1]
        add_ref = refs[2] if has_add else None
        o_ref = refs[3] if has_add else refs[2]
        acc_ref = refs[-1] if nk > 1 else None
        k = pl.program_id(2)
        part = lax.dot_general(a_ref[...], b_ref[...], dn, preferred_element_type=F32)

        def finish(r):
            if scale is not None:
                r = r * scale
            if has_add:
                r = r + add_ref[...]
            o_ref[...] = r.astype(out_dtype)

        if nk == 1:
            finish(part)
        else:
            @pl.when(k == 0)
            def _():
                acc_ref[...] = part

            @pl.when(k > 0)
            def _():
                acc_ref[...] += part

            @pl.when(k == nk - 1)
            def _():
                finish(acc_ref[...])

    if mode == "nn":
        a_spec = pl.BlockSpec((tm, tk), lambda j, i, k: (i, k))
        b_spec = pl.BlockSpec((tk, tn), lambda j, i, k: (k, j))
    elif mode == "nt":
        a_spec = pl.BlockSpec((tm, tk), lambda j, i, k: (i, k))
        b_spec = pl.BlockSpec((tn, tk), lambda j, i, k: (j, k))
    else:
        a_spec = pl.BlockSpec((tk, tm), lambda j, i, k: (k, i))
        b_spec = pl.BlockSpec((tk, tn), lambda j, i, k: (k, j))
    o_spec = pl.BlockSpec((tm, tn), lambda j, i, k: (i, j))
    in_specs = [a_spec, b_spec] + ([o_spec] if has_add else [])
    args = (a, b) + ((add,) if has_add else ())
    return pl.pallas_call(
        body, name=name, grid=(N // tn, M // tm, nk),
        in_specs=in_specs, out_specs=o_spec,
        out_shape=jax.ShapeDtypeStruct((M, N), out_dtype),
        scratch_shapes=[pltpu.VMEM((tm, tn), F32)] if nk > 1 else [],
        compiler_params=_cparams("parallel", "parallel", "arbitrary"),
    )(*args)


def _ln_math(z, g, b):
    mu = jnp.mean(z, axis=-1, keepdims=True)
    zc = z - mu
    var = jnp.mean(zc * zc, axis=-1, keepdims=True)
    rstd = lax.rsqrt(var + LN_EPS)
    xh = zc * rstd
    return xh * g + b, xh, rstd


def _pool_ln_fwd(h0, pw, ps, g, b, name, *, tm):
    Lp, D = h0.shape
    G = D // N_GROUPS
    halo_blocks = tm // MAX_WINDOW

    def body(x_ref, halo_ref, pw_ref, ps_ref, g_ref, b_ref,
             diff_ref, mix_ref, h_ref, hb_ref, xh_ref, rs_ref, ext_ref):
        i = pl.program_id(0)
        ext_ref[0:MAX_WINDOW, :] = jnp.where(i == 0, 0.0, halo_ref[...])
        ext_ref[MAX_WINDOW:MAX_WINDOW + tm, :] = x_ref[...]
        t1 = (i * tm + 1 + lax.broadcasted_iota(jnp.int32, (tm, 1), 0)).astype(F32)
        for gi, w in enumerate(POOL_WINDOWS):
            lo, hi = gi * G, (gi + 1) * G
            xg = x_ref[:, lo:hi]
            win = xg
            for j in range(1, w):
                win = win + ext_ref[MAX_WINDOW - j:MAX_WINDOW - j + tm, lo:hi]
            d = (win / jnp.minimum(t1, float(w)) - xg).astype(BF16)
            diff_ref[:, lo:hi] = d
            mix_ref[:, lo:hi] = jnp.dot(d, pw_ref[gi], preferred_element_type=F32)
        z = ALPHA * x_ref[...] + mix_ref[...] * ps_ref[...]
        h, xh, rstd = _ln_math(z, g_ref[...], b_ref[...])
        h_ref[...] = h
        hb_ref[...] = h.astype(BF16)
        xh_ref[...] = xh
        rs_ref[...] = rstd

    row = pl.BlockSpec((tm, D), lambda i: (i, 0))
    vec = pl.BlockSpec((1, D), lambda i: (0, 0))
    return pl.pallas_call(
        body, name=name, grid=(Lp // tm,),
        in_specs=[row,
                  pl.BlockSpec((MAX_WINDOW, D), lambda i: (jnp.maximum(i * halo_blocks - 1, 0), 0)),
                  pl.BlockSpec((N_GROUPS, G, G), lambda i: (0, 0, 0)), vec, vec, vec],
        out_specs=[row, row, row, row, row, pl.BlockSpec((tm, 1), lambda i: (i, 0))],
        out_shape=[jax.ShapeDtypeStruct((Lp, D), BF16), jax.ShapeDtypeStruct((Lp, D), F32),
                   jax.ShapeDtypeStruct((Lp, D), F32), jax.ShapeDtypeStruct((Lp, D), BF16),
                   jax.ShapeDtypeStruct((Lp, D), F32), jax.ShapeDtypeStruct((Lp, 1), F32)],
        scratch_shapes=[pltpu.VMEM((tm + MAX_WINDOW, D), F32)],
        compiler_params=_cparams("parallel"),
    )(h0, h0, pw, ps, g, b)


def _pool_bwd(dz, mixpre, pw, ps, name, *, tm):
    Lp, D = dz.shape
    G = D // N_GROUPS
    halo_blocks = tm // MAX_WINDOW
    n_halo = Lp // MAX_WINDOW
    ni = Lp // tm
    R = tm + MAX_WINDOW

    def body(dz_ref, halo_ref, mix_ref, pw_ref, ps_ref, dh_ref, dmb_ref, dsc_ref, ext_ref, dp_ref):
        i = pl.program_id(0)
        ext_ref[0:tm, :] = dz_ref[...]
        ext_ref[tm:R, :] = jnp.where(i == ni - 1, 0.0, halo_ref[...])
        dmix = (ext_ref[...] * ps_ref[...]).astype(BF16)
        dmb_ref[...] = dmix[0:tm]

        @pl.when(i == 0)
        def _():
            dsc_ref[...] = jnp.zeros_like(dsc_ref)

        dsc_ref[...] += jnp.sum(dz_ref[...] * mix_ref[...], axis=0, keepdims=True)
        t1 = (i * tm + 1 + lax.broadcasted_iota(jnp.int32, (R, 1), 0)).astype(F32)
        for gi, w in enumerate(POOL_WINDOWS):
            lo, hi = gi * G, (gi + 1) * G
            dd = lax.dot_general(dmix[:, lo:hi], pw_ref[gi], NT, preferred_element_type=F32)
            dp_ref[:, lo:hi] = dd / jnp.minimum(t1, float(w))
            back = dp_ref[0:tm, lo:hi]
            for j in range(1, w):
                back = back + dp_ref[j:j + tm, lo:hi]
            dh_ref[:, lo:hi] = ALPHA * dz_ref[:, lo:hi] - dd[0:tm] + back

    row = pl.BlockSpec((tm, D), lambda i: (i, 0))
    vec = pl.BlockSpec((1, D), lambda i: (0, 0))
    return pl.pallas_call(
        body, name=name, grid=(ni,),
        in_specs=[row,
                  pl.BlockSpec((MAX_WINDOW, D), lambda i: (jnp.minimum((i + 1) * halo_blocks, n_halo - 1), 0)),
                  row, pl.BlockSpec((N_GROUPS, G, G), lambda i: (0, 0, 0)), vec],
        out_specs=[row, row, vec],
        out_shape=[jax.ShapeDtypeStruct((Lp, D), F32), jax.ShapeDtypeStruct((Lp, D), BF16),
                   jax.ShapeDtypeStruct((1, D), F32)],
        scratch_shapes=[pltpu.VMEM((R, D), F32), pltpu.VMEM((R, D), F32)],
        compiler_params=_cparams("arbitrary"),
    )(dz, dz, mixpre, pw, ps)


def _pool_dw(diffb, dmb, name, *, tk):
    Lp, D = diffb.shape
    G = D // N_GROUPS

    def body(a_ref, b_ref, o_ref):
        @pl.when(pl.program_id(1) == 0)
        def _():
            o_ref[...] = jnp.zeros_like(o_ref)

        o_ref[0] += lax.dot_general(a_ref[...], b_ref[...], TN, preferred_element_type=F32)

    blk = pl.BlockSpec((tk, G), lambda g, k: (k, g))
    return pl.pallas_call(
        body, name=name, grid=(N_GROUPS, Lp // tk),
        in_specs=[blk, blk], out_specs=pl.BlockSpec((1, G, G), lambda g, k: (g, 0, 0)),
        out_shape=jax.ShapeDtypeStruct((N_GROUPS, G, G), F32),
        compiler_params=_cparams("parallel", "arbitrary"),
    )(diffb, dmb)


def _ln_fwd(resid, y, g, b, name, *, tm):
    Lp, D = resid.shape

    def body(r_ref, y_ref, g_ref, b_ref, h_ref, hb_ref, xh_ref, rs_ref):
        h, xh, rstd = _ln_math(ALPHA * r_ref[...] + y_ref[...], g_ref[...], b_ref[...])
        h_ref[...] = h
        hb_ref[...] = h.astype(BF16)
        xh_ref[...] = xh
        rs_ref[...] = rstd

    row = pl.BlockSpec((tm, D), lambda i: (i, 0))
    vec = pl.BlockSpec((1, D), lambda i: (0, 0))
    return pl.pallas_call(
        body, name=name, grid=(Lp // tm,),
        in_specs=[row, row, vec, vec],
        out_specs=[row, row, row, pl.BlockSpec((tm, 1), lambda i: (i, 0))],
        out_shape=[jax.ShapeDtypeStruct((Lp, D), F32), jax.ShapeDtypeStruct((Lp, D), BF16),
                   jax.ShapeDtypeStruct((Lp, D), F32), jax.ShapeDtypeStruct((Lp, 1), F32)],
        compiler_params=_cparams("parallel"),
    )(resid, y, g, b)


def _ln_bwd(parts, coefs, xh, rs, g, name, *, tm):
    Lp, D = xh.shape
    n = len(parts)

    def body(*refs):
        part_refs = refs[:n]
        xh_ref, rs_ref, g_ref = refs[n:n + 3]
        dz_ref, dzb_ref, dg_ref, db_ref = refs[n + 3:]
        dy = part_refs[0][...] if coefs[0] == 1.0 else coefs[0] * part_refs[0][...]
        for c, r in zip(coefs[1:], part_refs[1:]):
            dy = dy + (r[...] if c == 1.0 else c * r[...])
        x = xh_ref[...]
        dxh = dy * g_ref[...]
        m1 = jnp.mean(dxh, axis=-1, keepdims=True)
        m2 = jnp.mean(dxh * x, axis=-1, keepdims=True)
        dz = rs_ref[...] * (dxh - m1 - x * m2)
        dz_ref[...] = dz
        dzb_ref[...] = dz.astype(BF16)

        @pl.when(pl.program_id(0) == 0)
        def _():
            dg_ref[...] = jnp.zeros_like(dg_ref)
            db_ref[...] = jnp.zeros_like(db_ref)

        dg_ref[...] += jnp.sum(dy * x, axis=0, keepdims=True)
        db_ref[...] += jnp.sum(dy, axis=0, keepdims=True)

    row = pl.BlockSpec((tm, D), lambda i: (i, 0))
    vec = pl.BlockSpec((1, D), lambda i: (0, 0))
    return pl.pallas_call(
        body, name=name, grid=(Lp // tm,),
        in_specs=[row] * n + [row, pl.BlockSpec((tm, 1), lambda i: (i, 0)), vec],
        out_specs=[row, row, vec, vec],
        out_shape=[jax.ShapeDtypeStruct((Lp, D), F32), jax.ShapeDtypeStruct((Lp, D), BF16),
                   jax.ShapeDtypeStruct((1, D), F32), jax.ShapeDtypeStruct((1, D), F32)],
        compiler_params=_cparams("arbitrary"),
    )(*parts, xh, rs, g)


def _loss_head(h, tgt, name, *, tm, row_lo, row_hi):
    Lp, D = h.shape

    def body(h_ref, t_ref, dy_ref, loss_ref):
        i = pl.program_id(0)
        r = i * tm + lax.broadcasted_iota(jnp.int32, (tm, 1), 0)
        valid = (r >= row_lo) & (r < row_hi)
        e = jnp.where(valid, h_ref[...] - t_ref[...], 0.0)
        dy_ref[...] = e * (1.0 / D)

        @pl.when(i == 0)
        def _():
            loss_ref[...] = jnp.zeros_like(loss_ref)

        loss_ref[...] += 0.5 * jnp.sum(jnp.mean(e * e, axis=-1, keepdims=True), axis=0, keepdims=True)

    row = pl.BlockSpec((tm, D), lambda i: (i, 0))
    return pl.pallas_call(
        body, name=name, grid=(Lp // tm,),
        in_specs=[row, row], out_specs=[row, pl.BlockSpec((1, 1), lambda i: (0, 0))],
        out_shape=[jax.ShapeDtypeStruct((Lp, D), F32), jax.ShapeDtypeStruct((1, 1), F32)],
        compiler_params=_cparams("arbitrary"),
    )(h, tgt)


def _shift_rows_down(cur, prev, s, sub):
    return jnp.where(sub >= s, pltpu.roll(cur, s, 0), pltpu.roll(prev, s, 0))


def _shift_rows_up(cur, nxt, s, sub):
    return jnp.where(sub < SUBLANES - s, pltpu.roll(cur, SUBLANES - s, 0), pltpu.roll(nxt, SUBLANES - s, 0))


def _conv_group(cur, prev, cw_ref, cb_ref, sub):
    taps = [_shift_rows_down(cur, prev, 2, sub), _shift_rows_down(cur, prev, 1, sub), cur]
    c = cb_ref[...] + cw_ref[0:1, :] * taps[0] + cw_ref[1:2, :] * taps[1] + cw_ref[2:3, :] * taps[2]
    return c, taps


def _conv_glu_fwd(u, cw, cb, name, *, tm, tn):
    Lp, F2 = u.shape
    F = F2 // 2
    nj = F // tn
    halo_blocks = tm // SUBLANES
    S8 = SUBLANES
    assert GLU_STRIP == 2 * S8 and tm % GLU_STRIP == 0

    def body(ua_ref, ug_ref, pa_ref, pg_ref, cwa_ref, cwg_ref, cba_ref, cbg_ref, o_ref):
        first = pl.program_id(1) == 0
        sub = lax.broadcasted_iota(jnp.int32, (S8, tn), 0)

        def strip(r, prev_a, prev_g):
            out = []
            for g0 in (0, S8):
                a_cur = ua_ref[pl.ds(r + g0, S8), :]
                g_cur = ug_ref[pl.ds(r + g0, S8), :]
                a, _ = _conv_group(a_cur, prev_a, cwa_ref, cba_ref, sub)
                gate, _ = _conv_group(g_cur, prev_g, cwg_ref, cbg_ref, sub)
                out.append(a * jax.nn.sigmoid(a) * gate)
                prev_a, prev_g = a_cur, g_cur
            o_ref[pl.ds(r, GLU_STRIP), :] = jnp.concatenate(out, axis=0).astype(BF16)

        strip(0, jnp.where(first, 0.0, pa_ref[...]), jnp.where(first, 0.0, pg_ref[...]))

        def step(k, carry):
            r = pl.multiple_of(k * GLU_STRIP, GLU_STRIP)
            before = pl.ds(pl.multiple_of(r - S8, S8), S8)
            strip(r, ua_ref[before, :], ug_ref[before, :])
            return carry

        lax.fori_loop(1, tm // GLU_STRIP, step, 0)

    def prev(off):
        return pl.BlockSpec((SUBLANES, tn), lambda j, i: (jnp.maximum(i * halo_blocks - 1, 0), j + off))

    def cols(rows, off):
        return pl.BlockSpec((rows, tn), lambda j, i: (0, j + off))

    return pl.pallas_call(
        body, name=name, grid=(nj, Lp // tm),
        in_specs=[pl.BlockSpec((tm, tn), lambda j, i: (i, j)), pl.BlockSpec((tm, tn), lambda j, i: (i, j + nj)),
                  prev(0), prev(nj), cols(CONV_WIDTH, 0), cols(CONV_WIDTH, nj), cols(1, 0), cols(1, nj)],
        out_specs=pl.BlockSpec((tm, tn), lambda j, i: (i, j)),
        out_shape=jax.ShapeDtypeStruct((Lp, F), BF16),
        compiler_params=_cparams("parallel", "parallel"),
    )(u, u, u, u, cw, cw, cb, cb)


def _conv_glu_bwd(u, dact, cw, cb, name, *, tm, tn):
    Lp, F2 = u.shape
    F = F2 // 2
    nj = F // tn
    ni = Lp // tm
    halo_blocks = tm // SUBLANES
    n_halo = Lp // SUBLANES
    S8 = SUBLANES
    n_strips = tm // GLU_STRIP
    assert GLU_STRIP == 2 * S8 and tm % GLU_STRIP == 0

    def body(ua_ref, ug_ref, pa_ref, pg_ref, na_ref, ng_ref, da_ref, dn_ref,
             cwa_ref, cwg_ref, cba_ref, cbg_ref,
             dua_ref, dug_ref, dwa_ref, dwg_ref, dba_ref, dbg_ref,
             wacc_a, wacc_g, bacc_a, bacc_g):
        i = pl.program_id(1)
        first, last = i == 0, i == ni - 1
        sub = lax.broadcasted_iota(jnp.int32, (S8, tn), 0)
        for acc in (wacc_a, wacc_g, bacc_a, bacc_g):
            acc[...] = jnp.zeros_like(acc)

        def dconv(a_cur, a_prev, g_cur, g_prev, dact_rows):
            a, taps_a = _conv_group(a_cur, a_prev, cwa_ref, cba_ref, sub)
            gate, taps_g = _conv_group(g_cur, g_prev, cwg_ref, cbg_ref, sub)
            sg = jax.nn.sigmoid(a)
            dca = dact_rows * gate * (sg * (1.0 + a * (1.0 - sg)))
            dcg = dact_rows * (a * sg)
            return dca, dcg, taps_a, taps_g

        def du_group(dc, dc_after, cw_ref):
            return (cw_ref[2:3, :] * dc + cw_ref[1:2, :] * _shift_rows_up(dc, dc_after, 1, sub)
                    + cw_ref[0:1, :] * _shift_rows_up(dc, dc_after, 2, sub))

        def strip(r, a_prev, g_prev, dca_after, dcg_after):
            a0, a1 = ua_ref[pl.ds(r, S8), :], ua_ref[pl.ds(r + S8, S8), :]
            g0, g1 = ug_ref[pl.ds(r, S8), :], ug_ref[pl.ds(r + S8, S8), :]
            dca1, dcg1, ta1, tg1 = dconv(a1, a0, g1, g0, da_ref[pl.ds(r + S8, S8), :])
            dca0, dcg0, ta0, tg0 = dconv(a0, a_prev, g0, g_prev, da_ref[pl.ds(r, S8), :])
            dua_ref[pl.ds(r, GLU_STRIP), :] = jnp.concatenate(
                [du_group(dca0, dca1, cwa_ref), du_group(dca1, dca_after, cwa_ref)], axis=0).astype(BF16)
            dug_ref[pl.ds(r, GLU_STRIP), :] = jnp.concatenate(
                [du_group(dcg0, dcg1, cwg_ref), du_group(dcg1, dcg_after, cwg_ref)], axis=0).astype(BF16)
            for k in range(CONV_WIDTH):
                wacc_a[k] += dca0 * ta0[k] + dca1 * ta1[k]
                wacc_g[k] += dcg0 * tg0[k] + dcg1 * tg1[k]
            bacc_a[...] += dca0 + dca1
            bacc_g[...] += dcg0 + dcg1
            return dca0, dcg0

        tail = pl.ds(tm - S8, S8)
        dca_after, dcg_after, _, _ = dconv(na_ref[...], ua_ref[tail, :], ng_ref[...], ug_ref[tail, :],
                                           jnp.where(last, 0.0, dn_ref[...]))

        def step(t, carry):
            r = pl.multiple_of((n_strips - 1 - t) * GLU_STRIP, GLU_STRIP)
            before = pl.ds(pl.multiple_of(r - S8, S8), S8)
            return strip(r, ua_ref[before, :], ug_ref[before, :], *carry)

        dca_after, dcg_after = lax.fori_loop(0, n_strips - 1, step, (dca_after, dcg_after))
        strip(0, jnp.where(first, 0.0, pa_ref[...]), jnp.where(first, 0.0, pg_ref[...]), dca_after, dcg_after)

        @pl.when(first)
        def _():
            for r in (dwa_ref, dwg_ref, dba_ref, dbg_ref):
                r[...] = jnp.zeros_like(r)

        for wacc, bacc, dw_ref, db_ref in ((wacc_a, bacc_a, dwa_ref, dba_ref), (wacc_g, bacc_g, dwg_ref, dbg_ref)):
            db_ref[...] += jnp.sum(bacc[...], axis=0, keepdims=True)
            for k in range(CONV_WIDTH):
                dw_ref[k:k + 1, :] += jnp.sum(wacc[k], axis=0, keepdims=True)

    def tile(off):
        return pl.BlockSpec((tm, tn), lambda j, i: (i, j + off))

    def prev(off):
        return pl.BlockSpec((S8, tn), lambda j, i: (jnp.maximum(i * halo_blocks - 1, 0), j + off))

    def nxt(off):
        return pl.BlockSpec((S8, tn), lambda j, i: (jnp.minimum((i + 1) * halo_blocks, n_halo - 1), j + off))

    def cols(rows, off):
        return pl.BlockSpec((rows, tn), lambda j, i: (0, j + off))

    return pl.pallas_call(
        body, name=name, grid=(nj, ni),
        in_specs=[tile(0), tile(nj), prev(0), prev(nj), nxt(0), nxt(nj), tile(0), nxt(0),
                  cols(CONV_WIDTH, 0), cols(CONV_WIDTH, nj), cols(1, 0), cols(1, nj)],
        out_specs=[tile(0), tile(0), cols(CONV_WIDTH, 0), cols(CONV_WIDTH, 0), cols(1, 0), cols(1, 0)],
        out_shape=[jax.ShapeDtypeStruct((Lp, F), BF16), jax.ShapeDtypeStruct((Lp, F), BF16),
                   jax.ShapeDtypeStruct((CONV_WIDTH, F), F32), jax.ShapeDtypeStruct((CONV_WIDTH, F), F32),
                   jax.ShapeDtypeStruct((1, F), F32), jax.ShapeDtypeStruct((1, F), F32)],
        scratch_shapes=[pltpu.VMEM((CONV_WIDTH, S8, tn), F32), pltpu.VMEM((CONV_WIDTH, S8, tn), F32),
                        pltpu.VMEM((S8, tn), F32), pltpu.VMEM((S8, tn), F32)],
        compiler_params=_cparams("parallel", "arbitrary"),
    )(u, u, u, u, u, u, dact, dact, cw, cw, cb, cb)


def _logf_cumsum(pre, bf, name, *, tm):
    Lp, W = pre.shape

    def body(p_ref, b_ref, c_ref, carry_ref):
        i = pl.program_id(0)

        @pl.when(i == 0)
        def _():
            carry_ref[...] = jnp.zeros_like(carry_ref)

        x = p_ref[...] + b_ref[...]
        lf = jnp.minimum(x, 0.0) - jnp.log(1.0 + jnp.exp(-jnp.abs(x)))
        tri = (lax.broadcasted_iota(jnp.int32, (tm, tm), 0) >= lax.broadcasted_iota(jnp.int32, (tm, tm), 1)).astype(F32)
        c = jnp.dot(tri, lf, precision=lax.Precision.HIGHEST, preferred_element_type=F32) + carry_ref[...]
        c_ref[...] = c
        carry_ref[...] = c[tm - 1:tm, :]

    row = pl.BlockSpec((tm, W), lambda i: (i, 0))
    return pl.pallas_call(
        body, name=name, grid=(Lp // tm,),
        in_specs=[row, pl.BlockSpec((1, W), lambda i: (0, 0))], out_specs=row,
        out_shape=jax.ShapeDtypeStruct((Lp, W), F32),
        scratch_shapes=[pltpu.VMEM((1, W), F32)],
        compiler_params=_cparams("arbitrary"),
    )(pre, bf)


def _logf_bwd(dc_a, dc_b, pre, bf, name, *, tm):
    Lp, W = pre.shape
    ni = Lp // tm

    def body(dca_ref, dcb_ref, p_ref, b_ref, dpb_ref, db_ref, carry_ref):
        i = pl.program_id(0)

        @pl.when(i == 0)
        def _():
            carry_ref[...] = jnp.zeros_like(carry_ref)
            db_ref[...] = jnp.zeros_like(db_ref)

        triu = (lax.broadcasted_iota(jnp.int32, (tm, tm), 0) <= lax.broadcasted_iota(jnp.int32, (tm, tm), 1)).astype(F32)
        dl = jnp.dot(triu, dca_ref[...] + dcb_ref[...], precision=lax.Precision.HIGHEST,
                     preferred_element_type=F32) + carry_ref[...]
        carry_ref[...] = dl[0:1, :]
        dp = dl * jax.nn.sigmoid(-(p_ref[...] + b_ref[...]))
        dpb_ref[...] = dp.astype(BF16)
        db_ref[...] += jnp.sum(dp, axis=0, keepdims=True)

    rev = pl.BlockSpec((tm, W), lambda i: (ni - 1 - i, 0))
    vec = pl.BlockSpec((1, W), lambda i: (0, 0))
    return pl.pallas_call(
        body, name=name, grid=(ni,),
        in_specs=[rev, rev, rev, vec], out_specs=[rev, vec],
        out_shape=[jax.ShapeDtypeStruct((Lp, W), BF16), jax.ShapeDtypeStruct((1, W), F32)],
        scratch_shapes=[pltpu.VMEM((1, W), F32)],
        compiler_params=_cparams("arbitrary"),
    )(dc_a, dc_b, pre, bf)


def _attn_fwd(qh, kh4, vh4, crow4, name, *, tq):
    H, Lp, dh = qh.shape
    nq = Lp // tq
    S8 = SUBLANES
    HB = ATTN_HEADS
    n_scratch = 6
    lane_tiles = tq // LANES

    def to_column(row8):
        return jnp.transpose(jnp.concatenate([row8] * (LANES // S8), axis=0))

    def body(q_ref, k_ref, v_ref, c_ref, o_ref, lse_ref, *scratch):
        i = pl.program_id(1)
        heads = [scratch[n_scratch * hb:n_scratch * (hb + 1)] for hb in range(HB)]

        @pl.when(i == 0)
        def _():
            for hb, refs in enumerate(heads):
                for j in range(nq):
                    refs[5][j] = to_column(jnp.concatenate([c_ref[hb, j]] * S8, axis=0))

        for m_ref, l_ref, acc_ref, _, _, _ in heads:
            m_ref[...] = jnp.full_like(m_ref, NEG_INF)
            l_ref[...] = jnp.zeros_like(l_ref)
            acc_ref[...] = jnp.zeros_like(acc_ref)

        def chunk(j, masked):
            for hb, (_, _, _, st_ref, _, _) in enumerate(heads):
                st_ref[...] = lax.dot_general(k_ref[hb, j], q_ref[hb], NT, preferred_element_type=F32)
            for hb, (m_ref, l_ref, acc_ref, st_ref, pt_ref, cs_ref) in enumerate(heads):
                ct = c_ref[hb, i]
                mx = jnp.full((S8, tq), NEG_INF, F32)
                for r0 in range(0, tq, ATTN_STRIP):
                    rows = pl.ds(r0, ATTN_STRIP)
                    cs = jnp.concatenate([cs_ref[j, rows, :]] * lane_tiles, axis=1)
                    st = st_ref[rows, :] + (ct - cs)
                    if masked:
                        keep = (lax.broadcasted_iota(jnp.int32, (ATTN_STRIP, tq), 1)
                                >= r0 + lax.broadcasted_iota(jnp.int32, (ATTN_STRIP, tq), 0))
                        st = jnp.where(keep, st, NEG_INF)
                    st_ref[rows, :] = st
                    for g0 in range(0, ATTN_STRIP, S8):
                        mx = jnp.maximum(mx, st[g0:g0 + S8])
                m_prev = m_ref[...]
                m_new = jnp.maximum(m_prev, jnp.max(mx, axis=0, keepdims=True))
                alpha = jnp.exp(m_prev - m_new)
                m_ref[...] = m_new
                ls = jnp.zeros((S8, tq), F32)
                for r0 in range(0, tq, ATTN_STRIP):
                    pieces = [jnp.exp(st_ref[pl.ds(r0 + g0, S8), :] - m_new) for g0 in range(0, ATTN_STRIP, S8)]
                    for piece in pieces:
                        ls = ls + piece
                    pt_ref[pl.ds(r0, ATTN_STRIP), :] = jnp.concatenate(pieces, axis=0).astype(BF16)
                l_ref[...] = alpha * l_ref[...] + ls
                pv = lax.dot_general(pt_ref[...], v_ref[hb, j], TN, preferred_element_type=F32)
                acc_ref[...] = to_column(alpha)[:, :dh] * acc_ref[...] + pv

        def step(j, carry):
            chunk(j, False)
            return carry

        lax.fori_loop(0, i, step, 0)
        chunk(i, True)
        for hb, (m_ref, l_ref, acc_ref, _, _, _) in enumerate(heads):
            l_row = jnp.sum(l_ref[...], axis=0, keepdims=True)
            l8 = jnp.concatenate([l_row] * S8, axis=0)
            o_ref[hb] = acc_ref[...] / to_column(l8)[:, :dh]
            lse_ref[hb, 0] = m_ref[0:1, :] + jnp.log(l_row)

    per_head = [pltpu.VMEM((S8, tq), F32), pltpu.VMEM((S8, tq), F32), pltpu.VMEM((tq, dh), F32),
                pltpu.VMEM((tq, tq), F32), pltpu.VMEM((tq, tq), BF16), pltpu.VMEM((nq, tq, LANES), F32)]
    assert len(per_head) == n_scratch and H % HB == 0
    return pl.pallas_call(
        body, name=name, grid=(H // HB, nq),
        in_specs=[pl.BlockSpec((HB, tq, dh), lambda h, i: (h, i, 0)),
                  pl.BlockSpec((HB, nq, tq, dh), lambda h, i: (h, 0, 0, 0)),
                  pl.BlockSpec((HB, nq, tq, dh), lambda h, i: (h, 0, 0, 0)),
                  pl.BlockSpec((HB, nq, 1, tq), lambda h, i: (h, 0, 0, 0))],
        out_specs=[pl.BlockSpec((HB, tq, dh), lambda h, i: (h, i, 0)),
                   pl.BlockSpec((HB, 1, 1, tq), lambda h, i: (h, i, 0, 0))],
        out_shape=[jax.ShapeDtypeStruct((H, Lp, dh), F32), jax.ShapeDtypeStruct((H, nq, 1, tq), F32)],
        scratch_shapes=per_head * HB,
        compiler_params=_cparams("parallel", "arbitrary"),
    )(qh, kh4, vh4, crow4)


def _attn_delta(do, o, name, *, tm, n_heads):
    Lp, D = do.shape

    def body(do_ref, o_ref, d_ref):
        sel = (lax.broadcasted_iota(jnp.int32, (D, LANES), 0) // HEAD_DIM
               == lax.broadcasted_iota(jnp.int32, (D, LANES), 1)).astype(F32)
        do = do_ref[...].astype(BF16).astype(F32)
        d_ref[...] = jnp.dot(do * o_ref[...], sel, precision=lax.Precision.HIGHEST,
                             preferred_element_type=F32)

    row = pl.BlockSpec((tm, D), lambda i: (i, 0))
    return pl.pallas_call(
        body, name=name, grid=(Lp // tm,),
        in_specs=[row, row], out_specs=pl.BlockSpec((tm, LANES), lambda i: (i, 0)),
        out_shape=jax.ShapeDtypeStruct((Lp, LANES), F32),
        compiler_params=_cparams("parallel"),
    )(do, o)


def _attn_bwd(qh4, doh4, kh, vh, lse4, delta4, crow4, name, *, tq):
    H, nq, _, dh = qh4.shape
    Lp = nq * tq

    lane_tiles = tq // LANES

    def body(q_ref, do_ref, k_ref, v_ref, lse_ref, dl_ref, ct_ref,
             dq_ref, dk_ref, dv_ref, dcs_ref, dcq_ref, dk_acc, dv_acc, dc_acc, st_ref, dp_ref, pt_ref, ds_ref, cs_ref):
        j = pl.program_id(1)
        cs_ref[...] = jnp.transpose(jnp.broadcast_to(ct_ref[0, j], (LANES, tq)))

        @pl.when(j == 0)
        def _():
            dq_ref[...] = jnp.zeros_like(dq_ref)
            dcq_ref[...] = jnp.zeros_like(dcq_ref)

        k = k_ref[0]
        v = v_ref[0]
        dk_acc[...] = jnp.zeros_like(dk_acc)
        dv_acc[...] = jnp.zeros_like(dv_acc)
        dc_acc[...] = jnp.zeros_like(dc_acc)

        def pair(i, masked):
            q = q_ref[0, i]
            do = do_ref[0, i]
            st_ref[...] = lax.dot_general(k, q, NT, preferred_element_type=F32)
            dp_ref[...] = lax.dot_general(v, do, NT, preferred_element_type=F32)
            bias_q = ct_ref[0, i] - lse_ref[0, i]
            delta = dl_ref[0, i]
            col_sum = jnp.zeros((SUBLANES, tq), F32)
            for r0 in range(0, tq, ATTN_STRIP):
                rows = pl.ds(r0, ATTN_STRIP)
                st = st_ref[rows, :] + (bias_q - jnp.concatenate([cs_ref[rows, :]] * lane_tiles, axis=1))
                if masked:
                    keep = (lax.broadcasted_iota(jnp.int32, (ATTN_STRIP, tq), 1)
                            >= r0 + lax.broadcasted_iota(jnp.int32, (ATTN_STRIP, tq), 0))
                    st = jnp.where(keep, st, NEG_INF)
                pt = jnp.exp(st)
                dst = pt * (dp_ref[rows, :] - delta)
                pt_ref[rows, :] = pt.astype(BF16)
                ds_ref[rows, :] = dst.astype(BF16)
                dc_acc[rows, :] += jnp.sum(dst, axis=1, keepdims=True)
                for g0 in range(0, ATTN_STRIP, SUBLANES):
                    col_sum = col_sum + dst[g0:g0 + SUBLANES]
            dcq_ref[0, i] += jnp.sum(col_sum, axis=0, keepdims=True)
            dv_acc[...] += jnp.dot(pt_ref[...], do, preferred_element_type=F32)
            dk_acc[...] += jnp.dot(ds_ref[...], q, preferred_element_type=F32)
            dq_ref[0, i] += lax.dot_general(ds_ref[...], k, TN, preferred_element_type=F32)

        def step(i, carry):
            pair(i, False)
            return carry

        pair(j, True)
        lax.fori_loop(j + 1, nq, step, 0)
        dk_ref[0] = dk_acc[...]
        dv_ref[0] = dv_acc[...]
        dcs_ref[0] = -dc_acc[...]

    whole = pl.BlockSpec((1, nq, tq, dh), lambda h, j: (h, 0, 0, 0))
    tile = pl.BlockSpec((1, tq, dh), lambda h, j: (h, j, 0))
    rows = pl.BlockSpec((1, nq, 1, tq), lambda h, j: (h, 0, 0, 0))
    col = pl.BlockSpec((1, tq, 1), lambda h, j: (h, j, 0))
    return pl.pallas_call(
        body, name=name, grid=(H, nq),
        in_specs=[whole, whole, tile, tile, rows, rows, rows],
        out_specs=[whole, tile, tile, col, rows],
        out_shape=[jax.ShapeDtypeStruct((H, nq, tq, dh), F32), jax.ShapeDtypeStruct((H, Lp, dh), F32),
                   jax.ShapeDtypeStruct((H, Lp, dh), F32), jax.ShapeDtypeStruct((H, Lp, 1), F32),
                   jax.ShapeDtypeStruct((H, nq, 1, tq), F32)],
        scratch_shapes=[pltpu.VMEM((tq, dh), F32), pltpu.VMEM((tq, dh), F32), pltpu.VMEM((tq, 1), F32),
                        pltpu.VMEM((tq, tq), F32), pltpu.VMEM((tq, tq), F32),
                        pltpu.VMEM((tq, tq), BF16), pltpu.VMEM((tq, tq), BF16), pltpu.VMEM((tq, LANES), F32)],
        compiler_params=_cparams("parallel", "arbitrary"),
    )(qh4, doh4, kh, vh, lse4, delta4, crow4)


def _remote(src, dst, send_sems, recv_sems, k, to):
    return pltpu.make_async_remote_copy(src_ref=src, dst_ref=dst, send_sem=send_sems.at[k], recv_sem=recv_sems.at[k],
                                        device_id=to, device_id_type=MESH)


def _place():
    x, y, c = lax.axis_index("x"), lax.axis_index("y"), lax.axis_index("c")
    other_chips = [(1 - x, y), (x, 1 - y), (1 - x, 1 - y)]
    return x, y, c, other_chips


def _all_gather_weights(wb, wf, name):
    Rb, C = wb.shape
    Rf = wf.shape[0]
    hb = Rb // 2

    def body(wb_ref, wf_ref, ob_ref, of_ref, send_sems, recv_sems, local_sems):
        x, y, c, chips = _place()
        me = 2 * x + y
        sibling = (x, y, 1 - c)

        def half(chip, core):
            return ob_ref.at[chip, pl.ds(core * hb, hb), :]

        own_b = pltpu.make_async_copy(wb_ref, ob_ref.at[me], local_sems.at[0])
        own_f = pltpu.make_async_copy(wf_ref, of_ref.at[me], local_sems.at[1])
        own_b.start()
        own_f.start()
        sent = []
        for j, (cx, cy) in enumerate(chips):
            sent.append(_remote(wb_ref.at[pl.ds(c * hb, hb), :], half(me, c), send_sems, recv_sems, j, (cx, cy, c)))
            sent.append(_remote(wf_ref, of_ref.at[me], send_sems, recv_sems, 3 + j, (cx, cy, c)))
        for cp in sent:
            cp.start()
        for j, (cx, cy) in enumerate(chips):
            chip = 2 * cx + cy
            _remote(half(chip, c), half(chip, c), send_sems, recv_sems, j, sibling).wait_recv()
            fwd = _remote(half(chip, c), half(chip, c), send_sems, recv_sems, 6 + j, sibling)
            fwd.start()
            sent.append(fwd)
        for j, (cx, cy) in enumerate(chips):
            chip = 2 * cx + cy
            _remote(wf_ref, of_ref.at[chip], send_sems, recv_sems, 3 + j, sibling).wait_recv()
            _remote(half(chip, 1 - c), half(chip, 1 - c), send_sems, recv_sems, 6 + j, sibling).wait_recv()
        for cp in sent:
            cp.wait_send()
        own_b.wait()
        own_f.wait()

    any_spec = pl.BlockSpec(memory_space=pl.ANY)
    return pl.pallas_call(
        body, name=name,
        in_specs=[any_spec, any_spec], out_specs=[any_spec, any_spec],
        out_shape=[jax.ShapeDtypeStruct((N_CHIPS, Rb, C), BF16), jax.ShapeDtypeStruct((N_CHIPS, Rf, C), F32)],
        scratch_shapes=[pltpu.SemaphoreType.DMA((9,)), pltpu.SemaphoreType.DMA((9,)), pltpu.SemaphoreType.DMA((2,))],
    )(wb, wf)


def _send_half_to_sibling(buf, name):
    n, rows, C = buf.shape
    hr = rows // 2

    def body(src_ref, dst_ref, send_sems, recv_sems):
        x, y, c, _ = _place()
        cp = _remote(src_ref.at[pl.ds(0, n), pl.ds((1 - c) * hr, hr), :], dst_ref, send_sems, recv_sems, 0, (x, y, 1 - c))
        cp.start()
        cp.wait()

    any_spec = pl.BlockSpec(memory_space=pl.ANY)
    return pl.pallas_call(
        body, name=name, in_specs=[any_spec], out_specs=any_spec,
        out_shape=jax.ShapeDtypeStruct((n, hr, C), buf.dtype),
        scratch_shapes=[pltpu.SemaphoreType.DMA((1,)), pltpu.SemaphoreType.DMA((1,))],
    )(buf)


def _join_with_sibling(mine, name):
    n, hr, C = mine.shape

    def body(src_ref, out_ref, send_sems, recv_sems, local_sem):
        x, y, c, _ = _place()

        def half(core):
            return out_ref.at[pl.ds(0, n), pl.ds(core * hr, hr), :]

        own = pltpu.make_async_copy(src_ref, half(c), local_sem.at[0])
        own.start()
        cp = _remote(src_ref, half(c), send_sems, recv_sems, 0, (x, y, 1 - c))
        cp.start()
        _remote(src_ref, half(1 - c), send_sems, recv_sems, 0, (x, y, 1 - c)).wait_recv()
        cp.wait_send()
        own.wait()

    any_spec = pl.BlockSpec(memory_space=pl.ANY)
    return pl.pallas_call(
        body, name=name, in_specs=[any_spec], out_specs=any_spec,
        out_shape=jax.ShapeDtypeStruct((n, 2 * hr, C), mine.dtype),
        scratch_shapes=[pltpu.SemaphoreType.DMA((1,)), pltpu.SemaphoreType.DMA((1,)), pltpu.SemaphoreType.DMA((1,))],
    )(mine)


def _chip_exchange(part, rep, name):
    _, hr, C = part.shape
    rr = rep.shape[0]

    def body(part_ref, rep_ref, land_ref, reps_ref, send_sems, recv_sems, local_sem):
        x, y, c, chips = _place()
        me = 4 * x + 2 * y + c
        own = pltpu.make_async_copy(rep_ref, reps_ref.at[me], local_sem.at[0])
        own.start()
        sent = []
        for j, (cx, cy) in enumerate(chips):
            sent.append(_remote(part_ref.at[2 * cx + cy], land_ref.at[j], send_sems, recv_sems, j, (cx, cy, c)))
        for r in range(1, N_DEV):
            fx, fy, fc = (r >> 2) & 1, (r >> 1) & 1, r & 1
            to = (x ^ fx, y ^ fy, c ^ fc)
            sent.append(_remote(rep_ref, reps_ref.at[me], send_sems, recv_sems, 2 + r, to))
        for cp in sent:
            cp.start()
        for j in range(3):
            _remote(part_ref.at[0], land_ref.at[j], send_sems, recv_sems, j, (x, y, c)).wait_recv()
        for r in range(1, N_DEV):
            fx, fy, fc = (r >> 2) & 1, (r >> 1) & 1, r & 1
            frm = 4 * (x ^ fx) + 2 * (y ^ fy) + (c ^ fc)
            _remote(rep_ref, reps_ref.at[frm], send_sems, recv_sems, 2 + r, (x, y, c)).wait_recv()
        for cp in sent:
            cp.wait_send()
        own.wait()

    any_spec = pl.BlockSpec(memory_space=pl.ANY)
    return pl.pallas_call(
        body, name=name, in_specs=[any_spec, any_spec], out_specs=[any_spec, any_spec],
        out_shape=[jax.ShapeDtypeStruct((3, hr, C), F32), jax.ShapeDtypeStruct((N_DEV, rr, C), F32)],
        scratch_shapes=[pltpu.SemaphoreType.DMA((10,)), pltpu.SemaphoreType.DMA((10,)), pltpu.SemaphoreType.DMA((1,))],
    )(part, rep)


def _adamw_math(w, g, m, v):
    m = ADAM_B1 * m + (1.0 - ADAM_B1) * g
    v = ADAM_B2 * v + (1.0 - ADAM_B2) * (g * g)
    m_hat = m / (1.0 - ADAM_B1 ** ADAM_STEP)
    v_hat = v / (1.0 - ADAM_B2 ** ADAM_STEP)
    delta = -ADAM_LR * (m_hat / (jnp.sqrt(v_hat) + ADAM_EPS) + ADAM_WD * w)
    return delta, m, v


def _add_halves(buf, got, core, name, *, tr):
    n, hr, C = got.shape
    nb = hr // tr

    def body(core_ref, a_ref, b_ref, o_ref):
        o_ref[...] = a_ref[...] + b_ref[...]

    blk = pl.BlockSpec((1, tr, C), lambda s, i, core_ref: (s, i, 0))
    grid_spec = pltpu.PrefetchScalarGridSpec(
        num_scalar_prefetch=1, grid=(n, nb),
        in_specs=[pl.BlockSpec((1, tr, C), lambda s, i, core_ref: (s, core_ref[0] * nb + i, 0)), blk], out_specs=blk)
    return pl.pallas_call(
        body, name=name, grid_spec=grid_spec, out_shape=jax.ShapeDtypeStruct((n, hr, C), F32),
        compiler_params=_cparams("parallel", "parallel"),
    )(core, buf, got)


def _sum_adamw_half(part, landed, w, m, v, place, name, *, tr):
    n = landed.shape[0]
    _, hr, C = part.shape
    nb = hr // tr

    def body(place_ref, own_ref, land_ref, w_ref, m_ref, v_ref, o_ref):
        g = own_ref[0]
        for s in range(n):
            g = g + land_ref[s]
        delta, m_new, v_new = _adamw_math(w_ref[...], g, m_ref[...], v_ref[...])
        o_ref[0] = g
        o_ref[1] = delta
        o_ref[2] = m_new
        o_ref[3] = v_new

    half = pl.BlockSpec((tr, C), lambda i, place_ref: (place_ref[0] * nb + i, 0))
    grid_spec = pltpu.PrefetchScalarGridSpec(
        num_scalar_prefetch=1, grid=(nb,),
        in_specs=[pl.BlockSpec((1, tr, C), lambda i, place_ref: (place_ref[1], i, 0)),
                  pl.BlockSpec((n, tr, C), lambda i, place_ref: (0, i, 0)), half, half, half],
        out_specs=pl.BlockSpec((4, tr, C), lambda i, place_ref: (0, i, 0)))
    return pl.pallas_call(
        body, name=name, grid_spec=grid_spec, out_shape=jax.ShapeDtypeStruct((4, hr, C), F32),
        compiler_params=_cparams("parallel"),
    )(place, part, landed, w, m, v)


def _sum_adamw(own, landed, w, m, v, name, *, tr):
    n = landed.shape[0]
    hr, C = own.shape

    def body(own_ref, land_ref, w_ref, m_ref, v_ref, o_ref):
        g = own_ref[...]
        for s in range(n):
            g = g + land_ref[s]
        delta, m_new, v_new = _adamw_math(w_ref[...], g, m_ref[...], v_ref[...])
        o_ref[0] = g
        o_ref[1] = delta
        o_ref[2] = m_new
        o_ref[3] = v_new

    blk = pl.BlockSpec((tr, C), lambda i: (i, 0))
    return pl.pallas_call(
        body, name=name, grid=(hr // tr,),
        in_specs=[blk, pl.BlockSpec((n, tr, C), lambda i: (0, i, 0)), blk, blk, blk],
        out_specs=pl.BlockSpec((4, tr, C), lambda i: (0, i, 0)),
        out_shape=jax.ShapeDtypeStruct((4, hr, C), F32), compiler_params=_cparams("parallel"),
    )(own, landed, w, m, v)


def _rows_of(shape):
    n = 1
    for d in shape:
        n *= d
    return -(-n // PACK_COLS)


def _pack(arrays, total_rows, dtype):
    parts, used = [], 0
    for a in arrays:
        flat = a.reshape(-1).astype(dtype)
        fill = _rows_of(a.shape) * PACK_COLS - flat.shape[0]
        parts += [flat] + ([jnp.zeros((fill,), dtype)] if fill else [])
        used += _rows_of(a.shape)
    if total_rows > used:
        parts.append(jnp.zeros(((total_rows - used) * PACK_COLS,), dtype))
    return jnp.concatenate(parts).reshape(total_rows, PACK_COLS)


def _unpack(buf, shapes):
    lead = buf.shape[:-2]
    out, r = [], 0
    for shp in shapes:
        n = 1
        for d in shp:
            n *= d
        rows = _rows_of(shp)
        piece = buf[..., r:r + rows, :].reshape(lead + (rows * PACK_COLS,))[..., :n]
        out.append(piece.reshape(lead + tuple(shp)))
        r += rows
    return out


def _join_shards(stacked, axis):
    return jnp.concatenate([stacked[s] for s in range(N_CHIPS)], axis=axis)


def _shard_of(full, axis, chip):
    width = full.shape[axis] // N_CHIPS
    return lax.slice_in_dim(full, chip * width, (chip + 1) * width, axis=axis)


def _local_step(h0, tgt, W, *, seq, tm):
    Lp, D = h0.shape
    H = D // HEAD_DIM
    F2 = W["ffn_w_in"].shape[-1]
    F = F2 // 2
    te = tm // 2
    nq = Lp // tm
    cap = 1408
    tD, tF, tF2 = _pick(D, cap), _pick(F, cap), _pick(F2, cap)
    t2D = _pick(2 * D, cap)
    tcn = _pick(F, cap)

    def vec(a):
        return a.reshape(1, -1)

    ln_g, ln_b = W["ln_g"], W["ln_b"]
    wf_pad = jnp.pad(W["w_f"], ((0, 0), (0, LANES - H)))
    bf_pad = jnp.pad(W["b_f"], (0, LANES - H)).reshape(1, LANES)

    def ffn_fwd(hb, l, tag):
        u = _mm(hb, W["ffn_w_in"][l], "nn", F32, f"ffn{tag}_up", tm=tm, tn=tF2, tk=tD)
        act = _conv_glu_fwd(u, W["ffn_conv_w"][l], vec(W["ffn_conv_b"][l]), f"ffn{tag}_glu", tm=te, tn=tcn)
        y = _mm(act, W["ffn_w_out"][l], "nn", F32, f"ffn{tag}_down", tm=tm, tn=tD, tk=tF)
        return u, act, y

    def ffn_bwd(dzb, hb, u, act, l, tag):
        dact = _mm(dzb, W["ffn_w_out"][l], "nt", F32, f"ffn{tag}_dact", tm=tm, tn=tF, tk=tD)
        dw_out = _mm(act, dzb, "tn", F32, f"ffn{tag}_dwout", tm=tF, tn=tD, tk=tm)
        dua, dug, dwa, dwg, dba, dbg = _conv_glu_bwd(u, dact, W["ffn_conv_w"][l], vec(W["ffn_conv_b"][l]),
                                                     f"ffn{tag}_dglu", tm=te, tn=tcn)
        du = jnp.concatenate([dua, dug], axis=1)
        dcw = jnp.concatenate([dwa, dwg], axis=1)
        dcb = jnp.concatenate([dba, dbg], axis=1)
        dh = _mm(du, W["ffn_w_in"][l], "nt", F32, f"ffn{tag}_dh", tm=tm, tn=tD, tk=tF2)
        dw_in = _mm(hb, du, "tn", F32, f"ffn{tag}_dwin", tm=tD, tn=tF2, tk=tm)
        return dh, dw_in, dw_out, dcw, dcb[0]

    diffb, mixpre, h1, h1b, xh1, rs1 = _pool_ln_fwd(h0, W["pool_w"][0], W["pool_scale"], vec(ln_g[0, 0]),
                                                    vec(ln_b[0, 0]), "pool_ln_fwd", tm=te)
    u0, act0, y0 = ffn_fwd(h1b, 0, "0")
    h2, h2b, xh2, rs2 = _ln_fwd(h1, y0, vec(ln_g[0, 1]), vec(ln_b[0, 1]), "ln01_fwd", tm=te)

    kvb = _mm(h2b, W["w_kv"], "nn", BF16, "kv_proj", tm=tm, tn=t2D, tk=tD)
    qb = _mm(h2b, W["w_q"][0], "nn", BF16, "q_proj", tm=tm, tn=tD, tk=tD, scale=HEAD_DIM ** -0.5)
    pre = _mm(h2b, wf_pad, "nn", F32, "f_proj", tm=tm, tn=LANES, tk=tD)
    c = _logf_cumsum(pre, bf_pad, "logf_cumsum", tm=tm)

    def heads(a):
        return a.reshape(Lp, H, HEAD_DIM).transpose(1, 0, 2)

    def tokens(a):
        return a.transpose(1, 0, 2).reshape(Lp, D)

    qh, kh, vh = heads(qb), heads(kvb[:, :D]), heads(kvb[:, D:])
    crow4 = c[:, :H].T.reshape(H, nq, 1, tm)
    oh, lse4 = _attn_fwd(qh, kh.reshape(H, nq, tm, HEAD_DIM), vh.reshape(H, nq, tm, HEAD_DIM), crow4,
                         "attn_fwd", tq=tm)
    o_tok = tokens(oh)
    ob = o_tok.astype(BF16)
    y_attn = _mm(ob, W["w_o"][0], "nn", F32, "o_proj", tm=tm, tn=tD, tk=tD)
    h3, h3b, xh3, rs3 = _ln_fwd(h2, y_attn, vec(ln_g[1, 0]), vec(ln_b[1, 0]), "ln10_fwd", tm=te)
    u1, act1, y1 = ffn_fwd(h3b, 1, "1")
    h4, _, xh4, rs4 = _ln_fwd(h3, y1, vec(ln_g[1, 1]), vec(ln_b[1, 1]), "ln11_fwd", tm=te)
    dy, loss = _loss_head(h4, tgt, "loss_head", tm=te, row_lo=N_META, row_hi=N_META + seq)

    dz4, dz4b, dg11, db11 = _ln_bwd([dy], [1.0], xh4, rs4, vec(ln_g[1, 1]), "ln11_bwd", tm=te)
    dh3, dw_in1, dw_out1, dcw1, dcb1 = ffn_bwd(dz4b, h3b, u1, act1, 1, "1")
    dz3, dz3b, dg10, db10 = _ln_bwd([dz4, dh3], [ALPHA, 1.0], xh3, rs3, vec(ln_g[1, 0]), "ln10_bwd", tm=te)

    do_tok = _mm(dz3b, W["w_o"][0], "nt", F32, "o_proj_dx", tm=tm, tn=tD, tk=tD)
    dw_o = _mm(ob, dz3b, "tn", F32, "o_proj_dw", tm=tD, tn=tD, tk=tm)
    delta = _attn_delta(do_tok, o_tok, "attn_delta", tm=te, n_heads=H)
    doh4 = heads(do_tok.astype(BF16)).reshape(H, nq, tm, HEAD_DIM)
    dqh4, dkh, dvh, dcs, dcq = _attn_bwd(qh.reshape(H, nq, tm, HEAD_DIM), doh4, kh, vh,
                                    lse4, delta[:, :H].T.reshape(H, nq, 1, tm), crow4,
                                    "attn_bwd", tq=tm)
    dqb = tokens(dqh4.reshape(H, Lp, HEAD_DIM)).astype(BF16)
    dkvb = jnp.concatenate([tokens(dkh), tokens(dvh)], axis=1).astype(BF16)
    dc_keys = jnp.pad(dcs.reshape(H, Lp).T, ((0, 0), (0, LANES - H)))
    dc_queries = jnp.pad(dcq.reshape(H, Lp).T, ((0, 0), (0, LANES - H)))
    dpreb, dbf = _logf_bwd(dc_keys, dc_queries, pre, bf_pad, "logf_bwd", tm=tm)

    qs = HEAD_DIM ** -0.5
    dw_q = _mm(h2b, dqb, "tn", F32, "q_proj_dw", tm=tD, tn=tD, tk=tm, scale=qs)
    dw_kv = _mm(h2b, dkvb, "tn", F32, "kv_proj_dw", tm=tD, tn=t2D, tk=tm)
    dw_f = _mm(h2b, dpreb, "tn", F32, "f_proj_dw", tm=tD, tn=LANES, tk=tm)[:, :H]
    dh2 = _mm(dqb, W["w_q"][0], "nt", F32, "q_proj_dx", tm=tm, tn=tD, tk=tD, scale=qs)
    dh2 = _mm(dkvb, W["w_kv"], "nt", F32, "kv_proj_dx", tm=tm, tn=tD, tk=t2D, add=dh2)
    dh2 = _mm(dpreb, wf_pad, "nt", F32, "f_proj_dx", tm=tm, tn=tD, tk=LANES, add=dh2)
    dz2, dz2b, dg01, db01 = _ln_bwd([dz3, dh2], [ALPHA, 1.0], xh2, rs2, vec(ln_g[0, 1]), "ln01_bwd", tm=te)

    dh1, dw_in0, dw_out0, dcw0, dcb0 = ffn_bwd(dz2b, h1b, u0, act0, 0, "0")
    dz1, _, dg00, db00 = _ln_bwd([dz2, dh1], [ALPHA, 1.0], xh1, rs1, vec(ln_g[0, 0]), "ln00_bwd", tm=te)
    dh0, dmb, dscale = _pool_bwd(dz1, mixpre, W["pool_w"][0], W["pool_scale"], "pool_bwd", tm=te)
    dw_pool = _pool_dw(diffb, dmb, "pool_dw", tk=tm)

    grads = {
        "meta": dh0[:N_META],
        "pool_w": dw_pool[None],
        "pool_scale": dscale,
        "w_kv": dw_kv,
        "w_f": dw_f,
        "b_f": dbf[0, :H],
        "w_q": dw_q[None],
        "w_o": dw_o[None],
        "ffn_w_in": jnp.stack([dw_in0, dw_in1]),
        "ffn_conv_w": jnp.stack([dcw0, dcw1]),
        "ffn_conv_b": jnp.stack([dcb0, dcb1]),
        "ffn_w_out": jnp.stack([dw_out0, dw_out1]),
        "ln_g": jnp.stack([jnp.stack([dg00[0], dg01[0]]), jnp.stack([dg10[0], dg11[0]])]),
        "ln_b": jnp.stack([jnp.stack([db00[0], db01[0]]), jnp.stack([db10[0], db11[0]])]),
    }
    return loss, dh0, grads


def _row_tile(length):
    return 640 if length >= 4096 else 128


def kernel(x, meta, pool_w, pool_scale, w_kv, w_f, b_f, w_q, w_o, ffn_w_in, ffn_conv_w, ffn_conv_b, ffn_w_out, ln_g, ln_b, loss_target, m_meta, m_pool_w, m_pool_scale, m_w_kv, m_w_f, m_b_f, m_w_q, m_w_o, m_ffn_w_in, m_ffn_conv_w, m_ffn_conv_b, m_ffn_w_out, m_ln_g, m_ln_b, v_meta, v_pool_w, v_pool_scale, v_w_kv, v_w_f, v_b_f, v_w_q, v_w_o, v_ffn_w_in, v_ffn_conv_w, v_ffn_conv_b, v_ffn_w_out, v_ln_g, v_ln_b):
    weights = dict(meta=meta, pool_w=pool_w, pool_scale=pool_scale, w_kv=w_kv, w_f=w_f, b_f=b_f, w_q=w_q, w_o=w_o,
                   ffn_w_in=ffn_w_in, ffn_conv_w=ffn_conv_w, ffn_conv_b=ffn_conv_b, ffn_w_out=ffn_w_out,
                   ln_g=ln_g, ln_b=ln_b)
    mom1 = dict(meta=m_meta, pool_w=m_pool_w, pool_scale=m_pool_scale, w_kv=m_w_kv, w_f=m_w_f, b_f=m_b_f, w_q=m_w_q,
                w_o=m_w_o, ffn_w_in=m_ffn_w_in, ffn_conv_w=m_ffn_conv_w, ffn_conv_b=m_ffn_conv_b,
                ffn_w_out=m_ffn_w_out, ln_g=m_ln_g, ln_b=m_ln_b)
    mom2 = dict(meta=v_meta, pool_w=v_pool_w, pool_scale=v_pool_scale, w_kv=v_w_kv, w_f=v_w_f, b_f=v_b_f, w_q=v_w_q,
                w_o=v_w_o, ffn_w_in=v_ffn_w_in, ffn_conv_w=v_ffn_conv_w, ffn_conv_b=v_ffn_conv_b,
                ffn_w_out=v_ffn_w_out, ln_g=v_ln_g, ln_b=v_ln_b)
    _, seq, D = x.shape
    L = N_META + seq
    tm = _row_tile(L)
    Lp = _round_up(L, tm)
    c_idx = lax.axis_index("c")
    chip = 2 * lax.axis_index("x") + lax.axis_index("y")

    shard_shapes = {n: weights[n].shape for n in SHARDED}
    rows_b = _round_up(sum(_rows_of(shard_shapes[n]) for n in MATMUL_WEIGHTS), 32)
    rows_f = _round_up(sum(_rows_of(shard_shapes[n]) for n in VECTOR_WEIGHTS), SUBLANES)
    wb = _pack([weights[n] for n in MATMUL_WEIGHTS], rows_b, BF16)
    wf = _pack([weights[n] for n in VECTOR_WEIGHTS], rows_f, F32)
    gb, gf = _all_gather_weights(wb, wf, "weights_all_gather")
    full = {}
    for names, buf in ((MATMUL_WEIGHTS, gb), (VECTOR_WEIGHTS, gf)):
        for n, stacked in zip(names, _unpack(buf, [shard_shapes[n] for n in names])):
            full[n] = _join_shards(stacked, SHARD_AXIS[n])
    full["b_f"] = b_f
    full["ffn_conv_b"] = ffn_conv_b

    pad = jnp.zeros((Lp - L, D), F32)
    h0 = jnp.concatenate([full["meta"], x[0], pad], axis=0)
    tgt = jnp.concatenate([jnp.zeros((N_META, D), F32), loss_target[0], pad], axis=0)
    loss, dh0, grads = _local_step(h0, tgt, full, seq=seq, tm=tm)
    loss = lax.psum(loss[0, 0], AXES)
    grad_x = dh0[N_META:L][None]

    rows = _round_up(sum(_rows_of(shard_shapes[n]) for n in SHARDED), 2 * LANES)
    hr = rows // 2
    shapes = [shard_shapes[n] for n in SHARDED]
    gpack = jnp.stack([_pack([_shard_of(grads[n], SHARD_AXIS[n], s) for n in SHARDED], rows, F32)
                       for s in range(N_CHIPS)])
    core = c_idx.astype(jnp.int32).reshape(1)
    got = _send_half_to_sibling(gpack, "grads_to_sibling")
    part = _add_halves(gpack, got, core, "grads_chip_sum", tr=LANES)

    rep_shapes = [weights[n].shape for n in REPLICATED]
    rows_r = _round_up(sum(_rows_of(s) for s in rep_shapes), SUBLANES)
    rep = _pack([grads[n] for n in REPLICATED], rows_r, F32)
    landed, reps = _chip_exchange(part, rep, "grads_chip_exchange")

    def packed(d):
        return _pack([d[n] for n in SHARDED], rows, F32)

    place = jnp.stack([c_idx, chip]).astype(jnp.int32)
    mine = _sum_adamw_half(part, landed, packed(weights), packed(mom1), packed(mom2), place, "adamw_sharded", tr=LANES)
    res = _join_with_sibling(mine, "results_to_sibling")
    sharded_out = _unpack(res, shapes)

    def packr(d):
        return _pack([d[n] for n in REPLICATED], rows_r, F32)

    res_r = _sum_adamw(reps[0], reps[1:], packr(weights), packr(mom1), packr(mom2), "adamw_replicated", tr=rows_r)
    rep_out = _unpack(res_r, rep_shapes)

    out = {n: a for n, a in zip(SHARDED, sharded_out)}
    out.update({n: a for n, a in zip(REPLICATED, rep_out)})
    result = [loss, grad_x]
    for k in range(4):
        result += [out[n][k] for n in WEIGHT_ORDER]
    return tuple(result)
```

```python
import functools

import jax
import jax.numpy as jnp
from jax import lax
from jax.experimental import pallas as pl
from jax.experimental.pallas import tpu as pltpu

N_META = 16
POOL_WINDOWS = (2, 4, 8, 16)
MAX_WINDOW = max(POOL_WINDOWS)
N_GROUPS = len(POOL_WINDOWS)
HEAD_DIM = 64
DEPTH = 2
CONV_WIDTH = 3
ALPHA = (2.0 * DEPTH) ** 0.25
LN_EPS = 1e-5
NEG_INF = -1e30
ADAM_LR = 0.001
ADAM_B1 = 0.9
ADAM_B2 = 0.999
ADAM_EPS = 1e-08
ADAM_WD = 0.01
ADAM_STEP = 10

F32 = jnp.float32
BF16 = jnp.bfloat16
ATTN_STRIP = 32
ATTN_HEADS = 2
GLU_STRIP = 16
LANES = 128
SUBLANES = 8
PACK_COLS = 1024
VMEM_LIMIT = 56 * 1024 * 1024
AXES = ("x", "y", "c")
MESH = pl.DeviceIdType.MESH

NN = (((1,), (0,)), ((), ()))
NT = (((1,), (1,)), ((), ()))
TN = (((0,), (0,)), ((), ()))

SHARD_AXIS = {"meta": 1, "pool_w": 2, "pool_scale": 1, "w_kv": 1, "w_f": 0, "w_q": 1, "w_o": 1,
              "ffn_w_in": 2, "ffn_conv_w": 2, "ffn_w_out": 1, "ln_g": 2, "ln_b": 2}
SHARDED = ("meta", "pool_w", "pool_scale", "w_kv", "w_f", "w_q", "w_o", "ffn_w_in", "ffn_conv_w",
           "ffn_w_out", "ln_g", "ln_b")
REPLICATED = ("b_f", "ffn_conv_b")
MATMUL_WEIGHTS = ("pool_w", "w_kv", "w_f", "w_q", "w_o", "ffn_w_in", "ffn_w_out")
VECTOR_WEIGHTS = ("meta", "pool_scale", "ffn_conv_w", "ln_g", "ln_b")
WEIGHT_ORDER = ("meta", "pool_w", "pool_scale", "w_kv", "w_f", "b_f", "w_q", "w_o", "ffn_w_in",
                "ffn_conv_w", "ffn_conv_b", "ffn_w_out", "ln_g", "ln_b")
N_CHIPS = 4
N_DEV = 8


def _cparams(*sem):
    return pltpu.CompilerParams(dimension_semantics=sem, vmem_limit_bytes=VMEM_LIMIT)


def _round_up(n, m):
    return (n + m - 1) // m * m


def _pick(n, cap):
    if n <= cap:
        return n
    best = 0
    for t in range(LANES, cap + 1, LANES):
        if n % t == 0:
            best = t
    assert best, (n, cap)
    return best


def _mm(a, b, mode, out_dtype, name, *, tm, tn, tk, scale=None, add=None):
    if mode == "nn":
        (M, K), N = a.shape, b.shape[1]
    elif mode == "nt":
        (M, K), N = a.shape, b.shape[0]
    else:
        (K, M), N = a.shape, b.shape[1]
    assert M % tm == 0 and N % tn == 0 and K % tk == 0, (name, M, N, K, tm, tn, tk)
    nk = K // tk
    dn = {"nn": NN, "nt": NT, "tn": TN}[mode]
    has_add = add is not None

    def body(*refs):
        a_ref, b_ref = refs[0], refs[1]
        add_ref = refs[2] if has_add else None
        o_ref = refs[3] if has_add else refs[2]
        acc_ref = refs[-1] if nk > 1 else None
        k = pl.program_id(2)
        part = lax.dot_general(a_ref[...], b_ref[...], dn, preferred_element_type=F32)

        def finish(r):
            if scale is not None:
                r = r * scale
            if has_add:
                r = r + add_ref[...]
            o_ref[...] = r.astype(out_dtype)

        if nk == 1:
            finish(part)
        else:
            @pl.when(k == 0)
            def _():
                acc_ref[...] = part

            @pl.when(k > 0)
            def _():
                acc_ref[...] += part

            @pl.when(k == nk - 1)
            def _():
                finish(acc_ref[...])

    if mode == "nn":
        a_spec = pl.BlockSpec((tm, tk), lambda j, i, k: (i, k))
        b_spec = pl.BlockSpec((tk, tn), lambda j, i, k: (k, j))
    elif mode == "nt":
        a_spec = pl.BlockSpec((tm, tk), lambda j, i, k: (i, k))
        b_spec = pl.BlockSpec((tn, tk), lambda j, i, k: (j, k))
    else:
        a_spec = pl.BlockSpec((tk, tm), lambda j, i, k: (k, i))
        b_spec = pl.BlockSpec((tk, tn), lambda j, i, k: (k, j))
    o_spec = pl.BlockSpec((tm, tn), lambda j, i, k: (i, j))
    in_specs = [a_spec, b_spec] + ([o_spec] if has_add else [])
    args = (a, b) + ((add,) if has_add else ())
    return pl.pallas_call(
        body, name=name, grid=(N // tn, M // tm, nk),
        in_specs=in_specs, out_specs=o_spec,
        out_shape=jax.ShapeDtypeStruct((M, N), out_dtype),
        scratch_shapes=[pltpu.VMEM((tm, tn), F32)] if nk > 1 else [],
        compiler_params=_cparams("parallel", "parallel", "arbitrary"),
    )(*args)


def _ln_math(z, g, b):
    mu = jnp.mean(z, axis=-1, keepdims=True)
    zc = z - mu
    var = jnp.mean(zc * zc, axis=-1, keepdims=True)
    rstd = lax.rsqrt(var + LN_EPS)
    xh = zc * rstd
    return xh * g + b, xh, rstd


def _pool_ln_fwd(h0, pw, ps, g, b, name, *, tm):
    Lp, D = h0.shape
    G = D // N_GROUPS
    halo_blocks = tm // MAX_WINDOW

    def body(x_ref, halo_ref, pw_ref, ps_ref, g_ref, b_ref,
             diff_ref, mix_ref, h_ref, hb_ref, xh_ref, rs_ref, ext_ref):
        i = pl.program_id(0)
        ext_ref[0:MAX_WINDOW, :] = jnp.where(i == 0, 0.0, halo_ref[...])
        ext_ref[MAX_WINDOW:MAX_WINDOW + tm, :] = x_ref[...]
        t1 = (i * tm + 1 + lax.broadcasted_iota(jnp.int32, (tm, 1), 0)).astype(F32)
        for gi, w in enumerate(POOL_WINDOWS):
            lo, hi = gi * G, (gi + 1) * G
            xg = x_ref[:, lo:hi]
            win = xg
            for j in range(1, w):
                win = win + ext_ref[MAX_WINDOW - j:MAX_WINDOW - j + tm, lo:hi]
            d = (win / jnp.minimum(t1, float(w)) - xg).astype(BF16)
            diff_ref[:, lo:hi] = d
            mix_ref[:, lo:hi] = jnp.dot(d, pw_ref[gi], preferred_element_type=F32)
        z = ALPHA * x_ref[...] + mix_ref[...] * ps_ref[...]
        h, xh, rstd = _ln_math(z, g_ref[...], b_ref[...])
        h_ref[...] = h
        hb_ref[...] = h.astype(BF16)
        xh_ref[...] = xh
        rs_ref[...] = rstd

    row = pl.BlockSpec((tm, D), lambda i: (i, 0))
    vec = pl.BlockSpec((1, D), lambda i: (0, 0))
    return pl.pallas_call(
        body, name=name, grid=(Lp // tm,),
        in_specs=[row,
                  pl.BlockSpec((MAX_WINDOW, D), lambda i: (jnp.maximum(i * halo_blocks - 1, 0), 0)),
                  pl.BlockSpec((N_GROUPS, G, G), lambda i: (0, 0, 0)), vec, vec, vec],
        out_specs=[row, row, row, row, row, pl.BlockSpec((tm, 1), lambda i: (i, 0))],
        out_shape=[jax.ShapeDtypeStruct((Lp, D), BF16), jax.ShapeDtypeStruct((Lp, D), F32),
                   jax.ShapeDtypeStruct((Lp, D), F32), jax.ShapeDtypeStruct((Lp, D), BF16),
                   jax.ShapeDtypeStruct((Lp, D), F32), jax.ShapeDtypeStruct((Lp, 1), F32)],
        scratch_shapes=[pltpu.VMEM((tm + MAX_WINDOW, D), F32)],
        compiler_params=_cparams("parallel"),
    )(h0, h0, pw, ps, g, b)


def _pool_bwd(dz, mixpre, pw, ps, name, *, tm):
    Lp, D = dz.shape
    G = D // N_GROUPS
    halo_blocks = tm // MAX_WINDOW
    n_halo = Lp // MAX_WINDOW
    ni = Lp // tm
    R = tm + MAX_WINDOW

    def body(dz_ref, halo_ref, mix_ref, pw_ref, ps_ref, dh_ref, dmb_ref, dsc_ref, ext_ref, dp_ref):
        i = pl.program_id(0)
        ext_ref[0:tm, :] = dz_ref[...]
        ext_ref[tm:R, :] = jnp.where(i == ni - 1, 0.0, halo_ref[...])
        dmix = (ext_ref[...] * ps_ref[...]).astype(BF16)
        dmb_ref[...] = dmix[0:tm]

        @pl.when(i == 0)
        def _():
            dsc_ref[...] = jnp.zeros_like(dsc_ref)

        dsc_ref[...] += jnp.sum(dz_ref[...] * mix_ref[...], axis=0, keepdims=True)
        t1 = (i * tm + 1 + lax.broadcasted_iota(jnp.int32, (R, 1), 0)).astype(F32)
        for gi, w in enumerate(POOL_WINDOWS):
            lo, hi = gi * G, (gi + 1) * G
            dd = lax.dot_general(dmix[:, lo:hi], pw_ref[gi], NT, preferred_element_type=F32)
            dp_ref[:, lo:hi] = dd / jnp.minimum(t1, float(w))
            back = dp_ref[0:tm, lo:hi]
            for j in range(1, w):
                back = back + dp_ref[j:j + tm, lo:hi]
            dh_ref[:, lo:hi] = ALPHA * dz_ref[:, lo:hi] - dd[0:tm] + back

    row = pl.BlockSpec((tm, D), lambda i: (i, 0))
    vec = pl.BlockSpec((1, D), lambda i: (0, 0))
    return pl.pallas_call(
        body, name=name, grid=(ni,),
        in_specs=[row,
                  pl.BlockSpec((MAX_WINDOW, D), lambda i: (jnp.minimum((i + 1) * halo_blocks, n_halo - 1), 0)),
                  row, pl.BlockSpec((N_GROUPS, G, G), lambda i: (0, 0, 0)), vec],
        out_specs=[row, row, vec],
        out_shape=[jax.ShapeDtypeStruct((Lp, D), F32), jax.ShapeDtypeStruct((Lp, D), BF16),
                   jax.ShapeDtypeStruct((1, D), F32)],
        scratch_shapes=[pltpu.VMEM((R, D), F32), pltpu.VMEM((R, D), F32)],
        compiler_params=_cparams("arbitrary"),
    )(dz, dz, mixpre, pw, ps)


def _pool_dw(diffb, dmb, name, *, tk):
    Lp, D = diffb.shape
    G = D // N_GROUPS

    def body(a_ref, b_ref, o_ref):
        @pl.when(pl.program_id(1) == 0)
        def _():
            o_ref[...] = jnp.zeros_like(o_ref)

        o_ref[0] += lax.dot_general(a_ref[...], b_ref[...], TN, preferred_element_type=F32)

    blk = pl.BlockSpec((tk, G), lambda g, k: (k, g))
    return pl.pallas_call(
        body, name=name, grid=(N_GROUPS, Lp // tk),
        in_specs=[blk, blk], out_specs=pl.BlockSpec((1, G, G), lambda g, k: (g, 0, 0)),
        out_shape=jax.ShapeDtypeStruct((N_GROUPS, G, G), F32),
        compiler_params=_cparams("parallel", "arbitrary"),
    )(diffb, dmb)


def _ln_fwd(resid, y, g, b, name, *, tm):
    Lp, D = resid.shape

    def body(r_ref, y_ref, g_ref, b_ref, h_ref, hb_ref, xh_ref, rs_ref):
        h, xh, rstd = _ln_math(ALPHA * r_ref[...] + y_ref[...], g_ref[...], b_ref[...])
        h_ref[...] = h
        hb_ref[...] = h.astype(BF16)
        xh_ref[...] = xh
        rs_ref[...] = rstd

    row = pl.BlockSpec((tm, D), lambda i: (i, 0))
    vec = pl.BlockSpec((1, D), lambda i: (0, 0))
    return pl.pallas_call(
        body, name=name, grid=(Lp // tm,),
        in_specs=[row, row, vec, vec],
        out_specs=[row, row, row, pl.BlockSpec((tm, 1), lambda i: (i, 0))],
        out_shape=[jax.ShapeDtypeStruct((Lp, D), F32), jax.ShapeDtypeStruct((Lp, D), BF16),
                   jax.ShapeDtypeStruct((Lp, D), F32), jax.ShapeDtypeStruct((Lp, 1), F32)],
        compiler_params=_cparams("parallel"),
    )(resid, y, g, b)


def _ln_bwd(parts, coefs, xh, rs, g, name, *, tm):
    Lp, D = xh.shape
    n = len(parts)

    def body(*refs):
        part_refs = refs[:n]
        xh_ref, rs_ref, g_ref = refs[n:n + 3]
        dz_ref, dzb_ref, dg_ref, db_ref = refs[n + 3:]
        dy = part_refs[0][...] if coefs[0] == 1.0 else coefs[0] * part_refs[0][...]
        for c, r in zip(coefs[1:], part_refs[1:]):
            dy = dy + (r[...] if c == 1.0 else c * r[...])
        x = xh_ref[...]
        dxh = dy * g_ref[...]
        m1 = jnp.mean(dxh, axis=-1, keepdims=True)
        m2 = jnp.mean(dxh * x, axis=-1, keepdims=True)
        dz = rs_ref[...] * (dxh - m1 - x * m2)
        dz_ref[...] = dz
        dzb_ref[...] = dz.astype(BF16)

        @pl.when(pl.program_id(0) == 0)
        def _():
            dg_ref[...] = jnp.zeros_like(dg_ref)
            db_ref[...] = jnp.zeros_like(db_ref)

        dg_ref[...] += jnp.sum(dy * x, axis=0, keepdims=True)
        db_ref[...] += jnp.sum(dy, axis=0, keepdims=True)

    row = pl.BlockSpec((tm, D), lambda i: (i, 0))
    vec = pl.BlockSpec((1, D), lambda i: (0, 0))
    return pl.pallas_call(
        body, name=name, grid=(Lp // tm,),
        in_specs=[row] * n + [row, pl.BlockSpec((tm, 1), lambda i: (i, 0)), vec],
        out_specs=[row, row, vec, vec],
        out_shape=[jax.ShapeDtypeStruct((Lp, D), F32), jax.ShapeDtypeStruct((Lp, D), BF16),
                   jax.ShapeDtypeStruct((1, D), F32), jax.ShapeDtypeStruct((1, D), F32)],
        compiler_params=_cparams("arbitrary"),
    )(*parts, xh, rs, g)


def _loss_head(h, tgt, name, *, tm, row_lo, row_hi):
    Lp, D = h.shape

    def body(h_ref, t_ref, dy_ref, loss_ref):
        i = pl.program_id(0)
        r = i * tm + lax.broadcasted_iota(jnp.int32, (tm, 1), 0)
        valid = (r >= row_lo) & (r < row_hi)
        e = jnp.where(valid, h_ref[...] - t_ref[...], 0.0)
        dy_ref[...] = e * (1.0 / D)

        @pl.when(i == 0)
        def _():
            loss_ref[...] = jnp.zeros_like(loss_ref)

        loss_ref[...] += 0.5 * jnp.sum(jnp.mean(e * e, axis=-1, keepdims=True), axis=0, keepdims=True)

    row = pl.BlockSpec((tm, D), lambda i: (i, 0))
    return pl.pallas_call(
        body, name=name, grid=(Lp // tm,),
        in_specs=[row, row], out_specs=[row, pl.BlockSpec((1, 1), lambda i: (0, 0))],
        out_shape=[jax.ShapeDtypeStruct((Lp, D), F32), jax.ShapeDtypeStruct((1, 1), F32)],
        compiler_params=_cparams("arbitrary"),
    )(h, tgt)


def _shift_rows_down(cur, prev, s, sub):
    return jnp.where(sub >= s, pltpu.roll(cur, s, 0), pltpu.roll(prev, s, 0))


def _shift_rows_up(cur, nxt, s, sub):
    return jnp.where(sub < SUBLANES - s, pltpu.roll(cur, SUBLANES - s, 0), pltpu.roll(nxt, SUBLANES - s, 0))


def _conv_group(cur, prev, cw_ref, cb_ref, sub):
    taps = [_shift_rows_down(cur, prev, 2, sub), _shift_rows_down(cur, prev, 1, sub), cur]
    c = cb_ref[...] + cw_ref[0:1, :] * taps[0] + cw_ref[1:2, :] * taps[1] + cw_ref[2:3, :] * taps[2]
    return c, taps


def _conv_glu_fwd(u, cw, cb, name, *, tm, tn):
    Lp, F2 = u.shape
    F = F2 // 2
    nj = F // tn
    halo_blocks = tm // SUBLANES
    S8 = SUBLANES
    assert GLU_STRIP == 2 * S8 and tm % GLU_STRIP == 0

    def body(ua_ref, ug_ref, pa_ref, pg_ref, cwa_ref, cwg_ref, cba_ref, cbg_ref, o_ref):
        first = pl.program_id(1) == 0
        sub = lax.broadcasted_iota(jnp.int32, (S8, tn), 0)

        def strip(r, prev_a, prev_g):
            out = []
            for g0 in (0, S8):
                a_cur = ua_ref[pl.ds(r + g0, S8), :]
                g_cur = ug_ref[pl.ds(r + g0, S8), :]
                a, _ = _conv_group(a_cur, prev_a, cwa_ref, cba_ref, sub)
                gate, _ = _conv_group(g_cur, prev_g, cwg_ref, cbg_ref, sub)
                out.append(a * jax.nn.sigmoid(a) * gate)
                prev_a, prev_g = a_cur, g_cur
            o_ref[pl.ds(r, GLU_STRIP), :] = jnp.concatenate(out, axis=0).astype(BF16)

        strip(0, jnp.where(first, 0.0, pa_ref[...]), jnp.where(first, 0.0, pg_ref[...]))

        def step(k, carry):
            r = pl.multiple_of(k * GLU_STRIP, GLU_STRIP)
            before = pl.ds(pl.multiple_of(r - S8, S8), S8)
            strip(r, ua_ref[before, :], ug_ref[before, :])
            return carry

        lax.fori_loop(1, tm // GLU_STRIP, step, 0)

    def prev(off):
        return pl.BlockSpec((SUBLANES, tn), lambda j, i: (jnp.maximum(i * halo_blocks - 1, 0), j + off))

    def cols(rows, off):
        return pl.BlockSpec((rows, tn), lambda j, i: (0, j + off))

    return pl.pallas_call(
        body, name=name, grid=(nj, Lp // tm),
        in_specs=[pl.BlockSpec((tm, tn), lambda j, i: (i, j)), pl.BlockSpec((tm, tn), lambda j, i: (i, j + nj)),
                  prev(0), prev(nj), cols(CONV_WIDTH, 0), cols(CONV_WIDTH, nj), cols(1, 0), cols(1, nj)],
        out_specs=pl.BlockSpec((tm, tn), lambda j, i: (i, j)),
        out_shape=jax.ShapeDtypeStruct((Lp, F), BF16),
        compiler_params=_cparams("parallel", "parallel"),
    )(u, u, u, u, cw, cw, cb, cb)


def _conv_glu_bwd(u, dact, cw, cb, name, *, tm, tn):
    Lp, F2 = u.shape
    F = F2 // 2
    nj = F // tn
    ni = Lp // tm
    halo_blocks = tm // SUBLANES
    n_halo = Lp // SUBLANES
    S8 = SUBLANES
    n_strips = tm // GLU_STRIP
    assert GLU_STRIP == 2 * S8 and tm % GLU_STRIP == 0

    def body(ua_ref, ug_ref, pa_ref, pg_ref, na_ref, ng_ref, da_ref, dn_ref,
             cwa_ref, cwg_ref, cba_ref, cbg_ref,
             dua_ref, dug_ref, dwa_ref, dwg_ref, dba_ref, dbg_ref,
             wacc_a, wacc_g, bacc_a, bacc_g):
        i = pl.program_id(1)
        first, last = i == 0, i == ni - 1
        sub = lax.broadcasted_iota(jnp.int32, (S8, tn), 0)
        for acc in (wacc_a, wacc_g, bacc_a, bacc_g):
            acc[...] = jnp.zeros_like(acc)

        def dconv(a_cur, a_prev, g_cur, g_prev, dact_rows):
            a, taps_a = _conv_group(a_cur, a_prev, cwa_ref, cba_ref, sub)
            gate, taps_g = _conv_group(g_cur, g_prev, cwg_ref, cbg_ref, sub)
            sg = jax.nn.sigmoid(a)
            dca = dact_rows * gate * (sg * (1.0 + a * (1.0 - sg)))
            dcg = dact_rows * (a * sg)
            return dca, dcg, taps_a, taps_g

        def du_group(dc, dc_after, cw_ref):
            return (cw_ref[2:3, :] * dc + cw_ref[1:2, :] * _shift_rows_up(dc, dc_after, 1, sub)
                    + cw_ref[0:1, :] * _shift_rows_up(dc, dc_after, 2, sub))

        def strip(r, a_prev, g_prev, dca_after, dcg_after):
            a0, a1 = ua_ref[pl.ds(r, S8), :], ua_ref[pl.ds(r + S8, S8), :]
            g0, g1 = ug_ref[pl.ds(r, S8), :], ug_ref[pl.ds(r + S8, S8), :]
            dca1, dcg1, ta1, tg1 = dconv(a1, a0, g1, g0, da_ref[pl.ds(r + S8, S8), :])
            dca0, dcg0, ta0, tg0 = dconv(a0, a_prev, g0, g_prev, da_ref[pl.ds(r, S8), :])
            dua_ref[pl.ds(r, GLU_STRIP), :] = jnp.concatenate(
                [du_group(dca0, dca1, cwa_ref), du_group(dca1, dca_after, cwa_ref)], axis=0).astype(BF16)
            dug_ref[pl.ds(r, GLU_STRIP), :] = jnp.concatenate(
                [du_group(dcg0, dcg1, cwg_ref), du_group(dcg1, dcg_after, cwg_ref)], axis=0).astype(BF16)
            for k in range(CONV_WIDTH):
                wacc_a[k] += dca0 * ta0[k] + dca1 * ta1[k]
                wacc_g[k] += dcg0 * tg0[k] + dcg1 * tg1[k]
            bacc_a[...] += dca0 + dca1
            bacc_g[...] += dcg0 + dcg1
            return dca0, dcg0

        tail = pl.ds(tm - S8, S8)
        dca_after, dcg_after, _, _ = dconv(na_ref[...], ua_ref[tail, :], ng_ref[...], ug_ref[tail, :],
                                           jnp.where(last, 0.0, dn_ref[...]))

        def step(t, carry):
            r = pl.multiple_of((n_strips - 1 - t) * GLU_STRIP, GLU_STRIP)
            before = pl.ds(pl.multiple_of(r - S8, S8), S8)
            return strip(r, ua_ref[before, :], ug_ref[before, :], *carry)

        dca_after, dcg_after = lax.fori_loop(0, n_strips - 1, step, (dca_after, dcg_after))
        strip(0, jnp.where(first, 0.0, pa_ref[...]), jnp.where(first, 0.0, pg_ref[...]), dca_after, dcg_after)

        @pl.when(first)
        def _():
            for r in (dwa_ref, dwg_ref, dba_ref, dbg_ref):
                r[...] = jnp.zeros_like(r)

        for wacc, bacc, dw_ref, db_ref in ((wacc_a, bacc_a, dwa_ref, dba_ref), (wacc_g, bacc_g, dwg_ref, dbg_ref)):
            db_ref[...] += jnp.sum(bacc[...], axis=0, keepdims=True)
            for k in range(CONV_WIDTH):
                dw_ref[k:k + 1, :] += jnp.sum(wacc[k], axis=0, keepdims=True)

    def tile(off):
        return pl.BlockSpec((tm, tn), lambda j, i: (i, j + off))

    def prev(off):
        return pl.BlockSpec((S8, tn), lambda j, i: (jnp.maximum(i * halo_blocks - 1, 0), j + off))

    def nxt(off):
        return pl.BlockSpec((S8, tn), lambda j, i: (jnp.minimum((i + 1) * halo_blocks, n_halo - 1), j + off))

    def cols(rows, off):
        return pl.BlockSpec((rows, tn), lambda j, i: (0, j + off))

    return pl.pallas_call(
        body, name=name, grid=(nj, ni),
        in_specs=[tile(0), tile(nj), prev(0), prev(nj), nxt(0), nxt(nj), tile(0), nxt(0),
                  cols(CONV_WIDTH, 0), cols(CONV_WIDTH, nj), cols(1, 0), cols(1, nj)],
        out_specs=[tile(0), tile(0), cols(CONV_WIDTH, 0), cols(CONV_WIDTH, 0), cols(1, 0), cols(1, 0)],
        out_shape=[jax.ShapeDtypeStruct((Lp, F), BF16), jax.ShapeDtypeStruct((Lp, F), BF16),
                   jax.ShapeDtypeStruct((CONV_WIDTH, F), F32), jax.ShapeDtypeStruct((CONV_WIDTH, F), F32),
                   jax.ShapeDtypeStruct((1, F), F32), jax.ShapeDtypeStruct((1, F), F32)],
        scratch_shapes=[pltpu.VMEM((CONV_WIDTH, S8, tn), F32), pltpu.VMEM((CONV_WIDTH, S8, tn), F32),
                        pltpu.VMEM((S8, tn), F32), pltpu.VMEM((S8, tn), F32)],
        compiler_params=_cparams("parallel", "arbitrary"),
    )(u, u, u, u, u, u, dact, dact, cw, cw, cb, cb)


def _logf_cumsum(pre, bf, name, *, tm):
    Lp, W = pre.shape

    def body(p_ref, b_ref, c_ref, carry_ref):
        i = pl.program_id(0)

        @pl.when(i == 0)
        def _():
            carry_ref[...] = jnp.zeros_like(carry_ref)

        x = p_ref[...] + b_ref[...]
        lf = jnp.minimum(x, 0.0) - jnp.log(1.0 + jnp.exp(-jnp.abs(x)))
        tri = (lax.broadcasted_iota(jnp.int32, (tm, tm), 0) >= lax.broadcasted_iota(jnp.int32, (tm, tm), 1)).astype(F32)
        c = jnp.dot(tri, lf, precision=lax.Precision.HIGHEST, preferred_element_type=F32) + carry_ref[...]
        c_ref[...] = c
        carry_ref[...] = c[tm - 1:tm, :]

    row = pl.BlockSpec((tm, W), lambda i: (i, 0))
    return pl.pallas_call(
        body, name=name, grid=(Lp // tm,),
        in_specs=[row, pl.BlockSpec((1, W), lambda i: (0, 0))], out_specs=row,
        out_shape=jax.ShapeDtypeStruct((Lp, W), F32),
        scratch_shapes=[pltpu.VMEM((1, W), F32)],
        compiler_params=_cparams("arbitrary"),
    )(pre, bf)


def _logf_bwd(dc_a, dc_b, pre, bf, name, *, tm):
    Lp, W = pre.shape
    ni = Lp // tm

    def body(dca_ref, dcb_ref, p_ref, b_ref, dpb_ref, db_ref, carry_ref):
        i = pl.program_id(0)

        @pl.when(i == 0)
        def _():
            carry_ref[...] = jnp.zeros_like(carry_ref)
            db_ref[...] = jnp.zeros_like(db_ref)

        triu = (lax.broadcasted_iota(jnp.int32, (tm, tm), 0) <= lax.broadcasted_iota(jnp.int32, (tm, tm), 1)).astype(F32)
        dl = jnp.dot(triu, dca_ref[...] + dcb_ref[...], precision=lax.Precision.HIGHEST,
                     preferred_element_type=F32) + carry_ref[...]
        carry_ref[...] = dl[0:1, :]
        dp = dl * jax.nn.sigmoid(-(p_ref[...] + b_ref[...]))
        dpb_ref[...] = dp.astype(BF16)
        db_ref[...] += jnp.sum(dp, axis=0, keepdims=True)

    rev = pl.BlockSpec((tm, W), lambda i: (ni - 1 - i, 0))
    vec = pl.BlockSpec((1, W), lambda i: (0, 0))
    return pl.pallas_call(
        body, name=name, grid=(ni,),
        in_specs=[rev, rev, rev, vec], out_specs=[rev, vec],
        out_shape=[jax.ShapeDtypeStruct((Lp, W), BF16), jax.ShapeDtypeStruct((1, W), F32)],
        scratch_shapes=[pltpu.VMEM((1, W), F32)],
        compiler_params=_cparams("arbitrary"),
    )(dc_a, dc_b, pre, bf)


def _attn_fwd(qh, kh4, vh4, crow4, name, *, tq):
    H, Lp, dh = qh.shape
    nq = Lp // tq
    S8 = SUBLANES
    HB = ATTN_HEADS
    n_scratch = 6
    lane_tiles = tq // LANES

    def to_column(row8):
        return jnp.transpose(jnp.concatenate([row8] * (LANES // S8), axis=0))

    def body(q_ref, k_ref, v_ref, c_ref, o_ref, lse_ref, *scratch):
        i = pl.program_id(1)
        heads = [scratch[n_scratch * hb:n_scratch * (hb + 1)] for hb in range(HB)]

        @pl.when(i == 0)
        def _():
            for hb, refs in enumerate(heads):
                for j in range(nq):
                    refs[5][j] = to_column(jnp.concatenate([c_ref[hb, j]] * S8, axis=0))

        for m_ref, l_ref, acc_ref, _, _, _ in heads:
            m_ref[...] = jnp.full_like(m_ref, NEG_INF)
            l_ref[...] = jnp.zeros_like(l_ref)
            acc_ref[...] = jnp.zeros_like(acc_ref)

        def chunk(j, masked):
            for hb, (_, _, _, st_ref, _, _) in enumerate(heads):
                st_ref[...] = lax.dot_general(k_ref[hb, j], q_ref[hb], NT, preferred_element_type=F32)
            for hb, (m_ref, l_ref, acc_ref, st_ref, pt_ref, cs_ref) in enumerate(heads):
                ct = c_ref[hb, i]
                mx = jnp.full((S8, tq), NEG_INF, F32)
                for r0 in range(0, tq, ATTN_STRIP):
                    rows = pl.ds(r0, ATTN_STRIP)
                    cs = jnp.concatenate([cs_ref[j, rows, :]] * lane_tiles, axis=1)
                    st = st_ref[rows, :] + (ct - cs)
                    if masked:
                        keep = (lax.broadcasted_iota(jnp.int32, (ATTN_STRIP, tq), 1)
                                >= r0 + lax.broadcasted_iota(jnp.int32, (ATTN_STRIP, tq), 0))
                        st = jnp.where(keep, st, NEG_INF)
                    st_ref[rows, :] = st
                    for g0 in range(0, ATTN_STRIP, S8):
                        mx = jnp.maximum(mx, st[g0:g0 + S8])
                m_prev = m_ref[...]
                m_new = jnp.maximum(m_prev, jnp.max(mx, axis=0, keepdims=True))
                alpha = jnp.exp(m_prev - m_new)
                m_ref[...] = m_new
                ls = jnp.zeros((S8, tq), F32)
                for r0 in range(0, tq, ATTN_STRIP):
                    pieces = [jnp.exp(st_ref[pl.ds(r0 + g0, S8), :] - m_new) for g0 in range(0, ATTN_STRIP, S8)]
                    for piece in pieces:
                        ls = ls + piece
                    pt_ref[pl.ds(r0, ATTN_STRIP), :] = jnp.concatenate(pieces, axis=0).astype(BF16)
                l_ref[...] = alpha * l_ref[...] + ls
                pv = lax.dot_general(pt_ref[...], v_ref[hb, j], TN, preferred_element_type=F32)
                acc_ref[...] = to_column(alpha)[:, :dh] * acc_ref[...] + pv

        def step(j, carry):
            chunk(j, False)
            return carry

        lax.fori_loop(0, i, step, 0)
        chunk(i, True)
        for hb, (m_ref, l_ref, acc_ref, _, _, _) in enumerate(heads):
            l_row = jnp.sum(l_ref[...], axis=0, keepdims=True)
            l8 = jnp.concatenate([l_row] * S8, axis=0)
            o_ref[hb] = acc_ref[...] / to_column(l8)[:, :dh]
            lse_ref[hb, 0] = m_ref[0:1, :] + jnp.log(l_row)

    per_head = [pltpu.VMEM((S8, tq), F32), pltpu.VMEM((S8, tq), F32), pltpu.VMEM((tq, dh), F32),
                pltpu.VMEM((tq, tq), F32), pltpu.VMEM((tq, tq), BF16), pltpu.VMEM((nq, tq, LANES), F32)]
    assert len(per_head) == n_scratch and H % HB == 0
    return pl.pallas_call(
        body, name=name, grid=(H // HB, nq),
        in_specs=[pl.BlockSpec((HB, tq, dh), lambda h, i: (h, i, 0)),
                  pl.BlockSpec((HB, nq, tq, dh), lambda h, i: (h, 0, 0, 0)),
                  pl.BlockSpec((HB, nq, tq, dh), lambda h, i: (h, 0, 0, 0)),
                  pl.BlockSpec((HB, nq, 1, tq), lambda h, i: (h, 0, 0, 0))],
        out_specs=[pl.BlockSpec((HB, tq, dh), lambda h, i: (h, i, 0)),
                   pl.BlockSpec((HB, 1, 1, tq), lambda h, i: (h, i, 0, 0))],
        out_shape=[jax.ShapeDtypeStruct((H, Lp, dh), F32), jax.ShapeDtypeStruct((H, nq, 1, tq), F32)],
        scratch_shapes=per_head * HB,
        compiler_params=_cparams("parallel", "arbitrary"),
    )(qh, kh4, vh4, crow4)


def _attn_delta(do, o, name, *, tm, n_heads):
    Lp, D = do.shape

    def body(do_ref, o_ref, d_ref):
        sel = (lax.broadcasted_iota(jnp.int32, (D, LANES), 0) // HEAD_DIM
               == lax.broadcasted_iota(jnp.int32, (D, LANES), 1)).astype(F32)
        do = do_ref[...].astype(BF16).astype(F32)
        d_ref[...] = jnp.dot(do * o_ref[...], sel, precision=lax.Precision.HIGHEST,
                             preferred_element_type=F32)

    row = pl.BlockSpec((tm, D), lambda i: (i, 0))
    return pl.pallas_call(
        body, name=name, grid=(Lp // tm,),
        in_specs=[row, row], out_specs=pl.BlockSpec((tm, LANES), lambda i: (i, 0)),
        out_shape=jax.ShapeDtypeStruct((Lp, LANES), F32),
        compiler_params=_cparams("parallel"),
    )(do, o)


def _attn_bwd(qh4, doh4, kh, vh, lse4, delta4, crow4, name, *, tq):
    H, nq, _, dh = qh4.shape
    Lp = nq * tq

    lane_tiles = tq // LANES

    def body(q_ref, do_ref, k_ref, v_ref, lse_ref, dl_ref, ct_ref,
             dq_ref, dk_ref, dv_ref, dcs_ref, dcq_ref, dk_acc, dv_acc, dc_acc, st_ref, dp_ref, pt_ref, ds_ref, cs_ref):
        j = pl.program_id(1)
        cs_ref[...] = jnp.transpose(jnp.broadcast_to(ct_ref[0, j], (LANES, tq)))

        @pl.when(j == 0)
        def _():
            dq_ref[...] = jnp.zeros_like(dq_ref)
            dcq_ref[...] = jnp.zeros_like(dcq_ref)

        k = k_ref[0]
        v = v_ref[0]
        dk_acc[...] = jnp.zeros_like(dk_acc)
        dv_acc[...] = jnp.zeros_like(dv_acc)
        dc_acc[...] = jnp.zeros_like(dc_acc)

        def pair(i, masked):
            q = q_ref[0, i]
            do = do_ref[0, i]
            st_ref[...] = lax.dot_general(k, q, NT, preferred_element_type=F32)
            dp_ref[...] = lax.dot_general(v, do, NT, preferred_element_type=F32)
            bias_q = ct_ref[0, i] - lse_ref[0, i]
            delta = dl_ref[0, i]
            col_sum = jnp.zeros((SUBLANES, tq), F32)
            for r0 in range(0, tq, ATTN_STRIP):
                rows = pl.ds(r0, ATTN_STRIP)
                st = st_ref[rows, :] + (bias_q - jnp.concatenate([cs_ref[rows, :]] * lane_tiles, axis=1))
                if masked:
                    keep = (lax.broadcasted_iota(jnp.int32, (ATTN_STRIP, tq), 1)
                            >= r0 + lax.broadcasted_iota(jnp.int32, (ATTN_STRIP, tq), 0))
                    st = jnp.where(keep, st, NEG_INF)
                pt = jnp.exp(st)
                dst = pt * (dp_ref[rows, :] - delta)
                pt_ref[rows, :] = pt.astype(BF16)
                ds_ref[rows, :] = dst.astype(BF16)
                dc_acc[rows, :] += jnp.sum(dst, axis=1, keepdims=True)
                for g0 in range(0, ATTN_STRIP, SUBLANES):
                    col_sum = col_sum + dst[g0:g0 + SUBLANES]
            dcq_ref[0, i] += jnp.sum(col_sum, axis=0, keepdims=True)
            dv_acc[...] += jnp.dot(pt_ref[...], do, preferred_element_type=F32)
            dk_acc[...] += jnp.dot(ds_ref[...], q, preferred_element_type=F32)
            dq_ref[0, i] += lax.dot_general(ds_ref[...], k, TN, preferred_element_type=F32)

        def step(i, carry):
            pair(i, False)
            return carry

        pair(j, True)
        lax.fori_loop(j + 1, nq, step, 0)
        dk_ref[0] = dk_acc[...]
        dv_ref[0] = dv_acc[...]
        dcs_ref[0] = -dc_acc[...]

    whole = pl.BlockSpec((1, nq, tq, dh), lambda h, j: (h, 0, 0, 0))
    tile = pl.BlockSpec((1, tq, dh), lambda h, j: (h, j, 0))
    rows = pl.BlockSpec((1, nq, 1, tq), lambda h, j: (h, 0, 0, 0))
    col = pl.BlockSpec((1, tq, 1), lambda h, j: (h, j, 0))
    return pl.pallas_call(
        body, name=name, grid=(H, nq),
        in_specs=[whole, whole, tile, tile, rows, rows, rows],
        out_specs=[whole, tile, tile, col, rows],
        out_shape=[jax.ShapeDtypeStruct((H, nq, tq, dh), F32), jax.ShapeDtypeStruct((H, Lp, dh), F32),
                   jax.ShapeDtypeStruct((H, Lp, dh), F32), jax.ShapeDtypeStruct((H, Lp, 1), F32),
                   jax.ShapeDtypeStruct((H, nq, 1, tq), F32)],
        scratch_shapes=[pltpu.VMEM((tq, dh), F32), pltpu.VMEM((tq, dh), F32), pltpu.VMEM((tq, 1), F32),
                        pltpu.VMEM((tq, tq), F32), pltpu.VMEM((tq, tq), F32),
                        pltpu.VMEM((tq, tq), BF16), pltpu.VMEM((tq, tq), BF16), pltpu.VMEM((tq, LANES), F32)],
        compiler_params=_cparams("parallel", "arbitrary"),
    )(qh4, doh4, kh, vh, lse4, delta4, crow4)


def _remote(src, dst, send_sems, recv_sems, k, to):
    return pltpu.make_async_remote_copy(src_ref=src, dst_ref=dst, send_sem=send_sems.at[k], recv_sem=recv_sems.at[k],
                                        device_id=to, device_id_type=MESH)


def _place():
    x, y, c = lax.axis_index("x"), lax.axis_index("y"), lax.axis_index("c")
    other_chips = [(1 - x, y), (x, 1 - y), (1 - x, 1 - y)]
    return x, y, c, other_chips


def _all_gather_weights(wb, wf, name):
    Rb, C = wb.shape
    Rf = wf.shape[0]
    hb = Rb // 2

    def body(wb_ref, wf_ref, ob_ref, of_ref, send_sems, recv_sems):
        x, y, c, chips = _place()
        me = 2 * x + y
        sibling = (x, y, 1 - c)

        def half(chip, core):
            return ob_ref.at[chip, pl.ds(core * hb, hb), :]

        sent = []
        for j, (cx, cy) in enumerate(chips):
            sent.append(_remote(wb_ref.at[pl.ds(c * hb, hb), :], half(me, c), send_sems, recv_sems, j, (cx, cy, c)))
            sent.append(_remote(wf_ref, of_ref.at[me], send_sems, recv_sems, 3 + j, (cx, cy, c)))
        for cp in sent:
            cp.start()
        for j, (cx, cy) in enumerate(chips):
            chip = 2 * cx + cy
            _remote(half(chip, c), half(chip, c), send_sems, recv_sems, j, sibling).wait_recv()
            fwd = _remote(half(chip, c), half(chip, c), send_sems, recv_sems, 6 + j, sibling)
            fwd.start()
            sent.append(fwd)
        for j, (cx, cy) in enumerate(chips):
            chip = 2 * cx + cy
            _remote(wf_ref, of_ref.at[chip], send_sems, recv_sems, 3 + j, sibling).wait_recv()
            _remote(half(chip, 1 - c), half(chip, 1 - c), send_sems, recv_sems, 6 + j, sibling).wait_recv()
        for cp in sent:
            cp.wait_send()

    any_spec = pl.BlockSpec(memory_space=pl.ANY)
    return pl.pallas_call(
        body, name=name,
        in_specs=[any_spec, any_spec], out_specs=[any_spec, any_spec],
        out_shape=[jax.ShapeDtypeStruct((N_CHIPS, Rb, C), BF16), jax.ShapeDtypeStruct((N_CHIPS, Rf, C), F32)],
        scratch_shapes=[pltpu.SemaphoreType.DMA((9,)), pltpu.SemaphoreType.DMA((9,))],
    )(wb, wf)


def _send_half_to_sibling(buf, name):
    n, rows, C = buf.shape
    hr = rows // 2

    def body(src_ref, dst_ref, send_sems, recv_sems):
        x, y, c, _ = _place()
        cp = _remote(src_ref.at[pl.ds(0, n), pl.ds((1 - c) * hr, hr), :], dst_ref, send_sems, recv_sems, 0, (x, y, 1 - c))
        cp.start()
        cp.wait()

    any_spec = pl.BlockSpec(memory_space=pl.ANY)
    return pl.pallas_call(
        body, name=name, in_specs=[any_spec], out_specs=any_spec,
        out_shape=jax.ShapeDtypeStruct((n, hr, C), buf.dtype),
        scratch_shapes=[pltpu.SemaphoreType.DMA((1,)), pltpu.SemaphoreType.DMA((1,))],
    )(buf)


def _join_with_sibling(buf, name):
    n, rows, C = buf.shape
    hr = rows // 2

    def body(_, out_ref, send_sems, recv_sems):
        x, y, c, _ = _place()

        def half(core):
            return out_ref.at[pl.ds(0, n), pl.ds(core * hr, hr), :]

        cp = _remote(half(c), half(c), send_sems, recv_sems, 0, (x, y, 1 - c))
        cp.start()
        _remote(half(c), half(1 - c), send_sems, recv_sems, 0, (x, y, 1 - c)).wait_recv()
        cp.wait_send()

    any_spec = pl.BlockSpec(memory_space=pl.ANY)
    return pl.pallas_call(
        body, name=name, in_specs=[any_spec], out_specs=any_spec,
        out_shape=jax.ShapeDtypeStruct(buf.shape, buf.dtype), input_output_aliases={0: 0},
        scratch_shapes=[pltpu.SemaphoreType.DMA((1,)), pltpu.SemaphoreType.DMA((1,))],
    )(buf)


def _chip_exchange(part, rep, name):
    _, hr, C = part.shape
    rr = rep.shape[0]

    def body(part_ref, rep_ref, land_ref, reps_ref, send_sems, recv_sems, local_sem):
        x, y, c, chips = _place()
        me = 4 * x + 2 * y + c
        own = pltpu.make_async_copy(rep_ref, reps_ref.at[me], local_sem.at[0])
        own.start()
        sent = []
        for j, (cx, cy) in enumerate(chips):
            sent.append(_remote(part_ref.at[2 * cx + cy], land_ref.at[j], send_sems, recv_sems, j, (cx, cy, c)))
        for r in range(1, N_DEV):
            fx, fy, fc = (r >> 2) & 1, (r >> 1) & 1, r & 1
            to = (x ^ fx, y ^ fy, c ^ fc)
            sent.append(_remote(rep_ref, reps_ref.at[me], send_sems, recv_sems, 2 + r, to))
        for cp in sent:
            cp.start()
        for j in range(3):
            _remote(part_ref.at[0], land_ref.at[j], send_sems, recv_sems, j, (x, y, c)).wait_recv()
        for r in range(1, N_DEV):
            fx, fy, fc = (r >> 2) & 1, (r >> 1) & 1, r & 1
            frm = 4 * (x ^ fx) + 2 * (y ^ fy) + (c ^ fc)
            _remote(rep_ref, reps_ref.at[frm], send_sems, recv_sems, 2 + r, (x, y, c)).wait_recv()
        for cp in sent:
            cp.wait_send()
        own.wait()

    any_spec = pl.BlockSpec(memory_space=pl.ANY)
    return pl.pallas_call(
        body, name=name, in_specs=[any_spec, any_spec], out_specs=[any_spec, any_spec],
        out_shape=[jax.ShapeDtypeStruct((3, hr, C), F32), jax.ShapeDtypeStruct((N_DEV, rr, C), F32)],
        scratch_shapes=[pltpu.SemaphoreType.DMA((10,)), pltpu.SemaphoreType.DMA((10,)), pltpu.SemaphoreType.DMA((1,))],
    )(part, rep)


def _adamw_math(w, g, m, v):
    m = ADAM_B1 * m + (1.0 - ADAM_B1) * g
    v = ADAM_B2 * v + (1.0 - ADAM_B2) * (g * g)
    m_hat = m / (1.0 - ADAM_B1 ** ADAM_STEP)
    v_hat = v / (1.0 - ADAM_B2 ** ADAM_STEP)
    delta = -ADAM_LR * (m_hat / (jnp.sqrt(v_hat) + ADAM_EPS) + ADAM_WD * w)
    return delta, m, v


def _add_halves(buf, got, core, name, *, tr):
    n, hr, C = got.shape
    nb = hr // tr

    def body(core_ref, a_ref, b_ref, o_ref):
        o_ref[...] = a_ref[...] + b_ref[...]

    blk = pl.BlockSpec((1, tr, C), lambda s, i, core_ref: (s, i, 0))
    grid_spec = pltpu.PrefetchScalarGridSpec(
        num_scalar_prefetch=1, grid=(n, nb),
        in_specs=[pl.BlockSpec((1, tr, C), lambda s, i, core_ref: (s, core_ref[0] * nb + i, 0)), blk], out_specs=blk)
    return pl.pallas_call(
        body, name=name, grid_spec=grid_spec, out_shape=jax.ShapeDtypeStruct((n, hr, C), F32),
        compiler_params=_cparams("parallel", "parallel"),
    )(core, buf, got)


def _sum_adamw_half(part, landed, w, m, v, place, name, *, tr):
    n = landed.shape[0]
    _, hr, C = part.shape
    nb = hr // tr

    def body(place_ref, own_ref, land_ref, w_ref, m_ref, v_ref, o_ref):
        g = own_ref[0]
        for s in range(n):
            g = g + land_ref[s]
        delta, m_new, v_new = _adamw_math(w_ref[...], g, m_ref[...], v_ref[...])
        o_ref[0] = g
        o_ref[1] = delta
        o_ref[2] = m_new
        o_ref[3] = v_new

    half = pl.BlockSpec((tr, C), lambda i, place_ref: (place_ref[0] * nb + i, 0))
    grid_spec = pltpu.PrefetchScalarGridSpec(
        num_scalar_prefetch=1, grid=(nb,),
        in_specs=[pl.BlockSpec((1, tr, C), lambda i, place_ref: (place_ref[1], i, 0)),
                  pl.BlockSpec((n, tr, C), lambda i, place_ref: (0, i, 0)), half, half, half],
        out_specs=pl.BlockSpec((4, tr, C), lambda i, place_ref: (0, place_ref[0] * nb + i, 0)))
    return pl.pallas_call(
        body, name=name, grid_spec=grid_spec, out_shape=jax.ShapeDtypeStruct((4, 2 * hr, C), F32),
        compiler_params=_cparams("parallel"),
    )(place, part, landed, w, m, v)


def _sum_adamw(own, landed, w, m, v, name, *, tr):
    n = landed.shape[0]
    hr, C = own.shape

    def body(own_ref, land_ref, w_ref, m_ref, v_ref, o_ref):
        g = own_ref[...]
        for s in range(n):
            g = g + land_ref[s]
        delta, m_new, v_new = _adamw_math(w_ref[...], g, m_ref[...], v_ref[...])
        o_ref[0] = g
        o_ref[1] = delta
        o_ref[2] = m_new
        o_ref[3] = v_new

    blk = pl.BlockSpec((tr, C), lambda i: (i, 0))
    return pl.pallas_call(
        body, name=name, grid=(hr // tr,),
        in_specs=[blk, pl.BlockSpec((n, tr, C), lambda i: (0, i, 0)), blk, blk, blk],
        out_specs=pl.BlockSpec((4, tr, C), lambda i: (0, i, 0)),
        out_shape=jax.ShapeDtypeStruct((4, hr, C), F32), compiler_params=_cparams("parallel"),
    )(own, landed, w, m, v)


def _rows_of(shape):
    n = 1
    for d in shape:
        n *= d
    return -(-n // PACK_COLS)


def _pack(arrays, total_rows, dtype):
    parts, used = [], 0
    for a in arrays:
        flat = a.reshape(-1).astype(dtype)
        fill = _rows_of(a.shape) * PACK_COLS - flat.shape[0]
        parts += [flat] + ([jnp.zeros((fill,), dtype)] if fill else [])
        used += _rows_of(a.shape)
    if total_rows > used:
        parts.append(jnp.zeros(((total_rows - used) * PACK_COLS,), dtype))
    return jnp.concatenate(parts).reshape(total_rows, PACK_COLS)


def _unpack(buf, shapes):
    lead = buf.shape[:-2]
    out, r = [], 0
    for shp in shapes:
        n = 1
        for d in shp:
            n *= d
        rows = _rows_of(shp)
        piece = buf[..., r:r + rows, :].reshape(lead + (rows * PACK_COLS,))[..., :n]
        out.append(piece.reshape(lead + tuple(shp)))
        r += rows
    return out


def _join_shards(stacked, axis):
    return jnp.concatenate([stacked[s] for s in range(N_CHIPS)], axis=axis)


def _shard_of(full, axis, chip):
    width = full.shape[axis] // N_CHIPS
    return lax.slice_in_dim(full, chip * width, (chip + 1) * width, axis=axis)


def _local_step(h0, tgt, W, *, seq, tm):
    Lp, D = h0.shape
    H = D // HEAD_DIM
    F2 = W["ffn_w_in"].shape[-1]
    F = F2 // 2
    te = tm // 2
    nq = Lp // tm
    cap = 1408
    tD, tF, tF2 = _pick(D, cap), _pick(F, cap), _pick(F2, cap)
    t2D = _pick(2 * D, cap)
    tcn = _pick(F, cap)

    def vec(a):
        return a.reshape(1, -1)

    ln_g, ln_b = W["ln_g"], W["ln_b"]
    wf_pad = jnp.pad(W["w_f"], ((0, 0), (0, LANES - H)))
    bf_pad = jnp.pad(W["b_f"], (0, LANES - H)).reshape(1, LANES)

    def ffn_fwd(hb, l, tag):
        u = _mm(hb, W["ffn_w_in"][l], "nn", F32, f"ffn{tag}_up", tm=tm, tn=tF2, tk=tD)
        act = _conv_glu_fwd(u, W["ffn_conv_w"][l], vec(W["ffn_conv_b"][l]), f"ffn{tag}_glu", tm=te, tn=tcn)
        y = _mm(act, W["ffn_w_out"][l], "nn", F32, f"ffn{tag}_down", tm=tm, tn=tD, tk=tF)
        return u, act, y

    def ffn_bwd(dzb, hb, u, act, l, tag):
        dact = _mm(dzb, W["ffn_w_out"][l], "nt", F32, f"ffn{tag}_dact", tm=tm, tn=tF, tk=tD)
        dw_out = _mm(act, dzb, "tn", F32, f"ffn{tag}_dwout", tm=tF, tn=tD, tk=tm)
        dua, dug, dwa, dwg, dba, dbg = _conv_glu_bwd(u, dact, W["ffn_conv_w"][l], vec(W["ffn_conv_b"][l]),
                                                     f"ffn{tag}_dglu", tm=te, tn=tcn)
        du = jnp.concatenate([dua, dug], axis=1)
        dcw = jnp.concatenate([dwa, dwg], axis=1)
        dcb = jnp.concatenate([dba, dbg], axis=1)
        dh = _mm(du, W["ffn_w_in"][l], "nt", F32, f"ffn{tag}_dh", tm=tm, tn=tD, tk=tF2)
        dw_in = _mm(hb, du, "tn", F32, f"ffn{tag}_dwin", tm=tD, tn=tF2, tk=tm)
        return dh, dw_in, dw_out, dcw, dcb[0]

    diffb, mixpre, h1, h1b, xh1, rs1 = _pool_ln_fwd(h0, W["pool_w"][0], W["pool_scale"], vec(ln_g[0, 0]),
                                                    vec(ln_b[0, 0]), "pool_ln_fwd", tm=te)
    u0, act0, y0 = ffn_fwd(h1b, 0, "0")
    h2, h2b, xh2, rs2 = _ln_fwd(h1, y0, vec(ln_g[0, 1]), vec(ln_b[0, 1]), "ln01_fwd", tm=te)

    kvb = _mm(h2b, W["w_kv"], "nn", BF16, "kv_proj", tm=tm, tn=t2D, tk=tD)
    qb = _mm(h2b, W["w_q"][0], "nn", BF16, "q_proj", tm=tm, tn=tD, tk=tD, scale=HEAD_DIM ** -0.5)
    pre = _mm(h2b, wf_pad, "nn", F32, "f_proj", tm=tm, tn=LANES, tk=tD)
    c = _logf_cumsum(pre, bf_pad, "logf_cumsum", tm=tm)

    def heads(a):
        return a.reshape(Lp, H, HEAD_DIM).transpose(1, 0, 2)

    def tokens(a):
        return a.transpose(1, 0, 2).reshape(Lp, D)

    qh, kh, vh = heads(qb), heads(kvb[:, :D]), heads(kvb[:, D:])
    crow4 = c[:, :H].T.reshape(H, nq, 1, tm)
    oh, lse4 = _attn_fwd(qh, kh.reshape(H, nq, tm, HEAD_DIM), vh.reshape(H, nq, tm, HEAD_DIM), crow4,
                         "attn_fwd", tq=tm)
    o_tok = tokens(oh)
    ob = o_tok.astype(BF16)
    y_attn = _mm(ob, W["w_o"][0], "nn", F32, "o_proj", tm=tm, tn=tD, tk=tD)
    h3, h3b, xh3, rs3 = _ln_fwd(h2, y_attn, vec(ln_g[1, 0]), vec(ln_b[1, 0]), "ln10_fwd", tm=te)
    u1, act1, y1 = ffn_fwd(h3b, 1, "1")
    h4, _, xh4, rs4 = _ln_fwd(h3, y1, vec(ln_g[1, 1]), vec(ln_b[1, 1]), "ln11_fwd", tm=te)
    dy, loss = _loss_head(h4, tgt, "loss_head", tm=te, row_lo=N_META, row_hi=N_META + seq)

    dz4, dz4b, dg11, db11 = _ln_bwd([dy], [1.0], xh4, rs4, vec(ln_g[1, 1]), "ln11_bwd", tm=te)
    dh3, dw_in1, dw_out1, dcw1, dcb1 = ffn_bwd(dz4b, h3b, u1, act1, 1, "1")
    dz3, dz3b, dg10, db10 = _ln_bwd([dz4, dh3], [ALPHA, 1.0], xh3, rs3, vec(ln_g[1, 0]), "ln10_bwd", tm=te)

    do_tok = _mm(dz3b, W["w_o"][0], "nt", F32, "o_proj_dx", tm=tm, tn=tD, tk=tD)
    dw_o = _mm(ob, dz3b, "tn", F32, "o_proj_dw", tm=tD, tn=tD, tk=tm)
    delta = _attn_delta(do_tok, o_tok, "attn_delta", tm=te, n_heads=H)
    doh4 = heads(do_tok.astype(BF16)).reshape(H, nq, tm, HEAD_DIM)
    dqh4, dkh, dvh, dcs, dcq = _attn_bwd(qh.reshape(H, nq, tm, HEAD_DIM), doh4, kh, vh,
                                    lse4, delta[:, :H].T.reshape(H, nq, 1, tm), crow4,
                                    "attn_bwd", tq=tm)
    dqb = tokens(dqh4.reshape(H, Lp, HEAD_DIM)).astype(BF16)
    dkvb = jnp.concatenate([tokens(dkh), tokens(dvh)], axis=1).astype(BF16)
    dc_keys = jnp.pad(dcs.reshape(H, Lp).T, ((0, 0), (0, LANES - H)))
    dc_queries = jnp.pad(dcq.reshape(H, Lp).T, ((0, 0), (0, LANES - H)))
    dpreb, dbf = _logf_bwd(dc_keys, dc_queries, pre, bf_pad, "logf_bwd", tm=tm)

    qs = HEAD_DIM ** -0.5
    dw_q = _mm(h2b, dqb, "tn", F32, "q_proj_dw", tm=tD, tn=tD, tk=tm, scale=qs)
    dw_kv = _mm(h2b, dkvb, "tn", F32, "kv_proj_dw", tm=tD, tn=t2D, tk=tm)
    dw_f = _mm(h2b, dpreb, "tn", F32, "f_proj_dw", tm=tD, tn=LANES, tk=tm)[:, :H]
    dh2 = _mm(dqb, W["w_q"][0], "nt", F32, "q_proj_dx", tm=tm, tn=tD, tk=tD, scale=qs)
    dh2 = _mm(dkvb, W["w_kv"], "nt", F32, "kv_proj_dx", tm=tm, tn=tD, tk=t2D, add=dh2)
    dh2 = _mm(dpreb, wf_pad, "nt", F32, "f_proj_dx", tm=tm, tn=tD, tk=LANES, add=dh2)
    dz2, dz2b, dg01, db01 = _ln_bwd([dz3, dh2], [ALPHA, 1.0], xh2, rs2, vec(ln_g[0, 1]), "ln01_bwd", tm=te)

    dh1, dw_in0, dw_out0, dcw0, dcb0 = ffn_bwd(dz2b, h1b, u0, act0, 0, "0")
    dz1, _, dg00, db00 = _ln_bwd([dz2, dh1], [ALPHA, 1.0], xh1, rs1, vec(ln_g[0, 0]), "ln00_bwd", tm=te)
    dh0, dmb, dscale = _pool_bwd(dz1, mixpre, W["pool_w"][0], W["pool_scale"], "pool_bwd", tm=te)
    dw_pool = _pool_dw(diffb, dmb, "pool_dw", tk=tm)

    grads = {
        "meta": dh0[:N_META],
        "pool_w": dw_pool[None],
        "pool_scale": dscale,
        "w_kv": dw_kv,
        "w_f": dw_f,
        "b_f": dbf[0, :H],
        "w_q": dw_q[None],
        "w_o": dw_o[None],
        "ffn_w_in": jnp.stack([dw_in0, dw_in1]),
        "ffn_conv_w": jnp.stack([dcw0, dcw1]),
        "ffn_conv_b": jnp.stack([dcb0, dcb1]),
        "ffn_w_out": jnp.stack([dw_out0, dw_out1]),
        "ln_g": jnp.stack([jnp.stack([dg00[0], dg01[0]]), jnp.stack([dg10[0], dg11[0]])]),
        "ln_b": jnp.stack([jnp.stack([db00[0], db01[0]]), jnp.stack([db10[0], db11[0]])]),
    }
    return loss, dh0, grads


def _row_tile(length):
    return 640 if length >= 4096 else 128


def kernel(x, meta, pool_w, pool_scale, w_kv, w_f, b_f, w_q, w_o, ffn_w_in, ffn_conv_w, ffn_conv_b, ffn_w_out, ln_g, ln_b, loss_target, m_meta, m_pool_w, m_pool_scale, m_w_kv, m_w_f, m_b_f, m_w_q, m_w_o, m_ffn_w_in, m_ffn_conv_w, m_ffn_conv_b, m_ffn_w_out, m_ln_g, m_ln_b, v_meta, v_pool_w, v_pool_scale, v_w_kv, v_w_f, v_b_f, v_w_q, v_w_o, v_ffn_w_in, v_ffn_conv_w, v_ffn_conv_b, v_ffn_w_out, v_ln_g, v_ln_b):
    weights = dict(meta=meta, pool_w=pool_w, pool_scale=pool_scale, w_kv=w_kv, w_f=w_f, b_f=b_f, w_q=w_q, w_o=w_o,
                   ffn_w_in=ffn_w_in, ffn_conv_w=ffn_conv_w, ffn_conv_b=ffn_conv_b, ffn_w_out=ffn_w_out,
                   ln_g=ln_g, ln_b=ln_b)
    mom1 = dict(meta=m_meta, pool_w=m_pool_w, pool_scale=m_pool_scale, w_kv=m_w_kv, w_f=m_w_f, b_f=m_b_f, w_q=m_w_q,
                w_o=m_w_o, ffn_w_in=m_ffn_w_in, ffn_conv_w=m_ffn_conv_w, ffn_conv_b=m_ffn_conv_b,
                ffn_w_out=m_ffn_w_out, ln_g=m_ln_g, ln_b=m_ln_b)
    mom2 = dict(meta=v_meta, pool_w=v_pool_w, pool_scale=v_pool_scale, w_kv=v_w_kv, w_f=v_w_f, b_f=v_b_f, w_q=v_w_q,
                w_o=v_w_o, ffn_w_in=v_ffn_w_in, ffn_conv_w=v_ffn_conv_w, ffn_conv_b=v_ffn_conv_b,
                ffn_w_out=v_ffn_w_out, ln_g=v_ln_g, ln_b=v_ln_b)
    _, seq, D = x.shape
    L = N_META + seq
    tm = _row_tile(L)
    Lp = _round_up(L, tm)
    c_idx = lax.axis_index("c")
    chip = 2 * lax.axis_index("x") + lax.axis_index("y")

    shard_shapes = {n: weights[n].shape for n in SHARDED}
    rows_b = _round_up(sum(_rows_of(shard_shapes[n]) for n in MATMUL_WEIGHTS), 32)
    rows_f = _round_up(sum(_rows_of(shard_shapes[n]) for n in VECTOR_WEIGHTS), SUBLANES)
    wb = _pack([weights[n] for n in MATMUL_WEIGHTS], rows_b, BF16)
    wf = _pack([weights[n] for n in VECTOR_WEIGHTS], rows_f, F32)
    gb, gf = _all_gather_weights(wb, wf, "weights_all_gather")
    gb = lax.dynamic_update_index_in_dim(gb, wb, chip, axis=0)
    gf = lax.dynamic_update_index_in_dim(gf, wf, chip, axis=0)
    full = {}
    for names, buf in ((MATMUL_WEIGHTS, gb), (VECTOR_WEIGHTS, gf)):
        for n, stacked in zip(names, _unpack(buf, [shard_shapes[n] for n in names])):
            full[n] = _join_shards(stacked, SHARD_AXIS[n])
    full["b_f"] = b_f
    full["ffn_conv_b"] = ffn_conv_b

    pad = jnp.zeros((Lp - L, D), F32)
    h0 = jnp.concatenate([full["meta"], x[0], pad], axis=0)
    tgt = jnp.concatenate([jnp.zeros((N_META, D), F32), loss_target[0], pad], axis=0)
    loss, dh0, grads = _local_step(h0, tgt, full, seq=seq, tm=tm)
    loss = lax.psum(loss[0, 0], AXES)
    grad_x = dh0[N_META:L][None]

    rows = _round_up(sum(_rows_of(shard_shapes[n]) for n in SHARDED), 2 * LANES)
    hr = rows // 2
    shapes = [shard_shapes[n] for n in SHARDED]
    gpack = jnp.stack([_pack([_shard_of(grads[n], SHARD_AXIS[n], s) for n in SHARDED], rows, F32)
                       for s in range(N_CHIPS)])
    core = c_idx.astype(jnp.int32).reshape(1)
    got = _send_half_to_sibling(gpack, "grads_to_sibling")
    part = _add_halves(gpack, got, core, "grads_chip_sum", tr=LANES)

    rep_shapes = [weights[n].shape for n in REPLICATED]
    rows_r = _round_up(sum(_rows_of(s) for s in rep_shapes), SUBLANES)
    rep = _pack([grads[n] for n in REPLICATED], rows_r, F32)
    landed, reps = _chip_exchange(part, rep, "grads_chip_exchange")

    def packed(d):
        return _pack([d[n] for n in SHARDED], rows, F32)

    place = jnp.stack([c_idx, chip]).astype(jnp.int32)
    mine = _sum_adamw_half(part, landed, packed(weights), packed(mom1), packed(mom2), place, "adamw_sharded", tr=LANES)
    res = _join_with_sibling(mine, "results_to_sibling")
    sharded_out = _unpack(res, shapes)

    def packr(d):
        return _pack([d[n] for n in REPLICATED], rows_r, F32)

    res_r = _sum_adamw(reps[0], reps[1:], packr(weights), packr(mom1), packr(mom2), "adamw_replicated", tr=rows_r)
    rep_out = _unpack(res_r, rep_shapes)

    out = {n: a for n, a in zip(SHARDED, sharded_out)}
    out.update({n: a for n, a in zip(REPLICATED, rep_out)})
    result = [loss, grad_x]
    for k in range(4):
        result += [out[n][k] for n in WEIGHT_ORDER]
    return tuple(result)
```

```python
import functools

import jax
import jax.numpy as jnp
from jax import lax
from jax.experimental import pallas as pl
from jax.experimental.pallas import tpu as pltpu

N_META = 16
POOL_WINDOWS = (2, 4, 8, 16)
MAX_WINDOW = max(POOL_WINDOWS)
N_GROUPS = len(POOL_WINDOWS)
HEAD_DIM = 64
DEPTH = 2
CONV_WIDTH = 3
ALPHA = (2.0 * DEPTH) ** 0.25
LN_EPS = 1e-5
NEG_INF = -1e30
ADAM_LR = 0.001
ADAM_B1 = 0.9
ADAM_B2 = 0.999
ADAM_EPS = 1e-08
ADAM_WD = 0.01
ADAM_STEP = 10

F32 = jnp.float32
BF16 = jnp.bfloat16
ATTN_STRIP = 32
ATTN_HEADS = 2
GLU_STRIP = 16
LANES = 128
SUBLANES = 8
PACK_COLS = 1024
VMEM_LIMIT = 56 * 1024 * 1024
AXES = ("x", "y", "c")
MESH = pl.DeviceIdType.MESH

NN = (((1,), (0,)), ((), ()))
NT = (((1,), (1,)), ((), ()))
TN = (((0,), (0,)), ((), ()))

SHARD_AXIS = {"meta": 1, "pool_w": 2, "pool_scale": 1, "w_kv": 1, "w_f": 0, "w_q": 1, "w_o": 1,
              "ffn_w_in": 2, "ffn_conv_w": 2, "ffn_w_out": 1, "ln_g": 2, "ln_b": 2}
SHARDED = ("meta", "pool_w", "pool_scale", "w_kv", "w_f", "w_q", "w_o", "ffn_w_in", "ffn_conv_w",
           "ffn_w_out", "ln_g", "ln_b")
REPLICATED = ("b_f", "ffn_conv_b")
MATMUL_WEIGHTS = ("pool_w", "w_kv", "w_f", "w_q", "w_o", "ffn_w_in", "ffn_w_out")
VECTOR_WEIGHTS = ("meta", "pool_scale", "ffn_conv_w", "ln_g", "ln_b")
WEIGHT_ORDER = ("meta", "pool_w", "pool_scale", "w_kv", "w_f", "b_f", "w_q", "w_o", "ffn_w_in",
                "ffn_conv_w", "ffn_conv_b", "ffn_w_out", "ln_g", "ln_b")
BIG_SHARDED = (("w_kv", "CH"), ("w_q", "CH"), ("w_o", "CH"), ("ffn_w_in", "HC"), ("ffn_w_out", "HC"))
SMALL_SHARDED = ("meta", "pool_w", "pool_scale", "w_f", "ffn_conv_w", "ln_g", "ln_b")
ELEMENTWISE_BLOCK_BYTES = 3 * 512 * 1024
N_CHIPS = 4
N_DEV = 8


def _cparams(*sem):
    return pltpu.CompilerParams(dimension_semantics=sem, vmem_limit_bytes=VMEM_LIMIT)


def _round_up(n, m):
    return (n + m - 1) // m * m


def _pick(n, cap):
    if n <= cap:
        return n
    best = 0
    for t in range(LANES, cap + 1, LANES):
        if n % t == 0:
            best = t
    assert best, (n, cap)
    return best


def _mm(a, b, mode, out_dtype, name, *, tm, tn, tk, scale=None, add=None, chips=False, layer=None, into=None):
    if mode == "nn":
        (M, K), N = a.shape, b.shape[1]
    elif mode == "nt":
        (M, K), N = a.shape, b.shape[0]
    else:
        (K, M), N = a.shape, b.shape[1]
    assert M % tm == 0 and N % tn == 0 and K % tk == 0, (name, M, N, K, tm, tn, tk)
    nk = K // tk
    dn = {"nn": NN, "nt": NT, "tn": TN}[mode]
    has_add = add is not None
    has_into = into is not None
    assert not (has_add and (chips or layer is not None))

    def body(*refs):
        a_ref, b_ref = refs[0], refs[1]
        add_ref = refs[2] if has_add else None
        o_ref = refs[2 + has_add + has_into]
        acc_ref = refs[-1] if nk > 1 else None
        k = pl.program_id(2)
        part = lax.dot_general(a_ref[...], b_ref[...], dn, preferred_element_type=F32)

        def finish(r):
            if scale is not None:
                r = r * scale
            if has_add:
                r = r + add_ref[...]
            o_ref[...] = r.astype(out_dtype).reshape(o_ref.shape)

        if nk == 1:
            finish(part)
        else:
            @pl.when(k == 0)
            def _():
                acc_ref[...] = part

            @pl.when(k > 0)
            def _():
                acc_ref[...] += part

            @pl.when(k == nk - 1)
            def _():
                finish(acc_ref[...])

    if mode == "nn":
        a_spec = pl.BlockSpec((tm, tk), lambda j, i, k: (i, k))
        b_spec = pl.BlockSpec((tk, tn), lambda j, i, k: (k, j))
    elif mode == "nt":
        a_spec = pl.BlockSpec((tm, tk), lambda j, i, k: (i, k))
        b_spec = pl.BlockSpec((tn, tk), lambda j, i, k: (j, k))
    else:
        a_spec = pl.BlockSpec((tk, tm), lambda j, i, k: (k, i))
        b_spec = pl.BlockSpec((tk, tn), lambda j, i, k: (k, j))
    out_dims, blk = (M, N), (tm, tn)
    if chips:
        per_chip = N // N_CHIPS // tn
        assert per_chip * tn * N_CHIPS == N, (name, N, tn)
        out_dims, blk = (N_CHIPS, M, N // N_CHIPS), (1, tm, tn)
        where = lambda j, i: (j // per_chip, i, j % per_chip)
    else:
        where = lambda j, i: (i, j)
    if layer is not None:
        out_dims, blk = (DEPTH,) + out_dims, (1,) + blk
        o_spec = pl.BlockSpec(blk, lambda j, i, k: (layer,) + where(j, i))
    else:
        o_spec = pl.BlockSpec(blk, lambda j, i, k: where(j, i))
    in_specs = [a_spec, b_spec] + ([o_spec] if has_add else []) + ([pl.BlockSpec(memory_space=pl.ANY)] if has_into else [])
    args = (a, b) + ((add,) if has_add else ()) + ((into,) if has_into else ())
    return pl.pallas_call(
        body, name=name, grid=(N // tn, M // tm, nk),
        in_specs=in_specs, out_specs=o_spec,
        out_shape=jax.ShapeDtypeStruct(out_dims, out_dtype),
        input_output_aliases={len(args) - 1: 0} if has_into else {},
        scratch_shapes=[pltpu.VMEM((tm, tn), F32)] if nk > 1 else [],
        compiler_params=_cparams("parallel", "parallel", "arbitrary"),
    )(*args)


def _ln_math(z, g, b):
    mu = jnp.mean(z, axis=-1, keepdims=True)
    zc = z - mu
    var = jnp.mean(zc * zc, axis=-1, keepdims=True)
    rstd = lax.rsqrt(var + LN_EPS)
    xh = zc * rstd
    return xh * g + b, xh, rstd


def _pool_ln_fwd(h0, pw, ps, g, b, name, *, tm):
    Lp, D = h0.shape
    G = D // N_GROUPS
    halo_blocks = tm // MAX_WINDOW

    def body(x_ref, halo_ref, pw_ref, ps_ref, g_ref, b_ref,
             diff_ref, mix_ref, h_ref, hb_ref, xh_ref, rs_ref, ext_ref):
        i = pl.program_id(0)
        ext_ref[0:MAX_WINDOW, :] = jnp.where(i == 0, 0.0, halo_ref[...])
        ext_ref[MAX_WINDOW:MAX_WINDOW + tm, :] = x_ref[...]
        t1 = (i * tm + 1 + lax.broadcasted_iota(jnp.int32, (tm, 1), 0)).astype(F32)
        for gi, w in enumerate(POOL_WINDOWS):
            lo, hi = gi * G, (gi + 1) * G
            xg = x_ref[:, lo:hi]
            win = xg
            for j in range(1, w):
                win = win + ext_ref[MAX_WINDOW - j:MAX_WINDOW - j + tm, lo:hi]
            d = (win / jnp.minimum(t1, float(w)) - xg).astype(BF16)
            diff_ref[:, lo:hi] = d
            mix_ref[:, lo:hi] = jnp.dot(d, pw_ref[gi], preferred_element_type=F32)
        z = ALPHA * x_ref[...] + mix_ref[...] * ps_ref[...]
        h, xh, rstd = _ln_math(z, g_ref[...], b_ref[...])
        h_ref[...] = h
        hb_ref[...] = h.astype(BF16)
        xh_ref[...] = xh
        rs_ref[...] = rstd

    row = pl.BlockSpec((tm, D), lambda i: (i, 0))
    vec = pl.BlockSpec((1, D), lambda i: (0, 0))
    return pl.pallas_call(
        body, name=name, grid=(Lp // tm,),
        in_specs=[row,
                  pl.BlockSpec((MAX_WINDOW, D), lambda i: (jnp.maximum(i * halo_blocks - 1, 0), 0)),
                  pl.BlockSpec((N_GROUPS, G, G), lambda i: (0, 0, 0)), vec, vec, vec],
        out_specs=[row, row, row, row, row, pl.BlockSpec((tm, 1), lambda i: (i, 0))],
        out_shape=[jax.ShapeDtypeStruct((Lp, D), BF16), jax.ShapeDtypeStruct((Lp, D), F32),
                   jax.ShapeDtypeStruct((Lp, D), F32), jax.ShapeDtypeStruct((Lp, D), BF16),
                   jax.ShapeDtypeStruct((Lp, D), F32), jax.ShapeDtypeStruct((Lp, 1), F32)],
        scratch_shapes=[pltpu.VMEM((tm + MAX_WINDOW, D), F32)],
        compiler_params=_cparams("parallel"),
    )(h0, h0, pw, ps, g, b)


def _pool_bwd(dz, mixpre, pw, ps, name, *, tm):
    Lp, D = dz.shape
    G = D // N_GROUPS
    halo_blocks = tm // MAX_WINDOW
    n_halo = Lp // MAX_WINDOW
    ni = Lp // tm
    R = tm + MAX_WINDOW

    def body(dz_ref, halo_ref, mix_ref, pw_ref, ps_ref, dh_ref, dmb_ref, dsc_ref, ext_ref, dp_ref):
        i = pl.program_id(0)
        ext_ref[0:tm, :] = dz_ref[...]
        ext_ref[tm:R, :] = jnp.where(i == ni - 1, 0.0, halo_ref[...])
        dmix = (ext_ref[...] * ps_ref[...]).astype(BF16)
        dmb_ref[...] = dmix[0:tm]

        @pl.when(i == 0)
        def _():
            dsc_ref[...] = jnp.zeros_like(dsc_ref)

        dsc_ref[...] += jnp.sum(dz_ref[...] * mix_ref[...], axis=0, keepdims=True)
        t1 = (i * tm + 1 + lax.broadcasted_iota(jnp.int32, (R, 1), 0)).astype(F32)
        for gi, w in enumerate(POOL_WINDOWS):
            lo, hi = gi * G, (gi + 1) * G
            dd = lax.dot_general(dmix[:, lo:hi], pw_ref[gi], NT, preferred_element_type=F32)
            dp_ref[:, lo:hi] = dd / jnp.minimum(t1, float(w))
            back = dp_ref[0:tm, lo:hi]
            for j in range(1, w):
                back = back + dp_ref[j:j + tm, lo:hi]
            dh_ref[:, lo:hi] = ALPHA * dz_ref[:, lo:hi] - dd[0:tm] + back

    row = pl.BlockSpec((tm, D), lambda i: (i, 0))
    vec = pl.BlockSpec((1, D), lambda i: (0, 0))
    return pl.pallas_call(
        body, name=name, grid=(ni,),
        in_specs=[row,
                  pl.BlockSpec((MAX_WINDOW, D), lambda i: (jnp.minimum((i + 1) * halo_blocks, n_halo - 1), 0)),
                  row, pl.BlockSpec((N_GROUPS, G, G), lambda i: (0, 0, 0)), vec],
        out_specs=[row, row, vec],
        out_shape=[jax.ShapeDtypeStruct((Lp, D), F32), jax.ShapeDtypeStruct((Lp, D), BF16),
                   jax.ShapeDtypeStruct((1, D), F32)],
        scratch_shapes=[pltpu.VMEM((R, D), F32), pltpu.VMEM((R, D), F32)],
        compiler_params=_cparams("arbitrary"),
    )(dz, dz, mixpre, pw, ps)


def _pool_dw(diffb, dmb, name, *, tk):
    Lp, D = diffb.shape
    G = D // N_GROUPS

    def body(a_ref, b_ref, o_ref):
        @pl.when(pl.program_id(1) == 0)
        def _():
            o_ref[...] = jnp.zeros_like(o_ref)

        o_ref[0] += lax.dot_general(a_ref[...], b_ref[...], TN, preferred_element_type=F32)

    blk = pl.BlockSpec((tk, G), lambda g, k: (k, g))
    return pl.pallas_call(
        body, name=name, grid=(N_GROUPS, Lp // tk),
        in_specs=[blk, blk], out_specs=pl.BlockSpec((1, G, G), lambda g, k: (g, 0, 0)),
        out_shape=jax.ShapeDtypeStruct((N_GROUPS, G, G), F32),
        compiler_params=_cparams("parallel", "arbitrary"),
    )(diffb, dmb)


def _ln_fwd(resid, y, g, b, name, *, tm):
    Lp, D = resid.shape

    def body(r_ref, y_ref, g_ref, b_ref, h_ref, hb_ref, xh_ref, rs_ref):
        h, xh, rstd = _ln_math(ALPHA * r_ref[...] + y_ref[...], g_ref[...], b_ref[...])
        h_ref[...] = h
        hb_ref[...] = h.astype(BF16)
        xh_ref[...] = xh
        rs_ref[...] = rstd

    row = pl.BlockSpec((tm, D), lambda i: (i, 0))
    vec = pl.BlockSpec((1, D), lambda i: (0, 0))
    return pl.pallas_call(
        body, name=name, grid=(Lp // tm,),
        in_specs=[row, row, vec, vec],
        out_specs=[row, row, row, pl.BlockSpec((tm, 1), lambda i: (i, 0))],
        out_shape=[jax.ShapeDtypeStruct((Lp, D), F32), jax.ShapeDtypeStruct((Lp, D), BF16),
                   jax.ShapeDtypeStruct((Lp, D), F32), jax.ShapeDtypeStruct((Lp, 1), F32)],
        compiler_params=_cparams("parallel"),
    )(resid, y, g, b)


def _ln_bwd(parts, coefs, xh, rs, g, name, *, tm):
    Lp, D = xh.shape
    n = len(parts)

    def body(*refs):
        part_refs = refs[:n]
        xh_ref, rs_ref, g_ref = refs[n:n + 3]
        dz_ref, dzb_ref, dg_ref, db_ref = refs[n + 3:]
        dy = part_refs[0][...] if coefs[0] == 1.0 else coefs[0] * part_refs[0][...]
        for c, r in zip(coefs[1:], part_refs[1:]):
            dy = dy + (r[...] if c == 1.0 else c * r[...])
        x = xh_ref[...]
        dxh = dy * g_ref[...]
        m1 = jnp.mean(dxh, axis=-1, keepdims=True)
        m2 = jnp.mean(dxh * x, axis=-1, keepdims=True)
        dz = rs_ref[...] * (dxh - m1 - x * m2)
        dz_ref[...] = dz
        dzb_ref[...] = dz.astype(BF16)

        @pl.when(pl.program_id(0) == 0)
        def _():
            dg_ref[...] = jnp.zeros_like(dg_ref)
            db_ref[...] = jnp.zeros_like(db_ref)

        dg_ref[...] += jnp.sum(dy * x, axis=0, keepdims=True)
        db_ref[...] += jnp.sum(dy, axis=0, keepdims=True)

    row = pl.BlockSpec((tm, D), lambda i: (i, 0))
    vec = pl.BlockSpec((1, D), lambda i: (0, 0))
    return pl.pallas_call(
        body, name=name, grid=(Lp // tm,),
        in_specs=[row] * n + [row, pl.BlockSpec((tm, 1), lambda i: (i, 0)), vec],
        out_specs=[row, row, vec, vec],
        out_shape=[jax.ShapeDtypeStruct((Lp, D), F32), jax.ShapeDtypeStruct((Lp, D), BF16),
                   jax.ShapeDtypeStruct((1, D), F32), jax.ShapeDtypeStruct((1, D), F32)],
        compiler_params=_cparams("arbitrary"),
    )(*parts, xh, rs, g)


def _loss_head(h, tgt, name, *, tm, row_lo, row_hi):
    Lp, D = h.shape

    def body(h_ref, t_ref, dy_ref, loss_ref):
        i = pl.program_id(0)
        r = i * tm + lax.broadcasted_iota(jnp.int32, (tm, 1), 0)
        valid = (r >= row_lo) & (r < row_hi)
        e = jnp.where(valid, h_ref[...] - t_ref[...], 0.0)
        dy_ref[...] = e * (1.0 / D)

        @pl.when(i == 0)
        def _():
            loss_ref[...] = jnp.zeros_like(loss_ref)

        loss_ref[...] += 0.5 * jnp.sum(jnp.mean(e * e, axis=-1, keepdims=True), axis=0, keepdims=True)

    row = pl.BlockSpec((tm, D), lambda i: (i, 0))
    return pl.pallas_call(
        body, name=name, grid=(Lp // tm,),
        in_specs=[row, row], out_specs=[row, pl.BlockSpec((1, 1), lambda i: (0, 0))],
        out_shape=[jax.ShapeDtypeStruct((Lp, D), F32), jax.ShapeDtypeStruct((1, 1), F32)],
        compiler_params=_cparams("arbitrary"),
    )(h, tgt)


def _shift_rows_down(cur, prev, s, sub):
    return jnp.where(sub >= s, pltpu.roll(cur, s, 0), pltpu.roll(prev, s, 0))


def _shift_rows_up(cur, nxt, s, sub):
    return jnp.where(sub < SUBLANES - s, pltpu.roll(cur, SUBLANES - s, 0), pltpu.roll(nxt, SUBLANES - s, 0))


def _conv_group(cur, prev, cw_ref, cb_ref, sub):
    taps = [_shift_rows_down(cur, prev, 2, sub), _shift_rows_down(cur, prev, 1, sub), cur]
    c = cb_ref[...] + cw_ref[0:1, :] * taps[0] + cw_ref[1:2, :] * taps[1] + cw_ref[2:3, :] * taps[2]
    return c, taps


def _conv_glu_fwd(u, cw, cb, name, *, tm, tn):
    Lp, F2 = u.shape
    F = F2 // 2
    nj = F // tn
    halo_blocks = tm // SUBLANES
    S8 = SUBLANES
    assert GLU_STRIP == 2 * S8 and tm % GLU_STRIP == 0

    def body(ua_ref, ug_ref, pa_ref, pg_ref, cwa_ref, cwg_ref, cba_ref, cbg_ref, o_ref):
        first = pl.program_id(1) == 0
        sub = lax.broadcasted_iota(jnp.int32, (S8, tn), 0)

        def strip(r, prev_a, prev_g):
            out = []
            for g0 in (0, S8):
                a_cur = ua_ref[pl.ds(r + g0, S8), :]
                g_cur = ug_ref[pl.ds(r + g0, S8), :]
                a, _ = _conv_group(a_cur, prev_a, cwa_ref, cba_ref, sub)
                gate, _ = _conv_group(g_cur, prev_g, cwg_ref, cbg_ref, sub)
                out.append(a * jax.nn.sigmoid(a) * gate)
                prev_a, prev_g = a_cur, g_cur
            o_ref[pl.ds(r, GLU_STRIP), :] = jnp.concatenate(out, axis=0).astype(BF16)

        strip(0, jnp.where(first, 0.0, pa_ref[...]), jnp.where(first, 0.0, pg_ref[...]))

        def step(k, carry):
            r = pl.multiple_of(k * GLU_STRIP, GLU_STRIP)
            before = pl.ds(pl.multiple_of(r - S8, S8), S8)
            strip(r, ua_ref[before, :], ug_ref[before, :])
            return carry

        lax.fori_loop(1, tm // GLU_STRIP, step, 0)

    def prev(off):
        return pl.BlockSpec((SUBLANES, tn), lambda j, i: (jnp.maximum(i * halo_blocks - 1, 0), j + off))

    def cols(rows, off):
        return pl.BlockSpec((rows, tn), lambda j, i: (0, j + off))

    return pl.pallas_call(
        body, name=name, grid=(nj, Lp // tm),
        in_specs=[pl.BlockSpec((tm, tn), lambda j, i: (i, j)), pl.BlockSpec((tm, tn), lambda j, i: (i, j + nj)),
                  prev(0), prev(nj), cols(CONV_WIDTH, 0), cols(CONV_WIDTH, nj), cols(1, 0), cols(1, nj)],
        out_specs=pl.BlockSpec((tm, tn), lambda j, i: (i, j)),
        out_shape=jax.ShapeDtypeStruct((Lp, F), BF16),
        compiler_params=_cparams("parallel", "parallel"),
    )(u, u, u, u, cw, cw, cb, cb)


def _conv_glu_bwd(u, dact, cw, cb, name, *, tm, tn):
    Lp, F2 = u.shape
    F = F2 // 2
    nj = F // tn
    ni = Lp // tm
    halo_blocks = tm // SUBLANES
    n_halo = Lp // SUBLANES
    S8 = SUBLANES
    n_strips = tm // GLU_STRIP
    assert GLU_STRIP == 2 * S8 and tm % GLU_STRIP == 0

    def body(ua_ref, ug_ref, pa_ref, pg_ref, na_ref, ng_ref, da_ref, dn_ref,
             cwa_ref, cwg_ref, cba_ref, cbg_ref,
             dua_ref, dug_ref, dwa_ref, dwg_ref, dba_ref, dbg_ref,
             wacc_a, wacc_g, bacc_a, bacc_g):
        i = pl.program_id(1)
        first, last = i == 0, i == ni - 1
        sub = lax.broadcasted_iota(jnp.int32, (S8, tn), 0)
        for acc in (wacc_a, wacc_g, bacc_a, bacc_g):
            acc[...] = jnp.zeros_like(acc)

        def dconv(a_cur, a_prev, g_cur, g_prev, dact_rows):
            a, taps_a = _conv_group(a_cur, a_prev, cwa_ref, cba_ref, sub)
            gate, taps_g = _conv_group(g_cur, g_prev, cwg_ref, cbg_ref, sub)
            sg = jax.nn.sigmoid(a)
            dca = dact_rows * gate * (sg * (1.0 + a * (1.0 - sg)))
            dcg = dact_rows * (a * sg)
            return dca, dcg, taps_a, taps_g

        def du_group(dc, dc_after, cw_ref):
            return (cw_ref[2:3, :] * dc + cw_ref[1:2, :] * _shift_rows_up(dc, dc_after, 1, sub)
                    + cw_ref[0:1, :] * _shift_rows_up(dc, dc_after, 2, sub))

        def strip(r, a_prev, g_prev, dca_after, dcg_after):
            a0, a1 = ua_ref[pl.ds(r, S8), :], ua_ref[pl.ds(r + S8, S8), :]
            g0, g1 = ug_ref[pl.ds(r, S8), :], ug_ref[pl.ds(r + S8, S8), :]
            dca1, dcg1, ta1, tg1 = dconv(a1, a0, g1, g0, da_ref[pl.ds(r + S8, S8), :])
            dca0, dcg0, ta0, tg0 = dconv(a0, a_prev, g0, g_prev, da_ref[pl.ds(r, S8), :])
            dua_ref[pl.ds(r, GLU_STRIP), :] = jnp.concatenate(
                [du_group(dca0, dca1, cwa_ref), du_group(dca1, dca_after, cwa_ref)], axis=0).astype(BF16)
            dug_ref[pl.ds(r, GLU_STRIP), :] = jnp.concatenate(
                [du_group(dcg0, dcg1, cwg_ref), du_group(dcg1, dcg_after, cwg_ref)], axis=0).astype(BF16)
            for k in range(CONV_WIDTH):
                wacc_a[k] += dca0 * ta0[k] + dca1 * ta1[k]
                wacc_g[k] += dcg0 * tg0[k] + dcg1 * tg1[k]
            bacc_a[...] += dca0 + dca1
            bacc_g[...] += dcg0 + dcg1
            return dca0, dcg0

        tail = pl.ds(tm - S8, S8)
        dca_after, dcg_after, _, _ = dconv(na_ref[...], ua_ref[tail, :], ng_ref[...], ug_ref[tail, :],
                                           jnp.where(last, 0.0, dn_ref[...]))

        def step(t, carry):
            r = pl.multiple_of((n_strips - 1 - t) * GLU_STRIP, GLU_STRIP)
            before = pl.ds(pl.multiple_of(r - S8, S8), S8)
            return strip(r, ua_ref[before, :], ug_ref[before, :], *carry)

        dca_after, dcg_after = lax.fori_loop(0, n_strips - 1, step, (dca_after, dcg_after))
        strip(0, jnp.where(first, 0.0, pa_ref[...]), jnp.where(first, 0.0, pg_ref[...]), dca_after, dcg_after)

        @pl.when(first)
        def _():
            for r in (dwa_ref, dwg_ref, dba_ref, dbg_ref):
                r[...] = jnp.zeros_like(r)

        for wacc, bacc, dw_ref, db_ref in ((wacc_a, bacc_a, dwa_ref, dba_ref), (wacc_g, bacc_g, dwg_ref, dbg_ref)):
            db_ref[...] += jnp.sum(bacc[...], axis=0, keepdims=True)
            for k in range(CONV_WIDTH):
                dw_ref[k:k + 1, :] += jnp.sum(wacc[k], axis=0, keepdims=True)

    def tile(off):
        return pl.BlockSpec((tm, tn), lambda j, i: (i, j + off))

    def prev(off):
        return pl.BlockSpec((S8, tn), lambda j, i: (jnp.maximum(i * halo_blocks - 1, 0), j + off))

    def nxt(off):
        return pl.BlockSpec((S8, tn), lambda j, i: (jnp.minimum((i + 1) * halo_blocks, n_halo - 1), j + off))

    def cols(rows, off):
        return pl.BlockSpec((rows, tn), lambda j, i: (0, j + off))

    return pl.pallas_call(
        body, name=name, grid=(nj, ni),
        in_specs=[tile(0), tile(nj), prev(0), prev(nj), nxt(0), nxt(nj), tile(0), nxt(0),
                  cols(CONV_WIDTH, 0), cols(CONV_WIDTH, nj), cols(1, 0), cols(1, nj)],
        out_specs=[tile(0), tile(0), cols(CONV_WIDTH, 0), cols(CONV_WIDTH, 0), cols(1, 0), cols(1, 0)],
        out_shape=[jax.ShapeDtypeStruct((Lp, F), BF16), jax.ShapeDtypeStruct((Lp, F), BF16),
                   jax.ShapeDtypeStruct((CONV_WIDTH, F), F32), jax.ShapeDtypeStruct((CONV_WIDTH, F), F32),
                   jax.ShapeDtypeStruct((1, F), F32), jax.ShapeDtypeStruct((1, F), F32)],
        scratch_shapes=[pltpu.VMEM((CONV_WIDTH, S8, tn), F32), pltpu.VMEM((CONV_WIDTH, S8, tn), F32),
                        pltpu.VMEM((S8, tn), F32), pltpu.VMEM((S8, tn), F32)],
        compiler_params=_cparams("parallel", "arbitrary"),
    )(u, u, u, u, u, u, dact, dact, cw, cw, cb, cb)


def _logf_cumsum(pre, bf, name, *, tm):
    Lp, W = pre.shape

    def body(p_ref, b_ref, c_ref, carry_ref):
        i = pl.program_id(0)

        @pl.when(i == 0)
        def _():
            carry_ref[...] = jnp.zeros_like(carry_ref)

        x = p_ref[...] + b_ref[...]
        lf = jnp.minimum(x, 0.0) - jnp.log(1.0 + jnp.exp(-jnp.abs(x)))
        tri = (lax.broadcasted_iota(jnp.int32, (tm, tm), 0) >= lax.broadcasted_iota(jnp.int32, (tm, tm), 1)).astype(F32)
        c = jnp.dot(tri, lf, precision=lax.Precision.HIGHEST, preferred_element_type=F32) + carry_ref[...]
        c_ref[...] = c
        carry_ref[...] = c[tm - 1:tm, :]

    row = pl.BlockSpec((tm, W), lambda i: (i, 0))
    return pl.pallas_call(
        body, name=name, grid=(Lp // tm,),
        in_specs=[row, pl.BlockSpec((1, W), lambda i: (0, 0))], out_specs=row,
        out_shape=jax.ShapeDtypeStruct((Lp, W), F32),
        scratch_shapes=[pltpu.VMEM((1, W), F32)],
        compiler_params=_cparams("arbitrary"),
    )(pre, bf)


def _logf_bwd(dc_a, dc_b, pre, bf, name, *, tm):
    Lp, W = pre.shape
    ni = Lp // tm

    def body(dca_ref, dcb_ref, p_ref, b_ref, dpb_ref, db_ref, carry_ref):
        i = pl.program_id(0)

        @pl.when(i == 0)
        def _():
            carry_ref[...] = jnp.zeros_like(carry_ref)
            db_ref[...] = jnp.zeros_like(db_ref)

        triu = (lax.broadcasted_iota(jnp.int32, (tm, tm), 0) <= lax.broadcasted_iota(jnp.int32, (tm, tm), 1)).astype(F32)
        dl = jnp.dot(triu, dca_ref[...] + dcb_ref[...], precision=lax.Precision.HIGHEST,
                     preferred_element_type=F32) + carry_ref[...]
        carry_ref[...] = dl[0:1, :]
        dp = dl * jax.nn.sigmoid(-(p_ref[...] + b_ref[...]))
        dpb_ref[...] = dp.astype(BF16)
        db_ref[...] += jnp.sum(dp, axis=0, keepdims=True)

    rev = pl.BlockSpec((tm, W), lambda i: (ni - 1 - i, 0))
    vec = pl.BlockSpec((1, W), lambda i: (0, 0))
    return pl.pallas_call(
        body, name=name, grid=(ni,),
        in_specs=[rev, rev, rev, vec], out_specs=[rev, vec],
        out_shape=[jax.ShapeDtypeStruct((Lp, W), BF16), jax.ShapeDtypeStruct((1, W), F32)],
        scratch_shapes=[pltpu.VMEM((1, W), F32)],
        compiler_params=_cparams("arbitrary"),
    )(dc_a, dc_b, pre, bf)


def _attn_fwd(qh, kh4, vh4, crow4, name, *, tq):
    H, Lp, dh = qh.shape
    nq = Lp // tq
    S8 = SUBLANES
    HB = ATTN_HEADS
    n_scratch = 6
    lane_tiles = tq // LANES

    def to_column(row8):
        return jnp.transpose(jnp.concatenate([row8] * (LANES // S8), axis=0))

    def body(q_ref, k_ref, v_ref, c_ref, o_ref, lse_ref, *scratch):
        i = pl.program_id(1)
        heads = [scratch[n_scratch * hb:n_scratch * (hb + 1)] for hb in range(HB)]

        @pl.when(i == 0)
        def _():
            for hb, refs in enumerate(heads):
                for j in range(nq):
                    refs[5][j] = to_column(jnp.concatenate([c_ref[hb, j]] * S8, axis=0))

        for m_ref, l_ref, acc_ref, _, _, _ in heads:
            m_ref[...] = jnp.full_like(m_ref, NEG_INF)
            l_ref[...] = jnp.zeros_like(l_ref)
            acc_ref[...] = jnp.zeros_like(acc_ref)

        def chunk(j, masked):
            for hb, (_, _, _, st_ref, _, _) in enumerate(heads):
                st_ref[...] = lax.dot_general(k_ref[hb, j], q_ref[hb], NT, preferred_element_type=F32)
            for hb, (m_ref, l_ref, acc_ref, st_ref, pt_ref, cs_ref) in enumerate(heads):
                ct = c_ref[hb, i]
                mx = jnp.full((S8, tq), NEG_INF, F32)
                for r0 in range(0, tq, ATTN_STRIP):
                    rows = pl.ds(r0, ATTN_STRIP)
                    cs = jnp.concatenate([cs_ref[j, rows, :]] * lane_tiles, axis=1)
                    st = st_ref[rows, :] + (ct - cs)
                    if masked:
                        keep = (lax.broadcasted_iota(jnp.int32, (ATTN_STRIP, tq), 1)
                                >= r0 + lax.broadcasted_iota(jnp.int32, (ATTN_STRIP, tq), 0))
                        st = jnp.where(keep, st, NEG_INF)
                    st_ref[rows, :] = st
                    for g0 in range(0, ATTN_STRIP, S8):
                        mx = jnp.maximum(mx, st[g0:g0 + S8])
                m_prev = m_ref[...]
                m_new = jnp.maximum(m_prev, jnp.max(mx, axis=0, keepdims=True))
                alpha = jnp.exp(m_prev - m_new)
                m_ref[...] = m_new
                ls = jnp.zeros((S8, tq), F32)
                for r0 in range(0, tq, ATTN_STRIP):
                    pieces = [jnp.exp(st_ref[pl.ds(r0 + g0, S8), :] - m_new) for g0 in range(0, ATTN_STRIP, S8)]
                    for piece in pieces:
                        ls = ls + piece
                    pt_ref[pl.ds(r0, ATTN_STRIP), :] = jnp.concatenate(pieces, axis=0).astype(BF16)
                l_ref[...] = alpha * l_ref[...] + ls
                pv = lax.dot_general(pt_ref[...], v_ref[hb, j], TN, preferred_element_type=F32)
                acc_ref[...] = to_column(alpha)[:, :dh] * acc_ref[...] + pv

        def step(j, carry):
            chunk(j, False)
            return carry

        lax.fori_loop(0, i, step, 0)
        chunk(i, True)
        for hb, (m_ref, l_ref, acc_ref, _, _, _) in enumerate(heads):
            l_row = jnp.sum(l_ref[...], axis=0, keepdims=True)
            l8 = jnp.concatenate([l_row] * S8, axis=0)
            o_ref[hb] = acc_ref[...] / to_column(l8)[:, :dh]
            lse_ref[hb, 0] = m_ref[0:1, :] + jnp.log(l_row)

    per_head = [pltpu.VMEM((S8, tq), F32), pltpu.VMEM((S8, tq), F32), pltpu.VMEM((tq, dh), F32),
                pltpu.VMEM((tq, tq), F32), pltpu.VMEM((tq, tq), BF16), pltpu.VMEM((nq, tq, LANES), F32)]
    assert len(per_head) == n_scratch and H % HB == 0
    return pl.pallas_call(
        body, name=name, grid=(H // HB, nq),
        in_specs=[pl.BlockSpec((HB, tq, dh), lambda h, i: (h, i, 0)),
                  pl.BlockSpec((HB, nq, tq, dh), lambda h, i: (h, 0, 0, 0)),
                  pl.BlockSpec((HB, nq, tq, dh), lambda h, i: (h, 0, 0, 0)),
                  pl.BlockSpec((HB, nq, 1, tq), lambda h, i: (h, 0, 0, 0))],
        out_specs=[pl.BlockSpec((HB, tq, dh), lambda h, i: (h, i, 0)),
                   pl.BlockSpec((HB, 1, 1, tq), lambda h, i: (h, i, 0, 0))],
        out_shape=[jax.ShapeDtypeStruct((H, Lp, dh), F32), jax.ShapeDtypeStruct((H, nq, 1, tq), F32)],
        scratch_shapes=per_head * HB,
        compiler_params=_cparams("parallel", "arbitrary"),
    )(qh, kh4, vh4, crow4)


def _attn_delta(do, o, name, *, tm, n_heads):
    Lp, D = do.shape

    def body(do_ref, o_ref, d_ref):
        sel = (lax.broadcasted_iota(jnp.int32, (D, LANES), 0) // HEAD_DIM
               == lax.broadcasted_iota(jnp.int32, (D, LANES), 1)).astype(F32)
        do = do_ref[...].astype(BF16).astype(F32)
        d_ref[...] = jnp.dot(do * o_ref[...], sel, precision=lax.Precision.HIGHEST,
                             preferred_element_type=F32)

    row = pl.BlockSpec((tm, D), lambda i: (i, 0))
    return pl.pallas_call(
        body, name=name, grid=(Lp // tm,),
        in_specs=[row, row], out_specs=pl.BlockSpec((tm, LANES), lambda i: (i, 0)),
        out_shape=jax.ShapeDtypeStruct((Lp, LANES), F32),
        compiler_params=_cparams("parallel"),
    )(do, o)


def _attn_bwd(qh4, doh4, kh, vh, lse4, delta4, crow4, name, *, tq):
    H, nq, _, dh = qh4.shape
    Lp = nq * tq

    lane_tiles = tq // LANES

    def body(q_ref, do_ref, k_ref, v_ref, lse_ref, dl_ref, ct_ref,
             dq_ref, dk_ref, dv_ref, dcs_ref, dcq_ref, dk_acc, dv_acc, dc_acc, st_ref, dp_ref, pt_ref, ds_ref, cs_ref):
        j = pl.program_id(1)
        cs_ref[...] = jnp.transpose(jnp.broadcast_to(ct_ref[0, j], (LANES, tq)))

        @pl.when(j == 0)
        def _():
            dq_ref[...] = jnp.zeros_like(dq_ref)
            dcq_ref[...] = jnp.zeros_like(dcq_ref)

        k = k_ref[0]
        v = v_ref[0]
        dk_acc[...] = jnp.zeros_like(dk_acc)
        dv_acc[...] = jnp.zeros_like(dv_acc)
        dc_acc[...] = jnp.zeros_like(dc_acc)

        def pair(i, masked):
            q = q_ref[0, i]
            do = do_ref[0, i]
            st_ref[...] = lax.dot_general(k, q, NT, preferred_element_type=F32)
            dp_ref[...] = lax.dot_general(v, do, NT, preferred_element_type=F32)
            bias_q = ct_ref[0, i] - lse_ref[0, i]
            delta = dl_ref[0, i]
            col_sum = jnp.zeros((SUBLANES, tq), F32)
            for r0 in range(0, tq, ATTN_STRIP):
                rows = pl.ds(r0, ATTN_STRIP)
                st = st_ref[rows, :] + (bias_q - jnp.concatenate([cs_ref[rows, :]] * lane_tiles, axis=1))
                if masked:
                    keep = (lax.broadcasted_iota(jnp.int32, (ATTN_STRIP, tq), 1)
                            >= r0 + lax.broadcasted_iota(jnp.int32, (ATTN_STRIP, tq), 0))
                    st = jnp.where(keep, st, NEG_INF)
                pt = jnp.exp(st)
                dst = pt * (dp_ref[rows, :] - delta)
                pt_ref[rows, :] = pt.astype(BF16)
                ds_ref[rows, :] = dst.astype(BF16)
                dc_acc[rows, :] += jnp.sum(dst, axis=1, keepdims=True)
                for g0 in range(0, ATTN_STRIP, SUBLANES):
                    col_sum = col_sum + dst[g0:g0 + SUBLANES]
            dcq_ref[0, i] += jnp.sum(col_sum, axis=0, keepdims=True)
            dv_acc[...] += jnp.dot(pt_ref[...], do, preferred_element_type=F32)
            dk_acc[...] += jnp.dot(ds_ref[...], q, preferred_element_type=F32)
            dq_ref[0, i] += lax.dot_general(ds_ref[...], k, TN, preferred_element_type=F32)

        def step(i, carry):
            pair(i, False)
            return carry

        pair(j, True)
        lax.fori_loop(j + 1, nq, step, 0)
        dk_ref[0] = dk_acc[...]
        dv_ref[0] = dv_acc[...]
        dcs_ref[0] = -dc_acc[...]

    whole = pl.BlockSpec((1, nq, tq, dh), lambda h, j: (h, 0, 0, 0))
    tile = pl.BlockSpec((1, tq, dh), lambda h, j: (h, j, 0))
    rows = pl.BlockSpec((1, nq, 1, tq), lambda h, j: (h, 0, 0, 0))
    col = pl.BlockSpec((1, tq, 1), lambda h, j: (h, j, 0))
    return pl.pallas_call(
        body, name=name, grid=(H, nq),
        in_specs=[whole, whole, tile, tile, rows, rows, rows],
        out_specs=[whole, tile, tile, col, rows],
        out_shape=[jax.ShapeDtypeStruct((H, nq, tq, dh), F32), jax.ShapeDtypeStruct((H, Lp, dh), F32),
                   jax.ShapeDtypeStruct((H, Lp, dh), F32), jax.ShapeDtypeStruct((H, Lp, 1), F32),
                   jax.ShapeDtypeStruct((H, nq, 1, tq), F32)],
        scratch_shapes=[pltpu.VMEM((tq, dh), F32), pltpu.VMEM((tq, dh), F32), pltpu.VMEM((tq, 1), F32),
                        pltpu.VMEM((tq, tq), F32), pltpu.VMEM((tq, tq), F32),
                        pltpu.VMEM((tq, tq), BF16), pltpu.VMEM((tq, tq), BF16), pltpu.VMEM((tq, LANES), F32)],
        compiler_params=_cparams("parallel", "arbitrary"),
    )(qh4, doh4, kh, vh, lse4, delta4, crow4)


def _remote(src, dst, send_sems, recv_sems, k, to):
    return pltpu.make_async_remote_copy(src_ref=src, dst_ref=dst, send_sem=send_sems.at[k], recv_sem=recv_sems.at[k],
                                        device_id=to, device_id_type=MESH)


def _place():
    x, y, c = lax.axis_index("x"), lax.axis_index("y"), lax.axis_index("c")
    other_chips = [(1 - x, y), (x, 1 - y), (1 - x, 1 - y)]
    return x, y, c, other_chips


def _all_gather_weights(wb, wf, name):
    Rb, C = wb.shape
    Rf = wf.shape[0]
    hb = Rb // 2

    def body(wb_ref, wf_ref, ob_ref, of_ref, send_sems, recv_sems):
        x, y, c, chips = _place()
        me = 2 * x + y
        sibling = (x, y, 1 - c)

        def half(chip, core):
            return ob_ref.at[chip, pl.ds(core * hb, hb), :]

        sent = []
        for j, (cx, cy) in enumerate(chips):
            sent.append(_remote(wb_ref.at[pl.ds(c * hb, hb), :], half(me, c), send_sems, recv_sems, j, (cx, cy, c)))
            sent.append(_remote(wf_ref, of_ref.at[me], send_sems, recv_sems, 3 + j, (cx, cy, c)))
        for cp in sent:
            cp.start()
        for j, (cx, cy) in enumerate(chips):
            chip = 2 * cx + cy
            _remote(half(chip, c), half(chip, c), send_sems, recv_sems, j, sibling).wait_recv()
            fwd = _remote(half(chip, c), half(chip, c), send_sems, recv_sems, 6 + j, sibling)
            fwd.start()
            sent.append(fwd)
        for j, (cx, cy) in enumerate(chips):
            chip = 2 * cx + cy
            _remote(wf_ref, of_ref.at[chip], send_sems, recv_sems, 3 + j, sibling).wait_recv()
            _remote(half(chip, 1 - c), half(chip, 1 - c), send_sems, recv_sems, 6 + j, sibling).wait_recv()
        for cp in sent:
            cp.wait_send()

    any_spec = pl.BlockSpec(memory_space=pl.ANY)
    return pl.pallas_call(
        body, name=name,
        in_specs=[any_spec, any_spec], out_specs=[any_spec, any_spec],
        out_shape=[jax.ShapeDtypeStruct((N_CHIPS, Rb, C), BF16), jax.ShapeDtypeStruct((N_CHIPS, Rf, C), F32)],
        scratch_shapes=[pltpu.SemaphoreType.DMA((9,)), pltpu.SemaphoreType.DMA((9,))],
    )(wb, wf)


def _half_of(ref, order, half):
    return ref.at[pl.ds(0, N_CHIPS), half] if order == "CH" else ref.at[half]


def _halves_to_sibling(grads, orders, name):
    n = len(grads)

    def body(*refs):
        g_refs, a_refs, (send_sems, recv_sems) = refs[:n], refs[n:2 * n], refs[2 * n:]
        x, y, c, _ = _place()
        copies = [_remote(_half_of(g, o, 1 - c), a, send_sems, recv_sems, k, (x, y, 1 - c))
                  for k, (g, a, o) in enumerate(zip(g_refs, a_refs, orders))]
        for cp in copies:
            cp.start()
        for cp in copies:
            cp.wait()

    any_spec = pl.BlockSpec(memory_space=pl.ANY)
    shapes = [g.shape[2:] for g in grads]
    return pl.pallas_call(
        body, name=name, in_specs=[any_spec] * n, out_specs=[any_spec] * n,
        out_shape=[jax.ShapeDtypeStruct((N_CHIPS,) + s, F32) for s in shapes],
        scratch_shapes=[pltpu.SemaphoreType.DMA((n,)), pltpu.SemaphoreType.DMA((n,))],
    )(*grads)


def _chip_partial(g, a, core, order, wire, name, *, tr):
    _, R, C = a.shape
    narrow = wire != F32

    def body(core_ref, g_ref, a_ref, *outs):
        p = g_ref[0, 0] + a_ref[0]
        outs[0][0] = p
        if narrow:
            outs[1][0] = p.astype(wire)

    if order == "CH":
        g_spec = pl.BlockSpec((1, 1, tr, C), lambda s, i, core_ref: (s, core_ref[0], i, 0))
    else:
        g_spec = pl.BlockSpec((1, 1, tr, C), lambda s, i, core_ref: (core_ref[0], s, i, 0))
    blk = pl.BlockSpec((1, tr, C), lambda s, i, core_ref: (s, i, 0))
    grid_spec = pltpu.PrefetchScalarGridSpec(
        num_scalar_prefetch=1, grid=(N_CHIPS, R // tr), in_specs=[g_spec, blk],
        out_specs=[blk, blk] if narrow else [blk])
    out_shape = [jax.ShapeDtypeStruct((N_CHIPS, R, C), F32)] + ([jax.ShapeDtypeStruct((N_CHIPS, R, C), wire)] if narrow else [])
    outs = pl.pallas_call(body, name=name, grid_spec=grid_spec, out_shape=out_shape,
                          compiler_params=_cparams("parallel", "parallel"))(core, g, a)
    return outs[0], outs[-1]


def _chip_exchange(parts, rep, name):
    n = len(parts)
    rr, C = rep.shape

    def body(*refs):
        p_refs, rep_ref = refs[:n], refs[n]
        land_refs, reps_ref = refs[n + 1:2 * n + 1], refs[2 * n + 1]
        send_sems, recv_sems, local_sem = refs[2 * n + 2:]
        x, y, c, chips = _place()
        me = 4 * x + 2 * y + c
        own = pltpu.make_async_copy(rep_ref, reps_ref.at[me], local_sem.at[0])
        own.start()
        sent = []
        for k, (p, land) in enumerate(zip(p_refs, land_refs)):
            for j, (cx, cy) in enumerate(chips):
                sent.append(_remote(p.at[2 * cx + cy], land.at[j], send_sems, recv_sems, 3 * k + j, (cx, cy, c)))
        for r in range(1, N_DEV):
            fx, fy, fc = (r >> 2) & 1, (r >> 1) & 1, r & 1
            sent.append(_remote(rep_ref, reps_ref.at[me], send_sems, recv_sems, 3 * n - 1 + r, (x ^ fx, y ^ fy, c ^ fc)))
        for cp in sent:
            cp.start()
        for k, (p, land) in enumerate(zip(p_refs, land_refs)):
            for j in range(3):
                _remote(p.at[0], land.at[j], send_sems, recv_sems, 3 * k + j, (x, y, c)).wait_recv()
        for r in range(1, N_DEV):
            fx, fy, fc = (r >> 2) & 1, (r >> 1) & 1, r & 1
            frm = 4 * (x ^ fx) + 2 * (y ^ fy) + (c ^ fc)
            _remote(rep_ref, reps_ref.at[frm], send_sems, recv_sems, 3 * n - 1 + r, (x, y, c)).wait_recv()
        for cp in sent:
            cp.wait_send()
        own.wait()

    any_spec = pl.BlockSpec(memory_space=pl.ANY)
    n_sems = 3 * n + N_DEV - 1
    return pl.pallas_call(
        body, name=name, in_specs=[any_spec] * (n + 1), out_specs=[any_spec] * (n + 1),
        out_shape=[jax.ShapeDtypeStruct((3,) + p.shape[1:], p.dtype) for p in parts]
        + [jax.ShapeDtypeStruct((N_DEV, rr, C), F32)],
        scratch_shapes=[pltpu.SemaphoreType.DMA((n_sems,)), pltpu.SemaphoreType.DMA((n_sems,)),
                        pltpu.SemaphoreType.DMA((1,))],
    )(*parts, rep)


def _adamw_math(w, g, m, v):
    m = ADAM_B1 * m + (1.0 - ADAM_B1) * g
    v = ADAM_B2 * v + (1.0 - ADAM_B2) * (g * g)
    m_hat = m / (1.0 - ADAM_B1 ** ADAM_STEP)
    v_hat = v / (1.0 - ADAM_B2 ** ADAM_STEP)
    delta = -ADAM_LR * (m_hat / (jnp.sqrt(v_hat) + ADAM_EPS) + ADAM_WD * w)
    return delta, m, v


def _adamw_owned(part, landed, w, m, v, place, name, *, tr):
    _, R, C = part.shape

    def body(place_ref, own_ref, land_ref, w_ref, m_ref, v_ref, g_ref, d_ref, mo_ref, vo_ref):
        g = own_ref[0]
        for s in range(3):
            g = g + land_ref[s].astype(F32)
        delta, m_new, v_new = _adamw_math(w_ref[0], g, m_ref[0], v_ref[0])
        g_ref[0] = g
        d_ref[0] = delta
        mo_ref[0] = m_new
        vo_ref[0] = v_new

    half = pl.BlockSpec((1, tr, C), lambda i, place_ref: (place_ref[0], i, 0))
    grid_spec = pltpu.PrefetchScalarGridSpec(
        num_scalar_prefetch=1, grid=(R // tr,),
        in_specs=[pl.BlockSpec((1, tr, C), lambda i, place_ref: (place_ref[1], i, 0)),
                  pl.BlockSpec((3, tr, C), lambda i, place_ref: (0, i, 0)), half, half, half],
        out_specs=[half] * 4)
    return pl.pallas_call(
        body, name=name, grid_spec=grid_spec, out_shape=[jax.ShapeDtypeStruct((2, R, C), F32)] * 4,
        compiler_params=_cparams("parallel"),
    )(place, part, landed, w, m, v)


def _join_halves(bufs, name):
    n = len(bufs)

    def body(*refs):
        out_refs, (send_sems, recv_sems) = refs[n:2 * n], refs[2 * n:]
        x, y, c, _ = _place()
        copies = [_remote(o.at[c], o.at[c], send_sems, recv_sems, k, (x, y, 1 - c)) for k, o in enumerate(out_refs)]
        for cp in copies:
            cp.start()
        for k, o in enumerate(out_refs):
            _remote(o.at[c], o.at[1 - c], send_sems, recv_sems, k, (x, y, 1 - c)).wait_recv()
        for cp in copies:
            cp.wait_send()

    any_spec = pl.BlockSpec(memory_space=pl.ANY)
    return pl.pallas_call(
        body, name=name, in_specs=[any_spec] * n, out_specs=[any_spec] * n,
        out_shape=[jax.ShapeDtypeStruct(b.shape, b.dtype) for b in bufs],
        input_output_aliases={k: k for k in range(n)},
        scratch_shapes=[pltpu.SemaphoreType.DMA((n,)), pltpu.SemaphoreType.DMA((n,))],
    )(*bufs)


def _sum_adamw(own, landed, w, m, v, name, *, tr):
    n = landed.shape[0]
    hr, C = own.shape

    def body(own_ref, land_ref, w_ref, m_ref, v_ref, o_ref):
        g = own_ref[...]
        for s in range(n):
            g = g + land_ref[s]
        delta, m_new, v_new = _adamw_math(w_ref[...], g, m_ref[...], v_ref[...])
        o_ref[0] = g
        o_ref[1] = delta
        o_ref[2] = m_new
        o_ref[3] = v_new

    blk = pl.BlockSpec((tr, C), lambda i: (i, 0))
    return pl.pallas_call(
        body, name=name, grid=(hr // tr,),
        in_specs=[blk, pl.BlockSpec((n, tr, C), lambda i: (0, i, 0)), blk, blk, blk],
        out_specs=pl.BlockSpec((4, tr, C), lambda i: (0, i, 0)),
        out_shape=jax.ShapeDtypeStruct((4, hr, C), F32), compiler_params=_cparams("parallel"),
    )(own, landed, w, m, v)


def _rows_of(shape):
    n = 1
    for d in shape:
        n *= d
    return -(-n // PACK_COLS)


def _pack(arrays, total_rows, dtype):
    parts, used = [], 0
    for a in arrays:
        flat = a.reshape(-1).astype(dtype)
        fill = _rows_of(a.shape) * PACK_COLS - flat.shape[0]
        parts += [flat] + ([jnp.zeros((fill,), dtype)] if fill else [])
        used += _rows_of(a.shape)
    if total_rows > used:
        parts.append(jnp.zeros(((total_rows - used) * PACK_COLS,), dtype))
    return jnp.concatenate(parts).reshape(total_rows, PACK_COLS)


def _unpack(buf, shapes):
    lead = buf.shape[:-2]
    out, r = [], 0
    for shp in shapes:
        n = 1
        for d in shp:
            n *= d
        rows = _rows_of(shp)
        piece = buf[..., r:r + rows, :].reshape(lead + (rows * PACK_COLS,))[..., :n]
        out.append(piece.reshape(lead + tuple(shp)))
        r += rows
    return out


def _join_shards(stacked, axis):
    return jnp.concatenate([stacked[s] for s in range(N_CHIPS)], axis=axis)


def _shard_of(full, axis, chip):
    width = full.shape[axis] // N_CHIPS
    return lax.slice_in_dim(full, chip * width, (chip + 1) * width, axis=axis)


def _local_step(h0, tgt, W, *, seq, tm):
    Lp, D = h0.shape
    H = D // HEAD_DIM
    F2 = W["ffn_w_in"].shape[-1]
    F = F2 // 2
    te = tm // 2
    nq = Lp // tm
    cap = 1408
    tD, tF, tF2 = _pick(D, cap), _pick(F, cap), _pick(F2, cap)
    t2D = _pick(2 * D, cap)
    t2Dc, tF2c = _pick(2 * D // N_CHIPS, cap), _pick(F2 // N_CHIPS, cap)
    tcn = _pick(F, cap)

    def vec(a):
        return a.reshape(1, -1)

    ln_g, ln_b = W["ln_g"], W["ln_b"]
    wf_pad = jnp.pad(W["w_f"], ((0, 0), (0, LANES - H)))
    bf_pad = jnp.pad(W["b_f"], (0, LANES - H)).reshape(1, LANES)

    def ffn_fwd(hb, l, tag):
        u = _mm(hb, W["ffn_w_in"][l], "nn", F32, f"ffn{tag}_up", tm=tm, tn=tF2, tk=tD)
        act = _conv_glu_fwd(u, W["ffn_conv_w"][l], vec(W["ffn_conv_b"][l]), f"ffn{tag}_glu", tm=te, tn=tcn)
        y = _mm(act, W["ffn_w_out"][l], "nn", F32, f"ffn{tag}_down", tm=tm, tn=tD, tk=tF)
        return u, act, y

    def ffn_bwd(dzb, hb, u, act, l, tag, dw_in_acc, dw_out_acc):
        dact = _mm(dzb, W["ffn_w_out"][l], "nt", F32, f"ffn{tag}_dact", tm=tm, tn=tF, tk=tD)
        dw_out = _mm(act, dzb, "tn", F32, f"ffn{tag}_dwout", tm=tF, tn=tD, tk=tm, layer=l, into=dw_out_acc)
        dua, dug, dwa, dwg, dba, dbg = _conv_glu_bwd(u, dact, W["ffn_conv_w"][l], vec(W["ffn_conv_b"][l]),
                                                     f"ffn{tag}_dglu", tm=te, tn=tcn)
        du = jnp.concatenate([dua, dug], axis=1)
        dcw = jnp.concatenate([dwa, dwg], axis=1)
        dcb = jnp.concatenate([dba, dbg], axis=1)
        dh = _mm(du, W["ffn_w_in"][l], "nt", F32, f"ffn{tag}_dh", tm=tm, tn=tD, tk=tF2)
        dw_in = _mm(hb, du, "tn", F32, f"ffn{tag}_dwin", tm=tD, tn=tF2c, tk=tm, chips=True, layer=l, into=dw_in_acc)
        return dh, dw_in, dw_out, dcw, dcb[0]

    diffb, mixpre, h1, h1b, xh1, rs1 = _pool_ln_fwd(h0, W["pool_w"][0], W["pool_scale"], vec(ln_g[0, 0]),
                                                    vec(ln_b[0, 0]), "pool_ln_fwd", tm=te)
    u0, act0, y0 = ffn_fwd(h1b, 0, "0")
    h2, h2b, xh2, rs2 = _ln_fwd(h1, y0, vec(ln_g[0, 1]), vec(ln_b[0, 1]), "ln01_fwd", tm=te)

    kvb = _mm(h2b, W["w_kv"], "nn", BF16, "kv_proj", tm=tm, tn=t2D, tk=tD)
    qb = _mm(h2b, W["w_q"][0], "nn", BF16, "q_proj", tm=tm, tn=tD, tk=tD, scale=HEAD_DIM ** -0.5)
    pre = _mm(h2b, wf_pad, "nn", F32, "f_proj", tm=tm, tn=LANES, tk=tD)
    c = _logf_cumsum(pre, bf_pad, "logf_cumsum", tm=tm)

    def heads(a):
        return a.reshape(Lp, H, HEAD_DIM).transpose(1, 0, 2)

    def tokens(a):
        return a.transpose(1, 0, 2).reshape(Lp, D)

    qh, kh, vh = heads(qb), heads(kvb[:, :D]), heads(kvb[:, D:])
    crow4 = c[:, :H].T.reshape(H, nq, 1, tm)
    oh, lse4 = _attn_fwd(qh, kh.reshape(H, nq, tm, HEAD_DIM), vh.reshape(H, nq, tm, HEAD_DIM), crow4,
                         "attn_fwd", tq=tm)
    o_tok = tokens(oh)
    ob = o_tok.astype(BF16)
    y_attn = _mm(ob, W["w_o"][0], "nn", F32, "o_proj", tm=tm, tn=tD, tk=tD)
    h3, h3b, xh3, rs3 = _ln_fwd(h2, y_attn, vec(ln_g[1, 0]), vec(ln_b[1, 0]), "ln10_fwd", tm=te)
    u1, act1, y1 = ffn_fwd(h3b, 1, "1")
    h4, _, xh4, rs4 = _ln_fwd(h3, y1, vec(ln_g[1, 1]), vec(ln_b[1, 1]), "ln11_fwd", tm=te)
    dy, loss = _loss_head(h4, tgt, "loss_head", tm=te, row_lo=N_META, row_hi=N_META + seq)

    dz4, dz4b, dg11, db11 = _ln_bwd([dy], [1.0], xh4, rs4, vec(ln_g[1, 1]), "ln11_bwd", tm=te)
    dh3, dw_in, dw_out, dcw1, dcb1 = ffn_bwd(dz4b, h3b, u1, act1, 1, "1", None, None)
    dz3, dz3b, dg10, db10 = _ln_bwd([dz4, dh3], [ALPHA, 1.0], xh3, rs3, vec(ln_g[1, 0]), "ln10_bwd", tm=te)

    do_tok = _mm(dz3b, W["w_o"][0], "nt", F32, "o_proj_dx", tm=tm, tn=tD, tk=tD)
    dw_o = _mm(ob, dz3b, "tn", F32, "o_proj_dw", tm=tD, tn=tD, tk=tm)
    delta = _attn_delta(do_tok, o_tok, "attn_delta", tm=te, n_heads=H)
    doh4 = heads(do_tok.astype(BF16)).reshape(H, nq, tm, HEAD_DIM)
    dqh4, dkh, dvh, dcs, dcq = _attn_bwd(qh.reshape(H, nq, tm, HEAD_DIM), doh4, kh, vh,
                                    lse4, delta[:, :H].T.reshape(H, nq, 1, tm), crow4,
                                    "attn_bwd", tq=tm)
    dqb = tokens(dqh4.reshape(H, Lp, HEAD_DIM)).astype(BF16)
    dkvb = jnp.concatenate([tokens(dkh), tokens(dvh)], axis=1).astype(BF16)
    dc_keys = jnp.pad(dcs.reshape(H, Lp).T, ((0, 0), (0, LANES - H)))
    dc_queries = jnp.pad(dcq.reshape(H, Lp).T, ((0, 0), (0, LANES - H)))
    dpreb, dbf = _logf_bwd(dc_keys, dc_queries, pre, bf_pad, "logf_bwd", tm=tm)

    qs = HEAD_DIM ** -0.5
    dw_q = _mm(h2b, dqb, "tn", F32, "q_proj_dw", tm=tD, tn=tD, tk=tm, scale=qs)
    dw_kv = _mm(h2b, dkvb, "tn", F32, "kv_proj_dw", tm=tD, tn=t2Dc, tk=tm, chips=True)
    dw_f = _mm(h2b, dpreb, "tn", F32, "f_proj_dw", tm=tD, tn=LANES, tk=tm)[:, :H]
    dh2 = _mm(dqb, W["w_q"][0], "nt", F32, "q_proj_dx", tm=tm, tn=tD, tk=tD, scale=qs)
    dh2 = _mm(dkvb, W["w_kv"], "nt", F32, "kv_proj_dx", tm=tm, tn=tD, tk=t2D, add=dh2)
    dh2 = _mm(dpreb, wf_pad, "nt", F32, "f_proj_dx", tm=tm, tn=tD, tk=LANES, add=dh2)
    dz2, dz2b, dg01, db01 = _ln_bwd([dz3, dh2], [ALPHA, 1.0], xh2, rs2, vec(ln_g[0, 1]), "ln01_bwd", tm=te)

    dh1, dw_in, dw_out, dcw0, dcb0 = ffn_bwd(dz2b, h1b, u0, act0, 0, "0", dw_in, dw_out)
    dz1, _, dg00, db00 = _ln_bwd([dz2, dh1], [ALPHA, 1.0], xh1, rs1, vec(ln_g[0, 0]), "ln00_bwd", tm=te)
    dh0, dmb, dscale = _pool_bwd(dz1, mixpre, W["pool_w"][0], W["pool_scale"], "pool_bwd", tm=te)
    dw_pool = _pool_dw(diffb, dmb, "pool_dw", tk=tm)

    grads = {
        "meta": dh0[:N_META],
        "pool_w": dw_pool[None],
        "pool_scale": dscale,
        "w_kv": dw_kv,
        "w_f": dw_f,
        "b_f": dbf[0, :H],
        "w_q": dw_q[None],
        "w_o": dw_o[None],
        "ffn_w_in": dw_in,
        "ffn_conv_w": jnp.stack([dcw0, dcw1]),
        "ffn_conv_b": jnp.stack([dcb0, dcb1]),
        "ffn_w_out": dw_out,
        "ln_g": jnp.stack([jnp.stack([dg00[0], dg01[0]]), jnp.stack([dg10[0], dg11[0]])]),
        "ln_b": jnp.stack([jnp.stack([db00[0], db01[0]]), jnp.stack([db10[0], db11[0]])]),
    }
    return loss, dh0, grads


def _row_block(rows, cols):
    best = SUBLANES
    for t in range(SUBLANES, rows + 1, SUBLANES):
        if rows % t == 0 and t * cols * 4 <= ELEMENTWISE_BLOCK_BYTES:
            best = t
    return best


def _row_tile(length):
    return 640 if length >= 4096 else 128


def kernel(x, meta, pool_w, pool_scale, w_kv, w_f, b_f, w_q, w_o, ffn_w_in, ffn_conv_w, ffn_conv_b, ffn_w_out, ln_g, ln_b, loss_target, m_meta, m_pool_w, m_pool_scale, m_w_kv, m_w_f, m_b_f, m_w_q, m_w_o, m_ffn_w_in, m_ffn_conv_w, m_ffn_conv_b, m_ffn_w_out, m_ln_g, m_ln_b, v_meta, v_pool_w, v_pool_scale, v_w_kv, v_w_f, v_b_f, v_w_q, v_w_o, v_ffn_w_in, v_ffn_conv_w, v_ffn_conv_b, v_ffn_w_out, v_ln_g, v_ln_b):
    weights = dict(meta=meta, pool_w=pool_w, pool_scale=pool_scale, w_kv=w_kv, w_f=w_f, b_f=b_f, w_q=w_q, w_o=w_o,
                   ffn_w_in=ffn_w_in, ffn_conv_w=ffn_conv_w, ffn_conv_b=ffn_conv_b, ffn_w_out=ffn_w_out,
                   ln_g=ln_g, ln_b=ln_b)
    mom1 = dict(meta=m_meta, pool_w=m_pool_w, pool_scale=m_pool_scale, w_kv=m_w_kv, w_f=m_w_f, b_f=m_b_f, w_q=m_w_q,
                w_o=m_w_o, ffn_w_in=m_ffn_w_in, ffn_conv_w=m_ffn_conv_w, ffn_conv_b=m_ffn_conv_b,
                ffn_w_out=m_ffn_w_out, ln_g=m_ln_g, ln_b=m_ln_b)
    mom2 = dict(meta=v_meta, pool_w=v_pool_w, pool_scale=v_pool_scale, w_kv=v_w_kv, w_f=v_w_f, b_f=v_b_f, w_q=v_w_q,
                w_o=v_w_o, ffn_w_in=v_ffn_w_in, ffn_conv_w=v_ffn_conv_w, ffn_conv_b=v_ffn_conv_b,
                ffn_w_out=v_ffn_w_out, ln_g=v_ln_g, ln_b=v_ln_b)
    _, seq, D = x.shape
    L = N_META + seq
    tm = _row_tile(L)
    Lp = _round_up(L, tm)
    c_idx = lax.axis_index("c")
    chip = 2 * lax.axis_index("x") + lax.axis_index("y")

    shard_shapes = {n: weights[n].shape for n in SHARDED}
    rows_b = _round_up(sum(_rows_of(shard_shapes[n]) for n in MATMUL_WEIGHTS), 32)
    rows_f = _round_up(sum(_rows_of(shard_shapes[n]) for n in VECTOR_WEIGHTS), SUBLANES)
    wb = _pack([weights[n] for n in MATMUL_WEIGHTS], rows_b, BF16)
    wf = _pack([weights[n] for n in VECTOR_WEIGHTS], rows_f, F32)
    gb, gf = _all_gather_weights(wb, wf, "weights_all_gather")
    gb = lax.dynamic_update_index_in_dim(gb, wb, chip, axis=0)
    gf = lax.dynamic_update_index_in_dim(gf, wf, chip, axis=0)
    full = {}
    for names, buf in ((MATMUL_WEIGHTS, gb), (VECTOR_WEIGHTS, gf)):
        for n, stacked in zip(names, _unpack(buf, [shard_shapes[n] for n in names])):
            full[n] = _join_shards(stacked, SHARD_AXIS[n])
    full["b_f"] = b_f
    full["ffn_conv_b"] = ffn_conv_b

    pad = jnp.zeros((Lp - L, D), F32)
    h0 = jnp.concatenate([full["meta"], x[0], pad], axis=0)
    tgt = jnp.concatenate([jnp.zeros((N_META, D), F32), loss_target[0], pad], axis=0)
    loss, dh0, grads = _local_step(h0, tgt, full, seq=seq, tm=tm)
    loss = lax.psum(loss[0, 0], AXES)
    grad_x = dh0[N_META:L][None]

    core = c_idx.astype(jnp.int32).reshape(1)
    place = jnp.stack([c_idx, chip]).astype(jnp.int32)
    small_shapes = [shard_shapes[n] for n in SMALL_SHARDED]
    rows_s = _round_up(sum(_rows_of(s) for s in small_shapes), 2 * LANES)

    def packed_small(d):
        return _pack([d[n] for n in SMALL_SHARDED], rows_s, F32).reshape(2, rows_s // 2, PACK_COLS)

    names, orders, wires, g_views, wmv = [], [], [], [], []
    for n, order in BIG_SHARDED:
        shp = shard_shapes[n]
        C = shp[-1]
        R = weights[n].size // C // 2
        lead = (N_CHIPS, 2) if order == "CH" else (2, N_CHIPS)
        names.append(n)
        orders.append(order)
        wires.append(BF16)
        g_views.append(grads[n].reshape(lead + (R, C)))
        wmv.append([d[n].reshape(2, R, C) for d in (weights, mom1, mom2)])
    names.append("small")
    orders.append("CH")
    wires.append(F32)
    g_views.append(jnp.stack([_pack([_shard_of(grads[n], SHARD_AXIS[n], s) for n in SMALL_SHARDED], rows_s, F32)
                              for s in range(N_CHIPS)]).reshape(N_CHIPS, 2, rows_s // 2, PACK_COLS))
    wmv.append([packed_small(d) for d in (weights, mom1, mom2)])

    from_sibling = _halves_to_sibling(g_views, orders, "grads_to_sibling")
    parts, on_wire = [], []
    for n, order, wire, g, a in zip(names, orders, wires, g_views, from_sibling):
        p, pw = _chip_partial(g, a, core, order, wire, f"chip_sum_{n}", tr=_row_block(a.shape[1], a.shape[2]))
        parts.append(p)
        on_wire.append(pw)

    rep_shapes = [weights[n].shape for n in REPLICATED]
    rows_r = _round_up(sum(_rows_of(s) for s in rep_shapes), SUBLANES)
    rep = _pack([grads[n] for n in REPLICATED], rows_r, F32)
    *landed, reps = _chip_exchange(on_wire, rep, "grads_chip_exchange")

    halves = []
    for n, p, b, (w_, m_, v_) in zip(names, parts, landed, wmv):
        halves += _adamw_owned(p, b, w_, m_, v_, place, f"adamw_{n}", tr=_row_block(p.shape[1], p.shape[2]))
    joined = _join_halves(halves, "results_to_sibling")
    out = {}
    for k, n in enumerate(names[:-1]):
        out[n] = [a.reshape(shard_shapes[n]) for a in joined[4 * k:4 * k + 4]]
    small_out = [_unpack(a.reshape(rows_s, PACK_COLS), small_shapes) for a in joined[-4:]]
    for k, n in enumerate(SMALL_SHARDED):
        out[n] = [small_out[kind][k] for kind in range(4)]

    def packr(d):
        return _pack([d[n] for n in REPLICATED], rows_r, F32)

    res_r = _sum_adamw(reps[0], reps[1:], packr(weights), packr(mom1), packr(mom2), "adamw_replicated", tr=rows_r)
    rep_out = _unpack(res_r, rep_shapes)

    out.update({n: a for n, a in zip(REPLICATED, rep_out)})
    result = [loss, grad_x]
    for k in range(4):
        result += [out[n][k] for n in WEIGHT_ORDER]
    return tuple(result)
```

```python
import functools

import jax
import jax.numpy as jnp
from jax import lax
from jax.experimental import pallas as pl
from jax.experimental.pallas import tpu as pltpu

N_META = 16
POOL_WINDOWS = (2, 4, 8, 16)
MAX_WINDOW = max(POOL_WINDOWS)
N_GROUPS = len(POOL_WINDOWS)
HEAD_DIM = 64
DEPTH = 2
CONV_WIDTH = 3
ALPHA = (2.0 * DEPTH) ** 0.25
LN_EPS = 1e-5
NEG_INF = -1e30
ADAM_LR = 0.001
ADAM_B1 = 0.9
ADAM_B2 = 0.999
ADAM_EPS = 1e-08
ADAM_WD = 0.01
ADAM_STEP = 10

F32 = jnp.float32
BF16 = jnp.bfloat16
ATTN_STRIP = 32
GLU_STRIP = 16
LANES = 128
SUBLANES = 8
PACK_COLS = 1024
VMEM_LIMIT = 56 * 1024 * 1024
AXES = ("x", "y", "c")
MESH = pl.DeviceIdType.MESH

NN = (((1,), (0,)), ((), ()))
NT = (((1,), (1,)), ((), ()))
TN = (((0,), (0,)), ((), ()))

SHARD_AXIS = {"meta": 1, "pool_w": 2, "pool_scale": 1, "w_kv": 1, "w_f": 0, "w_q": 1, "w_o": 1,
              "ffn_w_in": 2, "ffn_conv_w": 2, "ffn_w_out": 1, "ln_g": 2, "ln_b": 2}
SHARDED = ("meta", "pool_w", "pool_scale", "w_kv", "w_f", "w_q", "w_o", "ffn_w_in", "ffn_conv_w",
           "ffn_w_out", "ln_g", "ln_b")
REPLICATED = ("b_f", "ffn_conv_b")
MATMUL_WEIGHTS = ("pool_w", "w_kv", "w_f", "w_q", "w_o", "ffn_w_in", "ffn_w_out")
VECTOR_WEIGHTS = ("meta", "pool_scale", "ffn_conv_w", "ln_g", "ln_b")
WEIGHT_ORDER = ("meta", "pool_w", "pool_scale", "w_kv", "w_f", "b_f", "w_q", "w_o", "ffn_w_in",
                "ffn_conv_w", "ffn_conv_b", "ffn_w_out", "ln_g", "ln_b")
BIG_SHARDED = (("w_kv", "CH"), ("w_q", "CH"), ("w_o", "CH"), ("ffn_w_in", "HC"), ("ffn_w_out", "HC"))
SMALL_SHARDED = ("meta", "pool_w", "pool_scale", "w_f", "ffn_conv_w", "ln_g", "ln_b")
ELEMENTWISE_BLOCK_BYTES = 3 * 512 * 1024
N_CHIPS = 4
N_DEV = 8


def _cparams(*sem):
    return pltpu.CompilerParams(dimension_semantics=sem, vmem_limit_bytes=VMEM_LIMIT)


def _round_up(n, m):
    return (n + m - 1) // m * m


def _pick(n, cap):
    if n <= cap:
        return n
    best = 0
    for t in range(LANES, cap + 1, LANES):
        if n % t == 0:
            best = t
    assert best, (n, cap)
    return best


def _mm(a, b, mode, out_dtype, name, *, tm, tn, tk, scale=None, add=None, chips=False, layer=None, into=None):
    if mode == "nn":
        (M, K), N = a.shape, b.shape[1]
    elif mode == "nt":
        (M, K), N = a.shape, b.shape[0]
    else:
        (K, M), N = a.shape, b.shape[1]
    assert M % tm == 0 and N % tn == 0 and K % tk == 0, (name, M, N, K, tm, tn, tk)
    nk = K // tk
    dn = {"nn": NN, "nt": NT, "tn": TN}[mode]
    has_add = add is not None
    has_into = into is not None
    assert not (has_add and (chips or layer is not None))

    def body(*refs):
        a_ref, b_ref = refs[0], refs[1]
        add_ref = refs[2] if has_add else None
        o_ref = refs[2 + has_add + has_into]
        acc_ref = refs[-1] if nk > 1 else None
        k = pl.program_id(2)
        part = lax.dot_general(a_ref[...], b_ref[...], dn, preferred_element_type=F32)

        def finish(r):
            if scale is not None:
                r = r * scale
            if has_add:
                r = r + add_ref[...]
            o_ref[...] = r.astype(out_dtype).reshape(o_ref.shape)

        if nk == 1:
            finish(part)
        else:
            @pl.when(k == 0)
            def _():
                acc_ref[...] = part

            @pl.when(k > 0)
            def _():
                acc_ref[...] += part

            @pl.when(k == nk - 1)
            def _():
                finish(acc_ref[...])

    if mode == "nn":
        a_spec = pl.BlockSpec((tm, tk), lambda j, i, k: (i, k))
        b_spec = pl.BlockSpec((tk, tn), lambda j, i, k: (k, j))
    elif mode == "nt":
        a_spec = pl.BlockSpec((tm, tk), lambda j, i, k: (i, k))
        b_spec = pl.BlockSpec((tn, tk), lambda j, i, k: (j, k))
    else:
        a_spec = pl.BlockSpec((tk, tm), lambda j, i, k: (k, i))
        b_spec = pl.BlockSpec((tk, tn), lambda j, i, k: (k, j))
    out_dims, blk = (M, N), (tm, tn)
    if chips:
        base, count = (0, N_CHIPS) if chips is True else chips
        per_chip = N // count // tn
        assert per_chip * tn * count == N, (name, N, tn)
        out_dims, blk = (N_CHIPS, M, N // count), (1, tm, tn)
        where = lambda j, i: (base + j // per_chip, i, j % per_chip)
    else:
        where = lambda j, i: (i, j)
    if layer is not None:
        out_dims, blk = (DEPTH,) + out_dims, (1,) + blk
        o_spec = pl.BlockSpec(blk, lambda j, i, k: (layer,) + where(j, i))
    else:
        o_spec = pl.BlockSpec(blk, lambda j, i, k: where(j, i))
    in_specs = [a_spec, b_spec] + ([o_spec] if has_add else []) + ([pl.BlockSpec(memory_space=pl.ANY)] if has_into else [])
    args = (a, b) + ((add,) if has_add else ()) + ((into,) if has_into else ())
    return pl.pallas_call(
        body, name=name, grid=(N // tn, M // tm, nk),
        in_specs=in_specs, out_specs=o_spec,
        out_shape=jax.ShapeDtypeStruct(out_dims, out_dtype),
        input_output_aliases={len(args) - 1: 0} if has_into else {},
        scratch_shapes=[pltpu.VMEM((tm, tn), F32)] if nk > 1 else [],
        compiler_params=_cparams("parallel", "parallel", "arbitrary"),
    )(*args)


def _ln_math(z, g, b):
    mu = jnp.mean(z, axis=-1, keepdims=True)
    zc = z - mu
    var = jnp.mean(zc * zc, axis=-1, keepdims=True)
    rstd = lax.rsqrt(var + LN_EPS)
    xh = zc * rstd
    return xh * g + b, xh, rstd


def _pool_ln_fwd(h0, pw, ps, g, b, name, *, tm):
    Lp, D = h0.shape
    G = D // N_GROUPS
    halo_blocks = tm // MAX_WINDOW

    def body(x_ref, halo_ref, pw_ref, ps_ref, g_ref, b_ref,
             diff_ref, mix_ref, h_ref, hb_ref, xh_ref, rs_ref, ext_ref):
        i = pl.program_id(0)
        ext_ref[0:MAX_WINDOW, :] = jnp.where(i == 0, 0.0, halo_ref[...])
        ext_ref[MAX_WINDOW:MAX_WINDOW + tm, :] = x_ref[...]
        t1 = (i * tm + 1 + lax.broadcasted_iota(jnp.int32, (tm, 1), 0)).astype(F32)
        for gi, w in enumerate(POOL_WINDOWS):
            lo, hi = gi * G, (gi + 1) * G
            xg = x_ref[:, lo:hi]
            win = xg
            for j in range(1, w):
                win = win + ext_ref[MAX_WINDOW - j:MAX_WINDOW - j + tm, lo:hi]
            d = (win / jnp.minimum(t1, float(w)) - xg).astype(BF16)
            diff_ref[:, lo:hi] = d
            mix_ref[:, lo:hi] = jnp.dot(d, pw_ref[gi], preferred_element_type=F32)
        z = ALPHA * x_ref[...] + mix_ref[...] * ps_ref[...]
        h, xh, rstd = _ln_math(z, g_ref[...], b_ref[...])
        h_ref[...] = h
        hb_ref[...] = h.astype(BF16)
        xh_ref[...] = xh
        rs_ref[...] = rstd

    row = pl.BlockSpec((tm, D), lambda i: (i, 0))
    vec = pl.BlockSpec((1, D), lambda i: (0, 0))
    return pl.pallas_call(
        body, name=name, grid=(Lp // tm,),
        in_specs=[row,
                  pl.BlockSpec((MAX_WINDOW, D), lambda i: (jnp.maximum(i * halo_blocks - 1, 0), 0)),
                  pl.BlockSpec((N_GROUPS, G, G), lambda i: (0, 0, 0)), vec, vec, vec],
        out_specs=[row, row, row, row, row, pl.BlockSpec((tm, 1), lambda i: (i, 0))],
        out_shape=[jax.ShapeDtypeStruct((Lp, D), BF16), jax.ShapeDtypeStruct((Lp, D), F32),
                   jax.ShapeDtypeStruct((Lp, D), F32), jax.ShapeDtypeStruct((Lp, D), BF16),
                   jax.ShapeDtypeStruct((Lp, D), F32), jax.ShapeDtypeStruct((Lp, 1), F32)],
        scratch_shapes=[pltpu.VMEM((tm + MAX_WINDOW, D), F32)],
        compiler_params=_cparams("parallel"),
    )(h0, h0, pw, ps, g, b)


def _pool_bwd(dz, mixpre, pw, ps, name, *, tm):
    Lp, D = dz.shape
    G = D // N_GROUPS
    halo_blocks = tm // MAX_WINDOW
    n_halo = Lp // MAX_WINDOW
    ni = Lp // tm
    R = tm + MAX_WINDOW

    def body(dz_ref, halo_ref, mix_ref, pw_ref, ps_ref, dh_ref, dmb_ref, dsc_ref, ext_ref, dp_ref):
        i = pl.program_id(0)
        ext_ref[0:tm, :] = dz_ref[...]
        ext_ref[tm:R, :] = jnp.where(i == ni - 1, 0.0, halo_ref[...])
        dmix = (ext_ref[...] * ps_ref[...]).astype(BF16)
        dmb_ref[...] = dmix[0:tm]

        @pl.when(i == 0)
        def _():
            dsc_ref[...] = jnp.zeros_like(dsc_ref)

        dsc_ref[...] += jnp.sum(dz_ref[...] * mix_ref[...], axis=0, keepdims=True)
        t1 = (i * tm + 1 + lax.broadcasted_iota(jnp.int32, (R, 1), 0)).astype(F32)
        for gi, w in enumerate(POOL_WINDOWS):
            lo, hi = gi * G, (gi + 1) * G
            dd = lax.dot_general(dmix[:, lo:hi], pw_ref[gi], NT, preferred_element_type=F32)
            dp_ref[:, lo:hi] = dd / jnp.minimum(t1, float(w))
            back = dp_ref[0:tm, lo:hi]
            for j in range(1, w):
                back = back + dp_ref[j:j + tm, lo:hi]
            dh_ref[:, lo:hi] = ALPHA * dz_ref[:, lo:hi] - dd[0:tm] + back

    row = pl.BlockSpec((tm, D), lambda i: (i, 0))
    vec = pl.BlockSpec((1, D), lambda i: (0, 0))
    return pl.pallas_call(
        body, name=name, grid=(ni,),
        in_specs=[row,
                  pl.BlockSpec((MAX_WINDOW, D), lambda i: (jnp.minimum((i + 1) * halo_blocks, n_halo - 1), 0)),
                  row, pl.BlockSpec((N_GROUPS, G, G), lambda i: (0, 0, 0)), vec],
        out_specs=[row, row, vec],
        out_shape=[jax.ShapeDtypeStruct((Lp, D), F32), jax.ShapeDtypeStruct((Lp, D), BF16),
                   jax.ShapeDtypeStruct((1, D), F32)],
        scratch_shapes=[pltpu.VMEM((R, D), F32), pltpu.VMEM((R, D), F32)],
        compiler_params=_cparams("arbitrary"),
    )(dz, dz, mixpre, pw, ps)


def _pool_dw(diffb, dmb, name, *, tk):
    Lp, D = diffb.shape
    G = D // N_GROUPS

    def body(a_ref, b_ref, o_ref):
        @pl.when(pl.program_id(1) == 0)
        def _():
            o_ref[...] = jnp.zeros_like(o_ref)

        o_ref[0] += lax.dot_general(a_ref[...], b_ref[...], TN, preferred_element_type=F32)

    blk = pl.BlockSpec((tk, G), lambda g, k: (k, g))
    return pl.pallas_call(
        body, name=name, grid=(N_GROUPS, Lp // tk),
        in_specs=[blk, blk], out_specs=pl.BlockSpec((1, G, G), lambda g, k: (g, 0, 0)),
        out_shape=jax.ShapeDtypeStruct((N_GROUPS, G, G), F32),
        compiler_params=_cparams("parallel", "arbitrary"),
    )(diffb, dmb)


def _ln_fwd(resid, y, g, b, name, *, tm):
    Lp, D = resid.shape

    def body(r_ref, y_ref, g_ref, b_ref, h_ref, hb_ref, xh_ref, rs_ref):
        h, xh, rstd = _ln_math(ALPHA * r_ref[...] + y_ref[...], g_ref[...], b_ref[...])
        h_ref[...] = h
        hb_ref[...] = h.astype(BF16)
        xh_ref[...] = xh
        rs_ref[...] = rstd

    row = pl.BlockSpec((tm, D), lambda i: (i, 0))
    vec = pl.BlockSpec((1, D), lambda i: (0, 0))
    return pl.pallas_call(
        body, name=name, grid=(Lp // tm,),
        in_specs=[row, row, vec, vec],
        out_specs=[row, row, row, pl.BlockSpec((tm, 1), lambda i: (i, 0))],
        out_shape=[jax.ShapeDtypeStruct((Lp, D), F32), jax.ShapeDtypeStruct((Lp, D), BF16),
                   jax.ShapeDtypeStruct((Lp, D), F32), jax.ShapeDtypeStruct((Lp, 1), F32)],
        compiler_params=_cparams("parallel"),
    )(resid, y, g, b)


def _ln_bwd(parts, coefs, xh, rs, g, name, *, tm):
    Lp, D = xh.shape
    n = len(parts)

    def body(*refs):
        part_refs = refs[:n]
        xh_ref, rs_ref, g_ref = refs[n:n + 3]
        dz_ref, dzb_ref, dg_ref, db_ref = refs[n + 3:]
        dy = part_refs[0][...] if coefs[0] == 1.0 else coefs[0] * part_refs[0][...]
        for c, r in zip(coefs[1:], part_refs[1:]):
            dy = dy + (r[...] if c == 1.0 else c * r[...])
        x = xh_ref[...]
        dxh = dy * g_ref[...]
        m1 = jnp.mean(dxh, axis=-1, keepdims=True)
        m2 = jnp.mean(dxh * x, axis=-1, keepdims=True)
        dz = rs_ref[...] * (dxh - m1 - x * m2)
        dz_ref[...] = dz
        dzb_ref[...] = dz.astype(BF16)

        @pl.when(pl.program_id(0) == 0)
        def _():
            dg_ref[...] = jnp.zeros_like(dg_ref)
            db_ref[...] = jnp.zeros_like(db_ref)

        dg_ref[...] += jnp.sum(dy * x, axis=0, keepdims=True)
        db_ref[...] += jnp.sum(dy, axis=0, keepdims=True)

    row = pl.BlockSpec((tm, D), lambda i: (i, 0))
    vec = pl.BlockSpec((1, D), lambda i: (0, 0))
    return pl.pallas_call(
        body, name=name, grid=(Lp // tm,),
        in_specs=[row] * n + [row, pl.BlockSpec((tm, 1), lambda i: (i, 0)), vec],
        out_specs=[row, row, vec, vec],
        out_shape=[jax.ShapeDtypeStruct((Lp, D), F32), jax.ShapeDtypeStruct((Lp, D), BF16),
                   jax.ShapeDtypeStruct((1, D), F32), jax.ShapeDtypeStruct((1, D), F32)],
        compiler_params=_cparams("arbitrary"),
    )(*parts, xh, rs, g)


def _loss_head(h, tgt, name, *, tm, row_lo, row_hi):
    Lp, D = h.shape

    def body(h_ref, t_ref, dy_ref, loss_ref):
        i = pl.program_id(0)
        r = i * tm + lax.broadcasted_iota(jnp.int32, (tm, 1), 0)
        valid = (r >= row_lo) & (r < row_hi)
        e = jnp.where(valid, h_ref[...] - t_ref[...], 0.0)
        dy_ref[...] = e * (1.0 / D)

        @pl.when(i == 0)
        def _():
            loss_ref[...] = jnp.zeros_like(loss_ref)

        loss_ref[...] += 0.5 * jnp.sum(jnp.mean(e * e, axis=-1, keepdims=True), axis=0, keepdims=True)

    row = pl.BlockSpec((tm, D), lambda i: (i, 0))
    return pl.pallas_call(
        body, name=name, grid=(Lp // tm,),
        in_specs=[row, row], out_specs=[row, pl.BlockSpec((1, 1), lambda i: (0, 0))],
        out_shape=[jax.ShapeDtypeStruct((Lp, D), F32), jax.ShapeDtypeStruct((1, 1), F32)],
        compiler_params=_cparams("arbitrary"),
    )(h, tgt)


def _shift_rows_down(cur, prev, s, sub):
    return jnp.where(sub >= s, pltpu.roll(cur, s, 0), pltpu.roll(prev, s, 0))


def _shift_rows_up(cur, nxt, s, sub):
    return jnp.where(sub < SUBLANES - s, pltpu.roll(cur, SUBLANES - s, 0), pltpu.roll(nxt, SUBLANES - s, 0))


def _conv_group(cur, prev, cw_ref, cb_ref, sub):
    taps = [_shift_rows_down(cur, prev, 2, sub), _shift_rows_down(cur, prev, 1, sub), cur]
    c = cb_ref[...] + cw_ref[0:1, :] * taps[0] + cw_ref[1:2, :] * taps[1] + cw_ref[2:3, :] * taps[2]
    return c, taps


def _conv_glu_fwd(u, cw, cb, name, *, tm, tn):
    Lp, F2 = u.shape
    F = F2 // 2
    nj = F // tn
    halo_blocks = tm // SUBLANES
    S8 = SUBLANES
    assert GLU_STRIP == 2 * S8 and tm % GLU_STRIP == 0

    def body(ua_ref, ug_ref, pa_ref, pg_ref, cwa_ref, cwg_ref, cba_ref, cbg_ref, o_ref):
        first = pl.program_id(1) == 0
        sub = lax.broadcasted_iota(jnp.int32, (S8, tn), 0)

        def strip(r, prev_a, prev_g):
            out = []
            for g0 in (0, S8):
                a_cur = ua_ref[pl.ds(r + g0, S8), :]
                g_cur = ug_ref[pl.ds(r + g0, S8), :]
                a, _ = _conv_group(a_cur, prev_a, cwa_ref, cba_ref, sub)
                gate, _ = _conv_group(g_cur, prev_g, cwg_ref, cbg_ref, sub)
                out.append(a * jax.nn.sigmoid(a) * gate)
                prev_a, prev_g = a_cur, g_cur
            o_ref[pl.ds(r, GLU_STRIP), :] = jnp.concatenate(out, axis=0).astype(BF16)

        strip(0, jnp.where(first, 0.0, pa_ref[...]), jnp.where(first, 0.0, pg_ref[...]))

        def step(k, carry):
            r = pl.multiple_of(k * GLU_STRIP, GLU_STRIP)
            before = pl.ds(pl.multiple_of(r - S8, S8), S8)
            strip(r, ua_ref[before, :], ug_ref[before, :])
            return carry

        lax.fori_loop(1, tm // GLU_STRIP, step, 0)

    def prev(off):
        return pl.BlockSpec((SUBLANES, tn), lambda j, i: (jnp.maximum(i * halo_blocks - 1, 0), j + off))

    def cols(rows, off):
        return pl.BlockSpec((rows, tn), lambda j, i: (0, j + off))

    return pl.pallas_call(
        body, name=name, grid=(nj, Lp // tm),
        in_specs=[pl.BlockSpec((tm, tn), lambda j, i: (i, j)), pl.BlockSpec((tm, tn), lambda j, i: (i, j + nj)),
                  prev(0), prev(nj), cols(CONV_WIDTH, 0), cols(CONV_WIDTH, nj), cols(1, 0), cols(1, nj)],
        out_specs=pl.BlockSpec((tm, tn), lambda j, i: (i, j)),
        out_shape=jax.ShapeDtypeStruct((Lp, F), BF16),
        compiler_params=_cparams("parallel", "parallel"),
    )(u, u, u, u, cw, cw, cb, cb)


def _conv_glu_bwd(u, dact, cw, cb, name, *, tm, tn):
    Lp, F2 = u.shape
    F = F2 // 2
    nj = F // tn
    ni = Lp // tm
    halo_blocks = tm // SUBLANES
    n_halo = Lp // SUBLANES
    S8 = SUBLANES
    n_strips = tm // GLU_STRIP
    assert GLU_STRIP == 2 * S8 and tm % GLU_STRIP == 0

    def body(ua_ref, ug_ref, pa_ref, pg_ref, na_ref, ng_ref, da_ref, dn_ref,
             cwa_ref, cwg_ref, cba_ref, cbg_ref,
             dua_ref, dug_ref, dwa_ref, dwg_ref, dba_ref, dbg_ref,
             wacc_a, wacc_g, bacc_a, bacc_g):
        i = pl.program_id(1)
        first, last = i == 0, i == ni - 1
        sub = lax.broadcasted_iota(jnp.int32, (S8, tn), 0)
        for acc in (wacc_a, wacc_g, bacc_a, bacc_g):
            acc[...] = jnp.zeros_like(acc)

        def dconv(a_cur, a_prev, g_cur, g_prev, dact_rows):
            a, taps_a = _conv_group(a_cur, a_prev, cwa_ref, cba_ref, sub)
            gate, taps_g = _conv_group(g_cur, g_prev, cwg_ref, cbg_ref, sub)
            sg = jax.nn.sigmoid(a)
            dca = dact_rows * gate * (sg * (1.0 + a * (1.0 - sg)))
            dcg = dact_rows * (a * sg)
            return dca, dcg, taps_a, taps_g

        def du_group(dc, dc_after, cw_ref):
            return (cw_ref[2:3, :] * dc + cw_ref[1:2, :] * _shift_rows_up(dc, dc_after, 1, sub)
                    + cw_ref[0:1, :] * _shift_rows_up(dc, dc_after, 2, sub))

        def strip(r, a_prev, g_prev, dca_after, dcg_after):
            a0, a1 = ua_ref[pl.ds(r, S8), :], ua_ref[pl.ds(r + S8, S8), :]
            g0, g1 = ug_ref[pl.ds(r, S8), :], ug_ref[pl.ds(r + S8, S8), :]
            dca1, dcg1, ta1, tg1 = dconv(a1, a0, g1, g0, da_ref[pl.ds(r + S8, S8), :])
            dca0, dcg0, ta0, tg0 = dconv(a0, a_prev, g0, g_prev, da_ref[pl.ds(r, S8), :])
            dua_ref[pl.ds(r, GLU_STRIP), :] = jnp.concatenate(
                [du_group(dca0, dca1, cwa_ref), du_group(dca1, dca_after, cwa_ref)], axis=0).astype(BF16)
            dug_ref[pl.ds(r, GLU_STRIP), :] = jnp.concatenate(
                [du_group(dcg0, dcg1, cwg_ref), du_group(dcg1, dcg_after, cwg_ref)], axis=0).astype(BF16)
            for k in range(CONV_WIDTH):
                wacc_a[k] += dca0 * ta0[k] + dca1 * ta1[k]
                wacc_g[k] += dcg0 * tg0[k] + dcg1 * tg1[k]
            bacc_a[...] += dca0 + dca1
            bacc_g[...] += dcg0 + dcg1
            return dca0, dcg0

        tail = pl.ds(tm - S8, S8)
        dca_after, dcg_after, _, _ = dconv(na_ref[...], ua_ref[tail, :], ng_ref[...], ug_ref[tail, :],
                                           jnp.where(last, 0.0, dn_ref[...]))

        def step(t, carry):
            r = pl.multiple_of((n_strips - 1 - t) * GLU_STRIP, GLU_STRIP)
            before = pl.ds(pl.multiple_of(r - S8, S8), S8)
            return strip(r, ua_ref[before, :], ug_ref[before, :], *carry)

        dca_after, dcg_after = lax.fori_loop(0, n_strips - 1, step, (dca_after, dcg_after))
        strip(0, jnp.where(first, 0.0, pa_ref[...]), jnp.where(first, 0.0, pg_ref[...]), dca_after, dcg_after)

        @pl.when(first)
        def _():
            for r in (dwa_ref, dwg_ref, dba_ref, dbg_ref):
                r[...] = jnp.zeros_like(r)

        for wacc, bacc, dw_ref, db_ref in ((wacc_a, bacc_a, dwa_ref, dba_ref), (wacc_g, bacc_g, dwg_ref, dbg_ref)):
            db_ref[...] += jnp.sum(bacc[...], axis=0, keepdims=True)
            for k in range(CONV_WIDTH):
                dw_ref[k:k + 1, :] += jnp.sum(wacc[k], axis=0, keepdims=True)

    def tile(off):
        return pl.BlockSpec((tm, tn), lambda j, i: (i, j + off))

    def prev(off):
        return pl.BlockSpec((S8, tn), lambda j, i: (jnp.maximum(i * halo_blocks - 1, 0), j + off))

    def nxt(off):
        return pl.BlockSpec((S8, tn), lambda j, i: (jnp.minimum((i + 1) * halo_blocks, n_halo - 1), j + off))

    def cols(rows, off):
        return pl.BlockSpec((rows, tn), lambda j, i: (0, j + off))

    return pl.pallas_call(
        body, name=name, grid=(nj, ni),
        in_specs=[tile(0), tile(nj), prev(0), prev(nj), nxt(0), nxt(nj), tile(0), nxt(0),
                  cols(CONV_WIDTH, 0), cols(CONV_WIDTH, nj), cols(1, 0), cols(1, nj)],
        out_specs=[tile(0), tile(0), cols(CONV_WIDTH, 0), cols(CONV_WIDTH, 0), cols(1, 0), cols(1, 0)],
        out_shape=[jax.ShapeDtypeStruct((Lp, F), BF16), jax.ShapeDtypeStruct((Lp, F), BF16),
                   jax.ShapeDtypeStruct((CONV_WIDTH, F), F32), jax.ShapeDtypeStruct((CONV_WIDTH, F), F32),
                   jax.ShapeDtypeStruct((1, F), F32), jax.ShapeDtypeStruct((1, F), F32)],
        scratch_shapes=[pltpu.VMEM((CONV_WIDTH, S8, tn), F32), pltpu.VMEM((CONV_WIDTH, S8, tn), F32),
                        pltpu.VMEM((S8, tn), F32), pltpu.VMEM((S8, tn), F32)],
        compiler_params=_cparams("parallel", "arbitrary"),
    )(u, u, u, u, u, u, dact, dact, cw, cw, cb, cb)


def _logf_cumsum(pre, bf, name, *, tm):
    Lp, W = pre.shape

    def body(p_ref, b_ref, c_ref, carry_ref):
        i = pl.program_id(0)

        @pl.when(i == 0)
        def _():
            carry_ref[...] = jnp.zeros_like(carry_ref)

        x = p_ref[...] + b_ref[...]
        lf = jnp.minimum(x, 0.0) - jnp.log(1.0 + jnp.exp(-jnp.abs(x)))
        tri = (lax.broadcasted_iota(jnp.int32, (tm, tm), 0) >= lax.broadcasted_iota(jnp.int32, (tm, tm), 1)).astype(F32)
        c = jnp.dot(tri, lf, precision=lax.Precision.HIGHEST, preferred_element_type=F32) + carry_ref[...]
        c_ref[...] = c
        carry_ref[...] = c[tm - 1:tm, :]

    row = pl.BlockSpec((tm, W), lambda i: (i, 0))
    return pl.pallas_call(
        body, name=name, grid=(Lp // tm,),
        in_specs=[row, pl.BlockSpec((1, W), lambda i: (0, 0))], out_specs=row,
        out_shape=jax.ShapeDtypeStruct((Lp, W), F32),
        scratch_shapes=[pltpu.VMEM((1, W), F32)],
        compiler_params=_cparams("arbitrary"),
    )(pre, bf)


def _logf_bwd(dc_a, dc_b, pre, bf, name, *, tm):
    Lp, W = pre.shape
    ni = Lp // tm

    def body(dca_ref, dcb_ref, p_ref, b_ref, dpb_ref, db_ref, carry_ref):
        i = pl.program_id(0)

        @pl.when(i == 0)
        def _():
            carry_ref[...] = jnp.zeros_like(carry_ref)
            db_ref[...] = jnp.zeros_like(db_ref)

        triu = (lax.broadcasted_iota(jnp.int32, (tm, tm), 0) <= lax.broadcasted_iota(jnp.int32, (tm, tm), 1)).astype(F32)
        dl = jnp.dot(triu, dca_ref[...] + dcb_ref[...], precision=lax.Precision.HIGHEST,
                     preferred_element_type=F32) + carry_ref[...]
        carry_ref[...] = dl[0:1, :]
        dp = dl * jax.nn.sigmoid(-(p_ref[...] + b_ref[...]))
        dpb_ref[...] = dp.astype(BF16)
        db_ref[...] += jnp.sum(dp, axis=0, keepdims=True)

    rev = pl.BlockSpec((tm, W), lambda i: (ni - 1 - i, 0))
    vec = pl.BlockSpec((1, W), lambda i: (0, 0))
    return pl.pallas_call(
        body, name=name, grid=(ni,),
        in_specs=[rev, rev, rev, vec], out_specs=[rev, vec],
        out_shape=[jax.ShapeDtypeStruct((Lp, W), BF16), jax.ShapeDtypeStruct((1, W), F32)],
        scratch_shapes=[pltpu.VMEM((1, W), F32)],
        compiler_params=_cparams("arbitrary"),
    )(dc_a, dc_b, pre, bf)


def _attn_fwd(qb, kvb, crow4, name, *, tq):
    Lp, D = qb.shape
    H = D // HEAD_DIM
    nq = Lp // tq
    S8 = SUBLANES
    HB = LANES // HEAD_DIM
    n_scratch = 6
    lane_tiles = tq // LANES
    assert HB == 2

    def to_column(row8):
        return jnp.transpose(jnp.concatenate([row8] * (LANES // S8), axis=0))

    def body(q_ref, k_ref, v_ref, c_ref, o_ref, ob_ref, lse_ref, *scratch):
        i = pl.program_id(1)
        heads = [scratch[n_scratch * hb:n_scratch * (hb + 1)] for hb in range(HB)]
        first_head = lax.broadcasted_iota(jnp.int32, (tq, LANES), 1) < HEAD_DIM
        q2 = q_ref[...]
        q_of = [jnp.where(first_head, q2, jnp.zeros_like(q2)), jnp.where(first_head, jnp.zeros_like(q2), q2)]

        @pl.when(i == 0)
        def _():
            for hb, refs in enumerate(heads):
                for j in range(nq):
                    refs[5][j] = to_column(jnp.concatenate([c_ref[hb, j]] * S8, axis=0))

        for m_ref, l_ref, acc_ref, _, _, _ in heads:
            m_ref[...] = jnp.full_like(m_ref, NEG_INF)
            l_ref[...] = jnp.zeros_like(l_ref)
            acc_ref[...] = jnp.zeros_like(acc_ref)

        def chunk(j, masked):
            keys = pl.ds(pl.multiple_of(j * tq, tq), tq)
            k2 = k_ref[keys, :]
            v2 = v_ref[keys, :]
            for hb, (_, _, _, st_ref, _, _) in enumerate(heads):
                st_ref[...] = lax.dot_general(k2, q_of[hb], NT, preferred_element_type=F32)
            for hb, (m_ref, l_ref, acc_ref, st_ref, pt_ref, cs_ref) in enumerate(heads):
                ct = c_ref[hb, i]
                mx = jnp.full((S8, tq), NEG_INF, F32)
                for r0 in range(0, tq, ATTN_STRIP):
                    rows = pl.ds(r0, ATTN_STRIP)
                    cs = jnp.concatenate([cs_ref[j, rows, :]] * lane_tiles, axis=1)
                    st = st_ref[rows, :] + (ct - cs)
                    if masked:
                        keep = (lax.broadcasted_iota(jnp.int32, (ATTN_STRIP, tq), 1)
                                >= r0 + lax.broadcasted_iota(jnp.int32, (ATTN_STRIP, tq), 0))
                        st = jnp.where(keep, st, NEG_INF)
                    st_ref[rows, :] = st
                    for g0 in range(0, ATTN_STRIP, S8):
                        mx = jnp.maximum(mx, st[g0:g0 + S8])
                m_prev = m_ref[...]
                m_new = jnp.maximum(m_prev, jnp.max(mx, axis=0, keepdims=True))
                alpha = jnp.exp(m_prev - m_new)
                m_ref[...] = m_new
                ls = jnp.zeros((S8, tq), F32)
                for r0 in range(0, tq, ATTN_STRIP):
                    pieces = [jnp.exp(st_ref[pl.ds(r0 + g0, S8), :] - m_new) for g0 in range(0, ATTN_STRIP, S8)]
                    for piece in pieces:
                        ls = ls + piece
                    pt_ref[pl.ds(r0, ATTN_STRIP), :] = jnp.concatenate(pieces, axis=0).astype(BF16)
                l_ref[...] = alpha * l_ref[...] + ls
                pv = lax.dot_general(pt_ref[...], v2, TN, preferred_element_type=F32)
                acc_ref[...] = to_column(alpha) * acc_ref[...] + pv

        def step(j, carry):
            chunk(j, False)
            return carry

        lax.fori_loop(0, i, step, 0)
        chunk(i, True)
        outs = []
        for hb, (m_ref, l_ref, acc_ref, _, _, _) in enumerate(heads):
            l_row = jnp.sum(l_ref[...], axis=0, keepdims=True)
            outs.append(acc_ref[...] / to_column(jnp.concatenate([l_row] * S8, axis=0)))
            lse_ref[hb, 0] = m_ref[0:1, :] + jnp.log(l_row)
        o2 = jnp.where(first_head, outs[0], outs[1])
        o_ref[...] = o2
        ob_ref[...] = o2.astype(BF16)

    per_head = [pltpu.VMEM((S8, tq), F32), pltpu.VMEM((S8, tq), F32), pltpu.VMEM((tq, LANES), F32),
                pltpu.VMEM((tq, tq), F32), pltpu.VMEM((tq, tq), BF16), pltpu.VMEM((nq, tq, LANES), F32)]
    assert len(per_head) == n_scratch
    v_blocks = D // LANES
    tile = pl.BlockSpec((tq, LANES), lambda p, i: (i, p))
    return pl.pallas_call(
        body, name=name, grid=(H // HB, nq),
        in_specs=[tile,
                  pl.BlockSpec((Lp, LANES), lambda p, i: (0, p)),
                  pl.BlockSpec((Lp, LANES), lambda p, i: (0, v_blocks + p)),
                  pl.BlockSpec((HB, nq, 1, tq), lambda p, i: (p, 0, 0, 0))],
        out_specs=[tile, tile, pl.BlockSpec((HB, 1, 1, tq), lambda p, i: (p, i, 0, 0))],
        out_shape=[jax.ShapeDtypeStruct((Lp, D), F32), jax.ShapeDtypeStruct((Lp, D), BF16),
                   jax.ShapeDtypeStruct((H, nq, 1, tq), F32)],
        scratch_shapes=per_head * HB,
        compiler_params=_cparams("parallel", "arbitrary"),
    )(qb, kvb, kvb, crow4)


def _attn_delta(do, o, name, *, tm, n_heads):
    Lp, D = do.shape

    def body(do_ref, o_ref, d_ref):
        sel = (lax.broadcasted_iota(jnp.int32, (D, LANES), 0) // HEAD_DIM
               == lax.broadcasted_iota(jnp.int32, (D, LANES), 1)).astype(F32)
        d_ref[...] = jnp.dot(do_ref[...].astype(F32) * o_ref[...], sel, precision=lax.Precision.HIGHEST,
                             preferred_element_type=F32)

    row = pl.BlockSpec((tm, D), lambda i: (i, 0))
    return pl.pallas_call(
        body, name=name, grid=(Lp // tm,),
        in_specs=[row, row], out_specs=pl.BlockSpec((tm, LANES), lambda i: (i, 0)),
        out_shape=jax.ShapeDtypeStruct((Lp, LANES), F32),
        compiler_params=_cparams("parallel"),
    )(do, o)


def _attn_bwd(qb, dob, kvb, lse4, delta4, crow4, name, *, tq):
    Lp, D = qb.shape
    H = D // HEAD_DIM
    nq = Lp // tq
    HB = LANES // HEAD_DIM
    lane_tiles = tq // LANES
    n_scratch = 8
    assert HB == 2

    def body(q_ref, do_ref, k_ref, v_ref, lse_ref, dl_ref, c_ref,
             dqb_ref, dk_ref, dv_ref, dcs_ref, dcq_ref, dq_ref, *scratch):
        j = pl.program_id(1)
        heads = [scratch[n_scratch * hb:n_scratch * (hb + 1)] for hb in range(HB)]
        first_head = lax.broadcasted_iota(jnp.int32, (tq, LANES), 1) < HEAD_DIM

        def split(x2):
            zero = jnp.zeros_like(x2)
            return [jnp.where(first_head, x2, zero), jnp.where(first_head, zero, x2)]

        @pl.when(j == 0)
        def _():
            dq_ref[...] = jnp.zeros_like(dq_ref)
            dcq_ref[...] = jnp.zeros_like(dcq_ref)

        k2 = k_ref[...]
        v2 = v_ref[...]
        for hb, (dk_acc, dv_acc, dc_acc, _, _, _, _, cs_ref) in enumerate(heads):
            dk_acc[...] = jnp.zeros_like(dk_acc)
            dv_acc[...] = jnp.zeros_like(dv_acc)
            dc_acc[...] = jnp.zeros_like(dc_acc)
            cs_ref[...] = jnp.transpose(jnp.broadcast_to(c_ref[hb, j], (LANES, tq)))

        def pair(i, masked):
            queries = pl.ds(pl.multiple_of(i * tq, tq), tq)
            q2 = q_ref[queries, :]
            do2 = do_ref[queries, :]
            q_of, do_of = split(q2), split(do2)
            for hb, (_, _, _, st_ref, dp_ref, _, _, _) in enumerate(heads):
                st_ref[...] = lax.dot_general(k2, q_of[hb], NT, preferred_element_type=F32)
                dp_ref[...] = lax.dot_general(v2, do_of[hb], NT, preferred_element_type=F32)
            dq_parts = []
            for hb, (dk_acc, dv_acc, dc_acc, st_ref, dp_ref, pt_ref, ds_ref, cs_ref) in enumerate(heads):
                bias_q = c_ref[hb, i] - lse_ref[hb, i]
                delta = dl_ref[hb, i]
                col_sum = jnp.zeros((SUBLANES, tq), F32)
                for r0 in range(0, tq, ATTN_STRIP):
                    rows = pl.ds(r0, ATTN_STRIP)
                    st = st_ref[rows, :] + (bias_q - jnp.concatenate([cs_ref[rows, :]] * lane_tiles, axis=1))
                    if masked:
                        keep = (lax.broadcasted_iota(jnp.int32, (ATTN_STRIP, tq), 1)
                                >= r0 + lax.broadcasted_iota(jnp.int32, (ATTN_STRIP, tq), 0))
                        st = jnp.where(keep, st, NEG_INF)
                    pt = jnp.exp(st)
                    dst = pt * (dp_ref[rows, :] - delta)
                    pt_ref[rows, :] = pt.astype(BF16)
                    ds_ref[rows, :] = dst.astype(BF16)
                    dc_acc[rows, :] += jnp.sum(dst, axis=1, keepdims=True)
                    for g0 in range(0, ATTN_STRIP, SUBLANES):
                        col_sum = col_sum + dst[g0:g0 + SUBLANES]
                dcq_ref[hb, i] += jnp.sum(col_sum, axis=0, keepdims=True)
                dv_acc[...] += jnp.dot(pt_ref[...], do2, preferred_element_type=F32)
                dk_acc[...] += jnp.dot(ds_ref[...], q2, preferred_element_type=F32)
                dq_parts.append(lax.dot_general(ds_ref[...], k2, TN, preferred_element_type=F32))
            dq_ref[queries, :] += jnp.where(first_head, dq_parts[0], dq_parts[1])

        def step(i, carry):
            pair(i, False)
            return carry

        pair(j, True)
        lax.fori_loop(j + 1, nq, step, 0)
        dk_ref[...] = jnp.where(first_head, heads[0][0][...], heads[1][0][...]).astype(BF16)
        dv_ref[...] = jnp.where(first_head, heads[0][1][...], heads[1][1][...]).astype(BF16)
        for hb in range(HB):
            dcs_ref[hb] = -heads[hb][2][...]

        @pl.when(j == nq - 1)
        def _():
            dqb_ref[...] = dq_ref[...].astype(BF16)

    per_head = [pltpu.VMEM((tq, LANES), F32), pltpu.VMEM((tq, LANES), F32), pltpu.VMEM((tq, 1), F32),
                pltpu.VMEM((tq, tq), F32), pltpu.VMEM((tq, tq), F32),
                pltpu.VMEM((tq, tq), BF16), pltpu.VMEM((tq, tq), BF16), pltpu.VMEM((tq, LANES), F32)]
    assert len(per_head) == n_scratch
    v_blocks = D // LANES
    whole = pl.BlockSpec((Lp, LANES), lambda p, j: (0, p))
    tile = pl.BlockSpec((tq, LANES), lambda p, j: (j, p))
    rows = pl.BlockSpec((HB, nq, 1, tq), lambda p, j: (p, 0, 0, 0))
    return pl.pallas_call(
        body, name=name, grid=(H // HB, nq),
        in_specs=[whole, whole, tile, pl.BlockSpec((tq, LANES), lambda p, j: (j, v_blocks + p)), rows, rows, rows],
        out_specs=[whole, tile, tile, pl.BlockSpec((HB, tq, 1), lambda p, j: (p, j, 0)), rows],
        out_shape=[jax.ShapeDtypeStruct((Lp, D), BF16), jax.ShapeDtypeStruct((Lp, D), BF16),
                   jax.ShapeDtypeStruct((Lp, D), BF16), jax.ShapeDtypeStruct((H, Lp, 1), F32),
                   jax.ShapeDtypeStruct((H, nq, 1, tq), F32)],
        scratch_shapes=[pltpu.VMEM((Lp, LANES), F32)] + per_head * HB,
        compiler_params=_cparams("parallel", "arbitrary"),
    )(qb, dob, kvb, kvb, lse4, delta4, crow4)


def _remote(src, dst, send_sems, recv_sems, k, to):
    return pltpu.make_async_remote_copy(src_ref=src, dst_ref=dst, send_sem=send_sems.at[k], recv_sem=recv_sems.at[k],
                                        device_id=to, device_id_type=MESH)


def _place():
    x, y, c = lax.axis_index("x"), lax.axis_index("y"), lax.axis_index("c")
    other_chips = [(1 - x, y), (x, 1 - y), (1 - x, 1 - y)]
    return x, y, c, other_chips


def _all_gather_weights(wb, wf, name):
    Rb, C = wb.shape
    Rf = wf.shape[0]
    hb = Rb // 2

    def body(wb_ref, wf_ref, ob_ref, of_ref, send_sems, recv_sems):
        x, y, c, chips = _place()
        me = 2 * x + y
        sibling = (x, y, 1 - c)

        def half(chip, core):
            return ob_ref.at[chip, pl.ds(core * hb, hb), :]

        sent = []
        for j, (cx, cy) in enumerate(chips):
            sent.append(_remote(wb_ref.at[pl.ds(c * hb, hb), :], half(me, c), send_sems, recv_sems, j, (cx, cy, c)))
            sent.append(_remote(wf_ref, of_ref.at[me], send_sems, recv_sems, 3 + j, (cx, cy, c)))
        for cp in sent:
            cp.start()
        for j, (cx, cy) in enumerate(chips):
            chip = 2 * cx + cy
            _remote(half(chip, c), half(chip, c), send_sems, recv_sems, j, sibling).wait_recv()
            fwd = _remote(half(chip, c), half(chip, c), send_sems, recv_sems, 6 + j, sibling)
            fwd.start()
            sent.append(fwd)
        for j, (cx, cy) in enumerate(chips):
            chip = 2 * cx + cy
            _remote(wf_ref, of_ref.at[chip], send_sems, recv_sems, 3 + j, sibling).wait_recv()
            _remote(half(chip, 1 - c), half(chip, 1 - c), send_sems, recv_sems, 6 + j, sibling).wait_recv()
        for cp in sent:
            cp.wait_send()

    any_spec = pl.BlockSpec(memory_space=pl.ANY)
    return pl.pallas_call(
        body, name=name,
        in_specs=[any_spec, any_spec], out_specs=[any_spec, any_spec],
        out_shape=[jax.ShapeDtypeStruct((N_CHIPS, Rb, C), BF16), jax.ShapeDtypeStruct((N_CHIPS, Rf, C), F32)],
        scratch_shapes=[pltpu.SemaphoreType.DMA((9,)), pltpu.SemaphoreType.DMA((9,))],
    )(wb, wf)


def _half_of(ref, order, half):
    return ref.at[pl.ds(0, N_CHIPS), half] if order == "CH" else ref.at[half]


def _halves_to_sibling(grads, orders, name):
    n = len(grads)

    def body(*refs):
        g_refs, a_refs, (send_sems, recv_sems) = refs[:n], refs[n:2 * n], refs[2 * n:]
        x, y, c, _ = _place()
        copies = [_remote(_half_of(g, o, 1 - c), a, send_sems, recv_sems, k, (x, y, 1 - c))
                  for k, (g, a, o) in enumerate(zip(g_refs, a_refs, orders))]
        for cp in copies:
            cp.start()
        for cp in copies:
            cp.wait()

    any_spec = pl.BlockSpec(memory_space=pl.ANY)
    shapes = [g.shape[2:] for g in grads]
    return pl.pallas_call(
        body, name=name, in_specs=[any_spec] * n, out_specs=[any_spec] * n,
        out_shape=[jax.ShapeDtypeStruct((N_CHIPS,) + s, F32) for s in shapes],
        scratch_shapes=[pltpu.SemaphoreType.DMA((n,)), pltpu.SemaphoreType.DMA((n,))],
    )(*grads)


def _chip_partial(g, a, core, order, wire, name, *, tr):
    _, R, C = a.shape
    narrow = wire != F32

    def body(core_ref, g_ref, a_ref, *outs):
        p = g_ref[0, 0] + a_ref[0]
        outs[0][0] = p
        if narrow:
            outs[1][0] = p.astype(wire)

    if order == "CH":
        g_spec = pl.BlockSpec((1, 1, tr, C), lambda s, i, core_ref: (s, core_ref[0], i, 0))
    else:
        g_spec = pl.BlockSpec((1, 1, tr, C), lambda s, i, core_ref: (core_ref[0], s, i, 0))
    blk = pl.BlockSpec((1, tr, C), lambda s, i, core_ref: (s, i, 0))
    grid_spec = pltpu.PrefetchScalarGridSpec(
        num_scalar_prefetch=1, grid=(N_CHIPS, R // tr), in_specs=[g_spec, blk],
        out_specs=[blk, blk] if narrow else [blk])
    out_shape = [jax.ShapeDtypeStruct((N_CHIPS, R, C), F32)] + ([jax.ShapeDtypeStruct((N_CHIPS, R, C), wire)] if narrow else [])
    outs = pl.pallas_call(body, name=name, grid_spec=grid_spec, out_shape=out_shape,
                          compiler_params=_cparams("parallel", "parallel"))(core, g, a)
    return outs[0], outs[-1]


def _chip_exchange(parts, rep, name):
    n = len(parts)
    rr, C = rep.shape

    def body(*refs):
        p_refs, rep_ref = refs[:n], refs[n]
        land_refs, reps_ref = refs[n + 1:2 * n + 1], refs[2 * n + 1]
        send_sems, recv_sems, local_sem = refs[2 * n + 2:]
        x, y, c, chips = _place()
        me = 4 * x + 2 * y + c
        own = pltpu.make_async_copy(rep_ref, reps_ref.at[me], local_sem.at[0])
        own.start()
        sent = []
        for k, (p, land) in enumerate(zip(p_refs, land_refs)):
            for j, (cx, cy) in enumerate(chips):
                sent.append(_remote(p.at[2 * cx + cy], land.at[j], send_sems, recv_sems, 3 * k + j, (cx, cy, c)))
        for r in range(1, N_DEV):
            fx, fy, fc = (r >> 2) & 1, (r >> 1) & 1, r & 1
            sent.append(_remote(rep_ref, reps_ref.at[me], send_sems, recv_sems, 3 * n - 1 + r, (x ^ fx, y ^ fy, c ^ fc)))
        for cp in sent:
            cp.start()
        for k, (p, land) in enumerate(zip(p_refs, land_refs)):
            for j in range(3):
                _remote(p.at[0], land.at[j], send_sems, recv_sems, 3 * k + j, (x, y, c)).wait_recv()
        for r in range(1, N_DEV):
            fx, fy, fc = (r >> 2) & 1, (r >> 1) & 1, r & 1
            frm = 4 * (x ^ fx) + 2 * (y ^ fy) + (c ^ fc)
            _remote(rep_ref, reps_ref.at[frm], send_sems, recv_sems, 3 * n - 1 + r, (x, y, c)).wait_recv()
        for cp in sent:
            cp.wait_send()
        own.wait()

    any_spec = pl.BlockSpec(memory_space=pl.ANY)
    n_sems = 3 * n + N_DEV - 1
    return pl.pallas_call(
        body, name=name, in_specs=[any_spec] * (n + 1), out_specs=[any_spec] * (n + 1),
        out_shape=[jax.ShapeDtypeStruct((3,) + p.shape[1:], p.dtype) for p in parts]
        + [jax.ShapeDtypeStruct((N_DEV, rr, C), F32)],
        scratch_shapes=[pltpu.SemaphoreType.DMA((n_sems,)), pltpu.SemaphoreType.DMA((n_sems,)),
                        pltpu.SemaphoreType.DMA((1,))],
    )(*parts, rep)


def _adamw_math(w, g, m, v):
    m = ADAM_B1 * m + (1.0 - ADAM_B1) * g
    v = ADAM_B2 * v + (1.0 - ADAM_B2) * (g * g)
    m_hat = m / (1.0 - ADAM_B1 ** ADAM_STEP)
    v_hat = v / (1.0 - ADAM_B2 ** ADAM_STEP)
    delta = -ADAM_LR * (m_hat / (jnp.sqrt(v_hat) + ADAM_EPS) + ADAM_WD * w)
    return delta, m, v


def _adamw_owned(part, landed, w, m, v, place, name, *, tr):
    _, R, C = part.shape

    def body(place_ref, own_ref, land_ref, w_ref, m_ref, v_ref, g_ref, d_ref, mo_ref, vo_ref):
        g = own_ref[0]
        for s in range(3):
            g = g + land_ref[s].astype(F32)
        delta, m_new, v_new = _adamw_math(w_ref[0], g, m_ref[0], v_ref[0])
        g_ref[0] = g
        d_ref[0] = delta
        mo_ref[0] = m_new
        vo_ref[0] = v_new

    half = pl.BlockSpec((1, tr, C), lambda i, place_ref: (place_ref[0], i, 0))
    grid_spec = pltpu.PrefetchScalarGridSpec(
        num_scalar_prefetch=1, grid=(R // tr,),
        in_specs=[pl.BlockSpec((1, tr, C), lambda i, place_ref: (place_ref[1], i, 0)),
                  pl.BlockSpec((3, tr, C), lambda i, place_ref: (0, i, 0)), half, half, half],
        out_specs=[half] * 4)
    return pl.pallas_call(
        body, name=name, grid_spec=grid_spec, out_shape=[jax.ShapeDtypeStruct((2, R, C), F32)] * 4,
        compiler_params=_cparams("parallel"),
    )(place, part, landed, w, m, v)


def _join_halves(bufs, name):
    n = len(bufs)

    def body(*refs):
        out_refs, (send_sems, recv_sems) = refs[n:2 * n], refs[2 * n:]
        x, y, c, _ = _place()
        copies = [_remote(o.at[c], o.at[c], send_sems, recv_sems, k, (x, y, 1 - c)) for k, o in enumerate(out_refs)]
        for cp in copies:
            cp.start()
        for k, o in enumerate(out_refs):
            _remote(o.at[c], o.at[1 - c], send_sems, recv_sems, k, (x, y, 1 - c)).wait_recv()
        for cp in copies:
            cp.wait_send()

    any_spec = pl.BlockSpec(memory_space=pl.ANY)
    return pl.pallas_call(
        body, name=name, in_specs=[any_spec] * n, out_specs=[any_spec] * n,
        out_shape=[jax.ShapeDtypeStruct(b.shape, b.dtype) for b in bufs],
        input_output_aliases={k: k for k in range(n)},
        scratch_shapes=[pltpu.SemaphoreType.DMA((n,)), pltpu.SemaphoreType.DMA((n,))],
    )(*bufs)


def _sum_adamw(own, landed, w, m, v, name, *, tr):
    n = landed.shape[0]
    hr, C = own.shape

    def body(own_ref, land_ref, w_ref, m_ref, v_ref, o_ref):
        g = own_ref[...]
        for s in range(n):
            g = g + land_ref[s]
        delta, m_new, v_new = _adamw_math(w_ref[...], g, m_ref[...], v_ref[...])
        o_ref[0] = g
        o_ref[1] = delta
        o_ref[2] = m_new
        o_ref[3] = v_new

    blk = pl.BlockSpec((tr, C), lambda i: (i, 0))
    return pl.pallas_call(
        body, name=name, grid=(hr // tr,),
        in_specs=[blk, pl.BlockSpec((n, tr, C), lambda i: (0, i, 0)), blk, blk, blk],
        out_specs=pl.BlockSpec((4, tr, C), lambda i: (0, i, 0)),
        out_shape=jax.ShapeDtypeStruct((4, hr, C), F32), compiler_params=_cparams("parallel"),
    )(own, landed, w, m, v)


def _rows_of(shape):
    n = 1
    for d in shape:
        n *= d
    return -(-n // PACK_COLS)


def _pack(arrays, total_rows, dtype):
    parts, used = [], 0
    for a in arrays:
        flat = a.reshape(-1).astype(dtype)
        fill = _rows_of(a.shape) * PACK_COLS - flat.shape[0]
        parts += [flat] + ([jnp.zeros((fill,), dtype)] if fill else [])
        used += _rows_of(a.shape)
    if total_rows > used:
        parts.append(jnp.zeros(((total_rows - used) * PACK_COLS,), dtype))
    return jnp.concatenate(parts).reshape(total_rows, PACK_COLS)


def _unpack(buf, shapes):
    lead = buf.shape[:-2]
    out, r = [], 0
    for shp in shapes:
        n = 1
        for d in shp:
            n *= d
        rows = _rows_of(shp)
        piece = buf[..., r:r + rows, :].reshape(lead + (rows * PACK_COLS,))[..., :n]
        out.append(piece.reshape(lead + tuple(shp)))
        r += rows
    return out


def _join_shards(stacked, axis):
    return jnp.concatenate([stacked[s] for s in range(N_CHIPS)], axis=axis)


def _shard_of(full, axis, chip):
    width = full.shape[axis] // N_CHIPS
    return lax.slice_in_dim(full, chip * width, (chip + 1) * width, axis=axis)


def _local_step(h0, tgt, W, *, seq, tm):
    Lp, D = h0.shape
    H = D // HEAD_DIM
    F2 = W["ffn_w_in"].shape[-1]
    F = F2 // 2
    te = tm // 2
    nq = Lp // tm
    cap = 1408
    tD, tF, tF2 = _pick(D, cap), _pick(F, cap), _pick(F2, cap)
    t2D = _pick(2 * D, cap)
    t2Dc, tF2c = _pick(2 * D // N_CHIPS, cap), _pick(F2 // N_CHIPS, cap)
    tcn = _pick(F, cap)

    def vec(a):
        return a.reshape(1, -1)

    ln_g, ln_b = W["ln_g"], W["ln_b"]
    wf_pad = jnp.pad(W["w_f"], ((0, 0), (0, LANES - H)))
    bf_pad = jnp.pad(W["b_f"], (0, LANES - H)).reshape(1, LANES)

    def ffn_fwd(hb, l, tag):
        u = _mm(hb, W["ffn_w_in"][l], "nn", F32, f"ffn{tag}_up", tm=tm, tn=tF2, tk=tD)
        act = _conv_glu_fwd(u, W["ffn_conv_w"][l], vec(W["ffn_conv_b"][l]), f"ffn{tag}_glu", tm=te, tn=tcn)
        y = _mm(act, W["ffn_w_out"][l], "nn", F32, f"ffn{tag}_down", tm=tm, tn=tD, tk=tF)
        return u, act, y

    def ffn_bwd(dzb, hb, u, act, l, tag, dw_in_acc, dw_out_acc):
        dact = _mm(dzb, W["ffn_w_out"][l], "nt", F32, f"ffn{tag}_dact", tm=tm, tn=tF, tk=tD)
        dw_out = _mm(act, dzb, "tn", F32, f"ffn{tag}_dwout", tm=tF, tn=tD, tk=tm, layer=l, into=dw_out_acc)
        dua, dug, dwa, dwg, dba, dbg = _conv_glu_bwd(u, dact, W["ffn_conv_w"][l], vec(W["ffn_conv_b"][l]),
                                                     f"ffn{tag}_dglu", tm=te, tn=tcn)
        du = jnp.concatenate([dua, dug], axis=1)
        dcw = jnp.concatenate([dwa, dwg], axis=1)
        dcb = jnp.concatenate([dba, dbg], axis=1)
        dh = _mm(du, W["ffn_w_in"][l], "nt", F32, f"ffn{tag}_dh", tm=tm, tn=tD, tk=tF2)
        dw_in = _mm(hb, du, "tn", F32, f"ffn{tag}_dwin", tm=tD, tn=tF2c, tk=tm, chips=True, layer=l, into=dw_in_acc)
        return dh, dw_in, dw_out, dcw, dcb[0]

    diffb, mixpre, h1, h1b, xh1, rs1 = _pool_ln_fwd(h0, W["pool_w"][0], W["pool_scale"], vec(ln_g[0, 0]),
                                                    vec(ln_b[0, 0]), "pool_ln_fwd", tm=te)
    u0, act0, y0 = ffn_fwd(h1b, 0, "0")
    h2, h2b, xh2, rs2 = _ln_fwd(h1, y0, vec(ln_g[0, 1]), vec(ln_b[0, 1]), "ln01_fwd", tm=te)

    kvb = _mm(h2b, W["w_kv"], "nn", BF16, "kv_proj", tm=tm, tn=t2D, tk=tD)
    qb = _mm(h2b, W["w_q"][0], "nn", BF16, "q_proj", tm=tm, tn=tD, tk=tD, scale=HEAD_DIM ** -0.5)
    pre = _mm(h2b, wf_pad, "nn", F32, "f_proj", tm=tm, tn=LANES, tk=tD)
    c = _logf_cumsum(pre, bf_pad, "logf_cumsum", tm=tm)

    crow4 = c[:, :H].T.reshape(H, nq, 1, tm)
    o_tok, ob, lse4 = _attn_fwd(qb, kvb, crow4, "attn_fwd", tq=tm)
    y_attn = _mm(ob, W["w_o"][0], "nn", F32, "o_proj", tm=tm, tn=tD, tk=tD)
    h3, h3b, xh3, rs3 = _ln_fwd(h2, y_attn, vec(ln_g[1, 0]), vec(ln_b[1, 0]), "ln10_fwd", tm=te)
    u1, act1, y1 = ffn_fwd(h3b, 1, "1")
    h4, _, xh4, rs4 = _ln_fwd(h3, y1, vec(ln_g[1, 1]), vec(ln_b[1, 1]), "ln11_fwd", tm=te)
    dy, loss = _loss_head(h4, tgt, "loss_head", tm=te, row_lo=N_META, row_hi=N_META + seq)

    dz4, dz4b, dg11, db11 = _ln_bwd([dy], [1.0], xh4, rs4, vec(ln_g[1, 1]), "ln11_bwd", tm=te)
    dh3, dw_in, dw_out, dcw1, dcb1 = ffn_bwd(dz4b, h3b, u1, act1, 1, "1", None, None)
    dz3, dz3b, dg10, db10 = _ln_bwd([dz4, dh3], [ALPHA, 1.0], xh3, rs3, vec(ln_g[1, 0]), "ln10_bwd", tm=te)

    dob = _mm(dz3b, W["w_o"][0], "nt", BF16, "o_proj_dx", tm=tm, tn=tD, tk=tD)
    dw_o = _mm(ob, dz3b, "tn", F32, "o_proj_dw", tm=tD, tn=tD, tk=tm)
    delta = _attn_delta(dob, o_tok, "attn_delta", tm=te, n_heads=H)
    dqb, dkb, dvb, dcs, dcq = _attn_bwd(qb, dob, kvb, lse4, delta[:, :H].T.reshape(H, nq, 1, tm), crow4,
                                        "attn_bwd", tq=tm)
    dc_keys = jnp.pad(dcs.reshape(H, Lp).T, ((0, 0), (0, LANES - H)))
    dc_queries = jnp.pad(dcq.reshape(H, Lp).T, ((0, 0), (0, LANES - H)))
    dpreb, dbf = _logf_bwd(dc_keys, dc_queries, pre, bf_pad, "logf_bwd", tm=tm)

    qs = HEAD_DIM ** -0.5
    dw_q = _mm(h2b, dqb, "tn", F32, "q_proj_dw", tm=tD, tn=tD, tk=tm, scale=qs)
    dw_kv = _mm(h2b, dkb, "tn", F32, "k_proj_dw", tm=tD, tn=t2Dc, tk=tm, chips=(0, N_CHIPS // 2))
    dw_kv = _mm(h2b, dvb, "tn", F32, "v_proj_dw", tm=tD, tn=t2Dc, tk=tm, chips=(N_CHIPS // 2, N_CHIPS // 2), into=dw_kv)
    dw_f = _mm(h2b, dpreb, "tn", F32, "f_proj_dw", tm=tD, tn=LANES, tk=tm)[:, :H]
    dh2 = _mm(dqb, W["w_q"][0], "nt", F32, "q_proj_dx", tm=tm, tn=tD, tk=tD, scale=qs)
    dh2 = _mm(dkb, W["w_kv"][:, :D], "nt", F32, "k_proj_dx", tm=tm, tn=tD, tk=tD, add=dh2)
    dh2 = _mm(dvb, W["w_kv"][:, D:], "nt", F32, "v_proj_dx", tm=tm, tn=tD, tk=tD, add=dh2)
    dh2 = _mm(dpreb, wf_pad, "nt", F32, "f_proj_dx", tm=tm, tn=tD, tk=LANES, add=dh2)
    dz2, dz2b, dg01, db01 = _ln_bwd([dz3, dh2], [ALPHA, 1.0], xh2, rs2, vec(ln_g[0, 1]), "ln01_bwd", tm=te)

    dh1, dw_in, dw_out, dcw0, dcb0 = ffn_bwd(dz2b, h1b, u0, act0, 0, "0", dw_in, dw_out)
    dz1, _, dg00, db00 = _ln_bwd([dz2, dh1], [ALPHA, 1.0], xh1, rs1, vec(ln_g[0, 0]), "ln00_bwd", tm=te)
    dh0, dmb, dscale = _pool_bwd(dz1, mixpre, W["pool_w"][0], W["pool_scale"], "pool_bwd", tm=te)
    dw_pool = _pool_dw(diffb, dmb, "pool_dw", tk=tm)

    grads = {
        "meta": dh0[:N_META],
        "pool_w": dw_pool[None],
        "pool_scale": dscale,
        "w_kv": dw_kv,
        "w_f": dw_f,
        "b_f": dbf[0, :H],
        "w_q": dw_q[None],
        "w_o": dw_o[None],
        "ffn_w_in": dw_in,
        "ffn_conv_w": jnp.stack([dcw0, dcw1]),
        "ffn_conv_b": jnp.stack([dcb0, dcb1]),
        "ffn_w_out": dw_out,
        "ln_g": jnp.stack([jnp.stack([dg00[0], dg01[0]]), jnp.stack([dg10[0], dg11[0]])]),
        "ln_b": jnp.stack([jnp.stack([db00[0], db01[0]]), jnp.stack([db10[0], db11[0]])]),
    }
    return loss, dh0, grads


def _row_block(rows, cols):
    best = SUBLANES
    for t in range(SUBLANES, rows + 1, SUBLANES):
        if rows % t == 0 and t * cols * 4 <= ELEMENTWISE_BLOCK_BYTES:
            best = t
    return best


def _row_tile(length):
    return 640 if length >= 4096 else 128


def kernel(x, meta, pool_w, pool_scale, w_kv, w_f, b_f, w_q, w_o, ffn_w_in, ffn_conv_w, ffn_conv_b, ffn_w_out, ln_g, ln_b, loss_target, m_meta, m_pool_w, m_pool_scale, m_w_kv, m_w_f, m_b_f, m_w_q, m_w_o, m_ffn_w_in, m_ffn_conv_w, m_ffn_conv_b, m_ffn_w_out, m_ln_g, m_ln_b, v_meta, v_pool_w, v_pool_scale, v_w_kv, v_w_f, v_b_f, v_w_q, v_w_o, v_ffn_w_in, v_ffn_conv_w, v_ffn_conv_b, v_ffn_w_out, v_ln_g, v_ln_b):
    weights = dict(meta=meta, pool_w=pool_w, pool_scale=pool_scale, w_kv=w_kv, w_f=w_f, b_f=b_f, w_q=w_q, w_o=w_o,
                   ffn_w_in=ffn_w_in, ffn_conv_w=ffn_conv_w, ffn_conv_b=ffn_conv_b, ffn_w_out=ffn_w_out,
                   ln_g=ln_g, ln_b=ln_b)
    mom1 = dict(meta=m_meta, pool_w=m_pool_w, pool_scale=m_pool_scale, w_kv=m_w_kv, w_f=m_w_f, b_f=m_b_f, w_q=m_w_q,
                w_o=m_w_o, ffn_w_in=m_ffn_w_in, ffn_conv_w=m_ffn_conv_w, ffn_conv_b=m_ffn_conv_b,
                ffn_w_out=m_ffn_w_out, ln_g=m_ln_g, ln_b=m_ln_b)
    mom2 = dict(meta=v_meta, pool_w=v_pool_w, pool_scale=v_pool_scale, w_kv=v_w_kv, w_f=v_w_f, b_f=v_b_f, w_q=v_w_q,
                w_o=v_w_o, ffn_w_in=v_ffn_w_in, ffn_conv_w=v_ffn_conv_w, ffn_conv_b=v_ffn_conv_b,
                ffn_w_out=v_ffn_w_out, ln_g=v_ln_g, ln_b=v_ln_b)
    _, seq, D = x.shape
    L = N_META + seq
    tm = _row_tile(L)
    Lp = _round_up(L, tm)
    c_idx = lax.axis_index("c")
    chip = 2 * lax.axis_index("x") + lax.axis_index("y")

    shard_shapes = {n: weights[n].shape for n in SHARDED}
    rows_b = _round_up(sum(_rows_of(shard_shapes[n]) for n in MATMUL_WEIGHTS), 32)
    rows_f = _round_up(sum(_rows_of(shard_shapes[n]) for n in VECTOR_WEIGHTS), SUBLANES)
    wb = _pack([weights[n] for n in MATMUL_WEIGHTS], rows_b, BF16)
    wf = _pack([weights[n] for n in VECTOR_WEIGHTS], rows_f, F32)
    gb, gf = _all_gather_weights(wb, wf, "weights_all_gather")
    gb = lax.dynamic_update_index_in_dim(gb, wb, chip, axis=0)
    gf = lax.dynamic_update_index_in_dim(gf, wf, chip, axis=0)
    full = {}
    for names, buf in ((MATMUL_WEIGHTS, gb), (VECTOR_WEIGHTS, gf)):
        for n, stacked in zip(names, _unpack(buf, [shard_shapes[n] for n in names])):
            full[n] = _join_shards(stacked, SHARD_AXIS[n])
    full["b_f"] = b_f
    full["ffn_conv_b"] = ffn_conv_b

    pad = jnp.zeros((Lp - L, D), F32)
    h0 = jnp.concatenate([full["meta"], x[0], pad], axis=0)
    tgt = jnp.concatenate([jnp.zeros((N_META, D), F32), loss_target[0], pad], axis=0)
    loss, dh0, grads = _local_step(h0, tgt, full, seq=seq, tm=tm)
    loss = lax.psum(loss[0, 0], AXES)
    grad_x = dh0[N_META:L][None]

    core = c_idx.astype(jnp.int32).reshape(1)
    place = jnp.stack([c_idx, chip]).astype(jnp.int32)
    small_shapes = [shard_shapes[n] for n in SMALL_SHARDED]
    rows_s = _round_up(sum(_rows_of(s) for s in small_shapes), 2 * LANES)

    def packed_small(d):
        return _pack([d[n] for n in SMALL_SHARDED], rows_s, F32).reshape(2, rows_s // 2, PACK_COLS)

    names, orders, wires, g_views, wmv = [], [], [], [], []
    for n, order in BIG_SHARDED:
        shp = shard_shapes[n]
        C = shp[-1]
        R = weights[n].size // C // 2
        lead = (N_CHIPS, 2) if order == "CH" else (2, N_CHIPS)
        names.append(n)
        orders.append(order)
        wires.append(BF16)
        g_views.append(grads[n].reshape(lead + (R, C)))
        wmv.append([d[n].reshape(2, R, C) for d in (weights, mom1, mom2)])
    names.append("small")
    orders.append("CH")
    wires.append(F32)
    g_views.append(jnp.stack([_pack([_shard_of(grads[n], SHARD_AXIS[n], s) for n in SMALL_SHARDED], rows_s, F32)
                              for s in range(N_CHIPS)]).reshape(N_CHIPS, 2, rows_s // 2, PACK_COLS))
    wmv.append([packed_small(d) for d in (weights, mom1, mom2)])

    from_sibling = _halves_to_sibling(g_views, orders, "grads_to_sibling")
    parts, on_wire = [], []
    for n, order, wire, g, a in zip(names, orders, wires, g_views, from_sibling):
        p, pw = _chip_partial(g, a, core, order, wire, f"chip_sum_{n}", tr=_row_block(a.shape[1], a.shape[2]))
        parts.append(p)
        on_wire.append(pw)

    rep_shapes = [weights[n].shape for n in REPLICATED]
    rows_r = _round_up(sum(_rows_of(s) for s in rep_shapes), SUBLANES)
    rep = _pack([grads[n] for n in REPLICATED], rows_r, F32)
    *landed, reps = _chip_exchange(on_wire, rep, "grads_chip_exchange")

    halves = []
    for n, p, b, (w_, m_, v_) in zip(names, parts, landed, wmv):
        halves += _adamw_owned(p, b, w_, m_, v_, place, f"adamw_{n}", tr=_row_block(p.shape[1], p.shape[2]))
    joined = _join_halves(halves, "results_to_sibling")
    out = {}
    for k, n in enumerate(names[:-1]):
        out[n] = [a.reshape(shard_shapes[n]) for a in joined[4 * k:4 * k + 4]]
    small_out = [_unpack(a.reshape(rows_s, PACK_COLS), small_shapes) for a in joined[-4:]]
    for k, n in enumerate(SMALL_SHARDED):
        out[n] = [small_out[kind][k] for kind in range(4)]

    def packr(d):
        return _pack([d[n] for n in REPLICATED], rows_r, F32)

    res_r = _sum_adamw(reps[0], reps[1:], packr(weights), packr(mom1), packr(mom2), "adamw_replicated", tr=rows_r)
    rep_out = _unpack(res_r, rep_shapes)

    out.update({n: a for n, a in zip(REPLICATED, rep_out)})
    result = [loss, grad_x]
    for k in range(4):
        result += [out[n][k] for n in WEIGHT_ORDER]
    return tuple(result)
```

```python
import functools

import jax
import jax.numpy as jnp
from jax import lax
from jax.experimental import pallas as pl
from jax.experimental.pallas import tpu as pltpu

N_META = 16
POOL_WINDOWS = (2, 4, 8, 16)
MAX_WINDOW = max(POOL_WINDOWS)
N_GROUPS = len(POOL_WINDOWS)
HEAD_DIM = 64
DEPTH = 2
CONV_WIDTH = 3
ALPHA = (2.0 * DEPTH) ** 0.25
LN_EPS = 1e-5
NEG_INF = -1e30
ADAM_LR = 0.001
ADAM_B1 = 0.9
ADAM_B2 = 0.999
ADAM_EPS = 1e-08
ADAM_WD = 0.01
ADAM_STEP = 10

F32 = jnp.float32
BF16 = jnp.bfloat16
ATTN_STRIP = 32
GLU_STRIP = 16
LANES = 128
SUBLANES = 8
PACK_COLS = 1024
VMEM_LIMIT = 56 * 1024 * 1024
AXES = ("x", "y", "c")
MESH = pl.DeviceIdType.MESH

NN = (((1,), (0,)), ((), ()))
NT = (((1,), (1,)), ((), ()))
TN = (((0,), (0,)), ((), ()))

SHARD_AXIS = {"meta": 1, "pool_w": 2, "pool_scale": 1, "w_kv": 1, "w_f": 0, "w_q": 1, "w_o": 1,
              "ffn_w_in": 2, "ffn_conv_w": 2, "ffn_w_out": 1, "ln_g": 2, "ln_b": 2}
SHARDED = ("meta", "pool_w", "pool_scale", "w_kv", "w_f", "w_q", "w_o", "ffn_w_in", "ffn_conv_w",
           "ffn_w_out", "ln_g", "ln_b")
REPLICATED = ("b_f", "ffn_conv_b")
MATMUL_WEIGHTS = ("pool_w", "w_kv", "w_f", "w_q", "w_o", "ffn_w_in", "ffn_w_out")
VECTOR_WEIGHTS = ("meta", "pool_scale", "ffn_conv_w", "ln_g", "ln_b")
WEIGHT_ORDER = ("meta", "pool_w", "pool_scale", "w_kv", "w_f", "b_f", "w_q", "w_o", "ffn_w_in",
                "ffn_conv_w", "ffn_conv_b", "ffn_w_out", "ln_g", "ln_b")
BIG_SHARDED = (("w_kv", "CH"), ("w_q", "CH"), ("w_o", "CH"), ("ffn_w_in", "HC"), ("ffn_w_out", "HC"))
SMALL_SHARDED = ("meta", "pool_w", "pool_scale", "w_f", "ffn_conv_w", "ln_g", "ln_b")
ELEMENTWISE_BLOCK_BYTES = 3 * 512 * 1024
N_CHIPS = 4
N_DEV = 8


def _cparams(*sem):
    return pltpu.CompilerParams(dimension_semantics=sem, vmem_limit_bytes=VMEM_LIMIT)


def _round_up(n, m):
    return (n + m - 1) // m * m


def _pick(n, cap):
    if n <= cap:
        return n
    best = 0
    for t in range(LANES, cap + 1, LANES):
        if n % t == 0:
            best = t
    assert best, (n, cap)
    return best


def _mm(a, b, mode, out_dtype, name, *, tm, tn, tk, scale=None, add=None, chips=False, layer=None, into=None):
    if mode == "nn":
        (M, K), N = a.shape, b.shape[1]
    elif mode == "nt":
        (M, K), N = a.shape, b.shape[0]
    else:
        (K, M), N = a.shape, b.shape[1]
    assert M % tm == 0 and N % tn == 0 and K % tk == 0, (name, M, N, K, tm, tn, tk)
    nk = K // tk
    dn = {"nn": NN, "nt": NT, "tn": TN}[mode]
    has_add = add is not None
    has_into = into is not None
    assert not (has_add and (chips or layer is not None))

    def body(*refs):
        a_ref, b_ref = refs[0], refs[1]
        add_ref = refs[2] if has_add else None
        o_ref = refs[2 + has_add + has_into]
        acc_ref = refs[-1] if nk > 1 else None
        k = pl.program_id(2)
        part = lax.dot_general(a_ref[...], b_ref[...], dn, preferred_element_type=F32)

        def finish(r):
            if scale is not None:
                r = r * scale
            if has_add:
                r = r + add_ref[...]
            o_ref[...] = r.astype(out_dtype).reshape(o_ref.shape)

        if nk == 1:
            finish(part)
        else:
            @pl.when(k == 0)
            def _():
                acc_ref[...] = part

            @pl.when(k > 0)
            def _():
                acc_ref[...] += part

            @pl.when(k == nk - 1)
            def _():
                finish(acc_ref[...])

    if mode == "nn":
        a_spec = pl.BlockSpec((tm, tk), lambda j, i, k: (i, k))
        b_spec = pl.BlockSpec((tk, tn), lambda j, i, k: (k, j))
    elif mode == "nt":
        a_spec = pl.BlockSpec((tm, tk), lambda j, i, k: (i, k))
        b_spec = pl.BlockSpec((tn, tk), lambda j, i, k: (j, k))
    else:
        a_spec = pl.BlockSpec((tk, tm), lambda j, i, k: (k, i))
        b_spec = pl.BlockSpec((tk, tn), lambda j, i, k: (k, j))
    out_dims, blk = (M, N), (tm, tn)
    if chips:
        base, count = (0, N_CHIPS) if chips is True else chips
        per_chip = N // count // tn
        assert per_chip * tn * count == N, (name, N, tn)
        out_dims, blk = (N_CHIPS, M, N // count), (1, tm, tn)
        where = lambda j, i: (base + j // per_chip, i, j % per_chip)
    else:
        where = lambda j, i: (i, j)
    if layer is not None:
        out_dims, blk = (DEPTH,) + out_dims, (1,) + blk
        o_spec = pl.BlockSpec(blk, lambda j, i, k: (layer,) + where(j, i))
    else:
        o_spec = pl.BlockSpec(blk, lambda j, i, k: where(j, i))
    in_specs = [a_spec, b_spec] + ([o_spec] if has_add else []) + ([pl.BlockSpec(memory_space=pl.ANY)] if has_into else [])
    args = (a, b) + ((add,) if has_add else ()) + ((into,) if has_into else ())
    return pl.pallas_call(
        body, name=name, grid=(N // tn, M // tm, nk),
        in_specs=in_specs, out_specs=o_spec,
        out_shape=jax.ShapeDtypeStruct(out_dims, out_dtype),
        input_output_aliases={len(args) - 1: 0} if has_into else {},
        scratch_shapes=[pltpu.VMEM((tm, tn), F32)] if nk > 1 else [],
        compiler_params=_cparams("parallel", "parallel", "arbitrary"),
    )(*args)


def _ln_math(z, g, b):
    mu = jnp.mean(z, axis=-1, keepdims=True)
    zc = z - mu
    var = jnp.mean(zc * zc, axis=-1, keepdims=True)
    rstd = lax.rsqrt(var + LN_EPS)
    xh = zc * rstd
    return xh * g + b, xh, rstd


def _pool_ln_fwd(h0, pw, ps, g, b, name, *, tm):
    Lp, D = h0.shape
    G = D // N_GROUPS
    halo_blocks = tm // MAX_WINDOW

    def body(x_ref, halo_ref, pw_ref, ps_ref, g_ref, b_ref,
             diff_ref, mix_ref, h_ref, hb_ref, xh_ref, rs_ref, ext_ref):
        i = pl.program_id(0)
        ext_ref[0:MAX_WINDOW, :] = jnp.where(i == 0, 0.0, halo_ref[...])
        ext_ref[MAX_WINDOW:MAX_WINDOW + tm, :] = x_ref[...]
        t1 = (i * tm + 1 + lax.broadcasted_iota(jnp.int32, (tm, 1), 0)).astype(F32)
        for gi, w in enumerate(POOL_WINDOWS):
            lo, hi = gi * G, (gi + 1) * G
            xg = x_ref[:, lo:hi]
            win = xg
            for j in range(1, w):
                win = win + ext_ref[MAX_WINDOW - j:MAX_WINDOW - j + tm, lo:hi]
            d = (win / jnp.minimum(t1, float(w)) - xg).astype(BF16)
            diff_ref[:, lo:hi] = d
            mix_ref[:, lo:hi] = jnp.dot(d, pw_ref[gi], preferred_element_type=F32)
        z = ALPHA * x_ref[...] + mix_ref[...] * ps_ref[...]
        h, xh, rstd = _ln_math(z, g_ref[...], b_ref[...])
        h_ref[...] = h
        hb_ref[...] = h.astype(BF16)
        xh_ref[...] = xh
        rs_ref[...] = rstd

    row = pl.BlockSpec((tm, D), lambda i: (i, 0))
    vec = pl.BlockSpec((1, D), lambda i: (0, 0))
    return pl.pallas_call(
        body, name=name, grid=(Lp // tm,),
        in_specs=[row,
                  pl.BlockSpec((MAX_WINDOW, D), lambda i: (jnp.maximum(i * halo_blocks - 1, 0), 0)),
                  pl.BlockSpec((N_GROUPS, G, G), lambda i: (0, 0, 0)), vec, vec, vec],
        out_specs=[row, row, row, row, row, pl.BlockSpec((tm, 1), lambda i: (i, 0))],
        out_shape=[jax.ShapeDtypeStruct((Lp, D), BF16), jax.ShapeDtypeStruct((Lp, D), F32),
                   jax.ShapeDtypeStruct((Lp, D), F32), jax.ShapeDtypeStruct((Lp, D), BF16),
                   jax.ShapeDtypeStruct((Lp, D), F32), jax.ShapeDtypeStruct((Lp, 1), F32)],
        scratch_shapes=[pltpu.VMEM((tm + MAX_WINDOW, D), F32)],
        compiler_params=_cparams("parallel"),
    )(h0, h0, pw, ps, g, b)


def _pool_bwd(dz, mixpre, pw, ps, name, *, tm):
    Lp, D = dz.shape
    G = D // N_GROUPS
    halo_blocks = tm // MAX_WINDOW
    n_halo = Lp // MAX_WINDOW
    ni = Lp // tm
    R = tm + MAX_WINDOW

    def body(dz_ref, halo_ref, mix_ref, pw_ref, ps_ref, dh_ref, dmb_ref, dsc_ref, ext_ref, dp_ref):
        i = pl.program_id(0)
        ext_ref[0:tm, :] = dz_ref[...]
        ext_ref[tm:R, :] = jnp.where(i == ni - 1, 0.0, halo_ref[...])
        dmix = (ext_ref[...] * ps_ref[...]).astype(BF16)
        dmb_ref[...] = dmix[0:tm]

        @pl.when(i == 0)
        def _():
            dsc_ref[...] = jnp.zeros_like(dsc_ref)

        dsc_ref[...] += jnp.sum(dz_ref[...] * mix_ref[...], axis=0, keepdims=True)
        t1 = (i * tm + 1 + lax.broadcasted_iota(jnp.int32, (R, 1), 0)).astype(F32)
        for gi, w in enumerate(POOL_WINDOWS):
            lo, hi = gi * G, (gi + 1) * G
            dd = lax.dot_general(dmix[:, lo:hi], pw_ref[gi], NT, preferred_element_type=F32)
            dp_ref[:, lo:hi] = dd / jnp.minimum(t1, float(w))
            back = dp_ref[0:tm, lo:hi]
            for j in range(1, w):
                back = back + dp_ref[j:j + tm, lo:hi]
            dh_ref[:, lo:hi] = ALPHA * dz_ref[:, lo:hi] - dd[0:tm] + back

    row = pl.BlockSpec((tm, D), lambda i: (i, 0))
    vec = pl.BlockSpec((1, D), lambda i: (0, 0))
    return pl.pallas_call(
        body, name=name, grid=(ni,),
        in_specs=[row,
                  pl.BlockSpec((MAX_WINDOW, D), lambda i: (jnp.minimum((i + 1) * halo_blocks, n_halo - 1), 0)),
                  row, pl.BlockSpec((N_GROUPS, G, G), lambda i: (0, 0, 0)), vec],
        out_specs=[row, row, vec],
        out_shape=[jax.ShapeDtypeStruct((Lp, D), F32), jax.ShapeDtypeStruct((Lp, D), BF16),
                   jax.ShapeDtypeStruct((1, D), F32)],
        scratch_shapes=[pltpu.VMEM((R, D), F32), pltpu.VMEM((R, D), F32)],
        compiler_params=_cparams("arbitrary"),
    )(dz, dz, mixpre, pw, ps)


def _pool_dw(diffb, dmb, name, *, tk):
    Lp, D = diffb.shape
    G = D // N_GROUPS

    def body(a_ref, b_ref, o_ref):
        @pl.when(pl.program_id(1) == 0)
        def _():
            o_ref[...] = jnp.zeros_like(o_ref)

        o_ref[0] += lax.dot_general(a_ref[...], b_ref[...], TN, preferred_element_type=F32)

    blk = pl.BlockSpec((tk, G), lambda g, k: (k, g))
    return pl.pallas_call(
        body, name=name, grid=(N_GROUPS, Lp // tk),
        in_specs=[blk, blk], out_specs=pl.BlockSpec((1, G, G), lambda g, k: (g, 0, 0)),
        out_shape=jax.ShapeDtypeStruct((N_GROUPS, G, G), F32),
        compiler_params=_cparams("parallel", "arbitrary"),
    )(diffb, dmb)


def _ln_fwd(resid, y, g, b, name, *, tm):
    Lp, D = resid.shape

    def body(r_ref, y_ref, g_ref, b_ref, h_ref, hb_ref, xh_ref, rs_ref):
        h, xh, rstd = _ln_math(ALPHA * r_ref[...] + y_ref[...], g_ref[...], b_ref[...])
        h_ref[...] = h
        hb_ref[...] = h.astype(BF16)
        xh_ref[...] = xh
        rs_ref[...] = rstd

    row = pl.BlockSpec((tm, D), lambda i: (i, 0))
    vec = pl.BlockSpec((1, D), lambda i: (0, 0))
    return pl.pallas_call(
        body, name=name, grid=(Lp // tm,),
        in_specs=[row, row, vec, vec],
        out_specs=[row, row, row, pl.BlockSpec((tm, 1), lambda i: (i, 0))],
        out_shape=[jax.ShapeDtypeStruct((Lp, D), F32), jax.ShapeDtypeStruct((Lp, D), BF16),
                   jax.ShapeDtypeStruct((Lp, D), F32), jax.ShapeDtypeStruct((Lp, 1), F32)],
        compiler_params=_cparams("parallel"),
    )(resid, y, g, b)


def _ln_bwd(parts, coefs, xh, rs, g, name, *, tm):
    Lp, D = xh.shape
    n = len(parts)

    def body(*refs):
        part_refs = refs[:n]
        xh_ref, rs_ref, g_ref = refs[n:n + 3]
        dz_ref, dzb_ref, dg_ref, db_ref = refs[n + 3:]
        dy = part_refs[0][...] if coefs[0] == 1.0 else coefs[0] * part_refs[0][...]
        for c, r in zip(coefs[1:], part_refs[1:]):
            dy = dy + (r[...] if c == 1.0 else c * r[...])
        x = xh_ref[...]
        dxh = dy * g_ref[...]
        m1 = jnp.mean(dxh, axis=-1, keepdims=True)
        m2 = jnp.mean(dxh * x, axis=-1, keepdims=True)
        dz = rs_ref[...] * (dxh - m1 - x * m2)
        dz_ref[...] = dz
        dzb_ref[...] = dz.astype(BF16)

        @pl.when(pl.program_id(0) == 0)
        def _():
            dg_ref[...] = jnp.zeros_like(dg_ref)
            db_ref[...] = jnp.zeros_like(db_ref)

        dg_ref[...] += jnp.sum(dy * x, axis=0, keepdims=True)
        db_ref[...] += jnp.sum(dy, axis=0, keepdims=True)

    row = pl.BlockSpec((tm, D), lambda i: (i, 0))
    vec = pl.BlockSpec((1, D), lambda i: (0, 0))
    return pl.pallas_call(
        body, name=name, grid=(Lp // tm,),
        in_specs=[row] * n + [row, pl.BlockSpec((tm, 1), lambda i: (i, 0)), vec],
        out_specs=[row, row, vec, vec],
        out_shape=[jax.ShapeDtypeStruct((Lp, D), F32), jax.ShapeDtypeStruct((Lp, D), BF16),
                   jax.ShapeDtypeStruct((1, D), F32), jax.ShapeDtypeStruct((1, D), F32)],
        compiler_params=_cparams("arbitrary"),
    )(*parts, xh, rs, g)


def _loss_head(h, tgt, name, *, tm, row_lo, row_hi):
    Lp, D = h.shape

    def body(h_ref, t_ref, dy_ref, loss_ref):
        i = pl.program_id(0)
        r = i * tm + lax.broadcasted_iota(jnp.int32, (tm, 1), 0)
        valid = (r >= row_lo) & (r < row_hi)
        e = jnp.where(valid, h_ref[...] - t_ref[...], 0.0)
        dy_ref[...] = e * (1.0 / D)

        @pl.when(i == 0)
        def _():
            loss_ref[...] = jnp.zeros_like(loss_ref)

        loss_ref[...] += 0.5 * jnp.sum(jnp.mean(e * e, axis=-1, keepdims=True), axis=0, keepdims=True)

    row = pl.BlockSpec((tm, D), lambda i: (i, 0))
    return pl.pallas_call(
        body, name=name, grid=(Lp // tm,),
        in_specs=[row, row], out_specs=[row, pl.BlockSpec((1, 1), lambda i: (0, 0))],
        out_shape=[jax.ShapeDtypeStruct((Lp, D), F32), jax.ShapeDtypeStruct((1, 1), F32)],
        compiler_params=_cparams("arbitrary"),
    )(h, tgt)


def _shift_rows_down(cur, prev, s, sub):
    return jnp.where(sub >= s, pltpu.roll(cur, s, 0), pltpu.roll(prev, s, 0))


def _shift_rows_up(cur, nxt, s, sub):
    return jnp.where(sub < SUBLANES - s, pltpu.roll(cur, SUBLANES - s, 0), pltpu.roll(nxt, SUBLANES - s, 0))


def _conv_group(cur, prev, cw_ref, cb_ref, sub):
    taps = [_shift_rows_down(cur, prev, 2, sub), _shift_rows_down(cur, prev, 1, sub), cur]
    c = cb_ref[...] + cw_ref[0:1, :] * taps[0] + cw_ref[1:2, :] * taps[1] + cw_ref[2:3, :] * taps[2]
    return c, taps


def _conv_glu_fwd(u, cw, cb, name, *, tm, tn):
    Lp, F2 = u.shape
    F = F2 // 2
    nj = F // tn
    halo_blocks = tm // SUBLANES
    S8 = SUBLANES
    assert GLU_STRIP == 2 * S8 and tm % GLU_STRIP == 0

    def body(ua_ref, ug_ref, pa_ref, pg_ref, cwa_ref, cwg_ref, cba_ref, cbg_ref, o_ref):
        first = pl.program_id(1) == 0
        sub = lax.broadcasted_iota(jnp.int32, (S8, tn), 0)

        def strip(r, prev_a, prev_g):
            out = []
            for g0 in (0, S8):
                a_cur = ua_ref[pl.ds(r + g0, S8), :]
                g_cur = ug_ref[pl.ds(r + g0, S8), :]
                a, _ = _conv_group(a_cur, prev_a, cwa_ref, cba_ref, sub)
                gate, _ = _conv_group(g_cur, prev_g, cwg_ref, cbg_ref, sub)
                out.append(a * jax.nn.sigmoid(a) * gate)
                prev_a, prev_g = a_cur, g_cur
            o_ref[pl.ds(r, GLU_STRIP), :] = jnp.concatenate(out, axis=0).astype(BF16)

        strip(0, jnp.where(first, 0.0, pa_ref[...]), jnp.where(first, 0.0, pg_ref[...]))

        def step(k, carry):
            r = pl.multiple_of(k * GLU_STRIP, GLU_STRIP)
            before = pl.ds(pl.multiple_of(r - S8, S8), S8)
            strip(r, ua_ref[before, :], ug_ref[before, :])
            return carry

        lax.fori_loop(1, tm // GLU_STRIP, step, 0)

    def prev(off):
        return pl.BlockSpec((SUBLANES, tn), lambda j, i: (jnp.maximum(i * halo_blocks - 1, 0), j + off))

    def cols(rows, off):
        return pl.BlockSpec((rows, tn), lambda j, i: (0, j + off))

    return pl.pallas_call(
        body, name=name, grid=(nj, Lp // tm),
        in_specs=[pl.BlockSpec((tm, tn), lambda j, i: (i, j)), pl.BlockSpec((tm, tn), lambda j, i: (i, j + nj)),
                  prev(0), prev(nj), cols(CONV_WIDTH, 0), cols(CONV_WIDTH, nj), cols(1, 0), cols(1, nj)],
        out_specs=pl.BlockSpec((tm, tn), lambda j, i: (i, j)),
        out_shape=jax.ShapeDtypeStruct((Lp, F), BF16),
        compiler_params=_cparams("parallel", "parallel"),
    )(u, u, u, u, cw, cw, cb, cb)


def _conv_glu_bwd(u, dact, cw, cb, name, *, tm, tn):
    Lp, F2 = u.shape
    F = F2 // 2
    nj = F // tn
    ni = Lp // tm
    halo_blocks = tm // SUBLANES
    n_halo = Lp // SUBLANES
    S8 = SUBLANES
    n_strips = tm // GLU_STRIP
    assert GLU_STRIP == 2 * S8 and tm % GLU_STRIP == 0

    def body(ua_ref, ug_ref, pa_ref, pg_ref, na_ref, ng_ref, da_ref, dn_ref,
             cwa_ref, cwg_ref, cba_ref, cbg_ref,
             dua_ref, dug_ref, dwa_ref, dwg_ref, dba_ref, dbg_ref,
             wacc_a, wacc_g, bacc_a, bacc_g):
        i = pl.program_id(1)
        first, last = i == 0, i == ni - 1
        sub = lax.broadcasted_iota(jnp.int32, (S8, tn), 0)
        for acc in (wacc_a, wacc_g, bacc_a, bacc_g):
            acc[...] = jnp.zeros_like(acc)

        def dconv(a_cur, a_prev, g_cur, g_prev, dact_rows):
            a, taps_a = _conv_group(a_cur, a_prev, cwa_ref, cba_ref, sub)
            gate, taps_g = _conv_group(g_cur, g_prev, cwg_ref, cbg_ref, sub)
            sg = jax.nn.sigmoid(a)
            dca = dact_rows * gate * (sg * (1.0 + a * (1.0 - sg)))
            dcg = dact_rows * (a * sg)
            return dca, dcg, taps_a, taps_g

        def du_group(dc, dc_after, cw_ref):
            return (cw_ref[2:3, :] * dc + cw_ref[1:2, :] * _shift_rows_up(dc, dc_after, 1, sub)
                    + cw_ref[0:1, :] * _shift_rows_up(dc, dc_after, 2, sub))

        def strip(r, a_prev, g_prev, dca_after, dcg_after):
            a0, a1 = ua_ref[pl.ds(r, S8), :], ua_ref[pl.ds(r + S8, S8), :]
            g0, g1 = ug_ref[pl.ds(r, S8), :], ug_ref[pl.ds(r + S8, S8), :]
            dca1, dcg1, ta1, tg1 = dconv(a1, a0, g1, g0, da_ref[pl.ds(r + S8, S8), :])
            dca0, dcg0, ta0, tg0 = dconv(a0, a_prev, g0, g_prev, da_ref[pl.ds(r, S8), :])
            dua_ref[pl.ds(r, GLU_STRIP), :] = jnp.concatenate(
                [du_group(dca0, dca1, cwa_ref), du_group(dca1, dca_after, cwa_ref)], axis=0).astype(BF16)
            dug_ref[pl.ds(r, GLU_STRIP), :] = jnp.concatenate(
                [du_group(dcg0, dcg1, cwg_ref), du_group(dcg1, dcg_after, cwg_ref)], axis=0).astype(BF16)
            for k in range(CONV_WIDTH):
                wacc_a[k] += dca0 * ta0[k] + dca1 * ta1[k]
                wacc_g[k] += dcg0 * tg0[k] + dcg1 * tg1[k]
            bacc_a[...] += dca0 + dca1
            bacc_g[...] += dcg0 + dcg1
            return dca0, dcg0

        tail = pl.ds(tm - S8, S8)
        dca_after, dcg_after, _, _ = dconv(na_ref[...], ua_ref[tail, :], ng_ref[...], ug_ref[tail, :],
                                           jnp.where(last, 0.0, dn_ref[...]))

        def step(t, carry):
            r = pl.multiple_of((n_strips - 1 - t) * GLU_STRIP, GLU_STRIP)
            before = pl.ds(pl.multiple_of(r - S8, S8), S8)
            return strip(r, ua_ref[before, :], ug_ref[before, :], *carry)

        dca_after, dcg_after = lax.fori_loop(0, n_strips - 1, step, (dca_after, dcg_after))
        strip(0, jnp.where(first, 0.0, pa_ref[...]), jnp.where(first, 0.0, pg_ref[...]), dca_after, dcg_after)

        @pl.when(first)
        def _():
            for r in (dwa_ref, dwg_ref, dba_ref, dbg_ref):
                r[...] = jnp.zeros_like(r)

        for wacc, bacc, dw_ref, db_ref in ((wacc_a, bacc_a, dwa_ref, dba_ref), (wacc_g, bacc_g, dwg_ref, dbg_ref)):
            db_ref[...] += jnp.sum(bacc[...], axis=0, keepdims=True)
            for k in range(CONV_WIDTH):
                dw_ref[k:k + 1, :] += jnp.sum(wacc[k], axis=0, keepdims=True)

    def tile(off):
        return pl.BlockSpec((tm, tn), lambda j, i: (i, j + off))

    def prev(off):
        return pl.BlockSpec((S8, tn), lambda j, i: (jnp.maximum(i * halo_blocks - 1, 0), j + off))

    def nxt(off):
        return pl.BlockSpec((S8, tn), lambda j, i: (jnp.minimum((i + 1) * halo_blocks, n_halo - 1), j + off))

    def cols(rows, off):
        return pl.BlockSpec((rows, tn), lambda j, i: (0, j + off))

    return pl.pallas_call(
        body, name=name, grid=(nj, ni),
        in_specs=[tile(0), tile(nj), prev(0), prev(nj), nxt(0), nxt(nj), tile(0), nxt(0),
                  cols(CONV_WIDTH, 0), cols(CONV_WIDTH, nj), cols(1, 0), cols(1, nj)],
        out_specs=[tile(0), tile(0), cols(CONV_WIDTH, 0), cols(CONV_WIDTH, 0), cols(1, 0), cols(1, 0)],
        out_shape=[jax.ShapeDtypeStruct((Lp, F), BF16), jax.ShapeDtypeStruct((Lp, F), BF16),
                   jax.ShapeDtypeStruct((CONV_WIDTH, F), F32), jax.ShapeDtypeStruct((CONV_WIDTH, F), F32),
                   jax.ShapeDtypeStruct((1, F), F32), jax.ShapeDtypeStruct((1, F), F32)],
        scratch_shapes=[pltpu.VMEM((CONV_WIDTH, S8, tn), F32), pltpu.VMEM((CONV_WIDTH, S8, tn), F32),
                        pltpu.VMEM((S8, tn), F32), pltpu.VMEM((S8, tn), F32)],
        compiler_params=_cparams("parallel", "arbitrary"),
    )(u, u, u, u, u, u, dact, dact, cw, cw, cb, cb)


def _logf_cumsum(pre, bf, name, *, tm):
    Lp, W = pre.shape

    def body(p_ref, b_ref, c_ref, carry_ref):
        i = pl.program_id(0)

        @pl.when(i == 0)
        def _():
            carry_ref[...] = jnp.zeros_like(carry_ref)

        x = p_ref[...] + b_ref[...]
        lf = jnp.minimum(x, 0.0) - jnp.log(1.0 + jnp.exp(-jnp.abs(x)))
        tri = (lax.broadcasted_iota(jnp.int32, (tm, tm), 0) >= lax.broadcasted_iota(jnp.int32, (tm, tm), 1)).astype(F32)
        c = jnp.dot(tri, lf, precision=lax.Precision.HIGHEST, preferred_element_type=F32) + carry_ref[...]
        c_ref[...] = c
        carry_ref[...] = c[tm - 1:tm, :]

    row = pl.BlockSpec((tm, W), lambda i: (i, 0))
    return pl.pallas_call(
        body, name=name, grid=(Lp // tm,),
        in_specs=[row, pl.BlockSpec((1, W), lambda i: (0, 0))], out_specs=row,
        out_shape=jax.ShapeDtypeStruct((Lp, W), F32),
        scratch_shapes=[pltpu.VMEM((1, W), F32)],
        compiler_params=_cparams("arbitrary"),
    )(pre, bf)


def _logf_bwd(dc_a, dc_b, pre, bf, name, *, tm):
    Lp, W = pre.shape
    ni = Lp // tm

    def body(dca_ref, dcb_ref, p_ref, b_ref, dpb_ref, db_ref, carry_ref):
        i = pl.program_id(0)

        @pl.when(i == 0)
        def _():
            carry_ref[...] = jnp.zeros_like(carry_ref)
            db_ref[...] = jnp.zeros_like(db_ref)

        triu = (lax.broadcasted_iota(jnp.int32, (tm, tm), 0) <= lax.broadcasted_iota(jnp.int32, (tm, tm), 1)).astype(F32)
        dl = jnp.dot(triu, dca_ref[...] + dcb_ref[...], precision=lax.Precision.HIGHEST,
                     preferred_element_type=F32) + carry_ref[...]
        carry_ref[...] = dl[0:1, :]
        dp = dl * jax.nn.sigmoid(-(p_ref[...] + b_ref[...]))
        dpb_ref[...] = dp.astype(BF16)
        db_ref[...] += jnp.sum(dp, axis=0, keepdims=True)

    rev = pl.BlockSpec((tm, W), lambda i: (ni - 1 - i, 0))
    vec = pl.BlockSpec((1, W), lambda i: (0, 0))
    return pl.pallas_call(
        body, name=name, grid=(ni,),
        in_specs=[rev, rev, rev, vec], out_specs=[rev, vec],
        out_shape=[jax.ShapeDtypeStruct((Lp, W), BF16), jax.ShapeDtypeStruct((1, W), F32)],
        scratch_shapes=[pltpu.VMEM((1, W), F32)],
        compiler_params=_cparams("arbitrary"),
    )(dc_a, dc_b, pre, bf)


def _attn_fwd(qb, kvb, crow4, name, *, tq):
    Lp, D = qb.shape
    H = D // HEAD_DIM
    nq = Lp // tq
    S8 = SUBLANES
    HB = LANES // HEAD_DIM
    n_scratch = 6
    lane_tiles = tq // LANES
    assert HB == 2

    def to_column(row8):
        return jnp.transpose(jnp.concatenate([row8] * (LANES // S8), axis=0))

    def body(q_ref, k_ref, v_ref, c_ref, o_ref, ob_ref, lse_ref, vt_ref, *scratch):
        i = pl.program_id(1)
        heads = [scratch[n_scratch * hb:n_scratch * (hb + 1)] for hb in range(HB)]
        first_head = lax.broadcasted_iota(jnp.int32, (tq, LANES), 1) < HEAD_DIM
        q2 = q_ref[...]
        q_of = [jnp.where(first_head, q2, jnp.zeros_like(q2)), jnp.where(first_head, jnp.zeros_like(q2), q2)]

        @pl.when(i == 0)
        def _():
            for j in range(nq):
                for hb, refs in enumerate(heads):
                    refs[5][j] = to_column(jnp.concatenate([c_ref[hb, j]] * S8, axis=0))
                vt_ref[j] = jnp.transpose(v_ref[pl.ds(j * tq, tq), :].astype(F32)).astype(BF16)

        for m_ref, l_ref, acc_ref, _, _, _ in heads:
            m_ref[...] = jnp.full_like(m_ref, NEG_INF)
            l_ref[...] = jnp.zeros_like(l_ref)
            acc_ref[...] = jnp.zeros_like(acc_ref)

        def chunk(j, masked):
            k2 = k_ref[pl.ds(pl.multiple_of(j * tq, tq), tq), :]
            for hb, (_, _, _, st_ref, _, _) in enumerate(heads):
                st_ref[...] = lax.dot_general(k2, q_of[hb], NT, preferred_element_type=F32)
            for hb, (m_ref, l_ref, acc_ref, st_ref, pt_ref, cs_ref) in enumerate(heads):
                ct = c_ref[hb, i]
                mx = jnp.full((S8, tq), NEG_INF, F32)
                for r0 in range(0, tq, ATTN_STRIP):
                    rows = pl.ds(r0, ATTN_STRIP)
                    cs = jnp.concatenate([cs_ref[j, rows, :]] * lane_tiles, axis=1)
                    st = st_ref[rows, :] + (ct - cs)
                    if masked:
                        keep = (lax.broadcasted_iota(jnp.int32, (ATTN_STRIP, tq), 1)
                                >= r0 + lax.broadcasted_iota(jnp.int32, (ATTN_STRIP, tq), 0))
                        st = jnp.where(keep, st, NEG_INF)
                    st_ref[rows, :] = st
                    for g0 in range(0, ATTN_STRIP, S8):
                        mx = jnp.maximum(mx, st[g0:g0 + S8])
                m_prev = m_ref[...]
                m_new = jnp.maximum(m_prev, jnp.max(mx, axis=0, keepdims=True))
                alpha = jnp.exp(m_prev - m_new)
                m_ref[...] = m_new
                ls = jnp.zeros((S8, tq), F32)
                for r0 in range(0, tq, ATTN_STRIP):
                    pieces = [jnp.exp(st_ref[pl.ds(r0 + g0, S8), :] - m_new) for g0 in range(0, ATTN_STRIP, S8)]
                    for piece in pieces:
                        ls = ls + piece
                    pt_ref[pl.ds(r0, ATTN_STRIP), :] = jnp.concatenate(pieces, axis=0).astype(BF16)
                l_ref[...] = alpha * l_ref[...] + ls
                pv = jnp.dot(vt_ref[j], pt_ref[...], preferred_element_type=F32)
                acc_ref[...] = jnp.concatenate([alpha] * (LANES // S8), axis=0) * acc_ref[...] + pv

        def step(j, carry):
            chunk(j, False)
            return carry

        lax.fori_loop(0, i, step, 0)
        chunk(i, True)
        outs = []
        for hb, (m_ref, l_ref, acc_ref, _, _, _) in enumerate(heads):
            l_row = jnp.sum(l_ref[...], axis=0, keepdims=True)
            outs.append(acc_ref[...] / l_row)
            lse_ref[hb, 0] = m_ref[0:1, :] + jnp.log(l_row)
        first_rows = lax.broadcasted_iota(jnp.int32, (LANES, tq), 0) < HEAD_DIM
        o2 = jnp.transpose(jnp.where(first_rows, outs[0], outs[1]))
        o_ref[...] = o2
        ob_ref[...] = o2.astype(BF16)

    per_head = [pltpu.VMEM((S8, tq), F32), pltpu.VMEM((S8, tq), F32), pltpu.VMEM((LANES, tq), F32),
                pltpu.VMEM((tq, tq), F32), pltpu.VMEM((tq, tq), BF16), pltpu.VMEM((nq, tq, LANES), F32)]
    assert len(per_head) == n_scratch
    v_blocks = D // LANES
    tile = pl.BlockSpec((tq, LANES), lambda p, i: (i, p))
    return pl.pallas_call(
        body, name=name, grid=(H // HB, nq),
        in_specs=[tile,
                  pl.BlockSpec((Lp, LANES), lambda p, i: (0, p)),
                  pl.BlockSpec((Lp, LANES), lambda p, i: (0, v_blocks + p)),
                  pl.BlockSpec((HB, nq, 1, tq), lambda p, i: (p, 0, 0, 0))],
        out_specs=[tile, tile, pl.BlockSpec((HB, 1, 1, tq), lambda p, i: (p, i, 0, 0))],
        out_shape=[jax.ShapeDtypeStruct((Lp, D), F32), jax.ShapeDtypeStruct((Lp, D), BF16),
                   jax.ShapeDtypeStruct((H, nq, 1, tq), F32)],
        scratch_shapes=[pltpu.VMEM((nq, LANES, tq), BF16)] + per_head * HB,
        compiler_params=_cparams("parallel", "arbitrary"),
    )(qb, kvb, kvb, crow4)


def _attn_delta(do, o, name, *, tm, n_heads):
    Lp, D = do.shape

    def body(do_ref, o_ref, d_ref):
        sel = (lax.broadcasted_iota(jnp.int32, (D, LANES), 0) // HEAD_DIM
               == lax.broadcasted_iota(jnp.int32, (D, LANES), 1)).astype(F32)
        d_ref[...] = jnp.dot(do_ref[...].astype(F32) * o_ref[...], sel, precision=lax.Precision.HIGHEST,
                             preferred_element_type=F32)

    row = pl.BlockSpec((tm, D), lambda i: (i, 0))
    return pl.pallas_call(
        body, name=name, grid=(Lp // tm,),
        in_specs=[row, row], out_specs=pl.BlockSpec((tm, LANES), lambda i: (i, 0)),
        out_shape=jax.ShapeDtypeStruct((Lp, LANES), F32),
        compiler_params=_cparams("parallel"),
    )(do, o)


def _attn_bwd(qb, dob, kvb, lse4, delta4, crow4, name, *, tq):
    Lp, D = qb.shape
    H = D // HEAD_DIM
    nq = Lp // tq
    HB = LANES // HEAD_DIM
    lane_tiles = tq // LANES
    n_scratch = 8
    assert HB == 2

    def body(q_ref, do_ref, k_ref, v_ref, lse_ref, dl_ref, c_ref,
             dqb_ref, dk_ref, dv_ref, dcs_ref, dcq_ref, dqt_ref, kt_ref, *scratch):
        j = pl.program_id(1)
        heads = [scratch[n_scratch * hb:n_scratch * (hb + 1)] for hb in range(HB)]
        first_head = lax.broadcasted_iota(jnp.int32, (tq, LANES), 1) < HEAD_DIM

        def split(x2):
            zero = jnp.zeros_like(x2)
            return [jnp.where(first_head, x2, zero), jnp.where(first_head, zero, x2)]

        @pl.when(j == 0)
        def _():
            dqt_ref[...] = jnp.zeros_like(dqt_ref)
            dcq_ref[...] = jnp.zeros_like(dcq_ref)

        k2 = k_ref[...]
        v2 = v_ref[...]
        kt_ref[...] = jnp.transpose(k2.astype(F32)).astype(BF16)
        first_rows = lax.broadcasted_iota(jnp.int32, (LANES, tq), 0) < HEAD_DIM
        for hb, (dk_acc, dv_acc, dc_acc, _, _, _, _, cs_ref) in enumerate(heads):
            dk_acc[...] = jnp.zeros_like(dk_acc)
            dv_acc[...] = jnp.zeros_like(dv_acc)
            dc_acc[...] = jnp.zeros_like(dc_acc)
            cs_ref[...] = jnp.transpose(jnp.broadcast_to(c_ref[hb, j], (LANES, tq)))

        def pair(i, masked):
            queries = pl.ds(pl.multiple_of(i * tq, tq), tq)
            q2 = q_ref[queries, :]
            do2 = do_ref[queries, :]
            q_of, do_of = split(q2), split(do2)
            for hb, (_, _, _, st_ref, dp_ref, _, _, _) in enumerate(heads):
                st_ref[...] = lax.dot_general(k2, q_of[hb], NT, preferred_element_type=F32)
                dp_ref[...] = lax.dot_general(v2, do_of[hb], NT, preferred_element_type=F32)
            dq_parts = []
            for hb, (dk_acc, dv_acc, dc_acc, st_ref, dp_ref, pt_ref, ds_ref, cs_ref) in enumerate(heads):
                bias_q = c_ref[hb, i] - lse_ref[hb, i]
                delta = dl_ref[hb, i]
                col_sum = jnp.zeros((SUBLANES, tq), F32)
                for r0 in range(0, tq, ATTN_STRIP):
                    rows = pl.ds(r0, ATTN_STRIP)
                    st = st_ref[rows, :] + (bias_q - jnp.concatenate([cs_ref[rows, :]] * lane_tiles, axis=1))
                    if masked:
                        keep = (lax.broadcasted_iota(jnp.int32, (ATTN_STRIP, tq), 1)
                                >= r0 + lax.broadcasted_iota(jnp.int32, (ATTN_STRIP, tq), 0))
                        st = jnp.where(keep, st, NEG_INF)
                    pt = jnp.exp(st)
                    dst = pt * (dp_ref[rows, :] - delta)
                    pt_ref[rows, :] = pt.astype(BF16)
                    ds_ref[rows, :] = dst.astype(BF16)
                    dc_acc[rows, :] += jnp.sum(dst, axis=1, keepdims=True)
                    for g0 in range(0, ATTN_STRIP, SUBLANES):
                        col_sum = col_sum + dst[g0:g0 + SUBLANES]
                dcq_ref[hb, i] += jnp.sum(col_sum, axis=0, keepdims=True)
                dv_acc[...] += jnp.dot(pt_ref[...], do2, preferred_element_type=F32)
                dk_acc[...] += jnp.dot(ds_ref[...], q2, preferred_element_type=F32)
                dq_parts.append(jnp.dot(kt_ref[...], ds_ref[...], preferred_element_type=F32))
            dqt_ref[i] += jnp.where(first_rows, dq_parts[0], dq_parts[1])

        def step(i, carry):
            pair(i, False)
            return carry

        pair(j, True)
        lax.fori_loop(j + 1, nq, step, 0)
        dk_ref[...] = jnp.where(first_head, heads[0][0][...], heads[1][0][...]).astype(BF16)
        dv_ref[...] = jnp.where(first_head, heads[0][1][...], heads[1][1][...]).astype(BF16)
        for hb in range(HB):
            dcs_ref[hb] = -heads[hb][2][...]

        @pl.when(j == nq - 1)
        def _():
            for i in range(nq):
                dqb_ref[pl.ds(i * tq, tq), :] = jnp.transpose(dqt_ref[i]).astype(BF16)

    per_head = [pltpu.VMEM((tq, LANES), F32), pltpu.VMEM((tq, LANES), F32), pltpu.VMEM((tq, 1), F32),
                pltpu.VMEM((tq, tq), F32), pltpu.VMEM((tq, tq), F32),
                pltpu.VMEM((tq, tq), BF16), pltpu.VMEM((tq, tq), BF16), pltpu.VMEM((tq, LANES), F32)]
    assert len(per_head) == n_scratch
    v_blocks = D // LANES
    whole = pl.BlockSpec((Lp, LANES), lambda p, j: (0, p))
    tile = pl.BlockSpec((tq, LANES), lambda p, j: (j, p))
    rows = pl.BlockSpec((HB, nq, 1, tq), lambda p, j: (p, 0, 0, 0))
    return pl.pallas_call(
        body, name=name, grid=(H // HB, nq),
        in_specs=[whole, whole, tile, pl.BlockSpec((tq, LANES), lambda p, j: (j, v_blocks + p)), rows, rows, rows],
        out_specs=[whole, tile, tile, pl.BlockSpec((HB, tq, 1), lambda p, j: (p, j, 0)), rows],
        out_shape=[jax.ShapeDtypeStruct((Lp, D), BF16), jax.ShapeDtypeStruct((Lp, D), BF16),
                   jax.ShapeDtypeStruct((Lp, D), BF16), jax.ShapeDtypeStruct((H, Lp, 1), F32),
                   jax.ShapeDtypeStruct((H, nq, 1, tq), F32)],
        scratch_shapes=[pltpu.VMEM((nq, LANES, tq), F32), pltpu.VMEM((LANES, tq), BF16)] + per_head * HB,
        compiler_params=_cparams("parallel", "arbitrary"),
    )(qb, dob, kvb, kvb, lse4, delta4, crow4)


def _remote(src, dst, send_sems, recv_sems, k, to):
    return pltpu.make_async_remote_copy(src_ref=src, dst_ref=dst, send_sem=send_sems.at[k], recv_sem=recv_sems.at[k],
                                        device_id=to, device_id_type=MESH)


def _place():
    x, y, c = lax.axis_index("x"), lax.axis_index("y"), lax.axis_index("c")
    other_chips = [(1 - x, y), (x, 1 - y), (1 - x, 1 - y)]
    return x, y, c, other_chips


def _all_gather_weights(wb, wf, name):
    Rb, C = wb.shape
    Rf = wf.shape[0]
    hb = Rb // 2

    def body(wb_ref, wf_ref, ob_ref, of_ref, send_sems, recv_sems):
        x, y, c, chips = _place()
        me = 2 * x + y
        sibling = (x, y, 1 - c)

        def half(chip, core):
            return ob_ref.at[chip, pl.ds(core * hb, hb), :]

        sent = []
        for j, (cx, cy) in enumerate(chips):
            sent.append(_remote(wb_ref.at[pl.ds(c * hb, hb), :], half(me, c), send_sems, recv_sems, j, (cx, cy, c)))
            sent.append(_remote(wf_ref, of_ref.at[me], send_sems, recv_sems, 3 + j, (cx, cy, c)))
        for cp in sent:
            cp.start()
        for j, (cx, cy) in enumerate(chips):
            chip = 2 * cx + cy
            _remote(half(chip, c), half(chip, c), send_sems, recv_sems, j, sibling).wait_recv()
            fwd = _remote(half(chip, c), half(chip, c), send_sems, recv_sems, 6 + j, sibling)
            fwd.start()
            sent.append(fwd)
        for j, (cx, cy) in enumerate(chips):
            chip = 2 * cx + cy
            _remote(wf_ref, of_ref.at[chip], send_sems, recv_sems, 3 + j, sibling).wait_recv()
            _remote(half(chip, 1 - c), half(chip, 1 - c), send_sems, recv_sems, 6 + j, sibling).wait_recv()
        for cp in sent:
            cp.wait_send()

    any_spec = pl.BlockSpec(memory_space=pl.ANY)
    return pl.pallas_call(
        body, name=name,
        in_specs=[any_spec, any_spec], out_specs=[any_spec, any_spec],
        out_shape=[jax.ShapeDtypeStruct((N_CHIPS, Rb, C), BF16), jax.ShapeDtypeStruct((N_CHIPS, Rf, C), F32)],
        scratch_shapes=[pltpu.SemaphoreType.DMA((9,)), pltpu.SemaphoreType.DMA((9,))],
    )(wb, wf)


def _half_of(ref, order, half):
    return ref.at[pl.ds(0, N_CHIPS), half] if order == "CH" else ref.at[half]


def _halves_to_sibling(grads, orders, name):
    n = len(grads)

    def body(*refs):
        g_refs, a_refs, (send_sems, recv_sems) = refs[:n], refs[n:2 * n], refs[2 * n:]
        x, y, c, _ = _place()
        copies = [_remote(_half_of(g, o, 1 - c), a, send_sems, recv_sems, k, (x, y, 1 - c))
                  for k, (g, a, o) in enumerate(zip(g_refs, a_refs, orders))]
        for cp in copies:
            cp.start()
        for cp in copies:
            cp.wait()

    any_spec = pl.BlockSpec(memory_space=pl.ANY)
    shapes = [g.shape[2:] for g in grads]
    return pl.pallas_call(
        body, name=name, in_specs=[any_spec] * n, out_specs=[any_spec] * n,
        out_shape=[jax.ShapeDtypeStruct((N_CHIPS,) + s, F32) for s in shapes],
        scratch_shapes=[pltpu.SemaphoreType.DMA((n,)), pltpu.SemaphoreType.DMA((n,))],
    )(*grads)


def _chip_partial(g, a, core, order, wire, name, *, tr):
    _, R, C = a.shape
    narrow = wire != F32

    def body(core_ref, g_ref, a_ref, *outs):
        p = g_ref[0, 0] + a_ref[0]
        outs[0][0] = p
        if narrow:
            outs[1][0] = p.astype(wire)

    if order == "CH":
        g_spec = pl.BlockSpec((1, 1, tr, C), lambda s, i, core_ref: (s, core_ref[0], i, 0))
    else:
        g_spec = pl.BlockSpec((1, 1, tr, C), lambda s, i, core_ref: (core_ref[0], s, i, 0))
    blk = pl.BlockSpec((1, tr, C), lambda s, i, core_ref: (s, i, 0))
    grid_spec = pltpu.PrefetchScalarGridSpec(
        num_scalar_prefetch=1, grid=(N_CHIPS, R // tr), in_specs=[g_spec, blk],
        out_specs=[blk, blk] if narrow else [blk])
    out_shape = [jax.ShapeDtypeStruct((N_CHIPS, R, C), F32)] + ([jax.ShapeDtypeStruct((N_CHIPS, R, C), wire)] if narrow else [])
    outs = pl.pallas_call(body, name=name, grid_spec=grid_spec, out_shape=out_shape,
                          compiler_params=_cparams("parallel", "parallel"))(core, g, a)
    return outs[0], outs[-1]


def _chip_exchange(parts, rep, name):
    n = len(parts)
    rr, C = rep.shape

    def body(*refs):
        p_refs, rep_ref = refs[:n], refs[n]
        land_refs, reps_ref = refs[n + 1:2 * n + 1], refs[2 * n + 1]
        send_sems, recv_sems, local_sem = refs[2 * n + 2:]
        x, y, c, chips = _place()
        me = 4 * x + 2 * y + c
        own = pltpu.make_async_copy(rep_ref, reps_ref.at[me], local_sem.at[0])
        own.start()
        sent = []
        for k, (p, land) in enumerate(zip(p_refs, land_refs)):
            for j, (cx, cy) in enumerate(chips):
                sent.append(_remote(p.at[2 * cx + cy], land.at[j], send_sems, recv_sems, 3 * k + j, (cx, cy, c)))
        for r in range(1, N_DEV):
            fx, fy, fc = (r >> 2) & 1, (r >> 1) & 1, r & 1
            sent.append(_remote(rep_ref, reps_ref.at[me], send_sems, recv_sems, 3 * n - 1 + r, (x ^ fx, y ^ fy, c ^ fc)))
        for cp in sent:
            cp.start()
        for k, (p, land) in enumerate(zip(p_refs, land_refs)):
            for j in range(3):
                _remote(p.at[0], land.at[j], send_sems, recv_sems, 3 * k + j, (x, y, c)).wait_recv()
        for r in range(1, N_DEV):
            fx, fy, fc = (r >> 2) & 1, (r >> 1) & 1, r & 1
            frm = 4 * (x ^ fx) + 2 * (y ^ fy) + (c ^ fc)
            _remote(rep_ref, reps_ref.at[frm], send_sems, recv_sems, 3 * n - 1 + r, (x, y, c)).wait_recv()
        for cp in sent:
            cp.wait_send()
        own.wait()

    any_spec = pl.BlockSpec(memory_space=pl.ANY)
    n_sems = 3 * n + N_DEV - 1
    return pl.pallas_call(
        body, name=name, in_specs=[any_spec] * (n + 1), out_specs=[any_spec] * (n + 1),
        out_shape=[jax.ShapeDtypeStruct((3,) + p.shape[1:], p.dtype) for p in parts]
        + [jax.ShapeDtypeStruct((N_DEV, rr, C), F32)],
        scratch_shapes=[pltpu.SemaphoreType.DMA((n_sems,)), pltpu.SemaphoreType.DMA((n_sems,)),
                        pltpu.SemaphoreType.DMA((1,))],
    )(*parts, rep)


def _adamw_math(w, g, m, v):
    m = ADAM_B1 * m + (1.0 - ADAM_B1) * g
    v = ADAM_B2 * v + (1.0 - ADAM_B2) * (g * g)
    m_hat = m / (1.0 - ADAM_B1 ** ADAM_STEP)
    v_hat = v / (1.0 - ADAM_B2 ** ADAM_STEP)
    delta = -ADAM_LR * (m_hat / (jnp.sqrt(v_hat) + ADAM_EPS) + ADAM_WD * w)
    return delta, m, v


def _adamw_owned(part, landed, w, m, v, place, name, *, tr):
    _, R, C = part.shape

    def body(place_ref, own_ref, land_ref, w_ref, m_ref, v_ref, g_ref, d_ref, mo_ref, vo_ref):
        g = own_ref[0]
        for s in range(3):
            g = g + land_ref[s].astype(F32)
        delta, m_new, v_new = _adamw_math(w_ref[0], g, m_ref[0], v_ref[0])
        g_ref[0] = g
        d_ref[0] = delta
        mo_ref[0] = m_new
        vo_ref[0] = v_new

    half = pl.BlockSpec((1, tr, C), lambda i, place_ref: (place_ref[0], i, 0))
    grid_spec = pltpu.PrefetchScalarGridSpec(
        num_scalar_prefetch=1, grid=(R // tr,),
        in_specs=[pl.BlockSpec((1, tr, C), lambda i, place_ref: (place_ref[1], i, 0)),
                  pl.BlockSpec((3, tr, C), lambda i, place_ref: (0, i, 0)), half, half, half],
        out_specs=[half] * 4)
    return pl.pallas_call(
        body, name=name, grid_spec=grid_spec, out_shape=[jax.ShapeDtypeStruct((2, R, C), F32)] * 4,
        compiler_params=_cparams("parallel"),
    )(place, part, landed, w, m, v)


def _join_halves(bufs, name):
    n = len(bufs)

    def body(*refs):
        out_refs, (send_sems, recv_sems) = refs[n:2 * n], refs[2 * n:]
        x, y, c, _ = _place()
        copies = [_remote(o.at[c], o.at[c], send_sems, recv_sems, k, (x, y, 1 - c)) for k, o in enumerate(out_refs)]
        for cp in copies:
            cp.start()
        for k, o in enumerate(out_refs):
            _remote(o.at[c], o.at[1 - c], send_sems, recv_sems, k, (x, y, 1 - c)).wait_recv()
        for cp in copies:
            cp.wait_send()

    any_spec = pl.BlockSpec(memory_space=pl.ANY)
    return pl.pallas_call(
        body, name=name, in_specs=[any_spec] * n, out_specs=[any_spec] * n,
        out_shape=[jax.ShapeDtypeStruct(b.shape, b.dtype) for b in bufs],
        input_output_aliases={k: k for k in range(n)},
        scratch_shapes=[pltpu.SemaphoreType.DMA((n,)), pltpu.SemaphoreType.DMA((n,))],
    )(*bufs)


def _sum_adamw(own, landed, w, m, v, name, *, tr):
    n = landed.shape[0]
    hr, C = own.shape

    def body(own_ref, land_ref, w_ref, m_ref, v_ref, o_ref):
        g = own_ref[...]
        for s in range(n):
            g = g + land_ref[s]
        delta, m_new, v_new = _adamw_math(w_ref[...], g, m_ref[...], v_ref[...])
        o_ref[0] = g
        o_ref[1] = delta
        o_ref[2] = m_new
        o_ref[3] = v_new

    blk = pl.BlockSpec((tr, C), lambda i: (i, 0))
    return pl.pallas_call(
        body, name=name, grid=(hr // tr,),
        in_specs=[blk, pl.BlockSpec((n, tr, C), lambda i: (0, i, 0)), blk, blk, blk],
        out_specs=pl.BlockSpec((4, tr, C), lambda i: (0, i, 0)),
        out_shape=jax.ShapeDtypeStruct((4, hr, C), F32), compiler_params=_cparams("parallel"),
    )(own, landed, w, m, v)


def _rows_of(shape):
    n = 1
    for d in shape:
        n *= d
    return -(-n // PACK_COLS)


def _pack(arrays, total_rows, dtype):
    parts, used = [], 0
    for a in arrays:
        flat = a.reshape(-1).astype(dtype)
        fill = _rows_of(a.shape) * PACK_COLS - flat.shape[0]
        parts += [flat] + ([jnp.zeros((fill,), dtype)] if fill else [])
        used += _rows_of(a.shape)
    if total_rows > used:
        parts.append(jnp.zeros(((total_rows - used) * PACK_COLS,), dtype))
    return jnp.concatenate(parts).reshape(total_rows, PACK_COLS)


def _unpack(buf, shapes):
    lead = buf.shape[:-2]
    out, r = [], 0
    for shp in shapes:
        n = 1
        for d in shp:
            n *= d
        rows = _rows_of(shp)
        piece = buf[..., r:r + rows, :].reshape(lead + (rows * PACK_COLS,))[..., :n]
        out.append(piece.reshape(lead + tuple(shp)))
        r += rows
    return out


def _join_shards(stacked, axis):
    return jnp.concatenate([stacked[s] for s in range(N_CHIPS)], axis=axis)


def _shard_of(full, axis, chip):
    width = full.shape[axis] // N_CHIPS
    return lax.slice_in_dim(full, chip * width, (chip + 1) * width, axis=axis)


def _local_step(h0, tgt, W, *, seq, tm):
    Lp, D = h0.shape
    H = D // HEAD_DIM
    F2 = W["ffn_w_in"].shape[-1]
    F = F2 // 2
    te = tm // 2
    nq = Lp // tm
    cap = 1408
    tD, tF, tF2 = _pick(D, cap), _pick(F, cap), _pick(F2, cap)
    t2D = _pick(2 * D, cap)
    t2Dc, tF2c = _pick(2 * D // N_CHIPS, cap), _pick(F2 // N_CHIPS, cap)
    tcn = _pick(F, cap)

    def vec(a):
        return a.reshape(1, -1)

    ln_g, ln_b = W["ln_g"], W["ln_b"]
    wf_pad = jnp.pad(W["w_f"], ((0, 0), (0, LANES - H)))
    bf_pad = jnp.pad(W["b_f"], (0, LANES - H)).reshape(1, LANES)

    def ffn_fwd(hb, l, tag):
        u = _mm(hb, W["ffn_w_in"][l], "nn", F32, f"ffn{tag}_up", tm=tm, tn=tF2, tk=tD)
        act = _conv_glu_fwd(u, W["ffn_conv_w"][l], vec(W["ffn_conv_b"][l]), f"ffn{tag}_glu", tm=te, tn=tcn)
        y = _mm(act, W["ffn_w_out"][l], "nn", F32, f"ffn{tag}_down", tm=tm, tn=tD, tk=tF)
        return u, act, y

    def ffn_bwd(dzb, hb, u, act, l, tag, dw_in_acc, dw_out_acc):
        dact = _mm(dzb, W["ffn_w_out"][l], "nt", F32, f"ffn{tag}_dact", tm=tm, tn=tF, tk=tD)
        dw_out = _mm(act, dzb, "tn", F32, f"ffn{tag}_dwout", tm=tF, tn=tD, tk=tm, layer=l, into=dw_out_acc)
        dua, dug, dwa, dwg, dba, dbg = _conv_glu_bwd(u, dact, W["ffn_conv_w"][l], vec(W["ffn_conv_b"][l]),
                                                     f"ffn{tag}_dglu", tm=te, tn=tcn)
        du = jnp.concatenate([dua, dug], axis=1)
        dcw = jnp.concatenate([dwa, dwg], axis=1)
        dcb = jnp.concatenate([dba, dbg], axis=1)
        dh = _mm(du, W["ffn_w_in"][l], "nt", F32, f"ffn{tag}_dh", tm=tm, tn=tD, tk=tF2)
        dw_in = _mm(hb, du, "tn", F32, f"ffn{tag}_dwin", tm=tD, tn=tF2c, tk=tm, chips=True, layer=l, into=dw_in_acc)
        return dh, dw_in, dw_out, dcw, dcb[0]

    diffb, mixpre, h1, h1b, xh1, rs1 = _pool_ln_fwd(h0, W["pool_w"][0], W["pool_scale"], vec(ln_g[0, 0]),
                                                    vec(ln_b[0, 0]), "pool_ln_fwd", tm=te)
    u0, act0, y0 = ffn_fwd(h1b, 0, "0")
    h2, h2b, xh2, rs2 = _ln_fwd(h1, y0, vec(ln_g[0, 1]), vec(ln_b[0, 1]), "ln01_fwd", tm=te)

    kvb = _mm(h2b, W["w_kv"], "nn", BF16, "kv_proj", tm=tm, tn=t2D, tk=tD)
    qb = _mm(h2b, W["w_q"][0], "nn", BF16, "q_proj", tm=tm, tn=tD, tk=tD, scale=HEAD_DIM ** -0.5)
    pre = _mm(h2b, wf_pad, "nn", F32, "f_proj", tm=tm, tn=LANES, tk=tD)
    c = _logf_cumsum(pre, bf_pad, "logf_cumsum", tm=tm)

    crow4 = c[:, :H].T.reshape(H, nq, 1, tm)
    o_tok, ob, lse4 = _attn_fwd(qb, kvb, crow4, "attn_fwd", tq=tm)
    y_attn = _mm(ob, W["w_o"][0], "nn", F32, "o_proj", tm=tm, tn=tD, tk=tD)
    h3, h3b, xh3, rs3 = _ln_fwd(h2, y_attn, vec(ln_g[1, 0]), vec(ln_b[1, 0]), "ln10_fwd", tm=te)
    u1, act1, y1 = ffn_fwd(h3b, 1, "1")
    h4, _, xh4, rs4 = _ln_fwd(h3, y1, vec(ln_g[1, 1]), vec(ln_b[1, 1]), "ln11_fwd", tm=te)
    dy, loss = _loss_head(h4, tgt, "loss_head", tm=te, row_lo=N_META, row_hi=N_META + seq)

    dz4, dz4b, dg11, db11 = _ln_bwd([dy], [1.0], xh4, rs4, vec(ln_g[1, 1]), "ln11_bwd", tm=te)
    dh3, dw_in, dw_out, dcw1, dcb1 = ffn_bwd(dz4b, h3b, u1, act1, 1, "1", None, None)
    dz3, dz3b, dg10, db10 = _ln_bwd([dz4, dh3], [ALPHA, 1.0], xh3, rs3, vec(ln_g[1, 0]), "ln10_bwd", tm=te)

    dob = _mm(dz3b, W["w_o"][0], "nt", BF16, "o_proj_dx", tm=tm, tn=tD, tk=tD)
    dw_o = _mm(ob, dz3b, "tn", F32, "o_proj_dw", tm=tD, tn=tD, tk=tm)
    delta = _attn_delta(dob, o_tok, "attn_delta", tm=te, n_heads=H)
    dqb, dkb, dvb, dcs, dcq = _attn_bwd(qb, dob, kvb, lse4, delta[:, :H].T.reshape(H, nq, 1, tm), crow4,
                                        "attn_bwd", tq=tm)
    dc_keys = jnp.pad(dcs.reshape(H, Lp).T, ((0, 0), (0, LANES - H)))
    dc_queries = jnp.pad(dcq.reshape(H, Lp).T, ((0, 0), (0, LANES - H)))
    dpreb, dbf = _logf_bwd(dc_keys, dc_queries, pre, bf_pad, "logf_bwd", tm=tm)

    qs = HEAD_DIM ** -0.5
    dw_q = _mm(h2b, dqb, "tn", F32, "q_proj_dw", tm=tD, tn=tD, tk=tm, scale=qs)
    dw_kv = _mm(h2b, dkb, "tn", F32, "k_proj_dw", tm=tD, tn=t2Dc, tk=tm, chips=(0, N_CHIPS // 2))
    dw_kv = _mm(h2b, dvb, "tn", F32, "v_proj_dw", tm=tD, tn=t2Dc, tk=tm, chips=(N_CHIPS // 2, N_CHIPS // 2), into=dw_kv)
    dw_f = _mm(h2b, dpreb, "tn", F32, "f_proj_dw", tm=tD, tn=LANES, tk=tm)[:, :H]
    dh2 = _mm(dqb, W["w_q"][0], "nt", F32, "q_proj_dx", tm=tm, tn=tD, tk=tD, scale=qs)
    dh2 = _mm(dkb, W["w_kv"][:, :D], "nt", F32, "k_proj_dx", tm=tm, tn=tD, tk=tD, add=dh2)
    dh2 = _mm(dvb, W["w_kv"][:, D:], "nt", F32, "v_proj_dx", tm=tm, tn=tD, tk=tD, add=dh2)
    dh2 = _mm(dpreb, wf_pad, "nt", F32, "f_proj_dx", tm=tm, tn=tD, tk=LANES, add=dh2)
    dz2, dz2b, dg01, db01 = _ln_bwd([dz3, dh2], [ALPHA, 1.0], xh2, rs2, vec(ln_g[0, 1]), "ln01_bwd", tm=te)

    dh1, dw_in, dw_out, dcw0, dcb0 = ffn_bwd(dz2b, h1b, u0, act0, 0, "0", dw_in, dw_out)
    dz1, _, dg00, db00 = _ln_bwd([dz2, dh1], [ALPHA, 1.0], xh1, rs1, vec(ln_g[0, 0]), "ln00_bwd", tm=te)
    dh0, dmb, dscale = _pool_bwd(dz1, mixpre, W["pool_w"][0], W["pool_scale"], "pool_bwd", tm=te)
    dw_pool = _pool_dw(diffb, dmb, "pool_dw", tk=tm)

    grads = {
        "meta": dh0[:N_META],
        "pool_w": dw_pool[None],
        "pool_scale": dscale,
        "w_kv": dw_kv,
        "w_f": dw_f,
        "b_f": dbf[0, :H],
        "w_q": dw_q[None],
        "w_o": dw_o[None],
        "ffn_w_in": dw_in,
        "ffn_conv_w": jnp.stack([dcw0, dcw1]),
        "ffn_conv_b": jnp.stack([dcb0, dcb1]),
        "ffn_w_out": dw_out,
        "ln_g": jnp.stack([jnp.stack([dg00[0], dg01[0]]), jnp.stack([dg10[0], dg11[0]])]),
        "ln_b": jnp.stack([jnp.stack([db00[0], db01[0]]), jnp.stack([db10[0], db11[0]])]),
    }
    return loss, dh0, grads


def _row_block(rows, cols):
    best = SUBLANES
    for t in range(SUBLANES, rows + 1, SUBLANES):
        if rows % t == 0 and t * cols * 4 <= ELEMENTWISE_BLOCK_BYTES:
            best = t
    return best


def _row_tile(length):
    return 640 if length >= 4096 else 128


def kernel(x, meta, pool_w, pool_scale, w_kv, w_f, b_f, w_q, w_o, ffn_w_in, ffn_conv_w, ffn_conv_b, ffn_w_out, ln_g, ln_b, loss_target, m_meta, m_pool_w, m_pool_scale, m_w_kv, m_w_f, m_b_f, m_w_q, m_w_o, m_ffn_w_in, m_ffn_conv_w, m_ffn_conv_b, m_ffn_w_out, m_ln_g, m_ln_b, v_meta, v_pool_w, v_pool_scale, v_w_kv, v_w_f, v_b_f, v_w_q, v_w_o, v_ffn_w_in, v_ffn_conv_w, v_ffn_conv_b, v_ffn_w_out, v_ln_g, v_ln_b):
    weights = dict(meta=meta, pool_w=pool_w, pool_scale=pool_scale, w_kv=w_kv, w_f=w_f, b_f=b_f, w_q=w_q, w_o=w_o,
                   ffn_w_in=ffn_w_in, ffn_conv_w=ffn_conv_w, ffn_conv_b=ffn_conv_b, ffn_w_out=ffn_w_out,
                   ln_g=ln_g, ln_b=ln_b)
    mom1 = dict(meta=m_meta, pool_w=m_pool_w, pool_scale=m_pool_scale, w_kv=m_w_kv, w_f=m_w_f, b_f=m_b_f, w_q=m_w_q,
                w_o=m_w_o, ffn_w_in=m_ffn_w_in, ffn_conv_w=m_ffn_conv_w, ffn_conv_b=m_ffn_conv_b,
                ffn_w_out=m_ffn_w_out, ln_g=m_ln_g, ln_b=m_ln_b)
    mom2 = dict(meta=v_meta, pool_w=v_pool_w, pool_scale=v_pool_scale, w_kv=v_w_kv, w_f=v_w_f, b_f=v_b_f, w_q=v_w_q,
                w_o=v_w_o, ffn_w_in=v_ffn_w_in, ffn_conv_w=v_ffn_conv_w, ffn_conv_b=v_ffn_conv_b,
                ffn_w_out=v_ffn_w_out, ln_g=v_ln_g, ln_b=v_ln_b)
    _, seq, D = x.shape
    L = N_META + seq
    tm = _row_tile(L)
    Lp = _round_up(L, tm)
    c_idx = lax.axis_index("c")
    chip = 2 * lax.axis_index("x") + lax.axis_index("y")

    shard_shapes = {n: weights[n].shape for n in SHARDED}
    rows_b = _round_up(sum(_rows_of(shard_shapes[n]) for n in MATMUL_WEIGHTS), 32)
    rows_f = _round_up(sum(_rows_of(shard_shapes[n]) for n in VECTOR_WEIGHTS), SUBLANES)
    wb = _pack([weights[n] for n in MATMUL_WEIGHTS], rows_b, BF16)
    wf = _pack([weights[n] for n in VECTOR_WEIGHTS], rows_f, F32)
    gb, gf = _all_gather_weights(wb, wf, "weights_all_gather")
    gb = lax.dynamic_update_index_in_dim(gb, wb, chip, axis=0)
    gf = lax.dynamic_update_index_in_dim(gf, wf, chip, axis=0)
    full = {}
    for names, buf in ((MATMUL_WEIGHTS, gb), (VECTOR_WEIGHTS, gf)):
        for n, stacked in zip(names, _unpack(buf, [shard_shapes[n] for n in names])):
            full[n] = _join_shards(stacked, SHARD_AXIS[n])
    full["b_f"] = b_f
    full["ffn_conv_b"] = ffn_conv_b

    pad = jnp.zeros((Lp - L, D), F32)
    h0 = jnp.concatenate([full["meta"], x[0], pad], axis=0)
    tgt = jnp.concatenate([jnp.zeros((N_META, D), F32), loss_target[0], pad], axis=0)
    loss, dh0, grads = _local_step(h0, tgt, full, seq=seq, tm=tm)
    loss = lax.psum(loss[0, 0], AXES)
    grad_x = dh0[N_META:L][None]

    core = c_idx.astype(jnp.int32).reshape(1)
    place = jnp.stack([c_idx, chip]).astype(jnp.int32)
    small_shapes = [shard_shapes[n] for n in SMALL_SHARDED]
    rows_s = _round_up(sum(_rows_of(s) for s in small_shapes), 2 * LANES)

    def packed_small(d):
        return _pack([d[n] for n in SMALL_SHARDED], rows_s, F32).reshape(2, rows_s // 2, PACK_COLS)

    names, orders, wires, g_views, wmv = [], [], [], [], []
    for n, order in BIG_SHARDED:
        shp = shard_shapes[n]
        C = shp[-1]
        R = weights[n].size // C // 2
        lead = (N_CHIPS, 2) if order == "CH" else (2, N_CHIPS)
        names.append(n)
        orders.append(order)
        wires.append(BF16)
        g_views.append(grads[n].reshape(lead + (R, C)))
        wmv.append([d[n].reshape(2, R, C) for d in (weights, mom1, mom2)])
    names.append("small")
    orders.append("CH")
    wires.append(F32)
    g_views.append(jnp.stack([_pack([_shard_of(grads[n], SHARD_AXIS[n], s) for n in SMALL_SHARDED], rows_s, F32)
                              for s in range(N_CHIPS)]).reshape(N_CHIPS, 2, rows_s // 2, PACK_COLS))
    wmv.append([packed_small(d) for d in (weights, mom1, mom2)])

    from_sibling = _halves_to_sibling(g_views, orders, "grads_to_sibling")
    parts, on_wire = [], []
    for n, order, wire, g, a in zip(names, orders, wires, g_views, from_sibling):
        p, pw = _chip_partial(g, a, core, order, wire, f"chip_sum_{n}", tr=_row_block(a.shape[1], a.shape[2]))
        parts.append(p)
        on_wire.append(pw)

    rep_shapes = [weights[n].shape for n in REPLICATED]
    rows_r = _round_up(sum(_rows_of(s) for s in rep_shapes), SUBLANES)
    rep = _pack([grads[n] for n in REPLICATED], rows_r, F32)
    *landed, reps = _chip_exchange(on_wire, rep, "grads_chip_exchange")

    halves = []
    for n, p, b, (w_, m_, v_) in zip(names, parts, landed, wmv):
        halves += _adamw_owned(p, b, w_, m_, v_, place, f"adamw_{n}", tr=_row_block(p.shape[1], p.shape[2]))
    joined = _join_halves(halves, "results_to_sibling")
    out = {}
    for k, n in enumerate(names[:-1]):
        out[n] = [a.reshape(shard_shapes[n]) for a in joined[4 * k:4 * k + 4]]
    small_out = [_unpack(a.reshape(rows_s, PACK_COLS), small_shapes) for a in joined[-4:]]
    for k, n in enumerate(SMALL_SHARDED):
        out[n] = [small_out[kind][k] for kind in range(4)]

    def packr(d):
        return _pack([d[n] for n in REPLICATED], rows_r, F32)

    res_r = _sum_adamw(reps[0], reps[1:], packr(weights), packr(mom1), packr(mom2), "adamw_replicated", tr=rows_r)
    rep_out = _unpack(res_r, rep_shapes)

    out.update({n: a for n, a in zip(REPLICATED, rep_out)})
    result = [loss, grad_x]
    for k in range(4):
        result += [out[n][k] for n in WEIGHT_ORDER]
    return tuple(result)
```

```python
import functools

import jax
import jax.numpy as jnp
from jax import lax
from jax.experimental import pallas as pl
from jax.experimental.pallas import tpu as pltpu

N_META = 16
POOL_WINDOWS = (2, 4, 8, 16)
MAX_WINDOW = max(POOL_WINDOWS)
N_GROUPS = len(POOL_WINDOWS)
HEAD_DIM = 64
DEPTH = 2
CONV_WIDTH = 3
ALPHA = (2.0 * DEPTH) ** 0.25
LN_EPS = 1e-5
NEG_INF = -1e30
ADAM_LR = 0.001
ADAM_B1 = 0.9
ADAM_B2 = 0.999
ADAM_EPS = 1e-08
ADAM_WD = 0.01
ADAM_STEP = 10

F32 = jnp.float32
BF16 = jnp.bfloat16
ATTN_STRIP = 32
GLU_STRIP = 16
LANES = 128
SUBLANES = 8
PACK_COLS = 1024
VMEM_LIMIT = 56 * 1024 * 1024
AXES = ("x", "y", "c")
MESH = pl.DeviceIdType.MESH

NN = (((1,), (0,)), ((), ()))
NT = (((1,), (1,)), ((), ()))
TN = (((0,), (0,)), ((), ()))

SHARD_AXIS = {"meta": 1, "pool_w": 2, "pool_scale": 1, "w_kv": 1, "w_f": 0, "w_q": 1, "w_o": 1,
              "ffn_w_in": 2, "ffn_conv_w": 2, "ffn_w_out": 1, "ln_g": 2, "ln_b": 2}
SHARDED = ("meta", "pool_w", "pool_scale", "w_kv", "w_f", "w_q", "w_o", "ffn_w_in", "ffn_conv_w",
           "ffn_w_out", "ln_g", "ln_b")
REPLICATED = ("b_f", "ffn_conv_b")
MATMUL_WEIGHTS = ("pool_w", "w_kv", "w_f", "w_q", "w_o", "ffn_w_in", "ffn_w_out")
VECTOR_WEIGHTS = ("meta", "pool_scale", "ffn_conv_w", "ln_g", "ln_b")
WEIGHT_ORDER = ("meta", "pool_w", "pool_scale", "w_kv", "w_f", "b_f", "w_q", "w_o", "ffn_w_in",
                "ffn_conv_w", "ffn_conv_b", "ffn_w_out", "ln_g", "ln_b")
BIG_SHARDED = (("w_kv", "CH"), ("w_q", "CH"), ("w_o", "CH"), ("ffn_w_in", "HC"), ("ffn_w_out", "HC"))
SMALL_SHARDED = ("meta", "pool_w", "pool_scale", "w_f", "ffn_conv_w", "ln_g", "ln_b")
ELEMENTWISE_BLOCK_BYTES = 3 * 512 * 1024
N_CHIPS = 4
N_DEV = 8


def _cparams(*sem):
    return pltpu.CompilerParams(dimension_semantics=sem, vmem_limit_bytes=VMEM_LIMIT)


def _round_up(n, m):
    return (n + m - 1) // m * m


def _pick(n, cap):
    if n <= cap:
        return n
    best = 0
    for t in range(LANES, cap + 1, LANES):
        if n % t == 0:
            best = t
    assert best, (n, cap)
    return best


def _mm(a, b, mode, out_dtype, name, *, tm, tn, tk, scale=None, add=None, chips=False, layer=None, into=None):
    if mode == "nn":
        (M, K), N = a.shape, b.shape[1]
    elif mode == "nt":
        (M, K), N = a.shape, b.shape[0]
    else:
        (K, M), N = a.shape, b.shape[1]
    assert M % tm == 0 and N % tn == 0 and K % tk == 0, (name, M, N, K, tm, tn, tk)
    nk = K // tk
    dn = {"nn": NN, "nt": NT, "tn": TN}[mode]
    has_add = add is not None
    has_into = into is not None
    assert not (has_add and (chips or layer is not None))

    def body(*refs):
        a_ref, b_ref = refs[0], refs[1]
        add_ref = refs[2] if has_add else None
        o_ref = refs[2 + has_add + has_into]
        acc_ref = refs[-1] if nk > 1 else None
        k = pl.program_id(2)
        part = lax.dot_general(a_ref[...], b_ref[...], dn, preferred_element_type=F32)

        def finish(r):
            if scale is not None:
                r = r * scale
            if has_add:
                r = r + add_ref[...]
            o_ref[...] = r.astype(out_dtype).reshape(o_ref.shape)

        if nk == 1:
            finish(part)
        else:
            @pl.when(k == 0)
            def _():
                acc_ref[...] = part

            @pl.when(k > 0)
            def _():
                acc_ref[...] += part

            @pl.when(k == nk - 1)
            def _():
                finish(acc_ref[...])

    if mode == "nn":
        a_spec = pl.BlockSpec((tm, tk), lambda j, i, k: (i, k))
        b_spec = pl.BlockSpec((tk, tn), lambda j, i, k: (k, j))
    elif mode == "nt":
        a_spec = pl.BlockSpec((tm, tk), lambda j, i, k: (i, k))
        b_spec = pl.BlockSpec((tn, tk), lambda j, i, k: (j, k))
    else:
        a_spec = pl.BlockSpec((tk, tm), lambda j, i, k: (k, i))
        b_spec = pl.BlockSpec((tk, tn), lambda j, i, k: (k, j))
    out_dims, blk = (M, N), (tm, tn)
    if chips:
        base, count = (0, N_CHIPS) if chips is True else chips
        per_chip = N // count // tn
        assert per_chip * tn * count == N, (name, N, tn)
        out_dims, blk = (N_CHIPS, M, N // count), (1, tm, tn)
        where = lambda j, i: (base + j // per_chip, i, j % per_chip)
    else:
        where = lambda j, i: (i, j)
    if layer is not None:
        out_dims, blk = (DEPTH,) + out_dims, (1,) + blk
        o_spec = pl.BlockSpec(blk, lambda j, i, k: (layer,) + where(j, i))
    else:
        o_spec = pl.BlockSpec(blk, lambda j, i, k: where(j, i))
    in_specs = [a_spec, b_spec] + ([o_spec] if has_add else []) + ([pl.BlockSpec(memory_space=pl.ANY)] if has_into else [])
    args = (a, b) + ((add,) if has_add else ()) + ((into,) if has_into else ())
    return pl.pallas_call(
        body, name=name, grid=(N // tn, M // tm, nk),
        in_specs=in_specs, out_specs=o_spec,
        out_shape=jax.ShapeDtypeStruct(out_dims, out_dtype),
        input_output_aliases={len(args) - 1: 0} if has_into else {},
        scratch_shapes=[pltpu.VMEM((tm, tn), F32)] if nk > 1 else [],
        compiler_params=_cparams("parallel", "parallel", "arbitrary"),
    )(*args)


def _ln_math(z, g, b):
    mu = jnp.mean(z, axis=-1, keepdims=True)
    zc = z - mu
    var = jnp.mean(zc * zc, axis=-1, keepdims=True)
    rstd = lax.rsqrt(var + LN_EPS)
    xh = zc * rstd
    return xh * g + b, xh, rstd


def _mm_ln(a, b, mode, name, *, tm, tk, forward, rows, vecs, scale=None, add=None):
    assert mode in ("nn", "nt")
    M, K = a.shape
    N = b.shape[1] if mode == "nn" else b.shape[0]
    assert M % tm == 0 and K % tk == 0, (name, M, K, tm, tk)
    nk = K // tk
    ni = M // tm
    dn = {"nn": NN, "nt": NT}[mode]
    has_add = add is not None
    n_in = 2 + has_add + len(rows) + len(vecs)

    def body(*refs):
        a_ref, b_ref = refs[0], refs[1]
        add_ref = refs[2] if has_add else None
        row_refs = refs[2 + has_add:2 + has_add + len(rows)]
        vec_refs = refs[2 + has_add + len(rows):n_in]
        outs = refs[n_in:n_in + 4]
        acc_ref = refs[-1] if nk > 1 else None
        i, k = pl.program_id(0), pl.program_id(1)
        part = lax.dot_general(a_ref[...], b_ref[...], dn, preferred_element_type=F32)

        def finish(y):
            if scale is not None:
                y = y * scale
            if has_add:
                y = y + add_ref[...]
            if forward:
                h, xh, rstd = _ln_math(ALPHA * row_refs[0][...] + y, vec_refs[0][...], vec_refs[1][...])
                outs[0][...] = h
                outs[1][...] = h.astype(BF16)
                outs[2][...] = xh
                outs[3][...] = rstd
            else:
                dy = ALPHA * row_refs[0][...] + y
                x = row_refs[1][...]
                dxh = dy * vec_refs[0][...]
                m1 = jnp.mean(dxh, axis=-1, keepdims=True)
                m2 = jnp.mean(dxh * x, axis=-1, keepdims=True)
                dz = row_refs[2][...] * (dxh - m1 - x * m2)
                outs[0][...] = dz
                outs[1][...] = dz.astype(BF16)

                @pl.when(i == 0)
                def _():
                    outs[2][...] = jnp.zeros_like(outs[2])
                    outs[3][...] = jnp.zeros_like(outs[3])

                outs[2][...] += jnp.sum(dy * x, axis=0, keepdims=True)
                outs[3][...] += jnp.sum(dy, axis=0, keepdims=True)

        if nk == 1:
            finish(part)
        else:
            @pl.when(k == 0)
            def _():
                acc_ref[...] = part

            @pl.when(k > 0)
            def _():
                acc_ref[...] += part

            @pl.when(k == nk - 1)
            def _():
                finish(acc_ref[...])

    a_spec = pl.BlockSpec((tm, tk), lambda i, k: (i, k))
    b_spec = pl.BlockSpec((tk, N), lambda i, k: (k, 0)) if mode == "nn" else pl.BlockSpec((N, tk), lambda i, k: (0, k))
    row = pl.BlockSpec((tm, N), lambda i, k: (i, 0))
    col = pl.BlockSpec((tm, 1), lambda i, k: (i, 0))
    vec = pl.BlockSpec((1, N), lambda i, k: (0, 0))
    row_specs = [row if r.shape[1] == N else col for r in rows]
    if forward:
        out_specs = [row, row, row, col]
        out_shape = [jax.ShapeDtypeStruct((M, N), F32), jax.ShapeDtypeStruct((M, N), BF16),
                     jax.ShapeDtypeStruct((M, N), F32), jax.ShapeDtypeStruct((M, 1), F32)]
    else:
        out_specs = [row, row, vec, vec]
        out_shape = [jax.ShapeDtypeStruct((M, N), F32), jax.ShapeDtypeStruct((M, N), BF16),
                     jax.ShapeDtypeStruct((1, N), F32), jax.ShapeDtypeStruct((1, N), F32)]
    args = (a, b) + ((add,) if has_add else ()) + tuple(rows) + tuple(vecs)
    return pl.pallas_call(
        body, name=name, grid=(ni, nk),
        in_specs=[a_spec, b_spec] + ([row] if has_add else []) + row_specs + [vec] * len(vecs),
        out_specs=out_specs, out_shape=out_shape,
        scratch_shapes=[pltpu.VMEM((tm, N), F32)] if nk > 1 else [],
        compiler_params=_cparams("parallel" if forward else "arbitrary", "arbitrary"),
    )(*args)


def _pool_ln_fwd(h0, pw, ps, g, b, name, *, tm):
    Lp, D = h0.shape
    G = D // N_GROUPS
    halo_blocks = tm // MAX_WINDOW

    def body(x_ref, halo_ref, pw_ref, ps_ref, g_ref, b_ref,
             diff_ref, mix_ref, h_ref, hb_ref, xh_ref, rs_ref, ext_ref):
        i = pl.program_id(0)
        ext_ref[0:MAX_WINDOW, :] = jnp.where(i == 0, 0.0, halo_ref[...])
        ext_ref[MAX_WINDOW:MAX_WINDOW + tm, :] = x_ref[...]
        t1 = (i * tm + 1 + lax.broadcasted_iota(jnp.int32, (tm, 1), 0)).astype(F32)
        for gi, w in enumerate(POOL_WINDOWS):
            lo, hi = gi * G, (gi + 1) * G
            xg = x_ref[:, lo:hi]
            win = xg
            for j in range(1, w):
                win = win + ext_ref[MAX_WINDOW - j:MAX_WINDOW - j + tm, lo:hi]
            d = (win / jnp.minimum(t1, float(w)) - xg).astype(BF16)
            diff_ref[:, lo:hi] = d
            mix_ref[:, lo:hi] = jnp.dot(d, pw_ref[gi], preferred_element_type=F32)
        z = ALPHA * x_ref[...] + mix_ref[...] * ps_ref[...]
        h, xh, rstd = _ln_math(z, g_ref[...], b_ref[...])
        h_ref[...] = h
        hb_ref[...] = h.astype(BF16)
        xh_ref[...] = xh
        rs_ref[...] = rstd

    row = pl.BlockSpec((tm, D), lambda i: (i, 0))
    vec = pl.BlockSpec((1, D), lambda i: (0, 0))
    return pl.pallas_call(
        body, name=name, grid=(Lp // tm,),
        in_specs=[row,
                  pl.BlockSpec((MAX_WINDOW, D), lambda i: (jnp.maximum(i * halo_blocks - 1, 0), 0)),
                  pl.BlockSpec((N_GROUPS, G, G), lambda i: (0, 0, 0)), vec, vec, vec],
        out_specs=[row, row, row, row, row, pl.BlockSpec((tm, 1), lambda i: (i, 0))],
        out_shape=[jax.ShapeDtypeStruct((Lp, D), BF16), jax.ShapeDtypeStruct((Lp, D), F32),
                   jax.ShapeDtypeStruct((Lp, D), F32), jax.ShapeDtypeStruct((Lp, D), BF16),
                   jax.ShapeDtypeStruct((Lp, D), F32), jax.ShapeDtypeStruct((Lp, 1), F32)],
        scratch_shapes=[pltpu.VMEM((tm + MAX_WINDOW, D), F32)],
        compiler_params=_cparams("parallel"),
    )(h0, h0, pw, ps, g, b)


def _pool_bwd(dz, mixpre, pw, ps, name, *, tm):
    Lp, D = dz.shape
    G = D // N_GROUPS
    halo_blocks = tm // MAX_WINDOW
    n_halo = Lp // MAX_WINDOW
    ni = Lp // tm
    R = tm + MAX_WINDOW

    def body(dz_ref, halo_ref, mix_ref, pw_ref, ps_ref, dh_ref, dmb_ref, dsc_ref, ext_ref, dp_ref):
        i = pl.program_id(0)
        ext_ref[0:tm, :] = dz_ref[...]
        ext_ref[tm:R, :] = jnp.where(i == ni - 1, 0.0, halo_ref[...])
        dmix = (ext_ref[...] * ps_ref[...]).astype(BF16)
        dmb_ref[...] = dmix[0:tm]

        @pl.when(i == 0)
        def _():
            dsc_ref[...] = jnp.zeros_like(dsc_ref)

        dsc_ref[...] += jnp.sum(dz_ref[...] * mix_ref[...], axis=0, keepdims=True)
        t1 = (i * tm + 1 + lax.broadcasted_iota(jnp.int32, (R, 1), 0)).astype(F32)
        for gi, w in enumerate(POOL_WINDOWS):
            lo, hi = gi * G, (gi + 1) * G
            dd = lax.dot_general(dmix[:, lo:hi], pw_ref[gi], NT, preferred_element_type=F32)
            dp_ref[:, lo:hi] = dd / jnp.minimum(t1, float(w))
            back = dp_ref[0:tm, lo:hi]
            for j in range(1, w):
                back = back + dp_ref[j:j + tm, lo:hi]
            dh_ref[:, lo:hi] = ALPHA * dz_ref[:, lo:hi] - dd[0:tm] + back

    row = pl.BlockSpec((tm, D), lambda i: (i, 0))
    vec = pl.BlockSpec((1, D), lambda i: (0, 0))
    return pl.pallas_call(
        body, name=name, grid=(ni,),
        in_specs=[row,
                  pl.BlockSpec((MAX_WINDOW, D), lambda i: (jnp.minimum((i + 1) * halo_blocks, n_halo - 1), 0)),
                  row, pl.BlockSpec((N_GROUPS, G, G), lambda i: (0, 0, 0)), vec],
        out_specs=[row, row, vec],
        out_shape=[jax.ShapeDtypeStruct((Lp, D), F32), jax.ShapeDtypeStruct((Lp, D), BF16),
                   jax.ShapeDtypeStruct((1, D), F32)],
        scratch_shapes=[pltpu.VMEM((R, D), F32), pltpu.VMEM((R, D), F32)],
        compiler_params=_cparams("arbitrary"),
    )(dz, dz, mixpre, pw, ps)


def _pool_dw(diffb, dmb, name, *, tk):
    Lp, D = diffb.shape
    G = D // N_GROUPS

    def body(a_ref, b_ref, o_ref):
        @pl.when(pl.program_id(1) == 0)
        def _():
            o_ref[...] = jnp.zeros_like(o_ref)

        o_ref[0] += lax.dot_general(a_ref[...], b_ref[...], TN, preferred_element_type=F32)

    blk = pl.BlockSpec((tk, G), lambda g, k: (k, g))
    return pl.pallas_call(
        body, name=name, grid=(N_GROUPS, Lp // tk),
        in_specs=[blk, blk], out_specs=pl.BlockSpec((1, G, G), lambda g, k: (g, 0, 0)),
        out_shape=jax.ShapeDtypeStruct((N_GROUPS, G, G), F32),
        compiler_params=_cparams("parallel", "arbitrary"),
    )(diffb, dmb)


def _ln_bwd(parts, coefs, xh, rs, g, name, *, tm):
    Lp, D = xh.shape
    n = len(parts)

    def body(*refs):
        part_refs = refs[:n]
        xh_ref, rs_ref, g_ref = refs[n:n + 3]
        dz_ref, dzb_ref, dg_ref, db_ref = refs[n + 3:]
        dy = part_refs[0][...] if coefs[0] == 1.0 else coefs[0] * part_refs[0][...]
        for c, r in zip(coefs[1:], part_refs[1:]):
            dy = dy + (r[...] if c == 1.0 else c * r[...])
        x = xh_ref[...]
        dxh = dy * g_ref[...]
        m1 = jnp.mean(dxh, axis=-1, keepdims=True)
        m2 = jnp.mean(dxh * x, axis=-1, keepdims=True)
        dz = rs_ref[...] * (dxh - m1 - x * m2)
        dz_ref[...] = dz
        dzb_ref[...] = dz.astype(BF16)

        @pl.when(pl.program_id(0) == 0)
        def _():
            dg_ref[...] = jnp.zeros_like(dg_ref)
            db_ref[...] = jnp.zeros_like(db_ref)

        dg_ref[...] += jnp.sum(dy * x, axis=0, keepdims=True)
        db_ref[...] += jnp.sum(dy, axis=0, keepdims=True)

    row = pl.BlockSpec((tm, D), lambda i: (i, 0))
    vec = pl.BlockSpec((1, D), lambda i: (0, 0))
    return pl.pallas_call(
        body, name=name, grid=(Lp // tm,),
        in_specs=[row] * n + [row, pl.BlockSpec((tm, 1), lambda i: (i, 0)), vec],
        out_specs=[row, row, vec, vec],
        out_shape=[jax.ShapeDtypeStruct((Lp, D), F32), jax.ShapeDtypeStruct((Lp, D), BF16),
                   jax.ShapeDtypeStruct((1, D), F32), jax.ShapeDtypeStruct((1, D), F32)],
        compiler_params=_cparams("arbitrary"),
    )(*parts, xh, rs, g)


def _loss_head(h, tgt, name, *, tm, row_lo, row_hi):
    Lp, D = h.shape

    def body(h_ref, t_ref, dy_ref, loss_ref):
        i = pl.program_id(0)
        r = i * tm + lax.broadcasted_iota(jnp.int32, (tm, 1), 0)
        valid = (r >= row_lo) & (r < row_hi)
        e = jnp.where(valid, h_ref[...] - t_ref[...], 0.0)
        dy_ref[...] = e * (1.0 / D)

        @pl.when(i == 0)
        def _():
            loss_ref[...] = jnp.zeros_like(loss_ref)

        loss_ref[...] += 0.5 * jnp.sum(jnp.mean(e * e, axis=-1, keepdims=True), axis=0, keepdims=True)

    row = pl.BlockSpec((tm, D), lambda i: (i, 0))
    return pl.pallas_call(
        body, name=name, grid=(Lp // tm,),
        in_specs=[row, row], out_specs=[row, pl.BlockSpec((1, 1), lambda i: (0, 0))],
        out_shape=[jax.ShapeDtypeStruct((Lp, D), F32), jax.ShapeDtypeStruct((1, 1), F32)],
        compiler_params=_cparams("arbitrary"),
    )(h, tgt)


def _shift_rows_down(cur, prev, s, sub):
    return jnp.where(sub >= s, pltpu.roll(cur, s, 0), pltpu.roll(prev, s, 0))


def _shift_rows_up(cur, nxt, s, sub):
    return jnp.where(sub < SUBLANES - s, pltpu.roll(cur, SUBLANES - s, 0), pltpu.roll(nxt, SUBLANES - s, 0))


def _conv_group(cur, prev, cw_ref, cb_ref, sub):
    taps = [_shift_rows_down(cur, prev, 2, sub), _shift_rows_down(cur, prev, 1, sub), cur]
    c = cb_ref[...] + cw_ref[0:1, :] * taps[0] + cw_ref[1:2, :] * taps[1] + cw_ref[2:3, :] * taps[2]
    return c, taps


def _conv_glu_fwd(u, cw, cb, name, *, tm, tn):
    Lp, F2 = u.shape
    F = F2 // 2
    nj = F // tn
    halo_blocks = tm // SUBLANES
    S8 = SUBLANES
    assert GLU_STRIP == 2 * S8 and tm % GLU_STRIP == 0

    def body(ua_ref, ug_ref, pa_ref, pg_ref, cwa_ref, cwg_ref, cba_ref, cbg_ref, o_ref):
        first = pl.program_id(1) == 0
        sub = lax.broadcasted_iota(jnp.int32, (S8, tn), 0)

        def strip(r, prev_a, prev_g):
            out = []
            for g0 in (0, S8):
                a_cur = ua_ref[pl.ds(r + g0, S8), :]
                g_cur = ug_ref[pl.ds(r + g0, S8), :]
                a, _ = _conv_group(a_cur, prev_a, cwa_ref, cba_ref, sub)
                gate, _ = _conv_group(g_cur, prev_g, cwg_ref, cbg_ref, sub)
                out.append(a * jax.nn.sigmoid(a) * gate)
                prev_a, prev_g = a_cur, g_cur
            o_ref[pl.ds(r, GLU_STRIP), :] = jnp.concatenate(out, axis=0).astype(BF16)

        strip(0, jnp.where(first, 0.0, pa_ref[...]), jnp.where(first, 0.0, pg_ref[...]))

        def step(k, carry):
            r = pl.multiple_of(k * GLU_STRIP, GLU_STRIP)
            before = pl.ds(pl.multiple_of(r - S8, S8), S8)
            strip(r, ua_ref[before, :], ug_ref[before, :])
            return carry

        lax.fori_loop(1, tm // GLU_STRIP, step, 0)

    def prev(off):
        return pl.BlockSpec((SUBLANES, tn), lambda j, i: (jnp.maximum(i * halo_blocks - 1, 0), j + off))

    def cols(rows, off):
        return pl.BlockSpec((rows, tn), lambda j, i: (0, j + off))

    return pl.pallas_call(
        body, name=name, grid=(nj, Lp // tm),
        in_specs=[pl.BlockSpec((tm, tn), lambda j, i: (i, j)), pl.BlockSpec((tm, tn), lambda j, i: (i, j + nj)),
                  prev(0), prev(nj), cols(CONV_WIDTH, 0), cols(CONV_WIDTH, nj), cols(1, 0), cols(1, nj)],
        out_specs=pl.BlockSpec((tm, tn), lambda j, i: (i, j)),
        out_shape=jax.ShapeDtypeStruct((Lp, F), BF16),
        compiler_params=_cparams("parallel", "parallel"),
    )(u, u, u, u, cw, cw, cb, cb)


def _conv_glu_bwd(u, dact, cw, cb, name, *, tm, tn):
    Lp, F2 = u.shape
    F = F2 // 2
    nj = F // tn
    ni = Lp // tm
    halo_blocks = tm // SUBLANES
    n_halo = Lp // SUBLANES
    S8 = SUBLANES
    n_strips = tm // GLU_STRIP
    assert GLU_STRIP == 2 * S8 and tm % GLU_STRIP == 0

    def body(ua_ref, ug_ref, pa_ref, pg_ref, na_ref, ng_ref, da_ref, dn_ref,
             cwa_ref, cwg_ref, cba_ref, cbg_ref,
             dua_ref, dug_ref, dwa_ref, dwg_ref, dba_ref, dbg_ref,
             wacc_a, wacc_g, bacc_a, bacc_g):
        i = pl.program_id(1)
        first, last = i == 0, i == ni - 1
        sub = lax.broadcasted_iota(jnp.int32, (S8, tn), 0)
        for acc in (wacc_a, wacc_g, bacc_a, bacc_g):
            acc[...] = jnp.zeros_like(acc)

        def dconv(a_cur, a_prev, g_cur, g_prev, dact_rows):
            a, taps_a = _conv_group(a_cur, a_prev, cwa_ref, cba_ref, sub)
            gate, taps_g = _conv_group(g_cur, g_prev, cwg_ref, cbg_ref, sub)
            sg = jax.nn.sigmoid(a)
            dca = dact_rows * gate * (sg * (1.0 + a * (1.0 - sg)))
            dcg = dact_rows * (a * sg)
            return dca, dcg, taps_a, taps_g

        def du_group(dc, dc_after, cw_ref):
            return (cw_ref[2:3, :] * dc + cw_ref[1:2, :] * _shift_rows_up(dc, dc_after, 1, sub)
                    + cw_ref[0:1, :] * _shift_rows_up(dc, dc_after, 2, sub))

        def strip(r, a_prev, g_prev, dca_after, dcg_after):
            a0, a1 = ua_ref[pl.ds(r, S8), :], ua_ref[pl.ds(r + S8, S8), :]
            g0, g1 = ug_ref[pl.ds(r, S8), :], ug_ref[pl.ds(r + S8, S8), :]
            dca1, dcg1, ta1, tg1 = dconv(a1, a0, g1, g0, da_ref[pl.ds(r + S8, S8), :])
            dca0, dcg0, ta0, tg0 = dconv(a0, a_prev, g0, g_prev, da_ref[pl.ds(r, S8), :])
            dua_ref[pl.ds(r, GLU_STRIP), :] = jnp.concatenate(
                [du_group(dca0, dca1, cwa_ref), du_group(dca1, dca_after, cwa_ref)], axis=0).astype(BF16)
            dug_ref[pl.ds(r, GLU_STRIP), :] = jnp.concatenate(
                [du_group(dcg0, dcg1, cwg_ref), du_group(dcg1, dcg_after, cwg_ref)], axis=0).astype(BF16)
            for k in range(CONV_WIDTH):
                wacc_a[k] += dca0 * ta0[k] + dca1 * ta1[k]
                wacc_g[k] += dcg0 * tg0[k] + dcg1 * tg1[k]
            bacc_a[...] += dca0 + dca1
            bacc_g[...] += dcg0 + dcg1
            return dca0, dcg0

        tail = pl.ds(tm - S8, S8)
        dca_after, dcg_after, _, _ = dconv(na_ref[...], ua_ref[tail, :], ng_ref[...], ug_ref[tail, :],
                                           jnp.where(last, 0.0, dn_ref[...]))

        def step(t, carry):
            r = pl.multiple_of((n_strips - 1 - t) * GLU_STRIP, GLU_STRIP)
            before = pl.ds(pl.multiple_of(r - S8, S8), S8)
            return strip(r, ua_ref[before, :], ug_ref[before, :], *carry)

        dca_after, dcg_after = lax.fori_loop(0, n_strips - 1, step, (dca_after, dcg_after))
        strip(0, jnp.where(first, 0.0, pa_ref[...]), jnp.where(first, 0.0, pg_ref[...]), dca_after, dcg_after)

        @pl.when(first)
        def _():
            for r in (dwa_ref, dwg_ref, dba_ref, dbg_ref):
                r[...] = jnp.zeros_like(r)

        for wacc, bacc, dw_ref, db_ref in ((wacc_a, bacc_a, dwa_ref, dba_ref), (wacc_g, bacc_g, dwg_ref, dbg_ref)):
            db_ref[...] += jnp.sum(bacc[...], axis=0, keepdims=True)
            for k in range(CONV_WIDTH):
                dw_ref[k:k + 1, :] += jnp.sum(wacc[k], axis=0, keepdims=True)

    def tile(off):
        return pl.BlockSpec((tm, tn), lambda j, i: (i, j + off))

    def prev(off):
        return pl.BlockSpec((S8, tn), lambda j, i: (jnp.maximum(i * halo_blocks - 1, 0), j + off))

    def nxt(off):
        return pl.BlockSpec((S8, tn), lambda j, i: (jnp.minimum((i + 1) * halo_blocks, n_halo - 1), j + off))

    def cols(rows, off):
        return pl.BlockSpec((rows, tn), lambda j, i: (0, j + off))

    return pl.pallas_call(
        body, name=name, grid=(nj, ni),
        in_specs=[tile(0), tile(nj), prev(0), prev(nj), nxt(0), nxt(nj), tile(0), nxt(0),
                  cols(CONV_WIDTH, 0), cols(CONV_WIDTH, nj), cols(1, 0), cols(1, nj)],
        out_specs=[tile(0), tile(0), cols(CONV_WIDTH, 0), cols(CONV_WIDTH, 0), cols(1, 0), cols(1, 0)],
        out_shape=[jax.ShapeDtypeStruct((Lp, F), BF16), jax.ShapeDtypeStruct((Lp, F), BF16),
                   jax.ShapeDtypeStruct((CONV_WIDTH, F), F32), jax.ShapeDtypeStruct((CONV_WIDTH, F), F32),
                   jax.ShapeDtypeStruct((1, F), F32), jax.ShapeDtypeStruct((1, F), F32)],
        scratch_shapes=[pltpu.VMEM((CONV_WIDTH, S8, tn), F32), pltpu.VMEM((CONV_WIDTH, S8, tn), F32),
                        pltpu.VMEM((S8, tn), F32), pltpu.VMEM((S8, tn), F32)],
        compiler_params=_cparams("parallel", "arbitrary"),
    )(u, u, u, u, u, u, dact, dact, cw, cw, cb, cb)


def _logf_cumsum(pre, bf, name, *, tm):
    Lp, W = pre.shape

    def body(p_ref, b_ref, c_ref, carry_ref):
        i = pl.program_id(0)

        @pl.when(i == 0)
        def _():
            carry_ref[...] = jnp.zeros_like(carry_ref)

        x = p_ref[...] + b_ref[...]
        lf = jnp.minimum(x, 0.0) - jnp.log(1.0 + jnp.exp(-jnp.abs(x)))
        tri = (lax.broadcasted_iota(jnp.int32, (tm, tm), 0) >= lax.broadcasted_iota(jnp.int32, (tm, tm), 1)).astype(F32)
        c = jnp.dot(tri, lf, precision=lax.Precision.HIGHEST, preferred_element_type=F32) + carry_ref[...]
        c_ref[...] = c
        carry_ref[...] = c[tm - 1:tm, :]

    row = pl.BlockSpec((tm, W), lambda i: (i, 0))
    return pl.pallas_call(
        body, name=name, grid=(Lp // tm,),
        in_specs=[row, pl.BlockSpec((1, W), lambda i: (0, 0))], out_specs=row,
        out_shape=jax.ShapeDtypeStruct((Lp, W), F32),
        scratch_shapes=[pltpu.VMEM((1, W), F32)],
        compiler_params=_cparams("arbitrary"),
    )(pre, bf)


def _logf_bwd(dc_a, dc_b, pre, bf, name, *, tm):
    Lp, W = pre.shape
    ni = Lp // tm

    def body(dca_ref, dcb_ref, p_ref, b_ref, dpb_ref, db_ref, carry_ref):
        i = pl.program_id(0)

        @pl.when(i == 0)
        def _():
            carry_ref[...] = jnp.zeros_like(carry_ref)
            db_ref[...] = jnp.zeros_like(db_ref)

        triu = (lax.broadcasted_iota(jnp.int32, (tm, tm), 0) <= lax.broadcasted_iota(jnp.int32, (tm, tm), 1)).astype(F32)
        dl = jnp.dot(triu, dca_ref[...] + dcb_ref[...], precision=lax.Precision.HIGHEST,
                     preferred_element_type=F32) + carry_ref[...]
        carry_ref[...] = dl[0:1, :]
        dp = dl * jax.nn.sigmoid(-(p_ref[...] + b_ref[...]))
        dpb_ref[...] = dp.astype(BF16)
        db_ref[...] += jnp.sum(dp, axis=0, keepdims=True)

    rev = pl.BlockSpec((tm, W), lambda i: (ni - 1 - i, 0))
    vec = pl.BlockSpec((1, W), lambda i: (0, 0))
    return pl.pallas_call(
        body, name=name, grid=(ni,),
        in_specs=[rev, rev, rev, vec], out_specs=[rev, vec],
        out_shape=[jax.ShapeDtypeStruct((Lp, W), BF16), jax.ShapeDtypeStruct((1, W), F32)],
        scratch_shapes=[pltpu.VMEM((1, W), F32)],
        compiler_params=_cparams("arbitrary"),
    )(dc_a, dc_b, pre, bf)


def _attn_fwd(qb, kvb, crow4, name, *, tq):
    Lp, D = qb.shape
    H = D // HEAD_DIM
    nq = Lp // tq
    S8 = SUBLANES
    HB = LANES // HEAD_DIM
    n_scratch = 6
    lane_tiles = tq // LANES
    assert HB == 2

    def to_column(row8):
        return jnp.transpose(jnp.concatenate([row8] * (LANES // S8), axis=0))

    def body(q_ref, k_ref, v_ref, c_ref, o_ref, ob_ref, lse_ref, vt_ref, *scratch):
        i = pl.program_id(1)
        heads = [scratch[n_scratch * hb:n_scratch * (hb + 1)] for hb in range(HB)]
        first_head = lax.broadcasted_iota(jnp.int32, (tq, LANES), 1) < HEAD_DIM
        q2 = q_ref[...]
        q_of = [jnp.where(first_head, q2, jnp.zeros_like(q2)), jnp.where(first_head, jnp.zeros_like(q2), q2)]

        @pl.when(i == 0)
        def _():
            for j in range(nq):
                for hb, refs in enumerate(heads):
                    refs[5][j] = to_column(jnp.concatenate([c_ref[hb, j]] * S8, axis=0))
                vt_ref[j] = jnp.transpose(v_ref[pl.ds(j * tq, tq), :].astype(F32)).astype(BF16)

        for m_ref, l_ref, acc_ref, _, _, _ in heads:
            m_ref[...] = jnp.full_like(m_ref, NEG_INF)
            l_ref[...] = jnp.zeros_like(l_ref)
            acc_ref[...] = jnp.zeros_like(acc_ref)

        def chunk(j, masked):
            k2 = k_ref[pl.ds(pl.multiple_of(j * tq, tq), tq), :]
            for hb, (_, _, _, st_ref, _, _) in enumerate(heads):
                st_ref[...] = lax.dot_general(k2, q_of[hb], NT, preferred_element_type=F32)
            for hb, (m_ref, l_ref, acc_ref, st_ref, pt_ref, cs_ref) in enumerate(heads):
                ct = c_ref[hb, i]
                mx = jnp.full((S8, tq), NEG_INF, F32)
                for r0 in range(0, tq, ATTN_STRIP):
                    rows = pl.ds(r0, ATTN_STRIP)
                    cs = jnp.concatenate([cs_ref[j, rows, :]] * lane_tiles, axis=1)
                    st = st_ref[rows, :] + (ct - cs)
                    if masked:
                        keep = (lax.broadcasted_iota(jnp.int32, (ATTN_STRIP, tq), 1)
                                >= r0 + lax.broadcasted_iota(jnp.int32, (ATTN_STRIP, tq), 0))
                        st = jnp.where(keep, st, NEG_INF)
                    st_ref[rows, :] = st
                    for g0 in range(0, ATTN_STRIP, S8):
                        mx = jnp.maximum(mx, st[g0:g0 + S8])
                m_prev = m_ref[...]
                m_new = jnp.maximum(m_prev, jnp.max(mx, axis=0, keepdims=True))
                alpha = jnp.exp(m_prev - m_new)
                m_ref[...] = m_new
                ls = jnp.zeros((S8, tq), F32)
                for r0 in range(0, tq, ATTN_STRIP):
                    pieces = [jnp.exp(st_ref[pl.ds(r0 + g0, S8), :] - m_new) for g0 in range(0, ATTN_STRIP, S8)]
                    for piece in pieces:
                        ls = ls + piece
                    pt_ref[pl.ds(r0, ATTN_STRIP), :] = jnp.concatenate(pieces, axis=0).astype(BF16)
                l_ref[...] = alpha * l_ref[...] + ls
                pv = jnp.dot(vt_ref[j], pt_ref[...], preferred_element_type=F32)
                acc_ref[...] = jnp.concatenate([alpha] * (LANES // S8), axis=0) * acc_ref[...] + pv

        def step(j, carry):
            chunk(j, False)
            return carry

        lax.fori_loop(0, i, step, 0)
        chunk(i, True)
        outs = []
        for hb, (m_ref, l_ref, acc_ref, _, _, _) in enumerate(heads):
            l_row = jnp.sum(l_ref[...], axis=0, keepdims=True)
            outs.append(acc_ref[...] / l_row)
            lse_ref[hb, 0] = m_ref[0:1, :] + jnp.log(l_row)
        first_rows = lax.broadcasted_iota(jnp.int32, (LANES, tq), 0) < HEAD_DIM
        o2 = jnp.transpose(jnp.where(first_rows, outs[0], outs[1]))
        o_ref[...] = o2
        ob_ref[...] = o2.astype(BF16)

    per_head = [pltpu.VMEM((S8, tq), F32), pltpu.VMEM((S8, tq), F32), pltpu.VMEM((LANES, tq), F32),
                pltpu.VMEM((tq, tq), F32), pltpu.VMEM((tq, tq), BF16), pltpu.VMEM((nq, tq, LANES), F32)]
    assert len(per_head) == n_scratch
    v_blocks = D // LANES
    tile = pl.BlockSpec((tq, LANES), lambda p, i: (i, p))
    return pl.pallas_call(
        body, name=name, grid=(H // HB, nq),
        in_specs=[tile,
                  pl.BlockSpec((Lp, LANES), lambda p, i: (0, p)),
                  pl.BlockSpec((Lp, LANES), lambda p, i: (0, v_blocks + p)),
                  pl.BlockSpec((HB, nq, 1, tq), lambda p, i: (p, 0, 0, 0))],
        out_specs=[tile, tile, pl.BlockSpec((HB, 1, 1, tq), lambda p, i: (p, i, 0, 0))],
        out_shape=[jax.ShapeDtypeStruct((Lp, D), F32), jax.ShapeDtypeStruct((Lp, D), BF16),
                   jax.ShapeDtypeStruct((H, nq, 1, tq), F32)],
        scratch_shapes=[pltpu.VMEM((nq, LANES, tq), BF16)] + per_head * HB,
        compiler_params=_cparams("parallel", "arbitrary"),
    )(qb, kvb, kvb, crow4)


def _attn_delta(do, o, name, *, tm, n_heads):
    Lp, D = do.shape

    def body(do_ref, o_ref, d_ref):
        sel = (lax.broadcasted_iota(jnp.int32, (D, LANES), 0) // HEAD_DIM
               == lax.broadcasted_iota(jnp.int32, (D, LANES), 1)).astype(F32)
        d_ref[...] = jnp.dot(do_ref[...].astype(F32) * o_ref[...], sel, precision=lax.Precision.HIGHEST,
                             preferred_element_type=F32)

    row = pl.BlockSpec((tm, D), lambda i: (i, 0))
    return pl.pallas_call(
        body, name=name, grid=(Lp // tm,),
        in_specs=[row, row], out_specs=pl.BlockSpec((tm, LANES), lambda i: (i, 0)),
        out_shape=jax.ShapeDtypeStruct((Lp, LANES), F32),
        compiler_params=_cparams("parallel"),
    )(do, o)


def _attn_bwd(qb, dob, kvb, lse4, delta4, crow4, name, *, tq):
    Lp, D = qb.shape
    H = D // HEAD_DIM
    nq = Lp // tq
    HB = LANES // HEAD_DIM
    lane_tiles = tq // LANES
    n_scratch = 8
    assert HB == 2

    def body(q_ref, do_ref, k_ref, v_ref, lse_ref, dl_ref, c_ref,
             dqb_ref, dk_ref, dv_ref, dcs_ref, dcq_ref, dqt_ref, kt_ref, *scratch):
        j = pl.program_id(1)
        heads = [scratch[n_scratch * hb:n_scratch * (hb + 1)] for hb in range(HB)]
        first_head = lax.broadcasted_iota(jnp.int32, (tq, LANES), 1) < HEAD_DIM

        def split(x2):
            zero = jnp.zeros_like(x2)
            return [jnp.where(first_head, x2, zero), jnp.where(first_head, zero, x2)]

        @pl.when(j == 0)
        def _():
            dqt_ref[...] = jnp.zeros_like(dqt_ref)
            dcq_ref[...] = jnp.zeros_like(dcq_ref)

        k2 = k_ref[...]
        v2 = v_ref[...]
        kt_ref[...] = jnp.transpose(k2.astype(F32)).astype(BF16)
        first_rows = lax.broadcasted_iota(jnp.int32, (LANES, tq), 0) < HEAD_DIM
        for hb, (dk_acc, dv_acc, dc_acc, _, _, _, _, cs_ref) in enumerate(heads):
            dk_acc[...] = jnp.zeros_like(dk_acc)
            dv_acc[...] = jnp.zeros_like(dv_acc)
            dc_acc[...] = jnp.zeros_like(dc_acc)
            cs_ref[...] = jnp.transpose(jnp.broadcast_to(c_ref[hb, j], (LANES, tq)))

        def pair(i, masked):
            queries = pl.ds(pl.multiple_of(i * tq, tq), tq)
            q2 = q_ref[queries, :]
            do2 = do_ref[queries, :]
            q_of, do_of = split(q2), split(do2)
            for hb, (_, _, _, st_ref, dp_ref, _, _, _) in enumerate(heads):
                st_ref[...] = lax.dot_general(k2, q_of[hb], NT, preferred_element_type=F32)
                dp_ref[...] = lax.dot_general(v2, do_of[hb], NT, preferred_element_type=F32)
            dq_parts = []
            for hb, (dk_acc, dv_acc, dc_acc, st_ref, dp_ref, pt_ref, ds_ref, cs_ref) in enumerate(heads):
                bias_q = c_ref[hb, i] - lse_ref[hb, i]
                delta = dl_ref[hb, i]
                col_sum = jnp.zeros((SUBLANES, tq), F32)
                for r0 in range(0, tq, ATTN_STRIP):
                    rows = pl.ds(r0, ATTN_STRIP)
                    st = st_ref[rows, :] + (bias_q - jnp.concatenate([cs_ref[rows, :]] * lane_tiles, axis=1))
                    if masked:
                        keep = (lax.broadcasted_iota(jnp.int32, (ATTN_STRIP, tq), 1)
                                >= r0 + lax.broadcasted_iota(jnp.int32, (ATTN_STRIP, tq), 0))
                        st = jnp.where(keep, st, NEG_INF)
                    pt = jnp.exp(st)
                    dst = pt * (dp_ref[rows, :] - delta)
                    pt_ref[rows, :] = pt.astype(BF16)
                    ds_ref[rows, :] = dst.astype(BF16)
                    dc_acc[rows, :] += jnp.sum(dst, axis=1, keepdims=True)
                    for g0 in range(0, ATTN_STRIP, SUBLANES):
                        col_sum = col_sum + dst[g0:g0 + SUBLANES]
                dcq_ref[hb, i] += jnp.sum(col_sum, axis=0, keepdims=True)
                dv_acc[...] += jnp.dot(pt_ref[...], do2, preferred_element_type=F32)
                dk_acc[...] += jnp.dot(ds_ref[...], q2, preferred_element_type=F32)
                dq_parts.append(jnp.dot(kt_ref[...], ds_ref[...], preferred_element_type=F32))
            dqt_ref[i] += jnp.where(first_rows, dq_parts[0], dq_parts[1])

        def step(i, carry):
            pair(i, False)
            return carry

        pair(j, True)
        lax.fori_loop(j + 1, nq, step, 0)
        dk_ref[...] = jnp.where(first_head, heads[0][0][...], heads[1][0][...]).astype(BF16)
        dv_ref[...] = jnp.where(first_head, heads[0][1][...], heads[1][1][...]).astype(BF16)
        for hb in range(HB):
            dcs_ref[hb] = -heads[hb][2][...]

        @pl.when(j == nq - 1)
        def _():
            for i in range(nq):
                dqb_ref[pl.ds(i * tq, tq), :] = jnp.transpose(dqt_ref[i]).astype(BF16)

    per_head = [pltpu.VMEM((tq, LANES), F32), pltpu.VMEM((tq, LANES), F32), pltpu.VMEM((tq, 1), F32),
                pltpu.VMEM((tq, tq), F32), pltpu.VMEM((tq, tq), F32),
                pltpu.VMEM((tq, tq), BF16), pltpu.VMEM((tq, tq), BF16), pltpu.VMEM((tq, LANES), F32)]
    assert len(per_head) == n_scratch
    v_blocks = D // LANES
    whole = pl.BlockSpec((Lp, LANES), lambda p, j: (0, p))
    tile = pl.BlockSpec((tq, LANES), lambda p, j: (j, p))
    rows = pl.BlockSpec((HB, nq, 1, tq), lambda p, j: (p, 0, 0, 0))
    return pl.pallas_call(
        body, name=name, grid=(H // HB, nq),
        in_specs=[whole, whole, tile, pl.BlockSpec((tq, LANES), lambda p, j: (j, v_blocks + p)), rows, rows, rows],
        out_specs=[whole, tile, tile, pl.BlockSpec((HB, tq, 1), lambda p, j: (p, j, 0)), rows],
        out_shape=[jax.ShapeDtypeStruct((Lp, D), BF16), jax.ShapeDtypeStruct((Lp, D), BF16),
                   jax.ShapeDtypeStruct((Lp, D), BF16), jax.ShapeDtypeStruct((H, Lp, 1), F32),
                   jax.ShapeDtypeStruct((H, nq, 1, tq), F32)],
        scratch_shapes=[pltpu.VMEM((nq, LANES, tq), F32), pltpu.VMEM((LANES, tq), BF16)] + per_head * HB,
        compiler_params=_cparams("parallel", "arbitrary"),
    )(qb, dob, kvb, kvb, lse4, delta4, crow4)


def _remote(src, dst, send_sems, recv_sems, k, to):
    return pltpu.make_async_remote_copy(src_ref=src, dst_ref=dst, send_sem=send_sems.at[k], recv_sem=recv_sems.at[k],
                                        device_id=to, device_id_type=MESH)


def _place():
    x, y, c = lax.axis_index("x"), lax.axis_index("y"), lax.axis_index("c")
    other_chips = [(1 - x, y), (x, 1 - y), (1 - x, 1 - y)]
    return x, y, c, other_chips


def _all_gather_weights(wb, wf, name):
    Rb, C = wb.shape
    Rf = wf.shape[0]
    hb = Rb // 2

    def body(wb_ref, wf_ref, ob_ref, of_ref, send_sems, recv_sems):
        x, y, c, chips = _place()
        me = 2 * x + y
        sibling = (x, y, 1 - c)

        def half(chip, core):
            return ob_ref.at[chip, pl.ds(core * hb, hb), :]

        sent = []
        for j, (cx, cy) in enumerate(chips):
            sent.append(_remote(wb_ref.at[pl.ds(c * hb, hb), :], half(me, c), send_sems, recv_sems, j, (cx, cy, c)))
            sent.append(_remote(wf_ref, of_ref.at[me], send_sems, recv_sems, 3 + j, (cx, cy, c)))
        for cp in sent:
            cp.start()
        for j, (cx, cy) in enumerate(chips):
            chip = 2 * cx + cy
            _remote(half(chip, c), half(chip, c), send_sems, recv_sems, j, sibling).wait_recv()
            fwd = _remote(half(chip, c), half(chip, c), send_sems, recv_sems, 6 + j, sibling)
            fwd.start()
            sent.append(fwd)
        for j, (cx, cy) in enumerate(chips):
            chip = 2 * cx + cy
            _remote(wf_ref, of_ref.at[chip], send_sems, recv_sems, 3 + j, sibling).wait_recv()
            _remote(half(chip, 1 - c), half(chip, 1 - c), send_sems, recv_sems, 6 + j, sibling).wait_recv()
        for cp in sent:
            cp.wait_send()

    any_spec = pl.BlockSpec(memory_space=pl.ANY)
    return pl.pallas_call(
        body, name=name,
        in_specs=[any_spec, any_spec], out_specs=[any_spec, any_spec],
        out_shape=[jax.ShapeDtypeStruct((N_CHIPS, Rb, C), BF16), jax.ShapeDtypeStruct((N_CHIPS, Rf, C), F32)],
        scratch_shapes=[pltpu.SemaphoreType.DMA((9,)), pltpu.SemaphoreType.DMA((9,))],
    )(wb, wf)


def _half_of(ref, order, half):
    return ref.at[pl.ds(0, N_CHIPS), half] if order == "CH" else ref.at[half]


def _halves_to_sibling(grads, orders, name):
    n = len(grads)

    def body(*refs):
        g_refs, a_refs, (send_sems, recv_sems) = refs[:n], refs[n:2 * n], refs[2 * n:]
        x, y, c, _ = _place()
        copies = [_remote(_half_of(g, o, 1 - c), a, send_sems, recv_sems, k, (x, y, 1 - c))
                  for k, (g, a, o) in enumerate(zip(g_refs, a_refs, orders))]
        for cp in copies:
            cp.start()
        for cp in copies:
            cp.wait()

    any_spec = pl.BlockSpec(memory_space=pl.ANY)
    shapes = [g.shape[2:] for g in grads]
    return pl.pallas_call(
        body, name=name, in_specs=[any_spec] * n, out_specs=[any_spec] * n,
        out_shape=[jax.ShapeDtypeStruct((N_CHIPS,) + s, F32) for s in shapes],
        scratch_shapes=[pltpu.SemaphoreType.DMA((n,)), pltpu.SemaphoreType.DMA((n,))],
    )(*grads)


def _chip_partial(g, a, core, order, wire, name, *, tr):
    _, R, C = a.shape
    narrow = wire != F32

    def body(core_ref, g_ref, a_ref, *outs):
        p = g_ref[0, 0] + a_ref[0]
        outs[0][0] = p
        if narrow:
            outs[1][0] = p.astype(wire)

    if order == "CH":
        g_spec = pl.BlockSpec((1, 1, tr, C), lambda s, i, core_ref: (s, core_ref[0], i, 0))
    else:
        g_spec = pl.BlockSpec((1, 1, tr, C), lambda s, i, core_ref: (core_ref[0], s, i, 0))
    blk = pl.BlockSpec((1, tr, C), lambda s, i, core_ref: (s, i, 0))
    grid_spec = pltpu.PrefetchScalarGridSpec(
        num_scalar_prefetch=1, grid=(N_CHIPS, R // tr), in_specs=[g_spec, blk],
        out_specs=[blk, blk] if narrow else [blk])
    out_shape = [jax.ShapeDtypeStruct((N_CHIPS, R, C), F32)] + ([jax.ShapeDtypeStruct((N_CHIPS, R, C), wire)] if narrow else [])
    outs = pl.pallas_call(body, name=name, grid_spec=grid_spec, out_shape=out_shape,
                          compiler_params=_cparams("parallel", "parallel"))(core, g, a)
    return outs[0], outs[-1]


def _chip_exchange(parts, rep, name):
    n = len(parts)
    rr, C = rep.shape

    def body(*refs):
        p_refs, rep_ref = refs[:n], refs[n]
        land_refs, reps_ref = refs[n + 1:2 * n + 1], refs[2 * n + 1]
        send_sems, recv_sems, local_sem = refs[2 * n + 2:]
        x, y, c, chips = _place()
        me = 4 * x + 2 * y + c
        own = pltpu.make_async_copy(rep_ref, reps_ref.at[me], local_sem.at[0])
        own.start()
        sent = []
        for k, (p, land) in enumerate(zip(p_refs, land_refs)):
            for j, (cx, cy) in enumerate(chips):
                sent.append(_remote(p.at[2 * cx + cy], land.at[j], send_sems, recv_sems, 3 * k + j, (cx, cy, c)))
        for r in range(1, N_DEV):
            fx, fy, fc = (r >> 2) & 1, (r >> 1) & 1, r & 1
            sent.append(_remote(rep_ref, reps_ref.at[me], send_sems, recv_sems, 3 * n - 1 + r, (x ^ fx, y ^ fy, c ^ fc)))
        for cp in sent:
            cp.start()
        for k, (p, land) in enumerate(zip(p_refs, land_refs)):
            for j in range(3):
                _remote(p.at[0], land.at[j], send_sems, recv_sems, 3 * k + j, (x, y, c)).wait_recv()
        for r in range(1, N_DEV):
            fx, fy, fc = (r >> 2) & 1, (r >> 1) & 1, r & 1
            frm = 4 * (x ^ fx) + 2 * (y ^ fy) + (c ^ fc)
            _remote(rep_ref, reps_ref.at[frm], send_sems, recv_sems, 3 * n - 1 + r, (x, y, c)).wait_recv()
        for cp in sent:
            cp.wait_send()
        own.wait()

    any_spec = pl.BlockSpec(memory_space=pl.ANY)
    n_sems = 3 * n + N_DEV - 1
    return pl.pallas_call(
        body, name=name, in_specs=[any_spec] * (n + 1), out_specs=[any_spec] * (n + 1),
        out_shape=[jax.ShapeDtypeStruct((3,) + p.shape[1:], p.dtype) for p in parts]
        + [jax.ShapeDtypeStruct((N_DEV, rr, C), F32)],
        scratch_shapes=[pltpu.SemaphoreType.DMA((n_sems,)), pltpu.SemaphoreType.DMA((n_sems,)),
                        pltpu.SemaphoreType.DMA((1,))],
    )(*parts, rep)


def _adamw_math(w, g, m, v):
    m = ADAM_B1 * m + (1.0 - ADAM_B1) * g
    v = ADAM_B2 * v + (1.0 - ADAM_B2) * (g * g)
    m_hat = m / (1.0 - ADAM_B1 ** ADAM_STEP)
    v_hat = v / (1.0 - ADAM_B2 ** ADAM_STEP)
    delta = -ADAM_LR * (m_hat / (jnp.sqrt(v_hat) + ADAM_EPS) + ADAM_WD * w)
    return delta, m, v


def _adamw_owned(part, landed, w, m, v, place, name, *, tr):
    _, R, C = part.shape

    def body(place_ref, own_ref, land_ref, w_ref, m_ref, v_ref, g_ref, d_ref, mo_ref, vo_ref):
        g = own_ref[0]
        for s in range(3):
            g = g + land_ref[s].astype(F32)
        delta, m_new, v_new = _adamw_math(w_ref[0], g, m_ref[0], v_ref[0])
        g_ref[0] = g
        d_ref[0] = delta
        mo_ref[0] = m_new
        vo_ref[0] = v_new

    half = pl.BlockSpec((1, tr, C), lambda i, place_ref: (place_ref[0], i, 0))
    grid_spec = pltpu.PrefetchScalarGridSpec(
        num_scalar_prefetch=1, grid=(R // tr,),
        in_specs=[pl.BlockSpec((1, tr, C), lambda i, place_ref: (place_ref[1], i, 0)),
                  pl.BlockSpec((3, tr, C), lambda i, place_ref: (0, i, 0)), half, half, half],
        out_specs=[half] * 4)
    return pl.pallas_call(
        body, name=name, grid_spec=grid_spec, out_shape=[jax.ShapeDtypeStruct((2, R, C), F32)] * 4,
        compiler_params=_cparams("parallel"),
    )(place, part, landed, w, m, v)


def _join_halves(bufs, name):
    n = len(bufs)

    def body(*refs):
        out_refs, (send_sems, recv_sems) = refs[n:2 * n], refs[2 * n:]
        x, y, c, _ = _place()
        copies = [_remote(o.at[c], o.at[c], send_sems, recv_sems, k, (x, y, 1 - c)) for k, o in enumerate(out_refs)]
        for cp in copies:
            cp.start()
        for k, o in enumerate(out_refs):
            _remote(o.at[c], o.at[1 - c], send_sems, recv_sems, k, (x, y, 1 - c)).wait_recv()
        for cp in copies:
            cp.wait_send()

    any_spec = pl.BlockSpec(memory_space=pl.ANY)
    return pl.pallas_call(
        body, name=name, in_specs=[any_spec] * n, out_specs=[any_spec] * n,
        out_shape=[jax.ShapeDtypeStruct(b.shape, b.dtype) for b in bufs],
        input_output_aliases={k: k for k in range(n)},
        scratch_shapes=[pltpu.SemaphoreType.DMA((n,)), pltpu.SemaphoreType.DMA((n,))],
    )(*bufs)


def _sum_adamw(own, landed, w, m, v, name, *, tr):
    n = landed.shape[0]
    hr, C = own.shape

    def body(own_ref, land_ref, w_ref, m_ref, v_ref, o_ref):
        g = own_ref[...]
        for s in range(n):
            g = g + land_ref[s]
        delta, m_new, v_new = _adamw_math(w_ref[...], g, m_ref[...], v_ref[...])
        o_ref[0] = g
        o_ref[1] = delta
        o_ref[2] = m_new
        o_ref[3] = v_new

    blk = pl.BlockSpec((tr, C), lambda i: (i, 0))
    return pl.pallas_call(
        body, name=name, grid=(hr // tr,),
        in_specs=[blk, pl.BlockSpec((n, tr, C), lambda i: (0, i, 0)), blk, blk, blk],
        out_specs=pl.BlockSpec((4, tr, C), lambda i: (0, i, 0)),
        out_shape=jax.ShapeDtypeStruct((4, hr, C), F32), compiler_params=_cparams("parallel"),
    )(own, landed, w, m, v)


def _rows_of(shape):
    n = 1
    for d in shape:
        n *= d
    return -(-n // PACK_COLS)


def _pack(arrays, total_rows, dtype):
    parts, used = [], 0
    for a in arrays:
        flat = a.reshape(-1).astype(dtype)
        fill = _rows_of(a.shape) * PACK_COLS - flat.shape[0]
        parts += [flat] + ([jnp.zeros((fill,), dtype)] if fill else [])
        used += _rows_of(a.shape)
    if total_rows > used:
        parts.append(jnp.zeros(((total_rows - used) * PACK_COLS,), dtype))
    return jnp.concatenate(parts).reshape(total_rows, PACK_COLS)


def _unpack(buf, shapes):
    lead = buf.shape[:-2]
    out, r = [], 0
    for shp in shapes:
        n = 1
        for d in shp:
            n *= d
        rows = _rows_of(shp)
        piece = buf[..., r:r + rows, :].reshape(lead + (rows * PACK_COLS,))[..., :n]
        out.append(piece.reshape(lead + tuple(shp)))
        r += rows
    return out


def _join_shards(stacked, axis):
    return jnp.concatenate([stacked[s] for s in range(N_CHIPS)], axis=axis)


def _shard_of(full, axis, chip):
    width = full.shape[axis] // N_CHIPS
    return lax.slice_in_dim(full, chip * width, (chip + 1) * width, axis=axis)


def _local_step(h0, tgt, W, *, seq, tm):
    Lp, D = h0.shape
    H = D // HEAD_DIM
    F2 = W["ffn_w_in"].shape[-1]
    F = F2 // 2
    te = tm // 2
    nq = Lp // tm
    cap = 1408
    tD, tF, tF2 = _pick(D, cap), _pick(F, cap), _pick(F2, cap)
    t2D = _pick(2 * D, cap)
    t2Dc, tF2c = _pick(2 * D // N_CHIPS, cap), _pick(F2 // N_CHIPS, cap)
    tcn = _pick(F, cap)

    def vec(a):
        return a.reshape(1, -1)

    ln_g, ln_b = W["ln_g"], W["ln_b"]
    wf_pad = jnp.pad(W["w_f"], ((0, 0), (0, LANES - H)))
    bf_pad = jnp.pad(W["b_f"], (0, LANES - H)).reshape(1, LANES)

    def ffn_fwd(h, hb, l, tag):
        u = _mm(hb, W["ffn_w_in"][l], "nn", F32, f"ffn{tag}_up", tm=tm, tn=tF2, tk=tD)
        act = _conv_glu_fwd(u, W["ffn_conv_w"][l], vec(W["ffn_conv_b"][l]), f"ffn{tag}_glu", tm=te, tn=tcn)
        normed = _mm_ln(act, W["ffn_w_out"][l], "nn", f"ffn{tag}_down_ln", tm=tm, tk=tF, forward=True,
                        rows=[h], vecs=[vec(ln_g[l, 1]), vec(ln_b[l, 1])])
        return u, act, normed

    def ffn_bwd(dz, dzb, hb, u, act, l, tag, dw_in_acc, dw_out_acc, xh_in, rs_in, g_in):
        dact = _mm(dzb, W["ffn_w_out"][l], "nt", F32, f"ffn{tag}_dact", tm=tm, tn=tF, tk=tD)
        dw_out = _mm(act, dzb, "tn", F32, f"ffn{tag}_dwout", tm=tF, tn=tD, tk=tm, layer=l, into=dw_out_acc)
        dua, dug, dwa, dwg, dba, dbg = _conv_glu_bwd(u, dact, W["ffn_conv_w"][l], vec(W["ffn_conv_b"][l]),
                                                     f"ffn{tag}_dglu", tm=te, tn=tcn)
        du = jnp.concatenate([dua, dug], axis=1)
        dcw = jnp.concatenate([dwa, dwg], axis=1)
        dcb = jnp.concatenate([dba, dbg], axis=1)
        prev = _mm_ln(du, W["ffn_w_in"][l], "nt", f"ffn{tag}_dh_ln", tm=tm, tk=tF2, forward=False,
                      rows=[dz, xh_in, rs_in], vecs=[g_in])
        dw_in = _mm(hb, du, "tn", F32, f"ffn{tag}_dwin", tm=tD, tn=tF2c, tk=tm, chips=True, layer=l, into=dw_in_acc)
        return prev, dw_in, dw_out, dcw, dcb[0]

    diffb, mixpre, h1, h1b, xh1, rs1 = _pool_ln_fwd(h0, W["pool_w"][0], W["pool_scale"], vec(ln_g[0, 0]),
                                                    vec(ln_b[0, 0]), "pool_ln_fwd", tm=te)
    u0, act0, (h2, h2b, xh2, rs2) = ffn_fwd(h1, h1b, 0, "0")

    kvb = _mm(h2b, W["w_kv"], "nn", BF16, "kv_proj", tm=tm, tn=t2D, tk=tD)
    qb = _mm(h2b, W["w_q"][0], "nn", BF16, "q_proj", tm=tm, tn=tD, tk=tD, scale=HEAD_DIM ** -0.5)
    pre = _mm(h2b, wf_pad, "nn", F32, "f_proj", tm=tm, tn=LANES, tk=tD)
    c = _logf_cumsum(pre, bf_pad, "logf_cumsum", tm=tm)

    crow4 = c[:, :H].T.reshape(H, nq, 1, tm)
    o_tok, ob, lse4 = _attn_fwd(qb, kvb, crow4, "attn_fwd", tq=tm)
    h3, h3b, xh3, rs3 = _mm_ln(ob, W["w_o"][0], "nn", "o_proj_ln", tm=tm, tk=tD, forward=True,
                               rows=[h2], vecs=[vec(ln_g[1, 0]), vec(ln_b[1, 0])])
    u1, act1, (h4, _, xh4, rs4) = ffn_fwd(h3, h3b, 1, "1")
    dy, loss = _loss_head(h4, tgt, "loss_head", tm=te, row_lo=N_META, row_hi=N_META + seq)

    dz4, dz4b, dg11, db11 = _ln_bwd([dy], [1.0], xh4, rs4, vec(ln_g[1, 1]), "ln11_bwd", tm=te)
    (dz3, dz3b, dg10, db10), dw_in, dw_out, dcw1, dcb1 = ffn_bwd(dz4, dz4b, h3b, u1, act1, 1, "1", None, None,
                                                                 xh3, rs3, vec(ln_g[1, 0]))

    dob = _mm(dz3b, W["w_o"][0], "nt", BF16, "o_proj_dx", tm=tm, tn=tD, tk=tD)
    dw_o = _mm(ob, dz3b, "tn", F32, "o_proj_dw", tm=tD, tn=tD, tk=tm)
    delta = _attn_delta(dob, o_tok, "attn_delta", tm=te, n_heads=H)
    dqb, dkb, dvb, dcs, dcq = _attn_bwd(qb, dob, kvb, lse4, delta[:, :H].T.reshape(H, nq, 1, tm), crow4,
                                        "attn_bwd", tq=tm)
    dc_keys = jnp.pad(dcs.reshape(H, Lp).T, ((0, 0), (0, LANES - H)))
    dc_queries = jnp.pad(dcq.reshape(H, Lp).T, ((0, 0), (0, LANES - H)))
    dpreb, dbf = _logf_bwd(dc_keys, dc_queries, pre, bf_pad, "logf_bwd", tm=tm)

    qs = HEAD_DIM ** -0.5
    dw_q = _mm(h2b, dqb, "tn", F32, "q_proj_dw", tm=tD, tn=tD, tk=tm, scale=qs)
    dw_kv = _mm(h2b, dkb, "tn", F32, "k_proj_dw", tm=tD, tn=t2Dc, tk=tm, chips=(0, N_CHIPS // 2))
    dw_kv = _mm(h2b, dvb, "tn", F32, "v_proj_dw", tm=tD, tn=t2Dc, tk=tm, chips=(N_CHIPS // 2, N_CHIPS // 2), into=dw_kv)
    dw_f = _mm(h2b, dpreb, "tn", F32, "f_proj_dw", tm=tD, tn=LANES, tk=tm)[:, :H]
    dh2 = _mm(dqb, W["w_q"][0], "nt", F32, "q_proj_dx", tm=tm, tn=tD, tk=tD, scale=qs)
    dh2 = _mm(dkb, W["w_kv"][:, :D], "nt", F32, "k_proj_dx", tm=tm, tn=tD, tk=tD, add=dh2)
    dh2 = _mm(dvb, W["w_kv"][:, D:], "nt", F32, "v_proj_dx", tm=tm, tn=tD, tk=tD, add=dh2)
    dz2, dz2b, dg01, db01 = _mm_ln(dpreb, wf_pad, "nt", "f_proj_dx_ln", tm=tm, tk=LANES, forward=False,
                                   rows=[dz3, xh2, rs2], vecs=[vec(ln_g[0, 1])], add=dh2)

    (dz1, _, dg00, db00), dw_in, dw_out, dcw0, dcb0 = ffn_bwd(dz2, dz2b, h1b, u0, act0, 0, "0", dw_in, dw_out,
                                                              xh1, rs1, vec(ln_g[0, 0]))
    dh0, dmb, dscale = _pool_bwd(dz1, mixpre, W["pool_w"][0], W["pool_scale"], "pool_bwd", tm=te)
    dw_pool = _pool_dw(diffb, dmb, "pool_dw", tk=tm)

    grads = {
        "meta": dh0[:N_META],
        "pool_w": dw_pool[None],
        "pool_scale": dscale,
        "w_kv": dw_kv,
        "w_f": dw_f,
        "b_f": dbf[0, :H],
        "w_q": dw_q[None],
        "w_o": dw_o[None],
        "ffn_w_in": dw_in,
        "ffn_conv_w": jnp.stack([dcw0, dcw1]),
        "ffn_conv_b": jnp.stack([dcb0, dcb1]),
        "ffn_w_out": dw_out,
        "ln_g": jnp.stack([jnp.stack([dg00[0], dg01[0]]), jnp.stack([dg10[0], dg11[0]])]),
        "ln_b": jnp.stack([jnp.stack([db00[0], db01[0]]), jnp.stack([db10[0], db11[0]])]),
    }
    return loss, dh0, grads


def _row_block(rows, cols):
    best = SUBLANES
    for t in range(SUBLANES, rows + 1, SUBLANES):
        if rows % t == 0 and t * cols * 4 <= ELEMENTWISE_BLOCK_BYTES:
            best = t
    return best


def _row_tile(length):
    return 640 if length >= 4096 else 128


def kernel(x, meta, pool_w, pool_scale, w_kv, w_f, b_f, w_q, w_o, ffn_w_in, ffn_conv_w, ffn_conv_b, ffn_w_out, ln_g, ln_b, loss_target, m_meta, m_pool_w, m_pool_scale, m_w_kv, m_w_f, m_b_f, m_w_q, m_w_o, m_ffn_w_in, m_ffn_conv_w, m_ffn_conv_b, m_ffn_w_out, m_ln_g, m_ln_b, v_meta, v_pool_w, v_pool_scale, v_w_kv, v_w_f, v_b_f, v_w_q, v_w_o, v_ffn_w_in, v_ffn_conv_w, v_ffn_conv_b, v_ffn_w_out, v_ln_g, v_ln_b):
    weights = dict(meta=meta, pool_w=pool_w, pool_scale=pool_scale, w_kv=w_kv, w_f=w_f, b_f=b_f, w_q=w_q, w_o=w_o,
                   ffn_w_in=ffn_w_in, ffn_conv_w=ffn_conv_w, ffn_conv_b=ffn_conv_b, ffn_w_out=ffn_w_out,
                   ln_g=ln_g, ln_b=ln_b)
    mom1 = dict(meta=m_meta, pool_w=m_pool_w, pool_scale=m_pool_scale, w_kv=m_w_kv, w_f=m_w_f, b_f=m_b_f, w_q=m_w_q,
                w_o=m_w_o, ffn_w_in=m_ffn_w_in, ffn_conv_w=m_ffn_conv_w, ffn_conv_b=m_ffn_conv_b,
                ffn_w_out=m_ffn_w_out, ln_g=m_ln_g, ln_b=m_ln_b)
    mom2 = dict(meta=v_meta, pool_w=v_pool_w, pool_scale=v_pool_scale, w_kv=v_w_kv, w_f=v_w_f, b_f=v_b_f, w_q=v_w_q,
                w_o=v_w_o, ffn_w_in=v_ffn_w_in, ffn_conv_w=v_ffn_conv_w, ffn_conv_b=v_ffn_conv_b,
                ffn_w_out=v_ffn_w_out, ln_g=v_ln_g, ln_b=v_ln_b)
    _, seq, D = x.shape
    L = N_META + seq
    tm = _row_tile(L)
    Lp = _round_up(L, tm)
    c_idx = lax.axis_index("c")
    chip = 2 * lax.axis_index("x") + lax.axis_index("y")

    shard_shapes = {n: weights[n].shape for n in SHARDED}
    rows_b = _round_up(sum(_rows_of(shard_shapes[n]) for n in MATMUL_WEIGHTS), 32)
    rows_f = _round_up(sum(_rows_of(shard_shapes[n]) for n in VECTOR_WEIGHTS), SUBLANES)
    wb = _pack([weights[n] for n in MATMUL_WEIGHTS], rows_b, BF16)
    wf = _pack([weights[n] for n in VECTOR_WEIGHTS], rows_f, F32)
    gb, gf = _all_gather_weights(wb, wf, "weights_all_gather")
    gb = lax.dynamic_update_index_in_dim(gb, wb, chip, axis=0)
    gf = lax.dynamic_update_index_in_dim(gf, wf, chip, axis=0)
    full = {}
    for names, buf in ((MATMUL_WEIGHTS, gb), (VECTOR_WEIGHTS, gf)):
        for n, stacked in zip(names, _unpack(buf, [shard_shapes[n] for n in names])):
            full[n] = _join_shards(stacked, SHARD_AXIS[n])
    full["b_f"] = b_f
    full["ffn_conv_b"] = ffn_conv_b

    pad = jnp.zeros((Lp - L, D), F32)
    h0 = jnp.concatenate([full["meta"], x[0], pad], axis=0)
    tgt = jnp.concatenate([jnp.zeros((N_META, D), F32), loss_target[0], pad], axis=0)
    loss, dh0, grads = _local_step(h0, tgt, full, seq=seq, tm=tm)
    loss = lax.psum(loss[0, 0], AXES)
    grad_x = dh0[N_META:L][None]

    core = c_idx.astype(jnp.int32).reshape(1)
    place = jnp.stack([c_idx, chip]).astype(jnp.int32)
    small_shapes = [shard_shapes[n] for n in SMALL_SHARDED]
    rows_s = _round_up(sum(_rows_of(s) for s in small_shapes), 2 * LANES)

    def packed_small(d):
        return _pack([d[n] for n in SMALL_SHARDED], rows_s, F32).reshape(2, rows_s // 2, PACK_COLS)

    names, orders, wires, g_views, wmv = [], [], [], [], []
    for n, order in BIG_SHARDED:
        shp = shard_shapes[n]
        C = shp[-1]
        R = weights[n].size // C // 2
        lead = (N_CHIPS, 2) if order == "CH" else (2, N_CHIPS)
        names.append(n)
        orders.append(order)
        wires.append(BF16)
        g_views.append(grads[n].reshape(lead + (R, C)))
        wmv.append([d[n].reshape(2, R, C) for d in (weights, mom1, mom2)])
    names.append("small")
    orders.append("CH")
    wires.append(F32)
    g_views.append(jnp.stack([_pack([_shard_of(grads[n], SHARD_AXIS[n], s) for n in SMALL_SHARDED], rows_s, F32)
                              for s in range(N_CHIPS)]).reshape(N_CHIPS, 2, rows_s // 2, PACK_COLS))
    wmv.append([packed_small(d) for d in (weights, mom1, mom2)])

    from_sibling = _halves_to_sibling(g_views, orders, "grads_to_sibling")
    parts, on_wire = [], []
    for n, order, wire, g, a in zip(names, orders, wires, g_views, from_sibling):
        p, pw = _chip_partial(g, a, core, order, wire, f"chip_sum_{n}", tr=_row_block(a.shape[1], a.shape[2]))
        parts.append(p)
        on_wire.append(pw)

    rep_shapes = [weights[n].shape for n in REPLICATED]
    rows_r = _round_up(sum(_rows_of(s) for s in rep_shapes), SUBLANES)
    rep = _pack([grads[n] for n in REPLICATED], rows_r, F32)
    *landed, reps = _chip_exchange(on_wire, rep, "grads_chip_exchange")

    halves = []
    for n, p, b, (w_, m_, v_) in zip(names, parts, landed, wmv):
        halves += _adamw_owned(p, b, w_, m_, v_, place, f"adamw_{n}", tr=_row_block(p.shape[1], p.shape[2]))
    joined = _join_halves(halves, "results_to_sibling")
    out = {}
    for k, n in enumerate(names[:-1]):
        out[n] = [a.reshape(shard_shapes[n]) for a in joined[4 * k:4 * k + 4]]
    small_out = [_unpack(a.reshape(rows_s, PACK_COLS), small_shapes) for a in joined[-4:]]
    for k, n in enumerate(SMALL_SHARDED):
        out[n] = [small_out[kind][k] for kind in range(4)]

    def packr(d):
        return _pack([d[n] for n in REPLICATED], rows_r, F32)

    res_r = _sum_adamw(reps[0], reps[1:], packr(weights), packr(mom1), packr(mom2), "adamw_replicated", tr=rows_r)
    rep_out = _unpack(res_r, rep_shapes)

    out.update({n: a for n, a in zip(REPLICATED, rep_out)})
    result = [loss, grad_x]
    for k in range(4):
        result += [out[n][k] for n in WEIGHT_ORDER]
    return tuple(result)
```

```python
import functools

import jax
import jax.numpy as jnp
from jax import lax
from jax.experimental import pallas as pl
from jax.experimental.pallas import tpu as pltpu

N_META = 16
POOL_WINDOWS = (2, 4, 8, 16)
MAX_WINDOW = max(POOL_WINDOWS)
N_GROUPS = len(POOL_WINDOWS)
HEAD_DIM = 64
DEPTH = 2
CONV_WIDTH = 3
ALPHA = (2.0 * DEPTH) ** 0.25
LN_EPS = 1e-5
NEG_INF = -1e30
ADAM_LR = 0.001
ADAM_B1 = 0.9
ADAM_B2 = 0.999
ADAM_EPS = 1e-08
ADAM_WD = 0.01
ADAM_STEP = 10

F32 = jnp.float32
BF16 = jnp.bfloat16
ATTN_FWD_PAIRS = 2
BIAS_SLOTS = 6
ATTN_STRIP = 32
GLU_STRIP = 16
LANES = 128
SUBLANES = 8
PACK_COLS = 1024
VMEM_LIMIT = 56 * 1024 * 1024
AXES = ("x", "y", "c")
MESH = pl.DeviceIdType.MESH

NN = (((1,), (0,)), ((), ()))
NT = (((1,), (1,)), ((), ()))
TN = (((0,), (0,)), ((), ()))

SHARD_AXIS = {"meta": 1, "pool_w": 2, "pool_scale": 1, "w_kv": 1, "w_f": 0, "w_q": 1, "w_o": 1,
              "ffn_w_in": 2, "ffn_conv_w": 2, "ffn_w_out": 1, "ln_g": 2, "ln_b": 2}
SHARDED = ("meta", "pool_w", "pool_scale", "w_kv", "w_f", "w_q", "w_o", "ffn_w_in", "ffn_conv_w",
           "ffn_w_out", "ln_g", "ln_b")
REPLICATED = ("b_f", "ffn_conv_b")
MATMUL_WEIGHTS = ("pool_w", "w_kv", "w_f", "w_q", "w_o", "ffn_w_in", "ffn_w_out")
VECTOR_WEIGHTS = ("meta", "pool_scale", "ffn_conv_w", "ln_g", "ln_b")
WEIGHT_ORDER = ("meta", "pool_w", "pool_scale", "w_kv", "w_f", "b_f", "w_q", "w_o", "ffn_w_in",
                "ffn_conv_w", "ffn_conv_b", "ffn_w_out", "ln_g", "ln_b")
BIG_SHARDED = (("w_kv", "CH"), ("w_q", "CH"), ("w_o", "CH"), ("ffn_w_in", "HC"), ("ffn_w_out", "HC"))
SMALL_SHARDED = ("meta", "pool_w", "pool_scale", "w_f", "ffn_conv_w", "ln_g", "ln_b")
ELEMENTWISE_BLOCK_BYTES = 3 * 512 * 1024
N_CHIPS = 4
N_DEV = 8


def _cparams(*sem):
    return pltpu.CompilerParams(dimension_semantics=sem, vmem_limit_bytes=VMEM_LIMIT)


def _round_up(n, m):
    return (n + m - 1) // m * m


def _pick(n, cap):
    if n <= cap:
        return n
    best = 0
    for t in range(LANES, cap + 1, LANES):
        if n % t == 0:
            best = t
    assert best, (n, cap)
    return best


def _mm(a, b, mode, out_dtype, name, *, tm, tn, tk, scale=None, add=None, chips=False, layer=None, into=None):
    if mode == "nn":
        (M, K), N = a.shape, b.shape[1]
    elif mode == "nt":
        (M, K), N = a.shape, b.shape[0]
    else:
        (K, M), N = a.shape, b.shape[1]
    assert M % tm == 0 and N % tn == 0 and K % tk == 0, (name, M, N, K, tm, tn, tk)
    nk = K // tk
    dn = {"nn": NN, "nt": NT, "tn": TN}[mode]
    has_add = add is not None
    has_into = into is not None
    assert not (has_add and (chips or layer is not None))

    def body(*refs):
        a_ref, b_ref = refs[0], refs[1]
        add_ref = refs[2] if has_add else None
        o_ref = refs[2 + has_add + has_into]
        acc_ref = refs[-1] if nk > 1 else None
        k = pl.program_id(2)
        part = lax.dot_general(a_ref[...], b_ref[...], dn, preferred_element_type=F32)

        def finish(r):
            if scale is not None:
                r = r * scale
            if has_add:
                r = r + add_ref[...]
            o_ref[...] = r.astype(out_dtype).reshape(o_ref.shape)

        if nk == 1:
            finish(part)
        else:
            @pl.when(k == 0)
            def _():
                acc_ref[...] = part

            @pl.when(k > 0)
            def _():
                acc_ref[...] += part

            @pl.when(k == nk - 1)
            def _():
                finish(acc_ref[...])

    if mode == "nn":
        a_spec = pl.BlockSpec((tm, tk), lambda j, i, k: (i, k))
        b_spec = pl.BlockSpec((tk, tn), lambda j, i, k: (k, j))
    elif mode == "nt":
        a_spec = pl.BlockSpec((tm, tk), lambda j, i, k: (i, k))
        b_spec = pl.BlockSpec((tn, tk), lambda j, i, k: (j, k))
    else:
        a_spec = pl.BlockSpec((tk, tm), lambda j, i, k: (k, i))
        b_spec = pl.BlockSpec((tk, tn), lambda j, i, k: (k, j))
    out_dims, blk = (M, N), (tm, tn)
    if chips:
        base, count = (0, N_CHIPS) if chips is True else chips
        per_chip = N // count // tn
        assert per_chip * tn * count == N, (name, N, tn)
        out_dims, blk = (N_CHIPS, M, N // count), (1, tm, tn)
        where = lambda j, i: (base + j // per_chip, i, j % per_chip)
    else:
        where = lambda j, i: (i, j)
    if layer is not None:
        out_dims, blk = (DEPTH,) + out_dims, (1,) + blk
        o_spec = pl.BlockSpec(blk, lambda j, i, k: (layer,) + where(j, i))
    else:
        o_spec = pl.BlockSpec(blk, lambda j, i, k: where(j, i))
    in_specs = [a_spec, b_spec] + ([o_spec] if has_add else []) + ([pl.BlockSpec(memory_space=pl.ANY)] if has_into else [])
    args = (a, b) + ((add,) if has_add else ()) + ((into,) if has_into else ())
    return pl.pallas_call(
        body, name=name, grid=(N // tn, M // tm, nk),
        in_specs=in_specs, out_specs=o_spec,
        out_shape=jax.ShapeDtypeStruct(out_dims, out_dtype),
        input_output_aliases={len(args) - 1: 0} if has_into else {},
        scratch_shapes=[pltpu.VMEM((tm, tn), F32)] if nk > 1 else [],
        compiler_params=_cparams("parallel", "parallel", "arbitrary"),
    )(*args)


def _ln_math(z, g, b):
    mu = jnp.mean(z, axis=-1, keepdims=True)
    zc = z - mu
    var = jnp.mean(zc * zc, axis=-1, keepdims=True)
    rstd = lax.rsqrt(var + LN_EPS)
    xh = zc * rstd
    return xh * g + b, xh, rstd


def _mm_ln(a, b, mode, name, *, tm, tk, forward, rows, vecs, scale=None, add=None):
    assert mode in ("nn", "nt")
    M, K = a.shape
    N = b.shape[1] if mode == "nn" else b.shape[0]
    assert M % tm == 0 and K % tk == 0, (name, M, K, tm, tk)
    nk = K // tk
    ni = M // tm
    dn = {"nn": NN, "nt": NT}[mode]
    has_add = add is not None
    n_in = 2 + has_add + len(rows) + len(vecs)

    def body(*refs):
        a_ref, b_ref = refs[0], refs[1]
        add_ref = refs[2] if has_add else None
        row_refs = refs[2 + has_add:2 + has_add + len(rows)]
        vec_refs = refs[2 + has_add + len(rows):n_in]
        outs = refs[n_in:n_in + 4]
        acc_ref = refs[-1] if nk > 1 else None
        i, k = pl.program_id(0), pl.program_id(1)
        part = lax.dot_general(a_ref[...], b_ref[...], dn, preferred_element_type=F32)

        def finish(y):
            if scale is not None:
                y = y * scale
            if has_add:
                y = y + add_ref[...]
            if forward:
                h, xh, rstd = _ln_math(ALPHA * row_refs[0][...] + y, vec_refs[0][...], vec_refs[1][...])
                outs[0][...] = h
                outs[1][...] = h.astype(BF16)
                outs[2][...] = xh
                outs[3][...] = rstd
            else:
                dy = ALPHA * row_refs[0][...] + y
                x = row_refs[1][...]
                dxh = dy * vec_refs[0][...]
                m1 = jnp.mean(dxh, axis=-1, keepdims=True)
                m2 = jnp.mean(dxh * x, axis=-1, keepdims=True)
                dz = row_refs[2][...] * (dxh - m1 - x * m2)
                outs[0][...] = dz
                outs[1][...] = dz.astype(BF16)

                @pl.when(i == 0)
                def _():
                    outs[2][...] = jnp.zeros_like(outs[2])
                    outs[3][...] = jnp.zeros_like(outs[3])

                outs[2][...] += jnp.sum(dy * x, axis=0, keepdims=True)
                outs[3][...] += jnp.sum(dy, axis=0, keepdims=True)

        if nk == 1:
            finish(part)
        else:
            @pl.when(k == 0)
            def _():
                acc_ref[...] = part

            @pl.when(k > 0)
            def _():
                acc_ref[...] += part

            @pl.when(k == nk - 1)
            def _():
                finish(acc_ref[...])

    a_spec = pl.BlockSpec((tm, tk), lambda i, k: (i, k))
    b_spec = pl.BlockSpec((tk, N), lambda i, k: (k, 0)) if mode == "nn" else pl.BlockSpec((N, tk), lambda i, k: (0, k))
    row = pl.BlockSpec((tm, N), lambda i, k: (i, 0))
    col = pl.BlockSpec((tm, 1), lambda i, k: (i, 0))
    vec = pl.BlockSpec((1, N), lambda i, k: (0, 0))
    row_specs = [row if r.shape[1] == N else col for r in rows]
    if forward:
        out_specs = [row, row, row, col]
        out_shape = [jax.ShapeDtypeStruct((M, N), F32), jax.ShapeDtypeStruct((M, N), BF16),
                     jax.ShapeDtypeStruct((M, N), F32), jax.ShapeDtypeStruct((M, 1), F32)]
    else:
        out_specs = [row, row, vec, vec]
        out_shape = [jax.ShapeDtypeStruct((M, N), F32), jax.ShapeDtypeStruct((M, N), BF16),
                     jax.ShapeDtypeStruct((1, N), F32), jax.ShapeDtypeStruct((1, N), F32)]
    args = (a, b) + ((add,) if has_add else ()) + tuple(rows) + tuple(vecs)
    return pl.pallas_call(
        body, name=name, grid=(ni, nk),
        in_specs=[a_spec, b_spec] + ([row] if has_add else []) + row_specs + [vec] * len(vecs),
        out_specs=out_specs, out_shape=out_shape,
        scratch_shapes=[pltpu.VMEM((tm, N), F32)] if nk > 1 else [],
        compiler_params=_cparams("parallel" if forward else "arbitrary", "arbitrary"),
    )(*args)


def _pool_ln_fwd(h0, pw, ps, g, b, name, *, tm):
    Lp, D = h0.shape
    G = D // N_GROUPS
    halo_blocks = tm // MAX_WINDOW

    def body(x_ref, halo_ref, pw_ref, ps_ref, g_ref, b_ref,
             diff_ref, mix_ref, h_ref, hb_ref, xh_ref, rs_ref, ext_ref):
        i = pl.program_id(0)
        ext_ref[0:MAX_WINDOW, :] = jnp.where(i == 0, 0.0, halo_ref[...])
        ext_ref[MAX_WINDOW:MAX_WINDOW + tm, :] = x_ref[...]
        t1 = (i * tm + 1 + lax.broadcasted_iota(jnp.int32, (tm, 1), 0)).astype(F32)
        for gi, w in enumerate(POOL_WINDOWS):
            lo, hi = gi * G, (gi + 1) * G
            xg = x_ref[:, lo:hi]
            win = xg
            for j in range(1, w):
                win = win + ext_ref[MAX_WINDOW - j:MAX_WINDOW - j + tm, lo:hi]
            d = (win / jnp.minimum(t1, float(w)) - xg).astype(BF16)
            diff_ref[:, lo:hi] = d
            mix_ref[:, lo:hi] = jnp.dot(d, pw_ref[gi], preferred_element_type=F32)
        z = ALPHA * x_ref[...] + mix_ref[...] * ps_ref[...]
        h, xh, rstd = _ln_math(z, g_ref[...], b_ref[...])
        h_ref[...] = h
        hb_ref[...] = h.astype(BF16)
        xh_ref[...] = xh
        rs_ref[...] = rstd

    row = pl.BlockSpec((tm, D), lambda i: (i, 0))
    vec = pl.BlockSpec((1, D), lambda i: (0, 0))
    return pl.pallas_call(
        body, name=name, grid=(Lp // tm,),
        in_specs=[row,
                  pl.BlockSpec((MAX_WINDOW, D), lambda i: (jnp.maximum(i * halo_blocks - 1, 0), 0)),
                  pl.BlockSpec((N_GROUPS, G, G), lambda i: (0, 0, 0)), vec, vec, vec],
        out_specs=[row, row, row, row, row, pl.BlockSpec((tm, 1), lambda i: (i, 0))],
        out_shape=[jax.ShapeDtypeStruct((Lp, D), BF16), jax.ShapeDtypeStruct((Lp, D), F32),
                   jax.ShapeDtypeStruct((Lp, D), F32), jax.ShapeDtypeStruct((Lp, D), BF16),
                   jax.ShapeDtypeStruct((Lp, D), F32), jax.ShapeDtypeStruct((Lp, 1), F32)],
        scratch_shapes=[pltpu.VMEM((tm + MAX_WINDOW, D), F32)],
        compiler_params=_cparams("parallel"),
    )(h0, h0, pw, ps, g, b)


def _pool_bwd(dz, mixpre, pw, ps, name, *, tm):
    Lp, D = dz.shape
    G = D // N_GROUPS
    halo_blocks = tm // MAX_WINDOW
    n_halo = Lp // MAX_WINDOW
    ni = Lp // tm
    R = tm + MAX_WINDOW

    def body(dz_ref, halo_ref, mix_ref, pw_ref, ps_ref, dh_ref, dmb_ref, dsc_ref, ext_ref, dp_ref):
        i = pl.program_id(0)
        ext_ref[0:tm, :] = dz_ref[...]
        ext_ref[tm:R, :] = jnp.where(i == ni - 1, 0.0, halo_ref[...])
        dmix = (ext_ref[...] * ps_ref[...]).astype(BF16)
        dmb_ref[...] = dmix[0:tm]

        @pl.when(i == 0)
        def _():
            dsc_ref[...] = jnp.zeros_like(dsc_ref)

        dsc_ref[...] += jnp.sum(dz_ref[...] * mix_ref[...], axis=0, keepdims=True)
        t1 = (i * tm + 1 + lax.broadcasted_iota(jnp.int32, (R, 1), 0)).astype(F32)
        for gi, w in enumerate(POOL_WINDOWS):
            lo, hi = gi * G, (gi + 1) * G
            dd = lax.dot_general(dmix[:, lo:hi], pw_ref[gi], NT, preferred_element_type=F32)
            dp_ref[:, lo:hi] = dd / jnp.minimum(t1, float(w))
            back = dp_ref[0:tm, lo:hi]
            for j in range(1, w):
                back = back + dp_ref[j:j + tm, lo:hi]
            dh_ref[:, lo:hi] = ALPHA * dz_ref[:, lo:hi] - dd[0:tm] + back

    row = pl.BlockSpec((tm, D), lambda i: (i, 0))
    vec = pl.BlockSpec((1, D), lambda i: (0, 0))
    return pl.pallas_call(
        body, name=name, grid=(ni,),
        in_specs=[row,
                  pl.BlockSpec((MAX_WINDOW, D), lambda i: (jnp.minimum((i + 1) * halo_blocks, n_halo - 1), 0)),
                  row, pl.BlockSpec((N_GROUPS, G, G), lambda i: (0, 0, 0)), vec],
        out_specs=[row, row, vec],
        out_shape=[jax.ShapeDtypeStruct((Lp, D), F32), jax.ShapeDtypeStruct((Lp, D), BF16),
                   jax.ShapeDtypeStruct((1, D), F32)],
        scratch_shapes=[pltpu.VMEM((R, D), F32), pltpu.VMEM((R, D), F32)],
        compiler_params=_cparams("arbitrary"),
    )(dz, dz, mixpre, pw, ps)


def _pool_dw(diffb, dmb, name, *, tk):
    Lp, D = diffb.shape
    G = D // N_GROUPS

    def body(a_ref, b_ref, o_ref):
        @pl.when(pl.program_id(1) == 0)
        def _():
            o_ref[...] = jnp.zeros_like(o_ref)

        o_ref[0] += lax.dot_general(a_ref[...], b_ref[...], TN, preferred_element_type=F32)

    blk = pl.BlockSpec((tk, G), lambda g, k: (k, g))
    return pl.pallas_call(
        body, name=name, grid=(N_GROUPS, Lp // tk),
        in_specs=[blk, blk], out_specs=pl.BlockSpec((1, G, G), lambda g, k: (g, 0, 0)),
        out_shape=jax.ShapeDtypeStruct((N_GROUPS, G, G), F32),
        compiler_params=_cparams("parallel", "arbitrary"),
    )(diffb, dmb)


def _ln_bwd(parts, coefs, xh, rs, g, name, *, tm):
    Lp, D = xh.shape
    n = len(parts)

    def body(*refs):
        part_refs = refs[:n]
        xh_ref, rs_ref, g_ref = refs[n:n + 3]
        dz_ref, dzb_ref, dg_ref, db_ref = refs[n + 3:]
        dy = part_refs[0][...] if coefs[0] == 1.0 else coefs[0] * part_refs[0][...]
        for c, r in zip(coefs[1:], part_refs[1:]):
            dy = dy + (r[...] if c == 1.0 else c * r[...])
        x = xh_ref[...]
        dxh = dy * g_ref[...]
        m1 = jnp.mean(dxh, axis=-1, keepdims=True)
        m2 = jnp.mean(dxh * x, axis=-1, keepdims=True)
        dz = rs_ref[...] * (dxh - m1 - x * m2)
        dz_ref[...] = dz
        dzb_ref[...] = dz.astype(BF16)

        @pl.when(pl.program_id(0) == 0)
        def _():
            dg_ref[...] = jnp.zeros_like(dg_ref)
            db_ref[...] = jnp.zeros_like(db_ref)

        dg_ref[...] += jnp.sum(dy * x, axis=0, keepdims=True)
        db_ref[...] += jnp.sum(dy, axis=0, keepdims=True)

    row = pl.BlockSpec((tm, D), lambda i: (i, 0))
    vec = pl.BlockSpec((1, D), lambda i: (0, 0))
    return pl.pallas_call(
        body, name=name, grid=(Lp // tm,),
        in_specs=[row] * n + [row, pl.BlockSpec((tm, 1), lambda i: (i, 0)), vec],
        out_specs=[row, row, vec, vec],
        out_shape=[jax.ShapeDtypeStruct((Lp, D), F32), jax.ShapeDtypeStruct((Lp, D), BF16),
                   jax.ShapeDtypeStruct((1, D), F32), jax.ShapeDtypeStruct((1, D), F32)],
        compiler_params=_cparams("arbitrary"),
    )(*parts, xh, rs, g)


def _loss_head(h, tgt, name, *, tm, row_lo, row_hi):
    Lp, D = h.shape

    def body(h_ref, t_ref, dy_ref, loss_ref):
        i = pl.program_id(0)
        r = i * tm + lax.broadcasted_iota(jnp.int32, (tm, 1), 0)
        valid = (r >= row_lo) & (r < row_hi)
        e = jnp.where(valid, h_ref[...] - t_ref[...], 0.0)
        dy_ref[...] = e * (1.0 / D)

        @pl.when(i == 0)
        def _():
            loss_ref[...] = jnp.zeros_like(loss_ref)

        loss_ref[...] += 0.5 * jnp.sum(jnp.mean(e * e, axis=-1, keepdims=True), axis=0, keepdims=True)

    row = pl.BlockSpec((tm, D), lambda i: (i, 0))
    return pl.pallas_call(
        body, name=name, grid=(Lp // tm,),
        in_specs=[row, row], out_specs=[row, pl.BlockSpec((1, 1), lambda i: (0, 0))],
        out_shape=[jax.ShapeDtypeStruct((Lp, D), F32), jax.ShapeDtypeStruct((1, 1), F32)],
        compiler_params=_cparams("arbitrary"),
    )(h, tgt)


def _shift_rows_down(cur, prev, s, sub):
    return jnp.where(sub >= s, pltpu.roll(cur, s, 0), pltpu.roll(prev, s, 0))


def _shift_rows_up(cur, nxt, s, sub):
    return jnp.where(sub < SUBLANES - s, pltpu.roll(cur, SUBLANES - s, 0), pltpu.roll(nxt, SUBLANES - s, 0))


def _conv_group(cur, prev, cw_ref, cb_ref, sub):
    taps = [_shift_rows_down(cur, prev, 2, sub), _shift_rows_down(cur, prev, 1, sub), cur]
    c = cb_ref[...] + cw_ref[0:1, :] * taps[0] + cw_ref[1:2, :] * taps[1] + cw_ref[2:3, :] * taps[2]
    return c, taps


def _conv_glu_fwd(u, cw, cb, name, *, tm, tn):
    Lp, F2 = u.shape
    F = F2 // 2
    nj = F // tn
    halo_blocks = tm // SUBLANES
    S8 = SUBLANES
    assert GLU_STRIP == 2 * S8 and tm % GLU_STRIP == 0

    def body(ua_ref, ug_ref, pa_ref, pg_ref, cwa_ref, cwg_ref, cba_ref, cbg_ref, o_ref):
        first = pl.program_id(1) == 0
        sub = lax.broadcasted_iota(jnp.int32, (S8, tn), 0)

        def strip(r, prev_a, prev_g):
            out = []
            for g0 in (0, S8):
                a_cur = ua_ref[pl.ds(r + g0, S8), :]
                g_cur = ug_ref[pl.ds(r + g0, S8), :]
                a, _ = _conv_group(a_cur, prev_a, cwa_ref, cba_ref, sub)
                gate, _ = _conv_group(g_cur, prev_g, cwg_ref, cbg_ref, sub)
                out.append(a * jax.nn.sigmoid(a) * gate)
                prev_a, prev_g = a_cur, g_cur
            o_ref[pl.ds(r, GLU_STRIP), :] = jnp.concatenate(out, axis=0).astype(BF16)

        strip(0, jnp.where(first, 0.0, pa_ref[...]), jnp.where(first, 0.0, pg_ref[...]))

        def step(k, carry):
            r = pl.multiple_of(k * GLU_STRIP, GLU_STRIP)
            before = pl.ds(pl.multiple_of(r - S8, S8), S8)
            strip(r, ua_ref[before, :], ug_ref[before, :])
            return carry

        lax.fori_loop(1, tm // GLU_STRIP, step, 0)

    def prev(off):
        return pl.BlockSpec((SUBLANES, tn), lambda j, i: (jnp.maximum(i * halo_blocks - 1, 0), j + off))

    def cols(rows, off):
        return pl.BlockSpec((rows, tn), lambda j, i: (0, j + off))

    return pl.pallas_call(
        body, name=name, grid=(nj, Lp // tm),
        in_specs=[pl.BlockSpec((tm, tn), lambda j, i: (i, j)), pl.BlockSpec((tm, tn), lambda j, i: (i, j + nj)),
                  prev(0), prev(nj), cols(CONV_WIDTH, 0), cols(CONV_WIDTH, nj), cols(1, 0), cols(1, nj)],
        out_specs=pl.BlockSpec((tm, tn), lambda j, i: (i, j)),
        out_shape=jax.ShapeDtypeStruct((Lp, F), BF16),
        compiler_params=_cparams("parallel", "parallel"),
    )(u, u, u, u, cw, cw, cb, cb)


def _conv_glu_bwd(u, dact, cw, cb, name, *, tm, tn):
    Lp, F2 = u.shape
    F = F2 // 2
    nj = F // tn
    ni = Lp // tm
    halo_blocks = tm // SUBLANES
    n_halo = Lp // SUBLANES
    S8 = SUBLANES
    n_strips = tm // GLU_STRIP
    assert GLU_STRIP == 2 * S8 and tm % GLU_STRIP == 0

    def body(ua_ref, ug_ref, pa_ref, pg_ref, na_ref, ng_ref, da_ref, dn_ref,
             cwa_ref, cwg_ref, cba_ref, cbg_ref,
             dua_ref, dug_ref, dwa_ref, dwg_ref, dba_ref, dbg_ref,
             wacc_a, wacc_g, bacc_a, bacc_g):
        i = pl.program_id(1)
        first, last = i == 0, i == ni - 1
        sub = lax.broadcasted_iota(jnp.int32, (S8, tn), 0)
        for acc in (wacc_a, wacc_g, bacc_a, bacc_g):
            acc[...] = jnp.zeros_like(acc)

        def dconv(a_cur, a_prev, g_cur, g_prev, dact_rows):
            a, taps_a = _conv_group(a_cur, a_prev, cwa_ref, cba_ref, sub)
            gate, taps_g = _conv_group(g_cur, g_prev, cwg_ref, cbg_ref, sub)
            sg = jax.nn.sigmoid(a)
            dca = dact_rows * gate * (sg * (1.0 + a * (1.0 - sg)))
            dcg = dact_rows * (a * sg)
            return dca, dcg, taps_a, taps_g

        def du_group(dc, dc_after, cw_ref):
            return (cw_ref[2:3, :] * dc + cw_ref[1:2, :] * _shift_rows_up(dc, dc_after, 1, sub)
                    + cw_ref[0:1, :] * _shift_rows_up(dc, dc_after, 2, sub))

        def strip(r, a_prev, g_prev, dca_after, dcg_after):
            a0, a1 = ua_ref[pl.ds(r, S8), :], ua_ref[pl.ds(r + S8, S8), :]
            g0, g1 = ug_ref[pl.ds(r, S8), :], ug_ref[pl.ds(r + S8, S8), :]
            dca1, dcg1, ta1, tg1 = dconv(a1, a0, g1, g0, da_ref[pl.ds(r + S8, S8), :])
            dca0, dcg0, ta0, tg0 = dconv(a0, a_prev, g0, g_prev, da_ref[pl.ds(r, S8), :])
            dua_ref[pl.ds(r, GLU_STRIP), :] = jnp.concatenate(
                [du_group(dca0, dca1, cwa_ref), du_group(dca1, dca_after, cwa_ref)], axis=0).astype(BF16)
            dug_ref[pl.ds(r, GLU_STRIP), :] = jnp.concatenate(
                [du_group(dcg0, dcg1, cwg_ref), du_group(dcg1, dcg_after, cwg_ref)], axis=0).astype(BF16)
            for k in range(CONV_WIDTH):
                wacc_a[k] += dca0 * ta0[k] + dca1 * ta1[k]
                wacc_g[k] += dcg0 * tg0[k] + dcg1 * tg1[k]
            bacc_a[...] += dca0 + dca1
            bacc_g[...] += dcg0 + dcg1
            return dca0, dcg0

        tail = pl.ds(tm - S8, S8)
        dca_after, dcg_after, _, _ = dconv(na_ref[...], ua_ref[tail, :], ng_ref[...], ug_ref[tail, :],
                                           jnp.where(last, 0.0, dn_ref[...]))

        def step(t, carry):
            r = pl.multiple_of((n_strips - 1 - t) * GLU_STRIP, GLU_STRIP)
            before = pl.ds(pl.multiple_of(r - S8, S8), S8)
            return strip(r, ua_ref[before, :], ug_ref[before, :], *carry)

        dca_after, dcg_after = lax.fori_loop(0, n_strips - 1, step, (dca_after, dcg_after))
        strip(0, jnp.where(first, 0.0, pa_ref[...]), jnp.where(first, 0.0, pg_ref[...]), dca_after, dcg_after)

        @pl.when(first)
        def _():
            for r in (dwa_ref, dwg_ref, dba_ref, dbg_ref):
                r[...] = jnp.zeros_like(r)

        for wacc, bacc, dw_ref, db_ref in ((wacc_a, bacc_a, dwa_ref, dba_ref), (wacc_g, bacc_g, dwg_ref, dbg_ref)):
            db_ref[...] += jnp.sum(bacc[...], axis=0, keepdims=True)
            for k in range(CONV_WIDTH):
                dw_ref[k:k + 1, :] += jnp.sum(wacc[k], axis=0, keepdims=True)

    def tile(off):
        return pl.BlockSpec((tm, tn), lambda j, i: (i, j + off))

    def prev(off):
        return pl.BlockSpec((S8, tn), lambda j, i: (jnp.maximum(i * halo_blocks - 1, 0), j + off))

    def nxt(off):
        return pl.BlockSpec((S8, tn), lambda j, i: (jnp.minimum((i + 1) * halo_blocks, n_halo - 1), j + off))

    def cols(rows, off):
        return pl.BlockSpec((rows, tn), lambda j, i: (0, j + off))

    return pl.pallas_call(
        body, name=name, grid=(nj, ni),
        in_specs=[tile(0), tile(nj), prev(0), prev(nj), nxt(0), nxt(nj), tile(0), nxt(0),
                  cols(CONV_WIDTH, 0), cols(CONV_WIDTH, nj), cols(1, 0), cols(1, nj)],
        out_specs=[tile(0), tile(0), cols(CONV_WIDTH, 0), cols(CONV_WIDTH, 0), cols(1, 0), cols(1, 0)],
        out_shape=[jax.ShapeDtypeStruct((Lp, F), BF16), jax.ShapeDtypeStruct((Lp, F), BF16),
                   jax.ShapeDtypeStruct((CONV_WIDTH, F), F32), jax.ShapeDtypeStruct((CONV_WIDTH, F), F32),
                   jax.ShapeDtypeStruct((1, F), F32), jax.ShapeDtypeStruct((1, F), F32)],
        scratch_shapes=[pltpu.VMEM((CONV_WIDTH, S8, tn), F32), pltpu.VMEM((CONV_WIDTH, S8, tn), F32),
                        pltpu.VMEM((S8, tn), F32), pltpu.VMEM((S8, tn), F32)],
        compiler_params=_cparams("parallel", "arbitrary"),
    )(u, u, u, u, u, u, dact, dact, cw, cw, cb, cb)


def _bias_routing(n_heads, width, first_slot):
    h = lax.broadcasted_iota(jnp.int32, (LANES, width), 0)
    col = lax.broadcasted_iota(jnp.int32, (LANES, width), 1)
    base = LANES * (h // 2) + HEAD_DIM * (1 - h % 2) + first_slot
    return [((col == base + t) & (h < n_heads)).astype(BF16) for t in range(3)]


def _three_terms(x):
    hi = x.astype(BF16)
    r1 = x - hi.astype(F32)
    lo = r1.astype(BF16)
    lo2 = (r1 - lo.astype(F32)).astype(BF16)
    return hi, lo, lo2


def _logf_cumsum(pre, bf, name, *, tm, n_heads, width):
    Lp, W = pre.shape

    def body(p_ref, b_ref, c_ref, kx_ref, qx_ref, carry_ref):
        i = pl.program_id(0)

        @pl.when(i == 0)
        def _():
            carry_ref[...] = jnp.zeros_like(carry_ref)

        x = p_ref[...] + b_ref[...]
        lf = jnp.minimum(x, 0.0) - jnp.log(1.0 + jnp.exp(-jnp.abs(x)))
        tri = (lax.broadcasted_iota(jnp.int32, (tm, tm), 0) >= lax.broadcasted_iota(jnp.int32, (tm, tm), 1)).astype(F32)
        c = jnp.dot(tri, lf, precision=lax.Precision.HIGHEST, preferred_element_type=F32) + carry_ref[...]
        c_ref[...] = c
        carry_ref[...] = c[tm - 1:tm, :]
        terms = _three_terms(c)
        slot = lax.broadcasted_iota(jnp.int32, (tm, width), 1) % HEAD_DIM
        ones_k = ((slot >= 3) & (slot < BIAS_SLOTS)).astype(F32)
        ones_q = (slot < 3).astype(F32)
        kx = sum(jnp.dot(t, r, preferred_element_type=F32) for t, r in zip(terms, _bias_routing(n_heads, width, 0)))
        qx = sum(jnp.dot(t, r, preferred_element_type=F32) for t, r in zip(terms, _bias_routing(n_heads, width, 3)))
        kx_ref[...] = (ones_k - kx).astype(BF16)
        qx_ref[...] = (ones_q + qx).astype(BF16)

    row = pl.BlockSpec((tm, W), lambda i: (i, 0))
    wide = pl.BlockSpec((tm, width), lambda i: (i, 0))
    return pl.pallas_call(
        body, name=name, grid=(Lp // tm,),
        in_specs=[row, pl.BlockSpec((1, W), lambda i: (0, 0))], out_specs=[row, wide, wide],
        out_shape=[jax.ShapeDtypeStruct((Lp, W), F32), jax.ShapeDtypeStruct((Lp, width), BF16),
                   jax.ShapeDtypeStruct((Lp, width), BF16)],
        scratch_shapes=[pltpu.VMEM((1, W), F32)],
        compiler_params=_cparams("arbitrary"),
    )(pre, bf)


def _logf_bwd(dc_a, dc_b, pre, bf, name, *, tm):
    Lp, W = pre.shape
    ni = Lp // tm

    def body(dca_ref, dcb_ref, p_ref, b_ref, dpb_ref, db_ref, carry_ref):
        i = pl.program_id(0)

        @pl.when(i == 0)
        def _():
            carry_ref[...] = jnp.zeros_like(carry_ref)
            db_ref[...] = jnp.zeros_like(db_ref)

        triu = (lax.broadcasted_iota(jnp.int32, (tm, tm), 0) <= lax.broadcasted_iota(jnp.int32, (tm, tm), 1)).astype(F32)
        dl = jnp.dot(triu, dca_ref[...] + dcb_ref[...], precision=lax.Precision.HIGHEST,
                     preferred_element_type=F32) + carry_ref[...]
        carry_ref[...] = dl[0:1, :]
        dp = dl * jax.nn.sigmoid(-(p_ref[...] + b_ref[...]))
        dpb_ref[...] = dp.astype(BF16)
        db_ref[...] += jnp.sum(dp, axis=0, keepdims=True)

    rev = pl.BlockSpec((tm, W), lambda i: (ni - 1 - i, 0))
    vec = pl.BlockSpec((1, W), lambda i: (0, 0))
    return pl.pallas_call(
        body, name=name, grid=(ni,),
        in_specs=[rev, rev, rev, vec], out_specs=[rev, vec],
        out_shape=[jax.ShapeDtypeStruct((Lp, W), BF16), jax.ShapeDtypeStruct((1, W), F32)],
        scratch_shapes=[pltpu.VMEM((1, W), F32)],
        compiler_params=_cparams("arbitrary"),
    )(dc_a, dc_b, pre, bf)


def _attn_fwd(qb, kvb, kx, qx, name, *, tq):
    Lp, D = qb.shape
    H = D // HEAD_DIM
    nq = Lp // tq
    S8 = SUBLANES
    assert LANES // HEAD_DIM == 2
    HB = 2 * ATTN_FWD_PAIRS
    W = LANES * ATTN_FWD_PAIRS
    n_scratch = 5

    def body(q_ref, qx_ref, k_ref, kx_ref, v_ref, o_ref, ob_ref, lse_ref, vt_ref, *scratch):
        i = pl.program_id(1)
        heads = [scratch[n_scratch * hb:n_scratch * (hb + 1)] for hb in range(HB)]
        lane = lax.broadcasted_iota(jnp.int32, (tq, LANES), 1)

        def own_lanes(hb, x2, extra2):
            return jnp.where((lane < HEAD_DIM) == (hb % 2 == 0), x2, extra2)

        q_of = [own_lanes(hb, q_ref[:, pl.ds(LANES * (hb // 2), LANES)], qx_ref[:, pl.ds(LANES * (hb // 2), LANES)])
                for hb in range(HB)]

        @pl.when(i == 0)
        def _():
            for j in range(nq):
                vt_ref[j] = jnp.transpose(v_ref[pl.ds(j * tq, tq), :].astype(F32)).astype(BF16)

        for m_ref, l_ref, acc_ref, _, _ in heads:
            m_ref[...] = jnp.full_like(m_ref, NEG_INF)
            l_ref[...] = jnp.zeros_like(l_ref)
            acc_ref[...] = jnp.zeros_like(acc_ref)

        def chunk(j, masked):
            keys = pl.ds(pl.multiple_of(j * tq, tq), tq)
            for hb, (_, _, _, st_ref, _) in enumerate(heads):
                pair = pl.ds(LANES * (hb // 2), LANES)
                k_own = own_lanes(hb, k_ref[keys, pair], kx_ref[keys, pair])
                st_ref[...] = lax.dot_general(k_own, q_of[hb], NT, preferred_element_type=F32)
            for hb, (m_ref, l_ref, acc_ref, st_ref, pt_ref) in enumerate(heads):
                mx = jnp.full((S8, tq), NEG_INF, F32)
                for r0 in range(0, tq, ATTN_STRIP):
                    rows = pl.ds(r0, ATTN_STRIP)
                    st = st_ref[rows, :]
                    if masked:
                        keep = (lax.broadcasted_iota(jnp.int32, (ATTN_STRIP, tq), 1)
                                >= r0 + lax.broadcasted_iota(jnp.int32, (ATTN_STRIP, tq), 0))
                        st = jnp.where(keep, st, NEG_INF)
                        st_ref[rows, :] = st
                    for g0 in range(0, ATTN_STRIP, S8):
                        mx = jnp.maximum(mx, st[g0:g0 + S8])
                m_prev = m_ref[...]
                m_new = jnp.maximum(m_prev, jnp.max(mx, axis=0, keepdims=True))
                alpha = jnp.exp(m_prev - m_new)
                m_ref[...] = m_new
                ls = jnp.zeros((S8, tq), F32)
                for r0 in range(0, tq, ATTN_STRIP):
                    pieces = [jnp.exp(st_ref[pl.ds(r0 + g0, S8), :] - m_new) for g0 in range(0, ATTN_STRIP, S8)]
                    for piece in pieces:
                        ls = ls + piece
                    pt_ref[pl.ds(r0, ATTN_STRIP), :] = jnp.concatenate(pieces, axis=0).astype(BF16)
                l_ref[...] = alpha * l_ref[...] + ls
                pv = jnp.dot(vt_ref[j, pl.ds(LANES * (hb // 2), LANES), :], pt_ref[...], preferred_element_type=F32)
                acc_ref[...] = jnp.concatenate([alpha] * (LANES // S8), axis=0) * acc_ref[...] + pv

        def step(j, carry):
            chunk(j, False)
            return carry

        lax.fori_loop(0, i, step, 0)
        chunk(i, True)
        outs = []
        for hb, (m_ref, l_ref, acc_ref, _, _) in enumerate(heads):
            l_row = jnp.sum(l_ref[...], axis=0, keepdims=True)
            outs.append(acc_ref[...] / l_row)
            lse_ref[hb, 0] = m_ref[0:1, :] + jnp.log(l_row)
        first_rows = lax.broadcasted_iota(jnp.int32, (LANES, tq), 0) < HEAD_DIM
        for pp in range(ATTN_FWD_PAIRS):
            o2 = jnp.transpose(jnp.where(first_rows, outs[2 * pp], outs[2 * pp + 1]))
            o_ref[:, pl.ds(LANES * pp, LANES)] = o2
            ob_ref[:, pl.ds(LANES * pp, LANES)] = o2.astype(BF16)

    per_head = [pltpu.VMEM((S8, tq), F32), pltpu.VMEM((S8, tq), F32), pltpu.VMEM((LANES, tq), F32),
                pltpu.VMEM((tq, tq), F32), pltpu.VMEM((tq, tq), BF16)]
    assert len(per_head) == n_scratch and H % HB == 0
    v_blocks = D // W
    tile = pl.BlockSpec((tq, W), lambda p, i: (i, p))
    whole = pl.BlockSpec((Lp, W), lambda p, i: (0, p))
    return pl.pallas_call(
        body, name=name, grid=(H // HB, nq),
        in_specs=[tile, tile, whole, whole, pl.BlockSpec((Lp, W), lambda p, i: (0, v_blocks + p))],
        out_specs=[tile, tile, pl.BlockSpec((HB, 1, 1, tq), lambda p, i: (p, i, 0, 0))],
        out_shape=[jax.ShapeDtypeStruct((Lp, D), F32), jax.ShapeDtypeStruct((Lp, D), BF16),
                   jax.ShapeDtypeStruct((H, nq, 1, tq), F32)],
        scratch_shapes=[pltpu.VMEM((nq, W, tq), BF16)] + per_head * HB,
        compiler_params=_cparams("parallel", "arbitrary"),
    )(qb, qx, kvb, kx, kvb)


def _attn_delta(do, o, name, *, tm, n_heads):
    Lp, D = do.shape

    def body(do_ref, o_ref, d_ref):
        sel = (lax.broadcasted_iota(jnp.int32, (D, LANES), 0) // HEAD_DIM
               == lax.broadcasted_iota(jnp.int32, (D, LANES), 1)).astype(F32)
        d_ref[...] = jnp.dot(do_ref[...].astype(F32) * o_ref[...], sel, precision=lax.Precision.HIGHEST,
                             preferred_element_type=F32)

    row = pl.BlockSpec((tm, D), lambda i: (i, 0))
    return pl.pallas_call(
        body, name=name, grid=(Lp // tm,),
        in_specs=[row, row], out_specs=pl.BlockSpec((tm, LANES), lambda i: (i, 0)),
        out_shape=jax.ShapeDtypeStruct((Lp, LANES), F32),
        compiler_params=_cparams("parallel"),
    )(do, o)


def _attn_bwd(qb, dob, kvb, lse4, delta4, crow4, name, *, tq):
    Lp, D = qb.shape
    H = D // HEAD_DIM
    nq = Lp // tq
    HB = LANES // HEAD_DIM
    lane_tiles = tq // LANES
    n_scratch = 8
    assert HB == 2

    def body(q_ref, do_ref, k_ref, v_ref, lse_ref, dl_ref, c_ref,
             dqb_ref, dk_ref, dv_ref, dcs_ref, dcq_ref, dqt_ref, kt_ref, *scratch):
        j = pl.program_id(1)
        heads = [scratch[n_scratch * hb:n_scratch * (hb + 1)] for hb in range(HB)]
        first_head = lax.broadcasted_iota(jnp.int32, (tq, LANES), 1) < HEAD_DIM

        def split(x2):
            zero = jnp.zeros_like(x2)
            return [jnp.where(first_head, x2, zero), jnp.where(first_head, zero, x2)]

        @pl.when(j == 0)
        def _():
            dqt_ref[...] = jnp.zeros_like(dqt_ref)
            dcq_ref[...] = jnp.zeros_like(dcq_ref)

        k2 = k_ref[...]
        v2 = v_ref[...]
        kt_ref[...] = jnp.transpose(k2.astype(F32)).astype(BF16)
        first_rows = lax.broadcasted_iota(jnp.int32, (LANES, tq), 0) < HEAD_DIM
        for hb, (dk_acc, dv_acc, dc_acc, _, _, _, _, cs_ref) in enumerate(heads):
            dk_acc[...] = jnp.zeros_like(dk_acc)
            dv_acc[...] = jnp.zeros_like(dv_acc)
            dc_acc[...] = jnp.zeros_like(dc_acc)
            cs_ref[...] = jnp.transpose(jnp.broadcast_to(c_ref[hb, j], (LANES, tq)))

        def pair(i, masked):
            queries = pl.ds(pl.multiple_of(i * tq, tq), tq)
            q2 = q_ref[queries, :]
            do2 = do_ref[queries, :]
            q_of, do_of = split(q2), split(do2)
            for hb, (_, _, _, st_ref, dp_ref, _, _, _) in enumerate(heads):
                st_ref[...] = lax.dot_general(k2, q_of[hb], NT, preferred_element_type=F32)
                dp_ref[...] = lax.dot_general(v2, do_of[hb], NT, preferred_element_type=F32)
            dq_parts = []
            for hb, (dk_acc, dv_acc, dc_acc, st_ref, dp_ref, pt_ref, ds_ref, cs_ref) in enumerate(heads):
                bias_q = c_ref[hb, i] - lse_ref[hb, i]
                delta = dl_ref[hb, i]
                col_sum = jnp.zeros((SUBLANES, tq), F32)
                for r0 in range(0, tq, ATTN_STRIP):
                    rows = pl.ds(r0, ATTN_STRIP)
                    st = st_ref[rows, :] + (bias_q - jnp.concatenate([cs_ref[rows, :]] * lane_tiles, axis=1))
                    if masked:
                        keep = (lax.broadcasted_iota(jnp.int32, (ATTN_STRIP, tq), 1)
                                >= r0 + lax.broadcasted_iota(jnp.int32, (ATTN_STRIP, tq), 0))
                        st = jnp.where(keep, st, NEG_INF)
                    pt = jnp.exp(st)
                    dst = pt * (dp_ref[rows, :] - delta)
                    pt_ref[rows, :] = pt.astype(BF16)
                    ds_ref[rows, :] = dst.astype(BF16)
                    dc_acc[rows, :] += jnp.sum(dst, axis=1, keepdims=True)
                    for g0 in range(0, ATTN_STRIP, SUBLANES):
                        col_sum = col_sum + dst[g0:g0 + SUBLANES]
                dcq_ref[hb, i] += jnp.sum(col_sum, axis=0, keepdims=True)
                dv_acc[...] += jnp.dot(pt_ref[...], do2, preferred_element_type=F32)
                dk_acc[...] += jnp.dot(ds_ref[...], q2, preferred_element_type=F32)
                dq_parts.append(jnp.dot(kt_ref[...], ds_ref[...], preferred_element_type=F32))
            dqt_ref[i] += jnp.where(first_rows, dq_parts[0], dq_parts[1])

        def step(i, carry):
            pair(i, False)
            return carry

        pair(j, True)
        lax.fori_loop(j + 1, nq, step, 0)
        dk_ref[...] = jnp.where(first_head, heads[0][0][...], heads[1][0][...]).astype(BF16)
        dv_ref[...] = jnp.where(first_head, heads[0][1][...], heads[1][1][...]).astype(BF16)
        for hb in range(HB):
            dcs_ref[hb] = -heads[hb][2][...]

        @pl.when(j == nq - 1)
        def _():
            for i in range(nq):
                dqb_ref[pl.ds(i * tq, tq), :] = jnp.transpose(dqt_ref[i]).astype(BF16)

    per_head = [pltpu.VMEM((tq, LANES), F32), pltpu.VMEM((tq, LANES), F32), pltpu.VMEM((tq, 1), F32),
                pltpu.VMEM((tq, tq), F32), pltpu.VMEM((tq, tq), F32),
                pltpu.VMEM((tq, tq), BF16), pltpu.VMEM((tq, tq), BF16), pltpu.VMEM((tq, LANES), F32)]
    assert len(per_head) == n_scratch
    v_blocks = D // LANES
    whole = pl.BlockSpec((Lp, LANES), lambda p, j: (0, p))
    tile = pl.BlockSpec((tq, LANES), lambda p, j: (j, p))
    rows = pl.BlockSpec((HB, nq, 1, tq), lambda p, j: (p, 0, 0, 0))
    return pl.pallas_call(
        body, name=name, grid=(H // HB, nq),
        in_specs=[whole, whole, tile, pl.BlockSpec((tq, LANES), lambda p, j: (j, v_blocks + p)), rows, rows, rows],
        out_specs=[whole, tile, tile, pl.BlockSpec((HB, tq, 1), lambda p, j: (p, j, 0)), rows],
        out_shape=[jax.ShapeDtypeStruct((Lp, D), BF16), jax.ShapeDtypeStruct((Lp, D), BF16),
                   jax.ShapeDtypeStruct((Lp, D), BF16), jax.ShapeDtypeStruct((H, Lp, 1), F32),
                   jax.ShapeDtypeStruct((H, nq, 1, tq), F32)],
        scratch_shapes=[pltpu.VMEM((nq, LANES, tq), F32), pltpu.VMEM((LANES, tq), BF16)] + per_head * HB,
        compiler_params=_cparams("parallel", "arbitrary"),
    )(qb, dob, kvb, kvb, lse4, delta4, crow4)


def _remote(src, dst, send_sems, recv_sems, k, to):
    return pltpu.make_async_remote_copy(src_ref=src, dst_ref=dst, send_sem=send_sems.at[k], recv_sem=recv_sems.at[k],
                                        device_id=to, device_id_type=MESH)


def _place():
    x, y, c = lax.axis_index("x"), lax.axis_index("y"), lax.axis_index("c")
    other_chips = [(1 - x, y), (x, 1 - y), (1 - x, 1 - y)]
    return x, y, c, other_chips


def _all_gather_weights(wb, wf, name):
    Rb, C = wb.shape
    Rf = wf.shape[0]
    hb = Rb // 2

    def body(wb_ref, wf_ref, ob_ref, of_ref, send_sems, recv_sems):
        x, y, c, chips = _place()
        me = 2 * x + y
        sibling = (x, y, 1 - c)

        def half(chip, core):
            return ob_ref.at[chip, pl.ds(core * hb, hb), :]

        sent = []
        for j, (cx, cy) in enumerate(chips):
            sent.append(_remote(wb_ref.at[pl.ds(c * hb, hb), :], half(me, c), send_sems, recv_sems, j, (cx, cy, c)))
            sent.append(_remote(wf_ref, of_ref.at[me], send_sems, recv_sems, 3 + j, (cx, cy, c)))
        for cp in sent:
            cp.start()
        for j, (cx, cy) in enumerate(chips):
            chip = 2 * cx + cy
            _remote(half(chip, c), half(chip, c), send_sems, recv_sems, j, sibling).wait_recv()
            fwd = _remote(half(chip, c), half(chip, c), send_sems, recv_sems, 6 + j, sibling)
            fwd.start()
            sent.append(fwd)
        for j, (cx, cy) in enumerate(chips):
            chip = 2 * cx + cy
            _remote(wf_ref, of_ref.at[chip], send_sems, recv_sems, 3 + j, sibling).wait_recv()
            _remote(half(chip, 1 - c), half(chip, 1 - c), send_sems, recv_sems, 6 + j, sibling).wait_recv()
        for cp in sent:
            cp.wait_send()

    any_spec = pl.BlockSpec(memory_space=pl.ANY)
    return pl.pallas_call(
        body, name=name,
        in_specs=[any_spec, any_spec], out_specs=[any_spec, any_spec],
        out_shape=[jax.ShapeDtypeStruct((N_CHIPS, Rb, C), BF16), jax.ShapeDtypeStruct((N_CHIPS, Rf, C), F32)],
        scratch_shapes=[pltpu.SemaphoreType.DMA((9,)), pltpu.SemaphoreType.DMA((9,))],
    )(wb, wf)


def _half_of(ref, order, half):
    return ref.at[pl.ds(0, N_CHIPS), half] if order == "CH" else ref.at[half]


def _halves_to_sibling(grads, orders, name):
    n = len(grads)

    def body(*refs):
        g_refs, a_refs, (send_sems, recv_sems) = refs[:n], refs[n:2 * n], refs[2 * n:]
        x, y, c, _ = _place()
        copies = [_remote(_half_of(g, o, 1 - c), a, send_sems, recv_sems, k, (x, y, 1 - c))
                  for k, (g, a, o) in enumerate(zip(g_refs, a_refs, orders))]
        for cp in copies:
            cp.start()
        for cp in copies:
            cp.wait()

    any_spec = pl.BlockSpec(memory_space=pl.ANY)
    shapes = [g.shape[2:] for g in grads]
    return pl.pallas_call(
        body, name=name, in_specs=[any_spec] * n, out_specs=[any_spec] * n,
        out_shape=[jax.ShapeDtypeStruct((N_CHIPS,) + s, F32) for s in shapes],
        scratch_shapes=[pltpu.SemaphoreType.DMA((n,)), pltpu.SemaphoreType.DMA((n,))],
    )(*grads)


def _chip_partial(g, a, core, order, wire, name, *, tr):
    _, R, C = a.shape
    narrow = wire != F32

    def body(core_ref, g_ref, a_ref, *outs):
        p = g_ref[0, 0] + a_ref[0]
        outs[0][0] = p
        if narrow:
            outs[1][0] = p.astype(wire)

    if order == "CH":
        g_spec = pl.BlockSpec((1, 1, tr, C), lambda s, i, core_ref: (s, core_ref[0], i, 0))
    else:
        g_spec = pl.BlockSpec((1, 1, tr, C), lambda s, i, core_ref: (core_ref[0], s, i, 0))
    blk = pl.BlockSpec((1, tr, C), lambda s, i, core_ref: (s, i, 0))
    grid_spec = pltpu.PrefetchScalarGridSpec(
        num_scalar_prefetch=1, grid=(N_CHIPS, R // tr), in_specs=[g_spec, blk],
        out_specs=[blk, blk] if narrow else [blk])
    out_shape = [jax.ShapeDtypeStruct((N_CHIPS, R, C), F32)] + ([jax.ShapeDtypeStruct((N_CHIPS, R, C), wire)] if narrow else [])
    outs = pl.pallas_call(body, name=name, grid_spec=grid_spec, out_shape=out_shape,
                          compiler_params=_cparams("parallel", "parallel"))(core, g, a)
    return outs[0], outs[-1]


def _chip_exchange(parts, rep, name):
    n = len(parts)
    rr, C = rep.shape

    def body(*refs):
        p_refs, rep_ref = refs[:n], refs[n]
        land_refs, reps_ref = refs[n + 1:2 * n + 1], refs[2 * n + 1]
        send_sems, recv_sems, local_sem = refs[2 * n + 2:]
        x, y, c, chips = _place()
        me = 4 * x + 2 * y + c
        own = pltpu.make_async_copy(rep_ref, reps_ref.at[me], local_sem.at[0])
        own.start()
        sent = []
        for k, (p, land) in enumerate(zip(p_refs, land_refs)):
            for j, (cx, cy) in enumerate(chips):
                sent.append(_remote(p.at[2 * cx + cy], land.at[j], send_sems, recv_sems, 3 * k + j, (cx, cy, c)))
        for r in range(1, N_DEV):
            fx, fy, fc = (r >> 2) & 1, (r >> 1) & 1, r & 1
            sent.append(_remote(rep_ref, reps_ref.at[me], send_sems, recv_sems, 3 * n - 1 + r, (x ^ fx, y ^ fy, c ^ fc)))
        for cp in sent:
            cp.start()
        for k, (p, land) in enumerate(zip(p_refs, land_refs)):
            for j in range(3):
                _remote(p.at[0], land.at[j], send_sems, recv_sems, 3 * k + j, (x, y, c)).wait_recv()
        for r in range(1, N_DEV):
            fx, fy, fc = (r >> 2) & 1, (r >> 1) & 1, r & 1
            frm = 4 * (x ^ fx) + 2 * (y ^ fy) + (c ^ fc)
            _remote(rep_ref, reps_ref.at[frm], send_sems, recv_sems, 3 * n - 1 + r, (x, y, c)).wait_recv()
        for cp in sent:
            cp.wait_send()
        own.wait()

    any_spec = pl.BlockSpec(memory_space=pl.ANY)
    n_sems = 3 * n + N_DEV - 1
    return pl.pallas_call(
        body, name=name, in_specs=[any_spec] * (n + 1), out_specs=[any_spec] * (n + 1),
        out_shape=[jax.ShapeDtypeStruct((3,) + p.shape[1:], p.dtype) for p in parts]
        + [jax.ShapeDtypeStruct((N_DEV, rr, C), F32)],
        scratch_shapes=[pltpu.SemaphoreType.DMA((n_sems,)), pltpu.SemaphoreType.DMA((n_sems,)),
                        pltpu.SemaphoreType.DMA((1,))],
    )(*parts, rep)


def _adamw_math(w, g, m, v):
    m = ADAM_B1 * m + (1.0 - ADAM_B1) * g
    v = ADAM_B2 * v + (1.0 - ADAM_B2) * (g * g)
    m_hat = m / (1.0 - ADAM_B1 ** ADAM_STEP)
    v_hat = v / (1.0 - ADAM_B2 ** ADAM_STEP)
    delta = -ADAM_LR * (m_hat / (jnp.sqrt(v_hat) + ADAM_EPS) + ADAM_WD * w)
    return delta, m, v


def _adamw_owned(part, landed, w, m, v, place, name, *, tr):
    _, R, C = part.shape

    def body(place_ref, own_ref, land_ref, w_ref, m_ref, v_ref, g_ref, d_ref, mo_ref, vo_ref):
        g = own_ref[0]
        for s in range(3):
            g = g + land_ref[s].astype(F32)
        delta, m_new, v_new = _adamw_math(w_ref[0], g, m_ref[0], v_ref[0])
        g_ref[0] = g
        d_ref[0] = delta
        mo_ref[0] = m_new
        vo_ref[0] = v_new

    half = pl.BlockSpec((1, tr, C), lambda i, place_ref: (place_ref[0], i, 0))
    grid_spec = pltpu.PrefetchScalarGridSpec(
        num_scalar_prefetch=1, grid=(R // tr,),
        in_specs=[pl.BlockSpec((1, tr, C), lambda i, place_ref: (place_ref[1], i, 0)),
                  pl.BlockSpec((3, tr, C), lambda i, place_ref: (0, i, 0)), half, half, half],
        out_specs=[half] * 4)
    return pl.pallas_call(
        body, name=name, grid_spec=grid_spec, out_shape=[jax.ShapeDtypeStruct((2, R, C), F32)] * 4,
        compiler_params=_cparams("parallel"),
    )(place, part, landed, w, m, v)


def _join_halves(bufs, name):
    n = len(bufs)

    def body(*refs):
        out_refs, (send_sems, recv_sems) = refs[n:2 * n], refs[2 * n:]
        x, y, c, _ = _place()
        copies = [_remote(o.at[c], o.at[c], send_sems, recv_sems, k, (x, y, 1 - c)) for k, o in enumerate(out_refs)]
        for cp in copies:
            cp.start()
        for k, o in enumerate(out_refs):
            _remote(o.at[c], o.at[1 - c], send_sems, recv_sems, k, (x, y, 1 - c)).wait_recv()
        for cp in copies:
            cp.wait_send()

    any_spec = pl.BlockSpec(memory_space=pl.ANY)
    return pl.pallas_call(
        body, name=name, in_specs=[any_spec] * n, out_specs=[any_spec] * n,
        out_shape=[jax.ShapeDtypeStruct(b.shape, b.dtype) for b in bufs],
        input_output_aliases={k: k for k in range(n)},
        scratch_shapes=[pltpu.SemaphoreType.DMA((n,)), pltpu.SemaphoreType.DMA((n,))],
    )(*bufs)


def _sum_adamw(own, landed, w, m, v, name, *, tr):
    n = landed.shape[0]
    hr, C = own.shape

    def body(own_ref, land_ref, w_ref, m_ref, v_ref, o_ref):
        g = own_ref[...]
        for s in range(n):
            g = g + land_ref[s]
        delta, m_new, v_new = _adamw_math(w_ref[...], g, m_ref[...], v_ref[...])
        o_ref[0] = g
        o_ref[1] = delta
        o_ref[2] = m_new
        o_ref[3] = v_new

    blk = pl.BlockSpec((tr, C), lambda i: (i, 0))
    return pl.pallas_call(
        body, name=name, grid=(hr // tr,),
        in_specs=[blk, pl.BlockSpec((n, tr, C), lambda i: (0, i, 0)), blk, blk, blk],
        out_specs=pl.BlockSpec((4, tr, C), lambda i: (0, i, 0)),
        out_shape=jax.ShapeDtypeStruct((4, hr, C), F32), compiler_params=_cparams("parallel"),
    )(own, landed, w, m, v)


def _rows_of(shape):
    n = 1
    for d in shape:
        n *= d
    return -(-n // PACK_COLS)


def _pack(arrays, total_rows, dtype):
    parts, used = [], 0
    for a in arrays:
        flat = a.reshape(-1).astype(dtype)
        fill = _rows_of(a.shape) * PACK_COLS - flat.shape[0]
        parts += [flat] + ([jnp.zeros((fill,), dtype)] if fill else [])
        used += _rows_of(a.shape)
    if total_rows > used:
        parts.append(jnp.zeros(((total_rows - used) * PACK_COLS,), dtype))
    return jnp.concatenate(parts).reshape(total_rows, PACK_COLS)


def _unpack(buf, shapes):
    lead = buf.shape[:-2]
    out, r = [], 0
    for shp in shapes:
        n = 1
        for d in shp:
            n *= d
        rows = _rows_of(shp)
        piece = buf[..., r:r + rows, :].reshape(lead + (rows * PACK_COLS,))[..., :n]
        out.append(piece.reshape(lead + tuple(shp)))
        r += rows
    return out


def _join_shards(stacked, axis):
    return jnp.concatenate([stacked[s] for s in range(N_CHIPS)], axis=axis)


def _shard_of(full, axis, chip):
    width = full.shape[axis] // N_CHIPS
    return lax.slice_in_dim(full, chip * width, (chip + 1) * width, axis=axis)


def _local_step(h0, tgt, W, *, seq, tm):
    Lp, D = h0.shape
    H = D // HEAD_DIM
    F2 = W["ffn_w_in"].shape[-1]
    F = F2 // 2
    te = tm // 2
    nq = Lp // tm
    cap = 1408
    tD, tF, tF2 = _pick(D, cap), _pick(F, cap), _pick(F2, cap)
    t2D = _pick(2 * D, cap)
    t2Dc, tF2c = _pick(2 * D // N_CHIPS, cap), _pick(F2 // N_CHIPS, cap)
    tcn = _pick(F, cap)

    def vec(a):
        return a.reshape(1, -1)

    ln_g, ln_b = W["ln_g"], W["ln_b"]
    wf_pad = jnp.pad(W["w_f"], ((0, 0), (0, LANES - H)))
    bf_pad = jnp.pad(W["b_f"], (0, LANES - H)).reshape(1, LANES)

    def ffn_fwd(h, hb, l, tag):
        u = _mm(hb, W["ffn_w_in"][l], "nn", F32, f"ffn{tag}_up", tm=tm, tn=tF2, tk=tD)
        act = _conv_glu_fwd(u, W["ffn_conv_w"][l], vec(W["ffn_conv_b"][l]), f"ffn{tag}_glu", tm=te, tn=tcn)
        normed = _mm_ln(act, W["ffn_w_out"][l], "nn", f"ffn{tag}_down_ln", tm=tm, tk=tF, forward=True,
                        rows=[h], vecs=[vec(ln_g[l, 1]), vec(ln_b[l, 1])])
        return u, act, normed

    def ffn_bwd(dz, dzb, hb, u, act, l, tag, dw_in_acc, dw_out_acc, xh_in, rs_in, g_in):
        dact = _mm(dzb, W["ffn_w_out"][l], "nt", F32, f"ffn{tag}_dact", tm=tm, tn=tF, tk=tD)
        dw_out = _mm(act, dzb, "tn", F32, f"ffn{tag}_dwout", tm=tF, tn=tD, tk=tm, layer=l, into=dw_out_acc)
        dua, dug, dwa, dwg, dba, dbg = _conv_glu_bwd(u, dact, W["ffn_conv_w"][l], vec(W["ffn_conv_b"][l]),
                                                     f"ffn{tag}_dglu", tm=te, tn=tcn)
        du = jnp.concatenate([dua, dug], axis=1)
        dcw = jnp.concatenate([dwa, dwg], axis=1)
        dcb = jnp.concatenate([dba, dbg], axis=1)
        prev = _mm_ln(du, W["ffn_w_in"][l], "nt", f"ffn{tag}_dh_ln", tm=tm, tk=tF2, forward=False,
                      rows=[dz, xh_in, rs_in], vecs=[g_in])
        dw_in = _mm(hb, du, "tn", F32, f"ffn{tag}_dwin", tm=tD, tn=tF2c, tk=tm, chips=True, layer=l, into=dw_in_acc)
        return prev, dw_in, dw_out, dcw, dcb[0]

    diffb, mixpre, h1, h1b, xh1, rs1 = _pool_ln_fwd(h0, W["pool_w"][0], W["pool_scale"], vec(ln_g[0, 0]),
                                                    vec(ln_b[0, 0]), "pool_ln_fwd", tm=te)
    u0, act0, (h2, h2b, xh2, rs2) = ffn_fwd(h1, h1b, 0, "0")

    kvb = _mm(h2b, W["w_kv"], "nn", BF16, "kv_proj", tm=tm, tn=t2D, tk=tD)
    qb = _mm(h2b, W["w_q"][0], "nn", BF16, "q_proj", tm=tm, tn=tD, tk=tD, scale=HEAD_DIM ** -0.5)
    pre = _mm(h2b, wf_pad, "nn", F32, "f_proj", tm=tm, tn=LANES, tk=tD)
    c, kx, qx = _logf_cumsum(pre, bf_pad, "logf_cumsum", tm=tm, n_heads=H, width=D)

    crow4 = c[:, :H].T.reshape(H, nq, 1, tm)
    o_tok, ob, lse4 = _attn_fwd(qb, kvb, kx, qx, "attn_fwd", tq=tm)
    h3, h3b, xh3, rs3 = _mm_ln(ob, W["w_o"][0], "nn", "o_proj_ln", tm=tm, tk=tD, forward=True,
                               rows=[h2], vecs=[vec(ln_g[1, 0]), vec(ln_b[1, 0])])
    u1, act1, (h4, _, xh4, rs4) = ffn_fwd(h3, h3b, 1, "1")
    dy, loss = _loss_head(h4, tgt, "loss_head", tm=te, row_lo=N_META, row_hi=N_META + seq)

    dz4, dz4b, dg11, db11 = _ln_bwd([dy], [1.0], xh4, rs4, vec(ln_g[1, 1]), "ln11_bwd", tm=te)
    (dz3, dz3b, dg10, db10), dw_in, dw_out, dcw1, dcb1 = ffn_bwd(dz4, dz4b, h3b, u1, act1, 1, "1", None, None,
                                                                 xh3, rs3, vec(ln_g[1, 0]))

    dob = _mm(dz3b, W["w_o"][0], "nt", BF16, "o_proj_dx", tm=tm, tn=tD, tk=tD)
    dw_o = _mm(ob, dz3b, "tn", F32, "o_proj_dw", tm=tD, tn=tD, tk=tm)
    delta = _attn_delta(dob, o_tok, "attn_delta", tm=te, n_heads=H)
    dqb, dkb, dvb, dcs, dcq = _attn_bwd(qb, dob, kvb, lse4, delta[:, :H].T.reshape(H, nq, 1, tm), crow4,
                                        "attn_bwd", tq=tm)
    dc_keys = jnp.pad(dcs.reshape(H, Lp).T, ((0, 0), (0, LANES - H)))
    dc_queries = jnp.pad(dcq.reshape(H, Lp).T, ((0, 0), (0, LANES - H)))
    dpreb, dbf = _logf_bwd(dc_keys, dc_queries, pre, bf_pad, "logf_bwd", tm=tm)

    qs = HEAD_DIM ** -0.5
    dw_q = _mm(h2b, dqb, "tn", F32, "q_proj_dw", tm=tD, tn=tD, tk=tm, scale=qs)
    dw_kv = _mm(h2b, dkb, "tn", F32, "k_proj_dw", tm=tD, tn=t2Dc, tk=tm, chips=(0, N_CHIPS // 2))
    dw_kv = _mm(h2b, dvb, "tn", F32, "v_proj_dw", tm=tD, tn=t2Dc, tk=tm, chips=(N_CHIPS // 2, N_CHIPS // 2), into=dw_kv)
    dw_f = _mm(h2b, dpreb, "tn", F32, "f_proj_dw", tm=tD, tn=LANES, tk=tm)[:, :H]
    dh2 = _mm(dqb, W["w_q"][0], "nt", F32, "q_proj_dx", tm=tm, tn=tD, tk=tD, scale=qs)
    dh2 = _mm(dkb, W["w_kv"][:, :D], "nt", F32, "k_proj_dx", tm=tm, tn=tD, tk=tD, add=dh2)
    dh2 = _mm(dvb, W["w_kv"][:, D:], "nt", F32, "v_proj_dx", tm=tm, tn=tD, tk=tD, add=dh2)
    dz2, dz2b, dg01, db01 = _mm_ln(dpreb, wf_pad, "nt", "f_proj_dx_ln", tm=tm, tk=LANES, forward=False,
                                   rows=[dz3, xh2, rs2], vecs=[vec(ln_g[0, 1])], add=dh2)

    (dz1, _, dg00, db00), dw_in, dw_out, dcw0, dcb0 = ffn_bwd(dz2, dz2b, h1b, u0, act0, 0, "0", dw_in, dw_out,
                                                              xh1, rs1, vec(ln_g[0, 0]))
    dh0, dmb, dscale = _pool_bwd(dz1, mixpre, W["pool_w"][0], W["pool_scale"], "pool_bwd", tm=te)
    dw_pool = _pool_dw(diffb, dmb, "pool_dw", tk=tm)

    grads = {
        "meta": dh0[:N_META],
        "pool_w": dw_pool[None],
        "pool_scale": dscale,
        "w_kv": dw_kv,
        "w_f": dw_f,
        "b_f": dbf[0, :H],
        "w_q": dw_q[None],
        "w_o": dw_o[None],
        "ffn_w_in": dw_in,
        "ffn_conv_w": jnp.stack([dcw0, dcw1]),
        "ffn_conv_b": jnp.stack([dcb0, dcb1]),
        "ffn_w_out": dw_out,
        "ln_g": jnp.stack([jnp.stack([dg00[0], dg01[0]]), jnp.stack([dg10[0], dg11[0]])]),
        "ln_b": jnp.stack([jnp.stack([db00[0], db01[0]]), jnp.stack([db10[0], db11[0]])]),
    }
    return loss, dh0, grads


def _row_block(rows, cols):
    best = SUBLANES
    for t in range(SUBLANES, rows + 1, SUBLANES):
        if rows % t == 0 and t * cols * 4 <= ELEMENTWISE_BLOCK_BYTES:
            best = t
    return best


def _row_tile(length):
    return 640 if length >= 4096 else 128


def kernel(x, meta, pool_w, pool_scale, w_kv, w_f, b_f, w_q, w_o, ffn_w_in, ffn_conv_w, ffn_conv_b, ffn_w_out, ln_g, ln_b, loss_target, m_meta, m_pool_w, m_pool_scale, m_w_kv, m_w_f, m_b_f, m_w_q, m_w_o, m_ffn_w_in, m_ffn_conv_w, m_ffn_conv_b, m_ffn_w_out, m_ln_g, m_ln_b, v_meta, v_pool_w, v_pool_scale, v_w_kv, v_w_f, v_b_f, v_w_q, v_w_o, v_ffn_w_in, v_ffn_conv_w, v_ffn_conv_b, v_ffn_w_out, v_ln_g, v_ln_b):
    weights = dict(meta=meta, pool_w=pool_w, pool_scale=pool_scale, w_kv=w_kv, w_f=w_f, b_f=b_f, w_q=w_q, w_o=w_o,
                   ffn_w_in=ffn_w_in, ffn_conv_w=ffn_conv_w, ffn_conv_b=ffn_conv_b, ffn_w_out=ffn_w_out,
                   ln_g=ln_g, ln_b=ln_b)
    mom1 = dict(meta=m_meta, pool_w=m_pool_w, pool_scale=m_pool_scale, w_kv=m_w_kv, w_f=m_w_f, b_f=m_b_f, w_q=m_w_q,
                w_o=m_w_o, ffn_w_in=m_ffn_w_in, ffn_conv_w=m_ffn_conv_w, ffn_conv_b=m_ffn_conv_b,
                ffn_w_out=m_ffn_w_out, ln_g=m_ln_g, ln_b=m_ln_b)
    mom2 = dict(meta=v_meta, pool_w=v_pool_w, pool_scale=v_pool_scale, w_kv=v_w_kv, w_f=v_w_f, b_f=v_b_f, w_q=v_w_q,
                w_o=v_w_o, ffn_w_in=v_ffn_w_in, ffn_conv_w=v_ffn_conv_w, ffn_conv_b=v_ffn_conv_b,
                ffn_w_out=v_ffn_w_out, ln_g=v_ln_g, ln_b=v_ln_b)
    _, seq, D = x.shape
    L = N_META + seq
    tm = _row_tile(L)
    Lp = _round_up(L, tm)
    c_idx = lax.axis_index("c")
    chip = 2 * lax.axis_index("x") + lax.axis_index("y")

    shard_shapes = {n: weights[n].shape for n in SHARDED}
    rows_b = _round_up(sum(_rows_of(shard_shapes[n]) for n in MATMUL_WEIGHTS), 32)
    rows_f = _round_up(sum(_rows_of(shard_shapes[n]) for n in VECTOR_WEIGHTS), SUBLANES)
    wb = _pack([weights[n] for n in MATMUL_WEIGHTS], rows_b, BF16)
    wf = _pack([weights[n] for n in VECTOR_WEIGHTS], rows_f, F32)
    gb, gf = _all_gather_weights(wb, wf, "weights_all_gather")
    gb = lax.dynamic_update_index_in_dim(gb, wb, chip, axis=0)
    gf = lax.dynamic_update_index_in_dim(gf, wf, chip, axis=0)
    full = {}
    for names, buf in ((MATMUL_WEIGHTS, gb), (VECTOR_WEIGHTS, gf)):
        for n, stacked in zip(names, _unpack(buf, [shard_shapes[n] for n in names])):
            full[n] = _join_shards(stacked, SHARD_AXIS[n])
    full["b_f"] = b_f
    full["ffn_conv_b"] = ffn_conv_b

    pad = jnp.zeros((Lp - L, D), F32)
    h0 = jnp.concatenate([full["meta"], x[0], pad], axis=0)
    tgt = jnp.concatenate([jnp.zeros((N_META, D), F32), loss_target[0], pad], axis=0)
    loss, dh0, grads = _local_step(h0, tgt, full, seq=seq, tm=tm)
    loss = lax.psum(loss[0, 0], AXES)
    grad_x = dh0[N_META:L][None]

    core = c_idx.astype(jnp.int32).reshape(1)
    place = jnp.stack([c_idx, chip]).astype(jnp.int32)
    small_shapes = [shard_shapes[n] for n in SMALL_SHARDED]
    rows_s = _round_up(sum(_rows_of(s) for s in small_shapes), 2 * LANES)

    def packed_small(d):
        return _pack([d[n] for n in SMALL_SHARDED], rows_s, F32).reshape(2, rows_s // 2, PACK_COLS)

    names, orders, wires, g_views, wmv = [], [], [], [], []
    for n, order in BIG_SHARDED:
        shp = shard_shapes[n]
        C = shp[-1]
        R = weights[n].size // C // 2
        lead = (N_CHIPS, 2) if order == "CH" else (2, N_CHIPS)
        names.append(n)
        orders.append(order)
        wires.append(BF16)
        g_views.append(grads[n].reshape(lead + (R, C)))
        wmv.append([d[n].reshape(2, R, C) for d in (weights, mom1, mom2)])
    names.append("small")
    orders.append("CH")
    wires.append(F32)
    g_views.append(jnp.stack([_pack([_shard_of(grads[n], SHARD_AXIS[n], s) for n in SMALL_SHARDED], rows_s, F32)
                              for s in range(N_CHIPS)]).reshape(N_CHIPS, 2, rows_s // 2, PACK_COLS))
    wmv.append([packed_small(d) for d in (weights, mom1, mom2)])

    from_sibling = _halves_to_sibling(g_views, orders, "grads_to_sibling")
    parts, on_wire = [], []
    for n, order, wire, g, a in zip(names, orders, wires, g_views, from_sibling):
        p, pw = _chip_partial(g, a, core, order, wire, f"chip_sum_{n}", tr=_row_block(a.shape[1], a.shape[2]))
        parts.append(p)
        on_wire.append(pw)

    rep_shapes = [weights[n].shape for n in REPLICATED]
    rows_r = _round_up(sum(_rows_of(s) for s in rep_shapes), SUBLANES)
    rep = _pack([grads[n] for n in REPLICATED], rows_r, F32)
    *landed, reps = _chip_exchange(on_wire, rep, "grads_chip_exchange")

    halves = []
    for n, p, b, (w_, m_, v_) in zip(names, parts, landed, wmv):
        halves += _adamw_owned(p, b, w_, m_, v_, place, f"adamw_{n}", tr=_row_block(p.shape[1], p.shape[2]))
    joined = _join_halves(halves, "results_to_sibling")
    out = {}
    for k, n in enumerate(names[:-1]):
        out[n] = [a.reshape(shard_shapes[n]) for a in joined[4 * k:4 * k + 4]]
    small_out = [_unpack(a.reshape(rows_s, PACK_COLS), small_shapes) for a in joined[-4:]]
    for k, n in enumerate(SMALL_SHARDED):
        out[n] = [small_out[kind][k] for kind in range(4)]

    def packr(d):
        return _pack([d[n] for n in REPLICATED], rows_r, F32)

    res_r = _sum_adamw(reps[0], reps[1:], packr(weights), packr(mom1), packr(mom2), "adamw_replicated", tr=rows_r)
    rep_out = _unpack(res_r, rep_shapes)

    out.update({n: a for n, a in zip(REPLICATED, rep_out)})
    result = [loss, grad_x]
    for k in range(4):
        result += [out[n][k] for n in WEIGHT_ORDER]
    return tuple(result)
```

```python
import functools

import jax
import jax.numpy as jnp
from jax import lax
from jax.experimental import pallas as pl
from jax.experimental.pallas import tpu as pltpu

N_META = 16
POOL_WINDOWS = (2, 4, 8, 16)
MAX_WINDOW = max(POOL_WINDOWS)
N_GROUPS = len(POOL_WINDOWS)
HEAD_DIM = 64
DEPTH = 2
CONV_WIDTH = 3
ALPHA = (2.0 * DEPTH) ** 0.25
LN_EPS = 1e-5
NEG_INF = -1e30
ADAM_LR = 0.001
ADAM_B1 = 0.9
ADAM_B2 = 0.999
ADAM_EPS = 1e-08
ADAM_WD = 0.01
ADAM_STEP = 10

F32 = jnp.float32
BF16 = jnp.bfloat16
ATTN_FWD_PAIRS = 2
BIAS_SLOTS = 6
ATTN_STRIP = 32
GLU_STRIP = 16
LANES = 128
SUBLANES = 8
PACK_COLS = 1024
VMEM_LIMIT = 56 * 1024 * 1024
AXES = ("x", "y", "c")
MESH = pl.DeviceIdType.MESH

NN = (((1,), (0,)), ((), ()))
NT = (((1,), (1,)), ((), ()))
TN = (((0,), (0,)), ((), ()))

SHARD_AXIS = {"meta": 1, "pool_w": 2, "pool_scale": 1, "w_kv": 1, "w_f": 0, "w_q": 1, "w_o": 1,
              "ffn_w_in": 2, "ffn_conv_w": 2, "ffn_w_out": 1, "ln_g": 2, "ln_b": 2}
SHARDED = ("meta", "pool_w", "pool_scale", "w_kv", "w_f", "w_q", "w_o", "ffn_w_in", "ffn_conv_w",
           "ffn_w_out", "ln_g", "ln_b")
REPLICATED = ("b_f", "ffn_conv_b")
MATMUL_WEIGHTS = ("pool_w", "w_kv", "w_f", "w_q", "w_o", "ffn_w_in", "ffn_w_out")
VECTOR_WEIGHTS = ("meta", "pool_scale", "ffn_conv_w", "ln_g", "ln_b")
WEIGHT_ORDER = ("meta", "pool_w", "pool_scale", "w_kv", "w_f", "b_f", "w_q", "w_o", "ffn_w_in",
                "ffn_conv_w", "ffn_conv_b", "ffn_w_out", "ln_g", "ln_b")
BIG_SHARDED = (("w_kv", "CH"), ("w_q", "CH"), ("w_o", "CH"), ("ffn_w_in", "HC"), ("ffn_w_out", "HC"))
SMALL_SHARDED = ("meta", "pool_w", "pool_scale", "w_f", "ffn_conv_w", "ln_g", "ln_b")
ELEMENTWISE_BLOCK_BYTES = 3 * 512 * 1024
N_CHIPS = 4
N_DEV = 8


def _cparams(*sem):
    return pltpu.CompilerParams(dimension_semantics=sem, vmem_limit_bytes=VMEM_LIMIT)


def _round_up(n, m):
    return (n + m - 1) // m * m


def _pick(n, cap):
    if n <= cap:
        return n
    best = 0
    for t in range(LANES, cap + 1, LANES):
        if n % t == 0:
            best = t
    assert best, (n, cap)
    return best


def _mm(a, b, mode, out_dtype, name, *, tm, tn, tk, scale=None, add=None, chips=False, layer=None, into=None,
        b_halves=False):
    if mode == "nn":
        (M, K), N = a.shape, b.shape[1]
    elif mode == "nt":
        (M, K), N = a.shape, b.shape[0]
    elif b_halves:
        (K, M), N = a.shape, 2 * b.shape[2]
    else:
        (K, M), N = a.shape, b.shape[1]
    assert M % tm == 0 and N % tn == 0 and K % tk == 0, (name, M, N, K, tm, tn, tk)
    nk = K // tk
    dn = {"nn": NN, "nt": NT, "tn": TN}[mode]
    has_add = add is not None
    has_into = into is not None
    assert not (has_add and (chips or layer is not None))

    def body(*refs):
        a_ref, b_ref = refs[0], refs[1]
        add_ref = refs[2] if has_add else None
        o_ref = refs[2 + has_add + has_into]
        acc_ref = refs[-1] if nk > 1 else None
        k = pl.program_id(2)
        part = lax.dot_general(a_ref[...], b_ref[0] if b_halves else b_ref[...], dn, preferred_element_type=F32)

        def finish(r):
            if scale is not None:
                r = r * scale
            if has_add:
                r = r + add_ref[...]
            o_ref[...] = r.astype(out_dtype).reshape(o_ref.shape)

        if nk == 1:
            finish(part)
        else:
            @pl.when(k == 0)
            def _():
                acc_ref[...] = part

            @pl.when(k > 0)
            def _():
                acc_ref[...] += part

            @pl.when(k == nk - 1)
            def _():
                finish(acc_ref[...])

    if mode == "nn":
        a_spec = pl.BlockSpec((tm, tk), lambda j, i, k: (i, k))
        b_spec = pl.BlockSpec((tk, tn), lambda j, i, k: (k, j))
    elif mode == "nt":
        a_spec = pl.BlockSpec((tm, tk), lambda j, i, k: (i, k))
        b_spec = pl.BlockSpec((tn, tk), lambda j, i, k: (j, k))
    else:
        a_spec = pl.BlockSpec((tk, tm), lambda j, i, k: (k, i))
        b_spec = pl.BlockSpec((tk, tn), lambda j, i, k: (k, j))
        if b_halves:
            per_half = N // 2 // tn
            b_spec = pl.BlockSpec((1, tk, tn), lambda j, i, k: (j // per_half, k, j % per_half))
    out_dims, blk = (M, N), (tm, tn)
    if chips:
        base, count = (0, N_CHIPS) if chips is True else chips
        per_chip = N // count // tn
        assert per_chip * tn * count == N, (name, N, tn)
        out_dims, blk = (N_CHIPS, M, N // count), (1, tm, tn)
        where = lambda j, i: (base + j // per_chip, i, j % per_chip)
    else:
        where = lambda j, i: (i, j)
    if layer is not None:
        out_dims, blk = (DEPTH,) + out_dims, (1,) + blk
        o_spec = pl.BlockSpec(blk, lambda j, i, k: (layer,) + where(j, i))
    else:
        o_spec = pl.BlockSpec(blk, lambda j, i, k: where(j, i))
    in_specs = [a_spec, b_spec] + ([o_spec] if has_add else []) + ([pl.BlockSpec(memory_space=pl.ANY)] if has_into else [])
    args = (a, b) + ((add,) if has_add else ()) + ((into,) if has_into else ())
    return pl.pallas_call(
        body, name=name, grid=(N // tn, M // tm, nk),
        in_specs=in_specs, out_specs=o_spec,
        out_shape=jax.ShapeDtypeStruct(out_dims, out_dtype),
        input_output_aliases={len(args) - 1: 0} if has_into else {},
        scratch_shapes=[pltpu.VMEM((tm, tn), F32)] if nk > 1 else [],
        compiler_params=_cparams("parallel", "parallel", "arbitrary"),
    )(*args)


def _ln_math(z, g, b):
    mu = jnp.mean(z, axis=-1, keepdims=True)
    zc = z - mu
    var = jnp.mean(zc * zc, axis=-1, keepdims=True)
    rstd = lax.rsqrt(var + LN_EPS)
    xh = zc * rstd
    return xh * g + b, xh, rstd


def _mm_ln(a, b, mode, name, *, tm, tk, forward, rows, vecs, scale=None, add=None, a_halves=False):
    assert mode in ("nn", "nt")
    M, K = (a.shape[1], 2 * a.shape[2]) if a_halves else a.shape
    N = b.shape[1] if mode == "nn" else b.shape[0]
    assert M % tm == 0 and K % tk == 0, (name, M, K, tm, tk)
    nk = K // tk
    ni = M // tm
    dn = {"nn": NN, "nt": NT}[mode]
    has_add = add is not None
    n_in = 2 + has_add + len(rows) + len(vecs)

    def body(*refs):
        a_ref, b_ref = refs[0], refs[1]
        add_ref = refs[2] if has_add else None
        row_refs = refs[2 + has_add:2 + has_add + len(rows)]
        vec_refs = refs[2 + has_add + len(rows):n_in]
        outs = refs[n_in:n_in + 4]
        acc_ref = refs[-1] if nk > 1 else None
        i, k = pl.program_id(0), pl.program_id(1)
        part = lax.dot_general(a_ref[0] if a_halves else a_ref[...], b_ref[...], dn, preferred_element_type=F32)

        def finish(y):
            if scale is not None:
                y = y * scale
            if has_add:
                y = y + add_ref[...]
            if forward:
                h, xh, rstd = _ln_math(ALPHA * row_refs[0][...] + y, vec_refs[0][...], vec_refs[1][...])
                outs[0][...] = h
                outs[1][...] = h.astype(BF16)
                outs[2][...] = xh
                outs[3][...] = rstd
            else:
                dy = ALPHA * row_refs[0][...] + y
                x = row_refs[1][...]
                dxh = dy * vec_refs[0][...]
                m1 = jnp.mean(dxh, axis=-1, keepdims=True)
                m2 = jnp.mean(dxh * x, axis=-1, keepdims=True)
                dz = row_refs[2][...] * (dxh - m1 - x * m2)
                outs[0][...] = dz
                outs[1][...] = dz.astype(BF16)

                @pl.when(i == 0)
                def _():
                    outs[2][...] = jnp.zeros_like(outs[2])
                    outs[3][...] = jnp.zeros_like(outs[3])

                outs[2][...] += jnp.sum(dy * x, axis=0, keepdims=True)
                outs[3][...] += jnp.sum(dy, axis=0, keepdims=True)

        if nk == 1:
            finish(part)
        else:
            @pl.when(k == 0)
            def _():
                acc_ref[...] = part

            @pl.when(k > 0)
            def _():
                acc_ref[...] += part

            @pl.when(k == nk - 1)
            def _():
                finish(acc_ref[...])

    a_spec = pl.BlockSpec((tm, tk), lambda i, k: (i, k))
    if a_halves:
        per_half = nk // 2
        a_spec = pl.BlockSpec((1, tm, tk), lambda i, k: (k // per_half, i, k % per_half))
    b_spec = pl.BlockSpec((tk, N), lambda i, k: (k, 0)) if mode == "nn" else pl.BlockSpec((N, tk), lambda i, k: (0, k))
    row = pl.BlockSpec((tm, N), lambda i, k: (i, 0))
    col = pl.BlockSpec((tm, 1), lambda i, k: (i, 0))
    vec = pl.BlockSpec((1, N), lambda i, k: (0, 0))
    row_specs = [row if r.shape[1] == N else col for r in rows]
    if forward:
        out_specs = [row, row, row, col]
        out_shape = [jax.ShapeDtypeStruct((M, N), F32), jax.ShapeDtypeStruct((M, N), BF16),
                     jax.ShapeDtypeStruct((M, N), F32), jax.ShapeDtypeStruct((M, 1), F32)]
    else:
        out_specs = [row, row, vec, vec]
        out_shape = [jax.ShapeDtypeStruct((M, N), F32), jax.ShapeDtypeStruct((M, N), BF16),
                     jax.ShapeDtypeStruct((1, N), F32), jax.ShapeDtypeStruct((1, N), F32)]
    args = (a, b) + ((add,) if has_add else ()) + tuple(rows) + tuple(vecs)
    return pl.pallas_call(
        body, name=name, grid=(ni, nk),
        in_specs=[a_spec, b_spec] + ([row] if has_add else []) + row_specs + [vec] * len(vecs),
        out_specs=out_specs, out_shape=out_shape,
        scratch_shapes=[pltpu.VMEM((tm, N), F32)] if nk > 1 else [],
        compiler_params=_cparams("parallel" if forward else "arbitrary", "arbitrary"),
    )(*args)


def _pool_ln_fwd(h0, pw, ps, g, b, name, *, tm):
    Lp, D = h0.shape
    G = D // N_GROUPS
    halo_blocks = tm // MAX_WINDOW

    def body(x_ref, halo_ref, pw_ref, ps_ref, g_ref, b_ref,
             diff_ref, mix_ref, h_ref, hb_ref, xh_ref, rs_ref, ext_ref):
        i = pl.program_id(0)
        ext_ref[0:MAX_WINDOW, :] = jnp.where(i == 0, 0.0, halo_ref[...])
        ext_ref[MAX_WINDOW:MAX_WINDOW + tm, :] = x_ref[...]
        t1 = (i * tm + 1 + lax.broadcasted_iota(jnp.int32, (tm, 1), 0)).astype(F32)
        for gi, w in enumerate(POOL_WINDOWS):
            lo, hi = gi * G, (gi + 1) * G
            xg = x_ref[:, lo:hi]
            win = xg
            for j in range(1, w):
                win = win + ext_ref[MAX_WINDOW - j:MAX_WINDOW - j + tm, lo:hi]
            d = (win / jnp.minimum(t1, float(w)) - xg).astype(BF16)
            diff_ref[:, lo:hi] = d
            mix_ref[:, lo:hi] = jnp.dot(d, pw_ref[gi], preferred_element_type=F32)
        z = ALPHA * x_ref[...] + mix_ref[...] * ps_ref[...]
        h, xh, rstd = _ln_math(z, g_ref[...], b_ref[...])
        h_ref[...] = h
        hb_ref[...] = h.astype(BF16)
        xh_ref[...] = xh
        rs_ref[...] = rstd

    row = pl.BlockSpec((tm, D), lambda i: (i, 0))
    vec = pl.BlockSpec((1, D), lambda i: (0, 0))
    return pl.pallas_call(
        body, name=name, grid=(Lp // tm,),
        in_specs=[row,
                  pl.BlockSpec((MAX_WINDOW, D), lambda i: (jnp.maximum(i * halo_blocks - 1, 0), 0)),
                  pl.BlockSpec((N_GROUPS, G, G), lambda i: (0, 0, 0)), vec, vec, vec],
        out_specs=[row, row, row, row, row, pl.BlockSpec((tm, 1), lambda i: (i, 0))],
        out_shape=[jax.ShapeDtypeStruct((Lp, D), BF16), jax.ShapeDtypeStruct((Lp, D), F32),
                   jax.ShapeDtypeStruct((Lp, D), F32), jax.ShapeDtypeStruct((Lp, D), BF16),
                   jax.ShapeDtypeStruct((Lp, D), F32), jax.ShapeDtypeStruct((Lp, 1), F32)],
        scratch_shapes=[pltpu.VMEM((tm + MAX_WINDOW, D), F32)],
        compiler_params=_cparams("parallel"),
    )(h0, h0, pw, ps, g, b)


def _pool_bwd(dz, mixpre, pw, ps, name, *, tm):
    Lp, D = dz.shape
    G = D // N_GROUPS
    halo_blocks = tm // MAX_WINDOW
    n_halo = Lp // MAX_WINDOW
    ni = Lp // tm
    R = tm + MAX_WINDOW

    def body(dz_ref, halo_ref, mix_ref, pw_ref, ps_ref, dh_ref, dmb_ref, dsc_ref, ext_ref, dp_ref):
        i = pl.program_id(0)
        ext_ref[0:tm, :] = dz_ref[...]
        ext_ref[tm:R, :] = jnp.where(i == ni - 1, 0.0, halo_ref[...])
        dmix = (ext_ref[...] * ps_ref[...]).astype(BF16)
        dmb_ref[...] = dmix[0:tm]

        @pl.when(i == 0)
        def _():
            dsc_ref[...] = jnp.zeros_like(dsc_ref)

        dsc_ref[...] += jnp.sum(dz_ref[...] * mix_ref[...], axis=0, keepdims=True)
        t1 = (i * tm + 1 + lax.broadcasted_iota(jnp.int32, (R, 1), 0)).astype(F32)
        for gi, w in enumerate(POOL_WINDOWS):
            lo, hi = gi * G, (gi + 1) * G
            dd = lax.dot_general(dmix[:, lo:hi], pw_ref[gi], NT, preferred_element_type=F32)
            dp_ref[:, lo:hi] = dd / jnp.minimum(t1, float(w))
            back = dp_ref[0:tm, lo:hi]
            for j in range(1, w):
                back = back + dp_ref[j:j + tm, lo:hi]
            dh_ref[:, lo:hi] = ALPHA * dz_ref[:, lo:hi] - dd[0:tm] + back

    row = pl.BlockSpec((tm, D), lambda i: (i, 0))
    vec = pl.BlockSpec((1, D), lambda i: (0, 0))
    return pl.pallas_call(
        body, name=name, grid=(ni,),
        in_specs=[row,
                  pl.BlockSpec((MAX_WINDOW, D), lambda i: (jnp.minimum((i + 1) * halo_blocks, n_halo - 1), 0)),
                  row, pl.BlockSpec((N_GROUPS, G, G), lambda i: (0, 0, 0)), vec],
        out_specs=[row, row, vec],
        out_shape=[jax.ShapeDtypeStruct((Lp, D), F32), jax.ShapeDtypeStruct((Lp, D), BF16),
                   jax.ShapeDtypeStruct((1, D), F32)],
        scratch_shapes=[pltpu.VMEM((R, D), F32), pltpu.VMEM((R, D), F32)],
        compiler_params=_cparams("arbitrary"),
    )(dz, dz, mixpre, pw, ps)


def _pool_dw(diffb, dmb, name, *, tk):
    Lp, D = diffb.shape
    G = D // N_GROUPS

    def body(a_ref, b_ref, o_ref):
        @pl.when(pl.program_id(1) == 0)
        def _():
            o_ref[...] = jnp.zeros_like(o_ref)

        o_ref[0] += lax.dot_general(a_ref[...], b_ref[...], TN, preferred_element_type=F32)

    blk = pl.BlockSpec((tk, G), lambda g, k: (k, g))
    return pl.pallas_call(
        body, name=name, grid=(N_GROUPS, Lp // tk),
        in_specs=[blk, blk], out_specs=pl.BlockSpec((1, G, G), lambda g, k: (g, 0, 0)),
        out_shape=jax.ShapeDtypeStruct((N_GROUPS, G, G), F32),
        compiler_params=_cparams("parallel", "arbitrary"),
    )(diffb, dmb)


def _ln_bwd(parts, coefs, xh, rs, g, name, *, tm):
    Lp, D = xh.shape
    n = len(parts)

    def body(*refs):
        part_refs = refs[:n]
        xh_ref, rs_ref, g_ref = refs[n:n + 3]
        dz_ref, dzb_ref, dg_ref, db_ref = refs[n + 3:]
        dy = part_refs[0][...] if coefs[0] == 1.0 else coefs[0] * part_refs[0][...]
        for c, r in zip(coefs[1:], part_refs[1:]):
            dy = dy + (r[...] if c == 1.0 else c * r[...])
        x = xh_ref[...]
        dxh = dy * g_ref[...]
        m1 = jnp.mean(dxh, axis=-1, keepdims=True)
        m2 = jnp.mean(dxh * x, axis=-1, keepdims=True)
        dz = rs_ref[...] * (dxh - m1 - x * m2)
        dz_ref[...] = dz
        dzb_ref[...] = dz.astype(BF16)

        @pl.when(pl.program_id(0) == 0)
        def _():
            dg_ref[...] = jnp.zeros_like(dg_ref)
            db_ref[...] = jnp.zeros_like(db_ref)

        dg_ref[...] += jnp.sum(dy * x, axis=0, keepdims=True)
        db_ref[...] += jnp.sum(dy, axis=0, keepdims=True)

    row = pl.BlockSpec((tm, D), lambda i: (i, 0))
    vec = pl.BlockSpec((1, D), lambda i: (0, 0))
    return pl.pallas_call(
        body, name=name, grid=(Lp // tm,),
        in_specs=[row] * n + [row, pl.BlockSpec((tm, 1), lambda i: (i, 0)), vec],
        out_specs=[row, row, vec, vec],
        out_shape=[jax.ShapeDtypeStruct((Lp, D), F32), jax.ShapeDtypeStruct((Lp, D), BF16),
                   jax.ShapeDtypeStruct((1, D), F32), jax.ShapeDtypeStruct((1, D), F32)],
        compiler_params=_cparams("arbitrary"),
    )(*parts, xh, rs, g)


def _loss_head(h, tgt, name, *, tm, row_lo, row_hi):
    Lp, D = h.shape

    def body(h_ref, t_ref, dy_ref, loss_ref):
        i = pl.program_id(0)
        r = i * tm + lax.broadcasted_iota(jnp.int32, (tm, 1), 0)
        valid = (r >= row_lo) & (r < row_hi)
        e = jnp.where(valid, h_ref[...] - t_ref[...], 0.0)
        dy_ref[...] = e * (1.0 / D)

        @pl.when(i == 0)
        def _():
            loss_ref[...] = jnp.zeros_like(loss_ref)

        loss_ref[...] += 0.5 * jnp.sum(jnp.mean(e * e, axis=-1, keepdims=True), axis=0, keepdims=True)

    row = pl.BlockSpec((tm, D), lambda i: (i, 0))
    return pl.pallas_call(
        body, name=name, grid=(Lp // tm,),
        in_specs=[row, row], out_specs=[row, pl.BlockSpec((1, 1), lambda i: (0, 0))],
        out_shape=[jax.ShapeDtypeStruct((Lp, D), F32), jax.ShapeDtypeStruct((1, 1), F32)],
        compiler_params=_cparams("arbitrary"),
    )(h, tgt)


def _shift_rows_down(cur, prev, s, sub):
    return jnp.where(sub >= s, pltpu.roll(cur, s, 0), pltpu.roll(prev, s, 0))


def _shift_rows_up(cur, nxt, s, sub):
    return jnp.where(sub < SUBLANES - s, pltpu.roll(cur, SUBLANES - s, 0), pltpu.roll(nxt, SUBLANES - s, 0))


def _conv_group(cur, prev, cw_ref, cb_ref, sub):
    taps = [_shift_rows_down(cur, prev, 2, sub), _shift_rows_down(cur, prev, 1, sub), cur]
    c = cb_ref[...] + cw_ref[0:1, :] * taps[0] + cw_ref[1:2, :] * taps[1] + cw_ref[2:3, :] * taps[2]
    return c, taps


def _conv_glu_fwd(u, cw, cb, name, *, tm, tn):
    Lp, F2 = u.shape
    F = F2 // 2
    nj = F // tn
    halo_blocks = tm // SUBLANES
    S8 = SUBLANES
    assert GLU_STRIP == 2 * S8 and tm % GLU_STRIP == 0

    def body(ua_ref, ug_ref, pa_ref, pg_ref, cwa_ref, cwg_ref, cba_ref, cbg_ref, o_ref):
        first = pl.program_id(1) == 0
        sub = lax.broadcasted_iota(jnp.int32, (S8, tn), 0)

        def strip(r, prev_a, prev_g):
            out = []
            for g0 in (0, S8):
                a_cur = ua_ref[pl.ds(r + g0, S8), :]
                g_cur = ug_ref[pl.ds(r + g0, S8), :]
                a, _ = _conv_group(a_cur, prev_a, cwa_ref, cba_ref, sub)
                gate, _ = _conv_group(g_cur, prev_g, cwg_ref, cbg_ref, sub)
                out.append(a * jax.nn.sigmoid(a) * gate)
                prev_a, prev_g = a_cur, g_cur
            o_ref[pl.ds(r, GLU_STRIP), :] = jnp.concatenate(out, axis=0).astype(BF16)

        strip(0, jnp.where(first, 0.0, pa_ref[...]), jnp.where(first, 0.0, pg_ref[...]))

        def step(k, carry):
            r = pl.multiple_of(k * GLU_STRIP, GLU_STRIP)
            before = pl.ds(pl.multiple_of(r - S8, S8), S8)
            strip(r, ua_ref[before, :], ug_ref[before, :])
            return carry

        lax.fori_loop(1, tm // GLU_STRIP, step, 0)

    def prev(off):
        return pl.BlockSpec((SUBLANES, tn), lambda j, i: (jnp.maximum(i * halo_blocks - 1, 0), j + off))

    def cols(rows, off):
        return pl.BlockSpec((rows, tn), lambda j, i: (0, j + off))

    return pl.pallas_call(
        body, name=name, grid=(nj, Lp // tm),
        in_specs=[pl.BlockSpec((tm, tn), lambda j, i: (i, j)), pl.BlockSpec((tm, tn), lambda j, i: (i, j + nj)),
                  prev(0), prev(nj), cols(CONV_WIDTH, 0), cols(CONV_WIDTH, nj), cols(1, 0), cols(1, nj)],
        out_specs=pl.BlockSpec((tm, tn), lambda j, i: (i, j)),
        out_shape=jax.ShapeDtypeStruct((Lp, F), BF16),
        compiler_params=_cparams("parallel", "parallel"),
    )(u, u, u, u, cw, cw, cb, cb)


def _conv_glu_bwd(u, dact, cw, cb, name, *, tm, tn):
    Lp, F2 = u.shape
    F = F2 // 2
    nj = F // tn
    ni = Lp // tm
    halo_blocks = tm // SUBLANES
    n_halo = Lp // SUBLANES
    S8 = SUBLANES
    n_strips = tm // GLU_STRIP
    assert GLU_STRIP == 2 * S8 and tm % GLU_STRIP == 0

    def body(ua_ref, ug_ref, pa_ref, pg_ref, na_ref, ng_ref, da_ref, dn_ref,
             cwa_ref, cwg_ref, cba_ref, cbg_ref,
             du_ref, dwa_ref, dwg_ref, dba_ref, dbg_ref,
             wacc_a, wacc_g, bacc_a, bacc_g):
        i = pl.program_id(1)
        first, last = i == 0, i == ni - 1
        sub = lax.broadcasted_iota(jnp.int32, (S8, tn), 0)
        for acc in (wacc_a, wacc_g, bacc_a, bacc_g):
            acc[...] = jnp.zeros_like(acc)

        def dconv(a_cur, a_prev, g_cur, g_prev, dact_rows):
            a, taps_a = _conv_group(a_cur, a_prev, cwa_ref, cba_ref, sub)
            gate, taps_g = _conv_group(g_cur, g_prev, cwg_ref, cbg_ref, sub)
            sg = jax.nn.sigmoid(a)
            dca = dact_rows * gate * (sg * (1.0 + a * (1.0 - sg)))
            dcg = dact_rows * (a * sg)
            return dca, dcg, taps_a, taps_g

        def du_group(dc, dc_after, cw_ref):
            return (cw_ref[2:3, :] * dc + cw_ref[1:2, :] * _shift_rows_up(dc, dc_after, 1, sub)
                    + cw_ref[0:1, :] * _shift_rows_up(dc, dc_after, 2, sub))

        def strip(r, a_prev, g_prev, dca_after, dcg_after):
            a0, a1 = ua_ref[pl.ds(r, S8), :], ua_ref[pl.ds(r + S8, S8), :]
            g0, g1 = ug_ref[pl.ds(r, S8), :], ug_ref[pl.ds(r + S8, S8), :]
            dca1, dcg1, ta1, tg1 = dconv(a1, a0, g1, g0, da_ref[pl.ds(r + S8, S8), :])
            dca0, dcg0, ta0, tg0 = dconv(a0, a_prev, g0, g_prev, da_ref[pl.ds(r, S8), :])
            du_ref[0, pl.ds(r, GLU_STRIP), :] = jnp.concatenate(
                [du_group(dca0, dca1, cwa_ref), du_group(dca1, dca_after, cwa_ref)], axis=0).astype(BF16)
            du_ref[1, pl.ds(r, GLU_STRIP), :] = jnp.concatenate(
                [du_group(dcg0, dcg1, cwg_ref), du_group(dcg1, dcg_after, cwg_ref)], axis=0).astype(BF16)
            for k in range(CONV_WIDTH):
                wacc_a[k] += dca0 * ta0[k] + dca1 * ta1[k]
                wacc_g[k] += dcg0 * tg0[k] + dcg1 * tg1[k]
            bacc_a[...] += dca0 + dca1
            bacc_g[...] += dcg0 + dcg1
            return dca0, dcg0

        tail = pl.ds(tm - S8, S8)
        dca_after, dcg_after, _, _ = dconv(na_ref[...], ua_ref[tail, :], ng_ref[...], ug_ref[tail, :],
                                           jnp.where(last, 0.0, dn_ref[...]))

        def step(t, carry):
            r = pl.multiple_of((n_strips - 1 - t) * GLU_STRIP, GLU_STRIP)
            before = pl.ds(pl.multiple_of(r - S8, S8), S8)
            return strip(r, ua_ref[before, :], ug_ref[before, :], *carry)

        dca_after, dcg_after = lax.fori_loop(0, n_strips - 1, step, (dca_after, dcg_after))
        strip(0, jnp.where(first, 0.0, pa_ref[...]), jnp.where(first, 0.0, pg_ref[...]), dca_after, dcg_after)

        @pl.when(first)
        def _():
            for r in (dwa_ref, dwg_ref, dba_ref, dbg_ref):
                r[...] = jnp.zeros_like(r)

        for wacc, bacc, dw_ref, db_ref in ((wacc_a, bacc_a, dwa_ref, dba_ref), (wacc_g, bacc_g, dwg_ref, dbg_ref)):
            db_ref[...] += jnp.sum(bacc[...], axis=0, keepdims=True)
            for k in range(CONV_WIDTH):
                dw_ref[k:k + 1, :] += jnp.sum(wacc[k], axis=0, keepdims=True)

    def tile(off):
        return pl.BlockSpec((tm, tn), lambda j, i: (i, j + off))

    def prev(off):
        return pl.BlockSpec((S8, tn), lambda j, i: (jnp.maximum(i * halo_blocks - 1, 0), j + off))

    def nxt(off):
        return pl.BlockSpec((S8, tn), lambda j, i: (jnp.minimum((i + 1) * halo_blocks, n_halo - 1), j + off))

    def cols(rows, off):
        return pl.BlockSpec((rows, tn), lambda j, i: (0, j + off))

    return pl.pallas_call(
        body, name=name, grid=(nj, ni),
        in_specs=[tile(0), tile(nj), prev(0), prev(nj), nxt(0), nxt(nj), tile(0), nxt(0),
                  cols(CONV_WIDTH, 0), cols(CONV_WIDTH, nj), cols(1, 0), cols(1, nj)],
        out_specs=[pl.BlockSpec((2, tm, tn), lambda j, i: (0, i, j)),
                   cols(CONV_WIDTH, 0), cols(CONV_WIDTH, 0), cols(1, 0), cols(1, 0)],
        out_shape=[jax.ShapeDtypeStruct((2, Lp, F), BF16),
                   jax.ShapeDtypeStruct((CONV_WIDTH, F), F32), jax.ShapeDtypeStruct((CONV_WIDTH, F), F32),
                   jax.ShapeDtypeStruct((1, F), F32), jax.ShapeDtypeStruct((1, F), F32)],
        scratch_shapes=[pltpu.VMEM((CONV_WIDTH, S8, tn), F32), pltpu.VMEM((CONV_WIDTH, S8, tn), F32),
                        pltpu.VMEM((S8, tn), F32), pltpu.VMEM((S8, tn), F32)],
        compiler_params=_cparams("parallel", "arbitrary"),
    )(u, u, u, u, u, u, dact, dact, cw, cw, cb, cb)


def _bias_routing(n_heads, width, first_slot):
    h = lax.broadcasted_iota(jnp.int32, (LANES, width), 0)
    col = lax.broadcasted_iota(jnp.int32, (LANES, width), 1)
    base = LANES * (h // 2) + HEAD_DIM * (1 - h % 2) + first_slot
    return [((col == base + t) & (h < n_heads)).astype(BF16) for t in range(3)]


def _three_terms(x):
    hi = x.astype(BF16)
    r1 = x - hi.astype(F32)
    lo = r1.astype(BF16)
    lo2 = (r1 - lo.astype(F32)).astype(BF16)
    return hi, lo, lo2


def _logf_cumsum(pre, bf, name, *, tm, n_heads, width):
    Lp, W = pre.shape

    def body(p_ref, b_ref, c_ref, kx_ref, qx_ref, carry_ref):
        i = pl.program_id(0)

        @pl.when(i == 0)
        def _():
            carry_ref[...] = jnp.zeros_like(carry_ref)

        x = p_ref[...] + b_ref[...]
        lf = jnp.minimum(x, 0.0) - jnp.log(1.0 + jnp.exp(-jnp.abs(x)))
        tri = (lax.broadcasted_iota(jnp.int32, (tm, tm), 0) >= lax.broadcasted_iota(jnp.int32, (tm, tm), 1)).astype(F32)
        c = jnp.dot(tri, lf, precision=lax.Precision.HIGHEST, preferred_element_type=F32) + carry_ref[...]
        c_ref[...] = c
        carry_ref[...] = c[tm - 1:tm, :]
        terms = _three_terms(c)
        slot = lax.broadcasted_iota(jnp.int32, (tm, width), 1) % HEAD_DIM
        ones_k = ((slot >= 3) & (slot < BIAS_SLOTS)).astype(F32)
        ones_q = (slot < 3).astype(F32)
        kx = sum(jnp.dot(t, r, preferred_element_type=F32) for t, r in zip(terms, _bias_routing(n_heads, width, 0)))
        qx = sum(jnp.dot(t, r, preferred_element_type=F32) for t, r in zip(terms, _bias_routing(n_heads, width, 3)))
        kx_ref[...] = (ones_k - kx).astype(BF16)
        qx_ref[...] = (ones_q + qx).astype(BF16)

    row = pl.BlockSpec((tm, W), lambda i: (i, 0))
    wide = pl.BlockSpec((tm, width), lambda i: (i, 0))
    return pl.pallas_call(
        body, name=name, grid=(Lp // tm,),
        in_specs=[row, pl.BlockSpec((1, W), lambda i: (0, 0))], out_specs=[row, wide, wide],
        out_shape=[jax.ShapeDtypeStruct((Lp, W), F32), jax.ShapeDtypeStruct((Lp, width), BF16),
                   jax.ShapeDtypeStruct((Lp, width), BF16)],
        scratch_shapes=[pltpu.VMEM((1, W), F32)],
        compiler_params=_cparams("arbitrary"),
    )(pre, bf)


def _logf_bwd(dc_a, dc_b, pre, bf, name, *, tm):
    Lp, W = pre.shape
    ni = Lp // tm

    def body(dca_ref, dcb_ref, p_ref, b_ref, dpb_ref, db_ref, carry_ref):
        i = pl.program_id(0)

        @pl.when(i == 0)
        def _():
            carry_ref[...] = jnp.zeros_like(carry_ref)
            db_ref[...] = jnp.zeros_like(db_ref)

        triu = (lax.broadcasted_iota(jnp.int32, (tm, tm), 0) <= lax.broadcasted_iota(jnp.int32, (tm, tm), 1)).astype(F32)
        dl = jnp.dot(triu, dca_ref[...] + dcb_ref[...], precision=lax.Precision.HIGHEST,
                     preferred_element_type=F32) + carry_ref[...]
        carry_ref[...] = dl[0:1, :]
        dp = dl * jax.nn.sigmoid(-(p_ref[...] + b_ref[...]))
        dpb_ref[...] = dp.astype(BF16)
        db_ref[...] += jnp.sum(dp, axis=0, keepdims=True)

    rev = pl.BlockSpec((tm, W), lambda i: (ni - 1 - i, 0))
    vec = pl.BlockSpec((1, W), lambda i: (0, 0))
    return pl.pallas_call(
        body, name=name, grid=(ni,),
        in_specs=[rev, rev, rev, vec], out_specs=[rev, vec],
        out_shape=[jax.ShapeDtypeStruct((Lp, W), BF16), jax.ShapeDtypeStruct((1, W), F32)],
        scratch_shapes=[pltpu.VMEM((1, W), F32)],
        compiler_params=_cparams("arbitrary"),
    )(dc_a, dc_b, pre, bf)


def _attn_fwd(qb, kvb, kx, qx, name, *, tq):
    Lp, D = qb.shape
    H = D // HEAD_DIM
    nq = Lp // tq
    S8 = SUBLANES
    assert LANES // HEAD_DIM == 2
    HB = 2 * ATTN_FWD_PAIRS
    W = LANES * ATTN_FWD_PAIRS
    n_scratch = 5

    def body(q_ref, qx_ref, k_ref, kx_ref, v_ref, o_ref, ob_ref, lse_ref, vt_ref, *scratch):
        i = pl.program_id(1)
        heads = [scratch[n_scratch * hb:n_scratch * (hb + 1)] for hb in range(HB)]
        lane = lax.broadcasted_iota(jnp.int32, (tq, LANES), 1)

        def own_lanes(hb, x2, extra2):
            return jnp.where((lane < HEAD_DIM) == (hb % 2 == 0), x2, extra2)

        q_of = [own_lanes(hb, q_ref[:, pl.ds(LANES * (hb // 2), LANES)], qx_ref[:, pl.ds(LANES * (hb // 2), LANES)])
                for hb in range(HB)]

        @pl.when(i == 0)
        def _():
            for j in range(nq):
                vt_ref[j] = jnp.transpose(v_ref[pl.ds(j * tq, tq), :].astype(F32)).astype(BF16)

        for m_ref, l_ref, acc_ref, _, _ in heads:
            m_ref[...] = jnp.full_like(m_ref, NEG_INF)
            l_ref[...] = jnp.zeros_like(l_ref)
            acc_ref[...] = jnp.zeros_like(acc_ref)

        def chunk(j, masked):
            keys = pl.ds(pl.multiple_of(j * tq, tq), tq)
            for hb, (_, _, _, st_ref, _) in enumerate(heads):
                pair = pl.ds(LANES * (hb // 2), LANES)
                k_own = own_lanes(hb, k_ref[keys, pair], kx_ref[keys, pair])
                st_ref[...] = lax.dot_general(k_own, q_of[hb], NT, preferred_element_type=F32)
            for hb, (m_ref, l_ref, acc_ref, st_ref, pt_ref) in enumerate(heads):
                mx = jnp.full((S8, tq), NEG_INF, F32)
                for r0 in range(0, tq, ATTN_STRIP):
                    rows = pl.ds(r0, ATTN_STRIP)
                    st = st_ref[rows, :]
                    if masked:
                        keep = (lax.broadcasted_iota(jnp.int32, (ATTN_STRIP, tq), 1)
                                >= r0 + lax.broadcasted_iota(jnp.int32, (ATTN_STRIP, tq), 0))
                        st = jnp.where(keep, st, NEG_INF)
                        st_ref[rows, :] = st
                    for g0 in range(0, ATTN_STRIP, S8):
                        mx = jnp.maximum(mx, st[g0:g0 + S8])
                m_prev = m_ref[...]
                m_new = jnp.maximum(m_prev, jnp.max(mx, axis=0, keepdims=True))
                alpha = jnp.exp(m_prev - m_new)
                m_ref[...] = m_new
                ls = jnp.zeros((S8, tq), F32)
                for r0 in range(0, tq, ATTN_STRIP):
                    pieces = [jnp.exp(st_ref[pl.ds(r0 + g0, S8), :] - m_new) for g0 in range(0, ATTN_STRIP, S8)]
                    for piece in pieces:
                        ls = ls + piece
                    pt_ref[pl.ds(r0, ATTN_STRIP), :] = jnp.concatenate(pieces, axis=0).astype(BF16)
                l_ref[...] = alpha * l_ref[...] + ls
                pv = jnp.dot(vt_ref[j, pl.ds(LANES * (hb // 2), LANES), :], pt_ref[...], preferred_element_type=F32)
                acc_ref[...] = jnp.concatenate([alpha] * (LANES // S8), axis=0) * acc_ref[...] + pv

        def step(j, carry):
            chunk(j, False)
            return carry

        lax.fori_loop(0, i, step, 0)
        chunk(i, True)
        outs = []
        for hb, (m_ref, l_ref, acc_ref, _, _) in enumerate(heads):
            l_row = jnp.sum(l_ref[...], axis=0, keepdims=True)
            outs.append(acc_ref[...] / l_row)
            lse_ref[hb, 0] = m_ref[0:1, :] + jnp.log(l_row)
        first_rows = lax.broadcasted_iota(jnp.int32, (LANES, tq), 0) < HEAD_DIM
        for pp in range(ATTN_FWD_PAIRS):
            o2 = jnp.transpose(jnp.where(first_rows, outs[2 * pp], outs[2 * pp + 1]))
            o_ref[:, pl.ds(LANES * pp, LANES)] = o2
            ob_ref[:, pl.ds(LANES * pp, LANES)] = o2.astype(BF16)

    per_head = [pltpu.VMEM((S8, tq), F32), pltpu.VMEM((S8, tq), F32), pltpu.VMEM((LANES, tq), F32),
                pltpu.VMEM((tq, tq), F32), pltpu.VMEM((tq, tq), BF16)]
    assert len(per_head) == n_scratch and H % HB == 0
    v_blocks = D // W
    tile = pl.BlockSpec((tq, W), lambda p, i: (i, p))
    whole = pl.BlockSpec((Lp, W), lambda p, i: (0, p))
    return pl.pallas_call(
        body, name=name, grid=(H // HB, nq),
        in_specs=[tile, tile, whole, whole, pl.BlockSpec((Lp, W), lambda p, i: (0, v_blocks + p))],
        out_specs=[tile, tile, pl.BlockSpec((HB, 1, 1, tq), lambda p, i: (p, i, 0, 0))],
        out_shape=[jax.ShapeDtypeStruct((Lp, D), F32), jax.ShapeDtypeStruct((Lp, D), BF16),
                   jax.ShapeDtypeStruct((H, nq, 1, tq), F32)],
        scratch_shapes=[pltpu.VMEM((nq, W, tq), BF16)] + per_head * HB,
        compiler_params=_cparams("parallel", "arbitrary"),
    )(qb, qx, kvb, kx, kvb)


def _attn_delta(do, o, name, *, tm, n_heads):
    Lp, D = do.shape

    def body(do_ref, o_ref, d_ref):
        sel = (lax.broadcasted_iota(jnp.int32, (D, LANES), 0) // HEAD_DIM
               == lax.broadcasted_iota(jnp.int32, (D, LANES), 1)).astype(F32)
        d_ref[...] = jnp.dot(do_ref[...].astype(F32) * o_ref[...], sel, precision=lax.Precision.HIGHEST,
                             preferred_element_type=F32)

    row = pl.BlockSpec((tm, D), lambda i: (i, 0))
    return pl.pallas_call(
        body, name=name, grid=(Lp // tm,),
        in_specs=[row, row], out_specs=pl.BlockSpec((tm, LANES), lambda i: (i, 0)),
        out_shape=jax.ShapeDtypeStruct((Lp, LANES), F32),
        compiler_params=_cparams("parallel"),
    )(do, o)


def _attn_bwd(qb, dob, kvb, lse4, delta4, crow4, name, *, tq):
    Lp, D = qb.shape
    H = D // HEAD_DIM
    nq = Lp // tq
    HB = LANES // HEAD_DIM
    lane_tiles = tq // LANES
    n_scratch = 8
    assert HB == 2

    def body(q_ref, do_ref, k_ref, v_ref, lse_ref, dl_ref, c_ref,
             dqb_ref, dk_ref, dv_ref, dcs_ref, dcq_ref, dqt_ref, kt_ref, *scratch):
        j = pl.program_id(1)
        heads = [scratch[n_scratch * hb:n_scratch * (hb + 1)] for hb in range(HB)]
        first_head = lax.broadcasted_iota(jnp.int32, (tq, LANES), 1) < HEAD_DIM

        def split(x2):
            zero = jnp.zeros_like(x2)
            return [jnp.where(first_head, x2, zero), jnp.where(first_head, zero, x2)]

        @pl.when(j == 0)
        def _():
            dqt_ref[...] = jnp.zeros_like(dqt_ref)
            dcq_ref[...] = jnp.zeros_like(dcq_ref)

        k2 = k_ref[...]
        v2 = v_ref[...]
        kt_ref[...] = jnp.transpose(k2.astype(F32)).astype(BF16)
        first_rows = lax.broadcasted_iota(jnp.int32, (LANES, tq), 0) < HEAD_DIM
        for hb, (dk_acc, dv_acc, dc_acc, _, _, _, _, cs_ref) in enumerate(heads):
            dk_acc[...] = jnp.zeros_like(dk_acc)
            dv_acc[...] = jnp.zeros_like(dv_acc)
            dc_acc[...] = jnp.zeros_like(dc_acc)
            cs_ref[...] = jnp.transpose(jnp.broadcast_to(c_ref[hb, j], (LANES, tq)))

        def pair(i, masked):
            queries = pl.ds(pl.multiple_of(i * tq, tq), tq)
            q2 = q_ref[queries, :]
            do2 = do_ref[queries, :]
            q_of, do_of = split(q2), split(do2)
            for hb, (_, _, _, st_ref, dp_ref, _, _, _) in enumerate(heads):
                st_ref[...] = lax.dot_general(k2, q_of[hb], NT, preferred_element_type=F32)
                dp_ref[...] = lax.dot_general(v2, do_of[hb], NT, preferred_element_type=F32)
            dq_parts = []
            for hb, (dk_acc, dv_acc, dc_acc, st_ref, dp_ref, pt_ref, ds_ref, cs_ref) in enumerate(heads):
                bias_q = c_ref[hb, i] - lse_ref[hb, i]
                delta = dl_ref[hb, i]
                col_sum = jnp.zeros((SUBLANES, tq), F32)
                for r0 in range(0, tq, ATTN_STRIP):
                    rows = pl.ds(r0, ATTN_STRIP)
                    st = st_ref[rows, :] + (bias_q - jnp.concatenate([cs_ref[rows, :]] * lane_tiles, axis=1))
                    if masked:
                        keep = (lax.broadcasted_iota(jnp.int32, (ATTN_STRIP, tq), 1)
                                >= r0 + lax.broadcasted_iota(jnp.int32, (ATTN_STRIP, tq), 0))
                        st = jnp.where(keep, st, NEG_INF)
                    pt = jnp.exp(st)
                    dst = pt * (dp_ref[rows, :] - delta)
                    pt_ref[rows, :] = pt.astype(BF16)
                    ds_ref[rows, :] = dst.astype(BF16)
                    dc_acc[rows, :] += jnp.sum(dst, axis=1, keepdims=True)
                    for g0 in range(0, ATTN_STRIP, SUBLANES):
                        col_sum = col_sum + dst[g0:g0 + SUBLANES]
                dcq_ref[hb, i] += jnp.sum(col_sum, axis=0, keepdims=True)
                dv_acc[...] += jnp.dot(pt_ref[...], do2, preferred_element_type=F32)
                dk_acc[...] += jnp.dot(ds_ref[...], q2, preferred_element_type=F32)
                dq_parts.append(jnp.dot(kt_ref[...], ds_ref[...], preferred_element_type=F32))
            dqt_ref[i] += jnp.where(first_rows, dq_parts[0], dq_parts[1])

        def step(i, carry):
            pair(i, False)
            return carry

        pair(j, True)
        lax.fori_loop(j + 1, nq, step, 0)
        dk_ref[...] = jnp.where(first_head, heads[0][0][...], heads[1][0][...]).astype(BF16)
        dv_ref[...] = jnp.where(first_head, heads[0][1][...], heads[1][1][...]).astype(BF16)
        for hb in range(HB):
            dcs_ref[hb, 0] = -jnp.transpose(jnp.broadcast_to(heads[hb][2][...], (tq, LANES)))[0:1, :]

        @pl.when(j == nq - 1)
        def _():
            for i in range(nq):
                dqb_ref[pl.ds(i * tq, tq), :] = jnp.transpose(dqt_ref[i]).astype(BF16)

    per_head = [pltpu.VMEM((tq, LANES), F32), pltpu.VMEM((tq, LANES), F32), pltpu.VMEM((tq, 1), F32),
                pltpu.VMEM((tq, tq), F32), pltpu.VMEM((tq, tq), F32),
                pltpu.VMEM((tq, tq), BF16), pltpu.VMEM((tq, tq), BF16), pltpu.VMEM((tq, LANES), F32)]
    assert len(per_head) == n_scratch
    v_blocks = D // LANES
    whole = pl.BlockSpec((Lp, LANES), lambda p, j: (0, p))
    tile = pl.BlockSpec((tq, LANES), lambda p, j: (j, p))
    rows = pl.BlockSpec((HB, nq, 1, tq), lambda p, j: (p, 0, 0, 0))
    return pl.pallas_call(
        body, name=name, grid=(H // HB, nq),
        in_specs=[whole, whole, tile, pl.BlockSpec((tq, LANES), lambda p, j: (j, v_blocks + p)), rows, rows, rows],
        out_specs=[whole, tile, tile, pl.BlockSpec((HB, 1, 1, tq), lambda p, j: (p, j, 0, 0)), rows],
        out_shape=[jax.ShapeDtypeStruct((Lp, D), BF16), jax.ShapeDtypeStruct((Lp, D), BF16),
                   jax.ShapeDtypeStruct((Lp, D), BF16), jax.ShapeDtypeStruct((H, nq, 1, tq), F32),
                   jax.ShapeDtypeStruct((H, nq, 1, tq), F32)],
        scratch_shapes=[pltpu.VMEM((nq, LANES, tq), F32), pltpu.VMEM((LANES, tq), BF16)] + per_head * HB,
        compiler_params=_cparams("parallel", "arbitrary"),
    )(qb, dob, kvb, kvb, lse4, delta4, crow4)


def _remote(src, dst, send_sems, recv_sems, k, to):
    return pltpu.make_async_remote_copy(src_ref=src, dst_ref=dst, send_sem=send_sems.at[k], recv_sem=recv_sems.at[k],
                                        device_id=to, device_id_type=MESH)


def _place():
    x, y, c = lax.axis_index("x"), lax.axis_index("y"), lax.axis_index("c")
    other_chips = [(1 - x, y), (x, 1 - y), (1 - x, 1 - y)]
    return x, y, c, other_chips


def _all_gather_weights(wb, wf, name):
    Rb, C = wb.shape
    Rf = wf.shape[0]
    hb = Rb // 2

    def body(wb_ref, wf_ref, ob_ref, of_ref, send_sems, recv_sems):
        x, y, c, chips = _place()
        me = 2 * x + y
        sibling = (x, y, 1 - c)

        def half(chip, core):
            return ob_ref.at[chip, pl.ds(core * hb, hb), :]

        sent = []
        for j, (cx, cy) in enumerate(chips):
            sent.append(_remote(wb_ref.at[pl.ds(c * hb, hb), :], half(me, c), send_sems, recv_sems, j, (cx, cy, c)))
            sent.append(_remote(wf_ref, of_ref.at[me], send_sems, recv_sems, 3 + j, (cx, cy, c)))
        for cp in sent:
            cp.start()
        for j, (cx, cy) in enumerate(chips):
            chip = 2 * cx + cy
            _remote(half(chip, c), half(chip, c), send_sems, recv_sems, j, sibling).wait_recv()
            fwd = _remote(half(chip, c), half(chip, c), send_sems, recv_sems, 6 + j, sibling)
            fwd.start()
            sent.append(fwd)
        for j, (cx, cy) in enumerate(chips):
            chip = 2 * cx + cy
            _remote(wf_ref, of_ref.at[chip], send_sems, recv_sems, 3 + j, sibling).wait_recv()
            _remote(half(chip, 1 - c), half(chip, 1 - c), send_sems, recv_sems, 6 + j, sibling).wait_recv()
        for cp in sent:
            cp.wait_send()

    any_spec = pl.BlockSpec(memory_space=pl.ANY)
    return pl.pallas_call(
        body, name=name,
        in_specs=[any_spec, any_spec], out_specs=[any_spec, any_spec],
        out_shape=[jax.ShapeDtypeStruct((N_CHIPS, Rb, C), BF16), jax.ShapeDtypeStruct((N_CHIPS, Rf, C), F32)],
        scratch_shapes=[pltpu.SemaphoreType.DMA((9,)), pltpu.SemaphoreType.DMA((9,))],
    )(wb, wf)


def _half_of(ref, order, half):
    return ref.at[pl.ds(0, N_CHIPS), half] if order == "CH" else ref.at[half]


def _halves_to_sibling(grads, orders, name):
    n = len(grads)

    def body(*refs):
        g_refs, a_refs, (send_sems, recv_sems) = refs[:n], refs[n:2 * n], refs[2 * n:]
        x, y, c, _ = _place()
        copies = [_remote(_half_of(g, o, 1 - c), a, send_sems, recv_sems, k, (x, y, 1 - c))
                  for k, (g, a, o) in enumerate(zip(g_refs, a_refs, orders))]
        for cp in copies:
            cp.start()
        for cp in copies:
            cp.wait()

    any_spec = pl.BlockSpec(memory_space=pl.ANY)
    shapes = [g.shape[2:] for g in grads]
    return pl.pallas_call(
        body, name=name, in_specs=[any_spec] * n, out_specs=[any_spec] * n,
        out_shape=[jax.ShapeDtypeStruct((N_CHIPS,) + s, F32) for s in shapes],
        scratch_shapes=[pltpu.SemaphoreType.DMA((n,)), pltpu.SemaphoreType.DMA((n,))],
    )(*grads)


def _chip_partial(g, a, core, order, wire, name, *, tr):
    _, R, C = a.shape
    narrow = wire != F32

    def body(core_ref, g_ref, a_ref, *outs):
        p = g_ref[0, 0] + a_ref[0]
        outs[0][0] = p
        if narrow:
            outs[1][0] = p.astype(wire)

    if order == "CH":
        g_spec = pl.BlockSpec((1, 1, tr, C), lambda s, i, core_ref: (s, core_ref[0], i, 0))
    else:
        g_spec = pl.BlockSpec((1, 1, tr, C), lambda s, i, core_ref: (core_ref[0], s, i, 0))
    blk = pl.BlockSpec((1, tr, C), lambda s, i, core_ref: (s, i, 0))
    grid_spec = pltpu.PrefetchScalarGridSpec(
        num_scalar_prefetch=1, grid=(N_CHIPS, R // tr), in_specs=[g_spec, blk],
        out_specs=[blk, blk] if narrow else [blk])
    out_shape = [jax.ShapeDtypeStruct((N_CHIPS, R, C), F32)] + ([jax.ShapeDtypeStruct((N_CHIPS, R, C), wire)] if narrow else [])
    outs = pl.pallas_call(body, name=name, grid_spec=grid_spec, out_shape=out_shape,
                          compiler_params=_cparams("parallel", "parallel"))(core, g, a)
    return outs[0], outs[-1]


def _chip_exchange(parts, rep, name):
    n = len(parts)
    rr, C = rep.shape

    def body(*refs):
        p_refs, rep_ref = refs[:n], refs[n]
        land_refs, reps_ref = refs[n + 1:2 * n + 1], refs[2 * n + 1]
        send_sems, recv_sems, local_sem = refs[2 * n + 2:]
        x, y, c, chips = _place()
        me = 4 * x + 2 * y + c
        own = pltpu.make_async_copy(rep_ref, reps_ref.at[me], local_sem.at[0])
        own.start()
        sent = []
        for k, (p, land) in enumerate(zip(p_refs, land_refs)):
            for j, (cx, cy) in enumerate(chips):
                sent.append(_remote(p.at[2 * cx + cy], land.at[j], send_sems, recv_sems, 3 * k + j, (cx, cy, c)))
        for r in range(1, N_DEV):
            fx, fy, fc = (r >> 2) & 1, (r >> 1) & 1, r & 1
            sent.append(_remote(rep_ref, reps_ref.at[me], send_sems, recv_sems, 3 * n - 1 + r, (x ^ fx, y ^ fy, c ^ fc)))
        for cp in sent:
            cp.start()
        for k, (p, land) in enumerate(zip(p_refs, land_refs)):
            for j in range(3):
                _remote(p.at[0], land.at[j], send_sems, recv_sems, 3 * k + j, (x, y, c)).wait_recv()
        for r in range(1, N_DEV):
            fx, fy, fc = (r >> 2) & 1, (r >> 1) & 1, r & 1
            frm = 4 * (x ^ fx) + 2 * (y ^ fy) + (c ^ fc)
            _remote(rep_ref, reps_ref.at[frm], send_sems, recv_sems, 3 * n - 1 + r, (x, y, c)).wait_recv()
        for cp in sent:
            cp.wait_send()
        own.wait()

    any_spec = pl.BlockSpec(memory_space=pl.ANY)
    n_sems = 3 * n + N_DEV - 1
    return pl.pallas_call(
        body, name=name, in_specs=[any_spec] * (n + 1), out_specs=[any_spec] * (n + 1),
        out_shape=[jax.ShapeDtypeStruct((3,) + p.shape[1:], p.dtype) for p in parts]
        + [jax.ShapeDtypeStruct((N_DEV, rr, C), F32)],
        scratch_shapes=[pltpu.SemaphoreType.DMA((n_sems,)), pltpu.SemaphoreType.DMA((n_sems,)),
                        pltpu.SemaphoreType.DMA((1,))],
    )(*parts, rep)


def _adamw_math(w, g, m, v):
    m = ADAM_B1 * m + (1.0 - ADAM_B1) * g
    v = ADAM_B2 * v + (1.0 - ADAM_B2) * (g * g)
    m_hat = m / (1.0 - ADAM_B1 ** ADAM_STEP)
    v_hat = v / (1.0 - ADAM_B2 ** ADAM_STEP)
    delta = -ADAM_LR * (m_hat / (jnp.sqrt(v_hat) + ADAM_EPS) + ADAM_WD * w)
    return delta, m, v


def _adamw_owned(part, landed, w, m, v, place, name, *, tr):
    _, R, C = part.shape

    def body(place_ref, own_ref, land_ref, w_ref, m_ref, v_ref, g_ref, d_ref, mo_ref, vo_ref):
        g = own_ref[0]
        for s in range(3):
            g = g + land_ref[s].astype(F32)
        delta, m_new, v_new = _adamw_math(w_ref[0], g, m_ref[0], v_ref[0])
        g_ref[0] = g
        d_ref[0] = delta
        mo_ref[0] = m_new
        vo_ref[0] = v_new

    half = pl.BlockSpec((1, tr, C), lambda i, place_ref: (place_ref[0], i, 0))
    grid_spec = pltpu.PrefetchScalarGridSpec(
        num_scalar_prefetch=1, grid=(R // tr,),
        in_specs=[pl.BlockSpec((1, tr, C), lambda i, place_ref: (place_ref[1], i, 0)),
                  pl.BlockSpec((3, tr, C), lambda i, place_ref: (0, i, 0)), half, half, half],
        out_specs=[half] * 4)
    return pl.pallas_call(
        body, name=name, grid_spec=grid_spec, out_shape=[jax.ShapeDtypeStruct((2, R, C), F32)] * 4,
        compiler_params=_cparams("parallel"),
    )(place, part, landed, w, m, v)


def _join_halves(bufs, name):
    n = len(bufs)

    def body(*refs):
        out_refs, (send_sems, recv_sems) = refs[n:2 * n], refs[2 * n:]
        x, y, c, _ = _place()
        copies = [_remote(o.at[c], o.at[c], send_sems, recv_sems, k, (x, y, 1 - c)) for k, o in enumerate(out_refs)]
        for cp in copies:
            cp.start()
        for k, o in enumerate(out_refs):
            _remote(o.at[c], o.at[1 - c], send_sems, recv_sems, k, (x, y, 1 - c)).wait_recv()
        for cp in copies:
            cp.wait_send()

    any_spec = pl.BlockSpec(memory_space=pl.ANY)
    return pl.pallas_call(
        body, name=name, in_specs=[any_spec] * n, out_specs=[any_spec] * n,
        out_shape=[jax.ShapeDtypeStruct(b.shape, b.dtype) for b in bufs],
        input_output_aliases={k: k for k in range(n)},
        scratch_shapes=[pltpu.SemaphoreType.DMA((n,)), pltpu.SemaphoreType.DMA((n,))],
    )(*bufs)


def _sum_adamw(own, landed, w, m, v, name, *, tr):
    n = landed.shape[0]
    hr, C = own.shape

    def body(own_ref, land_ref, w_ref, m_ref, v_ref, o_ref):
        g = own_ref[...]
        for s in range(n):
            g = g + land_ref[s]
        delta, m_new, v_new = _adamw_math(w_ref[...], g, m_ref[...], v_ref[...])
        o_ref[0] = g
        o_ref[1] = delta
        o_ref[2] = m_new
        o_ref[3] = v_new

    blk = pl.BlockSpec((tr, C), lambda i: (i, 0))
    return pl.pallas_call(
        body, name=name, grid=(hr // tr,),
        in_specs=[blk, pl.BlockSpec((n, tr, C), lambda i: (0, i, 0)), blk, blk, blk],
        out_specs=pl.BlockSpec((4, tr, C), lambda i: (0, i, 0)),
        out_shape=jax.ShapeDtypeStruct((4, hr, C), F32), compiler_params=_cparams("parallel"),
    )(own, landed, w, m, v)


def _rows_of(shape):
    n = 1
    for d in shape:
        n *= d
    return -(-n // PACK_COLS)


def _pack(arrays, total_rows, dtype):
    parts, used = [], 0
    for a in arrays:
        flat = a.reshape(-1).astype(dtype)
        fill = _rows_of(a.shape) * PACK_COLS - flat.shape[0]
        parts += [flat] + ([jnp.zeros((fill,), dtype)] if fill else [])
        used += _rows_of(a.shape)
    if total_rows > used:
        parts.append(jnp.zeros(((total_rows - used) * PACK_COLS,), dtype))
    return jnp.concatenate(parts).reshape(total_rows, PACK_COLS)


def _unpack(buf, shapes):
    lead = buf.shape[:-2]
    out, r = [], 0
    for shp in shapes:
        n = 1
        for d in shp:
            n *= d
        rows = _rows_of(shp)
        piece = buf[..., r:r + rows, :].reshape(lead + (rows * PACK_COLS,))[..., :n]
        out.append(piece.reshape(lead + tuple(shp)))
        r += rows
    return out


def _join_shards(stacked, axis):
    return jnp.concatenate([stacked[s] for s in range(N_CHIPS)], axis=axis)


def _shard_of(full, axis, chip):
    width = full.shape[axis] // N_CHIPS
    return lax.slice_in_dim(full, chip * width, (chip + 1) * width, axis=axis)


def _local_step(h0, tgt, W, *, seq, tm):
    Lp, D = h0.shape
    H = D // HEAD_DIM
    F2 = W["ffn_w_in"].shape[-1]
    F = F2 // 2
    te = tm // 2
    nq = Lp // tm
    cap = 1408
    tD, tF, tF2 = _pick(D, cap), _pick(F, cap), _pick(F2, cap)
    t2D = _pick(2 * D, cap)
    t2Dc, tF2c = _pick(2 * D // N_CHIPS, cap), _pick(F2 // N_CHIPS, cap)
    tcn = _pick(F, cap)

    def vec(a):
        return a.reshape(1, -1)

    ln_g, ln_b = W["ln_g"], W["ln_b"]
    wf_pad = jnp.pad(W["w_f"], ((0, 0), (0, LANES - H)))
    bf_pad = jnp.pad(W["b_f"], (0, LANES - H)).reshape(1, LANES)

    def ffn_fwd(h, hb, l, tag):
        u = _mm(hb, W["ffn_w_in"][l], "nn", F32, f"ffn{tag}_up", tm=tm, tn=tF2, tk=tD)
        act = _conv_glu_fwd(u, W["ffn_conv_w"][l], vec(W["ffn_conv_b"][l]), f"ffn{tag}_glu", tm=te, tn=tcn)
        normed = _mm_ln(act, W["ffn_w_out"][l], "nn", f"ffn{tag}_down_ln", tm=tm, tk=tF, forward=True,
                        rows=[h], vecs=[vec(ln_g[l, 1]), vec(ln_b[l, 1])])
        return u, act, normed

    def ffn_bwd(dz, dzb, hb, u, act, l, tag, dw_in_acc, dw_out_acc, xh_in, rs_in, g_in):
        dact = _mm(dzb, W["ffn_w_out"][l], "nt", F32, f"ffn{tag}_dact", tm=tm, tn=tF, tk=tD)
        dw_out = _mm(act, dzb, "tn", F32, f"ffn{tag}_dwout", tm=tF, tn=tD, tk=tm, layer=l, into=dw_out_acc)
        du, dwa, dwg, dba, dbg = _conv_glu_bwd(u, dact, W["ffn_conv_w"][l], vec(W["ffn_conv_b"][l]),
                                               f"ffn{tag}_dglu", tm=te, tn=tcn)
        dcw = jnp.concatenate([dwa, dwg], axis=1)
        dcb = jnp.concatenate([dba, dbg], axis=1)
        prev = _mm_ln(du, W["ffn_w_in"][l], "nt", f"ffn{tag}_dh_ln", tm=tm, tk=tF, forward=False,
                      rows=[dz, xh_in, rs_in], vecs=[g_in], a_halves=True)
        dw_in = _mm(hb, du, "tn", F32, f"ffn{tag}_dwin", tm=tD, tn=tF2c, tk=tm, chips=True, layer=l, into=dw_in_acc,
                    b_halves=True)
        return prev, dw_in, dw_out, dcw, dcb[0]

    diffb, mixpre, h1, h1b, xh1, rs1 = _pool_ln_fwd(h0, W["pool_w"][0], W["pool_scale"], vec(ln_g[0, 0]),
                                                    vec(ln_b[0, 0]), "pool_ln_fwd", tm=te)
    u0, act0, (h2, h2b, xh2, rs2) = ffn_fwd(h1, h1b, 0, "0")

    kvb = _mm(h2b, W["w_kv"], "nn", BF16, "kv_proj", tm=tm, tn=t2D, tk=tD)
    qb = _mm(h2b, W["w_q"][0], "nn", BF16, "q_proj", tm=tm, tn=tD, tk=tD, scale=HEAD_DIM ** -0.5)
    pre = _mm(h2b, wf_pad, "nn", F32, "f_proj", tm=tm, tn=LANES, tk=tD)
    c, kx, qx = _logf_cumsum(pre, bf_pad, "logf_cumsum", tm=tm, n_heads=H, width=D)

    crow4 = c[:, :H].T.reshape(H, nq, 1, tm)
    o_tok, ob, lse4 = _attn_fwd(qb, kvb, kx, qx, "attn_fwd", tq=tm)
    h3, h3b, xh3, rs3 = _mm_ln(ob, W["w_o"][0], "nn", "o_proj_ln", tm=tm, tk=tD, forward=True,
                               rows=[h2], vecs=[vec(ln_g[1, 0]), vec(ln_b[1, 0])])
    u1, act1, (h4, _, xh4, rs4) = ffn_fwd(h3, h3b, 1, "1")
    dy, loss = _loss_head(h4, tgt, "loss_head", tm=te, row_lo=N_META, row_hi=N_META + seq)

    dz4, dz4b, dg11, db11 = _ln_bwd([dy], [1.0], xh4, rs4, vec(ln_g[1, 1]), "ln11_bwd", tm=te)
    (dz3, dz3b, dg10, db10), dw_in, dw_out, dcw1, dcb1 = ffn_bwd(dz4, dz4b, h3b, u1, act1, 1, "1", None, None,
                                                                 xh3, rs3, vec(ln_g[1, 0]))

    dob = _mm(dz3b, W["w_o"][0], "nt", BF16, "o_proj_dx", tm=tm, tn=tD, tk=tD)
    dw_o = _mm(ob, dz3b, "tn", F32, "o_proj_dw", tm=tD, tn=tD, tk=tm)
    delta = _attn_delta(dob, o_tok, "attn_delta", tm=te, n_heads=H)
    dqb, dkb, dvb, dcs, dcq = _attn_bwd(qb, dob, kvb, lse4, delta[:, :H].T.reshape(H, nq, 1, tm), crow4,
                                        "attn_bwd", tq=tm)
    dc_keys = jnp.pad(dcs.reshape(H, Lp).T, ((0, 0), (0, LANES - H)))
    dc_queries = jnp.pad(dcq.reshape(H, Lp).T, ((0, 0), (0, LANES - H)))
    dpreb, dbf = _logf_bwd(dc_keys, dc_queries, pre, bf_pad, "logf_bwd", tm=tm)

    qs = HEAD_DIM ** -0.5
    dw_q = _mm(h2b, dqb, "tn", F32, "q_proj_dw", tm=tD, tn=tD, tk=tm, scale=qs)
    dw_kv = _mm(h2b, dkb, "tn", F32, "k_proj_dw", tm=tD, tn=t2Dc, tk=tm, chips=(0, N_CHIPS // 2))
    dw_kv = _mm(h2b, dvb, "tn", F32, "v_proj_dw", tm=tD, tn=t2Dc, tk=tm, chips=(N_CHIPS // 2, N_CHIPS // 2), into=dw_kv)
    dw_f = _mm(h2b, dpreb, "tn", F32, "f_proj_dw", tm=tD, tn=LANES, tk=tm)[:, :H]
    dh2 = _mm(dqb, W["w_q"][0], "nt", F32, "q_proj_dx", tm=tm, tn=tD, tk=tD, scale=qs)
    dh2 = _mm(dkb, W["w_kv"][:, :D], "nt", F32, "k_proj_dx", tm=tm, tn=tD, tk=tD, add=dh2)
    dh2 = _mm(dvb, W["w_kv"][:, D:], "nt", F32, "v_proj_dx", tm=tm, tn=tD, tk=tD, add=dh2)
    dz2, dz2b, dg01, db01 = _mm_ln(dpreb, wf_pad, "nt", "f_proj_dx_ln", tm=tm, tk=LANES, forward=False,
                                   rows=[dz3, xh2, rs2], vecs=[vec(ln_g[0, 1])], add=dh2)

    (dz1, _, dg00, db00), dw_in, dw_out, dcw0, dcb0 = ffn_bwd(dz2, dz2b, h1b, u0, act0, 0, "0", dw_in, dw_out,
                                                              xh1, rs1, vec(ln_g[0, 0]))
    dh0, dmb, dscale = _pool_bwd(dz1, mixpre, W["pool_w"][0], W["pool_scale"], "pool_bwd", tm=te)
    dw_pool = _pool_dw(diffb, dmb, "pool_dw", tk=tm)

    grads = {
        "meta": dh0[:N_META],
        "pool_w": dw_pool[None],
        "pool_scale": dscale,
        "w_kv": dw_kv,
        "w_f": dw_f,
        "b_f": dbf[0, :H],
        "w_q": dw_q[None],
        "w_o": dw_o[None],
        "ffn_w_in": dw_in,
        "ffn_conv_w": jnp.stack([dcw0, dcw1]),
        "ffn_conv_b": jnp.stack([dcb0, dcb1]),
        "ffn_w_out": dw_out,
        "ln_g": jnp.stack([jnp.stack([dg00[0], dg01[0]]), jnp.stack([dg10[0], dg11[0]])]),
        "ln_b": jnp.stack([jnp.stack([db00[0], db01[0]]), jnp.stack([db10[0], db11[0]])]),
    }
    return loss, dh0, grads


def _row_block(rows, cols):
    best = SUBLANES
    for t in range(SUBLANES, rows + 1, SUBLANES):
        if rows % t == 0 and t * cols * 4 <= ELEMENTWISE_BLOCK_BYTES:
            best = t
    return best


def _row_tile(length):
    return 640 if length >= 4096 else 128


def kernel(x, meta, pool_w, pool_scale, w_kv, w_f, b_f, w_q, w_o, ffn_w_in, ffn_conv_w, ffn_conv_b, ffn_w_out, ln_g, ln_b, loss_target, m_meta, m_pool_w, m_pool_scale, m_w_kv, m_w_f, m_b_f, m_w_q, m_w_o, m_ffn_w_in, m_ffn_conv_w, m_ffn_conv_b, m_ffn_w_out, m_ln_g, m_ln_b, v_meta, v_pool_w, v_pool_scale, v_w_kv, v_w_f, v_b_f, v_w_q, v_w_o, v_ffn_w_in, v_ffn_conv_w, v_ffn_conv_b, v_ffn_w_out, v_ln_g, v_ln_b):
    weights = dict(meta=meta, pool_w=pool_w, pool_scale=pool_scale, w_kv=w_kv, w_f=w_f, b_f=b_f, w_q=w_q, w_o=w_o,
                   ffn_w_in=ffn_w_in, ffn_conv_w=ffn_conv_w, ffn_conv_b=ffn_conv_b, ffn_w_out=ffn_w_out,
                   ln_g=ln_g, ln_b=ln_b)
    mom1 = dict(meta=m_meta, pool_w=m_pool_w, pool_scale=m_pool_scale, w_kv=m_w_kv, w_f=m_w_f, b_f=m_b_f, w_q=m_w_q,
                w_o=m_w_o, ffn_w_in=m_ffn_w_in, ffn_conv_w=m_ffn_conv_w, ffn_conv_b=m_ffn_conv_b,
                ffn_w_out=m_ffn_w_out, ln_g=m_ln_g, ln_b=m_ln_b)
    mom2 = dict(meta=v_meta, pool_w=v_pool_w, pool_scale=v_pool_scale, w_kv=v_w_kv, w_f=v_w_f, b_f=v_b_f, w_q=v_w_q,
                w_o=v_w_o, ffn_w_in=v_ffn_w_in, ffn_conv_w=v_ffn_conv_w, ffn_conv_b=v_ffn_conv_b,
                ffn_w_out=v_ffn_w_out, ln_g=v_ln_g, ln_b=v_ln_b)
    _, seq, D = x.shape
    L = N_META + seq
    tm = _row_tile(L)
    Lp = _round_up(L, tm)
    c_idx = lax.axis_index("c")
    chip = 2 * lax.axis_index("x") + lax.axis_index("y")

    shard_shapes = {n: weights[n].shape for n in SHARDED}
    rows_b = _round_up(sum(_rows_of(shard_shapes[n]) for n in MATMUL_WEIGHTS), 32)
    rows_f = _round_up(sum(_rows_of(shard_shapes[n]) for n in VECTOR_WEIGHTS), SUBLANES)
    wb = _pack([weights[n] for n in MATMUL_WEIGHTS], rows_b, BF16)
    wf = _pack([weights[n] for n in VECTOR_WEIGHTS], rows_f, F32)
    gb, gf = _all_gather_weights(wb, wf, "weights_all_gather")
    gb = lax.dynamic_update_index_in_dim(gb, wb, chip, axis=0)
    gf = lax.dynamic_update_index_in_dim(gf, wf, chip, axis=0)
    full = {}
    for names, buf in ((MATMUL_WEIGHTS, gb), (VECTOR_WEIGHTS, gf)):
        for n, stacked in zip(names, _unpack(buf, [shard_shapes[n] for n in names])):
            full[n] = _join_shards(stacked, SHARD_AXIS[n])
    full["b_f"] = b_f
    full["ffn_conv_b"] = ffn_conv_b

    pad = jnp.zeros((Lp - L, D), F32)
    h0 = jnp.concatenate([full["meta"], x[0], pad], axis=0)
    tgt = jnp.concatenate([jnp.zeros((N_META, D), F32), loss_target[0], pad], axis=0)
    loss, dh0, grads = _local_step(h0, tgt, full, seq=seq, tm=tm)
    loss = lax.psum(loss[0, 0], AXES)
    grad_x = dh0[N_META:L][None]

    core = c_idx.astype(jnp.int32).reshape(1)
    place = jnp.stack([c_idx, chip]).astype(jnp.int32)
    small_shapes = [shard_shapes[n] for n in SMALL_SHARDED]
    rows_s = _round_up(sum(_rows_of(s) for s in small_shapes), 2 * LANES)

    def packed_small(d):
        return _pack([d[n] for n in SMALL_SHARDED], rows_s, F32).reshape(2, rows_s // 2, PACK_COLS)

    names, orders, wires, g_views, wmv = [], [], [], [], []
    for n, order in BIG_SHARDED:
        shp = shard_shapes[n]
        C = shp[-1]
        R = weights[n].size // C // 2
        lead = (N_CHIPS, 2) if order == "CH" else (2, N_CHIPS)
        names.append(n)
        orders.append(order)
        wires.append(BF16)
        g_views.append(grads[n].reshape(lead + (R, C)))
        wmv.append([d[n].reshape(2, R, C) for d in (weights, mom1, mom2)])
    names.append("small")
    orders.append("CH")
    wires.append(F32)
    g_views.append(jnp.stack([_pack([_shard_of(grads[n], SHARD_AXIS[n], s) for n in SMALL_SHARDED], rows_s, F32)
                              for s in range(N_CHIPS)]).reshape(N_CHIPS, 2, rows_s // 2, PACK_COLS))
    wmv.append([packed_small(d) for d in (weights, mom1, mom2)])

    from_sibling = _halves_to_sibling(g_views, orders, "grads_to_sibling")
    parts, on_wire = [], []
    for n, order, wire, g, a in zip(names, orders, wires, g_views, from_sibling):
        p, pw = _chip_partial(g, a, core, order, wire, f"chip_sum_{n}", tr=_row_block(a.shape[1], a.shape[2]))
        parts.append(p)
        on_wire.append(pw)

    rep_shapes = [weights[n].shape for n in REPLICATED]
    rows_r = _round_up(sum(_rows_of(s) for s in rep_shapes), SUBLANES)
    rep = _pack([grads[n] for n in REPLICATED], rows_r, F32)
    *landed, reps = _chip_exchange(on_wire, rep, "grads_chip_exchange")

    halves = []
    for n, p, b, (w_, m_, v_) in zip(names, parts, landed, wmv):
        halves += _adamw_owned(p, b, w_, m_, v_, place, f"adamw_{n}", tr=_row_block(p.shape[1], p.shape[2]))
    joined = _join_halves(halves, "results_to_sibling")
    out = {}
    for k, n in enumerate(names[:-1]):
        out[n] = [a.reshape(shard_shapes[n]) for a in joined[4 * k:4 * k + 4]]
    small_out = [_unpack(a.reshape(rows_s, PACK_COLS), small_shapes) for a in joined[-4:]]
    for k, n in enumerate(SMALL_SHARDED):
        out[n] = [small_out[kind][k] for kind in range(4)]

    def packr(d):
        return _pack([d[n] for n in REPLICATED], rows_r, F32)

    res_r = _sum_adamw(reps[0], reps[1:], packr(weights), packr(mom1), packr(mom2), "adamw_replicated", tr=rows_r)
    rep_out = _unpack(res_r, rep_shapes)

    out.update({n: a for n, a in zip(REPLICATED, rep_out)})
    result = [loss, grad_x]
    for k in range(4):
        result += [out[n][k] for n in WEIGHT_ORDER]
    return tuple(result)
```

```python
import functools

import jax
import jax.numpy as jnp
from jax import lax
from jax.experimental import pallas as pl
from jax.experimental.pallas import tpu as pltpu

N_META = 16
POOL_WINDOWS = (2, 4, 8, 16)
MAX_WINDOW = max(POOL_WINDOWS)
N_GROUPS = len(POOL_WINDOWS)
HEAD_DIM = 64
DEPTH = 2
CONV_WIDTH = 3
ALPHA = (2.0 * DEPTH) ** 0.25
LN_EPS = 1e-5
NEG_INF = -1e30
ADAM_LR = 0.001
ADAM_B1 = 0.9
ADAM_B2 = 0.999
ADAM_EPS = 1e-08
ADAM_WD = 0.01
ADAM_STEP = 10

F32 = jnp.float32
BF16 = jnp.bfloat16
ATTN_FWD_PAIRS = 2
BIAS_SLOTS = 6
ATTN_STRIP = 32
GLU_STRIP = 16
LANES = 128
SUBLANES = 8
PACK_COLS = 1024
VMEM_LIMIT = 56 * 1024 * 1024
AXES = ("x", "y", "c")
MESH = pl.DeviceIdType.MESH

NN = (((1,), (0,)), ((), ()))
NT = (((1,), (1,)), ((), ()))
TN = (((0,), (0,)), ((), ()))

SHARD_AXIS = {"meta": 1, "pool_w": 2, "pool_scale": 1, "w_kv": 1, "w_f": 0, "w_q": 1, "w_o": 1,
              "ffn_w_in": 2, "ffn_conv_w": 2, "ffn_w_out": 1, "ln_g": 2, "ln_b": 2}
SHARDED = ("meta", "pool_w", "pool_scale", "w_kv", "w_f", "w_q", "w_o", "ffn_w_in", "ffn_conv_w",
           "ffn_w_out", "ln_g", "ln_b")
REPLICATED = ("b_f", "ffn_conv_b")
MATMUL_WEIGHTS = ("pool_w", "w_kv", "w_f", "w_q", "w_o", "ffn_w_in", "ffn_w_out")
VECTOR_WEIGHTS = ("meta", "pool_scale", "ffn_conv_w", "ln_g", "ln_b")
WEIGHT_ORDER = ("meta", "pool_w", "pool_scale", "w_kv", "w_f", "b_f", "w_q", "w_o", "ffn_w_in",
                "ffn_conv_w", "ffn_conv_b", "ffn_w_out", "ln_g", "ln_b")
BIG_SHARDED = (("w_kv", "CH"), ("w_q", "CH"), ("w_o", "CH"), ("ffn_w_in", "HC"), ("ffn_w_out", "HC"))
SMALL_SHARDED = ("meta", "pool_w", "pool_scale", "w_f", "ffn_conv_w", "ln_g", "ln_b")
ELEMENTWISE_BLOCK_BYTES = 3 * 512 * 1024
N_CHIPS = 4
N_DEV = 8


def _cparams(*sem):
    return pltpu.CompilerParams(dimension_semantics=sem, vmem_limit_bytes=VMEM_LIMIT)


def _round_up(n, m):
    return (n + m - 1) // m * m


def _pick(n, cap):
    if n <= cap:
        return n
    best = 0
    for t in range(LANES, cap + 1, LANES):
        if n % t == 0:
            best = t
    assert best, (n, cap)
    return best


def _mm(a, b, mode, out_dtype, name, *, tm, tn, tk, scale=None, add=None, chips=False, layer=None, into=None,
        b_halves=False, head_dots_with=None):
    if mode == "nn":
        (M, K), N = a.shape, b.shape[1]
    elif mode == "nt":
        (M, K), N = a.shape, b.shape[0]
    elif b_halves:
        (K, M), N = a.shape, 2 * b.shape[2]
    else:
        (K, M), N = a.shape, b.shape[1]
    assert M % tm == 0 and N % tn == 0 and K % tk == 0, (name, M, N, K, tm, tn, tk)
    nk = K // tk
    dn = {"nn": NN, "nt": NT, "tn": TN}[mode]
    has_add = add is not None
    has_into = into is not None
    has_dots = head_dots_with is not None
    assert not (has_add and (chips or layer is not None))
    assert not has_dots or (tn == N and not (chips or layer is not None or has_into))

    def body(*refs):
        a_ref, b_ref = refs[0], refs[1]
        add_ref = refs[2] if has_add else None
        with_ref = refs[2 + has_add + has_into] if has_dots else None
        o_ref = refs[2 + has_add + has_into + has_dots]
        dots_ref = refs[3 + has_add + has_into + has_dots] if has_dots else None
        acc_ref = refs[-1] if nk > 1 else None
        k = pl.program_id(2)
        part = lax.dot_general(a_ref[...], b_ref[0] if b_halves else b_ref[...], dn, preferred_element_type=F32)

        def finish(r):
            if scale is not None:
                r = r * scale
            if has_add:
                r = r + add_ref[...]
            out = r.astype(out_dtype)
            o_ref[...] = out.reshape(o_ref.shape)
            if has_dots:
                sel = (lax.broadcasted_iota(jnp.int32, (N, LANES), 0) // HEAD_DIM
                       == lax.broadcasted_iota(jnp.int32, (N, LANES), 1)).astype(F32)
                dots_ref[...] = jnp.dot(out.astype(F32) * with_ref[...], sel, precision=lax.Precision.HIGHEST,
                                        preferred_element_type=F32)

        if nk == 1:
            finish(part)
        else:
            @pl.when(k == 0)
            def _():
                acc_ref[...] = part

            @pl.when(k > 0)
            def _():
                acc_ref[...] += part

            @pl.when(k == nk - 1)
            def _():
                finish(acc_ref[...])

    if mode == "nn":
        a_spec = pl.BlockSpec((tm, tk), lambda j, i, k: (i, k))
        b_spec = pl.BlockSpec((tk, tn), lambda j, i, k: (k, j))
    elif mode == "nt":
        a_spec = pl.BlockSpec((tm, tk), lambda j, i, k: (i, k))
        b_spec = pl.BlockSpec((tn, tk), lambda j, i, k: (j, k))
    else:
        a_spec = pl.BlockSpec((tk, tm), lambda j, i, k: (k, i))
        b_spec = pl.BlockSpec((tk, tn), lambda j, i, k: (k, j))
        if b_halves:
            per_half = N // 2 // tn
            b_spec = pl.BlockSpec((1, tk, tn), lambda j, i, k: (j // per_half, k, j % per_half))
    out_dims, blk = (M, N), (tm, tn)
    if chips:
        base, count = (0, N_CHIPS) if chips is True else chips
        per_chip = N // count // tn
        assert per_chip * tn * count == N, (name, N, tn)
        out_dims, blk = (N_CHIPS, M, N // count), (1, tm, tn)
        where = lambda j, i: (base + j // per_chip, i, j % per_chip)
    else:
        where = lambda j, i: (i, j)
    if layer is not None:
        out_dims, blk = (DEPTH,) + out_dims, (1,) + blk
        o_spec = pl.BlockSpec(blk, lambda j, i, k: (layer,) + where(j, i))
    else:
        o_spec = pl.BlockSpec(blk, lambda j, i, k: where(j, i))
    in_specs = ([a_spec, b_spec] + ([o_spec] if has_add else []) + ([pl.BlockSpec(memory_space=pl.ANY)] if has_into else [])
                + ([o_spec] if has_dots else []))
    args = (a, b) + ((add,) if has_add else ()) + ((into,) if has_into else ()) + ((head_dots_with,) if has_dots else ())
    out_specs, out_shape = o_spec, jax.ShapeDtypeStruct(out_dims, out_dtype)
    if has_dots:
        out_specs = [o_spec, pl.BlockSpec((tm, LANES), lambda j, i, k: (i, 0))]
        out_shape = [out_shape, jax.ShapeDtypeStruct((M, LANES), F32)]
    return pl.pallas_call(
        body, name=name, grid=(N // tn, M // tm, nk),
        in_specs=in_specs, out_specs=out_specs,
        out_shape=out_shape,
        input_output_aliases={2 + has_add: 0} if has_into else {},
        scratch_shapes=[pltpu.VMEM((tm, tn), F32)] if nk > 1 else [],
        compiler_params=_cparams("parallel", "parallel", "arbitrary"),
    )(*args)


def _ln_math(z, g, b):
    mu = jnp.mean(z, axis=-1, keepdims=True)
    zc = z - mu
    var = jnp.mean(zc * zc, axis=-1, keepdims=True)
    rstd = lax.rsqrt(var + LN_EPS)
    xh = zc * rstd
    return xh * g + b, xh, rstd


def _mm_ln(a, b, mode, name, *, tm, tk, forward, rows, vecs, scale=None, add=None, a_halves=False):
    assert mode in ("nn", "nt")
    M, K = (a.shape[1], 2 * a.shape[2]) if a_halves else a.shape
    N = b.shape[1] if mode == "nn" else b.shape[0]
    assert M % tm == 0 and K % tk == 0, (name, M, K, tm, tk)
    nk = K // tk
    ni = M // tm
    dn = {"nn": NN, "nt": NT}[mode]
    has_add = add is not None
    n_in = 2 + has_add + len(rows) + len(vecs)

    def body(*refs):
        a_ref, b_ref = refs[0], refs[1]
        add_ref = refs[2] if has_add else None
        row_refs = refs[2 + has_add:2 + has_add + len(rows)]
        vec_refs = refs[2 + has_add + len(rows):n_in]
        outs = refs[n_in:n_in + 4]
        acc_ref = refs[-1] if nk > 1 else None
        i, k = pl.program_id(0), pl.program_id(1)
        part = lax.dot_general(a_ref[0] if a_halves else a_ref[...], b_ref[...], dn, preferred_element_type=F32)

        def finish(y):
            if scale is not None:
                y = y * scale
            if has_add:
                y = y + add_ref[...]
            if forward:
                h, xh, rstd = _ln_math(ALPHA * row_refs[0][...] + y, vec_refs[0][...], vec_refs[1][...])
                outs[0][...] = h
                outs[1][...] = h.astype(BF16)
                outs[2][...] = xh
                outs[3][...] = rstd
            else:
                dy = ALPHA * row_refs[0][...] + y
                x = row_refs[1][...]
                dxh = dy * vec_refs[0][...]
                m1 = jnp.mean(dxh, axis=-1, keepdims=True)
                m2 = jnp.mean(dxh * x, axis=-1, keepdims=True)
                dz = row_refs[2][...] * (dxh - m1 - x * m2)
                outs[0][...] = dz
                outs[1][...] = dz.astype(BF16)

                @pl.when(i == 0)
                def _():
                    outs[2][...] = jnp.zeros_like(outs[2])
                    outs[3][...] = jnp.zeros_like(outs[3])

                outs[2][...] += jnp.sum(dy * x, axis=0, keepdims=True)
                outs[3][...] += jnp.sum(dy, axis=0, keepdims=True)

        if nk == 1:
            finish(part)
        else:
            @pl.when(k == 0)
            def _():
                acc_ref[...] = part

            @pl.when(k > 0)
            def _():
                acc_ref[...] += part

            @pl.when(k == nk - 1)
            def _():
                finish(acc_ref[...])

    a_spec = pl.BlockSpec((tm, tk), lambda i, k: (i, k))
    if a_halves:
        per_half = nk // 2
        a_spec = pl.BlockSpec((1, tm, tk), lambda i, k: (k // per_half, i, k % per_half))
    b_spec = pl.BlockSpec((tk, N), lambda i, k: (k, 0)) if mode == "nn" else pl.BlockSpec((N, tk), lambda i, k: (0, k))
    row = pl.BlockSpec((tm, N), lambda i, k: (i, 0))
    col = pl.BlockSpec((tm, 1), lambda i, k: (i, 0))
    vec = pl.BlockSpec((1, N), lambda i, k: (0, 0))
    row_specs = [row if r.shape[1] == N else col for r in rows]
    if forward:
        out_specs = [row, row, row, col]
        out_shape = [jax.ShapeDtypeStruct((M, N), F32), jax.ShapeDtypeStruct((M, N), BF16),
                     jax.ShapeDtypeStruct((M, N), F32), jax.ShapeDtypeStruct((M, 1), F32)]
    else:
        out_specs = [row, row, vec, vec]
        out_shape = [jax.ShapeDtypeStruct((M, N), F32), jax.ShapeDtypeStruct((M, N), BF16),
                     jax.ShapeDtypeStruct((1, N), F32), jax.ShapeDtypeStruct((1, N), F32)]
    args = (a, b) + ((add,) if has_add else ()) + tuple(rows) + tuple(vecs)
    return pl.pallas_call(
        body, name=name, grid=(ni, nk),
        in_specs=[a_spec, b_spec] + ([row] if has_add else []) + row_specs + [vec] * len(vecs),
        out_specs=out_specs, out_shape=out_shape,
        scratch_shapes=[pltpu.VMEM((tm, N), F32)] if nk > 1 else [],
        compiler_params=_cparams("parallel" if forward else "arbitrary", "arbitrary"),
    )(*args)


def _pool_ln_fwd(h0, pw, ps, g, b, name, *, tm):
    Lp, D = h0.shape
    G = D // N_GROUPS
    halo_blocks = tm // MAX_WINDOW

    def body(x_ref, halo_ref, pw_ref, ps_ref, g_ref, b_ref,
             diff_ref, mix_ref, h_ref, hb_ref, xh_ref, rs_ref, ext_ref):
        i = pl.program_id(0)
        ext_ref[0:MAX_WINDOW, :] = jnp.where(i == 0, 0.0, halo_ref[...])
        ext_ref[MAX_WINDOW:MAX_WINDOW + tm, :] = x_ref[...]
        t1 = (i * tm + 1 + lax.broadcasted_iota(jnp.int32, (tm, 1), 0)).astype(F32)
        for gi, w in enumerate(POOL_WINDOWS):
            lo, hi = gi * G, (gi + 1) * G
            xg = x_ref[:, lo:hi]
            win = xg
            for j in range(1, w):
                win = win + ext_ref[MAX_WINDOW - j:MAX_WINDOW - j + tm, lo:hi]
            d = (win / jnp.minimum(t1, float(w)) - xg).astype(BF16)
            diff_ref[:, lo:hi] = d
            mix_ref[:, lo:hi] = jnp.dot(d, pw_ref[gi], preferred_element_type=F32)
        z = ALPHA * x_ref[...] + mix_ref[...] * ps_ref[...]
        h, xh, rstd = _ln_math(z, g_ref[...], b_ref[...])
        h_ref[...] = h
        hb_ref[...] = h.astype(BF16)
        xh_ref[...] = xh
        rs_ref[...] = rstd

    row = pl.BlockSpec((tm, D), lambda i: (i, 0))
    vec = pl.BlockSpec((1, D), lambda i: (0, 0))
    return pl.pallas_call(
        body, name=name, grid=(Lp // tm,),
        in_specs=[row,
                  pl.BlockSpec((MAX_WINDOW, D), lambda i: (jnp.maximum(i * halo_blocks - 1, 0), 0)),
                  pl.BlockSpec((N_GROUPS, G, G), lambda i: (0, 0, 0)), vec, vec, vec],
        out_specs=[row, row, row, row, row, pl.BlockSpec((tm, 1), lambda i: (i, 0))],
        out_shape=[jax.ShapeDtypeStruct((Lp, D), BF16), jax.ShapeDtypeStruct((Lp, D), F32),
                   jax.ShapeDtypeStruct((Lp, D), F32), jax.ShapeDtypeStruct((Lp, D), BF16),
                   jax.ShapeDtypeStruct((Lp, D), F32), jax.ShapeDtypeStruct((Lp, 1), F32)],
        scratch_shapes=[pltpu.VMEM((tm + MAX_WINDOW, D), F32)],
        compiler_params=_cparams("parallel"),
    )(h0, h0, pw, ps, g, b)


def _pool_bwd(dz, mixpre, pw, ps, name, *, tm):
    Lp, D = dz.shape
    G = D // N_GROUPS
    halo_blocks = tm // MAX_WINDOW
    n_halo = Lp // MAX_WINDOW
    ni = Lp // tm
    R = tm + MAX_WINDOW

    def body(dz_ref, halo_ref, mix_ref, pw_ref, ps_ref, dh_ref, dmb_ref, dsc_ref, ext_ref, dp_ref):
        i = pl.program_id(0)
        ext_ref[0:tm, :] = dz_ref[...]
        ext_ref[tm:R, :] = jnp.where(i == ni - 1, 0.0, halo_ref[...])
        dmix = (ext_ref[...] * ps_ref[...]).astype(BF16)
        dmb_ref[...] = dmix[0:tm]

        @pl.when(i == 0)
        def _():
            dsc_ref[...] = jnp.zeros_like(dsc_ref)

        dsc_ref[...] += jnp.sum(dz_ref[...] * mix_ref[...], axis=0, keepdims=True)
        t1 = (i * tm + 1 + lax.broadcasted_iota(jnp.int32, (R, 1), 0)).astype(F32)
        for gi, w in enumerate(POOL_WINDOWS):
            lo, hi = gi * G, (gi + 1) * G
            dd = lax.dot_general(dmix[:, lo:hi], pw_ref[gi], NT, preferred_element_type=F32)
            dp_ref[:, lo:hi] = dd / jnp.minimum(t1, float(w))
            back = dp_ref[0:tm, lo:hi]
            for j in range(1, w):
                back = back + dp_ref[j:j + tm, lo:hi]
            dh_ref[:, lo:hi] = ALPHA * dz_ref[:, lo:hi] - dd[0:tm] + back

    row = pl.BlockSpec((tm, D), lambda i: (i, 0))
    vec = pl.BlockSpec((1, D), lambda i: (0, 0))
    return pl.pallas_call(
        body, name=name, grid=(ni,),
        in_specs=[row,
                  pl.BlockSpec((MAX_WINDOW, D), lambda i: (jnp.minimum((i + 1) * halo_blocks, n_halo - 1), 0)),
                  row, pl.BlockSpec((N_GROUPS, G, G), lambda i: (0, 0, 0)), vec],
        out_specs=[row, row, vec],
        out_shape=[jax.ShapeDtypeStruct((Lp, D), F32), jax.ShapeDtypeStruct((Lp, D), BF16),
                   jax.ShapeDtypeStruct((1, D), F32)],
        scratch_shapes=[pltpu.VMEM((R, D), F32), pltpu.VMEM((R, D), F32)],
        compiler_params=_cparams("arbitrary"),
    )(dz, dz, mixpre, pw, ps)


def _pool_dw(diffb, dmb, name, *, tk):
    Lp, D = diffb.shape
    G = D // N_GROUPS

    def body(a_ref, b_ref, o_ref):
        @pl.when(pl.program_id(1) == 0)
        def _():
            o_ref[...] = jnp.zeros_like(o_ref)

        o_ref[0] += lax.dot_general(a_ref[...], b_ref[...], TN, preferred_element_type=F32)

    blk = pl.BlockSpec((tk, G), lambda g, k: (k, g))
    return pl.pallas_call(
        body, name=name, grid=(N_GROUPS, Lp // tk),
        in_specs=[blk, blk], out_specs=pl.BlockSpec((1, G, G), lambda g, k: (g, 0, 0)),
        out_shape=jax.ShapeDtypeStruct((N_GROUPS, G, G), F32),
        compiler_params=_cparams("parallel", "arbitrary"),
    )(diffb, dmb)


def _loss_ln_bwd(h, tgt, xh, rs, g, name, *, tm, row_lo, row_hi):
    Lp, D = h.shape

    def body(h_ref, t_ref, xh_ref, rs_ref, g_ref, dz_ref, dzb_ref, dg_ref, db_ref, loss_ref):
        i = pl.program_id(0)
        r = i * tm + lax.broadcasted_iota(jnp.int32, (tm, 1), 0)
        valid = (r >= row_lo) & (r < row_hi)
        e = jnp.where(valid, h_ref[...] - t_ref[...], 0.0)
        dy = e * (1.0 / D)
        x = xh_ref[...]
        dxh = dy * g_ref[...]
        m1 = jnp.mean(dxh, axis=-1, keepdims=True)
        m2 = jnp.mean(dxh * x, axis=-1, keepdims=True)
        dz = rs_ref[...] * (dxh - m1 - x * m2)
        dz_ref[...] = dz
        dzb_ref[...] = dz.astype(BF16)

        @pl.when(i == 0)
        def _():
            dg_ref[...] = jnp.zeros_like(dg_ref)
            db_ref[...] = jnp.zeros_like(db_ref)
            loss_ref[...] = jnp.zeros_like(loss_ref)

        dg_ref[...] += jnp.sum(dy * x, axis=0, keepdims=True)
        db_ref[...] += jnp.sum(dy, axis=0, keepdims=True)
        loss_ref[...] += 0.5 * jnp.sum(jnp.mean(e * e, axis=-1, keepdims=True), axis=0, keepdims=True)

    row = pl.BlockSpec((tm, D), lambda i: (i, 0))
    vec = pl.BlockSpec((1, D), lambda i: (0, 0))
    return pl.pallas_call(
        body, name=name, grid=(Lp // tm,),
        in_specs=[row, row, row, pl.BlockSpec((tm, 1), lambda i: (i, 0)), vec],
        out_specs=[row, row, vec, vec, pl.BlockSpec((1, 1), lambda i: (0, 0))],
        out_shape=[jax.ShapeDtypeStruct((Lp, D), F32), jax.ShapeDtypeStruct((Lp, D), BF16),
                   jax.ShapeDtypeStruct((1, D), F32), jax.ShapeDtypeStruct((1, D), F32),
                   jax.ShapeDtypeStruct((1, 1), F32)],
        compiler_params=_cparams("arbitrary"),
    )(h, tgt, xh, rs, g)


def _shift_rows_down(cur, prev, s, sub):
    return jnp.where(sub >= s, pltpu.roll(cur, s, 0), pltpu.roll(prev, s, 0))


def _shift_rows_up(cur, nxt, s, sub):
    return jnp.where(sub < SUBLANES - s, pltpu.roll(cur, SUBLANES - s, 0), pltpu.roll(nxt, SUBLANES - s, 0))


def _conv_group(cur, prev, cw_ref, cb_ref, sub):
    taps = [_shift_rows_down(cur, prev, 2, sub), _shift_rows_down(cur, prev, 1, sub), cur]
    c = cb_ref[...] + cw_ref[0:1, :] * taps[0] + cw_ref[1:2, :] * taps[1] + cw_ref[2:3, :] * taps[2]
    return c, taps


def _conv_glu_fwd(u, cw, cb, name, *, tm, tn):
    Lp, F2 = u.shape
    F = F2 // 2
    nj = F // tn
    halo_blocks = tm // SUBLANES
    S8 = SUBLANES
    assert GLU_STRIP == 2 * S8 and tm % GLU_STRIP == 0

    def body(ua_ref, ug_ref, pa_ref, pg_ref, cwa_ref, cwg_ref, cba_ref, cbg_ref, o_ref):
        first = pl.program_id(1) == 0
        sub = lax.broadcasted_iota(jnp.int32, (S8, tn), 0)

        def strip(r, prev_a, prev_g):
            out = []
            for g0 in (0, S8):
                a_cur = ua_ref[pl.ds(r + g0, S8), :]
                g_cur = ug_ref[pl.ds(r + g0, S8), :]
                a, _ = _conv_group(a_cur, prev_a, cwa_ref, cba_ref, sub)
                gate, _ = _conv_group(g_cur, prev_g, cwg_ref, cbg_ref, sub)
                out.append(a * jax.nn.sigmoid(a) * gate)
                prev_a, prev_g = a_cur, g_cur
            o_ref[pl.ds(r, GLU_STRIP), :] = jnp.concatenate(out, axis=0).astype(BF16)

        strip(0, jnp.where(first, 0.0, pa_ref[...]), jnp.where(first, 0.0, pg_ref[...]))

        def step(k, carry):
            r = pl.multiple_of(k * GLU_STRIP, GLU_STRIP)
            before = pl.ds(pl.multiple_of(r - S8, S8), S8)
            strip(r, ua_ref[before, :], ug_ref[before, :])
            return carry

        lax.fori_loop(1, tm // GLU_STRIP, step, 0)

    def prev(off):
        return pl.BlockSpec((SUBLANES, tn), lambda j, i: (jnp.maximum(i * halo_blocks - 1, 0), j + off))

    def cols(rows, off):
        return pl.BlockSpec((rows, tn), lambda j, i: (0, j + off))

    return pl.pallas_call(
        body, name=name, grid=(nj, Lp // tm),
        in_specs=[pl.BlockSpec((tm, tn), lambda j, i: (i, j)), pl.BlockSpec((tm, tn), lambda j, i: (i, j + nj)),
                  prev(0), prev(nj), cols(CONV_WIDTH, 0), cols(CONV_WIDTH, nj), cols(1, 0), cols(1, nj)],
        out_specs=pl.BlockSpec((tm, tn), lambda j, i: (i, j)),
        out_shape=jax.ShapeDtypeStruct((Lp, F), BF16),
        compiler_params=_cparams("parallel", "parallel"),
    )(u, u, u, u, cw, cw, cb, cb)


def _conv_glu_bwd(u, dact, cw, cb, name, *, tm, tn):
    Lp, F2 = u.shape
    F = F2 // 2
    nj = F // tn
    ni = Lp // tm
    halo_blocks = tm // SUBLANES
    n_halo = Lp // SUBLANES
    S8 = SUBLANES
    n_strips = tm // GLU_STRIP
    assert GLU_STRIP == 2 * S8 and tm % GLU_STRIP == 0

    def body(ua_ref, ug_ref, pa_ref, pg_ref, na_ref, ng_ref, da_ref, dn_ref,
             cwa_ref, cwg_ref, cba_ref, cbg_ref,
             du_ref, dwa_ref, dwg_ref, dba_ref, dbg_ref,
             wacc_a, wacc_g, bacc_a, bacc_g):
        i = pl.program_id(1)
        first, last = i == 0, i == ni - 1
        sub = lax.broadcasted_iota(jnp.int32, (S8, tn), 0)
        for acc in (wacc_a, wacc_g, bacc_a, bacc_g):
            acc[...] = jnp.zeros_like(acc)

        def dconv(a_cur, a_prev, g_cur, g_prev, dact_rows):
            a, taps_a = _conv_group(a_cur, a_prev, cwa_ref, cba_ref, sub)
            gate, taps_g = _conv_group(g_cur, g_prev, cwg_ref, cbg_ref, sub)
            sg = jax.nn.sigmoid(a)
            dca = dact_rows * gate * (sg * (1.0 + a * (1.0 - sg)))
            dcg = dact_rows * (a * sg)
            return dca, dcg, taps_a, taps_g

        def du_group(dc, dc_after, cw_ref):
            return (cw_ref[2:3, :] * dc + cw_ref[1:2, :] * _shift_rows_up(dc, dc_after, 1, sub)
                    + cw_ref[0:1, :] * _shift_rows_up(dc, dc_after, 2, sub))

        def strip(r, a_prev, g_prev, dca_after, dcg_after):
            a0, a1 = ua_ref[pl.ds(r, S8), :], ua_ref[pl.ds(r + S8, S8), :]
            g0, g1 = ug_ref[pl.ds(r, S8), :], ug_ref[pl.ds(r + S8, S8), :]
            dca1, dcg1, ta1, tg1 = dconv(a1, a0, g1, g0, da_ref[pl.ds(r + S8, S8), :])
            dca0, dcg0, ta0, tg0 = dconv(a0, a_prev, g0, g_prev, da_ref[pl.ds(r, S8), :])
            du_ref[0, pl.ds(r, GLU_STRIP), :] = jnp.concatenate(
                [du_group(dca0, dca1, cwa_ref), du_group(dca1, dca_after, cwa_ref)], axis=0).astype(BF16)
            du_ref[1, pl.ds(r, GLU_STRIP), :] = jnp.concatenate(
                [du_group(dcg0, dcg1, cwg_ref), du_group(dcg1, dcg_after, cwg_ref)], axis=0).astype(BF16)
            for k in range(CONV_WIDTH):
                wacc_a[k] += dca0 * ta0[k] + dca1 * ta1[k]
                wacc_g[k] += dcg0 * tg0[k] + dcg1 * tg1[k]
            bacc_a[...] += dca0 + dca1
            bacc_g[...] += dcg0 + dcg1
            return dca0, dcg0

        tail = pl.ds(tm - S8, S8)
        dca_after, dcg_after, _, _ = dconv(na_ref[...], ua_ref[tail, :], ng_ref[...], ug_ref[tail, :],
                                           jnp.where(last, 0.0, dn_ref[...]))

        def step(t, carry):
            r = pl.multiple_of((n_strips - 1 - t) * GLU_STRIP, GLU_STRIP)
            before = pl.ds(pl.multiple_of(r - S8, S8), S8)
            return strip(r, ua_ref[before, :], ug_ref[before, :], *carry)

        dca_after, dcg_after = lax.fori_loop(0, n_strips - 1, step, (dca_after, dcg_after))
        strip(0, jnp.where(first, 0.0, pa_ref[...]), jnp.where(first, 0.0, pg_ref[...]), dca_after, dcg_after)

        @pl.when(first)
        def _():
            for r in (dwa_ref, dwg_ref, dba_ref, dbg_ref):
                r[...] = jnp.zeros_like(r)

        for wacc, bacc, dw_ref, db_ref in ((wacc_a, bacc_a, dwa_ref, dba_ref), (wacc_g, bacc_g, dwg_ref, dbg_ref)):
            db_ref[...] += jnp.sum(bacc[...], axis=0, keepdims=True)
            for k in range(CONV_WIDTH):
                dw_ref[k:k + 1, :] += jnp.sum(wacc[k], axis=0, keepdims=True)

    def tile(off):
        return pl.BlockSpec((tm, tn), lambda j, i: (i, j + off))

    def prev(off):
        return pl.BlockSpec((S8, tn), lambda j, i: (jnp.maximum(i * halo_blocks - 1, 0), j + off))

    def nxt(off):
        return pl.BlockSpec((S8, tn), lambda j, i: (jnp.minimum((i + 1) * halo_blocks, n_halo - 1), j + off))

    def cols(rows, off):
        return pl.BlockSpec((rows, tn), lambda j, i: (0, j + off))

    return pl.pallas_call(
        body, name=name, grid=(nj, ni),
        in_specs=[tile(0), tile(nj), prev(0), prev(nj), nxt(0), nxt(nj), tile(0), nxt(0),
                  cols(CONV_WIDTH, 0), cols(CONV_WIDTH, nj), cols(1, 0), cols(1, nj)],
        out_specs=[pl.BlockSpec((2, tm, tn), lambda j, i: (0, i, j)),
                   cols(CONV_WIDTH, 0), cols(CONV_WIDTH, 0), cols(1, 0), cols(1, 0)],
        out_shape=[jax.ShapeDtypeStruct((2, Lp, F), BF16),
                   jax.ShapeDtypeStruct((CONV_WIDTH, F), F32), jax.ShapeDtypeStruct((CONV_WIDTH, F), F32),
                   jax.ShapeDtypeStruct((1, F), F32), jax.ShapeDtypeStruct((1, F), F32)],
        scratch_shapes=[pltpu.VMEM((CONV_WIDTH, S8, tn), F32), pltpu.VMEM((CONV_WIDTH, S8, tn), F32),
                        pltpu.VMEM((S8, tn), F32), pltpu.VMEM((S8, tn), F32)],
        compiler_params=_cparams("parallel", "arbitrary"),
    )(u, u, u, u, u, u, dact, dact, cw, cw, cb, cb)


def _bias_routing(n_heads, width, first_slot):
    h = lax.broadcasted_iota(jnp.int32, (LANES, width), 0)
    col = lax.broadcasted_iota(jnp.int32, (LANES, width), 1)
    base = LANES * (h // 2) + HEAD_DIM * (1 - h % 2) + first_slot
    return [((col == base + t) & (h < n_heads)).astype(BF16) for t in range(3)]


def _three_terms(x):
    hi = x.astype(BF16)
    r1 = x - hi.astype(F32)
    lo = r1.astype(BF16)
    lo2 = (r1 - lo.astype(F32)).astype(BF16)
    return hi, lo, lo2


def _logf_cumsum(pre, bf, name, *, tm, n_heads, width):
    Lp, W = pre.shape

    def body(p_ref, b_ref, c_ref, kx_ref, qx_ref, carry_ref):
        i = pl.program_id(0)

        @pl.when(i == 0)
        def _():
            carry_ref[...] = jnp.zeros_like(carry_ref)

        x = p_ref[...] + b_ref[...]
        lf = jnp.minimum(x, 0.0) - jnp.log(1.0 + jnp.exp(-jnp.abs(x)))
        tri = (lax.broadcasted_iota(jnp.int32, (tm, tm), 0) >= lax.broadcasted_iota(jnp.int32, (tm, tm), 1)).astype(F32)
        c = jnp.dot(tri, lf, precision=lax.Precision.HIGHEST, preferred_element_type=F32) + carry_ref[...]
        c_ref[...] = c
        carry_ref[...] = c[tm - 1:tm, :]
        terms = _three_terms(c)
        slot = lax.broadcasted_iota(jnp.int32, (tm, width), 1) % HEAD_DIM
        ones_k = ((slot >= 3) & (slot < BIAS_SLOTS)).astype(F32)
        ones_q = (slot < 3).astype(F32)
        kx = sum(jnp.dot(t, r, preferred_element_type=F32) for t, r in zip(terms, _bias_routing(n_heads, width, 0)))
        qx = sum(jnp.dot(t, r, preferred_element_type=F32) for t, r in zip(terms, _bias_routing(n_heads, width, 3)))
        kx_ref[...] = (ones_k - kx).astype(BF16)
        qx_ref[...] = (ones_q + qx).astype(BF16)

    row = pl.BlockSpec((tm, W), lambda i: (i, 0))
    wide = pl.BlockSpec((tm, width), lambda i: (i, 0))
    return pl.pallas_call(
        body, name=name, grid=(Lp // tm,),
        in_specs=[row, pl.BlockSpec((1, W), lambda i: (0, 0))], out_specs=[row, wide, wide],
        out_shape=[jax.ShapeDtypeStruct((Lp, W), F32), jax.ShapeDtypeStruct((Lp, width), BF16),
                   jax.ShapeDtypeStruct((Lp, width), BF16)],
        scratch_shapes=[pltpu.VMEM((1, W), F32)],
        compiler_params=_cparams("arbitrary"),
    )(pre, bf)


def _logf_bwd(dc_a, dc_b, pre, bf, name, *, tm):
    Lp, W = pre.shape
    ni = Lp // tm

    def body(dca_ref, dcb_ref, p_ref, b_ref, dpb_ref, db_ref, carry_ref):
        i = pl.program_id(0)

        @pl.when(i == 0)
        def _():
            carry_ref[...] = jnp.zeros_like(carry_ref)
            db_ref[...] = jnp.zeros_like(db_ref)

        triu = (lax.broadcasted_iota(jnp.int32, (tm, tm), 0) <= lax.broadcasted_iota(jnp.int32, (tm, tm), 1)).astype(F32)
        dl = jnp.dot(triu, dca_ref[...] + dcb_ref[...], precision=lax.Precision.HIGHEST,
                     preferred_element_type=F32) + carry_ref[...]
        carry_ref[...] = dl[0:1, :]
        dp = dl * jax.nn.sigmoid(-(p_ref[...] + b_ref[...]))
        dpb_ref[...] = dp.astype(BF16)
        db_ref[...] += jnp.sum(dp, axis=0, keepdims=True)

    rev = pl.BlockSpec((tm, W), lambda i: (ni - 1 - i, 0))
    vec = pl.BlockSpec((1, W), lambda i: (0, 0))
    return pl.pallas_call(
        body, name=name, grid=(ni,),
        in_specs=[rev, rev, rev, vec], out_specs=[rev, vec],
        out_shape=[jax.ShapeDtypeStruct((Lp, W), BF16), jax.ShapeDtypeStruct((1, W), F32)],
        scratch_shapes=[pltpu.VMEM((1, W), F32)],
        compiler_params=_cparams("arbitrary"),
    )(dc_a, dc_b, pre, bf)


def _attn_fwd(qb, kvb, kx, qx, name, *, tq):
    Lp, D = qb.shape
    H = D // HEAD_DIM
    nq = Lp // tq
    S8 = SUBLANES
    assert LANES // HEAD_DIM == 2
    HB = 2 * ATTN_FWD_PAIRS
    W = LANES * ATTN_FWD_PAIRS
    n_scratch = 5

    def body(q_ref, qx_ref, k_ref, kx_ref, v_ref, o_ref, ob_ref, lse_ref, vt_ref, *scratch):
        i = pl.program_id(1)
        heads = [scratch[n_scratch * hb:n_scratch * (hb + 1)] for hb in range(HB)]
        lane = lax.broadcasted_iota(jnp.int32, (tq, LANES), 1)

        def own_lanes(hb, x2, extra2):
            return jnp.where((lane < HEAD_DIM) == (hb % 2 == 0), x2, extra2)

        q_of = [own_lanes(hb, q_ref[:, pl.ds(LANES * (hb // 2), LANES)], qx_ref[:, pl.ds(LANES * (hb // 2), LANES)])
                for hb in range(HB)]

        @pl.when(i == 0)
        def _():
            for j in range(nq):
                vt_ref[j] = jnp.transpose(v_ref[pl.ds(j * tq, tq), :].astype(F32)).astype(BF16)

        for m_ref, l_ref, acc_ref, _, _ in heads:
            m_ref[...] = jnp.full_like(m_ref, NEG_INF)
            l_ref[...] = jnp.zeros_like(l_ref)
            acc_ref[...] = jnp.zeros_like(acc_ref)

        def chunk(j, masked):
            keys = pl.ds(pl.multiple_of(j * tq, tq), tq)
            for hb, (_, _, _, st_ref, _) in enumerate(heads):
                pair = pl.ds(LANES * (hb // 2), LANES)
                k_own = own_lanes(hb, k_ref[keys, pair], kx_ref[keys, pair])
                st_ref[...] = lax.dot_general(k_own, q_of[hb], NT, preferred_element_type=F32)
            for hb, (m_ref, l_ref, acc_ref, st_ref, pt_ref) in enumerate(heads):
                mx = jnp.full((S8, tq), NEG_INF, F32)
                for r0 in range(0, tq, ATTN_STRIP):
                    rows = pl.ds(r0, ATTN_STRIP)
                    st = st_ref[rows, :]
                    if masked:
                        keep = (lax.broadcasted_iota(jnp.int32, (ATTN_STRIP, tq), 1)
                                >= r0 + lax.broadcasted_iota(jnp.int32, (ATTN_STRIP, tq), 0))
                        st = jnp.where(keep, st, NEG_INF)
                        st_ref[rows, :] = st
                    for g0 in range(0, ATTN_STRIP, S8):
                        mx = jnp.maximum(mx, st[g0:g0 + S8])
                m_prev = m_ref[...]
                m_new = jnp.maximum(m_prev, jnp.max(mx, axis=0, keepdims=True))
                alpha = jnp.exp(m_prev - m_new)
                m_ref[...] = m_new
                ls = jnp.zeros((S8, tq), F32)
                for r0 in range(0, tq, ATTN_STRIP):
                    pieces = [jnp.exp(st_ref[pl.ds(r0 + g0, S8), :] - m_new) for g0 in range(0, ATTN_STRIP, S8)]
                    for piece in pieces:
                        ls = ls + piece
                    pt_ref[pl.ds(r0, ATTN_STRIP), :] = jnp.concatenate(pieces, axis=0).astype(BF16)
                l_ref[...] = alpha * l_ref[...] + ls
                pv = jnp.dot(vt_ref[j, pl.ds(LANES * (hb // 2), LANES), :], pt_ref[...], preferred_element_type=F32)
                acc_ref[...] = jnp.concatenate([alpha] * (LANES // S8), axis=0) * acc_ref[...] + pv

        def step(j, carry):
            chunk(j, False)
            return carry

        lax.fori_loop(0, i, step, 0)
        chunk(i, True)
        outs = []
        for hb, (m_ref, l_ref, acc_ref, _, _) in enumerate(heads):
            l_row = jnp.sum(l_ref[...], axis=0, keepdims=True)
            outs.append(acc_ref[...] / l_row)
            lse_ref[hb, 0] = m_ref[0:1, :] + jnp.log(l_row)
        first_rows = lax.broadcasted_iota(jnp.int32, (LANES, tq), 0) < HEAD_DIM
        for pp in range(ATTN_FWD_PAIRS):
            o2 = jnp.transpose(jnp.where(first_rows, outs[2 * pp], outs[2 * pp + 1]))
            o_ref[:, pl.ds(LANES * pp, LANES)] = o2
            ob_ref[:, pl.ds(LANES * pp, LANES)] = o2.astype(BF16)

    per_head = [pltpu.VMEM((S8, tq), F32), pltpu.VMEM((S8, tq), F32), pltpu.VMEM((LANES, tq), F32),
                pltpu.VMEM((tq, tq), F32), pltpu.VMEM((tq, tq), BF16)]
    assert len(per_head) == n_scratch and H % HB == 0
    v_blocks = D // W
    tile = pl.BlockSpec((tq, W), lambda p, i: (i, p))
    whole = pl.BlockSpec((Lp, W), lambda p, i: (0, p))
    return pl.pallas_call(
        body, name=name, grid=(H // HB, nq),
        in_specs=[tile, tile, whole, whole, pl.BlockSpec((Lp, W), lambda p, i: (0, v_blocks + p))],
        out_specs=[tile, tile, pl.BlockSpec((HB, 1, 1, tq), lambda p, i: (p, i, 0, 0))],
        out_shape=[jax.ShapeDtypeStruct((Lp, D), F32), jax.ShapeDtypeStruct((Lp, D), BF16),
                   jax.ShapeDtypeStruct((H, nq, 1, tq), F32)],
        scratch_shapes=[pltpu.VMEM((nq, W, tq), BF16)] + per_head * HB,
        compiler_params=_cparams("parallel", "arbitrary"),
    )(qb, qx, kvb, kx, kvb)


def _attn_bwd(qb, dob, kvb, lse4, delta4, crow4, name, *, tq):
    Lp, D = qb.shape
    H = D // HEAD_DIM
    nq = Lp // tq
    HB = LANES // HEAD_DIM
    lane_tiles = tq // LANES
    n_scratch = 8
    assert HB == 2

    def body(q_ref, do_ref, k_ref, v_ref, lse_ref, dl_ref, c_ref,
             dqb_ref, dk_ref, dv_ref, dcs_ref, dcq_ref, dqt_ref, kt_ref, *scratch):
        j = pl.program_id(1)
        heads = [scratch[n_scratch * hb:n_scratch * (hb + 1)] for hb in range(HB)]
        first_head = lax.broadcasted_iota(jnp.int32, (tq, LANES), 1) < HEAD_DIM

        def split(x2):
            zero = jnp.zeros_like(x2)
            return [jnp.where(first_head, x2, zero), jnp.where(first_head, zero, x2)]

        @pl.when(j == 0)
        def _():
            dqt_ref[...] = jnp.zeros_like(dqt_ref)
            dcq_ref[...] = jnp.zeros_like(dcq_ref)

        k2 = k_ref[...]
        v2 = v_ref[...]
        kt_ref[...] = jnp.transpose(k2.astype(F32)).astype(BF16)
        first_rows = lax.broadcasted_iota(jnp.int32, (LANES, tq), 0) < HEAD_DIM
        for hb, (dk_acc, dv_acc, dc_acc, _, _, _, _, cs_ref) in enumerate(heads):
            dk_acc[...] = jnp.zeros_like(dk_acc)
            dv_acc[...] = jnp.zeros_like(dv_acc)
            dc_acc[...] = jnp.zeros_like(dc_acc)
            cs_ref[...] = jnp.transpose(jnp.broadcast_to(c_ref[hb, j], (LANES, tq)))

        def pair(i, masked):
            queries = pl.ds(pl.multiple_of(i * tq, tq), tq)
            q2 = q_ref[queries, :]
            do2 = do_ref[queries, :]
            q_of, do_of = split(q2), split(do2)
            for hb, (_, _, _, st_ref, dp_ref, _, _, _) in enumerate(heads):
                st_ref[...] = lax.dot_general(k2, q_of[hb], NT, preferred_element_type=F32)
                dp_ref[...] = lax.dot_general(v2, do_of[hb], NT, preferred_element_type=F32)
            dq_parts = []
            for hb, (dk_acc, dv_acc, dc_acc, st_ref, dp_ref, pt_ref, ds_ref, cs_ref) in enumerate(heads):
                bias_q = c_ref[hb, i] - lse_ref[hb, i]
                delta = dl_ref[hb, i]
                col_sum = jnp.zeros((SUBLANES, tq), F32)
                for r0 in range(0, tq, ATTN_STRIP):
                    rows = pl.ds(r0, ATTN_STRIP)
                    st = st_ref[rows, :] + (bias_q - jnp.concatenate([cs_ref[rows, :]] * lane_tiles, axis=1))
                    if masked:
                        keep = (lax.broadcasted_iota(jnp.int32, (ATTN_STRIP, tq), 1)
                                >= r0 + lax.broadcasted_iota(jnp.int32, (ATTN_STRIP, tq), 0))
                        st = jnp.where(keep, st, NEG_INF)
                    pt = jnp.exp(st)
                    dst = pt * (dp_ref[rows, :] - delta)
                    pt_ref[rows, :] = pt.astype(BF16)
                    ds_ref[rows, :] = dst.astype(BF16)
                    dc_acc[rows, :] += jnp.sum(dst, axis=1, keepdims=True)
                    for g0 in range(0, ATTN_STRIP, SUBLANES):
                        col_sum = col_sum + dst[g0:g0 + SUBLANES]
                dcq_ref[hb, i] += jnp.sum(col_sum, axis=0, keepdims=True)
                dv_acc[...] += jnp.dot(pt_ref[...], do2, preferred_element_type=F32)
                dk_acc[...] += jnp.dot(ds_ref[...], q2, preferred_element_type=F32)
                dq_parts.append(jnp.dot(kt_ref[...], ds_ref[...], preferred_element_type=F32))
            dqt_ref[i] += jnp.where(first_rows, dq_parts[0], dq_parts[1])

        def step(i, carry):
            pair(i, False)
            return carry

        pair(j, True)
        lax.fori_loop(j + 1, nq, step, 0)
        dk_ref[...] = jnp.where(first_head, heads[0][0][...], heads[1][0][...]).astype(BF16)
        dv_ref[...] = jnp.where(first_head, heads[0][1][...], heads[1][1][...]).astype(BF16)
        for hb in range(HB):
            dcs_ref[hb, 0] = -jnp.transpose(jnp.broadcast_to(heads[hb][2][...], (tq, LANES)))[0:1, :]

        @pl.when(j == nq - 1)
        def _():
            for i in range(nq):
                dqb_ref[pl.ds(i * tq, tq), :] = jnp.transpose(dqt_ref[i]).astype(BF16)

    per_head = [pltpu.VMEM((tq, LANES), F32), pltpu.VMEM((tq, LANES), F32), pltpu.VMEM((tq, 1), F32),
                pltpu.VMEM((tq, tq), F32), pltpu.VMEM((tq, tq), F32),
                pltpu.VMEM((tq, tq), BF16), pltpu.VMEM((tq, tq), BF16), pltpu.VMEM((tq, LANES), F32)]
    assert len(per_head) == n_scratch
    v_blocks = D // LANES
    whole = pl.BlockSpec((Lp, LANES), lambda p, j: (0, p))
    tile = pl.BlockSpec((tq, LANES), lambda p, j: (j, p))
    rows = pl.BlockSpec((HB, nq, 1, tq), lambda p, j: (p, 0, 0, 0))
    return pl.pallas_call(
        body, name=name, grid=(H // HB, nq),
        in_specs=[whole, whole, tile, pl.BlockSpec((tq, LANES), lambda p, j: (j, v_blocks + p)), rows, rows, rows],
        out_specs=[whole, tile, tile, pl.BlockSpec((HB, 1, 1, tq), lambda p, j: (p, j, 0, 0)), rows],
        out_shape=[jax.ShapeDtypeStruct((Lp, D), BF16), jax.ShapeDtypeStruct((Lp, D), BF16),
                   jax.ShapeDtypeStruct((Lp, D), BF16), jax.ShapeDtypeStruct((H, nq, 1, tq), F32),
                   jax.ShapeDtypeStruct((H, nq, 1, tq), F32)],
        scratch_shapes=[pltpu.VMEM((nq, LANES, tq), F32), pltpu.VMEM((LANES, tq), BF16)] + per_head * HB,
        compiler_params=_cparams("parallel", "arbitrary"),
    )(qb, dob, kvb, kvb, lse4, delta4, crow4)


def _remote(src, dst, send_sems, recv_sems, k, to):
    return pltpu.make_async_remote_copy(src_ref=src, dst_ref=dst, send_sem=send_sems.at[k], recv_sem=recv_sems.at[k],
                                        device_id=to, device_id_type=MESH)


def _place():
    x, y, c = lax.axis_index("x"), lax.axis_index("y"), lax.axis_index("c")
    other_chips = [(1 - x, y), (x, 1 - y), (1 - x, 1 - y)]
    return x, y, c, other_chips


def _all_gather_weights(wb, wf, name):
    Rb, C = wb.shape
    Rf = wf.shape[0]
    hb = Rb // 2

    def body(wb_ref, wf_ref, ob_ref, of_ref, send_sems, recv_sems):
        x, y, c, chips = _place()
        me = 2 * x + y
        sibling = (x, y, 1 - c)

        def half(chip, core):
            return ob_ref.at[chip, pl.ds(core * hb, hb), :]

        sent = []
        for j, (cx, cy) in enumerate(chips):
            sent.append(_remote(wb_ref.at[pl.ds(c * hb, hb), :], half(me, c), send_sems, recv_sems, j, (cx, cy, c)))
            sent.append(_remote(wf_ref, of_ref.at[me], send_sems, recv_sems, 3 + j, (cx, cy, c)))
        for cp in sent:
            cp.start()
        for j, (cx, cy) in enumerate(chips):
            chip = 2 * cx + cy
            _remote(half(chip, c), half(chip, c), send_sems, recv_sems, j, sibling).wait_recv()
            fwd = _remote(half(chip, c), half(chip, c), send_sems, recv_sems, 6 + j, sibling)
            fwd.start()
            sent.append(fwd)
        for j, (cx, cy) in enumerate(chips):
            chip = 2 * cx + cy
            _remote(wf_ref, of_ref.at[chip], send_sems, recv_sems, 3 + j, sibling).wait_recv()
            _remote(half(chip, 1 - c), half(chip, 1 - c), send_sems, recv_sems, 6 + j, sibling).wait_recv()
        for cp in sent:
            cp.wait_send()

    any_spec = pl.BlockSpec(memory_space=pl.ANY)
    return pl.pallas_call(
        body, name=name,
        in_specs=[any_spec, any_spec], out_specs=[any_spec, any_spec],
        out_shape=[jax.ShapeDtypeStruct((N_CHIPS, Rb, C), BF16), jax.ShapeDtypeStruct((N_CHIPS, Rf, C), F32)],
        scratch_shapes=[pltpu.SemaphoreType.DMA((9,)), pltpu.SemaphoreType.DMA((9,))],
    )(wb, wf)


def _half_of(ref, order, half):
    return ref.at[pl.ds(0, N_CHIPS), half] if order == "CH" else ref.at[half]


def _halves_to_sibling(grads, orders, name):
    n = len(grads)

    def body(*refs):
        g_refs, a_refs, (send_sems, recv_sems) = refs[:n], refs[n:2 * n], refs[2 * n:]
        x, y, c, _ = _place()
        copies = [_remote(_half_of(g, o, 1 - c), a, send_sems, recv_sems, k, (x, y, 1 - c))
                  for k, (g, a, o) in enumerate(zip(g_refs, a_refs, orders))]
        for cp in copies:
            cp.start()
        for cp in copies:
            cp.wait()

    any_spec = pl.BlockSpec(memory_space=pl.ANY)
    shapes = [g.shape[2:] for g in grads]
    return pl.pallas_call(
        body, name=name, in_specs=[any_spec] * n, out_specs=[any_spec] * n,
        out_shape=[jax.ShapeDtypeStruct((N_CHIPS,) + s, F32) for s in shapes],
        scratch_shapes=[pltpu.SemaphoreType.DMA((n,)), pltpu.SemaphoreType.DMA((n,))],
    )(*grads)


def _chip_partial(g, a, core, order, wire, name, *, tr):
    _, R, C = a.shape
    narrow = wire != F32

    def body(core_ref, g_ref, a_ref, *outs):
        p = g_ref[0, 0] + a_ref[0]
        outs[0][0] = p
        if narrow:
            outs[1][0] = p.astype(wire)

    if order == "CH":
        g_spec = pl.BlockSpec((1, 1, tr, C), lambda s, i, core_ref: (s, core_ref[0], i, 0))
    else:
        g_spec = pl.BlockSpec((1, 1, tr, C), lambda s, i, core_ref: (core_ref[0], s, i, 0))
    blk = pl.BlockSpec((1, tr, C), lambda s, i, core_ref: (s, i, 0))
    grid_spec = pltpu.PrefetchScalarGridSpec(
        num_scalar_prefetch=1, grid=(N_CHIPS, R // tr), in_specs=[g_spec, blk],
        out_specs=[blk, blk] if narrow else [blk])
    out_shape = [jax.ShapeDtypeStruct((N_CHIPS, R, C), F32)] + ([jax.ShapeDtypeStruct((N_CHIPS, R, C), wire)] if narrow else [])
    outs = pl.pallas_call(body, name=name, grid_spec=grid_spec, out_shape=out_shape,
                          compiler_params=_cparams("parallel", "parallel"))(core, g, a)
    return outs[0], outs[-1]


def _chip_exchange(parts, rep, name):
    n = len(parts)
    rr, C = rep.shape

    def body(*refs):
        p_refs, rep_ref = refs[:n], refs[n]
        land_refs, reps_ref = refs[n + 1:2 * n + 1], refs[2 * n + 1]
        send_sems, recv_sems, local_sem = refs[2 * n + 2:]
        x, y, c, chips = _place()
        me = 4 * x + 2 * y + c
        own = pltpu.make_async_copy(rep_ref, reps_ref.at[me], local_sem.at[0])
        own.start()
        sent = []
        for k, (p, land) in enumerate(zip(p_refs, land_refs)):
            for j, (cx, cy) in enumerate(chips):
                sent.append(_remote(p.at[2 * cx + cy], land.at[j], send_sems, recv_sems, 3 * k + j, (cx, cy, c)))
        for r in range(1, N_DEV):
            fx, fy, fc = (r >> 2) & 1, (r >> 1) & 1, r & 1
            sent.append(_remote(rep_ref, reps_ref.at[me], send_sems, recv_sems, 3 * n - 1 + r, (x ^ fx, y ^ fy, c ^ fc)))
        for cp in sent:
            cp.start()
        for k, (p, land) in enumerate(zip(p_refs, land_refs)):
            for j in range(3):
                _remote(p.at[0], land.at[j], send_sems, recv_sems, 3 * k + j, (x, y, c)).wait_recv()
        for r in range(1, N_DEV):
            fx, fy, fc = (r >> 2) & 1, (r >> 1) & 1, r & 1
            frm = 4 * (x ^ fx) + 2 * (y ^ fy) + (c ^ fc)
            _remote(rep_ref, reps_ref.at[frm], send_sems, recv_sems, 3 * n - 1 + r, (x, y, c)).wait_recv()
        for cp in sent:
            cp.wait_send()
        own.wait()

    any_spec = pl.BlockSpec(memory_space=pl.ANY)
    n_sems = 3 * n + N_DEV - 1
    return pl.pallas_call(
        body, name=name, in_specs=[any_spec] * (n + 1), out_specs=[any_spec] * (n + 1),
        out_shape=[jax.ShapeDtypeStruct((3,) + p.shape[1:], p.dtype) for p in parts]
        + [jax.ShapeDtypeStruct((N_DEV, rr, C), F32)],
        scratch_shapes=[pltpu.SemaphoreType.DMA((n_sems,)), pltpu.SemaphoreType.DMA((n_sems,)),
                        pltpu.SemaphoreType.DMA((1,))],
    )(*parts, rep)


def _adamw_math(w, g, m, v):
    m = ADAM_B1 * m + (1.0 - ADAM_B1) * g
    v = ADAM_B2 * v + (1.0 - ADAM_B2) * (g * g)
    m_hat = m / (1.0 - ADAM_B1 ** ADAM_STEP)
    v_hat = v / (1.0 - ADAM_B2 ** ADAM_STEP)
    delta = -ADAM_LR * (m_hat / (jnp.sqrt(v_hat) + ADAM_EPS) + ADAM_WD * w)
    return delta, m, v


def _adamw_owned(part, landed, w, m, v, place, name, *, tr):
    _, R, C = part.shape

    def body(place_ref, own_ref, land_ref, w_ref, m_ref, v_ref, g_ref, d_ref, mo_ref, vo_ref):
        g = own_ref[0]
        for s in range(3):
            g = g + land_ref[s].astype(F32)
        delta, m_new, v_new = _adamw_math(w_ref[0], g, m_ref[0], v_ref[0])
        g_ref[0] = g
        d_ref[0] = delta
        mo_ref[0] = m_new
        vo_ref[0] = v_new

    half = pl.BlockSpec((1, tr, C), lambda i, place_ref: (place_ref[0], i, 0))
    grid_spec = pltpu.PrefetchScalarGridSpec(
        num_scalar_prefetch=1, grid=(R // tr,),
        in_specs=[pl.BlockSpec((1, tr, C), lambda i, place_ref: (place_ref[1], i, 0)),
                  pl.BlockSpec((3, tr, C), lambda i, place_ref: (0, i, 0)), half, half, half],
        out_specs=[half] * 4)
    return pl.pallas_call(
        body, name=name, grid_spec=grid_spec, out_shape=[jax.ShapeDtypeStruct((2, R, C), F32)] * 4,
        compiler_params=_cparams("parallel"),
    )(place, part, landed, w, m, v)


def _join_halves(bufs, name):
    n = len(bufs)

    def body(*refs):
        out_refs, (send_sems, recv_sems) = refs[n:2 * n], refs[2 * n:]
        x, y, c, _ = _place()
        copies = [_remote(o.at[c], o.at[c], send_sems, recv_sems, k, (x, y, 1 - c)) for k, o in enumerate(out_refs)]
        for cp in copies:
            cp.start()
        for k, o in enumerate(out_refs):
            _remote(o.at[c], o.at[1 - c], send_sems, recv_sems, k, (x, y, 1 - c)).wait_recv()
        for cp in copies:
            cp.wait_send()

    any_spec = pl.BlockSpec(memory_space=pl.ANY)
    return pl.pallas_call(
        body, name=name, in_specs=[any_spec] * n, out_specs=[any_spec] * n,
        out_shape=[jax.ShapeDtypeStruct(b.shape, b.dtype) for b in bufs],
        input_output_aliases={k: k for k in range(n)},
        scratch_shapes=[pltpu.SemaphoreType.DMA((n,)), pltpu.SemaphoreType.DMA((n,))],
    )(*bufs)


def _sum_adamw(own, landed, w, m, v, name, *, tr):
    n = landed.shape[0]
    hr, C = own.shape

    def body(own_ref, land_ref, w_ref, m_ref, v_ref, o_ref):
        g = own_ref[...]
        for s in range(n):
            g = g + land_ref[s]
        delta, m_new, v_new = _adamw_math(w_ref[...], g, m_ref[...], v_ref[...])
        o_ref[0] = g
        o_ref[1] = delta
        o_ref[2] = m_new
        o_ref[3] = v_new

    blk = pl.BlockSpec((tr, C), lambda i: (i, 0))
    return pl.pallas_call(
        body, name=name, grid=(hr // tr,),
        in_specs=[blk, pl.BlockSpec((n, tr, C), lambda i: (0, i, 0)), blk, blk, blk],
        out_specs=pl.BlockSpec((4, tr, C), lambda i: (0, i, 0)),
        out_shape=jax.ShapeDtypeStruct((4, hr, C), F32), compiler_params=_cparams("parallel"),
    )(own, landed, w, m, v)


def _rows_of(shape):
    n = 1
    for d in shape:
        n *= d
    return -(-n // PACK_COLS)


def _pack(arrays, total_rows, dtype):
    parts, used = [], 0
    for a in arrays:
        flat = a.reshape(-1).astype(dtype)
        fill = _rows_of(a.shape) * PACK_COLS - flat.shape[0]
        parts += [flat] + ([jnp.zeros((fill,), dtype)] if fill else [])
        used += _rows_of(a.shape)
    if total_rows > used:
        parts.append(jnp.zeros(((total_rows - used) * PACK_COLS,), dtype))
    return jnp.concatenate(parts).reshape(total_rows, PACK_COLS)


def _unpack(buf, shapes):
    lead = buf.shape[:-2]
    out, r = [], 0
    for shp in shapes:
        n = 1
        for d in shp:
            n *= d
        rows = _rows_of(shp)
        piece = buf[..., r:r + rows, :].reshape(lead + (rows * PACK_COLS,))[..., :n]
        out.append(piece.reshape(lead + tuple(shp)))
        r += rows
    return out


def _join_shards(stacked, axis):
    return jnp.concatenate([stacked[s] for s in range(N_CHIPS)], axis=axis)


def _shard_of(full, axis, chip):
    width = full.shape[axis] // N_CHIPS
    return lax.slice_in_dim(full, chip * width, (chip + 1) * width, axis=axis)


def _local_step(h0, tgt, W, *, seq, tm):
    Lp, D = h0.shape
    H = D // HEAD_DIM
    F2 = W["ffn_w_in"].shape[-1]
    F = F2 // 2
    te = tm // 2
    nq = Lp // tm
    cap = 1408
    tD, tF, tF2 = _pick(D, cap), _pick(F, cap), _pick(F2, cap)
    t2D = _pick(2 * D, cap)
    t2Dc, tF2c = _pick(2 * D // N_CHIPS, cap), _pick(F2 // N_CHIPS, cap)
    tcn = _pick(F, cap)

    def vec(a):
        return a.reshape(1, -1)

    ln_g, ln_b = W["ln_g"], W["ln_b"]
    wf_pad = jnp.pad(W["w_f"], ((0, 0), (0, LANES - H)))
    bf_pad = jnp.pad(W["b_f"], (0, LANES - H)).reshape(1, LANES)

    def ffn_fwd(h, hb, l, tag):
        u = _mm(hb, W["ffn_w_in"][l], "nn", F32, f"ffn{tag}_up", tm=tm, tn=tF2, tk=tD)
        act = _conv_glu_fwd(u, W["ffn_conv_w"][l], vec(W["ffn_conv_b"][l]), f"ffn{tag}_glu", tm=te, tn=tcn)
        normed = _mm_ln(act, W["ffn_w_out"][l], "nn", f"ffn{tag}_down_ln", tm=tm, tk=tF, forward=True,
                        rows=[h], vecs=[vec(ln_g[l, 1]), vec(ln_b[l, 1])])
        return u, act, normed

    def ffn_bwd(dz, dzb, hb, u, act, l, tag, dw_in_acc, dw_out_acc, xh_in, rs_in, g_in):
        dact = _mm(dzb, W["ffn_w_out"][l], "nt", F32, f"ffn{tag}_dact", tm=tm, tn=tF, tk=tD)
        dw_out = _mm(act, dzb, "tn", F32, f"ffn{tag}_dwout", tm=tF, tn=tD, tk=tm, layer=l, into=dw_out_acc)
        du, dwa, dwg, dba, dbg = _conv_glu_bwd(u, dact, W["ffn_conv_w"][l], vec(W["ffn_conv_b"][l]),
                                               f"ffn{tag}_dglu", tm=te, tn=tcn)
        dcw = jnp.concatenate([dwa, dwg], axis=1)
        dcb = jnp.concatenate([dba, dbg], axis=1)
        prev = _mm_ln(du, W["ffn_w_in"][l], "nt", f"ffn{tag}_dh_ln", tm=tm, tk=tF, forward=False,
                      rows=[dz, xh_in, rs_in], vecs=[g_in], a_halves=True)
        dw_in = _mm(hb, du, "tn", F32, f"ffn{tag}_dwin", tm=tD, tn=tF2c, tk=tm, chips=True, layer=l, into=dw_in_acc,
                    b_halves=True)
        return prev, dw_in, dw_out, dcw, dcb[0]

    diffb, mixpre, h1, h1b, xh1, rs1 = _pool_ln_fwd(h0, W["pool_w"][0], W["pool_scale"], vec(ln_g[0, 0]),
                                                    vec(ln_b[0, 0]), "pool_ln_fwd", tm=te)
    u0, act0, (h2, h2b, xh2, rs2) = ffn_fwd(h1, h1b, 0, "0")

    kvb = _mm(h2b, W["w_kv"], "nn", BF16, "kv_proj", tm=tm, tn=t2D, tk=tD)
    qb = _mm(h2b, W["w_q"][0], "nn", BF16, "q_proj", tm=tm, tn=tD, tk=tD, scale=HEAD_DIM ** -0.5)
    pre = _mm(h2b, wf_pad, "nn", F32, "f_proj", tm=tm, tn=LANES, tk=tD)
    c, kx, qx = _logf_cumsum(pre, bf_pad, "logf_cumsum", tm=tm, n_heads=H, width=D)

    crow4 = c[:, :H].T.reshape(H, nq, 1, tm)
    o_tok, ob, lse4 = _attn_fwd(qb, kvb, kx, qx, "attn_fwd", tq=tm)
    h3, h3b, xh3, rs3 = _mm_ln(ob, W["w_o"][0], "nn", "o_proj_ln", tm=tm, tk=tD, forward=True,
                               rows=[h2], vecs=[vec(ln_g[1, 0]), vec(ln_b[1, 0])])
    u1, act1, (h4, _, xh4, rs4) = ffn_fwd(h3, h3b, 1, "1")

    dz4, dz4b, dg11, db11, loss = _loss_ln_bwd(h4, tgt, xh4, rs4, vec(ln_g[1, 1]), "loss_ln11_bwd", tm=te,
                                               row_lo=N_META, row_hi=N_META + seq)
    (dz3, dz3b, dg10, db10), dw_in, dw_out, dcw1, dcb1 = ffn_bwd(dz4, dz4b, h3b, u1, act1, 1, "1", None, None,
                                                                 xh3, rs3, vec(ln_g[1, 0]))

    dob, delta = _mm(dz3b, W["w_o"][0], "nt", BF16, "o_proj_dx", tm=tm, tn=D, tk=tD, head_dots_with=o_tok)
    dw_o = _mm(ob, dz3b, "tn", F32, "o_proj_dw", tm=tD, tn=tD, tk=tm)
    dqb, dkb, dvb, dcs, dcq = _attn_bwd(qb, dob, kvb, lse4, delta[:, :H].T.reshape(H, nq, 1, tm), crow4,
                                        "attn_bwd", tq=tm)
    dc_keys = jnp.pad(dcs.reshape(H, Lp).T, ((0, 0), (0, LANES - H)))
    dc_queries = jnp.pad(dcq.reshape(H, Lp).T, ((0, 0), (0, LANES - H)))
    dpreb, dbf = _logf_bwd(dc_keys, dc_queries, pre, bf_pad, "logf_bwd", tm=tm)

    qs = HEAD_DIM ** -0.5
    dw_q = _mm(h2b, dqb, "tn", F32, "q_proj_dw", tm=tD, tn=tD, tk=tm, scale=qs)
    dw_kv = _mm(h2b, dkb, "tn", F32, "k_proj_dw", tm=tD, tn=t2Dc, tk=tm, chips=(0, N_CHIPS // 2))
    dw_kv = _mm(h2b, dvb, "tn", F32, "v_proj_dw", tm=tD, tn=t2Dc, tk=tm, chips=(N_CHIPS // 2, N_CHIPS // 2), into=dw_kv)
    dw_f = _mm(h2b, dpreb, "tn", F32, "f_proj_dw", tm=tD, tn=LANES, tk=tm)[:, :H]
    dh2 = _mm(dqb, W["w_q"][0], "nt", F32, "q_proj_dx", tm=tm, tn=tD, tk=tD, scale=qs)
    dh2 = _mm(dkb, W["w_kv"][:, :D], "nt", F32, "k_proj_dx", tm=tm, tn=tD, tk=tD, add=dh2)
    dh2 = _mm(dvb, W["w_kv"][:, D:], "nt", F32, "v_proj_dx", tm=tm, tn=tD, tk=tD, add=dh2)
    dz2, dz2b, dg01, db01 = _mm_ln(dpreb, wf_pad, "nt", "f_proj_dx_ln", tm=tm, tk=LANES, forward=False,
                                   rows=[dz3, xh2, rs2], vecs=[vec(ln_g[0, 1])], add=dh2)

    (dz1, _, dg00, db00), dw_in, dw_out, dcw0, dcb0 = ffn_bwd(dz2, dz2b, h1b, u0, act0, 0, "0", dw_in, dw_out,
                                                              xh1, rs1, vec(ln_g[0, 0]))
    dh0, dmb, dscale = _pool_bwd(dz1, mixpre, W["pool_w"][0], W["pool_scale"], "pool_bwd", tm=te)
    dw_pool = _pool_dw(diffb, dmb, "pool_dw", tk=tm)

    grads = {
        "meta": dh0[:N_META],
        "pool_w": dw_pool[None],
        "pool_scale": dscale,
        "w_kv": dw_kv,
        "w_f": dw_f,
        "b_f": dbf[0, :H],
        "w_q": dw_q[None],
        "w_o": dw_o[None],
        "ffn_w_in": dw_in,
        "ffn_conv_w": jnp.stack([dcw0, dcw1]),
        "ffn_conv_b": jnp.stack([dcb0, dcb1]),
        "ffn_w_out": dw_out,
        "ln_g": jnp.stack([jnp.stack([dg00[0], dg01[0]]), jnp.stack([dg10[0], dg11[0]])]),
        "ln_b": jnp.stack([jnp.stack([db00[0], db01[0]]), jnp.stack([db10[0], db11[0]])]),
    }
    return loss, dh0, grads


def _row_block(rows, cols):
    best = SUBLANES
    for t in range(SUBLANES, rows + 1, SUBLANES):
        if rows % t == 0 and t * cols * 4 <= ELEMENTWISE_BLOCK_BYTES:
            best = t
    return best


def _row_tile(length):
    return 640 if length >= 4096 else 128


def kernel(x, meta, pool_w, pool_scale, w_kv, w_f, b_f, w_q, w_o, ffn_w_in, ffn_conv_w, ffn_conv_b, ffn_w_out, ln_g, ln_b, loss_target, m_meta, m_pool_w, m_pool_scale, m_w_kv, m_w_f, m_b_f, m_w_q, m_w_o, m_ffn_w_in, m_ffn_conv_w, m_ffn_conv_b, m_ffn_w_out, m_ln_g, m_ln_b, v_meta, v_pool_w, v_pool_scale, v_w_kv, v_w_f, v_b_f, v_w_q, v_w_o, v_ffn_w_in, v_ffn_conv_w, v_ffn_conv_b, v_ffn_w_out, v_ln_g, v_ln_b):
    weights = dict(meta=meta, pool_w=pool_w, pool_scale=pool_scale, w_kv=w_kv, w_f=w_f, b_f=b_f, w_q=w_q, w_o=w_o,
                   ffn_w_in=ffn_w_in, ffn_conv_w=ffn_conv_w, ffn_conv_b=ffn_conv_b, ffn_w_out=ffn_w_out,
                   ln_g=ln_g, ln_b=ln_b)
    mom1 = dict(meta=m_meta, pool_w=m_pool_w, pool_scale=m_pool_scale, w_kv=m_w_kv, w_f=m_w_f, b_f=m_b_f, w_q=m_w_q,
                w_o=m_w_o, ffn_w_in=m_ffn_w_in, ffn_conv_w=m_ffn_conv_w, ffn_conv_b=m_ffn_conv_b,
                ffn_w_out=m_ffn_w_out, ln_g=m_ln_g, ln_b=m_ln_b)
    mom2 = dict(meta=v_meta, pool_w=v_pool_w, pool_scale=v_pool_scale, w_kv=v_w_kv, w_f=v_w_f, b_f=v_b_f, w_q=v_w_q,
                w_o=v_w_o, ffn_w_in=v_ffn_w_in, ffn_conv_w=v_ffn_conv_w, ffn_conv_b=v_ffn_conv_b,
                ffn_w_out=v_ffn_w_out, ln_g=v_ln_g, ln_b=v_ln_b)
    _, seq, D = x.shape
    L = N_META + seq
    tm = _row_tile(L)
    Lp = _round_up(L, tm)
    c_idx = lax.axis_index("c")
    chip = 2 * lax.axis_index("x") + lax.axis_index("y")

    shard_shapes = {n: weights[n].shape for n in SHARDED}
    rows_b = _round_up(sum(_rows_of(shard_shapes[n]) for n in MATMUL_WEIGHTS), 32)
    rows_f = _round_up(sum(_rows_of(shard_shapes[n]) for n in VECTOR_WEIGHTS), SUBLANES)
    wb = _pack([weights[n] for n in MATMUL_WEIGHTS], rows_b, BF16)
    wf = _pack([weights[n] for n in VECTOR_WEIGHTS], rows_f, F32)
    gb, gf = _all_gather_weights(wb, wf, "weights_all_gather")
    gb = lax.dynamic_update_index_in_dim(gb, wb, chip, axis=0)
    gf = lax.dynamic_update_index_in_dim(gf, wf, chip, axis=0)
    full = {}
    for names, buf in ((MATMUL_WEIGHTS, gb), (VECTOR_WEIGHTS, gf)):
        for n, stacked in zip(names, _unpack(buf, [shard_shapes[n] for n in names])):
            full[n] = _join_shards(stacked, SHARD_AXIS[n])
    full["b_f"] = b_f
    full["ffn_conv_b"] = ffn_conv_b

    pad = jnp.zeros((Lp - L, D), F32)
    h0 = jnp.concatenate([full["meta"], x[0], pad], axis=0)
    tgt = jnp.concatenate([jnp.zeros((N_META, D), F32), loss_target[0], pad], axis=0)
    loss, dh0, grads = _local_step(h0, tgt, full, seq=seq, tm=tm)
    loss = lax.psum(loss[0, 0], AXES)
    grad_x = dh0[N_META:L][None]

    core = c_idx.astype(jnp.int32).reshape(1)
    place = jnp.stack([c_idx, chip]).astype(jnp.int32)
    small_shapes = [shard_shapes[n] for n in SMALL_SHARDED]
    rows_s = _round_up(sum(_rows_of(s) for s in small_shapes), 2 * LANES)

    def packed_small(d):
        return _pack([d[n] for n in SMALL_SHARDED], rows_s, F32).reshape(2, rows_s // 2, PACK_COLS)

    names, orders, wires, g_views, wmv = [], [], [], [], []
    for n, order in BIG_SHARDED:
        shp = shard_shapes[n]
        C = shp[-1]
        R = weights[n].size // C // 2
        lead = (N_CHIPS, 2) if order == "CH" else (2, N_CHIPS)
        names.append(n)
        orders.append(order)
        wires.append(BF16)
        g_views.append(grads[n].reshape(lead + (R, C)))
        wmv.append([d[n].reshape(2, R, C) for d in (weights, mom1, mom2)])
    names.append("small")
    orders.append("CH")
    wires.append(F32)
    g_views.append(jnp.stack([_pack([_shard_of(grads[n], SHARD_AXIS[n], s) for n in SMALL_SHARDED], rows_s, F32)
                              for s in range(N_CHIPS)]).reshape(N_CHIPS, 2, rows_s // 2, PACK_COLS))
    wmv.append([packed_small(d) for d in (weights, mom1, mom2)])

    from_sibling = _halves_to_sibling(g_views, orders, "grads_to_sibling")
    parts, on_wire = [], []
    for n, order, wire, g, a in zip(names, orders, wires, g_views, from_sibling):
        p, pw = _chip_partial(g, a, core, order, wire, f"chip_sum_{n}", tr=_row_block(a.shape[1], a.shape[2]))
        parts.append(p)
        on_wire.append(pw)

    rep_shapes = [weights[n].shape for n in REPLICATED]
    rows_r = _round_up(sum(_rows_of(s) for s in rep_shapes), SUBLANES)
    rep = _pack([grads[n] for n in REPLICATED], rows_r, F32)
    *landed, reps = _chip_exchange(on_wire, rep, "grads_chip_exchange")

    halves = []
    for n, p, b, (w_, m_, v_) in zip(names, parts, landed, wmv):
        halves += _adamw_owned(p, b, w_, m_, v_, place, f"adamw_{n}", tr=_row_block(p.shape[1], p.shape[2]))
    joined = _join_halves(halves, "results_to_sibling")
    out = {}
    for k, n in enumerate(names[:-1]):
        out[n] = [a.reshape(shard_shapes[n]) for a in joined[4 * k:4 * k + 4]]
    small_out = [_unpack(a.reshape(rows_s, PACK_COLS), small_shapes) for a in joined[-4:]]
    for k, n in enumerate(SMALL_SHARDED):
        out[n] = [small_out[kind][k] for kind in range(4)]

    def packr(d):
        return _pack([d[n] for n in REPLICATED], rows_r, F32)

    res_r = _sum_adamw(reps[0], reps[1:], packr(weights), packr(mom1), packr(mom2), "adamw_replicated", tr=rows_r)
    rep_out = _unpack(res_r, rep_shapes)

    out.update({n: a for n, a in zip(REPLICATED, rep_out)})
    result = [loss, grad_x]
    for k in range(4):
        result += [out[n][k] for n in WEIGHT_ORDER]
    return tuple(result)
```

```python
import functools

import jax
import jax.numpy as jnp
from jax import lax
from jax.experimental import pallas as pl
from jax.experimental.pallas import tpu as pltpu

N_META = 16
POOL_WINDOWS = (2, 4, 8, 16)
MAX_WINDOW = max(POOL_WINDOWS)
N_GROUPS = len(POOL_WINDOWS)
HEAD_DIM = 64
DEPTH = 2
CONV_WIDTH = 3
ALPHA = (2.0 * DEPTH) ** 0.25
LN_EPS = 1e-5
NEG_INF = -1e30
ADAM_LR = 0.001
ADAM_B1 = 0.9
ADAM_B2 = 0.999
ADAM_EPS = 1e-08
ADAM_WD = 0.01
ADAM_STEP = 10

F32 = jnp.float32
BF16 = jnp.bfloat16
ATTN_FWD_PAIRS = 2
BIAS_SLOTS = 6
ATTN_STRIP = 32
GLU_STRIP = 16
LANES = 128
SUBLANES = 8
PACK_COLS = 1024
VMEM_LIMIT = 56 * 1024 * 1024
AXES = ("x", "y", "c")
MESH = pl.DeviceIdType.MESH

NN = (((1,), (0,)), ((), ()))
NT = (((1,), (1,)), ((), ()))
TN = (((0,), (0,)), ((), ()))

SHARD_AXIS = {"meta": 1, "pool_w": 2, "pool_scale": 1, "w_kv": 1, "w_f": 0, "w_q": 1, "w_o": 1,
              "ffn_w_in": 2, "ffn_conv_w": 2, "ffn_w_out": 1, "ln_g": 2, "ln_b": 2}
SHARDED = ("meta", "pool_w", "pool_scale", "w_kv", "w_f", "w_q", "w_o", "ffn_w_in", "ffn_conv_w",
           "ffn_w_out", "ln_g", "ln_b")
REPLICATED = ("b_f", "ffn_conv_b")
MATMUL_WEIGHTS = ("pool_w", "w_kv", "w_f", "w_q", "w_o", "ffn_w_in", "ffn_w_out")
VECTOR_WEIGHTS = ("meta", "pool_scale", "ffn_conv_w", "ln_g", "ln_b")
WEIGHT_ORDER = ("meta", "pool_w", "pool_scale", "w_kv", "w_f", "b_f", "w_q", "w_o", "ffn_w_in",
                "ffn_conv_w", "ffn_conv_b", "ffn_w_out", "ln_g", "ln_b")
BIG_SHARDED = (("w_kv", "CH"), ("w_q", "CH"), ("w_o", "CH"), ("ffn_w_in", "HC"), ("ffn_w_out", "HC"))
SMALL_SHARDED = ("meta", "pool_w", "pool_scale", "w_f", "ffn_conv_w", "ln_g", "ln_b")
ELEMENTWISE_BLOCK_BYTES = 3 * 512 * 1024
N_CHIPS = 4
N_DEV = 8


def _cparams(*sem):
    return pltpu.CompilerParams(dimension_semantics=sem, vmem_limit_bytes=VMEM_LIMIT)


def _round_up(n, m):
    return (n + m - 1) // m * m


def _pick(n, cap):
    if n <= cap:
        return n
    best = 0
    for t in range(LANES, cap + 1, LANES):
        if n % t == 0:
            best = t
    assert best, (n, cap)
    return best


def _mm(a, b, mode, out_dtype, name, *, tm, tn, tk, scale=None, add=None, chips=False, layer=None, into=None,
        b_halves=False, head_dots_with=None):
    if mode == "nn":
        (M, K), N = a.shape, b.shape[1]
    elif mode == "nt":
        (M, K), N = a.shape, b.shape[0]
    elif b_halves:
        (K, M), N = a.shape, 2 * b.shape[2]
    else:
        (K, M), N = a.shape, b.shape[1]
    assert M % tm == 0 and N % tn == 0 and K % tk == 0, (name, M, N, K, tm, tn, tk)
    nk = K // tk
    dn = {"nn": NN, "nt": NT, "tn": TN}[mode]
    has_add = add is not None
    has_into = into is not None
    has_dots = head_dots_with is not None
    assert not (has_add and (chips or layer is not None))
    assert not has_dots or (tn == N and not (chips or layer is not None or has_into))

    def body(*refs):
        a_ref, b_ref = refs[0], refs[1]
        add_ref = refs[2] if has_add else None
        with_ref = refs[2 + has_add + has_into] if has_dots else None
        o_ref = refs[2 + has_add + has_into + has_dots]
        dots_ref = refs[3 + has_add + has_into + has_dots] if has_dots else None
        acc_ref = refs[-1] if nk > 1 else None
        k = pl.program_id(2)
        part = lax.dot_general(a_ref[...], b_ref[0] if b_halves else b_ref[...], dn, preferred_element_type=F32)

        def finish(r):
            if scale is not None:
                r = r * scale
            if has_add:
                r = r + add_ref[...]
            out = r.astype(out_dtype)
            o_ref[...] = out.reshape(o_ref.shape)
            if has_dots:
                sel = (lax.broadcasted_iota(jnp.int32, (N, LANES), 0) // HEAD_DIM
                       == lax.broadcasted_iota(jnp.int32, (N, LANES), 1)).astype(F32)
                dots_ref[...] = jnp.dot(out.astype(F32) * with_ref[...], sel, precision=lax.Precision.HIGHEST,
                                        preferred_element_type=F32)

        if nk == 1:
            finish(part)
        else:
            @pl.when(k == 0)
            def _():
                acc_ref[...] = part

            @pl.when(k > 0)
            def _():
                acc_ref[...] += part

            @pl.when(k == nk - 1)
            def _():
                finish(acc_ref[...])

    if mode == "nn":
        a_spec = pl.BlockSpec((tm, tk), lambda j, i, k: (i, k))
        b_spec = pl.BlockSpec((tk, tn), lambda j, i, k: (k, j))
    elif mode == "nt":
        a_spec = pl.BlockSpec((tm, tk), lambda j, i, k: (i, k))
        b_spec = pl.BlockSpec((tn, tk), lambda j, i, k: (j, k))
    else:
        a_spec = pl.BlockSpec((tk, tm), lambda j, i, k: (k, i))
        b_spec = pl.BlockSpec((tk, tn), lambda j, i, k: (k, j))
        if b_halves:
            per_half = N // 2 // tn
            b_spec = pl.BlockSpec((1, tk, tn), lambda j, i, k: (j // per_half, k, j % per_half))
    out_dims, blk = (M, N), (tm, tn)
    if chips:
        base, count = (0, N_CHIPS) if chips is True else chips
        per_chip = N // count // tn
        assert per_chip * tn * count == N, (name, N, tn)
        out_dims, blk = (N_CHIPS, M, N // count), (1, tm, tn)
        where = lambda j, i: (base + j // per_chip, i, j % per_chip)
    else:
        where = lambda j, i: (i, j)
    if layer is not None:
        out_dims, blk = (DEPTH,) + out_dims, (1,) + blk
        o_spec = pl.BlockSpec(blk, lambda j, i, k: (layer,) + where(j, i))
    else:
        o_spec = pl.BlockSpec(blk, lambda j, i, k: where(j, i))
    in_specs = ([a_spec, b_spec] + ([o_spec] if has_add else []) + ([pl.BlockSpec(memory_space=pl.ANY)] if has_into else [])
                + ([o_spec] if has_dots else []))
    args = (a, b) + ((add,) if has_add else ()) + ((into,) if has_into else ()) + ((head_dots_with,) if has_dots else ())
    out_specs, out_shape = o_spec, jax.ShapeDtypeStruct(out_dims, out_dtype)
    if has_dots:
        out_specs = [o_spec, pl.BlockSpec((tm, LANES), lambda j, i, k: (i, 0))]
        out_shape = [out_shape, jax.ShapeDtypeStruct((M, LANES), F32)]
    return pl.pallas_call(
        body, name=name, grid=(N // tn, M // tm, nk),
        in_specs=in_specs, out_specs=out_specs,
        out_shape=out_shape,
        input_output_aliases={2 + has_add: 0} if has_into else {},
        scratch_shapes=[pltpu.VMEM((tm, tn), F32)] if nk > 1 else [],
        compiler_params=_cparams("parallel", "parallel", "arbitrary"),
    )(*args)


def _ln_math(z, g, b):
    mu = jnp.mean(z, axis=-1, keepdims=True)
    zc = z - mu
    var = jnp.mean(zc * zc, axis=-1, keepdims=True)
    rstd = lax.rsqrt(var + LN_EPS)
    xh = zc * rstd
    return xh * g + b, xh, rstd


def _mm_ln(a, b, mode, name, *, tm, tk, forward, rows, vecs, scale=None, add=None, a_halves=False):
    assert mode in ("nn", "nt")
    M, K = (a.shape[1], 2 * a.shape[2]) if a_halves else a.shape
    N = b.shape[1] if mode == "nn" else b.shape[0]
    assert M % tm == 0 and K % tk == 0, (name, M, K, tm, tk)
    nk = K // tk
    ni = M // tm
    dn = {"nn": NN, "nt": NT}[mode]
    has_add = add is not None
    n_in = 2 + has_add + len(rows) + len(vecs)

    def body(*refs):
        a_ref, b_ref = refs[0], refs[1]
        add_ref = refs[2] if has_add else None
        row_refs = refs[2 + has_add:2 + has_add + len(rows)]
        vec_refs = refs[2 + has_add + len(rows):n_in]
        outs = refs[n_in:n_in + 4]
        acc_ref = refs[-1] if nk > 1 else None
        i, k = pl.program_id(0), pl.program_id(1)
        part = lax.dot_general(a_ref[0] if a_halves else a_ref[...], b_ref[...], dn, preferred_element_type=F32)

        def finish(y):
            if scale is not None:
                y = y * scale
            if has_add:
                y = y + add_ref[...]
            if forward:
                h, xh, rstd = _ln_math(ALPHA * row_refs[0][...] + y, vec_refs[0][...], vec_refs[1][...])
                outs[0][...] = h
                outs[1][...] = h.astype(BF16)
                outs[2][...] = xh
                outs[3][...] = rstd
            else:
                dy = ALPHA * row_refs[0][...] + y
                x = row_refs[1][...]
                dxh = dy * vec_refs[0][...]
                m1 = jnp.mean(dxh, axis=-1, keepdims=True)
                m2 = jnp.mean(dxh * x, axis=-1, keepdims=True)
                dz = row_refs[2][...] * (dxh - m1 - x * m2)
                outs[0][...] = dz
                outs[1][...] = dz.astype(BF16)

                @pl.when(i == 0)
                def _():
                    outs[2][...] = jnp.zeros_like(outs[2])
                    outs[3][...] = jnp.zeros_like(outs[3])

                outs[2][...] += jnp.sum(dy * x, axis=0, keepdims=True)
                outs[3][...] += jnp.sum(dy, axis=0, keepdims=True)

        if nk == 1:
            finish(part)
        else:
            @pl.when(k == 0)
            def _():
                acc_ref[...] = part

            @pl.when(k > 0)
            def _():
                acc_ref[...] += part

            @pl.when(k == nk - 1)
            def _():
                finish(acc_ref[...])

    a_spec = pl.BlockSpec((tm, tk), lambda i, k: (i, k))
    if a_halves:
        per_half = nk // 2
        a_spec = pl.BlockSpec((1, tm, tk), lambda i, k: (k // per_half, i, k % per_half))
    b_spec = pl.BlockSpec((tk, N), lambda i, k: (k, 0)) if mode == "nn" else pl.BlockSpec((N, tk), lambda i, k: (0, k))
    row = pl.BlockSpec((tm, N), lambda i, k: (i, 0))
    col = pl.BlockSpec((tm, 1), lambda i, k: (i, 0))
    vec = pl.BlockSpec((1, N), lambda i, k: (0, 0))
    row_specs = [row if r.shape[1] == N else col for r in rows]
    if forward:
        out_specs = [row, row, row, col]
        out_shape = [jax.ShapeDtypeStruct((M, N), F32), jax.ShapeDtypeStruct((M, N), BF16),
                     jax.ShapeDtypeStruct((M, N), F32), jax.ShapeDtypeStruct((M, 1), F32)]
    else:
        out_specs = [row, row, vec, vec]
        out_shape = [jax.ShapeDtypeStruct((M, N), F32), jax.ShapeDtypeStruct((M, N), BF16),
                     jax.ShapeDtypeStruct((1, N), F32), jax.ShapeDtypeStruct((1, N), F32)]
    args = (a, b) + ((add,) if has_add else ()) + tuple(rows) + tuple(vecs)
    return pl.pallas_call(
        body, name=name, grid=(ni, nk),
        in_specs=[a_spec, b_spec] + ([row] if has_add else []) + row_specs + [vec] * len(vecs),
        out_specs=out_specs, out_shape=out_shape,
        scratch_shapes=[pltpu.VMEM((tm, N), F32)] if nk > 1 else [],
        compiler_params=_cparams("parallel" if forward else "arbitrary", "arbitrary"),
    )(*args)


def _pool_ln_fwd(h0, pw, ps, g, b, name, *, tm):
    Lp, D = h0.shape
    G = D // N_GROUPS
    halo_blocks = tm // MAX_WINDOW

    def body(x_ref, halo_ref, pw_ref, ps_ref, g_ref, b_ref,
             diff_ref, mix_ref, h_ref, hb_ref, xh_ref, rs_ref, ext_ref):
        i = pl.program_id(0)
        ext_ref[0:MAX_WINDOW, :] = jnp.where(i == 0, 0.0, halo_ref[...])
        ext_ref[MAX_WINDOW:MAX_WINDOW + tm, :] = x_ref[...]
        t1 = (i * tm + 1 + lax.broadcasted_iota(jnp.int32, (tm, 1), 0)).astype(F32)
        for gi, w in enumerate(POOL_WINDOWS):
            lo, hi = gi * G, (gi + 1) * G
            xg = x_ref[:, lo:hi]
            win = xg
            for j in range(1, w):
                win = win + ext_ref[MAX_WINDOW - j:MAX_WINDOW - j + tm, lo:hi]
            d = (win / jnp.minimum(t1, float(w)) - xg).astype(BF16)
            diff_ref[:, lo:hi] = d
            mix_ref[:, lo:hi] = jnp.dot(d, pw_ref[gi], preferred_element_type=F32)
        z = ALPHA * x_ref[...] + mix_ref[...] * ps_ref[...]
        h, xh, rstd = _ln_math(z, g_ref[...], b_ref[...])
        h_ref[...] = h
        hb_ref[...] = h.astype(BF16)
        xh_ref[...] = xh
        rs_ref[...] = rstd

    row = pl.BlockSpec((tm, D), lambda i: (i, 0))
    vec = pl.BlockSpec((1, D), lambda i: (0, 0))
    return pl.pallas_call(
        body, name=name, grid=(Lp // tm,),
        in_specs=[row,
                  pl.BlockSpec((MAX_WINDOW, D), lambda i: (jnp.maximum(i * halo_blocks - 1, 0), 0)),
                  pl.BlockSpec((N_GROUPS, G, G), lambda i: (0, 0, 0)), vec, vec, vec],
        out_specs=[row, row, row, row, row, pl.BlockSpec((tm, 1), lambda i: (i, 0))],
        out_shape=[jax.ShapeDtypeStruct((Lp, D), BF16), jax.ShapeDtypeStruct((Lp, D), F32),
                   jax.ShapeDtypeStruct((Lp, D), F32), jax.ShapeDtypeStruct((Lp, D), BF16),
                   jax.ShapeDtypeStruct((Lp, D), F32), jax.ShapeDtypeStruct((Lp, 1), F32)],
        scratch_shapes=[pltpu.VMEM((tm + MAX_WINDOW, D), F32)],
        compiler_params=_cparams("parallel"),
    )(h0, h0, pw, ps, g, b)


def _pool_bwd(dz, mixpre, pw, ps, name, *, tm):
    Lp, D = dz.shape
    G = D // N_GROUPS
    halo_blocks = tm // MAX_WINDOW
    n_halo = Lp // MAX_WINDOW
    ni = Lp // tm
    R = tm + MAX_WINDOW

    def body(dz_ref, halo_ref, mix_ref, pw_ref, ps_ref, dh_ref, dmb_ref, dsc_ref, ext_ref, dp_ref):
        i = pl.program_id(0)
        ext_ref[0:tm, :] = dz_ref[...]
        ext_ref[tm:R, :] = jnp.where(i == ni - 1, 0.0, halo_ref[...])
        dmix = (ext_ref[...] * ps_ref[...]).astype(BF16)
        dmb_ref[...] = dmix[0:tm]

        @pl.when(i == 0)
        def _():
            dsc_ref[...] = jnp.zeros_like(dsc_ref)

        dsc_ref[...] += jnp.sum(dz_ref[...] * mix_ref[...], axis=0, keepdims=True)
        t1 = (i * tm + 1 + lax.broadcasted_iota(jnp.int32, (R, 1), 0)).astype(F32)
        for gi, w in enumerate(POOL_WINDOWS):
            lo, hi = gi * G, (gi + 1) * G
            dd = lax.dot_general(dmix[:, lo:hi], pw_ref[gi], NT, preferred_element_type=F32)
            dp_ref[:, lo:hi] = dd / jnp.minimum(t1, float(w))
            back = dp_ref[0:tm, lo:hi]
            for j in range(1, w):
                back = back + dp_ref[j:j + tm, lo:hi]
            dh_ref[:, lo:hi] = ALPHA * dz_ref[:, lo:hi] - dd[0:tm] + back

    row = pl.BlockSpec((tm, D), lambda i: (i, 0))
    vec = pl.BlockSpec((1, D), lambda i: (0, 0))
    return pl.pallas_call(
        body, name=name, grid=(ni,),
        in_specs=[row,
                  pl.BlockSpec((MAX_WINDOW, D), lambda i: (jnp.minimum((i + 1) * halo_blocks, n_halo - 1), 0)),
                  row, pl.BlockSpec((N_GROUPS, G, G), lambda i: (0, 0, 0)), vec],
        out_specs=[row, row, vec],
        out_shape=[jax.ShapeDtypeStruct((Lp, D), F32), jax.ShapeDtypeStruct((Lp, D), BF16),
                   jax.ShapeDtypeStruct((1, D), F32)],
        scratch_shapes=[pltpu.VMEM((R, D), F32), pltpu.VMEM((R, D), F32)],
        compiler_params=_cparams("arbitrary"),
    )(dz, dz, mixpre, pw, ps)


def _pool_dw(diffb, dmb, name, *, tk):
    Lp, D = diffb.shape
    G = D // N_GROUPS

    def body(a_ref, b_ref, o_ref):
        @pl.when(pl.program_id(1) == 0)
        def _():
            o_ref[...] = jnp.zeros_like(o_ref)

        o_ref[0] += lax.dot_general(a_ref[...], b_ref[...], TN, preferred_element_type=F32)

    blk = pl.BlockSpec((tk, G), lambda g, k: (k, g))
    return pl.pallas_call(
        body, name=name, grid=(N_GROUPS, Lp // tk),
        in_specs=[blk, blk], out_specs=pl.BlockSpec((1, G, G), lambda g, k: (g, 0, 0)),
        out_shape=jax.ShapeDtypeStruct((N_GROUPS, G, G), F32),
        compiler_params=_cparams("parallel", "arbitrary"),
    )(diffb, dmb)


def _loss_ln_bwd(h, tgt, xh, rs, g, name, *, tm, row_lo, row_hi):
    Lp, D = h.shape

    def body(h_ref, t_ref, xh_ref, rs_ref, g_ref, dz_ref, dzb_ref, dg_ref, db_ref, loss_ref):
        i = pl.program_id(0)
        r = i * tm + lax.broadcasted_iota(jnp.int32, (tm, 1), 0)
        valid = (r >= row_lo) & (r < row_hi)
        e = jnp.where(valid, h_ref[...] - t_ref[...], 0.0)
        dy = e * (1.0 / D)
        x = xh_ref[...]
        dxh = dy * g_ref[...]
        m1 = jnp.mean(dxh, axis=-1, keepdims=True)
        m2 = jnp.mean(dxh * x, axis=-1, keepdims=True)
        dz = rs_ref[...] * (dxh - m1 - x * m2)
        dz_ref[...] = dz
        dzb_ref[...] = dz.astype(BF16)

        @pl.when(i == 0)
        def _():
            dg_ref[...] = jnp.zeros_like(dg_ref)
            db_ref[...] = jnp.zeros_like(db_ref)
            loss_ref[...] = jnp.zeros_like(loss_ref)

        dg_ref[...] += jnp.sum(dy * x, axis=0, keepdims=True)
        db_ref[...] += jnp.sum(dy, axis=0, keepdims=True)
        loss_ref[...] += 0.5 * jnp.sum(jnp.mean(e * e, axis=-1, keepdims=True), axis=0, keepdims=True)

    row = pl.BlockSpec((tm, D), lambda i: (i, 0))
    vec = pl.BlockSpec((1, D), lambda i: (0, 0))
    return pl.pallas_call(
        body, name=name, grid=(Lp // tm,),
        in_specs=[row, row, row, pl.BlockSpec((tm, 1), lambda i: (i, 0)), vec],
        out_specs=[row, row, vec, vec, pl.BlockSpec((1, 1), lambda i: (0, 0))],
        out_shape=[jax.ShapeDtypeStruct((Lp, D), F32), jax.ShapeDtypeStruct((Lp, D), BF16),
                   jax.ShapeDtypeStruct((1, D), F32), jax.ShapeDtypeStruct((1, D), F32),
                   jax.ShapeDtypeStruct((1, 1), F32)],
        compiler_params=_cparams("arbitrary"),
    )(h, tgt, xh, rs, g)


def _shift_rows_down(cur, prev, s, sub):
    return jnp.where(sub >= s, pltpu.roll(cur, s, 0), pltpu.roll(prev, s, 0))


def _shift_rows_up(cur, nxt, s, sub):
    return jnp.where(sub < SUBLANES - s, pltpu.roll(cur, SUBLANES - s, 0), pltpu.roll(nxt, SUBLANES - s, 0))


def _conv_group(cur, prev, cw_ref, cb_ref, sub):
    taps = [_shift_rows_down(cur, prev, 2, sub), _shift_rows_down(cur, prev, 1, sub), cur]
    c = cb_ref[...] + cw_ref[0:1, :] * taps[0] + cw_ref[1:2, :] * taps[1] + cw_ref[2:3, :] * taps[2]
    return c, taps


def _conv_glu_fwd(u, cw, cb, name, *, tm, tn):
    Lp, F2 = u.shape
    F = F2 // 2
    nj = F // tn
    halo_blocks = tm // SUBLANES
    S8 = SUBLANES
    assert GLU_STRIP == 2 * S8 and tm % GLU_STRIP == 0

    def body(ua_ref, ug_ref, pa_ref, pg_ref, cwa_ref, cwg_ref, cba_ref, cbg_ref, o_ref):
        first = pl.program_id(1) == 0
        sub = lax.broadcasted_iota(jnp.int32, (S8, tn), 0)

        def strip(r, prev_a, prev_g):
            out = []
            for g0 in (0, S8):
                a_cur = ua_ref[pl.ds(r + g0, S8), :]
                g_cur = ug_ref[pl.ds(r + g0, S8), :]
                a, _ = _conv_group(a_cur, prev_a, cwa_ref, cba_ref, sub)
                gate, _ = _conv_group(g_cur, prev_g, cwg_ref, cbg_ref, sub)
                out.append(a * jax.nn.sigmoid(a) * gate)
                prev_a, prev_g = a_cur, g_cur
            o_ref[pl.ds(r, GLU_STRIP), :] = jnp.concatenate(out, axis=0).astype(BF16)

        strip(0, jnp.where(first, 0.0, pa_ref[...]), jnp.where(first, 0.0, pg_ref[...]))

        def step(k, carry):
            r = pl.multiple_of(k * GLU_STRIP, GLU_STRIP)
            before = pl.ds(pl.multiple_of(r - S8, S8), S8)
            strip(r, ua_ref[before, :], ug_ref[before, :])
            return carry

        lax.fori_loop(1, tm // GLU_STRIP, step, 0)

    def prev(off):
        return pl.BlockSpec((SUBLANES, tn), lambda j, i: (jnp.maximum(i * halo_blocks - 1, 0), j + off))

    def cols(rows, off):
        return pl.BlockSpec((rows, tn), lambda j, i: (0, j + off))

    return pl.pallas_call(
        body, name=name, grid=(nj, Lp // tm),
        in_specs=[pl.BlockSpec((tm, tn), lambda j, i: (i, j)), pl.BlockSpec((tm, tn), lambda j, i: (i, j + nj)),
                  prev(0), prev(nj), cols(CONV_WIDTH, 0), cols(CONV_WIDTH, nj), cols(1, 0), cols(1, nj)],
        out_specs=pl.BlockSpec((tm, tn), lambda j, i: (i, j)),
        out_shape=jax.ShapeDtypeStruct((Lp, F), BF16),
        compiler_params=_cparams("parallel", "parallel"),
    )(u, u, u, u, cw, cw, cb, cb)


def _conv_glu_bwd(u, dact, cw, cb, name, *, tm, tn):
    Lp, F2 = u.shape
    F = F2 // 2
    nj = F // tn
    ni = Lp // tm
    halo_blocks = tm // SUBLANES
    n_halo = Lp // SUBLANES
    S8 = SUBLANES
    n_strips = tm // GLU_STRIP
    assert GLU_STRIP == 2 * S8 and tm % GLU_STRIP == 0

    def body(ua_ref, ug_ref, pa_ref, pg_ref, na_ref, ng_ref, da_ref, dn_ref,
             cwa_ref, cwg_ref, cba_ref, cbg_ref,
             du_ref, dwa_ref, dwg_ref, dba_ref, dbg_ref,
             wacc_a, wacc_g, bacc_a, bacc_g):
        i = pl.program_id(1)
        first, last = i == 0, i == ni - 1
        sub = lax.broadcasted_iota(jnp.int32, (S8, tn), 0)
        for acc in (wacc_a, wacc_g, bacc_a, bacc_g):
            acc[...] = jnp.zeros_like(acc)

        def dconv(a_cur, a_prev, g_cur, g_prev, dact_rows):
            a, taps_a = _conv_group(a_cur, a_prev, cwa_ref, cba_ref, sub)
            gate, taps_g = _conv_group(g_cur, g_prev, cwg_ref, cbg_ref, sub)
            sg = jax.nn.sigmoid(a)
            dca = dact_rows * gate * (sg * (1.0 + a * (1.0 - sg)))
            dcg = dact_rows * (a * sg)
            return dca, dcg, taps_a, taps_g

        def du_group(dc, dc_after, cw_ref):
            return (cw_ref[2:3, :] * dc + cw_ref[1:2, :] * _shift_rows_up(dc, dc_after, 1, sub)
                    + cw_ref[0:1, :] * _shift_rows_up(dc, dc_after, 2, sub))

        def strip(r, a_prev, g_prev, dca_after, dcg_after):
            a0, a1 = ua_ref[pl.ds(r, S8), :], ua_ref[pl.ds(r + S8, S8), :]
            g0, g1 = ug_ref[pl.ds(r, S8), :], ug_ref[pl.ds(r + S8, S8), :]
            dca1, dcg1, ta1, tg1 = dconv(a1, a0, g1, g0, da_ref[pl.ds(r + S8, S8), :])
            dca0, dcg0, ta0, tg0 = dconv(a0, a_prev, g0, g_prev, da_ref[pl.ds(r, S8), :])
            du_ref[0, pl.ds(r, GLU_STRIP), :] = jnp.concatenate(
                [du_group(dca0, dca1, cwa_ref), du_group(dca1, dca_after, cwa_ref)], axis=0).astype(BF16)
            du_ref[1, pl.ds(r, GLU_STRIP), :] = jnp.concatenate(
                [du_group(dcg0, dcg1, cwg_ref), du_group(dcg1, dcg_after, cwg_ref)], axis=0).astype(BF16)
            for k in range(CONV_WIDTH):
                wacc_a[k] += dca0 * ta0[k] + dca1 * ta1[k]
                wacc_g[k] += dcg0 * tg0[k] + dcg1 * tg1[k]
            bacc_a[...] += dca0 + dca1
            bacc_g[...] += dcg0 + dcg1
            return dca0, dcg0

        tail = pl.ds(tm - S8, S8)
        dca_after, dcg_after, _, _ = dconv(na_ref[...], ua_ref[tail, :], ng_ref[...], ug_ref[tail, :],
                                           jnp.where(last, 0.0, dn_ref[...]))

        def step(t, carry):
            r = pl.multiple_of((n_strips - 1 - t) * GLU_STRIP, GLU_STRIP)
            before = pl.ds(pl.multiple_of(r - S8, S8), S8)
            return strip(r, ua_ref[before, :], ug_ref[before, :], *carry)

        dca_after, dcg_after = lax.fori_loop(0, n_strips - 1, step, (dca_after, dcg_after))
        strip(0, jnp.where(first, 0.0, pa_ref[...]), jnp.where(first, 0.0, pg_ref[...]), dca_after, dcg_after)

        @pl.when(first)
        def _():
            for r in (dwa_ref, dwg_ref, dba_ref, dbg_ref):
                r[...] = jnp.zeros_like(r)

        for wacc, bacc, dw_ref, db_ref in ((wacc_a, bacc_a, dwa_ref, dba_ref), (wacc_g, bacc_g, dwg_ref, dbg_ref)):
            db_ref[...] += jnp.sum(bacc[...], axis=0, keepdims=True)
            for k in range(CONV_WIDTH):
                dw_ref[k:k + 1, :] += jnp.sum(wacc[k], axis=0, keepdims=True)

    def tile(off):
        return pl.BlockSpec((tm, tn), lambda j, i: (i, j + off))

    def prev(off):
        return pl.BlockSpec((S8, tn), lambda j, i: (jnp.maximum(i * halo_blocks - 1, 0), j + off))

    def nxt(off):
        return pl.BlockSpec((S8, tn), lambda j, i: (jnp.minimum((i + 1) * halo_blocks, n_halo - 1), j + off))

    def cols(rows, off):
        return pl.BlockSpec((rows, tn), lambda j, i: (0, j + off))

    return pl.pallas_call(
        body, name=name, grid=(nj, ni),
        in_specs=[tile(0), tile(nj), prev(0), prev(nj), nxt(0), nxt(nj), tile(0), nxt(0),
                  cols(CONV_WIDTH, 0), cols(CONV_WIDTH, nj), cols(1, 0), cols(1, nj)],
        out_specs=[pl.BlockSpec((2, tm, tn), lambda j, i: (0, i, j)),
                   cols(CONV_WIDTH, 0), cols(CONV_WIDTH, 0), cols(1, 0), cols(1, 0)],
        out_shape=[jax.ShapeDtypeStruct((2, Lp, F), BF16),
                   jax.ShapeDtypeStruct((CONV_WIDTH, F), F32), jax.ShapeDtypeStruct((CONV_WIDTH, F), F32),
                   jax.ShapeDtypeStruct((1, F), F32), jax.ShapeDtypeStruct((1, F), F32)],
        scratch_shapes=[pltpu.VMEM((CONV_WIDTH, S8, tn), F32), pltpu.VMEM((CONV_WIDTH, S8, tn), F32),
                        pltpu.VMEM((S8, tn), F32), pltpu.VMEM((S8, tn), F32)],
        compiler_params=_cparams("parallel", "arbitrary"),
    )(u, u, u, u, u, u, dact, dact, cw, cw, cb, cb)


def _bias_routing(n_heads, width, first_slot):
    h = lax.broadcasted_iota(jnp.int32, (LANES, width), 0)
    col = lax.broadcasted_iota(jnp.int32, (LANES, width), 1)
    base = LANES * (h // 2) + HEAD_DIM * (1 - h % 2) + first_slot
    return [((col == base + t) & (h < n_heads)).astype(BF16) for t in range(3)]


def _three_terms(x):
    hi = x.astype(BF16)
    r1 = x - hi.astype(F32)
    lo = r1.astype(BF16)
    lo2 = (r1 - lo.astype(F32)).astype(BF16)
    return hi, lo, lo2


def _logf_cumsum(pre, bf, name, *, tm, n_heads, width):
    Lp, W = pre.shape

    def body(p_ref, b_ref, c_ref, kx_ref, qx_ref, carry_ref):
        i = pl.program_id(0)

        @pl.when(i == 0)
        def _():
            carry_ref[...] = jnp.zeros_like(carry_ref)

        x = p_ref[...] + b_ref[...]
        lf = jnp.minimum(x, 0.0) - jnp.log(1.0 + jnp.exp(-jnp.abs(x)))
        tri = (lax.broadcasted_iota(jnp.int32, (tm, tm), 0) >= lax.broadcasted_iota(jnp.int32, (tm, tm), 1)).astype(F32)
        c = jnp.dot(tri, lf, precision=lax.Precision.HIGHEST, preferred_element_type=F32) + carry_ref[...]
        c_ref[...] = c
        carry_ref[...] = c[tm - 1:tm, :]
        terms = _three_terms(c)
        slot = lax.broadcasted_iota(jnp.int32, (tm, width), 1) % HEAD_DIM
        ones_k = ((slot >= 3) & (slot < BIAS_SLOTS)).astype(F32)
        ones_q = (slot < 3).astype(F32)
        kx = sum(jnp.dot(t, r, preferred_element_type=F32) for t, r in zip(terms, _bias_routing(n_heads, width, 0)))
        qx = sum(jnp.dot(t, r, preferred_element_type=F32) for t, r in zip(terms, _bias_routing(n_heads, width, 3)))
        kx_ref[...] = (ones_k - kx).astype(BF16)
        qx_ref[...] = (ones_q + qx).astype(BF16)

    row = pl.BlockSpec((tm, W), lambda i: (i, 0))
    wide = pl.BlockSpec((tm, width), lambda i: (i, 0))
    return pl.pallas_call(
        body, name=name, grid=(Lp // tm,),
        in_specs=[row, pl.BlockSpec((1, W), lambda i: (0, 0))], out_specs=[row, wide, wide],
        out_shape=[jax.ShapeDtypeStruct((Lp, W), F32), jax.ShapeDtypeStruct((Lp, width), BF16),
                   jax.ShapeDtypeStruct((Lp, width), BF16)],
        scratch_shapes=[pltpu.VMEM((1, W), F32)],
        compiler_params=_cparams("arbitrary"),
    )(pre, bf)


def _logf_bwd(dc_a, dc_b, pre, bf, name, *, tm):
    Lp, W = pre.shape
    ni = Lp // tm

    def body(dca_ref, dcb_ref, p_ref, b_ref, dpb_ref, db_ref, carry_ref):
        i = pl.program_id(0)

        @pl.when(i == 0)
        def _():
            carry_ref[...] = jnp.zeros_like(carry_ref)
            db_ref[...] = jnp.zeros_like(db_ref)

        triu = (lax.broadcasted_iota(jnp.int32, (tm, tm), 0) <= lax.broadcasted_iota(jnp.int32, (tm, tm), 1)).astype(F32)
        dl = jnp.dot(triu, dca_ref[...] + dcb_ref[...], precision=lax.Precision.HIGHEST,
                     preferred_element_type=F32) + carry_ref[...]
        carry_ref[...] = dl[0:1, :]
        dp = dl * jax.nn.sigmoid(-(p_ref[...] + b_ref[...]))
        dpb_ref[...] = dp.astype(BF16)
        db_ref[...] += jnp.sum(dp, axis=0, keepdims=True)

    rev = pl.BlockSpec((tm, W), lambda i: (ni - 1 - i, 0))
    vec = pl.BlockSpec((1, W), lambda i: (0, 0))
    return pl.pallas_call(
        body, name=name, grid=(ni,),
        in_specs=[rev, rev, rev, vec], out_specs=[rev, vec],
        out_shape=[jax.ShapeDtypeStruct((Lp, W), BF16), jax.ShapeDtypeStruct((1, W), F32)],
        scratch_shapes=[pltpu.VMEM((1, W), F32)],
        compiler_params=_cparams("arbitrary"),
    )(dc_a, dc_b, pre, bf)


def _attn_fwd(qb, kvb, kx, qx, name, *, tq):
    Lp, D = qb.shape
    H = D // HEAD_DIM
    nq = Lp // tq
    S8 = SUBLANES
    assert LANES // HEAD_DIM == 2
    HB = 2 * ATTN_FWD_PAIRS
    W = LANES * ATTN_FWD_PAIRS
    n_scratch = 5

    def body(q_ref, qx_ref, k_ref, kx_ref, v_ref, o_ref, ob_ref, lse_ref, vt_ref, *scratch):
        i = pl.program_id(1)
        heads = [scratch[n_scratch * hb:n_scratch * (hb + 1)] for hb in range(HB)]
        lane = lax.broadcasted_iota(jnp.int32, (tq, LANES), 1)

        def own_lanes(hb, x2, extra2):
            return jnp.where((lane < HEAD_DIM) == (hb % 2 == 0), x2, extra2)

        q_of = [own_lanes(hb, q_ref[:, pl.ds(LANES * (hb // 2), LANES)], qx_ref[:, pl.ds(LANES * (hb // 2), LANES)])
                for hb in range(HB)]

        @pl.when(i == 0)
        def _():
            for j in range(nq):
                vt_ref[j] = jnp.transpose(v_ref[pl.ds(j * tq, tq), :].astype(F32)).astype(BF16)

        for m_ref, l_ref, acc_ref, _, _ in heads:
            m_ref[...] = jnp.full_like(m_ref, NEG_INF)
            l_ref[...] = jnp.zeros_like(l_ref)
            acc_ref[...] = jnp.zeros_like(acc_ref)

        def chunk(j, masked):
            keys = pl.ds(pl.multiple_of(j * tq, tq), tq)
            for hb, (_, _, _, st_ref, _) in enumerate(heads):
                pair = pl.ds(LANES * (hb // 2), LANES)
                k_own = own_lanes(hb, k_ref[keys, pair], kx_ref[keys, pair])
                st_ref[...] = lax.dot_general(k_own, q_of[hb], NT, preferred_element_type=F32)
            for hb, (m_ref, l_ref, acc_ref, st_ref, pt_ref) in enumerate(heads):
                mx = jnp.full((S8, tq), NEG_INF, F32)
                for r0 in range(0, tq, ATTN_STRIP):
                    rows = pl.ds(r0, ATTN_STRIP)
                    st = st_ref[rows, :]
                    if masked:
                        keep = (lax.broadcasted_iota(jnp.int32, (ATTN_STRIP, tq), 1)
                                >= r0 + lax.broadcasted_iota(jnp.int32, (ATTN_STRIP, tq), 0))
                        st = jnp.where(keep, st, NEG_INF)
                        st_ref[rows, :] = st
                    for g0 in range(0, ATTN_STRIP, S8):
                        mx = jnp.maximum(mx, st[g0:g0 + S8])
                m_prev = m_ref[...]
                m_new = jnp.maximum(m_prev, jnp.max(mx, axis=0, keepdims=True))
                alpha = jnp.exp(m_prev - m_new)
                m_ref[...] = m_new
                ls = jnp.zeros((S8, tq), F32)
                for r0 in range(0, tq, ATTN_STRIP):
                    pieces = [jnp.exp(st_ref[pl.ds(r0 + g0, S8), :] - m_new) for g0 in range(0, ATTN_STRIP, S8)]
                    for piece in pieces:
                        ls = ls + piece
                    pt_ref[pl.ds(r0, ATTN_STRIP), :] = jnp.concatenate(pieces, axis=0).astype(BF16)
                l_ref[...] = alpha * l_ref[...] + ls
                pv = jnp.dot(vt_ref[j, pl.ds(LANES * (hb // 2), LANES), :], pt_ref[...], preferred_element_type=F32)
                acc_ref[...] = jnp.concatenate([alpha] * (LANES // S8), axis=0) * acc_ref[...] + pv

        def step(j, carry):
            chunk(j, False)
            return carry

        lax.fori_loop(0, i, step, 0)
        chunk(i, True)
        outs = []
        for hb, (m_ref, l_ref, acc_ref, _, _) in enumerate(heads):
            l_row = jnp.sum(l_ref[...], axis=0, keepdims=True)
            outs.append(acc_ref[...] / l_row)
            lse_ref[hb, 0] = m_ref[0:1, :] + jnp.log(l_row)
        first_rows = lax.broadcasted_iota(jnp.int32, (LANES, tq), 0) < HEAD_DIM
        for pp in range(ATTN_FWD_PAIRS):
            o2 = jnp.transpose(jnp.where(first_rows, outs[2 * pp], outs[2 * pp + 1]))
            o_ref[:, pl.ds(LANES * pp, LANES)] = o2
            ob_ref[:, pl.ds(LANES * pp, LANES)] = o2.astype(BF16)

    per_head = [pltpu.VMEM((S8, tq), F32), pltpu.VMEM((S8, tq), F32), pltpu.VMEM((LANES, tq), F32),
                pltpu.VMEM((tq, tq), F32), pltpu.VMEM((tq, tq), BF16)]
    assert len(per_head) == n_scratch and H % HB == 0
    v_blocks = D // W
    tile = pl.BlockSpec((tq, W), lambda p, i: (i, p))
    whole = pl.BlockSpec((Lp, W), lambda p, i: (0, p))
    return pl.pallas_call(
        body, name=name, grid=(H // HB, nq),
        in_specs=[tile, tile, whole, whole, pl.BlockSpec((Lp, W), lambda p, i: (0, v_blocks + p))],
        out_specs=[tile, tile, pl.BlockSpec((HB, 1, 1, tq), lambda p, i: (p, i, 0, 0))],
        out_shape=[jax.ShapeDtypeStruct((Lp, D), F32), jax.ShapeDtypeStruct((Lp, D), BF16),
                   jax.ShapeDtypeStruct((H, nq, 1, tq), F32)],
        scratch_shapes=[pltpu.VMEM((nq, W, tq), BF16)] + per_head * HB,
        compiler_params=_cparams("parallel", "arbitrary"),
    )(qb, qx, kvb, kx, kvb)


def _attn_bwd(qb, dob, kvb, lse4, delta4, crow4, name, *, tq):
    Lp, D = qb.shape
    H = D // HEAD_DIM
    nq = Lp // tq
    HB = LANES // HEAD_DIM
    lane_tiles = tq // LANES
    n_scratch = 8
    assert HB == 2

    def body(q_ref, do_ref, k_ref, v_ref, lse_ref, dl_ref, c_ref,
             dqb_ref, dk_ref, dv_ref, dcs_ref, dcq_ref, dqt_ref, kt_ref, *scratch):
        j = pl.program_id(1)
        heads = [scratch[n_scratch * hb:n_scratch * (hb + 1)] for hb in range(HB)]
        first_head = lax.broadcasted_iota(jnp.int32, (tq, LANES), 1) < HEAD_DIM

        def split(x2):
            zero = jnp.zeros_like(x2)
            return [jnp.where(first_head, x2, zero), jnp.where(first_head, zero, x2)]

        @pl.when(j == 0)
        def _():
            dqt_ref[...] = jnp.zeros_like(dqt_ref)
            dcq_ref[...] = jnp.zeros_like(dcq_ref)

        k2 = k_ref[...]
        v2 = v_ref[...]
        kt_ref[...] = jnp.transpose(k2.astype(F32)).astype(BF16)
        first_rows = lax.broadcasted_iota(jnp.int32, (LANES, tq), 0) < HEAD_DIM
        for hb, (dk_acc, dv_acc, dc_acc, _, _, _, _, cs_ref) in enumerate(heads):
            dk_acc[...] = jnp.zeros_like(dk_acc)
            dv_acc[...] = jnp.zeros_like(dv_acc)
            dc_acc[...] = jnp.zeros_like(dc_acc)
            cs_ref[...] = jnp.transpose(jnp.broadcast_to(c_ref[hb, j], (LANES, tq)))

        def pair(i, masked):
            queries = pl.ds(pl.multiple_of(i * tq, tq), tq)
            q2 = q_ref[queries, :]
            do2 = do_ref[queries, :]
            q_of, do_of = split(q2), split(do2)
            for hb, (_, _, _, st_ref, dp_ref, _, _, _) in enumerate(heads):
                st_ref[...] = lax.dot_general(k2, q_of[hb], NT, preferred_element_type=F32)
                dp_ref[...] = lax.dot_general(v2, do_of[hb], NT, preferred_element_type=F32)
            dq_parts = []
            for hb, (dk_acc, dv_acc, dc_acc, st_ref, dp_ref, pt_ref, ds_ref, cs_ref) in enumerate(heads):
                bias_q = c_ref[hb, i] - lse_ref[hb, i]
                delta = dl_ref[hb, i]
                col_sum = jnp.zeros((SUBLANES, tq), F32)
                for r0 in range(0, tq, ATTN_STRIP):
                    rows = pl.ds(r0, ATTN_STRIP)
                    st = st_ref[rows, :] + (bias_q - jnp.concatenate([cs_ref[rows, :]] * lane_tiles, axis=1))
                    if masked:
                        keep = (lax.broadcasted_iota(jnp.int32, (ATTN_STRIP, tq), 1)
                                >= r0 + lax.broadcasted_iota(jnp.int32, (ATTN_STRIP, tq), 0))
                        st = jnp.where(keep, st, NEG_INF)
                    pt = jnp.exp(st)
                    dst = pt * (dp_ref[rows, :] - delta)
                    pt_ref[rows, :] = pt.astype(BF16)
                    ds_ref[rows, :] = dst.astype(BF16)
                    dc_acc[rows, :] += jnp.sum(dst, axis=1, keepdims=True)
                    for g0 in range(0, ATTN_STRIP, SUBLANES):
                        col_sum = col_sum + dst[g0:g0 + SUBLANES]
                dcq_ref[hb, i] += jnp.sum(col_sum, axis=0, keepdims=True)
                dv_acc[...] += jnp.dot(pt_ref[...], do2, preferred_element_type=F32)
                dk_acc[...] += jnp.dot(ds_ref[...], q2, preferred_element_type=F32)
                dq_parts.append(jnp.dot(kt_ref[...], ds_ref[...], preferred_element_type=F32))
            dqt_ref[i] += jnp.where(first_rows, dq_parts[0], dq_parts[1])

        def step(i, carry):
            pair(i, False)
            return carry

        pair(j, True)
        lax.fori_loop(j + 1, nq, step, 0)
        dk_ref[...] = jnp.where(first_head, heads[0][0][...], heads[1][0][...]).astype(BF16)
        dv_ref[...] = jnp.where(first_head, heads[0][1][...], heads[1][1][...]).astype(BF16)
        for hb in range(HB):
            dcs_ref[hb, 0] = -jnp.transpose(jnp.broadcast_to(heads[hb][2][...], (tq, LANES)))[0:1, :]

        @pl.when(j == nq - 1)
        def _():
            for i in range(nq):
                dqb_ref[pl.ds(i * tq, tq), :] = jnp.transpose(dqt_ref[i]).astype(BF16)

    per_head = [pltpu.VMEM((tq, LANES), F32), pltpu.VMEM((tq, LANES), F32), pltpu.VMEM((tq, 1), F32),
                pltpu.VMEM((tq, tq), F32), pltpu.VMEM((tq, tq), F32),
                pltpu.VMEM((tq, tq), BF16), pltpu.VMEM((tq, tq), BF16), pltpu.VMEM((tq, LANES), F32)]
    assert len(per_head) == n_scratch
    v_blocks = D // LANES
    whole = pl.BlockSpec((Lp, LANES), lambda p, j: (0, p))
    tile = pl.BlockSpec((tq, LANES), lambda p, j: (j, p))
    rows = pl.BlockSpec((HB, nq, 1, tq), lambda p, j: (p, 0, 0, 0))
    return pl.pallas_call(
        body, name=name, grid=(H // HB, nq),
        in_specs=[whole, whole, tile, pl.BlockSpec((tq, LANES), lambda p, j: (j, v_blocks + p)), rows, rows, rows],
        out_specs=[whole, tile, tile, pl.BlockSpec((HB, 1, 1, tq), lambda p, j: (p, j, 0, 0)), rows],
        out_shape=[jax.ShapeDtypeStruct((Lp, D), BF16), jax.ShapeDtypeStruct((Lp, D), BF16),
                   jax.ShapeDtypeStruct((Lp, D), BF16), jax.ShapeDtypeStruct((H, nq, 1, tq), F32),
                   jax.ShapeDtypeStruct((H, nq, 1, tq), F32)],
        scratch_shapes=[pltpu.VMEM((nq, LANES, tq), F32), pltpu.VMEM((LANES, tq), BF16)] + per_head * HB,
        compiler_params=_cparams("parallel", "arbitrary"),
    )(qb, dob, kvb, kvb, lse4, delta4, crow4)


def _remote(src, dst, send_sems, recv_sems, k, to):
    return pltpu.make_async_remote_copy(src_ref=src, dst_ref=dst, send_sem=send_sems.at[k], recv_sem=recv_sems.at[k],
                                        device_id=to, device_id_type=MESH)


def _place():
    x, y, c = lax.axis_index("x"), lax.axis_index("y"), lax.axis_index("c")
    other_chips = [(1 - x, y), (x, 1 - y), (1 - x, 1 - y)]
    return x, y, c, other_chips


def _all_gather_weights(wb, wf, name):
    Rb, C = wb.shape
    Rf = wf.shape[0]
    hb = Rb // 2

    def body(wb_ref, wf_ref, ob_ref, of_ref, send_sems, recv_sems):
        x, y, c, chips = _place()
        me = 2 * x + y
        sibling = (x, y, 1 - c)

        def half(chip, core):
            return ob_ref.at[chip, pl.ds(core * hb, hb), :]

        sent = []
        for j, (cx, cy) in enumerate(chips):
            sent.append(_remote(wb_ref.at[pl.ds(c * hb, hb), :], half(me, c), send_sems, recv_sems, j, (cx, cy, c)))
            sent.append(_remote(wf_ref, of_ref.at[me], send_sems, recv_sems, 3 + j, (cx, cy, c)))
        for cp in sent:
            cp.start()
        for j, (cx, cy) in enumerate(chips):
            chip = 2 * cx + cy
            _remote(half(chip, c), half(chip, c), send_sems, recv_sems, j, sibling).wait_recv()
            fwd = _remote(half(chip, c), half(chip, c), send_sems, recv_sems, 6 + j, sibling)
            fwd.start()
            sent.append(fwd)
        for j, (cx, cy) in enumerate(chips):
            chip = 2 * cx + cy
            _remote(wf_ref, of_ref.at[chip], send_sems, recv_sems, 3 + j, sibling).wait_recv()
            _remote(half(chip, 1 - c), half(chip, 1 - c), send_sems, recv_sems, 6 + j, sibling).wait_recv()
        for cp in sent:
            cp.wait_send()

    any_spec = pl.BlockSpec(memory_space=pl.ANY)
    return pl.pallas_call(
        body, name=name,
        in_specs=[any_spec, any_spec], out_specs=[any_spec, any_spec],
        out_shape=[jax.ShapeDtypeStruct((N_CHIPS, Rb, C), BF16), jax.ShapeDtypeStruct((N_CHIPS, Rf, C), F32)],
        scratch_shapes=[pltpu.SemaphoreType.DMA((9,)), pltpu.SemaphoreType.DMA((9,))],
    )(wb, wf)


def _half_of(ref, order, half):
    return ref.at[pl.ds(0, N_CHIPS), half] if order == "CH" else ref.at[half]


def _halves_to_sibling(grads, orders, name):
    n = len(grads)

    def body(*refs):
        g_refs, a_refs, (send_sems, recv_sems) = refs[:n], refs[n:2 * n], refs[2 * n:]
        x, y, c, _ = _place()
        copies = [_remote(_half_of(g, o, 1 - c), a, send_sems, recv_sems, k, (x, y, 1 - c))
                  for k, (g, a, o) in enumerate(zip(g_refs, a_refs, orders))]
        for cp in copies:
            cp.start()
        for cp in copies:
            cp.wait()

    any_spec = pl.BlockSpec(memory_space=pl.ANY)
    shapes = [g.shape[2:] for g in grads]
    return pl.pallas_call(
        body, name=name, in_specs=[any_spec] * n, out_specs=[any_spec] * n,
        out_shape=[jax.ShapeDtypeStruct((N_CHIPS,) + s, F32) for s in shapes],
        scratch_shapes=[pltpu.SemaphoreType.DMA((n,)), pltpu.SemaphoreType.DMA((n,))],
    )(*grads)


def _chip_partial(g, a, core, order, wire, name, *, tr):
    _, R, C = a.shape
    narrow = wire != F32

    def body(core_ref, g_ref, a_ref, *outs):
        p = g_ref[0, 0] + a_ref[0]
        outs[0][0] = p
        if narrow:
            outs[1][0] = p.astype(wire)

    if order == "CH":
        g_spec = pl.BlockSpec((1, 1, tr, C), lambda s, i, core_ref: (s, core_ref[0], i, 0))
    else:
        g_spec = pl.BlockSpec((1, 1, tr, C), lambda s, i, core_ref: (core_ref[0], s, i, 0))
    blk = pl.BlockSpec((1, tr, C), lambda s, i, core_ref: (s, i, 0))
    grid_spec = pltpu.PrefetchScalarGridSpec(
        num_scalar_prefetch=1, grid=(N_CHIPS, R // tr), in_specs=[g_spec, blk],
        out_specs=[blk, blk] if narrow else [blk])
    out_shape = [jax.ShapeDtypeStruct((N_CHIPS, R, C), F32)] + ([jax.ShapeDtypeStruct((N_CHIPS, R, C), wire)] if narrow else [])
    outs = pl.pallas_call(body, name=name, grid_spec=grid_spec, out_shape=out_shape,
                          compiler_params=_cparams("parallel", "parallel"))(core, g, a)
    return outs[0], outs[-1]


def _chip_exchange(parts, rep, name):
    n = len(parts)
    rr, C = rep.shape

    def body(*refs):
        p_refs, rep_ref = refs[:n], refs[n]
        land_refs, reps_ref = refs[n + 1:2 * n + 1], refs[2 * n + 1]
        send_sems, recv_sems, local_sem = refs[2 * n + 2:]
        x, y, c, chips = _place()
        me = 4 * x + 2 * y + c
        own = pltpu.make_async_copy(rep_ref, reps_ref.at[me], local_sem.at[0])
        own.start()
        sent = []
        for k, (p, land) in enumerate(zip(p_refs, land_refs)):
            for j, (cx, cy) in enumerate(chips):
                sent.append(_remote(p.at[2 * cx + cy], land.at[j], send_sems, recv_sems, 3 * k + j, (cx, cy, c)))
        for r in range(1, N_DEV):
            fx, fy, fc = (r >> 2) & 1, (r >> 1) & 1, r & 1
            sent.append(_remote(rep_ref, reps_ref.at[me], send_sems, recv_sems, 3 * n - 1 + r, (x ^ fx, y ^ fy, c ^ fc)))
        for cp in sent:
            cp.start()
        for k, (p, land) in enumerate(zip(p_refs, land_refs)):
            for j in range(3):
                _remote(p.at[0], land.at[j], send_sems, recv_sems, 3 * k + j, (x, y, c)).wait_recv()
        for r in range(1, N_DEV):
            fx, fy, fc = (r >> 2) & 1, (r >> 1) & 1, r & 1
            frm = 4 * (x ^ fx) + 2 * (y ^ fy) + (c ^ fc)
            _remote(rep_ref, reps_ref.at[frm], send_sems, recv_sems, 3 * n - 1 + r, (x, y, c)).wait_recv()
        for cp in sent:
            cp.wait_send()
        own.wait()

    any_spec = pl.BlockSpec(memory_space=pl.ANY)
    n_sems = 3 * n + N_DEV - 1
    return pl.pallas_call(
        body, name=name, in_specs=[any_spec] * (n + 1), out_specs=[any_spec] * (n + 1),
        out_shape=[jax.ShapeDtypeStruct((3,) + p.shape[1:], p.dtype) for p in parts]
        + [jax.ShapeDtypeStruct((N_DEV, rr, C), F32)],
        scratch_shapes=[pltpu.SemaphoreType.DMA((n_sems,)), pltpu.SemaphoreType.DMA((n_sems,)),
                        pltpu.SemaphoreType.DMA((1,))],
    )(*parts, rep)


def _adamw_math(w, g, m, v):
    m = ADAM_B1 * m + (1.0 - ADAM_B1) * g
    v = ADAM_B2 * v + (1.0 - ADAM_B2) * (g * g)
    m_hat = m / (1.0 - ADAM_B1 ** ADAM_STEP)
    v_hat = v / (1.0 - ADAM_B2 ** ADAM_STEP)
    delta = -ADAM_LR * (m_hat / (jnp.sqrt(v_hat) + ADAM_EPS) + ADAM_WD * w)
    return delta, m, v


def _adamw_owned(part, landed, w, m, v, place, name, *, tr):
    _, R, C = part.shape

    def body(place_ref, own_ref, land_ref, w_ref, m_ref, v_ref, g_ref, d_ref, mo_ref, vo_ref):
        g = own_ref[0]
        for s in range(3):
            g = g + land_ref[s].astype(F32)
        delta, m_new, v_new = _adamw_math(w_ref[0], g, m_ref[0], v_ref[0])
        g_ref[0] = g
        d_ref[0] = delta
        mo_ref[0] = m_new
        vo_ref[0] = v_new

    half = pl.BlockSpec((1, tr, C), lambda i, place_ref: (place_ref[0], i, 0))
    grid_spec = pltpu.PrefetchScalarGridSpec(
        num_scalar_prefetch=1, grid=(R // tr,),
        in_specs=[pl.BlockSpec((1, tr, C), lambda i, place_ref: (place_ref[1], i, 0)),
                  pl.BlockSpec((3, tr, C), lambda i, place_ref: (0, i, 0)), half, half, half],
        out_specs=[half] * 4)
    return pl.pallas_call(
        body, name=name, grid_spec=grid_spec, out_shape=[jax.ShapeDtypeStruct((2, R, C), F32)] * 4,
        compiler_params=_cparams("parallel"),
    )(place, part, landed, w, m, v)


def _join_halves(bufs, name):
    n = len(bufs)

    def body(*refs):
        out_refs, (send_sems, recv_sems) = refs[n:2 * n], refs[2 * n:]
        x, y, c, _ = _place()
        copies = [_remote(o.at[c], o.at[c], send_sems, recv_sems, k, (x, y, 1 - c)) for k, o in enumerate(out_refs)]
        for cp in copies:
            cp.start()
        for k, o in enumerate(out_refs):
            _remote(o.at[c], o.at[1 - c], send_sems, recv_sems, k, (x, y, 1 - c)).wait_recv()
        for cp in copies:
            cp.wait_send()

    any_spec = pl.BlockSpec(memory_space=pl.ANY)
    return pl.pallas_call(
        body, name=name, in_specs=[any_spec] * n, out_specs=[any_spec] * n,
        out_shape=[jax.ShapeDtypeStruct(b.shape, b.dtype) for b in bufs],
        input_output_aliases={k: k for k in range(n)},
        scratch_shapes=[pltpu.SemaphoreType.DMA((n,)), pltpu.SemaphoreType.DMA((n,))],
    )(*bufs)


def _sum_adamw(own, landed, w, m, v, name, *, tr):
    n = landed.shape[0]
    hr, C = own.shape

    def body(own_ref, land_ref, w_ref, m_ref, v_ref, o_ref):
        g = own_ref[...]
        for s in range(n):
            g = g + land_ref[s]
        delta, m_new, v_new = _adamw_math(w_ref[...], g, m_ref[...], v_ref[...])
        o_ref[0] = g
        o_ref[1] = delta
        o_ref[2] = m_new
        o_ref[3] = v_new

    blk = pl.BlockSpec((tr, C), lambda i: (i, 0))
    return pl.pallas_call(
        body, name=name, grid=(hr // tr,),
        in_specs=[blk, pl.BlockSpec((n, tr, C), lambda i: (0, i, 0)), blk, blk, blk],
        out_specs=pl.BlockSpec((4, tr, C), lambda i: (0, i, 0)),
        out_shape=jax.ShapeDtypeStruct((4, hr, C), F32), compiler_params=_cparams("parallel"),
    )(own, landed, w, m, v)


def _rows_of(shape):
    n = 1
    for d in shape:
        n *= d
    return -(-n // PACK_COLS)


def _pack(arrays, total_rows, dtype):
    parts, used = [], 0
    for a in arrays:
        flat = a.reshape(-1).astype(dtype)
        fill = _rows_of(a.shape) * PACK_COLS - flat.shape[0]
        parts += [flat] + ([jnp.zeros((fill,), dtype)] if fill else [])
        used += _rows_of(a.shape)
    if total_rows > used:
        parts.append(jnp.zeros(((total_rows - used) * PACK_COLS,), dtype))
    return jnp.concatenate(parts).reshape(total_rows, PACK_COLS)


def _unpack(buf, shapes):
    lead = buf.shape[:-2]
    out, r = [], 0
    for shp in shapes:
        n = 1
        for d in shp:
            n *= d
        rows = _rows_of(shp)
        piece = buf[..., r:r + rows, :].reshape(lead + (rows * PACK_COLS,))[..., :n]
        out.append(piece.reshape(lead + tuple(shp)))
        r += rows
    return out


def _join_shards(stacked, axis):
    return jnp.concatenate([stacked[s] for s in range(N_CHIPS)], axis=axis)


def _shard_of(full, axis, chip):
    width = full.shape[axis] // N_CHIPS
    return lax.slice_in_dim(full, chip * width, (chip + 1) * width, axis=axis)


def _local_step(h0, tgt, W, *, seq, tm):
    Lp, D = h0.shape
    H = D // HEAD_DIM
    F2 = W["ffn_w_in"].shape[-1]
    F = F2 // 2
    te = tm // 2
    nq = Lp // tm
    cap = 1408
    tD, t2D = _pick(D, cap), _pick(2 * D, cap)
    tF, tF2 = _pick(F, 2 * cap), _pick(F2, 2 * cap)
    t2Dc, tF2c = _pick(2 * D // N_CHIPS, cap), _pick(F2 // N_CHIPS, cap)
    tFm = tcn = _pick(F, cap)
    tkL = max(t for t in range(tm, 2048 + 1, tm) if Lp % t == 0) if Lp > 2048 else Lp

    def vec(a):
        return a.reshape(1, -1)

    ln_g, ln_b = W["ln_g"], W["ln_b"]
    wf_pad = jnp.pad(W["w_f"], ((0, 0), (0, LANES - H)))
    bf_pad = jnp.pad(W["b_f"], (0, LANES - H)).reshape(1, LANES)

    def ffn_fwd(h, hb, l, tag):
        u = _mm(hb, W["ffn_w_in"][l], "nn", F32, f"ffn{tag}_up", tm=tm, tn=tF2, tk=tD)
        act = _conv_glu_fwd(u, W["ffn_conv_w"][l], vec(W["ffn_conv_b"][l]), f"ffn{tag}_glu", tm=te, tn=tcn)
        normed = _mm_ln(act, W["ffn_w_out"][l], "nn", f"ffn{tag}_down_ln", tm=tm, tk=tF, forward=True,
                        rows=[h], vecs=[vec(ln_g[l, 1]), vec(ln_b[l, 1])])
        return u, act, normed

    def ffn_bwd(dz, dzb, hb, u, act, l, tag, dw_in_acc, dw_out_acc, xh_in, rs_in, g_in):
        dact = _mm(dzb, W["ffn_w_out"][l], "nt", F32, f"ffn{tag}_dact", tm=tm, tn=tF, tk=tD)
        dw_out = _mm(act, dzb, "tn", F32, f"ffn{tag}_dwout", tm=tFm, tn=tD, tk=tkL, layer=l, into=dw_out_acc)
        du, dwa, dwg, dba, dbg = _conv_glu_bwd(u, dact, W["ffn_conv_w"][l], vec(W["ffn_conv_b"][l]),
                                               f"ffn{tag}_dglu", tm=te, tn=tcn)
        dcw = jnp.concatenate([dwa, dwg], axis=1)
        dcb = jnp.concatenate([dba, dbg], axis=1)
        prev = _mm_ln(du, W["ffn_w_in"][l], "nt", f"ffn{tag}_dh_ln", tm=tm, tk=tF, forward=False,
                      rows=[dz, xh_in, rs_in], vecs=[g_in], a_halves=True)
        dw_in = _mm(hb, du, "tn", F32, f"ffn{tag}_dwin", tm=tD, tn=tF2c, tk=tkL, chips=True, layer=l, into=dw_in_acc,
                    b_halves=True)
        return prev, dw_in, dw_out, dcw, dcb[0]

    diffb, mixpre, h1, h1b, xh1, rs1 = _pool_ln_fwd(h0, W["pool_w"][0], W["pool_scale"], vec(ln_g[0, 0]),
                                                    vec(ln_b[0, 0]), "pool_ln_fwd", tm=te)
    u0, act0, (h2, h2b, xh2, rs2) = ffn_fwd(h1, h1b, 0, "0")

    kvb = _mm(h2b, W["w_kv"], "nn", BF16, "kv_proj", tm=tm, tn=t2D, tk=tD)
    qb = _mm(h2b, W["w_q"][0], "nn", BF16, "q_proj", tm=tm, tn=tD, tk=tD, scale=HEAD_DIM ** -0.5)
    pre = _mm(h2b, wf_pad, "nn", F32, "f_proj", tm=tm, tn=LANES, tk=tD)
    c, kx, qx = _logf_cumsum(pre, bf_pad, "logf_cumsum", tm=tm, n_heads=H, width=D)

    crow4 = c[:, :H].T.reshape(H, nq, 1, tm)
    o_tok, ob, lse4 = _attn_fwd(qb, kvb, kx, qx, "attn_fwd", tq=tm)
    h3, h3b, xh3, rs3 = _mm_ln(ob, W["w_o"][0], "nn", "o_proj_ln", tm=tm, tk=tD, forward=True,
                               rows=[h2], vecs=[vec(ln_g[1, 0]), vec(ln_b[1, 0])])
    u1, act1, (h4, _, xh4, rs4) = ffn_fwd(h3, h3b, 1, "1")

    dz4, dz4b, dg11, db11, loss = _loss_ln_bwd(h4, tgt, xh4, rs4, vec(ln_g[1, 1]), "loss_ln11_bwd", tm=te,
                                               row_lo=N_META, row_hi=N_META + seq)
    (dz3, dz3b, dg10, db10), dw_in, dw_out, dcw1, dcb1 = ffn_bwd(dz4, dz4b, h3b, u1, act1, 1, "1", None, None,
                                                                 xh3, rs3, vec(ln_g[1, 0]))

    dob, delta = _mm(dz3b, W["w_o"][0], "nt", BF16, "o_proj_dx", tm=tm, tn=D, tk=tD, head_dots_with=o_tok)
    dw_o = _mm(ob, dz3b, "tn", F32, "o_proj_dw", tm=tD, tn=tD, tk=tkL)
    dqb, dkb, dvb, dcs, dcq = _attn_bwd(qb, dob, kvb, lse4, delta[:, :H].T.reshape(H, nq, 1, tm), crow4,
                                        "attn_bwd", tq=tm)
    dc_keys = jnp.pad(dcs.reshape(H, Lp).T, ((0, 0), (0, LANES - H)))
    dc_queries = jnp.pad(dcq.reshape(H, Lp).T, ((0, 0), (0, LANES - H)))
    dpreb, dbf = _logf_bwd(dc_keys, dc_queries, pre, bf_pad, "logf_bwd", tm=tm)

    qs = HEAD_DIM ** -0.5
    dw_q = _mm(h2b, dqb, "tn", F32, "q_proj_dw", tm=tD, tn=tD, tk=tkL, scale=qs)
    dw_kv = _mm(h2b, dkb, "tn", F32, "k_proj_dw", tm=tD, tn=t2Dc, tk=tkL, chips=(0, N_CHIPS // 2))
    dw_kv = _mm(h2b, dvb, "tn", F32, "v_proj_dw", tm=tD, tn=t2Dc, tk=tkL, chips=(N_CHIPS // 2, N_CHIPS // 2), into=dw_kv)
    dw_f = _mm(h2b, dpreb, "tn", F32, "f_proj_dw", tm=tD, tn=LANES, tk=tkL)[:, :H]
    dh2 = _mm(dqb, W["w_q"][0], "nt", F32, "q_proj_dx", tm=tm, tn=tD, tk=tD, scale=qs)
    dh2 = _mm(dkb, W["w_kv"][:, :D], "nt", F32, "k_proj_dx", tm=tm, tn=tD, tk=tD, add=dh2)
    dh2 = _mm(dvb, W["w_kv"][:, D:], "nt", F32, "v_proj_dx", tm=tm, tn=tD, tk=tD, add=dh2)
    dz2, dz2b, dg01, db01 = _mm_ln(dpreb, wf_pad, "nt", "f_proj_dx_ln", tm=tm, tk=LANES, forward=False,
                                   rows=[dz3, xh2, rs2], vecs=[vec(ln_g[0, 1])], add=dh2)

    (dz1, _, dg00, db00), dw_in, dw_out, dcw0, dcb0 = ffn_bwd(dz2, dz2b, h1b, u0, act0, 0, "0", dw_in, dw_out,
                                                              xh1, rs1, vec(ln_g[0, 0]))
    dh0, dmb, dscale = _pool_bwd(dz1, mixpre, W["pool_w"][0], W["pool_scale"], "pool_bwd", tm=te)
    dw_pool = _pool_dw(diffb, dmb, "pool_dw", tk=tm)

    grads = {
        "meta": dh0[:N_META],
        "pool_w": dw_pool[None],
        "pool_scale": dscale,
        "w_kv": dw_kv,
        "w_f": dw_f,
        "b_f": dbf[0, :H],
        "w_q": dw_q[None],
        "w_o": dw_o[None],
        "ffn_w_in": dw_in,
        "ffn_conv_w": jnp.stack([dcw0, dcw1]),
        "ffn_conv_b": jnp.stack([dcb0, dcb1]),
        "ffn_w_out": dw_out,
        "ln_g": jnp.stack([jnp.stack([dg00[0], dg01[0]]), jnp.stack([dg10[0], dg11[0]])]),
        "ln_b": jnp.stack([jnp.stack([db00[0], db01[0]]), jnp.stack([db10[0], db11[0]])]),
    }
    return loss, dh0, grads


def _row_block(rows, cols):
    best = SUBLANES
    for t in range(SUBLANES, rows + 1, SUBLANES):
        if rows % t == 0 and t * cols * 4 <= ELEMENTWISE_BLOCK_BYTES:
            best = t
    return best


def _row_tile(length):
    return 640 if length >= 4096 else 128


def kernel(x, meta, pool_w, pool_scale, w_kv, w_f, b_f, w_q, w_o, ffn_w_in, ffn_conv_w, ffn_conv_b, ffn_w_out, ln_g, ln_b, loss_target, m_meta, m_pool_w, m_pool_scale, m_w_kv, m_w_f, m_b_f, m_w_q, m_w_o, m_ffn_w_in, m_ffn_conv_w, m_ffn_conv_b, m_ffn_w_out, m_ln_g, m_ln_b, v_meta, v_pool_w, v_pool_scale, v_w_kv, v_w_f, v_b_f, v_w_q, v_w_o, v_ffn_w_in, v_ffn_conv_w, v_ffn_conv_b, v_ffn_w_out, v_ln_g, v_ln_b):
    weights = dict(meta=meta, pool_w=pool_w, pool_scale=pool_scale, w_kv=w_kv, w_f=w_f, b_f=b_f, w_q=w_q, w_o=w_o,
                   ffn_w_in=ffn_w_in, ffn_conv_w=ffn_conv_w, ffn_conv_b=ffn_conv_b, ffn_w_out=ffn_w_out,
                   ln_g=ln_g, ln_b=ln_b)
    mom1 = dict(meta=m_meta, pool_w=m_pool_w, pool_scale=m_pool_scale, w_kv=m_w_kv, w_f=m_w_f, b_f=m_b_f, w_q=m_w_q,
                w_o=m_w_o, ffn_w_in=m_ffn_w_in, ffn_conv_w=m_ffn_conv_w, ffn_conv_b=m_ffn_conv_b,
                ffn_w_out=m_ffn_w_out, ln_g=m_ln_g, ln_b=m_ln_b)
    mom2 = dict(meta=v_meta, pool_w=v_pool_w, pool_scale=v_pool_scale, w_kv=v_w_kv, w_f=v_w_f, b_f=v_b_f, w_q=v_w_q,
                w_o=v_w_o, ffn_w_in=v_ffn_w_in, ffn_conv_w=v_ffn_conv_w, ffn_conv_b=v_ffn_conv_b,
                ffn_w_out=v_ffn_w_out, ln_g=v_ln_g, ln_b=v_ln_b)
    _, seq, D = x.shape
    L = N_META + seq
    tm = _row_tile(L)
    Lp = _round_up(L, tm)
    c_idx = lax.axis_index("c")
    chip = 2 * lax.axis_index("x") + lax.axis_index("y")

    shard_shapes = {n: weights[n].shape for n in SHARDED}
    rows_b = _round_up(sum(_rows_of(shard_shapes[n]) for n in MATMUL_WEIGHTS), 32)
    rows_f = _round_up(sum(_rows_of(shard_shapes[n]) for n in VECTOR_WEIGHTS), SUBLANES)
    wb = _pack([weights[n] for n in MATMUL_WEIGHTS], rows_b, BF16)
    wf = _pack([weights[n] for n in VECTOR_WEIGHTS], rows_f, F32)
    gb, gf = _all_gather_weights(wb, wf, "weights_all_gather")
    gb = lax.dynamic_update_index_in_dim(gb, wb, chip, axis=0)
    gf = lax.dynamic_update_index_in_dim(gf, wf, chip, axis=0)
    full = {}
    for names, buf in ((MATMUL_WEIGHTS, gb), (VECTOR_WEIGHTS, gf)):
        for n, stacked in zip(names, _unpack(buf, [shard_shapes[n] for n in names])):
            full[n] = _join_shards(stacked, SHARD_AXIS[n])
    full["b_f"] = b_f
    full["ffn_conv_b"] = ffn_conv_b

    pad = jnp.zeros((Lp - L, D), F32)
    h0 = jnp.concatenate([full["meta"], x[0], pad], axis=0)
    tgt = jnp.concatenate([jnp.zeros((N_META, D), F32), loss_target[0], pad], axis=0)
    loss, dh0, grads = _local_step(h0, tgt, full, seq=seq, tm=tm)
    loss = lax.psum(loss[0, 0], AXES)
    grad_x = dh0[N_META:L][None]

    core = c_idx.astype(jnp.int32).reshape(1)
    place = jnp.stack([c_idx, chip]).astype(jnp.int32)
    small_shapes = [shard_shapes[n] for n in SMALL_SHARDED]
    rows_s = _round_up(sum(_rows_of(s) for s in small_shapes), 2 * LANES)

    def packed_small(d):
        return _pack([d[n] for n in SMALL_SHARDED], rows_s, F32).reshape(2, rows_s // 2, PACK_COLS)

    names, orders, wires, g_views, wmv = [], [], [], [], []
    for n, order in BIG_SHARDED:
        shp = shard_shapes[n]
        C = shp[-1]
        R = weights[n].size // C // 2
        lead = (N_CHIPS, 2) if order == "CH" else (2, N_CHIPS)
        names.append(n)
        orders.append(order)
        wires.append(BF16)
        g_views.append(grads[n].reshape(lead + (R, C)))
        wmv.append([d[n].reshape(2, R, C) for d in (weights, mom1, mom2)])
    names.append("small")
    orders.append("CH")
    wires.append(F32)
    g_views.append(jnp.stack([_pack([_shard_of(grads[n], SHARD_AXIS[n], s) for n in SMALL_SHARDED], rows_s, F32)
                              for s in range(N_CHIPS)]).reshape(N_CHIPS, 2, rows_s // 2, PACK_COLS))
    wmv.append([packed_small(d) for d in (weights, mom1, mom2)])

    from_sibling = _halves_to_sibling(g_views, orders, "grads_to_sibling")
    parts, on_wire = [], []
    for n, order, wire, g, a in zip(names, orders, wires, g_views, from_sibling):
        p, pw = _chip_partial(g, a, core, order, wire, f"chip_sum_{n}", tr=_row_block(a.shape[1], a.shape[2]))
        parts.append(p)
        on_wire.append(pw)

    rep_shapes = [weights[n].shape for n in REPLICATED]
    rows_r = _round_up(sum(_rows_of(s) for s in rep_shapes), SUBLANES)
    rep = _pack([grads[n] for n in REPLICATED], rows_r, F32)
    *landed, reps = _chip_exchange(on_wire, rep, "grads_chip_exchange")

    halves = []
    for n, p, b, (w_, m_, v_) in zip(names, parts, landed, wmv):
        halves += _adamw_owned(p, b, w_, m_, v_, place, f"adamw_{n}", tr=_row_block(p.shape[1], p.shape[2]))
    joined = _join_halves(halves, "results_to_sibling")
    out = {}
    for k, n in enumerate(names[:-1]):
        out[n] = [a.reshape(shard_shapes[n]) for a in joined[4 * k:4 * k + 4]]
    small_out = [_unpack(a.reshape(rows_s, PACK_COLS), small_shapes) for a in joined[-4:]]
    for k, n in enumerate(SMALL_SHARDED):
        out[n] = [small_out[kind][k] for kind in range(4)]

    def packr(d):
        return _pack([d[n] for n in REPLICATED], rows_r, F32)

    res_r = _sum_adamw(reps[0], reps[1:], packr(weights), packr(mom1), packr(mom2), "adamw_replicated", tr=rows_r)
    rep_out = _unpack(res_r, rep_shapes)

    out.update({n: a for n, a in zip(REPLICATED, rep_out)})
    result = [loss, grad_x]
    for k in range(4):
        result += [out[n][k] for n in WEIGHT_ORDER]
    return tuple(result)
```

```python
import functools

import jax
import jax.numpy as jnp
from jax import lax
from jax.experimental import pallas as pl
from jax.experimental.pallas import tpu as pltpu

N_META = 16
POOL_WINDOWS = (2, 4, 8, 16)
MAX_WINDOW = max(POOL_WINDOWS)
N_GROUPS = len(POOL_WINDOWS)
HEAD_DIM = 64
DEPTH = 2
CONV_WIDTH = 3
ALPHA = (2.0 * DEPTH) ** 0.25
LN_EPS = 1e-5
NEG_INF = -1e30
ADAM_LR = 0.001
ADAM_B1 = 0.9
ADAM_B2 = 0.999
ADAM_EPS = 1e-08
ADAM_WD = 0.01
ADAM_STEP = 10

F32 = jnp.float32
BF16 = jnp.bfloat16
ATTN_FWD_PAIRS = 1
BIAS_SLOTS = 6
ATTN_STRIP = 32
GLU_STRIP = 16
LANES = 128
SUBLANES = 8
PACK_COLS = 1024
VMEM_LIMIT = 56 * 1024 * 1024
AXES = ("x", "y", "c")
MESH = pl.DeviceIdType.MESH

NN = (((1,), (0,)), ((), ()))
NT = (((1,), (1,)), ((), ()))
TN = (((0,), (0,)), ((), ()))

SHARD_AXIS = {"meta": 1, "pool_w": 2, "pool_scale": 1, "w_kv": 1, "w_f": 0, "w_q": 1, "w_o": 1,
              "ffn_w_in": 2, "ffn_conv_w": 2, "ffn_w_out": 1, "ln_g": 2, "ln_b": 2}
SHARDED = ("meta", "pool_w", "pool_scale", "w_kv", "w_f", "w_q", "w_o", "ffn_w_in", "ffn_conv_w",
           "ffn_w_out", "ln_g", "ln_b")
REPLICATED = ("b_f", "ffn_conv_b")
MATMUL_WEIGHTS = ("pool_w", "w_kv", "w_f", "w_q", "w_o", "ffn_w_in", "ffn_w_out")
VECTOR_WEIGHTS = ("meta", "pool_scale", "ffn_conv_w", "ln_g", "ln_b")
WEIGHT_ORDER = ("meta", "pool_w", "pool_scale", "w_kv", "w_f", "b_f", "w_q", "w_o", "ffn_w_in",
                "ffn_conv_w", "ffn_conv_b", "ffn_w_out", "ln_g", "ln_b")
BIG_SHARDED = (("w_kv", "CH"), ("w_q", "CH"), ("w_o", "CH"), ("ffn_w_in", "HC"), ("ffn_w_out", "HC"))
SMALL_SHARDED = ("meta", "pool_w", "pool_scale", "w_f", "ffn_conv_w", "ln_g", "ln_b")
ELEMENTWISE_BLOCK_BYTES = 3 * 512 * 1024
N_CHIPS = 4
N_DEV = 8


def _cparams(*sem):
    return pltpu.CompilerParams(dimension_semantics=sem, vmem_limit_bytes=VMEM_LIMIT)


def _round_up(n, m):
    return (n + m - 1) // m * m


def _pick(n, cap):
    if n <= cap:
        return n
    best = 0
    for t in range(LANES, cap + 1, LANES):
        if n % t == 0:
            best = t
    assert best, (n, cap)
    return best


def _mm(a, b, mode, out_dtype, name, *, tm, tn, tk, scale=None, add=None, chips=False, layer=None, into=None,
        b_halves=False, head_dots_with=None):
    if mode == "nn":
        (M, K), N = a.shape, b.shape[1]
    elif mode == "nt":
        (M, K), N = a.shape, b.shape[0]
    elif b_halves:
        (K, M), N = a.shape, 2 * b.shape[2]
    else:
        (K, M), N = a.shape, b.shape[1]
    assert M % tm == 0 and N % tn == 0 and K % tk == 0, (name, M, N, K, tm, tn, tk)
    nk = K // tk
    dn = {"nn": NN, "nt": NT, "tn": TN}[mode]
    has_add = add is not None
    has_into = into is not None
    has_dots = head_dots_with is not None
    assert not (has_add and (chips or layer is not None))
    assert not has_dots or (tn == N and not (chips or layer is not None or has_into))

    def body(*refs):
        a_ref, b_ref = refs[0], refs[1]
        add_ref = refs[2] if has_add else None
        with_ref = refs[2 + has_add + has_into] if has_dots else None
        o_ref = refs[2 + has_add + has_into + has_dots]
        dots_ref = refs[3 + has_add + has_into + has_dots] if has_dots else None
        acc_ref = refs[-1] if nk > 1 else None
        k = pl.program_id(2)
        part = lax.dot_general(a_ref[...], b_ref[0] if b_halves else b_ref[...], dn, preferred_element_type=F32)

        def finish(r):
            if scale is not None:
                r = r * scale
            if has_add:
                r = r + add_ref[...]
            out = r.astype(out_dtype)
            o_ref[...] = out.reshape(o_ref.shape)
            if has_dots:
                sel = (lax.broadcasted_iota(jnp.int32, (N, LANES), 0) // HEAD_DIM
                       == lax.broadcasted_iota(jnp.int32, (N, LANES), 1)).astype(F32)
                dots_ref[...] = jnp.dot(out.astype(F32) * with_ref[...], sel, precision=lax.Precision.HIGHEST,
                                        preferred_element_type=F32)

        if nk == 1:
            finish(part)
        else:
            @pl.when(k == 0)
            def _():
                acc_ref[...] = part

            @pl.when(k > 0)
            def _():
                acc_ref[...] += part

            @pl.when(k == nk - 1)
            def _():
                finish(acc_ref[...])

    if mode == "nn":
        a_spec = pl.BlockSpec((tm, tk), lambda j, i, k: (i, k))
        b_spec = pl.BlockSpec((tk, tn), lambda j, i, k: (k, j))
    elif mode == "nt":
        a_spec = pl.BlockSpec((tm, tk), lambda j, i, k: (i, k))
        b_spec = pl.BlockSpec((tn, tk), lambda j, i, k: (j, k))
    else:
        a_spec = pl.BlockSpec((tk, tm), lambda j, i, k: (k, i))
        b_spec = pl.BlockSpec((tk, tn), lambda j, i, k: (k, j))
        if b_halves:
            per_half = N // 2 // tn
            b_spec = pl.BlockSpec((1, tk, tn), lambda j, i, k: (j // per_half, k, j % per_half))
    out_dims, blk = (M, N), (tm, tn)
    if chips:
        base, count = (0, N_CHIPS) if chips is True else chips
        per_chip = N // count // tn
        assert per_chip * tn * count == N, (name, N, tn)
        out_dims, blk = (N_CHIPS, M, N // count), (1, tm, tn)
        where = lambda j, i: (base + j // per_chip, i, j % per_chip)
    else:
        where = lambda j, i: (i, j)
    if layer is not None:
        out_dims, blk = (DEPTH,) + out_dims, (1,) + blk
        o_spec = pl.BlockSpec(blk, lambda j, i, k: (layer,) + where(j, i))
    else:
        o_spec = pl.BlockSpec(blk, lambda j, i, k: where(j, i))
    in_specs = ([a_spec, b_spec] + ([o_spec] if has_add else []) + ([pl.BlockSpec(memory_space=pl.ANY)] if has_into else [])
                + ([o_spec] if has_dots else []))
    args = (a, b) + ((add,) if has_add else ()) + ((into,) if has_into else ()) + ((head_dots_with,) if has_dots else ())
    out_specs, out_shape = o_spec, jax.ShapeDtypeStruct(out_dims, out_dtype)
    if has_dots:
        out_specs = [o_spec, pl.BlockSpec((tm, LANES), lambda j, i, k: (i, 0))]
        out_shape = [out_shape, jax.ShapeDtypeStruct((M, LANES), F32)]
    return pl.pallas_call(
        body, name=name, grid=(N // tn, M // tm, nk),
        in_specs=in_specs, out_specs=out_specs,
        out_shape=out_shape,
        input_output_aliases={2 + has_add: 0} if has_into else {},
        scratch_shapes=[pltpu.VMEM((tm, tn), F32)] if nk > 1 else [],
        compiler_params=_cparams("parallel", "parallel", "arbitrary"),
    )(*args)


def _ln_math(z, g, b):
    mu = jnp.mean(z, axis=-1, keepdims=True)
    zc = z - mu
    var = jnp.mean(zc * zc, axis=-1, keepdims=True)
    rstd = lax.rsqrt(var + LN_EPS)
    xh = zc * rstd
    return xh * g + b, xh, rstd


def _mm_ln(a, b, mode, name, *, tm, tk, forward, rows, vecs, scale=None, add=None, a_halves=False):
    assert mode in ("nn", "nt")
    M, K = (a.shape[1], 2 * a.shape[2]) if a_halves else a.shape
    N = b.shape[1] if mode == "nn" else b.shape[0]
    assert M % tm == 0 and K % tk == 0, (name, M, K, tm, tk)
    nk = K // tk
    ni = M // tm
    dn = {"nn": NN, "nt": NT}[mode]
    has_add = add is not None
    n_in = 2 + has_add + len(rows) + len(vecs)

    def body(*refs):
        a_ref, b_ref = refs[0], refs[1]
        add_ref = refs[2] if has_add else None
        row_refs = refs[2 + has_add:2 + has_add + len(rows)]
        vec_refs = refs[2 + has_add + len(rows):n_in]
        outs = refs[n_in:n_in + 4]
        acc_ref = refs[-1] if nk > 1 else None
        i, k = pl.program_id(0), pl.program_id(1)
        part = lax.dot_general(a_ref[0] if a_halves else a_ref[...], b_ref[...], dn, preferred_element_type=F32)

        def finish(y):
            if scale is not None:
                y = y * scale
            if has_add:
                y = y + add_ref[...]
            if forward:
                h, xh, rstd = _ln_math(ALPHA * row_refs[0][...] + y, vec_refs[0][...], vec_refs[1][...])
                outs[0][...] = h
                outs[1][...] = h.astype(BF16)
                outs[2][...] = xh
                outs[3][...] = rstd
            else:
                dy = ALPHA * row_refs[0][...] + y
                x = row_refs[1][...]
                dxh = dy * vec_refs[0][...]
                m1 = jnp.mean(dxh, axis=-1, keepdims=True)
                m2 = jnp.mean(dxh * x, axis=-1, keepdims=True)
                dz = row_refs[2][...] * (dxh - m1 - x * m2)
                outs[0][...] = dz
                outs[1][...] = dz.astype(BF16)

                @pl.when(i == 0)
                def _():
                    outs[2][...] = jnp.zeros_like(outs[2])
                    outs[3][...] = jnp.zeros_like(outs[3])

                outs[2][...] += jnp.sum(dy * x, axis=0, keepdims=True)
                outs[3][...] += jnp.sum(dy, axis=0, keepdims=True)

        if nk == 1:
            finish(part)
        else:
            @pl.when(k == 0)
            def _():
                acc_ref[...] = part

            @pl.when(k > 0)
            def _():
                acc_ref[...] += part

            @pl.when(k == nk - 1)
            def _():
                finish(acc_ref[...])

    a_spec = pl.BlockSpec((tm, tk), lambda i, k: (i, k))
    if a_halves:
        per_half = nk // 2
        a_spec = pl.BlockSpec((1, tm, tk), lambda i, k: (k // per_half, i, k % per_half))
    b_spec = pl.BlockSpec((tk, N), lambda i, k: (k, 0)) if mode == "nn" else pl.BlockSpec((N, tk), lambda i, k: (0, k))
    row = pl.BlockSpec((tm, N), lambda i, k: (i, 0))
    col = pl.BlockSpec((tm, 1), lambda i, k: (i, 0))
    vec = pl.BlockSpec((1, N), lambda i, k: (0, 0))
    row_specs = [row if r.shape[1] == N else col for r in rows]
    if forward:
        out_specs = [row, row, row, col]
        out_shape = [jax.ShapeDtypeStruct((M, N), F32), jax.ShapeDtypeStruct((M, N), BF16),
                     jax.ShapeDtypeStruct((M, N), F32), jax.ShapeDtypeStruct((M, 1), F32)]
    else:
        out_specs = [row, row, vec, vec]
        out_shape = [jax.ShapeDtypeStruct((M, N), F32), jax.ShapeDtypeStruct((M, N), BF16),
                     jax.ShapeDtypeStruct((1, N), F32), jax.ShapeDtypeStruct((1, N), F32)]
    args = (a, b) + ((add,) if has_add else ()) + tuple(rows) + tuple(vecs)
    return pl.pallas_call(
        body, name=name, grid=(ni, nk),
        in_specs=[a_spec, b_spec] + ([row] if has_add else []) + row_specs + [vec] * len(vecs),
        out_specs=out_specs, out_shape=out_shape,
        scratch_shapes=[pltpu.VMEM((tm, N), F32)] if nk > 1 else [],
        compiler_params=_cparams("parallel" if forward else "arbitrary", "arbitrary"),
    )(*args)


def _pool_ln_fwd(h0, pw, ps, g, b, name, *, tm):
    Lp, D = h0.shape
    G = D // N_GROUPS
    halo_blocks = tm // MAX_WINDOW

    def body(x_ref, halo_ref, pw_ref, ps_ref, g_ref, b_ref,
             diff_ref, mix_ref, h_ref, hb_ref, xh_ref, rs_ref, ext_ref):
        i = pl.program_id(0)
        ext_ref[0:MAX_WINDOW, :] = jnp.where(i == 0, 0.0, halo_ref[...])
        ext_ref[MAX_WINDOW:MAX_WINDOW + tm, :] = x_ref[...]
        t1 = (i * tm + 1 + lax.broadcasted_iota(jnp.int32, (tm, 1), 0)).astype(F32)
        for gi, w in enumerate(POOL_WINDOWS):
            lo, hi = gi * G, (gi + 1) * G
            xg = x_ref[:, lo:hi]
            win = xg
            for j in range(1, w):
                win = win + ext_ref[MAX_WINDOW - j:MAX_WINDOW - j + tm, lo:hi]
            d = (win / jnp.minimum(t1, float(w)) - xg).astype(BF16)
            diff_ref[:, lo:hi] = d
            mix_ref[:, lo:hi] = jnp.dot(d, pw_ref[gi], preferred_element_type=F32)
        z = ALPHA * x_ref[...] + mix_ref[...] * ps_ref[...]
        h, xh, rstd = _ln_math(z, g_ref[...], b_ref[...])
        h_ref[...] = h
        hb_ref[...] = h.astype(BF16)
        xh_ref[...] = xh
        rs_ref[...] = rstd

    row = pl.BlockSpec((tm, D), lambda i: (i, 0))
    vec = pl.BlockSpec((1, D), lambda i: (0, 0))
    return pl.pallas_call(
        body, name=name, grid=(Lp // tm,),
        in_specs=[row,
                  pl.BlockSpec((MAX_WINDOW, D), lambda i: (jnp.maximum(i * halo_blocks - 1, 0), 0)),
                  pl.BlockSpec((N_GROUPS, G, G), lambda i: (0, 0, 0)), vec, vec, vec],
        out_specs=[row, row, row, row, row, pl.BlockSpec((tm, 1), lambda i: (i, 0))],
        out_shape=[jax.ShapeDtypeStruct((Lp, D), BF16), jax.ShapeDtypeStruct((Lp, D), F32),
                   jax.ShapeDtypeStruct((Lp, D), F32), jax.ShapeDtypeStruct((Lp, D), BF16),
                   jax.ShapeDtypeStruct((Lp, D), F32), jax.ShapeDtypeStruct((Lp, 1), F32)],
        scratch_shapes=[pltpu.VMEM((tm + MAX_WINDOW, D), F32)],
        compiler_params=_cparams("parallel"),
    )(h0, h0, pw, ps, g, b)


def _pool_bwd(dz, mixpre, pw, ps, name, *, tm):
    Lp, D = dz.shape
    G = D // N_GROUPS
    halo_blocks = tm // MAX_WINDOW
    n_halo = Lp // MAX_WINDOW
    ni = Lp // tm
    R = tm + MAX_WINDOW

    def body(dz_ref, halo_ref, mix_ref, pw_ref, ps_ref, dh_ref, dmb_ref, dsc_ref, ext_ref, dp_ref):
        i = pl.program_id(0)
        ext_ref[0:tm, :] = dz_ref[...]
        ext_ref[tm:R, :] = jnp.where(i == ni - 1, 0.0, halo_ref[...])
        dmix = (ext_ref[...] * ps_ref[...]).astype(BF16)
        dmb_ref[...] = dmix[0:tm]

        @pl.when(i == 0)
        def _():
            dsc_ref[...] = jnp.zeros_like(dsc_ref)

        dsc_ref[...] += jnp.sum(dz_ref[...] * mix_ref[...], axis=0, keepdims=True)
        t1 = (i * tm + 1 + lax.broadcasted_iota(jnp.int32, (R, 1), 0)).astype(F32)
        for gi, w in enumerate(POOL_WINDOWS):
            lo, hi = gi * G, (gi + 1) * G
            dd = lax.dot_general(dmix[:, lo:hi], pw_ref[gi], NT, preferred_element_type=F32)
            dp_ref[:, lo:hi] = dd / jnp.minimum(t1, float(w))
            back = dp_ref[0:tm, lo:hi]
            for j in range(1, w):
                back = back + dp_ref[j:j + tm, lo:hi]
            dh_ref[:, lo:hi] = ALPHA * dz_ref[:, lo:hi] - dd[0:tm] + back

    row = pl.BlockSpec((tm, D), lambda i: (i, 0))
    vec = pl.BlockSpec((1, D), lambda i: (0, 0))
    return pl.pallas_call(
        body, name=name, grid=(ni,),
        in_specs=[row,
                  pl.BlockSpec((MAX_WINDOW, D), lambda i: (jnp.minimum((i + 1) * halo_blocks, n_halo - 1), 0)),
                  row, pl.BlockSpec((N_GROUPS, G, G), lambda i: (0, 0, 0)), vec],
        out_specs=[row, row, vec],
        out_shape=[jax.ShapeDtypeStruct((Lp, D), F32), jax.ShapeDtypeStruct((Lp, D), BF16),
                   jax.ShapeDtypeStruct((1, D), F32)],
        scratch_shapes=[pltpu.VMEM((R, D), F32), pltpu.VMEM((R, D), F32)],
        compiler_params=_cparams("arbitrary"),
    )(dz, dz, mixpre, pw, ps)


def _pool_dw(diffb, dmb, name, *, tk):
    Lp, D = diffb.shape
    G = D // N_GROUPS

    def body(a_ref, b_ref, o_ref):
        @pl.when(pl.program_id(1) == 0)
        def _():
            o_ref[...] = jnp.zeros_like(o_ref)

        o_ref[0] += lax.dot_general(a_ref[...], b_ref[...], TN, preferred_element_type=F32)

    blk = pl.BlockSpec((tk, G), lambda g, k: (k, g))
    return pl.pallas_call(
        body, name=name, grid=(N_GROUPS, Lp // tk),
        in_specs=[blk, blk], out_specs=pl.BlockSpec((1, G, G), lambda g, k: (g, 0, 0)),
        out_shape=jax.ShapeDtypeStruct((N_GROUPS, G, G), F32),
        compiler_params=_cparams("parallel", "arbitrary"),
    )(diffb, dmb)


def _loss_ln_bwd(h, tgt, xh, rs, g, name, *, tm, row_lo, row_hi):
    Lp, D = h.shape

    def body(h_ref, t_ref, xh_ref, rs_ref, g_ref, dz_ref, dzb_ref, dg_ref, db_ref, loss_ref):
        i = pl.program_id(0)
        r = i * tm + lax.broadcasted_iota(jnp.int32, (tm, 1), 0)
        valid = (r >= row_lo) & (r < row_hi)
        e = jnp.where(valid, h_ref[...] - t_ref[...], 0.0)
        dy = e * (1.0 / D)
        x = xh_ref[...]
        dxh = dy * g_ref[...]
        m1 = jnp.mean(dxh, axis=-1, keepdims=True)
        m2 = jnp.mean(dxh * x, axis=-1, keepdims=True)
        dz = rs_ref[...] * (dxh - m1 - x * m2)
        dz_ref[...] = dz
        dzb_ref[...] = dz.astype(BF16)

        @pl.when(i == 0)
        def _():
            dg_ref[...] = jnp.zeros_like(dg_ref)
            db_ref[...] = jnp.zeros_like(db_ref)
            loss_ref[...] = jnp.zeros_like(loss_ref)

        dg_ref[...] += jnp.sum(dy * x, axis=0, keepdims=True)
        db_ref[...] += jnp.sum(dy, axis=0, keepdims=True)
        loss_ref[...] += 0.5 * jnp.sum(jnp.mean(e * e, axis=-1, keepdims=True), axis=0, keepdims=True)

    row = pl.BlockSpec((tm, D), lambda i: (i, 0))
    vec = pl.BlockSpec((1, D), lambda i: (0, 0))
    return pl.pallas_call(
        body, name=name, grid=(Lp // tm,),
        in_specs=[row, row, row, pl.BlockSpec((tm, 1), lambda i: (i, 0)), vec],
        out_specs=[row, row, vec, vec, pl.BlockSpec((1, 1), lambda i: (0, 0))],
        out_shape=[jax.ShapeDtypeStruct((Lp, D), F32), jax.ShapeDtypeStruct((Lp, D), BF16),
                   jax.ShapeDtypeStruct((1, D), F32), jax.ShapeDtypeStruct((1, D), F32),
                   jax.ShapeDtypeStruct((1, 1), F32)],
        compiler_params=_cparams("arbitrary"),
    )(h, tgt, xh, rs, g)


def _shift_rows_down(cur, prev, s, sub):
    return jnp.where(sub >= s, pltpu.roll(cur, s, 0), pltpu.roll(prev, s, 0))


def _shift_rows_up(cur, nxt, s, sub):
    return jnp.where(sub < SUBLANES - s, pltpu.roll(cur, SUBLANES - s, 0), pltpu.roll(nxt, SUBLANES - s, 0))


def _conv_group(cur, prev, cw_ref, cb_ref, sub):
    taps = [_shift_rows_down(cur, prev, 2, sub), _shift_rows_down(cur, prev, 1, sub), cur]
    c = cb_ref[...] + cw_ref[0:1, :] * taps[0] + cw_ref[1:2, :] * taps[1] + cw_ref[2:3, :] * taps[2]
    return c, taps


def _conv_glu_fwd(u, cw, cb, name, *, tm, tn):
    Lp, F2 = u.shape
    F = F2 // 2
    nj = F // tn
    halo_blocks = tm // SUBLANES
    S8 = SUBLANES
    assert GLU_STRIP == 2 * S8 and tm % GLU_STRIP == 0

    def body(ua_ref, ug_ref, pa_ref, pg_ref, cwa_ref, cwg_ref, cba_ref, cbg_ref, o_ref):
        first = pl.program_id(1) == 0
        sub = lax.broadcasted_iota(jnp.int32, (S8, tn), 0)

        def strip(r, prev_a, prev_g):
            out = []
            for g0 in (0, S8):
                a_cur = ua_ref[pl.ds(r + g0, S8), :]
                g_cur = ug_ref[pl.ds(r + g0, S8), :]
                a, _ = _conv_group(a_cur, prev_a, cwa_ref, cba_ref, sub)
                gate, _ = _conv_group(g_cur, prev_g, cwg_ref, cbg_ref, sub)
                out.append(a * jax.nn.sigmoid(a) * gate)
                prev_a, prev_g = a_cur, g_cur
            o_ref[pl.ds(r, GLU_STRIP), :] = jnp.concatenate(out, axis=0).astype(BF16)

        strip(0, jnp.where(first, 0.0, pa_ref[...]), jnp.where(first, 0.0, pg_ref[...]))

        def step(k, carry):
            r = pl.multiple_of(k * GLU_STRIP, GLU_STRIP)
            before = pl.ds(pl.multiple_of(r - S8, S8), S8)
            strip(r, ua_ref[before, :], ug_ref[before, :])
            return carry

        lax.fori_loop(1, tm // GLU_STRIP, step, 0)

    def prev(off):
        return pl.BlockSpec((SUBLANES, tn), lambda j, i: (jnp.maximum(i * halo_blocks - 1, 0), j + off))

    def cols(rows, off):
        return pl.BlockSpec((rows, tn), lambda j, i: (0, j + off))

    return pl.pallas_call(
        body, name=name, grid=(nj, Lp // tm),
        in_specs=[pl.BlockSpec((tm, tn), lambda j, i: (i, j)), pl.BlockSpec((tm, tn), lambda j, i: (i, j + nj)),
                  prev(0), prev(nj), cols(CONV_WIDTH, 0), cols(CONV_WIDTH, nj), cols(1, 0), cols(1, nj)],
        out_specs=pl.BlockSpec((tm, tn), lambda j, i: (i, j)),
        out_shape=jax.ShapeDtypeStruct((Lp, F), BF16),
        compiler_params=_cparams("parallel", "parallel"),
    )(u, u, u, u, cw, cw, cb, cb)


def _conv_glu_bwd(u, dact, cw, cb, name, *, tm, tn):
    Lp, F2 = u.shape
    F = F2 // 2
    nj = F // tn
    ni = Lp // tm
    halo_blocks = tm // SUBLANES
    n_halo = Lp // SUBLANES
    S8 = SUBLANES
    n_strips = tm // GLU_STRIP
    assert GLU_STRIP == 2 * S8 and tm % GLU_STRIP == 0

    def body(ua_ref, ug_ref, pa_ref, pg_ref, na_ref, ng_ref, da_ref, dn_ref,
             cwa_ref, cwg_ref, cba_ref, cbg_ref,
             du_ref, dwa_ref, dwg_ref, dba_ref, dbg_ref,
             wacc_a, wacc_g, bacc_a, bacc_g):
        i = pl.program_id(1)
        first, last = i == 0, i == ni - 1
        sub = lax.broadcasted_iota(jnp.int32, (S8, tn), 0)
        for acc in (wacc_a, wacc_g, bacc_a, bacc_g):
            acc[...] = jnp.zeros_like(acc)

        def dconv(a_cur, a_prev, g_cur, g_prev, dact_rows):
            a, taps_a = _conv_group(a_cur, a_prev, cwa_ref, cba_ref, sub)
            gate, taps_g = _conv_group(g_cur, g_prev, cwg_ref, cbg_ref, sub)
            sg = jax.nn.sigmoid(a)
            dca = dact_rows * gate * (sg * (1.0 + a * (1.0 - sg)))
            dcg = dact_rows * (a * sg)
            return dca, dcg, taps_a, taps_g

        def du_group(dc, dc_after, cw_ref):
            return (cw_ref[2:3, :] * dc + cw_ref[1:2, :] * _shift_rows_up(dc, dc_after, 1, sub)
                    + cw_ref[0:1, :] * _shift_rows_up(dc, dc_after, 2, sub))

        def strip(r, a_prev, g_prev, dca_after, dcg_after):
            a0, a1 = ua_ref[pl.ds(r, S8), :], ua_ref[pl.ds(r + S8, S8), :]
            g0, g1 = ug_ref[pl.ds(r, S8), :], ug_ref[pl.ds(r + S8, S8), :]
            dca1, dcg1, ta1, tg1 = dconv(a1, a0, g1, g0, da_ref[pl.ds(r + S8, S8), :])
            dca0, dcg0, ta0, tg0 = dconv(a0, a_prev, g0, g_prev, da_ref[pl.ds(r, S8), :])
            du_ref[0, pl.ds(r, GLU_STRIP), :] = jnp.concatenate(
                [du_group(dca0, dca1, cwa_ref), du_group(dca1, dca_after, cwa_ref)], axis=0).astype(BF16)
            du_ref[1, pl.ds(r, GLU_STRIP), :] = jnp.concatenate(
                [du_group(dcg0, dcg1, cwg_ref), du_group(dcg1, dcg_after, cwg_ref)], axis=0).astype(BF16)
            for k in range(CONV_WIDTH):
                wacc_a[k] += dca0 * ta0[k] + dca1 * ta1[k]
                wacc_g[k] += dcg0 * tg0[k] + dcg1 * tg1[k]
            bacc_a[...] += dca0 + dca1
            bacc_g[...] += dcg0 + dcg1
            return dca0, dcg0

        tail = pl.ds(tm - S8, S8)
        dca_after, dcg_after, _, _ = dconv(na_ref[...], ua_ref[tail, :], ng_ref[...], ug_ref[tail, :],
                                           jnp.where(last, 0.0, dn_ref[...]))

        def step(t, carry):
            r = pl.multiple_of((n_strips - 1 - t) * GLU_STRIP, GLU_STRIP)
            before = pl.ds(pl.multiple_of(r - S8, S8), S8)
            return strip(r, ua_ref[before, :], ug_ref[before, :], *carry)

        dca_after, dcg_after = lax.fori_loop(0, n_strips - 1, step, (dca_after, dcg_after))
        strip(0, jnp.where(first, 0.0, pa_ref[...]), jnp.where(first, 0.0, pg_ref[...]), dca_after, dcg_after)

        @pl.when(first)
        def _():
            for r in (dwa_ref, dwg_ref, dba_ref, dbg_ref):
                r[...] = jnp.zeros_like(r)

        for wacc, bacc, dw_ref, db_ref in ((wacc_a, bacc_a, dwa_ref, dba_ref), (wacc_g, bacc_g, dwg_ref, dbg_ref)):
            db_ref[...] += jnp.sum(bacc[...], axis=0, keepdims=True)
            for k in range(CONV_WIDTH):
                dw_ref[k:k + 1, :] += jnp.sum(wacc[k], axis=0, keepdims=True)

    def tile(off):
        return pl.BlockSpec((tm, tn), lambda j, i: (i, j + off))

    def prev(off):
        return pl.BlockSpec((S8, tn), lambda j, i: (jnp.maximum(i * halo_blocks - 1, 0), j + off))

    def nxt(off):
        return pl.BlockSpec((S8, tn), lambda j, i: (jnp.minimum((i + 1) * halo_blocks, n_halo - 1), j + off))

    def cols(rows, off):
        return pl.BlockSpec((rows, tn), lambda j, i: (0, j + off))

    return pl.pallas_call(
        body, name=name, grid=(nj, ni),
        in_specs=[tile(0), tile(nj), prev(0), prev(nj), nxt(0), nxt(nj), tile(0), nxt(0),
                  cols(CONV_WIDTH, 0), cols(CONV_WIDTH, nj), cols(1, 0), cols(1, nj)],
        out_specs=[pl.BlockSpec((2, tm, tn), lambda j, i: (0, i, j)),
                   cols(CONV_WIDTH, 0), cols(CONV_WIDTH, 0), cols(1, 0), cols(1, 0)],
        out_shape=[jax.ShapeDtypeStruct((2, Lp, F), BF16),
                   jax.ShapeDtypeStruct((CONV_WIDTH, F), F32), jax.ShapeDtypeStruct((CONV_WIDTH, F), F32),
                   jax.ShapeDtypeStruct((1, F), F32), jax.ShapeDtypeStruct((1, F), F32)],
        scratch_shapes=[pltpu.VMEM((CONV_WIDTH, S8, tn), F32), pltpu.VMEM((CONV_WIDTH, S8, tn), F32),
                        pltpu.VMEM((S8, tn), F32), pltpu.VMEM((S8, tn), F32)],
        compiler_params=_cparams("parallel", "arbitrary"),
    )(u, u, u, u, u, u, dact, dact, cw, cw, cb, cb)


def _bias_routing(n_heads, width, first_slot):
    h = lax.broadcasted_iota(jnp.int32, (LANES, width), 0)
    col = lax.broadcasted_iota(jnp.int32, (LANES, width), 1)
    base = LANES * (h // 2) + HEAD_DIM * (1 - h % 2) + first_slot
    return [((col == base + t) & (h < n_heads)).astype(BF16) for t in range(3)]


def _three_terms(x):
    hi = x.astype(BF16)
    r1 = x - hi.astype(F32)
    lo = r1.astype(BF16)
    lo2 = (r1 - lo.astype(F32)).astype(BF16)
    return hi, lo, lo2


def _logf_cumsum(pre, bf, name, *, tm, n_heads, width):
    Lp, W = pre.shape

    def body(p_ref, b_ref, c_ref, kx_ref, qx_ref, carry_ref):
        i = pl.program_id(0)

        @pl.when(i == 0)
        def _():
            carry_ref[...] = jnp.zeros_like(carry_ref)

        x = p_ref[...] + b_ref[...]
        lf = jnp.minimum(x, 0.0) - jnp.log(1.0 + jnp.exp(-jnp.abs(x)))
        tri = (lax.broadcasted_iota(jnp.int32, (tm, tm), 0) >= lax.broadcasted_iota(jnp.int32, (tm, tm), 1)).astype(F32)
        c = jnp.dot(tri, lf, precision=lax.Precision.HIGHEST, preferred_element_type=F32) + carry_ref[...]
        c_ref[...] = c
        carry_ref[...] = c[tm - 1:tm, :]
        terms = _three_terms(c)
        slot = lax.broadcasted_iota(jnp.int32, (tm, width), 1) % HEAD_DIM
        ones_k = ((slot >= 3) & (slot < BIAS_SLOTS)).astype(F32)
        ones_q = (slot < 3).astype(F32)
        kx = sum(jnp.dot(t, r, preferred_element_type=F32) for t, r in zip(terms, _bias_routing(n_heads, width, 0)))
        qx = sum(jnp.dot(t, r, preferred_element_type=F32) for t, r in zip(terms, _bias_routing(n_heads, width, 3)))
        kx_ref[...] = (ones_k - kx).astype(BF16)
        qx_ref[...] = (ones_q + qx).astype(BF16)

    row = pl.BlockSpec((tm, W), lambda i: (i, 0))
    wide = pl.BlockSpec((tm, width), lambda i: (i, 0))
    return pl.pallas_call(
        body, name=name, grid=(Lp // tm,),
        in_specs=[row, pl.BlockSpec((1, W), lambda i: (0, 0))], out_specs=[row, wide, wide],
        out_shape=[jax.ShapeDtypeStruct((Lp, W), F32), jax.ShapeDtypeStruct((Lp, width), BF16),
                   jax.ShapeDtypeStruct((Lp, width), BF16)],
        scratch_shapes=[pltpu.VMEM((1, W), F32)],
        compiler_params=_cparams("arbitrary"),
    )(pre, bf)


def _logf_bwd(dc_a, dc_b, pre, bf, name, *, tm):
    Lp, W = pre.shape
    ni = Lp // tm

    def body(dca_ref, dcb_ref, p_ref, b_ref, dpb_ref, db_ref, carry_ref):
        i = pl.program_id(0)

        @pl.when(i == 0)
        def _():
            carry_ref[...] = jnp.zeros_like(carry_ref)
            db_ref[...] = jnp.zeros_like(db_ref)

        triu = (lax.broadcasted_iota(jnp.int32, (tm, tm), 0) <= lax.broadcasted_iota(jnp.int32, (tm, tm), 1)).astype(F32)
        dl = jnp.dot(triu, dca_ref[...] + dcb_ref[...], precision=lax.Precision.HIGHEST,
                     preferred_element_type=F32) + carry_ref[...]
        carry_ref[...] = dl[0:1, :]
        dp = dl * jax.nn.sigmoid(-(p_ref[...] + b_ref[...]))
        dpb_ref[...] = dp.astype(BF16)
        db_ref[...] += jnp.sum(dp, axis=0, keepdims=True)

    rev = pl.BlockSpec((tm, W), lambda i: (ni - 1 - i, 0))
    vec = pl.BlockSpec((1, W), lambda i: (0, 0))
    return pl.pallas_call(
        body, name=name, grid=(ni,),
        in_specs=[rev, rev, rev, vec], out_specs=[rev, vec],
        out_shape=[jax.ShapeDtypeStruct((Lp, W), BF16), jax.ShapeDtypeStruct((1, W), F32)],
        scratch_shapes=[pltpu.VMEM((1, W), F32)],
        compiler_params=_cparams("arbitrary"),
    )(dc_a, dc_b, pre, bf)


def _attn_fwd(qb, kvb, kx, qx, name, *, tq):
    Lp, D = qb.shape
    H = D // HEAD_DIM
    nq = Lp // tq
    S8 = SUBLANES
    assert LANES // HEAD_DIM == 2
    HB = 2 * ATTN_FWD_PAIRS
    W = LANES * ATTN_FWD_PAIRS
    n_scratch = 5

    def body(q_ref, qx_ref, k_ref, kx_ref, v_ref, o_ref, ob_ref, lse_ref, vt_ref, *scratch):
        i = pl.program_id(1)
        heads = [scratch[n_scratch * hb:n_scratch * (hb + 1)] for hb in range(HB)]
        lane = lax.broadcasted_iota(jnp.int32, (tq, LANES), 1)

        def own_lanes(hb, x2, extra2):
            return jnp.where((lane < HEAD_DIM) == (hb % 2 == 0), x2, extra2)

        q_of = [own_lanes(hb, q_ref[:, pl.ds(LANES * (hb // 2), LANES)], qx_ref[:, pl.ds(LANES * (hb // 2), LANES)])
                for hb in range(HB)]

        @pl.when(i == 0)
        def _():
            for j in range(nq):
                vt_ref[j] = jnp.transpose(v_ref[pl.ds(j * tq, tq), :].astype(F32)).astype(BF16)

        for m_ref, l_ref, acc_ref, _, _ in heads:
            m_ref[...] = jnp.full_like(m_ref, NEG_INF)
            l_ref[...] = jnp.zeros_like(l_ref)
            acc_ref[...] = jnp.zeros_like(acc_ref)

        def chunk(j, masked):
            keys = pl.ds(pl.multiple_of(j * tq, tq), tq)
            for hb, (_, _, _, st_ref, _) in enumerate(heads):
                pair = pl.ds(LANES * (hb // 2), LANES)
                k_own = own_lanes(hb, k_ref[keys, pair], kx_ref[keys, pair])
                st_ref[...] = lax.dot_general(k_own, q_of[hb], NT, preferred_element_type=F32)
            for hb, (m_ref, l_ref, acc_ref, st_ref, pt_ref) in enumerate(heads):
                mx = jnp.full((S8, tq), NEG_INF, F32)
                for r0 in range(0, tq, ATTN_STRIP):
                    rows = pl.ds(r0, ATTN_STRIP)
                    st = st_ref[rows, :]
                    if masked:
                        keep = (lax.broadcasted_iota(jnp.int32, (ATTN_STRIP, tq), 1)
                                >= r0 + lax.broadcasted_iota(jnp.int32, (ATTN_STRIP, tq), 0))
                        st = jnp.where(keep, st, NEG_INF)
                        st_ref[rows, :] = st
                    for g0 in range(0, ATTN_STRIP, S8):
                        mx = jnp.maximum(mx, st[g0:g0 + S8])
                m_prev = m_ref[...]
                m_new = jnp.maximum(m_prev, jnp.max(mx, axis=0, keepdims=True))
                alpha = jnp.exp(m_prev - m_new)
                m_ref[...] = m_new
                ls = jnp.zeros((S8, tq), F32)
                for r0 in range(0, tq, ATTN_STRIP):
                    pieces = [jnp.exp(st_ref[pl.ds(r0 + g0, S8), :] - m_new) for g0 in range(0, ATTN_STRIP, S8)]
                    for piece in pieces:
                        ls = ls + piece
                    pt_ref[pl.ds(r0, ATTN_STRIP), :] = jnp.concatenate(pieces, axis=0).astype(BF16)
                l_ref[...] = alpha * l_ref[...] + ls
                pv = jnp.dot(vt_ref[j, pl.ds(LANES * (hb // 2), LANES), :], pt_ref[...], preferred_element_type=F32)
                acc_ref[...] = jnp.concatenate([alpha] * (LANES // S8), axis=0) * acc_ref[...] + pv

        def step(j, carry):
            chunk(j, False)
            return carry

        lax.fori_loop(0, i, step, 0)
        chunk(i, True)
        outs = []
        for hb, (m_ref, l_ref, acc_ref, _, _) in enumerate(heads):
            l_row = jnp.sum(l_ref[...], axis=0, keepdims=True)
            outs.append(acc_ref[...] / l_row)
            lse_ref[hb, 0] = m_ref[0:1, :] + jnp.log(l_row)
        first_rows = lax.broadcasted_iota(jnp.int32, (LANES, tq), 0) < HEAD_DIM
        for pp in range(ATTN_FWD_PAIRS):
            o2 = jnp.transpose(jnp.where(first_rows, outs[2 * pp], outs[2 * pp + 1]))
            o_ref[:, pl.ds(LANES * pp, LANES)] = o2
            ob_ref[:, pl.ds(LANES * pp, LANES)] = o2.astype(BF16)

    per_head = [pltpu.VMEM((S8, tq), F32), pltpu.VMEM((S8, tq), F32), pltpu.VMEM((LANES, tq), F32),
                pltpu.VMEM((tq, tq), F32), pltpu.VMEM((tq, tq), BF16)]
    assert len(per_head) == n_scratch and H % HB == 0
    v_blocks = D // W
    tile = pl.BlockSpec((tq, W), lambda p, i: (i, p))
    whole = pl.BlockSpec((Lp, W), lambda p, i: (0, p))
    return pl.pallas_call(
        body, name=name, grid=(H // HB, nq),
        in_specs=[tile, tile, whole, whole, pl.BlockSpec((Lp, W), lambda p, i: (0, v_blocks + p))],
        out_specs=[tile, tile, pl.BlockSpec((HB, 1, 1, tq), lambda p, i: (p, i, 0, 0))],
        out_shape=[jax.ShapeDtypeStruct((Lp, D), F32), jax.ShapeDtypeStruct((Lp, D), BF16),
                   jax.ShapeDtypeStruct((H, nq, 1, tq), F32)],
        scratch_shapes=[pltpu.VMEM((nq, W, tq), BF16)] + per_head * HB,
        compiler_params=_cparams("parallel", "arbitrary"),
    )(qb, qx, kvb, kx, kvb)


def _attn_bwd(qb, dob, kvb, lse4, delta4, crow4, name, *, tq):
    Lp, D = qb.shape
    H = D // HEAD_DIM
    nq = Lp // tq
    HB = LANES // HEAD_DIM
    lane_tiles = tq // LANES
    n_scratch = 8
    assert HB == 2

    def body(q_ref, do_ref, k_ref, v_ref, lse_ref, dl_ref, c_ref,
             dqb_ref, dk_ref, dv_ref, dcs_ref, dcq_ref, dqt_ref, kt_ref, *scratch):
        j = pl.program_id(1)
        heads = [scratch[n_scratch * hb:n_scratch * (hb + 1)] for hb in range(HB)]
        first_head = lax.broadcasted_iota(jnp.int32, (tq, LANES), 1) < HEAD_DIM

        def split(x2):
            zero = jnp.zeros_like(x2)
            return [jnp.where(first_head, x2, zero), jnp.where(first_head, zero, x2)]

        @pl.when(j == 0)
        def _():
            dqt_ref[...] = jnp.zeros_like(dqt_ref)
            dcq_ref[...] = jnp.zeros_like(dcq_ref)

        k2 = k_ref[...]
        v2 = v_ref[...]
        kt_ref[...] = jnp.transpose(k2.astype(F32)).astype(BF16)
        first_rows = lax.broadcasted_iota(jnp.int32, (LANES, tq), 0) < HEAD_DIM
        for hb, (dk_acc, dv_acc, dc_acc, _, _, _, _, cs_ref) in enumerate(heads):
            dk_acc[...] = jnp.zeros_like(dk_acc)
            dv_acc[...] = jnp.zeros_like(dv_acc)
            dc_acc[...] = jnp.zeros_like(dc_acc)
            cs_ref[...] = jnp.transpose(jnp.broadcast_to(c_ref[hb, j], (LANES, tq)))

        def pair(i, masked):
            queries = pl.ds(pl.multiple_of(i * tq, tq), tq)
            q2 = q_ref[queries, :]
            do2 = do_ref[queries, :]
            q_of, do_of = split(q2), split(do2)
            for hb, (_, _, _, st_ref, dp_ref, _, _, _) in enumerate(heads):
                st_ref[...] = lax.dot_general(k2, q_of[hb], NT, preferred_element_type=F32)
                dp_ref[...] = lax.dot_general(v2, do_of[hb], NT, preferred_element_type=F32)
            dq_parts = []
            for hb, (dk_acc, dv_acc, dc_acc, st_ref, dp_ref, pt_ref, ds_ref, cs_ref) in enumerate(heads):
                bias_q = c_ref[hb, i] - lse_ref[hb, i]
                delta = dl_ref[hb, i]
                col_sum = jnp.zeros((SUBLANES, tq), F32)
                for r0 in range(0, tq, ATTN_STRIP):
                    rows = pl.ds(r0, ATTN_STRIP)
                    st = st_ref[rows, :] + (bias_q - jnp.concatenate([cs_ref[rows, :]] * lane_tiles, axis=1))
                    if masked:
                        keep = (lax.broadcasted_iota(jnp.int32, (ATTN_STRIP, tq), 1)
                                >= r0 + lax.broadcasted_iota(jnp.int32, (ATTN_STRIP, tq), 0))
                        st = jnp.where(keep, st, NEG_INF)
                    pt = jnp.exp(st)
                    dst = pt * (dp_ref[rows, :] - delta)
                    pt_ref[rows, :] = pt.astype(BF16)
                    ds_ref[rows, :] = dst.astype(BF16)
                    dc_acc[rows, :] += jnp.sum(dst, axis=1, keepdims=True)
                    for g0 in range(0, ATTN_STRIP, SUBLANES):
                        col_sum = col_sum + dst[g0:g0 + SUBLANES]
                dcq_ref[hb, i] += jnp.sum(col_sum, axis=0, keepdims=True)
                dv_acc[...] += jnp.dot(pt_ref[...], do2, preferred_element_type=F32)
                dk_acc[...] += jnp.dot(ds_ref[...], q2, preferred_element_type=F32)
                dq_parts.append(jnp.dot(kt_ref[...], ds_ref[...], preferred_element_type=F32))
            dqt_ref[i] += jnp.where(first_rows, dq_parts[0], dq_parts[1])

        def step(i, carry):
            pair(i, False)
            return carry

        pair(j, True)
        lax.fori_loop(j + 1, nq, step, 0)
        dk_ref[...] = jnp.where(first_head, heads[0][0][...], heads[1][0][...]).astype(BF16)
        dv_ref[...] = jnp.where(first_head, heads[0][1][...], heads[1][1][...]).astype(BF16)
        for hb in range(HB):
            dcs_ref[hb, 0] = -jnp.transpose(jnp.broadcast_to(heads[hb][2][...], (tq, LANES)))[0:1, :]

        @pl.when(j == nq - 1)
        def _():
            for i in range(nq):
                dqb_ref[pl.ds(i * tq, tq), :] = jnp.transpose(dqt_ref[i]).astype(BF16)

    per_head = [pltpu.VMEM((tq, LANES), F32), pltpu.VMEM((tq, LANES), F32), pltpu.VMEM((tq, 1), F32),
                pltpu.VMEM((tq, tq), F32), pltpu.VMEM((tq, tq), F32),
                pltpu.VMEM((tq, tq), BF16), pltpu.VMEM((tq, tq), BF16), pltpu.VMEM((tq, LANES), F32)]
    assert len(per_head) == n_scratch
    v_blocks = D // LANES
    whole = pl.BlockSpec((Lp, LANES), lambda p, j: (0, p))
    tile = pl.BlockSpec((tq, LANES), lambda p, j: (j, p))
    rows = pl.BlockSpec((HB, nq, 1, tq), lambda p, j: (p, 0, 0, 0))
    return pl.pallas_call(
        body, name=name, grid=(H // HB, nq),
        in_specs=[whole, whole, tile, pl.BlockSpec((tq, LANES), lambda p, j: (j, v_blocks + p)), rows, rows, rows],
        out_specs=[whole, tile, tile, pl.BlockSpec((HB, 1, 1, tq), lambda p, j: (p, j, 0, 0)), rows],
        out_shape=[jax.ShapeDtypeStruct((Lp, D), BF16), jax.ShapeDtypeStruct((Lp, D), BF16),
                   jax.ShapeDtypeStruct((Lp, D), BF16), jax.ShapeDtypeStruct((H, nq, 1, tq), F32),
                   jax.ShapeDtypeStruct((H, nq, 1, tq), F32)],
        scratch_shapes=[pltpu.VMEM((nq, LANES, tq), F32), pltpu.VMEM((LANES, tq), BF16)] + per_head * HB,
        compiler_params=_cparams("parallel", "arbitrary"),
    )(qb, dob, kvb, kvb, lse4, delta4, crow4)


def _remote(src, dst, send_sems, recv_sems, k, to):
    return pltpu.make_async_remote_copy(src_ref=src, dst_ref=dst, send_sem=send_sems.at[k], recv_sem=recv_sems.at[k],
                                        device_id=to, device_id_type=MESH)


def _place():
    x, y, c = lax.axis_index("x"), lax.axis_index("y"), lax.axis_index("c")
    other_chips = [(1 - x, y), (x, 1 - y), (1 - x, 1 - y)]
    return x, y, c, other_chips


def _all_gather_weights(wb, wf, name):
    Rb, C = wb.shape
    Rf = wf.shape[0]
    hb = Rb // 2

    def body(wb_ref, wf_ref, ob_ref, of_ref, send_sems, recv_sems):
        x, y, c, chips = _place()
        me = 2 * x + y
        sibling = (x, y, 1 - c)

        def half(chip, core):
            return ob_ref.at[chip, pl.ds(core * hb, hb), :]

        sent = []
        for j, (cx, cy) in enumerate(chips):
            sent.append(_remote(wb_ref.at[pl.ds(c * hb, hb), :], half(me, c), send_sems, recv_sems, j, (cx, cy, c)))
            sent.append(_remote(wf_ref, of_ref.at[me], send_sems, recv_sems, 3 + j, (cx, cy, c)))
        for cp in sent:
            cp.start()
        for j, (cx, cy) in enumerate(chips):
            chip = 2 * cx + cy
            _remote(half(chip, c), half(chip, c), send_sems, recv_sems, j, sibling).wait_recv()
            fwd = _remote(half(chip, c), half(chip, c), send_sems, recv_sems, 6 + j, sibling)
            fwd.start()
            sent.append(fwd)
        for j, (cx, cy) in enumerate(chips):
            chip = 2 * cx + cy
            _remote(wf_ref, of_ref.at[chip], send_sems, recv_sems, 3 + j, sibling).wait_recv()
            _remote(half(chip, 1 - c), half(chip, 1 - c), send_sems, recv_sems, 6 + j, sibling).wait_recv()
        for cp in sent:
            cp.wait_send()

    any_spec = pl.BlockSpec(memory_space=pl.ANY)
    return pl.pallas_call(
        body, name=name,
        in_specs=[any_spec, any_spec], out_specs=[any_spec, any_spec],
        out_shape=[jax.ShapeDtypeStruct((N_CHIPS, Rb, C), BF16), jax.ShapeDtypeStruct((N_CHIPS, Rf, C), F32)],
        scratch_shapes=[pltpu.SemaphoreType.DMA((9,)), pltpu.SemaphoreType.DMA((9,))],
    )(wb, wf)


def _half_of(ref, order, half):
    return ref.at[pl.ds(0, N_CHIPS), half] if order == "CH" else ref.at[half]


def _halves_to_sibling(grads, orders, name):
    n = len(grads)

    def body(*refs):
        g_refs, a_refs, (send_sems, recv_sems) = refs[:n], refs[n:2 * n], refs[2 * n:]
        x, y, c, _ = _place()
        copies = [_remote(_half_of(g, o, 1 - c), a, send_sems, recv_sems, k, (x, y, 1 - c))
                  for k, (g, a, o) in enumerate(zip(g_refs, a_refs, orders))]
        for cp in copies:
            cp.start()
        for cp in copies:
            cp.wait()

    any_spec = pl.BlockSpec(memory_space=pl.ANY)
    shapes = [g.shape[2:] for g in grads]
    return pl.pallas_call(
        body, name=name, in_specs=[any_spec] * n, out_specs=[any_spec] * n,
        out_shape=[jax.ShapeDtypeStruct((N_CHIPS,) + s, F32) for s in shapes],
        scratch_shapes=[pltpu.SemaphoreType.DMA((n,)), pltpu.SemaphoreType.DMA((n,))],
    )(*grads)


def _chip_partial(g, a, core, order, wire, name, *, tr):
    _, R, C = a.shape
    narrow = wire != F32

    def body(core_ref, g_ref, a_ref, *outs):
        p = g_ref[0, 0] + a_ref[0]
        outs[0][0] = p
        if narrow:
            outs[1][0] = p.astype(wire)

    if order == "CH":
        g_spec = pl.BlockSpec((1, 1, tr, C), lambda s, i, core_ref: (s, core_ref[0], i, 0))
    else:
        g_spec = pl.BlockSpec((1, 1, tr, C), lambda s, i, core_ref: (core_ref[0], s, i, 0))
    blk = pl.BlockSpec((1, tr, C), lambda s, i, core_ref: (s, i, 0))
    grid_spec = pltpu.PrefetchScalarGridSpec(
        num_scalar_prefetch=1, grid=(N_CHIPS, R // tr), in_specs=[g_spec, blk],
        out_specs=[blk, blk] if narrow else [blk])
    out_shape = [jax.ShapeDtypeStruct((N_CHIPS, R, C), F32)] + ([jax.ShapeDtypeStruct((N_CHIPS, R, C), wire)] if narrow else [])
    outs = pl.pallas_call(body, name=name, grid_spec=grid_spec, out_shape=out_shape,
                          compiler_params=_cparams("parallel", "parallel"))(core, g, a)
    return outs[0], outs[-1]


def _chip_exchange(parts, rep, name):
    n = len(parts)
    rr, C = rep.shape

    def body(*refs):
        p_refs, rep_ref = refs[:n], refs[n]
        land_refs, reps_ref = refs[n + 1:2 * n + 1], refs[2 * n + 1]
        send_sems, recv_sems, local_sem = refs[2 * n + 2:]
        x, y, c, chips = _place()
        me = 4 * x + 2 * y + c
        own = pltpu.make_async_copy(rep_ref, reps_ref.at[me], local_sem.at[0])
        own.start()
        sent = []
        for k, (p, land) in enumerate(zip(p_refs, land_refs)):
            for j, (cx, cy) in enumerate(chips):
                sent.append(_remote(p.at[2 * cx + cy], land.at[j], send_sems, recv_sems, 3 * k + j, (cx, cy, c)))
        for r in range(1, N_DEV):
            fx, fy, fc = (r >> 2) & 1, (r >> 1) & 1, r & 1
            sent.append(_remote(rep_ref, reps_ref.at[me], send_sems, recv_sems, 3 * n - 1 + r, (x ^ fx, y ^ fy, c ^ fc)))
        for cp in sent:
            cp.start()
        for k, (p, land) in enumerate(zip(p_refs, land_refs)):
            for j in range(3):
                _remote(p.at[0], land.at[j], send_sems, recv_sems, 3 * k + j, (x, y, c)).wait_recv()
        for r in range(1, N_DEV):
            fx, fy, fc = (r >> 2) & 1, (r >> 1) & 1, r & 1
            frm = 4 * (x ^ fx) + 2 * (y ^ fy) + (c ^ fc)
            _remote(rep_ref, reps_ref.at[frm], send_sems, recv_sems, 3 * n - 1 + r, (x, y, c)).wait_recv()
        for cp in sent:
            cp.wait_send()
        own.wait()

    any_spec = pl.BlockSpec(memory_space=pl.ANY)
    n_sems = 3 * n + N_DEV - 1
    return pl.pallas_call(
        body, name=name, in_specs=[any_spec] * (n + 1), out_specs=[any_spec] * (n + 1),
        out_shape=[jax.ShapeDtypeStruct((3,) + p.shape[1:], p.dtype) for p in parts]
        + [jax.ShapeDtypeStruct((N_DEV, rr, C), F32)],
        scratch_shapes=[pltpu.SemaphoreType.DMA((n_sems,)), pltpu.SemaphoreType.DMA((n_sems,)),
                        pltpu.SemaphoreType.DMA((1,))],
    )(*parts, rep)


def _adamw_math(w, g, m, v):
    m = ADAM_B1 * m + (1.0 - ADAM_B1) * g
    v = ADAM_B2 * v + (1.0 - ADAM_B2) * (g * g)
    m_hat = m / (1.0 - ADAM_B1 ** ADAM_STEP)
    v_hat = v / (1.0 - ADAM_B2 ** ADAM_STEP)
    delta = -ADAM_LR * (m_hat / (jnp.sqrt(v_hat) + ADAM_EPS) + ADAM_WD * w)
    return delta, m, v


def _adamw_owned(part, landed, w, m, v, place, name, *, tr):
    _, R, C = part.shape

    def body(place_ref, own_ref, land_ref, w_ref, m_ref, v_ref, g_ref, d_ref, mo_ref, vo_ref):
        g = own_ref[0]
        for s in range(3):
            g = g + land_ref[s].astype(F32)
        delta, m_new, v_new = _adamw_math(w_ref[0], g, m_ref[0], v_ref[0])
        g_ref[0] = g
        d_ref[0] = delta
        mo_ref[0] = m_new
        vo_ref[0] = v_new

    half = pl.BlockSpec((1, tr, C), lambda i, place_ref: (place_ref[0], i, 0))
    grid_spec = pltpu.PrefetchScalarGridSpec(
        num_scalar_prefetch=1, grid=(R // tr,),
        in_specs=[pl.BlockSpec((1, tr, C), lambda i, place_ref: (place_ref[1], i, 0)),
                  pl.BlockSpec((3, tr, C), lambda i, place_ref: (0, i, 0)), half, half, half],
        out_specs=[half] * 4)
    return pl.pallas_call(
        body, name=name, grid_spec=grid_spec, out_shape=[jax.ShapeDtypeStruct((2, R, C), F32)] * 4,
        compiler_params=_cparams("parallel"),
    )(place, part, landed, w, m, v)


def _join_halves(bufs, name):
    n = len(bufs)

    def body(*refs):
        out_refs, (send_sems, recv_sems) = refs[n:2 * n], refs[2 * n:]
        x, y, c, _ = _place()
        copies = [_remote(o.at[c], o.at[c], send_sems, recv_sems, k, (x, y, 1 - c)) for k, o in enumerate(out_refs)]
        for cp in copies:
            cp.start()
        for k, o in enumerate(out_refs):
            _remote(o.at[c], o.at[1 - c], send_sems, recv_sems, k, (x, y, 1 - c)).wait_recv()
        for cp in copies:
            cp.wait_send()

    any_spec = pl.BlockSpec(memory_space=pl.ANY)
    return pl.pallas_call(
        body, name=name, in_specs=[any_spec] * n, out_specs=[any_spec] * n,
        out_shape=[jax.ShapeDtypeStruct(b.shape, b.dtype) for b in bufs],
        input_output_aliases={k: k for k in range(n)},
        scratch_shapes=[pltpu.SemaphoreType.DMA((n,)), pltpu.SemaphoreType.DMA((n,))],
    )(*bufs)


def _sum_adamw(own, landed, w, m, v, name, *, tr):
    n = landed.shape[0]
    hr, C = own.shape

    def body(own_ref, land_ref, w_ref, m_ref, v_ref, o_ref):
        g = own_ref[...]
        for s in range(n):
            g = g + land_ref[s]
        delta, m_new, v_new = _adamw_math(w_ref[...], g, m_ref[...], v_ref[...])
        o_ref[0] = g
        o_ref[1] = delta
        o_ref[2] = m_new
        o_ref[3] = v_new

    blk = pl.BlockSpec((tr, C), lambda i: (i, 0))
    return pl.pallas_call(
        body, name=name, grid=(hr // tr,),
        in_specs=[blk, pl.BlockSpec((n, tr, C), lambda i: (0, i, 0)), blk, blk, blk],
        out_specs=pl.BlockSpec((4, tr, C), lambda i: (0, i, 0)),
        out_shape=jax.ShapeDtypeStruct((4, hr, C), F32), compiler_params=_cparams("parallel"),
    )(own, landed, w, m, v)


def _rows_of(shape):
    n = 1
    for d in shape:
        n *= d
    return -(-n // PACK_COLS)


def _pack(arrays, total_rows, dtype):
    parts, used = [], 0
    for a in arrays:
        flat = a.reshape(-1).astype(dtype)
        fill = _rows_of(a.shape) * PACK_COLS - flat.shape[0]
        parts += [flat] + ([jnp.zeros((fill,), dtype)] if fill else [])
        used += _rows_of(a.shape)
    if total_rows > used:
        parts.append(jnp.zeros(((total_rows - used) * PACK_COLS,), dtype))
    return jnp.concatenate(parts).reshape(total_rows, PACK_COLS)


def _unpack(buf, shapes):
    lead = buf.shape[:-2]
    out, r = [], 0
    for shp in shapes:
        n = 1
        for d in shp:
            n *= d
        rows = _rows_of(shp)
        piece = buf[..., r:r + rows, :].reshape(lead + (rows * PACK_COLS,))[..., :n]
        out.append(piece.reshape(lead + tuple(shp)))
        r += rows
    return out


def _join_shards(stacked, axis):
    return jnp.concatenate([stacked[s] for s in range(N_CHIPS)], axis=axis)


def _shard_of(full, axis, chip):
    width = full.shape[axis] // N_CHIPS
    return lax.slice_in_dim(full, chip * width, (chip + 1) * width, axis=axis)


def _local_step(h0, tgt, W, *, seq, tm):
    Lp, D = h0.shape
    H = D // HEAD_DIM
    F2 = W["ffn_w_in"].shape[-1]
    F = F2 // 2
    te = tm // 2
    nq = Lp // tm
    cap = 1408
    tD, t2D = _pick(D, cap), _pick(2 * D, cap)
    tF, tF2 = _pick(F, 2 * cap), _pick(F2, 2 * cap)
    t2Dc, tF2c = _pick(2 * D // N_CHIPS, cap), _pick(F2 // N_CHIPS, cap)
    tFm = tcn = _pick(F, cap)
    tkL = max(t for t in range(tm, 2048 + 1, tm) if Lp % t == 0) if Lp > 2048 else Lp

    def vec(a):
        return a.reshape(1, -1)

    ln_g, ln_b = W["ln_g"], W["ln_b"]
    wf_pad = jnp.pad(W["w_f"], ((0, 0), (0, LANES - H)))
    bf_pad = jnp.pad(W["b_f"], (0, LANES - H)).reshape(1, LANES)

    def ffn_fwd(h, hb, l, tag):
        u = _mm(hb, W["ffn_w_in"][l], "nn", F32, f"ffn{tag}_up", tm=tm, tn=tF2, tk=tD)
        act = _conv_glu_fwd(u, W["ffn_conv_w"][l], vec(W["ffn_conv_b"][l]), f"ffn{tag}_glu", tm=te, tn=tcn)
        normed = _mm_ln(act, W["ffn_w_out"][l], "nn", f"ffn{tag}_down_ln", tm=tm, tk=tF, forward=True,
                        rows=[h], vecs=[vec(ln_g[l, 1]), vec(ln_b[l, 1])])
        return u, act, normed

    def ffn_bwd(dz, dzb, hb, u, act, l, tag, dw_in_acc, dw_out_acc, xh_in, rs_in, g_in):
        dact = _mm(dzb, W["ffn_w_out"][l], "nt", F32, f"ffn{tag}_dact", tm=tm, tn=tF, tk=tD)
        dw_out = _mm(act, dzb, "tn", F32, f"ffn{tag}_dwout", tm=tFm, tn=tD, tk=tkL, layer=l, into=dw_out_acc)
        du, dwa, dwg, dba, dbg = _conv_glu_bwd(u, dact, W["ffn_conv_w"][l], vec(W["ffn_conv_b"][l]),
                                               f"ffn{tag}_dglu", tm=te, tn=tcn)
        dcw = jnp.concatenate([dwa, dwg], axis=1)
        dcb = jnp.concatenate([dba, dbg], axis=1)
        prev = _mm_ln(du, W["ffn_w_in"][l], "nt", f"ffn{tag}_dh_ln", tm=tm, tk=tFm, forward=False,
                      rows=[dz, xh_in, rs_in], vecs=[g_in], a_halves=True)
        dw_in = _mm(hb, du, "tn", F32, f"ffn{tag}_dwin", tm=tD, tn=tF2c, tk=tkL, chips=True, layer=l, into=dw_in_acc,
                    b_halves=True)
        return prev, dw_in, dw_out, dcw, dcb[0]

    diffb, mixpre, h1, h1b, xh1, rs1 = _pool_ln_fwd(h0, W["pool_w"][0], W["pool_scale"], vec(ln_g[0, 0]),
                                                    vec(ln_b[0, 0]), "pool_ln_fwd", tm=te)
    u0, act0, (h2, h2b, xh2, rs2) = ffn_fwd(h1, h1b, 0, "0")

    kvb = _mm(h2b, W["w_kv"], "nn", BF16, "kv_proj", tm=tm, tn=t2D, tk=tD)
    qb = _mm(h2b, W["w_q"][0], "nn", BF16, "q_proj", tm=tm, tn=tD, tk=tD, scale=HEAD_DIM ** -0.5)
    pre = _mm(h2b, wf_pad, "nn", F32, "f_proj", tm=tm, tn=LANES, tk=tD)
    c, kx, qx = _logf_cumsum(pre, bf_pad, "logf_cumsum", tm=tm, n_heads=H, width=D)

    crow4 = c[:, :H].T.reshape(H, nq, 1, tm)
    o_tok, ob, lse4 = _attn_fwd(qb, kvb, kx, qx, "attn_fwd", tq=tm)
    h3, h3b, xh3, rs3 = _mm_ln(ob, W["w_o"][0], "nn", "o_proj_ln", tm=tm, tk=tD, forward=True,
                               rows=[h2], vecs=[vec(ln_g[1, 0]), vec(ln_b[1, 0])])
    u1, act1, (h4, _, xh4, rs4) = ffn_fwd(h3, h3b, 1, "1")

    dz4, dz4b, dg11, db11, loss = _loss_ln_bwd(h4, tgt, xh4, rs4, vec(ln_g[1, 1]), "loss_ln11_bwd", tm=te,
                                               row_lo=N_META, row_hi=N_META + seq)
    (dz3, dz3b, dg10, db10), dw_in, dw_out, dcw1, dcb1 = ffn_bwd(dz4, dz4b, h3b, u1, act1, 1, "1", None, None,
                                                                 xh3, rs3, vec(ln_g[1, 0]))

    dob, delta = _mm(dz3b, W["w_o"][0], "nt", BF16, "o_proj_dx", tm=tm, tn=D, tk=tD, head_dots_with=o_tok)
    dw_o = _mm(ob, dz3b, "tn", F32, "o_proj_dw", tm=tD, tn=tD, tk=tkL)
    dqb, dkb, dvb, dcs, dcq = _attn_bwd(qb, dob, kvb, lse4, delta[:, :H].T.reshape(H, nq, 1, tm), crow4,
                                        "attn_bwd", tq=tm)
    dc_keys = jnp.pad(dcs.reshape(H, Lp).T, ((0, 0), (0, LANES - H)))
    dc_queries = jnp.pad(dcq.reshape(H, Lp).T, ((0, 0), (0, LANES - H)))
    dpreb, dbf = _logf_bwd(dc_keys, dc_queries, pre, bf_pad, "logf_bwd", tm=tm)

    qs = HEAD_DIM ** -0.5
    dw_q = _mm(h2b, dqb, "tn", F32, "q_proj_dw", tm=tD, tn=tD, tk=tkL, scale=qs)
    dw_kv = _mm(h2b, dkb, "tn", F32, "k_proj_dw", tm=tD, tn=t2Dc, tk=tkL, chips=(0, N_CHIPS // 2))
    dw_kv = _mm(h2b, dvb, "tn", F32, "v_proj_dw", tm=tD, tn=t2Dc, tk=tkL, chips=(N_CHIPS // 2, N_CHIPS // 2), into=dw_kv)
    dw_f = _mm(h2b, dpreb, "tn", F32, "f_proj_dw", tm=tD, tn=LANES, tk=tkL)[:, :H]
    dh2 = _mm(dqb, W["w_q"][0], "nt", F32, "q_proj_dx", tm=tm, tn=tD, tk=tD, scale=qs)
    dh2 = _mm(dkb, W["w_kv"][:, :D], "nt", F32, "k_proj_dx", tm=tm, tn=tD, tk=tD, add=dh2)
    dh2 = _mm(dvb, W["w_kv"][:, D:], "nt", F32, "v_proj_dx", tm=tm, tn=tD, tk=tD, add=dh2)
    dz2, dz2b, dg01, db01 = _mm_ln(dpreb, wf_pad, "nt", "f_proj_dx_ln", tm=tm, tk=LANES, forward=False,
                                   rows=[dz3, xh2, rs2], vecs=[vec(ln_g[0, 1])], add=dh2)

    (dz1, _, dg00, db00), dw_in, dw_out, dcw0, dcb0 = ffn_bwd(dz2, dz2b, h1b, u0, act0, 0, "0", dw_in, dw_out,
                                                              xh1, rs1, vec(ln_g[0, 0]))
    dh0, dmb, dscale = _pool_bwd(dz1, mixpre, W["pool_w"][0], W["pool_scale"], "pool_bwd", tm=te)
    dw_pool = _pool_dw(diffb, dmb, "pool_dw", tk=tm)

    grads = {
        "meta": dh0[:N_META],
        "pool_w": dw_pool[None],
        "pool_scale": dscale,
        "w_kv": dw_kv,
        "w_f": dw_f,
        "b_f": dbf[0, :H],
        "w_q": dw_q[None],
        "w_o": dw_o[None],
        "ffn_w_in": dw_in,
        "ffn_conv_w": jnp.stack([dcw0, dcw1]),
        "ffn_conv_b": jnp.stack([dcb0, dcb1]),
        "ffn_w_out": dw_out,
        "ln_g": jnp.stack([jnp.stack([dg00[0], dg01[0]]), jnp.stack([dg10[0], dg11[0]])]),
        "ln_b": jnp.stack([jnp.stack([db00[0], db01[0]]), jnp.stack([db10[0], db11[0]])]),
    }
    return loss, dh0, grads


def _row_block(rows, cols):
    best = SUBLANES
    for t in range(SUBLANES, rows + 1, SUBLANES):
        if rows % t == 0 and t * cols * 4 <= ELEMENTWISE_BLOCK_BYTES:
            best = t
    return best


def _row_tile(length):
    return 768 if length >= 4096 else 128


def kernel(x, meta, pool_w, pool_scale, w_kv, w_f, b_f, w_q, w_o, ffn_w_in, ffn_conv_w, ffn_conv_b, ffn_w_out, ln_g, ln_b, loss_target, m_meta, m_pool_w, m_pool_scale, m_w_kv, m_w_f, m_b_f, m_w_q, m_w_o, m_ffn_w_in, m_ffn_conv_w, m_ffn_conv_b, m_ffn_w_out, m_ln_g, m_ln_b, v_meta, v_pool_w, v_pool_scale, v_w_kv, v_w_f, v_b_f, v_w_q, v_w_o, v_ffn_w_in, v_ffn_conv_w, v_ffn_conv_b, v_ffn_w_out, v_ln_g, v_ln_b):
    weights = dict(meta=meta, pool_w=pool_w, pool_scale=pool_scale, w_kv=w_kv, w_f=w_f, b_f=b_f, w_q=w_q, w_o=w_o,
                   ffn_w_in=ffn_w_in, ffn_conv_w=ffn_conv_w, ffn_conv_b=ffn_conv_b, ffn_w_out=ffn_w_out,
                   ln_g=ln_g, ln_b=ln_b)
    mom1 = dict(meta=m_meta, pool_w=m_pool_w, pool_scale=m_pool_scale, w_kv=m_w_kv, w_f=m_w_f, b_f=m_b_f, w_q=m_w_q,
                w_o=m_w_o, ffn_w_in=m_ffn_w_in, ffn_conv_w=m_ffn_conv_w, ffn_conv_b=m_ffn_conv_b,
                ffn_w_out=m_ffn_w_out, ln_g=m_ln_g, ln_b=m_ln_b)
    mom2 = dict(meta=v_meta, pool_w=v_pool_w, pool_scale=v_pool_scale, w_kv=v_w_kv, w_f=v_w_f, b_f=v_b_f, w_q=v_w_q,
                w_o=v_w_o, ffn_w_in=v_ffn_w_in, ffn_conv_w=v_ffn_conv_w, ffn_conv_b=v_ffn_conv_b,
                ffn_w_out=v_ffn_w_out, ln_g=v_ln_g, ln_b=v_ln_b)
    _, seq, D = x.shape
    L = N_META + seq
    tm = _row_tile(L)
    Lp = _round_up(L, tm)
    c_idx = lax.axis_index("c")
    chip = 2 * lax.axis_index("x") + lax.axis_index("y")

    shard_shapes = {n: weights[n].shape for n in SHARDED}
    rows_b = _round_up(sum(_rows_of(shard_shapes[n]) for n in MATMUL_WEIGHTS), 32)
    rows_f = _round_up(sum(_rows_of(shard_shapes[n]) for n in VECTOR_WEIGHTS), SUBLANES)
    wb = _pack([weights[n] for n in MATMUL_WEIGHTS], rows_b, BF16)
    wf = _pack([weights[n] for n in VECTOR_WEIGHTS], rows_f, F32)
    gb, gf = _all_gather_weights(wb, wf, "weights_all_gather")
    gb = lax.dynamic_update_index_in_dim(gb, wb, chip, axis=0)
    gf = lax.dynamic_update_index_in_dim(gf, wf, chip, axis=0)
    full = {}
    for names, buf in ((MATMUL_WEIGHTS, gb), (VECTOR_WEIGHTS, gf)):
        for n, stacked in zip(names, _unpack(buf, [shard_shapes[n] for n in names])):
            full[n] = _join_shards(stacked, SHARD_AXIS[n])
    full["b_f"] = b_f
    full["ffn_conv_b"] = ffn_conv_b

    pad = jnp.zeros((Lp - L, D), F32)
    h0 = jnp.concatenate([full["meta"], x[0], pad], axis=0)
    tgt = jnp.concatenate([jnp.zeros((N_META, D), F32), loss_target[0], pad], axis=0)
    loss, dh0, grads = _local_step(h0, tgt, full, seq=seq, tm=tm)
    loss = lax.psum(loss[0, 0], AXES)
    grad_x = dh0[N_META:L][None]

    core = c_idx.astype(jnp.int32).reshape(1)
    place = jnp.stack([c_idx, chip]).astype(jnp.int32)
    small_shapes = [shard_shapes[n] for n in SMALL_SHARDED]
    rows_s = _round_up(sum(_rows_of(s) for s in small_shapes), 2 * LANES)

    def packed_small(d):
        return _pack([d[n] for n in SMALL_SHARDED], rows_s, F32).reshape(2, rows_s // 2, PACK_COLS)

    names, orders, wires, g_views, wmv = [], [], [], [], []
    for n, order in BIG_SHARDED:
        shp = shard_shapes[n]
        C = shp[-1]
        R = weights[n].size // C // 2
        lead = (N_CHIPS, 2) if order == "CH" else (2, N_CHIPS)
        names.append(n)
        orders.append(order)
        wires.append(BF16)
        g_views.append(grads[n].reshape(lead + (R, C)))
        wmv.append([d[n].reshape(2, R, C) for d in (weights, mom1, mom2)])
    names.append("small")
    orders.append("CH")
    wires.append(F32)
    g_views.append(jnp.stack([_pack([_shard_of(grads[n], SHARD_AXIS[n], s) for n in SMALL_SHARDED], rows_s, F32)
                              for s in range(N_CHIPS)]).reshape(N_CHIPS, 2, rows_s // 2, PACK_COLS))
    wmv.append([packed_small(d) for d in (weights, mom1, mom2)])

    from_sibling = _halves_to_sibling(g_views, orders, "grads_to_sibling")
    parts, on_wire = [], []
    for n, order, wire, g, a in zip(names, orders, wires, g_views, from_sibling):
        p, pw = _chip_partial(g, a, core, order, wire, f"chip_sum_{n}", tr=_row_block(a.shape[1], a.shape[2]))
        parts.append(p)
        on_wire.append(pw)

    rep_shapes = [weights[n].shape for n in REPLICATED]
    rows_r = _round_up(sum(_rows_of(s) for s in rep_shapes), SUBLANES)
    rep = _pack([grads[n] for n in REPLICATED], rows_r, F32)
    *landed, reps = _chip_exchange(on_wire, rep, "grads_chip_exchange")

    halves = []
    for n, p, b, (w_, m_, v_) in zip(names, parts, landed, wmv):
        halves += _adamw_owned(p, b, w_, m_, v_, place, f"adamw_{n}", tr=_row_block(p.shape[1], p.shape[2]))
    joined = _join_halves(halves, "results_to_sibling")
    out = {}
    for k, n in enumerate(names[:-1]):
        out[n] = [a.reshape(shard_shapes[n]) for a in joined[4 * k:4 * k + 4]]
    small_out = [_unpack(a.reshape(rows_s, PACK_COLS), small_shapes) for a in joined[-4:]]
    for k, n in enumerate(SMALL_SHARDED):
        out[n] = [small_out[kind][k] for kind in range(4)]

    def packr(d):
        return _pack([d[n] for n in REPLICATED], rows_r, F32)

    res_r = _sum_adamw(reps[0], reps[1:], packr(weights), packr(mom1), packr(mom2), "adamw_replicated", tr=rows_r)
    rep_out = _unpack(res_r, rep_shapes)

    out.update({n: a for n, a in zip(REPLICATED, rep_out)})
    result = [loss, grad_x]
    for k in range(4):
        result += [out[n][k] for n in WEIGHT_ORDER]
    return tuple(result)
```

```python
import functools

import jax
import jax.numpy as jnp
from jax import lax
from jax.experimental import pallas as pl
from jax.experimental.pallas import tpu as pltpu

N_META = 16
POOL_WINDOWS = (2, 4, 8, 16)
MAX_WINDOW = max(POOL_WINDOWS)
N_GROUPS = len(POOL_WINDOWS)
HEAD_DIM = 64
DEPTH = 2
CONV_WIDTH = 3
ALPHA = (2.0 * DEPTH) ** 0.25
LN_EPS = 1e-5
NEG_INF = -1e30
ADAM_LR = 0.001
ADAM_B1 = 0.9
ADAM_B2 = 0.999
ADAM_EPS = 1e-08
ADAM_WD = 0.01
ADAM_STEP = 10

F32 = jnp.float32
BF16 = jnp.bfloat16
ATTN_FWD_PAIRS = 2
BIAS_SLOTS = 6
ATTN_STRIP = 32
GLU_STRIP = 16
LANES = 128
SUBLANES = 8
PACK_COLS = 1024
VMEM_LIMIT = 56 * 1024 * 1024
AXES = ("x", "y", "c")
MESH = pl.DeviceIdType.MESH

NN = (((1,), (0,)), ((), ()))
NT = (((1,), (1,)), ((), ()))
TN = (((0,), (0,)), ((), ()))

SHARD_AXIS = {"meta": 1, "pool_w": 2, "pool_scale": 1, "w_kv": 1, "w_f": 0, "w_q": 1, "w_o": 1,
              "ffn_w_in": 2, "ffn_conv_w": 2, "ffn_w_out": 1, "ln_g": 2, "ln_b": 2}
SHARDED = ("meta", "pool_w", "pool_scale", "w_kv", "w_f", "w_q", "w_o", "ffn_w_in", "ffn_conv_w",
           "ffn_w_out", "ln_g", "ln_b")
REPLICATED = ("b_f", "ffn_conv_b")
MATMUL_WEIGHTS = ("pool_w", "w_kv", "w_f", "w_q", "w_o", "ffn_w_in", "ffn_w_out")
VECTOR_WEIGHTS = ("meta", "pool_scale", "ffn_conv_w", "ln_g", "ln_b")
WEIGHT_ORDER = ("meta", "pool_w", "pool_scale", "w_kv", "w_f", "b_f", "w_q", "w_o", "ffn_w_in",
                "ffn_conv_w", "ffn_conv_b", "ffn_w_out", "ln_g", "ln_b")
BIG_SHARDED = (("w_kv", "CH"), ("w_q", "CH"), ("w_o", "CH"), ("ffn_w_in", "HC"), ("ffn_w_out", "HC"))
SMALL_SHARDED = ("meta", "pool_w", "pool_scale", "w_f", "ffn_conv_w", "ln_g", "ln_b")
ELEMENTWISE_BLOCK_BYTES = 3 * 512 * 1024
N_CHIPS = 4
N_DEV = 8


def _cparams(*sem):
    return pltpu.CompilerParams(dimension_semantics=sem, vmem_limit_bytes=VMEM_LIMIT)


def _round_up(n, m):
    return (n + m - 1) // m * m


def _pick(n, cap):
    if n <= cap:
        return n
    best = 0
    for t in range(LANES, cap + 1, LANES):
        if n % t == 0:
            best = t
    assert best, (n, cap)
    return best


def _mm(a, b, mode, out_dtype, name, *, tm, tn, tk, scale=None, add=None, chips=False, layer=None, into=None,
        b_halves=False, head_dots_with=None):
    if mode == "nn":
        (M, K), N = a.shape, b.shape[1]
    elif mode == "nt":
        (M, K), N = a.shape, b.shape[0]
    elif b_halves:
        (K, M), N = a.shape, 2 * b.shape[2]
    else:
        (K, M), N = a.shape, b.shape[1]
    assert M % tm == 0 and N % tn == 0 and K % tk == 0, (name, M, N, K, tm, tn, tk)
    nk = K // tk
    dn = {"nn": NN, "nt": NT, "tn": TN}[mode]
    has_add = add is not None
    has_into = into is not None
    has_dots = head_dots_with is not None
    assert not (has_add and (chips or layer is not None))
    assert not has_dots or (tn == N and not (chips or layer is not None or has_into))

    def body(*refs):
        a_ref, b_ref = refs[0], refs[1]
        add_ref = refs[2] if has_add else None
        with_ref = refs[2 + has_add + has_into] if has_dots else None
        o_ref = refs[2 + has_add + has_into + has_dots]
        dots_ref = refs[3 + has_add + has_into + has_dots] if has_dots else None
        acc_ref = refs[-1] if nk > 1 else None
        k = pl.program_id(2)
        part = lax.dot_general(a_ref[...], b_ref[0] if b_halves else b_ref[...], dn, preferred_element_type=F32)

        def finish(r):
            if scale is not None:
                r = r * scale
            if has_add:
                r = r + add_ref[...]
            out = r.astype(out_dtype)
            o_ref[...] = out.reshape(o_ref.shape)
            if has_dots:
                sel = (lax.broadcasted_iota(jnp.int32, (N, LANES), 0) // HEAD_DIM
                       == lax.broadcasted_iota(jnp.int32, (N, LANES), 1)).astype(F32)
                dots_ref[...] = jnp.dot(out.astype(F32) * with_ref[...], sel, precision=lax.Precision.HIGHEST,
                                        preferred_element_type=F32)

        if nk == 1:
            finish(part)
        else:
            @pl.when(k == 0)
            def _():
                acc_ref[...] = part

            @pl.when(k > 0)
            def _():
                acc_ref[...] += part

            @pl.when(k == nk - 1)
            def _():
                finish(acc_ref[...])

    if mode == "nn":
        a_spec = pl.BlockSpec((tm, tk), lambda j, i, k: (i, k))
        b_spec = pl.BlockSpec((tk, tn), lambda j, i, k: (k, j))
    elif mode == "nt":
        a_spec = pl.BlockSpec((tm, tk), lambda j, i, k: (i, k))
        b_spec = pl.BlockSpec((tn, tk), lambda j, i, k: (j, k))
    else:
        a_spec = pl.BlockSpec((tk, tm), lambda j, i, k: (k, i))
        b_spec = pl.BlockSpec((tk, tn), lambda j, i, k: (k, j))
        if b_halves:
            per_half = N // 2 // tn
            b_spec = pl.BlockSpec((1, tk, tn), lambda j, i, k: (j // per_half, k, j % per_half))
    out_dims, blk = (M, N), (tm, tn)
    if chips:
        base, count = (0, N_CHIPS) if chips is True else chips
        per_chip = N // count // tn
        assert per_chip * tn * count == N, (name, N, tn)
        out_dims, blk = (N_CHIPS, M, N // count), (1, tm, tn)
        where = lambda j, i: (base + j // per_chip, i, j % per_chip)
    else:
        where = lambda j, i: (i, j)
    if layer is not None:
        out_dims, blk = (DEPTH,) + out_dims, (1,) + blk
        o_spec = pl.BlockSpec(blk, lambda j, i, k: (layer,) + where(j, i))
    else:
        o_spec = pl.BlockSpec(blk, lambda j, i, k: where(j, i))
    in_specs = ([a_spec, b_spec] + ([o_spec] if has_add else []) + ([pl.BlockSpec(memory_space=pl.ANY)] if has_into else [])
                + ([o_spec] if has_dots else []))
    args = (a, b) + ((add,) if has_add else ()) + ((into,) if has_into else ()) + ((head_dots_with,) if has_dots else ())
    out_specs, out_shape = o_spec, jax.ShapeDtypeStruct(out_dims, out_dtype)
    if has_dots:
        out_specs = [o_spec, pl.BlockSpec((tm, LANES), lambda j, i, k: (i, 0))]
        out_shape = [out_shape, jax.ShapeDtypeStruct((M, LANES), F32)]
    return pl.pallas_call(
        body, name=name, grid=(N // tn, M // tm, nk),
        in_specs=in_specs, out_specs=out_specs,
        out_shape=out_shape,
        input_output_aliases={2 + has_add: 0} if has_into else {},
        scratch_shapes=[pltpu.VMEM((tm, tn), F32)] if nk > 1 else [],
        compiler_params=_cparams("parallel", "parallel", "arbitrary"),
    )(*args)


def _ln_math(z, g, b):
    mu = jnp.mean(z, axis=-1, keepdims=True)
    zc = z - mu
    var = jnp.mean(zc * zc, axis=-1, keepdims=True)
    rstd = lax.rsqrt(var + LN_EPS)
    xh = zc * rstd
    return xh * g + b, xh, rstd


def _mm_ln(a, b, mode, name, *, tm, tk, forward, rows, vecs, scale=None, add=None, a_halves=False):
    assert mode in ("nn", "nt")
    M, K = (a.shape[1], 2 * a.shape[2]) if a_halves else a.shape
    N = b.shape[1] if mode == "nn" else b.shape[0]
    assert M % tm == 0 and K % tk == 0, (name, M, K, tm, tk)
    nk = K // tk
    ni = M // tm
    dn = {"nn": NN, "nt": NT}[mode]
    has_add = add is not None
    n_in = 2 + has_add + len(rows) + len(vecs)

    def body(*refs):
        a_ref, b_ref = refs[0], refs[1]
        add_ref = refs[2] if has_add else None
        row_refs = refs[2 + has_add:2 + has_add + len(rows)]
        vec_refs = refs[2 + has_add + len(rows):n_in]
        outs = refs[n_in:n_in + 4]
        acc_ref = refs[-1] if nk > 1 else None
        i, k = pl.program_id(0), pl.program_id(1)
        part = lax.dot_general(a_ref[0] if a_halves else a_ref[...], b_ref[...], dn, preferred_element_type=F32)

        def finish(y):
            if scale is not None:
                y = y * scale
            if has_add:
                y = y + add_ref[...]
            if forward:
                h, xh, rstd = _ln_math(ALPHA * row_refs[0][...] + y, vec_refs[0][...], vec_refs[1][...])
                outs[0][...] = h
                outs[1][...] = h.astype(BF16)
                outs[2][...] = xh
                outs[3][...] = rstd
            else:
                dy = ALPHA * row_refs[0][...] + y
                x = row_refs[1][...]
                dxh = dy * vec_refs[0][...]
                m1 = jnp.mean(dxh, axis=-1, keepdims=True)
                m2 = jnp.mean(dxh * x, axis=-1, keepdims=True)
                dz = row_refs[2][...] * (dxh - m1 - x * m2)
                outs[0][...] = dz
                outs[1][...] = dz.astype(BF16)

                @pl.when(i == 0)
                def _():
                    outs[2][...] = jnp.zeros_like(outs[2])
                    outs[3][...] = jnp.zeros_like(outs[3])

                outs[2][...] += jnp.sum(dy * x, axis=0, keepdims=True)
                outs[3][...] += jnp.sum(dy, axis=0, keepdims=True)

        if nk == 1:
            finish(part)
        else:
            @pl.when(k == 0)
            def _():
                acc_ref[...] = part

            @pl.when(k > 0)
            def _():
                acc_ref[...] += part

            @pl.when(k == nk - 1)
            def _():
                finish(acc_ref[...])

    a_spec = pl.BlockSpec((tm, tk), lambda i, k: (i, k))
    if a_halves:
        per_half = nk // 2
        a_spec = pl.BlockSpec((1, tm, tk), lambda i, k: (k // per_half, i, k % per_half))
    b_spec = pl.BlockSpec((tk, N), lambda i, k: (k, 0)) if mode == "nn" else pl.BlockSpec((N, tk), lambda i, k: (0, k))
    row = pl.BlockSpec((tm, N), lambda i, k: (i, 0))
    col = pl.BlockSpec((tm, 1), lambda i, k: (i, 0))
    vec = pl.BlockSpec((1, N), lambda i, k: (0, 0))
    row_specs = [row if r.shape[1] == N else col for r in rows]
    if forward:
        out_specs = [row, row, row, col]
        out_shape = [jax.ShapeDtypeStruct((M, N), F32), jax.ShapeDtypeStruct((M, N), BF16),
                     jax.ShapeDtypeStruct((M, N), F32), jax.ShapeDtypeStruct((M, 1), F32)]
    else:
        out_specs = [row, row, vec, vec]
        out_shape = [jax.ShapeDtypeStruct((M, N), F32), jax.ShapeDtypeStruct((M, N), BF16),
                     jax.ShapeDtypeStruct((1, N), F32), jax.ShapeDtypeStruct((1, N), F32)]
    args = (a, b) + ((add,) if has_add else ()) + tuple(rows) + tuple(vecs)
    return pl.pallas_call(
        body, name=name, grid=(ni, nk),
        in_specs=[a_spec, b_spec] + ([row] if has_add else []) + row_specs + [vec] * len(vecs),
        out_specs=out_specs, out_shape=out_shape,
        scratch_shapes=[pltpu.VMEM((tm, N), F32)] if nk > 1 else [],
        compiler_params=_cparams("parallel" if forward else "arbitrary", "arbitrary"),
    )(*args)


def _pool_ln_fwd(h0, pw, ps, g, b, name, *, tm):
    Lp, D = h0.shape
    G = D // N_GROUPS
    halo_blocks = tm // MAX_WINDOW

    def body(x_ref, halo_ref, pw_ref, ps_ref, g_ref, b_ref,
             diff_ref, mix_ref, h_ref, hb_ref, xh_ref, rs_ref, ext_ref):
        i = pl.program_id(0)
        ext_ref[0:MAX_WINDOW, :] = jnp.where(i == 0, 0.0, halo_ref[...])
        ext_ref[MAX_WINDOW:MAX_WINDOW + tm, :] = x_ref[...]
        t1 = (i * tm + 1 + lax.broadcasted_iota(jnp.int32, (tm, 1), 0)).astype(F32)
        for gi, w in enumerate(POOL_WINDOWS):
            lo, hi = gi * G, (gi + 1) * G
            xg = x_ref[:, lo:hi]
            win = xg
            for j in range(1, w):
                win = win + ext_ref[MAX_WINDOW - j:MAX_WINDOW - j + tm, lo:hi]
            d = (win / jnp.minimum(t1, float(w)) - xg).astype(BF16)
            diff_ref[:, lo:hi] = d
            mix_ref[:, lo:hi] = jnp.dot(d, pw_ref[gi], preferred_element_type=F32)
        z = ALPHA * x_ref[...] + mix_ref[...] * ps_ref[...]
        h, xh, rstd = _ln_math(z, g_ref[...], b_ref[...])
        h_ref[...] = h
        hb_ref[...] = h.astype(BF16)
        xh_ref[...] = xh
        rs_ref[...] = rstd

    row = pl.BlockSpec((tm, D), lambda i: (i, 0))
    vec = pl.BlockSpec((1, D), lambda i: (0, 0))
    return pl.pallas_call(
        body, name=name, grid=(Lp // tm,),
        in_specs=[row,
                  pl.BlockSpec((MAX_WINDOW, D), lambda i: (jnp.maximum(i * halo_blocks - 1, 0), 0)),
                  pl.BlockSpec((N_GROUPS, G, G), lambda i: (0, 0, 0)), vec, vec, vec],
        out_specs=[row, row, row, row, row, pl.BlockSpec((tm, 1), lambda i: (i, 0))],
        out_shape=[jax.ShapeDtypeStruct((Lp, D), BF16), jax.ShapeDtypeStruct((Lp, D), F32),
                   jax.ShapeDtypeStruct((Lp, D), F32), jax.ShapeDtypeStruct((Lp, D), BF16),
                   jax.ShapeDtypeStruct((Lp, D), F32), jax.ShapeDtypeStruct((Lp, 1), F32)],
        scratch_shapes=[pltpu.VMEM((tm + MAX_WINDOW, D), F32)],
        compiler_params=_cparams("parallel"),
    )(h0, h0, pw, ps, g, b)


def _pool_bwd(dz, mixpre, pw, ps, name, *, tm):
    Lp, D = dz.shape
    G = D // N_GROUPS
    halo_blocks = tm // MAX_WINDOW
    n_halo = Lp // MAX_WINDOW
    ni = Lp // tm
    R = tm + MAX_WINDOW

    def body(dz_ref, halo_ref, mix_ref, pw_ref, ps_ref, dh_ref, dmb_ref, dsc_ref, ext_ref, dp_ref):
        i = pl.program_id(0)
        ext_ref[0:tm, :] = dz_ref[...]
        ext_ref[tm:R, :] = jnp.where(i == ni - 1, 0.0, halo_ref[...])
        dmix = (ext_ref[...] * ps_ref[...]).astype(BF16)
        dmb_ref[...] = dmix[0:tm]

        @pl.when(i == 0)
        def _():
            dsc_ref[...] = jnp.zeros_like(dsc_ref)

        dsc_ref[...] += jnp.sum(dz_ref[...] * mix_ref[...], axis=0, keepdims=True)
        t1 = (i * tm + 1 + lax.broadcasted_iota(jnp.int32, (R, 1), 0)).astype(F32)
        for gi, w in enumerate(POOL_WINDOWS):
            lo, hi = gi * G, (gi + 1) * G
            dd = lax.dot_general(dmix[:, lo:hi], pw_ref[gi], NT, preferred_element_type=F32)
            dp_ref[:, lo:hi] = dd / jnp.minimum(t1, float(w))
            back = dp_ref[0:tm, lo:hi]
            for j in range(1, w):
                back = back + dp_ref[j:j + tm, lo:hi]
            dh_ref[:, lo:hi] = ALPHA * dz_ref[:, lo:hi] - dd[0:tm] + back

    row = pl.BlockSpec((tm, D), lambda i: (i, 0))
    vec = pl.BlockSpec((1, D), lambda i: (0, 0))
    return pl.pallas_call(
        body, name=name, grid=(ni,),
        in_specs=[row,
                  pl.BlockSpec((MAX_WINDOW, D), lambda i: (jnp.minimum((i + 1) * halo_blocks, n_halo - 1), 0)),
                  row, pl.BlockSpec((N_GROUPS, G, G), lambda i: (0, 0, 0)), vec],
        out_specs=[row, row, vec],
        out_shape=[jax.ShapeDtypeStruct((Lp, D), F32), jax.ShapeDtypeStruct((Lp, D), BF16),
                   jax.ShapeDtypeStruct((1, D), F32)],
        scratch_shapes=[pltpu.VMEM((R, D), F32), pltpu.VMEM((R, D), F32)],
        compiler_params=_cparams("arbitrary"),
    )(dz, dz, mixpre, pw, ps)


def _pool_dw(diffb, dmb, name, *, tk):
    Lp, D = diffb.shape
    G = D // N_GROUPS

    def body(a_ref, b_ref, o_ref):
        @pl.when(pl.program_id(1) == 0)
        def _():
            o_ref[...] = jnp.zeros_like(o_ref)

        o_ref[0] += lax.dot_general(a_ref[...], b_ref[...], TN, preferred_element_type=F32)

    blk = pl.BlockSpec((tk, G), lambda g, k: (k, g))
    return pl.pallas_call(
        body, name=name, grid=(N_GROUPS, Lp // tk),
        in_specs=[blk, blk], out_specs=pl.BlockSpec((1, G, G), lambda g, k: (g, 0, 0)),
        out_shape=jax.ShapeDtypeStruct((N_GROUPS, G, G), F32),
        compiler_params=_cparams("parallel", "arbitrary"),
    )(diffb, dmb)


def _loss_ln_bwd(h, tgt, xh, rs, g, name, *, tm, row_lo, row_hi):
    Lp, D = h.shape

    def body(h_ref, t_ref, xh_ref, rs_ref, g_ref, dz_ref, dzb_ref, dg_ref, db_ref, loss_ref):
        i = pl.program_id(0)
        r = i * tm + lax.broadcasted_iota(jnp.int32, (tm, 1), 0)
        valid = (r >= row_lo) & (r < row_hi)
        e = jnp.where(valid, h_ref[...] - t_ref[...], 0.0)
        dy = e * (1.0 / D)
        x = xh_ref[...]
        dxh = dy * g_ref[...]
        m1 = jnp.mean(dxh, axis=-1, keepdims=True)
        m2 = jnp.mean(dxh * x, axis=-1, keepdims=True)
        dz = rs_ref[...] * (dxh - m1 - x * m2)
        dz_ref[...] = dz
        dzb_ref[...] = dz.astype(BF16)

        @pl.when(i == 0)
        def _():
            dg_ref[...] = jnp.zeros_like(dg_ref)
            db_ref[...] = jnp.zeros_like(db_ref)
            loss_ref[...] = jnp.zeros_like(loss_ref)

        dg_ref[...] += jnp.sum(dy * x, axis=0, keepdims=True)
        db_ref[...] += jnp.sum(dy, axis=0, keepdims=True)
        loss_ref[...] += 0.5 * jnp.sum(jnp.mean(e * e, axis=-1, keepdims=True), axis=0, keepdims=True)

    row = pl.BlockSpec((tm, D), lambda i: (i, 0))
    vec = pl.BlockSpec((1, D), lambda i: (0, 0))
    return pl.pallas_call(
        body, name=name, grid=(Lp // tm,),
        in_specs=[row, row, row, pl.BlockSpec((tm, 1), lambda i: (i, 0)), vec],
        out_specs=[row, row, vec, vec, pl.BlockSpec((1, 1), lambda i: (0, 0))],
        out_shape=[jax.ShapeDtypeStruct((Lp, D), F32), jax.ShapeDtypeStruct((Lp, D), BF16),
                   jax.ShapeDtypeStruct((1, D), F32), jax.ShapeDtypeStruct((1, D), F32),
                   jax.ShapeDtypeStruct((1, 1), F32)],
        compiler_params=_cparams("arbitrary"),
    )(h, tgt, xh, rs, g)


def _shift_rows_down(cur, prev, s, sub):
    return jnp.where(sub >= s, pltpu.roll(cur, s, 0), pltpu.roll(prev, s, 0))


def _shift_rows_up(cur, nxt, s, sub):
    return jnp.where(sub < SUBLANES - s, pltpu.roll(cur, SUBLANES - s, 0), pltpu.roll(nxt, SUBLANES - s, 0))


def _conv_group(cur, prev, cw_ref, cb_ref, sub):
    taps = [_shift_rows_down(cur, prev, 2, sub), _shift_rows_down(cur, prev, 1, sub), cur]
    c = cb_ref[...] + cw_ref[0:1, :] * taps[0] + cw_ref[1:2, :] * taps[1] + cw_ref[2:3, :] * taps[2]
    return c, taps


def _conv_glu_fwd(u, cw, cb, name, *, tm, tn):
    Lp, F2 = u.shape
    F = F2 // 2
    nj = F // tn
    halo_blocks = tm // SUBLANES
    S8 = SUBLANES
    assert GLU_STRIP == 2 * S8 and tm % GLU_STRIP == 0

    def body(ua_ref, ug_ref, pa_ref, pg_ref, cwa_ref, cwg_ref, cba_ref, cbg_ref, o_ref):
        first = pl.program_id(1) == 0
        sub = lax.broadcasted_iota(jnp.int32, (S8, tn), 0)

        def strip(r, prev_a, prev_g):
            out = []
            for g0 in (0, S8):
                a_cur = ua_ref[pl.ds(r + g0, S8), :]
                g_cur = ug_ref[pl.ds(r + g0, S8), :]
                a, _ = _conv_group(a_cur, prev_a, cwa_ref, cba_ref, sub)
                gate, _ = _conv_group(g_cur, prev_g, cwg_ref, cbg_ref, sub)
                out.append(a * jax.nn.sigmoid(a) * gate)
                prev_a, prev_g = a_cur, g_cur
            o_ref[pl.ds(r, GLU_STRIP), :] = jnp.concatenate(out, axis=0).astype(BF16)

        strip(0, jnp.where(first, 0.0, pa_ref[...]), jnp.where(first, 0.0, pg_ref[...]))

        def step(k, carry):
            r = pl.multiple_of(k * GLU_STRIP, GLU_STRIP)
            before = pl.ds(pl.multiple_of(r - S8, S8), S8)
            strip(r, ua_ref[before, :], ug_ref[before, :])
            return carry

        lax.fori_loop(1, tm // GLU_STRIP, step, 0)

    def prev(off):
        return pl.BlockSpec((SUBLANES, tn), lambda j, i: (jnp.maximum(i * halo_blocks - 1, 0), j + off))

    def cols(rows, off):
        return pl.BlockSpec((rows, tn), lambda j, i: (0, j + off))

    return pl.pallas_call(
        body, name=name, grid=(nj, Lp // tm),
        in_specs=[pl.BlockSpec((tm, tn), lambda j, i: (i, j)), pl.BlockSpec((tm, tn), lambda j, i: (i, j + nj)),
                  prev(0), prev(nj), cols(CONV_WIDTH, 0), cols(CONV_WIDTH, nj), cols(1, 0), cols(1, nj)],
        out_specs=pl.BlockSpec((tm, tn), lambda j, i: (i, j)),
        out_shape=jax.ShapeDtypeStruct((Lp, F), BF16),
        compiler_params=_cparams("parallel", "parallel"),
    )(u, u, u, u, cw, cw, cb, cb)


def _conv_glu_bwd(u, dact, cw, cb, name, *, tm, tn):
    Lp, F2 = u.shape
    F = F2 // 2
    nj = F // tn
    ni = Lp // tm
    halo_blocks = tm // SUBLANES
    n_halo = Lp // SUBLANES
    S8 = SUBLANES
    n_strips = tm // GLU_STRIP
    assert GLU_STRIP == 2 * S8 and tm % GLU_STRIP == 0

    def body(ua_ref, ug_ref, pa_ref, pg_ref, na_ref, ng_ref, da_ref, dn_ref,
             cwa_ref, cwg_ref, cba_ref, cbg_ref,
             du_ref, dwa_ref, dwg_ref, dba_ref, dbg_ref,
             wacc_a, wacc_g, bacc_a, bacc_g):
        i = pl.program_id(1)
        first, last = i == 0, i == ni - 1
        sub = lax.broadcasted_iota(jnp.int32, (S8, tn), 0)
        for acc in (wacc_a, wacc_g, bacc_a, bacc_g):
            acc[...] = jnp.zeros_like(acc)

        def dconv(a_cur, a_prev, g_cur, g_prev, dact_rows):
            a, taps_a = _conv_group(a_cur, a_prev, cwa_ref, cba_ref, sub)
            gate, taps_g = _conv_group(g_cur, g_prev, cwg_ref, cbg_ref, sub)
            sg = jax.nn.sigmoid(a)
            dca = dact_rows * gate * (sg * (1.0 + a * (1.0 - sg)))
            dcg = dact_rows * (a * sg)
            return dca, dcg, taps_a, taps_g

        def du_group(dc, dc_after, cw_ref):
            return (cw_ref[2:3, :] * dc + cw_ref[1:2, :] * _shift_rows_up(dc, dc_after, 1, sub)
                    + cw_ref[0:1, :] * _shift_rows_up(dc, dc_after, 2, sub))

        def strip(r, a_prev, g_prev, dca_after, dcg_after):
            a0, a1 = ua_ref[pl.ds(r, S8), :], ua_ref[pl.ds(r + S8, S8), :]
            g0, g1 = ug_ref[pl.ds(r, S8), :], ug_ref[pl.ds(r + S8, S8), :]
            dca1, dcg1, ta1, tg1 = dconv(a1, a0, g1, g0, da_ref[pl.ds(r + S8, S8), :])
            dca0, dcg0, ta0, tg0 = dconv(a0, a_prev, g0, g_prev, da_ref[pl.ds(r, S8), :])
            du_ref[0, pl.ds(r, GLU_STRIP), :] = jnp.concatenate(
                [du_group(dca0, dca1, cwa_ref), du_group(dca1, dca_after, cwa_ref)], axis=0).astype(BF16)
            du_ref[1, pl.ds(r, GLU_STRIP), :] = jnp.concatenate(
                [du_group(dcg0, dcg1, cwg_ref), du_group(dcg1, dcg_after, cwg_ref)], axis=0).astype(BF16)
            for k in range(CONV_WIDTH):
                wacc_a[k] += dca0 * ta0[k] + dca1 * ta1[k]
                wacc_g[k] += dcg0 * tg0[k] + dcg1 * tg1[k]
            bacc_a[...] += dca0 + dca1
            bacc_g[...] += dcg0 + dcg1
            return dca0, dcg0

        tail = pl.ds(tm - S8, S8)
        dca_after, dcg_after, _, _ = dconv(na_ref[...], ua_ref[tail, :], ng_ref[...], ug_ref[tail, :],
                                           jnp.where(last, 0.0, dn_ref[...]))

        def step(t, carry):
            r = pl.multiple_of((n_strips - 1 - t) * GLU_STRIP, GLU_STRIP)
            before = pl.ds(pl.multiple_of(r - S8, S8), S8)
            return strip(r, ua_ref[before, :], ug_ref[before, :], *carry)

        dca_after, dcg_after = lax.fori_loop(0, n_strips - 1, step, (dca_after, dcg_after))
        strip(0, jnp.where(first, 0.0, pa_ref[...]), jnp.where(first, 0.0, pg_ref[...]), dca_after, dcg_after)

        @pl.when(first)
        def _():
            for r in (dwa_ref, dwg_ref, dba_ref, dbg_ref):
                r[...] = jnp.zeros_like(r)

        for wacc, bacc, dw_ref, db_ref in ((wacc_a, bacc_a, dwa_ref, dba_ref), (wacc_g, bacc_g, dwg_ref, dbg_ref)):
            db_ref[...] += jnp.sum(bacc[...], axis=0, keepdims=True)
            for k in range(CONV_WIDTH):
                dw_ref[k:k + 1, :] += jnp.sum(wacc[k], axis=0, keepdims=True)

    def tile(off):
        return pl.BlockSpec((tm, tn), lambda j, i: (i, j + off))

    def prev(off):
        return pl.BlockSpec((S8, tn), lambda j, i: (jnp.maximum(i * halo_blocks - 1, 0), j + off))

    def nxt(off):
        return pl.BlockSpec((S8, tn), lambda j, i: (jnp.minimum((i + 1) * halo_blocks, n_halo - 1), j + off))

    def cols(rows, off):
        return pl.BlockSpec((rows, tn), lambda j, i: (0, j + off))

    return pl.pallas_call(
        body, name=name, grid=(nj, ni),
        in_specs=[tile(0), tile(nj), prev(0), prev(nj), nxt(0), nxt(nj), tile(0), nxt(0),
                  cols(CONV_WIDTH, 0), cols(CONV_WIDTH, nj), cols(1, 0), cols(1, nj)],
        out_specs=[pl.BlockSpec((2, tm, tn), lambda j, i: (0, i, j)),
                   cols(CONV_WIDTH, 0), cols(CONV_WIDTH, 0), cols(1, 0), cols(1, 0)],
        out_shape=[jax.ShapeDtypeStruct((2, Lp, F), BF16),
                   jax.ShapeDtypeStruct((CONV_WIDTH, F), F32), jax.ShapeDtypeStruct((CONV_WIDTH, F), F32),
                   jax.ShapeDtypeStruct((1, F), F32), jax.ShapeDtypeStruct((1, F), F32)],
        scratch_shapes=[pltpu.VMEM((CONV_WIDTH, S8, tn), F32), pltpu.VMEM((CONV_WIDTH, S8, tn), F32),
                        pltpu.VMEM((S8, tn), F32), pltpu.VMEM((S8, tn), F32)],
        compiler_params=_cparams("parallel", "arbitrary"),
    )(u, u, u, u, u, u, dact, dact, cw, cw, cb, cb)


def _bias_routing(n_heads, width, first_slot):
    h = lax.broadcasted_iota(jnp.int32, (LANES, width), 0)
    col = lax.broadcasted_iota(jnp.int32, (LANES, width), 1)
    base = LANES * (h // 2) + HEAD_DIM * (1 - h % 2) + first_slot
    return [((col == base + t) & (h < n_heads)).astype(BF16) for t in range(3)]


def _three_terms(x):
    hi = x.astype(BF16)
    r1 = x - hi.astype(F32)
    lo = r1.astype(BF16)
    lo2 = (r1 - lo.astype(F32)).astype(BF16)
    return hi, lo, lo2


def _logf_cumsum(pre, bf, name, *, tm, n_heads, width):
    Lp, W = pre.shape

    def body(p_ref, b_ref, c_ref, kx_ref, qx_ref, carry_ref):
        i = pl.program_id(0)

        @pl.when(i == 0)
        def _():
            carry_ref[...] = jnp.zeros_like(carry_ref)

        x = p_ref[...] + b_ref[...]
        lf = jnp.minimum(x, 0.0) - jnp.log(1.0 + jnp.exp(-jnp.abs(x)))
        tri = (lax.broadcasted_iota(jnp.int32, (tm, tm), 0) >= lax.broadcasted_iota(jnp.int32, (tm, tm), 1)).astype(F32)
        c = jnp.dot(tri, lf, precision=lax.Precision.HIGHEST, preferred_element_type=F32) + carry_ref[...]
        c_ref[...] = c
        carry_ref[...] = c[tm - 1:tm, :]
        terms = _three_terms(c)
        slot = lax.broadcasted_iota(jnp.int32, (tm, width), 1) % HEAD_DIM
        ones_k = ((slot >= 3) & (slot < BIAS_SLOTS)).astype(F32)
        ones_q = (slot < 3).astype(F32)
        kx = sum(jnp.dot(t, r, preferred_element_type=F32) for t, r in zip(terms, _bias_routing(n_heads, width, 0)))
        qx = sum(jnp.dot(t, r, preferred_element_type=F32) for t, r in zip(terms, _bias_routing(n_heads, width, 3)))
        kx_ref[...] = (ones_k - kx).astype(BF16)
        qx_ref[...] = (ones_q + qx).astype(BF16)

    row = pl.BlockSpec((tm, W), lambda i: (i, 0))
    wide = pl.BlockSpec((tm, width), lambda i: (i, 0))
    return pl.pallas_call(
        body, name=name, grid=(Lp // tm,),
        in_specs=[row, pl.BlockSpec((1, W), lambda i: (0, 0))], out_specs=[row, wide, wide],
        out_shape=[jax.ShapeDtypeStruct((Lp, W), F32), jax.ShapeDtypeStruct((Lp, width), BF16),
                   jax.ShapeDtypeStruct((Lp, width), BF16)],
        scratch_shapes=[pltpu.VMEM((1, W), F32)],
        compiler_params=_cparams("arbitrary"),
    )(pre, bf)


def _logf_bwd(dc_a, dc_b, pre, bf, name, *, tm):
    Lp, W = pre.shape
    ni = Lp // tm

    def body(dca_ref, dcb_ref, p_ref, b_ref, dpb_ref, db_ref, carry_ref):
        i = pl.program_id(0)

        @pl.when(i == 0)
        def _():
            carry_ref[...] = jnp.zeros_like(carry_ref)
            db_ref[...] = jnp.zeros_like(db_ref)

        triu = (lax.broadcasted_iota(jnp.int32, (tm, tm), 0) <= lax.broadcasted_iota(jnp.int32, (tm, tm), 1)).astype(F32)
        dl = jnp.dot(triu, dca_ref[...] + dcb_ref[...], precision=lax.Precision.HIGHEST,
                     preferred_element_type=F32) + carry_ref[...]
        carry_ref[...] = dl[0:1, :]
        dp = dl * jax.nn.sigmoid(-(p_ref[...] + b_ref[...]))
        dpb_ref[...] = dp.astype(BF16)
        db_ref[...] += jnp.sum(dp, axis=0, keepdims=True)

    rev = pl.BlockSpec((tm, W), lambda i: (ni - 1 - i, 0))
    vec = pl.BlockSpec((1, W), lambda i: (0, 0))
    return pl.pallas_call(
        body, name=name, grid=(ni,),
        in_specs=[rev, rev, rev, vec], out_specs=[rev, vec],
        out_shape=[jax.ShapeDtypeStruct((Lp, W), BF16), jax.ShapeDtypeStruct((1, W), F32)],
        scratch_shapes=[pltpu.VMEM((1, W), F32)],
        compiler_params=_cparams("arbitrary"),
    )(dc_a, dc_b, pre, bf)


def _attn_fwd(qb, kvb, kx, qx, name, *, tq):
    Lp, D = qb.shape
    H = D // HEAD_DIM
    nq = Lp // tq
    S8 = SUBLANES
    assert LANES // HEAD_DIM == 2
    HB = 2 * ATTN_FWD_PAIRS
    W = LANES * ATTN_FWD_PAIRS
    n_scratch = 5

    def body(q_ref, qx_ref, k_ref, kx_ref, v_ref, o_ref, ob_ref, lse_ref, vt_ref, *scratch):
        i = pl.program_id(1)
        heads = [scratch[n_scratch * hb:n_scratch * (hb + 1)] for hb in range(HB)]
        lane = lax.broadcasted_iota(jnp.int32, (tq, LANES), 1)

        def own_lanes(hb, x2, extra2):
            return jnp.where((lane < HEAD_DIM) == (hb % 2 == 0), x2, extra2)

        q_of = [own_lanes(hb, q_ref[:, pl.ds(LANES * (hb // 2), LANES)], qx_ref[:, pl.ds(LANES * (hb // 2), LANES)])
                for hb in range(HB)]

        @pl.when(i == 0)
        def _():
            for j in range(nq):
                vt_ref[j] = jnp.transpose(v_ref[pl.ds(j * tq, tq), :].astype(F32)).astype(BF16)

        for m_ref, l_ref, acc_ref, _, _ in heads:
            m_ref[...] = jnp.full_like(m_ref, NEG_INF)
            l_ref[...] = jnp.zeros_like(l_ref)
            acc_ref[...] = jnp.zeros_like(acc_ref)

        def chunk(j, masked):
            keys = pl.ds(pl.multiple_of(j * tq, tq), tq)
            for hb, (_, _, _, st_ref, _) in enumerate(heads):
                pair = pl.ds(LANES * (hb // 2), LANES)
                k_own = own_lanes(hb, k_ref[keys, pair], kx_ref[keys, pair])
                st_ref[...] = lax.dot_general(k_own, q_of[hb], NT, preferred_element_type=F32)
            for hb, (m_ref, l_ref, acc_ref, st_ref, pt_ref) in enumerate(heads):
                mx = jnp.full((S8, tq), NEG_INF, F32)
                for r0 in range(0, tq, ATTN_STRIP):
                    rows = pl.ds(r0, ATTN_STRIP)
                    st = st_ref[rows, :]
                    if masked:
                        keep = (lax.broadcasted_iota(jnp.int32, (ATTN_STRIP, tq), 1)
                                >= r0 + lax.broadcasted_iota(jnp.int32, (ATTN_STRIP, tq), 0))
                        st = jnp.where(keep, st, NEG_INF)
                        st_ref[rows, :] = st
                    for g0 in range(0, ATTN_STRIP, S8):
                        mx = jnp.maximum(mx, st[g0:g0 + S8])
                m_prev = m_ref[...]
                m_new = jnp.maximum(m_prev, jnp.max(mx, axis=0, keepdims=True))
                alpha = jnp.exp(m_prev - m_new)
                m_ref[...] = m_new
                ls = jnp.zeros((S8, tq), F32)
                for r0 in range(0, tq, ATTN_STRIP):
                    pieces = [jnp.exp(st_ref[pl.ds(r0 + g0, S8), :] - m_new) for g0 in range(0, ATTN_STRIP, S8)]
                    for piece in pieces:
                        ls = ls + piece
                    pt_ref[pl.ds(r0, ATTN_STRIP), :] = jnp.concatenate(pieces, axis=0).astype(BF16)
                l_ref[...] = alpha * l_ref[...] + ls
                pv = jnp.dot(vt_ref[j, pl.ds(LANES * (hb // 2), LANES), :], pt_ref[...], preferred_element_type=F32)
                acc_ref[...] = jnp.concatenate([alpha] * (LANES // S8), axis=0) * acc_ref[...] + pv

        def step(j, carry):
            chunk(j, False)
            return carry

        lax.fori_loop(0, i, step, 0)
        chunk(i, True)
        outs = []
        for hb, (m_ref, l_ref, acc_ref, _, _) in enumerate(heads):
            l_row = jnp.sum(l_ref[...], axis=0, keepdims=True)
            outs.append(acc_ref[...] / l_row)
            lse_ref[hb, 0] = m_ref[0:1, :] + jnp.log(l_row)
        first_rows = lax.broadcasted_iota(jnp.int32, (LANES, tq), 0) < HEAD_DIM
        for pp in range(ATTN_FWD_PAIRS):
            o2 = jnp.transpose(jnp.where(first_rows, outs[2 * pp], outs[2 * pp + 1]))
            o_ref[:, pl.ds(LANES * pp, LANES)] = o2
            ob_ref[:, pl.ds(LANES * pp, LANES)] = o2.astype(BF16)

    per_head = [pltpu.VMEM((S8, tq), F32), pltpu.VMEM((S8, tq), F32), pltpu.VMEM((LANES, tq), F32),
                pltpu.VMEM((tq, tq), F32), pltpu.VMEM((tq, tq), BF16)]
    assert len(per_head) == n_scratch and H % HB == 0
    v_blocks = D // W
    tile = pl.BlockSpec((tq, W), lambda p, i: (i, p))
    once = pl.Buffered(1)
    whole = pl.BlockSpec((Lp, W), lambda p, i: (0, p), pipeline_mode=once)
    return pl.pallas_call(
        body, name=name, grid=(H // HB, nq),
        in_specs=[tile, tile, whole, whole, pl.BlockSpec((Lp, W), lambda p, i: (0, v_blocks + p), pipeline_mode=once)],
        out_specs=[tile, tile, pl.BlockSpec((HB, 1, 1, tq), lambda p, i: (p, i, 0, 0))],
        out_shape=[jax.ShapeDtypeStruct((Lp, D), F32), jax.ShapeDtypeStruct((Lp, D), BF16),
                   jax.ShapeDtypeStruct((H, nq, 1, tq), F32)],
        scratch_shapes=[pltpu.VMEM((nq, W, tq), BF16)] + per_head * HB,
        compiler_params=_cparams("parallel", "arbitrary"),
    )(qb, qx, kvb, kx, kvb)


def _attn_bwd(qb, dob, kvb, lse4, delta4, crow4, name, *, tq):
    Lp, D = qb.shape
    H = D // HEAD_DIM
    nq = Lp // tq
    HB = LANES // HEAD_DIM
    lane_tiles = tq // LANES
    n_scratch = 8
    assert HB == 2

    def body(q_ref, do_ref, k_ref, v_ref, lse_ref, dl_ref, c_ref,
             dqb_ref, dk_ref, dv_ref, dcs_ref, dcq_ref, dqt_ref, kt_ref, *scratch):
        j = pl.program_id(1)
        heads = [scratch[n_scratch * hb:n_scratch * (hb + 1)] for hb in range(HB)]
        first_head = lax.broadcasted_iota(jnp.int32, (tq, LANES), 1) < HEAD_DIM

        def split(x2):
            zero = jnp.zeros_like(x2)
            return [jnp.where(first_head, x2, zero), jnp.where(first_head, zero, x2)]

        @pl.when(j == 0)
        def _():
            dqt_ref[...] = jnp.zeros_like(dqt_ref)
            dcq_ref[...] = jnp.zeros_like(dcq_ref)

        k2 = k_ref[...]
        v2 = v_ref[...]
        kt_ref[...] = jnp.transpose(k2.astype(F32)).astype(BF16)
        first_rows = lax.broadcasted_iota(jnp.int32, (LANES, tq), 0) < HEAD_DIM
        for hb, (dk_acc, dv_acc, dc_acc, _, _, _, _, cs_ref) in enumerate(heads):
            dk_acc[...] = jnp.zeros_like(dk_acc)
            dv_acc[...] = jnp.zeros_like(dv_acc)
            dc_acc[...] = jnp.zeros_like(dc_acc)
            cs_ref[...] = jnp.transpose(jnp.broadcast_to(c_ref[hb, j], (LANES, tq)))

        def pair(i, masked):
            queries = pl.ds(pl.multiple_of(i * tq, tq), tq)
            q2 = q_ref[queries, :]
            do2 = do_ref[queries, :]
            q_of, do_of = split(q2), split(do2)
            for hb, (_, _, _, st_ref, dp_ref, _, _, _) in enumerate(heads):
                st_ref[...] = lax.dot_general(k2, q_of[hb], NT, preferred_element_type=F32)
                dp_ref[...] = lax.dot_general(v2, do_of[hb], NT, preferred_element_type=F32)
            dq_parts = []
            for hb, (dk_acc, dv_acc, dc_acc, st_ref, dp_ref, pt_ref, ds_ref, cs_ref) in enumerate(heads):
                bias_q = c_ref[hb, i] - lse_ref[hb, i]
                delta = dl_ref[hb, i]
                col_sum = jnp.zeros((SUBLANES, tq), F32)
                for r0 in range(0, tq, ATTN_STRIP):
                    rows = pl.ds(r0, ATTN_STRIP)
                    st = st_ref[rows, :] + (bias_q - jnp.concatenate([cs_ref[rows, :]] * lane_tiles, axis=1))
                    if masked:
                        keep = (lax.broadcasted_iota(jnp.int32, (ATTN_STRIP, tq), 1)
                                >= r0 + lax.broadcasted_iota(jnp.int32, (ATTN_STRIP, tq), 0))
                        st = jnp.where(keep, st, NEG_INF)
                    pt = jnp.exp(st)
                    dst = pt * (dp_ref[rows, :] - delta)
                    pt_ref[rows, :] = pt.astype(BF16)
                    ds_ref[rows, :] = dst.astype(BF16)
                    dc_acc[rows, :] += jnp.sum(dst, axis=1, keepdims=True)
                    for g0 in range(0, ATTN_STRIP, SUBLANES):
                        col_sum = col_sum + dst[g0:g0 + SUBLANES]
                dcq_ref[hb, i] += jnp.sum(col_sum, axis=0, keepdims=True)
                dv_acc[...] += jnp.dot(pt_ref[...], do2, preferred_element_type=F32)
                dk_acc[...] += jnp.dot(ds_ref[...], q2, preferred_element_type=F32)
                dq_parts.append(jnp.dot(kt_ref[...], ds_ref[...], preferred_element_type=F32))
            dqt_ref[i] += jnp.where(first_rows, dq_parts[0], dq_parts[1])

        def step(i, carry):
            pair(i, False)
            return carry

        pair(j, True)
        lax.fori_loop(j + 1, nq, step, 0)
        dk_ref[...] = jnp.where(first_head, heads[0][0][...], heads[1][0][...]).astype(BF16)
        dv_ref[...] = jnp.where(first_head, heads[0][1][...], heads[1][1][...]).astype(BF16)
        for hb in range(HB):
            dcs_ref[hb, 0] = -jnp.transpose(jnp.broadcast_to(heads[hb][2][...], (tq, LANES)))[0:1, :]

        @pl.when(j == nq - 1)
        def _():
            for i in range(nq):
                dqb_ref[pl.ds(i * tq, tq), :] = jnp.transpose(dqt_ref[i]).astype(BF16)

    per_head = [pltpu.VMEM((tq, LANES), F32), pltpu.VMEM((tq, LANES), F32), pltpu.VMEM((tq, 1), F32),
                pltpu.VMEM((tq, tq), F32), pltpu.VMEM((tq, tq), F32),
                pltpu.VMEM((tq, tq), BF16), pltpu.VMEM((tq, tq), BF16), pltpu.VMEM((tq, LANES), F32)]
    assert len(per_head) == n_scratch
    v_blocks = D // LANES
    whole = pl.BlockSpec((Lp, LANES), lambda p, j: (0, p))
    tile = pl.BlockSpec((tq, LANES), lambda p, j: (j, p))
    rows = pl.BlockSpec((HB, nq, 1, tq), lambda p, j: (p, 0, 0, 0))
    return pl.pallas_call(
        body, name=name, grid=(H // HB, nq),
        in_specs=[whole, whole, tile, pl.BlockSpec((tq, LANES), lambda p, j: (j, v_blocks + p)), rows, rows, rows],
        out_specs=[whole, tile, tile, pl.BlockSpec((HB, 1, 1, tq), lambda p, j: (p, j, 0, 0)), rows],
        out_shape=[jax.ShapeDtypeStruct((Lp, D), BF16), jax.ShapeDtypeStruct((Lp, D), BF16),
                   jax.ShapeDtypeStruct((Lp, D), BF16), jax.ShapeDtypeStruct((H, nq, 1, tq), F32),
                   jax.ShapeDtypeStruct((H, nq, 1, tq), F32)],
        scratch_shapes=[pltpu.VMEM((nq, LANES, tq), F32), pltpu.VMEM((LANES, tq), BF16)] + per_head * HB,
        compiler_params=_cparams("parallel", "arbitrary"),
    )(qb, dob, kvb, kvb, lse4, delta4, crow4)


def _remote(src, dst, send_sems, recv_sems, k, to):
    return pltpu.make_async_remote_copy(src_ref=src, dst_ref=dst, send_sem=send_sems.at[k], recv_sem=recv_sems.at[k],
                                        device_id=to, device_id_type=MESH)


def _place():
    x, y, c = lax.axis_index("x"), lax.axis_index("y"), lax.axis_index("c")
    other_chips = [(1 - x, y), (x, 1 - y), (1 - x, 1 - y)]
    return x, y, c, other_chips


def _all_gather_weights(wb, wf, name):
    Rb, C = wb.shape
    Rf = wf.shape[0]
    hb = Rb // 2

    def body(wb_ref, wf_ref, ob_ref, of_ref, send_sems, recv_sems):
        x, y, c, chips = _place()
        me = 2 * x + y
        sibling = (x, y, 1 - c)

        def half(chip, core):
            return ob_ref.at[chip, pl.ds(core * hb, hb), :]

        sent = []
        for j, (cx, cy) in enumerate(chips):
            sent.append(_remote(wb_ref.at[pl.ds(c * hb, hb), :], half(me, c), send_sems, recv_sems, j, (cx, cy, c)))
            sent.append(_remote(wf_ref, of_ref.at[me], send_sems, recv_sems, 3 + j, (cx, cy, c)))
        for cp in sent:
            cp.start()
        for j, (cx, cy) in enumerate(chips):
            chip = 2 * cx + cy
            _remote(half(chip, c), half(chip, c), send_sems, recv_sems, j, sibling).wait_recv()
            fwd = _remote(half(chip, c), half(chip, c), send_sems, recv_sems, 6 + j, sibling)
            fwd.start()
            sent.append(fwd)
        for j, (cx, cy) in enumerate(chips):
            chip = 2 * cx + cy
            _remote(wf_ref, of_ref.at[chip], send_sems, recv_sems, 3 + j, sibling).wait_recv()
            _remote(half(chip, 1 - c), half(chip, 1 - c), send_sems, recv_sems, 6 + j, sibling).wait_recv()
        for cp in sent:
            cp.wait_send()

    any_spec = pl.BlockSpec(memory_space=pl.ANY)
    return pl.pallas_call(
        body, name=name,
        in_specs=[any_spec, any_spec], out_specs=[any_spec, any_spec],
        out_shape=[jax.ShapeDtypeStruct((N_CHIPS, Rb, C), BF16), jax.ShapeDtypeStruct((N_CHIPS, Rf, C), F32)],
        scratch_shapes=[pltpu.SemaphoreType.DMA((9,)), pltpu.SemaphoreType.DMA((9,))],
    )(wb, wf)


def _half_of(ref, order, half):
    return ref.at[pl.ds(0, N_CHIPS), half] if order == "CH" else ref.at[half]


def _halves_to_sibling(grads, orders, name):
    n = len(grads)

    def body(*refs):
        g_refs, a_refs, (send_sems, recv_sems) = refs[:n], refs[n:2 * n], refs[2 * n:]
        x, y, c, _ = _place()
        copies = [_remote(_half_of(g, o, 1 - c), a, send_sems, recv_sems, k, (x, y, 1 - c))
                  for k, (g, a, o) in enumerate(zip(g_refs, a_refs, orders))]
        for cp in copies:
            cp.start()
        for cp in copies:
            cp.wait()

    any_spec = pl.BlockSpec(memory_space=pl.ANY)
    shapes = [g.shape[2:] for g in grads]
    return pl.pallas_call(
        body, name=name, in_specs=[any_spec] * n, out_specs=[any_spec] * n,
        out_shape=[jax.ShapeDtypeStruct((N_CHIPS,) + s, F32) for s in shapes],
        scratch_shapes=[pltpu.SemaphoreType.DMA((n,)), pltpu.SemaphoreType.DMA((n,))],
    )(*grads)


def _chip_partial(g, a, core, order, wire, name, *, tr):
    _, R, C = a.shape
    narrow = wire != F32

    def body(core_ref, g_ref, a_ref, *outs):
        p = g_ref[0, 0] + a_ref[0]
        outs[0][0] = p
        if narrow:
            outs[1][0] = p.astype(wire)

    if order == "CH":
        g_spec = pl.BlockSpec((1, 1, tr, C), lambda s, i, core_ref: (s, core_ref[0], i, 0))
    else:
        g_spec = pl.BlockSpec((1, 1, tr, C), lambda s, i, core_ref: (core_ref[0], s, i, 0))
    blk = pl.BlockSpec((1, tr, C), lambda s, i, core_ref: (s, i, 0))
    grid_spec = pltpu.PrefetchScalarGridSpec(
        num_scalar_prefetch=1, grid=(N_CHIPS, R // tr), in_specs=[g_spec, blk],
        out_specs=[blk, blk] if narrow else [blk])
    out_shape = [jax.ShapeDtypeStruct((N_CHIPS, R, C), F32)] + ([jax.ShapeDtypeStruct((N_CHIPS, R, C), wire)] if narrow else [])
    outs = pl.pallas_call(body, name=name, grid_spec=grid_spec, out_shape=out_shape,
                          compiler_params=_cparams("parallel", "parallel"))(core, g, a)
    return outs[0], outs[-1]


def _chip_exchange(parts, rep, name):
    n = len(parts)
    rr, C = rep.shape

    def body(*refs):
        p_refs, rep_ref = refs[:n], refs[n]
        land_refs, reps_ref = refs[n + 1:2 * n + 1], refs[2 * n + 1]
        send_sems, recv_sems, local_sem = refs[2 * n + 2:]
        x, y, c, chips = _place()
        me = 4 * x + 2 * y + c
        own = pltpu.make_async_copy(rep_ref, reps_ref.at[me], local_sem.at[0])
        own.start()
        sent = []
        for k, (p, land) in enumerate(zip(p_refs, land_refs)):
            for j, (cx, cy) in enumerate(chips):
                sent.append(_remote(p.at[2 * cx + cy], land.at[j], send_sems, recv_sems, 3 * k + j, (cx, cy, c)))
        for r in range(1, N_DEV):
            fx, fy, fc = (r >> 2) & 1, (r >> 1) & 1, r & 1
            sent.append(_remote(rep_ref, reps_ref.at[me], send_sems, recv_sems, 3 * n - 1 + r, (x ^ fx, y ^ fy, c ^ fc)))
        for cp in sent:
            cp.start()
        for k, (p, land) in enumerate(zip(p_refs, land_refs)):
            for j in range(3):
                _remote(p.at[0], land.at[j], send_sems, recv_sems, 3 * k + j, (x, y, c)).wait_recv()
        for r in range(1, N_DEV):
            fx, fy, fc = (r >> 2) & 1, (r >> 1) & 1, r & 1
            frm = 4 * (x ^ fx) + 2 * (y ^ fy) + (c ^ fc)
            _remote(rep_ref, reps_ref.at[frm], send_sems, recv_sems, 3 * n - 1 + r, (x, y, c)).wait_recv()
        for cp in sent:
            cp.wait_send()
        own.wait()

    any_spec = pl.BlockSpec(memory_space=pl.ANY)
    n_sems = 3 * n + N_DEV - 1
    return pl.pallas_call(
        body, name=name, in_specs=[any_spec] * (n + 1), out_specs=[any_spec] * (n + 1),
        out_shape=[jax.ShapeDtypeStruct((3,) + p.shape[1:], p.dtype) for p in parts]
        + [jax.ShapeDtypeStruct((N_DEV, rr, C), F32)],
        scratch_shapes=[pltpu.SemaphoreType.DMA((n_sems,)), pltpu.SemaphoreType.DMA((n_sems,)),
                        pltpu.SemaphoreType.DMA((1,))],
    )(*parts, rep)


def _adamw_math(w, g, m, v):
    m = ADAM_B1 * m + (1.0 - ADAM_B1) * g
    v = ADAM_B2 * v + (1.0 - ADAM_B2) * (g * g)
    m_hat = m / (1.0 - ADAM_B1 ** ADAM_STEP)
    v_hat = v / (1.0 - ADAM_B2 ** ADAM_STEP)
    delta = -ADAM_LR * (m_hat / (jnp.sqrt(v_hat) + ADAM_EPS) + ADAM_WD * w)
    return delta, m, v


def _adamw_owned(part, landed, w, m, v, place, name, *, tr):
    _, R, C = part.shape

    def body(place_ref, own_ref, land_ref, w_ref, m_ref, v_ref, g_ref, d_ref, mo_ref, vo_ref):
        g = own_ref[0]
        for s in range(3):
            g = g + land_ref[s].astype(F32)
        delta, m_new, v_new = _adamw_math(w_ref[0], g, m_ref[0], v_ref[0])
        g_ref[0] = g
        d_ref[0] = delta
        mo_ref[0] = m_new
        vo_ref[0] = v_new

    half = pl.BlockSpec((1, tr, C), lambda i, place_ref: (place_ref[0], i, 0))
    grid_spec = pltpu.PrefetchScalarGridSpec(
        num_scalar_prefetch=1, grid=(R // tr,),
        in_specs=[pl.BlockSpec((1, tr, C), lambda i, place_ref: (place_ref[1], i, 0)),
                  pl.BlockSpec((3, tr, C), lambda i, place_ref: (0, i, 0)), half, half, half],
        out_specs=[half] * 4)
    return pl.pallas_call(
        body, name=name, grid_spec=grid_spec, out_shape=[jax.ShapeDtypeStruct((2, R, C), F32)] * 4,
        compiler_params=_cparams("parallel"),
    )(place, part, landed, w, m, v)


def _join_halves(bufs, name):
    n = len(bufs)

    def body(*refs):
        out_refs, (send_sems, recv_sems) = refs[n:2 * n], refs[2 * n:]
        x, y, c, _ = _place()
        copies = [_remote(o.at[c], o.at[c], send_sems, recv_sems, k, (x, y, 1 - c)) for k, o in enumerate(out_refs)]
        for cp in copies:
            cp.start()
        for k, o in enumerate(out_refs):
            _remote(o.at[c], o.at[1 - c], send_sems, recv_sems, k, (x, y, 1 - c)).wait_recv()
        for cp in copies:
            cp.wait_send()

    any_spec = pl.BlockSpec(memory_space=pl.ANY)
    return pl.pallas_call(
        body, name=name, in_specs=[any_spec] * n, out_specs=[any_spec] * n,
        out_shape=[jax.ShapeDtypeStruct(b.shape, b.dtype) for b in bufs],
        input_output_aliases={k: k for k in range(n)},
        scratch_shapes=[pltpu.SemaphoreType.DMA((n,)), pltpu.SemaphoreType.DMA((n,))],
    )(*bufs)


def _sum_adamw(own, landed, w, m, v, name, *, tr):
    n = landed.shape[0]
    hr, C = own.shape

    def body(own_ref, land_ref, w_ref, m_ref, v_ref, o_ref):
        g = own_ref[...]
        for s in range(n):
            g = g + land_ref[s]
        delta, m_new, v_new = _adamw_math(w_ref[...], g, m_ref[...], v_ref[...])
        o_ref[0] = g
        o_ref[1] = delta
        o_ref[2] = m_new
        o_ref[3] = v_new

    blk = pl.BlockSpec((tr, C), lambda i: (i, 0))
    return pl.pallas_call(
        body, name=name, grid=(hr // tr,),
        in_specs=[blk, pl.BlockSpec((n, tr, C), lambda i: (0, i, 0)), blk, blk, blk],
        out_specs=pl.BlockSpec((4, tr, C), lambda i: (0, i, 0)),
        out_shape=jax.ShapeDtypeStruct((4, hr, C), F32), compiler_params=_cparams("parallel"),
    )(own, landed, w, m, v)


def _rows_of(shape):
    n = 1
    for d in shape:
        n *= d
    return -(-n // PACK_COLS)


def _pack(arrays, total_rows, dtype):
    parts, used = [], 0
    for a in arrays:
        flat = a.reshape(-1).astype(dtype)
        fill = _rows_of(a.shape) * PACK_COLS - flat.shape[0]
        parts += [flat] + ([jnp.zeros((fill,), dtype)] if fill else [])
        used += _rows_of(a.shape)
    if total_rows > used:
        parts.append(jnp.zeros(((total_rows - used) * PACK_COLS,), dtype))
    return jnp.concatenate(parts).reshape(total_rows, PACK_COLS)


def _unpack(buf, shapes):
    lead = buf.shape[:-2]
    out, r = [], 0
    for shp in shapes:
        n = 1
        for d in shp:
            n *= d
        rows = _rows_of(shp)
        piece = buf[..., r:r + rows, :].reshape(lead + (rows * PACK_COLS,))[..., :n]
        out.append(piece.reshape(lead + tuple(shp)))
        r += rows
    return out


def _join_shards(stacked, axis):
    return jnp.concatenate([stacked[s] for s in range(N_CHIPS)], axis=axis)


def _shard_of(full, axis, chip):
    width = full.shape[axis] // N_CHIPS
    return lax.slice_in_dim(full, chip * width, (chip + 1) * width, axis=axis)


def _local_step(h0, tgt, W, *, seq, tm):
    Lp, D = h0.shape
    H = D // HEAD_DIM
    F2 = W["ffn_w_in"].shape[-1]
    F = F2 // 2
    te = tm // 2
    nq = Lp // tm
    cap = 1408
    tD, t2D = _pick(D, cap), _pick(2 * D, cap)
    tF, tF2 = _pick(F, 2 * cap), _pick(F2, 2 * cap)
    t2Dc, tF2c = _pick(2 * D // N_CHIPS, cap), _pick(F2 // N_CHIPS, cap)
    tFm = tcn = _pick(F, cap)
    tkL = max(t for t in range(tm, 2048 + 1, tm) if Lp % t == 0) if Lp > 2048 else Lp

    def vec(a):
        return a.reshape(1, -1)

    ln_g, ln_b = W["ln_g"], W["ln_b"]
    wf_pad = jnp.pad(W["w_f"], ((0, 0), (0, LANES - H)))
    bf_pad = jnp.pad(W["b_f"], (0, LANES - H)).reshape(1, LANES)

    def ffn_fwd(h, hb, l, tag):
        u = _mm(hb, W["ffn_w_in"][l], "nn", F32, f"ffn{tag}_up", tm=tm, tn=tF2, tk=tD)
        act = _conv_glu_fwd(u, W["ffn_conv_w"][l], vec(W["ffn_conv_b"][l]), f"ffn{tag}_glu", tm=te, tn=tcn)
        normed = _mm_ln(act, W["ffn_w_out"][l], "nn", f"ffn{tag}_down_ln", tm=tm, tk=tF, forward=True,
                        rows=[h], vecs=[vec(ln_g[l, 1]), vec(ln_b[l, 1])])
        return u, act, normed

    def ffn_bwd(dz, dzb, hb, u, act, l, tag, dw_in_acc, dw_out_acc, xh_in, rs_in, g_in):
        dact = _mm(dzb, W["ffn_w_out"][l], "nt", F32, f"ffn{tag}_dact", tm=tm, tn=tF, tk=tD)
        dw_out = _mm(act, dzb, "tn", F32, f"ffn{tag}_dwout", tm=tFm, tn=tD, tk=tkL, layer=l, into=dw_out_acc)
        du, dwa, dwg, dba, dbg = _conv_glu_bwd(u, dact, W["ffn_conv_w"][l], vec(W["ffn_conv_b"][l]),
                                               f"ffn{tag}_dglu", tm=te, tn=tcn)
        dcw = jnp.concatenate([dwa, dwg], axis=1)
        dcb = jnp.concatenate([dba, dbg], axis=1)
        prev = _mm_ln(du, W["ffn_w_in"][l], "nt", f"ffn{tag}_dh_ln", tm=tm, tk=tFm, forward=False,
                      rows=[dz, xh_in, rs_in], vecs=[g_in], a_halves=True)
        dw_in = _mm(hb, du, "tn", F32, f"ffn{tag}_dwin", tm=tD, tn=tF2c, tk=tkL, chips=True, layer=l, into=dw_in_acc,
                    b_halves=True)
        return prev, dw_in, dw_out, dcw, dcb[0]

    diffb, mixpre, h1, h1b, xh1, rs1 = _pool_ln_fwd(h0, W["pool_w"][0], W["pool_scale"], vec(ln_g[0, 0]),
                                                    vec(ln_b[0, 0]), "pool_ln_fwd", tm=te)
    u0, act0, (h2, h2b, xh2, rs2) = ffn_fwd(h1, h1b, 0, "0")

    kvb = _mm(h2b, W["w_kv"], "nn", BF16, "kv_proj", tm=tm, tn=t2D, tk=tD)
    qb = _mm(h2b, W["w_q"][0], "nn", BF16, "q_proj", tm=tm, tn=tD, tk=tD, scale=HEAD_DIM ** -0.5)
    pre = _mm(h2b, wf_pad, "nn", F32, "f_proj", tm=tm, tn=LANES, tk=tD)
    c, kx, qx = _logf_cumsum(pre, bf_pad, "logf_cumsum", tm=tm, n_heads=H, width=D)

    crow4 = c[:, :H].T.reshape(H, nq, 1, tm)
    o_tok, ob, lse4 = _attn_fwd(qb, kvb, kx, qx, "attn_fwd", tq=tm)
    h3, h3b, xh3, rs3 = _mm_ln(ob, W["w_o"][0], "nn", "o_proj_ln", tm=tm, tk=tD, forward=True,
                               rows=[h2], vecs=[vec(ln_g[1, 0]), vec(ln_b[1, 0])])
    u1, act1, (h4, _, xh4, rs4) = ffn_fwd(h3, h3b, 1, "1")

    dz4, dz4b, dg11, db11, loss = _loss_ln_bwd(h4, tgt, xh4, rs4, vec(ln_g[1, 1]), "loss_ln11_bwd", tm=te,
                                               row_lo=N_META, row_hi=N_META + seq)
    (dz3, dz3b, dg10, db10), dw_in, dw_out, dcw1, dcb1 = ffn_bwd(dz4, dz4b, h3b, u1, act1, 1, "1", None, None,
                                                                 xh3, rs3, vec(ln_g[1, 0]))

    dob, delta = _mm(dz3b, W["w_o"][0], "nt", BF16, "o_proj_dx", tm=tm, tn=D, tk=tD, head_dots_with=o_tok)
    dw_o = _mm(ob, dz3b, "tn", F32, "o_proj_dw", tm=tD, tn=tD, tk=tkL)
    dqb, dkb, dvb, dcs, dcq = _attn_bwd(qb, dob, kvb, lse4, delta[:, :H].T.reshape(H, nq, 1, tm), crow4,
                                        "attn_bwd", tq=tm)
    dc_keys = jnp.pad(dcs.reshape(H, Lp).T, ((0, 0), (0, LANES - H)))
    dc_queries = jnp.pad(dcq.reshape(H, Lp).T, ((0, 0), (0, LANES - H)))
    dpreb, dbf = _logf_bwd(dc_keys, dc_queries, pre, bf_pad, "logf_bwd", tm=tm)

    qs = HEAD_DIM ** -0.5
    dw_q = _mm(h2b, dqb, "tn", F32, "q_proj_dw", tm=tD, tn=tD, tk=tkL, scale=qs)
    dw_kv = _mm(h2b, dkb, "tn", F32, "k_proj_dw", tm=tD, tn=t2Dc, tk=tkL, chips=(0, N_CHIPS // 2))
    dw_kv = _mm(h2b, dvb, "tn", F32, "v_proj_dw", tm=tD, tn=t2Dc, tk=tkL, chips=(N_CHIPS // 2, N_CHIPS // 2), into=dw_kv)
    dw_f = _mm(h2b, dpreb, "tn", F32, "f_proj_dw", tm=tD, tn=LANES, tk=tkL)[:, :H]
    dh2 = _mm(dqb, W["w_q"][0], "nt", F32, "q_proj_dx", tm=tm, tn=tD, tk=tD, scale=qs)
    dh2 = _mm(dkb, W["w_kv"][:, :D], "nt", F32, "k_proj_dx", tm=tm, tn=tD, tk=tD, add=dh2)
    dh2 = _mm(dvb, W["w_kv"][:, D:], "nt", F32, "v_proj_dx", tm=tm, tn=tD, tk=tD, add=dh2)
    dz2, dz2b, dg01, db01 = _mm_ln(dpreb, wf_pad, "nt", "f_proj_dx_ln", tm=tm, tk=LANES, forward=False,
                                   rows=[dz3, xh2, rs2], vecs=[vec(ln_g[0, 1])], add=dh2)

    (dz1, _, dg00, db00), dw_in, dw_out, dcw0, dcb0 = ffn_bwd(dz2, dz2b, h1b, u0, act0, 0, "0", dw_in, dw_out,
                                                              xh1, rs1, vec(ln_g[0, 0]))
    dh0, dmb, dscale = _pool_bwd(dz1, mixpre, W["pool_w"][0], W["pool_scale"], "pool_bwd", tm=te)
    dw_pool = _pool_dw(diffb, dmb, "pool_dw", tk=tm)

    grads = {
        "meta": dh0[:N_META],
        "pool_w": dw_pool[None],
        "pool_scale": dscale,
        "w_kv": dw_kv,
        "w_f": dw_f,
        "b_f": dbf[0, :H],
        "w_q": dw_q[None],
        "w_o": dw_o[None],
        "ffn_w_in": dw_in,
        "ffn_conv_w": jnp.stack([dcw0, dcw1]),
        "ffn_conv_b": jnp.stack([dcb0, dcb1]),
        "ffn_w_out": dw_out,
        "ln_g": jnp.stack([jnp.stack([dg00[0], dg01[0]]), jnp.stack([dg10[0], dg11[0]])]),
        "ln_b": jnp.stack([jnp.stack([db00[0], db01[0]]), jnp.stack([db10[0], db11[0]])]),
    }
    return loss, dh0, grads


def _row_block(rows, cols):
    best = SUBLANES
    for t in range(SUBLANES, rows + 1, SUBLANES):
        if rows % t == 0 and t * cols * 4 <= ELEMENTWISE_BLOCK_BYTES:
            best = t
    return best


def _row_tile(length):
    return 768 if length >= 4096 else 128


def kernel(x, meta, pool_w, pool_scale, w_kv, w_f, b_f, w_q, w_o, ffn_w_in, ffn_conv_w, ffn_conv_b, ffn_w_out, ln_g, ln_b, loss_target, m_meta, m_pool_w, m_pool_scale, m_w_kv, m_w_f, m_b_f, m_w_q, m_w_o, m_ffn_w_in, m_ffn_conv_w, m_ffn_conv_b, m_ffn_w_out, m_ln_g, m_ln_b, v_meta, v_pool_w, v_pool_scale, v_w_kv, v_w_f, v_b_f, v_w_q, v_w_o, v_ffn_w_in, v_ffn_conv_w, v_ffn_conv_b, v_ffn_w_out, v_ln_g, v_ln_b):
    weights = dict(meta=meta, pool_w=pool_w, pool_scale=pool_scale, w_kv=w_kv, w_f=w_f, b_f=b_f, w_q=w_q, w_o=w_o,
                   ffn_w_in=ffn_w_in, ffn_conv_w=ffn_conv_w, ffn_conv_b=ffn_conv_b, ffn_w_out=ffn_w_out,
                   ln_g=ln_g, ln_b=ln_b)
    mom1 = dict(meta=m_meta, pool_w=m_pool_w, pool_scale=m_pool_scale, w_kv=m_w_kv, w_f=m_w_f, b_f=m_b_f, w_q=m_w_q,
                w_o=m_w_o, ffn_w_in=m_ffn_w_in, ffn_conv_w=m_ffn_conv_w, ffn_conv_b=m_ffn_conv_b,
                ffn_w_out=m_ffn_w_out, ln_g=m_ln_g, ln_b=m_ln_b)
    mom2 = dict(meta=v_meta, pool_w=v_pool_w, pool_scale=v_pool_scale, w_kv=v_w_kv, w_f=v_w_f, b_f=v_b_f, w_q=v_w_q,
                w_o=v_w_o, ffn_w_in=v_ffn_w_in, ffn_conv_w=v_ffn_conv_w, ffn_conv_b=v_ffn_conv_b,
                ffn_w_out=v_ffn_w_out, ln_g=v_ln_g, ln_b=v_ln_b)
    _, seq, D = x.shape
    L = N_META + seq
    tm = _row_tile(L)
    Lp = _round_up(L, tm)
    c_idx = lax.axis_index("c")
    chip = 2 * lax.axis_index("x") + lax.axis_index("y")

    shard_shapes = {n: weights[n].shape for n in SHARDED}
    rows_b = _round_up(sum(_rows_of(shard_shapes[n]) for n in MATMUL_WEIGHTS), 32)
    rows_f = _round_up(sum(_rows_of(shard_shapes[n]) for n in VECTOR_WEIGHTS), SUBLANES)
    wb = _pack([weights[n] for n in MATMUL_WEIGHTS], rows_b, BF16)
    wf = _pack([weights[n] for n in VECTOR_WEIGHTS], rows_f, F32)
    gb, gf = _all_gather_weights(wb, wf, "weights_all_gather")
    gb = lax.dynamic_update_index_in_dim(gb, wb, chip, axis=0)
    gf = lax.dynamic_update_index_in_dim(gf, wf, chip, axis=0)
    full = {}
    for names, buf in ((MATMUL_WEIGHTS, gb), (VECTOR_WEIGHTS, gf)):
        for n, stacked in zip(names, _unpack(buf, [shard_shapes[n] for n in names])):
            full[n] = _join_shards(stacked, SHARD_AXIS[n])
    full["b_f"] = b_f
    full["ffn_conv_b"] = ffn_conv_b

    pad = jnp.zeros((Lp - L, D), F32)
    h0 = jnp.concatenate([full["meta"], x[0], pad], axis=0)
    tgt = jnp.concatenate([jnp.zeros((N_META, D), F32), loss_target[0], pad], axis=0)
    loss, dh0, grads = _local_step(h0, tgt, full, seq=seq, tm=tm)
    loss = lax.psum(loss[0, 0], AXES)
    grad_x = dh0[N_META:L][None]

    core = c_idx.astype(jnp.int32).reshape(1)
    place = jnp.stack([c_idx, chip]).astype(jnp.int32)
    small_shapes = [shard_shapes[n] for n in SMALL_SHARDED]
    rows_s = _round_up(sum(_rows_of(s) for s in small_shapes), 2 * LANES)

    def packed_small(d):
        return _pack([d[n] for n in SMALL_SHARDED], rows_s, F32).reshape(2, rows_s // 2, PACK_COLS)

    names, orders, wires, g_views, wmv = [], [], [], [], []
    for n, order in BIG_SHARDED:
        shp = shard_shapes[n]
        C = shp[-1]
        R = weights[n].size // C // 2
        lead = (N_CHIPS, 2) if order == "CH" else (2, N_CHIPS)
        names.append(n)
        orders.append(order)
        wires.append(BF16)
        g_views.append(grads[n].reshape(lead + (R, C)))
        wmv.append([d[n].reshape(2, R, C) for d in (weights, mom1, mom2)])
    names.append("small")
    orders.append("CH")
    wires.append(F32)
    g_views.append(jnp.stack([_pack([_shard_of(grads[n], SHARD_AXIS[n], s) for n in SMALL_SHARDED], rows_s, F32)
                              for s in range(N_CHIPS)]).reshape(N_CHIPS, 2, rows_s // 2, PACK_COLS))
    wmv.append([packed_small(d) for d in (weights, mom1, mom2)])

    from_sibling = _halves_to_sibling(g_views, orders, "grads_to_sibling")
    parts, on_wire = [], []
    for n, order, wire, g, a in zip(names, orders, wires, g_views, from_sibling):
        p, pw = _chip_partial(g, a, core, order, wire, f"chip_sum_{n}", tr=_row_block(a.shape[1], a.shape[2]))
        parts.append(p)
        on_wire.append(pw)

    rep_shapes = [weights[n].shape for n in REPLICATED]
    rows_r = _round_up(sum(_rows_of(s) for s in rep_shapes), SUBLANES)
    rep = _pack([grads[n] for n in REPLICATED], rows_r, F32)
    *landed, reps = _chip_exchange(on_wire, rep, "grads_chip_exchange")

    halves = []
    for n, p, b, (w_, m_, v_) in zip(names, parts, landed, wmv):
        halves += _adamw_owned(p, b, w_, m_, v_, place, f"adamw_{n}", tr=_row_block(p.shape[1], p.shape[2]))
    joined = _join_halves(halves, "results_to_sibling")
    out = {}
    for k, n in enumerate(names[:-1]):
        out[n] = [a.reshape(shard_shapes[n]) for a in joined[4 * k:4 * k + 4]]
    small_out = [_unpack(a.reshape(rows_s, PACK_COLS), small_shapes) for a in joined[-4:]]
    for k, n in enumerate(SMALL_SHARDED):
        out[n] = [small_out[kind][k] for kind in range(4)]

    def packr(d):
        return _pack([d[n] for n in REPLICATED], rows_r, F32)

    res_r = _sum_adamw(reps[0], reps[1:], packr(weights), packr(mom1), packr(mom2), "adamw_replicated", tr=rows_r)
    rep_out = _unpack(res_r, rep_shapes)

    out.update({n: a for n, a in zip(REPLICATED, rep_out)})
    result = [loss, grad_x]
    for k in range(4):
        result += [out[n][k] for n in WEIGHT_ORDER]
    return tuple(result)
```
